```python
import math
import jax, jax.numpy as jnp
from jax import lax
import numpy as np

D_MODEL = 2048
BATCH = 4
SEQ = 2048
DEPTH = 1
DEC_BATCH = 8
DEC_SEQ = 32
PAST_LEN = 1024

CHUNK = 64
HEAD_DIM = 64
ATT_HEADS = D_MODEL // 2 // HEAD_DIM
ATT_KV_HEADS = 2
ATT_GROUP = ATT_HEADS // ATT_KV_HEADS
ATT_WIDTH = ATT_HEADS * HEAD_DIM
KV_WIDTH = ATT_KV_HEADS * HEAD_DIM
WINDOW = 128
WINDOW_CHUNKS = WINDOW // CHUNK
ROPE_THETA = 10000.0
ATT_SCALE = HEAD_DIM ** -0.5
RW_HEADS = D_MODEL // 2 // HEAD_DIM
RW_WIDTH = RW_HEADS * HEAD_DIM
DECAY_LORA = 64
AAA_LORA = 64
GATE_LORA = 160
LNX_EPS = 64e-5
ATT_SIZES = (ATT_WIDTH, KV_WIDTH, KV_WIDTH)
RW_SIZES = (RW_WIDTH, DECAY_LORA, RW_WIDTH, RW_WIDTH, AAA_LORA, GATE_LORA)
ATT_COLS = ATT_WIDTH + 2 * KV_WIDTH
RW_COLS = 3 * RW_WIDTH + DECAY_LORA + AAA_LORA + GATE_LORA
IN_COLS = ATT_COLS + RW_COLS
MIX_WIDTH = ATT_WIDTH + RW_WIDTH
D_FF = 5632
CONV_W = 3
LN_EPS = 1e-5
ALPHA = (2 * DEPTH) ** 0.25
BETA = (8 * DEPTH) ** -0.25

kernel_name = 'hybrid_swa_sink_rwkv7_convffn_stream_step'


def layer_norm(x, g, b, eps=LN_EPS):
    xf = x.astype(jnp.float32)
    mu = jnp.mean(xf, -1, keepdims=True)
    var = jnp.mean(jnp.square(xf - mu), -1, keepdims=True)
    return ((xf - mu) * lax.rsqrt(var + eps) * g + b).astype(x.dtype)


def split_cols(p, sizes):
    out, start = [], 0
    for s in sizes:
        out.append(p[..., start:start + s])
        start += s
    return out


def rope(x, pos):
    half = HEAD_DIM // 2
    inv = ROPE_THETA ** (-jnp.arange(half, dtype=jnp.float32) / half)
    ang = pos.astype(jnp.float32)[:, None] * inv[None, :]
    cos = jnp.cos(ang)[None, :, None, :]
    sin = jnp.sin(ang)[None, :, None, :]
    xf = x.astype(jnp.float32)
    x1, x2 = xf[..., :half], xf[..., half:]
    return jnp.concatenate([x1 * cos - x2 * sin, x2 * cos + x1 * sin], -1).astype(x.dtype)


def sink_softmax(s, sink):
    m = jnp.maximum(jnp.max(s, -1), sink)
    p = jnp.exp(s - m[..., None])
    den = jnp.sum(p, -1) + jnp.exp(sink - m)
    return p / den[..., None]


def band_attention_prompt(q, k, v, sinks):
    B, T = q.shape[0], q.shape[1]
    nc = T // CHUNK
    qc = q.reshape(B, nc, CHUNK, ATT_KV_HEADS, ATT_GROUP, HEAD_DIM)

    def band(t):
        tc = t.reshape(B, nc, CHUNK, ATT_KV_HEADS, HEAD_DIM)
        tp = jnp.pad(tc, ((0, 0), (WINDOW_CHUNKS, 0), (0, 0), (0, 0), (0, 0)))
        return jnp.concatenate([tp[:, i:i + nc] for i in range(WINDOW_CHUNKS + 1)], axis=2)

    kb, vb = band(k), band(v)
    slot_chunk = jnp.repeat(jnp.arange(WINDOW_CHUNKS + 1), CHUNK)
    valid = (jnp.arange(nc)[:, None] - WINDOW_CHUNKS + slot_chunk[None, :]) >= 0
    s = jnp.einsum('bnqkgd,bnskd->bnkgqs', qc, kb, preferred_element_type=jnp.float32) * ATT_SCALE
    s = jnp.where(valid[None, :, None, None, None, :], s, -jnp.inf)
    p = sink_softmax(s, sinks.astype(jnp.float32).reshape(ATT_KV_HEADS, ATT_GROUP, 1))
    o = jnp.einsum('bnkgqs,bnskd->bnqkgd', p.astype(v.dtype), vb)
    return o.reshape(B, T, ATT_WIDTH)


def window_attention_step(q, k_new, v_new, k_buf, v_buf, sinks):
    B, T = q.shape[0], q.shape[1]
    k_all = jnp.concatenate([k_buf, k_new], axis=1)
    v_all = jnp.concatenate([v_buf, v_new], axis=1)
    qg = q.reshape(B, T, ATT_KV_HEADS, ATT_GROUP, HEAD_DIM)
    s = jnp.einsum('bqkgd,bskd->bkgqs', qg, k_all, preferred_element_type=jnp.float32) * ATT_SCALE
    p = sink_softmax(s, sinks.astype(jnp.float32).reshape(ATT_KV_HEADS, ATT_GROUP, 1))
    o = jnp.einsum('bkgqs,bskd->bqkgd', p.astype(v_all.dtype), v_all)
    return o.reshape(B, T, ATT_WIDTH), k_all[:, -WINDOW:], v_all[:, -WINDOW:]


def wkv7_scan(r, decay, k, v, a, b, s0):
    def step(S, inp):
        r_t, w_t, k_t, v_t, a_t, b_t = inp
        sa = jnp.einsum('bhij,bhj->bhi', S, a_t)
        S = S * w_t[:, :, None, :] + sa[..., None] * b_t[:, :, None, :] + v_t[..., None] * k_t[:, :, None, :]
        return S, jnp.einsum('bhij,bhj->bhi', S, r_t)

    xs = tuple(jnp.moveaxis(t, 1, 0) for t in (r, decay, k, v, a, b))
    s_last, out = lax.scan(step, s0, xs)
    return jnp.moveaxis(out, 0, 1), s_last


def token_mixer(x, pos, shift_prev, wkv0, k_buf, v_buf, lp):
    B, T = x.shape[0], x.shape[1]
    p = x @ lp['w_in']
    p_att, p_rw = p[..., :ATT_COLS], p[..., ATT_COLS:]
    q, k, v = split_cols(p_att, ATT_SIZES)
    q = rope(q.reshape(B, T, ATT_HEADS, HEAD_DIM), pos)
    k = rope(k.reshape(B, T, ATT_KV_HEADS, HEAD_DIM), pos)
    v = v.reshape(B, T, ATT_KV_HEADS, HEAD_DIM)
    if k_buf is None:
        att = band_attention_prompt(q, k, v, lp['sinks'])
        k_win, v_win = k[:, -WINDOW:], v[:, -WINDOW:]
    else:
        att, k_win, v_win = window_attention_step(q, k, v, k_buf, v_buf, lp['sinks'])
    prev = jnp.concatenate([shift_prev, p_rw[:, :-1]], axis=1)
    xs = p_rw + (prev - p_rw) * lp['mu']
    r, wd, kr, vr, ad, gd = split_cols(xs, RW_SIZES)
    w_log = -jax.nn.softplus(-(lp['w0'] + jnp.tanh(wd) @ lp['w2'])) - 0.5
    decay = jnp.exp(-jnp.exp(w_log.astype(jnp.float32)))
    a = jax.nn.sigmoid(lp['a0'] + ad @ lp['a2'])
    g = jax.nn.sigmoid(gd) @ lp['g2']

    def heads(t):
        return t.reshape(B, T, RW_HEADS, HEAD_DIM).astype(jnp.float32)

    kk = heads(kr * lp['k_k'])
    kk = kk / jnp.maximum(jnp.sqrt(jnp.sum(kk * kk, -1, keepdims=True)), 1e-12)
    kr = kr * (1.0 + (a - 1.0) * lp['k_a'])
    rh, kh, vh, ah = heads(r), heads(kr), heads(vr), heads(a)
    o, wkv_last = wkv7_scan(rh, heads(decay), kh, vh, -kk, kk * ah, wkv0.astype(jnp.float32))
    mo = jnp.mean(o, -1, keepdims=True)
    vo = jnp.mean(jnp.square(o - mo), -1, keepdims=True)
    on = ((o - mo) * lax.rsqrt(vo + LNX_EPS)).reshape(B, T, RW_WIDTH) * lp['lnx_g'] + lp['lnx_b']
    bonus = jnp.sum(rh * kh * lp['r_k'].astype(jnp.float32), -1, keepdims=True) * vh
    rw_out = ((on + bonus.reshape(B, T, RW_WIDTH)) * g).astype(x.dtype)
    mixed = jnp.concatenate([att.astype(x.dtype), rw_out], axis=-1) @ lp['w_out']
    return mixed, k_win, v_win, wkv_last.astype(wkv0.dtype), p_rw[:, -1:]


def conv_ffn(x, conv_prev, lp):
    T = x.shape[1]
    up = x @ lp['w_up']
    ext = jnp.concatenate([conv_prev, up], axis=1)
    c = lp['conv_b']
    for i in range(CONV_W):
        c = c + ext[:, i:i + T] * lp['conv_w'][i]
    gate, val = c[..., :D_FF], c[..., D_FF:]
    y = (jax.nn.gelu(gate, approximate=False) * val) @ lp['w_down']
    return y, ext[:, -(CONV_W - 1):]


def setup_inputs(seed: int = 0) -> dict:
    key = jax.random.key(seed)
    ks = jax.random.split(key, 32)
    f32 = jnp.float32
    nrm = lambda k, shape, s=1.0: (jax.random.normal(k, shape, f32) * s)
    return {
        'x_prompt': nrm(ks[0], (BATCH, SEQ, D_MODEL)),
        'x_sample': nrm(ks[1], (DEC_BATCH, DEC_SEQ, D_MODEL)),
        'cache_k': nrm(ks[2], (DEPTH, DEC_BATCH, WINDOW, ATT_KV_HEADS, HEAD_DIM)),
        'cache_v': nrm(ks[3], (DEPTH, DEC_BATCH, WINDOW, ATT_KV_HEADS, HEAD_DIM)),
        'state_wkv': nrm(ks[4], (DEPTH, DEC_BATCH, RW_HEADS, HEAD_DIM, HEAD_DIM), 0.5),
        'state_shift': nrm(ks[5], (DEPTH, DEC_BATCH, 1, RW_COLS)),
        'state_ffn_conv': nrm(ks[6], (DEPTH, DEC_BATCH, CONV_W - 1, 2 * D_FF)),
        'ln_in_g': 1.0 + nrm(ks[7], (D_MODEL,), 0.02),
        'ln_in_b': nrm(ks[8], (D_MODEL,), 0.02),
        'w_in': nrm(ks[9], (DEPTH, D_MODEL, IN_COLS), D_MODEL ** -0.5),
        'attn_sinks': nrm(ks[10], (DEPTH, ATT_HEADS)),
        'rw_mu': jax.random.uniform(ks[11], (DEPTH, RW_COLS), f32),
        'rw_w0': jax.random.uniform(ks[12], (DEPTH, RW_WIDTH), f32, -6.0, -1.0),
        'rw_w2': nrm(ks[13], (DEPTH, DECAY_LORA, RW_WIDTH), 0.1 * DECAY_LORA ** -0.5),
        'rw_a0': nrm(ks[14], (DEPTH, RW_WIDTH), 0.1),
        'rw_a2': nrm(ks[15], (DEPTH, AAA_LORA, RW_WIDTH), 0.1 * AAA_LORA ** -0.5),
        'rw_g2': nrm(ks[16], (DEPTH, GATE_LORA, RW_WIDTH), GATE_LORA ** -0.5),
        'rw_k_k': 0.85 + nrm(ks[17], (DEPTH, RW_WIDTH), 0.02),
        'rw_k_a': 1.0 + nrm(ks[18], (DEPTH, RW_WIDTH), 0.02),
        'rw_r_k': nrm(ks[19], (DEPTH, RW_HEADS, HEAD_DIM), 0.1),
        'rw_lnx_g': 1.0 + nrm(ks[20], (DEPTH, RW_WIDTH), 0.02),
        'rw_lnx_b': nrm(ks[21], (DEPTH, RW_WIDTH), 0.02),
        'w_out': nrm(ks[22], (DEPTH, MIX_WIDTH, D_MODEL), BETA * MIX_WIDTH ** -0.5),
        'ln1_g': 1.0 + nrm(ks[23], (DEPTH, D_MODEL), 0.02),
        'ln1_b': nrm(ks[24], (DEPTH, D_MODEL), 0.02),
        'ffn_w_up': nrm(ks[25], (DEPTH, D_MODEL, 2 * D_FF), D_MODEL ** -0.5),
        'ffn_conv_w': nrm(ks[26], (DEPTH, CONV_W, 2 * D_FF), CONV_W ** -0.5),
        'ffn_conv_b': nrm(ks[27], (DEPTH, 2 * D_FF), 0.02),
        'ffn_w_down': nrm(ks[28], (DEPTH, D_FF, D_MODEL), BETA * D_FF ** -0.5),
        'ln2_g': 1.0 + nrm(ks[29], (DEPTH, D_MODEL), 0.02),
        'ln2_b': nrm(ks[30], (DEPTH, D_MODEL), 0.02),
    }


def reference(x_prompt, x_sample, cache_k, cache_v, state_wkv, state_shift, state_ffn_conv,
              ln_in_g, ln_in_b, w_in, attn_sinks, rw_mu, rw_w0, rw_w2, rw_a0, rw_a2, rw_g2,
              rw_k_k, rw_k_a, rw_r_k, rw_lnx_g, rw_lnx_b, w_out, ln1_g, ln1_b,
              ffn_w_up, ffn_conv_w, ffn_conv_b, ffn_w_down, ln2_g, ln2_b):
    def layer_params(l):
        return {'w_in': w_in[l], 'sinks': attn_sinks[l], 'mu': rw_mu[l], 'w0': rw_w0[l], 'w2': rw_w2[l],
                'a0': rw_a0[l], 'a2': rw_a2[l], 'g2': rw_g2[l], 'k_k': rw_k_k[l], 'k_a': rw_k_a[l],
                'r_k': rw_r_k[l], 'lnx_g': rw_lnx_g[l], 'lnx_b': rw_lnx_b[l], 'w_out': w_out[l],
                'w_up': ffn_w_up[l], 'conv_w': ffn_conv_w[l], 'conv_b': ffn_conv_b[l], 'w_down': ffn_w_down[l]}

    def run(x, pos, layer_states):
        h = layer_norm(x, ln_in_g, ln_in_b)
        ks_, vs_, ws_, ss_, cs_ = [], [], [], [], []
        for l in range(DEPTH):
            ck, cv, cw, csh, cf = layer_states[l]
            lp = layer_params(l)
            m, kb, vb, wkv, sh = token_mixer(h, pos, csh, cw, ck, cv, lp)
            h = layer_norm(ALPHA * h + m, ln1_g[l], ln1_b[l])
            f, cb = conv_ffn(h, cf, lp)
            h = layer_norm(ALPHA * h + f, ln2_g[l], ln2_b[l])
            ks_.append(kb); vs_.append(vb); ws_.append(wkv); ss_.append(sh); cs_.append(cb)
        return h, jnp.stack(ks_), jnp.stack(vs_), jnp.stack(ws_), jnp.stack(ss_), jnp.stack(cs_)

    bp, tp = x_prompt.shape[0], x_prompt.shape[1]
    prompt_states = [(None, None,
                      jnp.zeros((bp, RW_HEADS, HEAD_DIM, HEAD_DIM), jnp.float32),
                      jnp.zeros((bp, 1, RW_COLS), x_prompt.dtype),
                      jnp.zeros((bp, CONV_W - 1, 2 * D_FF), x_prompt.dtype)) for _ in range(DEPTH)]
    y_prompt, p_k, p_v, p_wkv, p_shift, p_conv = run(x_prompt, jnp.arange(tp, dtype=jnp.int32), prompt_states)

    ts = x_sample.shape[1]
    sample_states = [(cache_k[l], cache_v[l], state_wkv[l], state_shift[l], state_ffn_conv[l]) for l in range(DEPTH)]
    y_sample, s_k, s_v, s_wkv, s_shift, s_conv = run(
        x_sample, PAST_LEN + jnp.arange(ts, dtype=jnp.int32), sample_states)
    return (y_prompt, y_sample, p_k, p_v, p_wkv, p_shift, p_conv, s_k, s_v, s_wkv, s_shift, s_conv)
```

```python
import functools
import math

import jax
import jax.numpy as jnp
from jax import lax
from jax.experimental import pallas as pl
from jax.experimental.pallas import tpu as pltpu

D_MODEL = 2048
HEAD_DIM = 64
ATT_HEADS = 16
ATT_KV_HEADS = 2
ATT_GROUP = ATT_HEADS // ATT_KV_HEADS
ATT_WIDTH = ATT_HEADS * HEAD_DIM
KV_WIDTH = ATT_KV_HEADS * HEAD_DIM
CHUNK = 64
WINDOW = 128
ROPE_THETA = 10000.0
ATT_SCALE = HEAD_DIM ** -0.5
RW_HEADS = 16
RW_WIDTH = RW_HEADS * HEAD_DIM
DECAY_LORA = 64
AAA_LORA = 64
GATE_LORA = 160
LNX_EPS = 64e-5
ATT_COLS = ATT_WIDTH + 2 * KV_WIDTH
RW_COLS = 3 * RW_WIDTH + DECAY_LORA + AAA_LORA + GATE_LORA
D_FF = 5632
CONV_W = 3
LN_EPS = 1e-5
DEPTH = 1
ALPHA = (2 * DEPTH) ** 0.25
PAST_LEN = 1024

LORA_W = 512
P_R, P_K, P_V, P_Q = 0, 1024, 2048, 3072
P_LORA = 4096
P_AK = P_LORA + LORA_W
P_AV = P_AK + KV_WIDTH
P_COLS = P_AV + KV_WIDTH
LANES = 128
SUBLANES = 8
VMEM_LIMIT = 56 * 1024 * 1024

HI = lax.Precision.HIGHEST
F32 = jnp.float32
BF16 = jnp.bfloat16


def _cparams(sem):
    return pltpu.CompilerParams(dimension_semantics=sem, vmem_limit_bytes=VMEM_LIMIT)


def _layer_norm(y, g, b):
    mu = jnp.mean(y, -1, keepdims=True)
    d = y - mu
    var = jnp.mean(d * d, -1, keepdims=True)
    return d * lax.rsqrt(var + LN_EPS) * g + b


A_TN = 256
A_Q_BLOCKS = (P_Q // A_TN, (P_Q + ATT_WIDTH) // A_TN)
A_K_BLOCK = P_AK // A_TN


def _rope128(x, cos, sin_signed):
    lane = lax.broadcasted_iota(jnp.int32, x.shape, 1)
    first_half = (lane % HEAD_DIM) < (HEAD_DIM // 2)
    rot = jnp.where(first_half, pltpu.roll(x, LANES - HEAD_DIM // 2, 1), pltpu.roll(x, HEAD_DIM // 2, 1))
    return x * cos + rot * sin_signed


def _in_proj_kernel(x_ref, g_ref, b_ref, w_ref, cos_ref, sin_ref, p_ref, h_ref, hb_ref):
    j = pl.program_id(1)

    @pl.when(j == 0)
    def _():
        h = _layer_norm(x_ref[...], g_ref[...], b_ref[...])
        h_ref[...] = h
        hb_ref[...] = h.astype(BF16)

    acc = jnp.dot(hb_ref[...], w_ref[...], preferred_element_type=F32)
    is_q = jnp.logical_and(j >= A_Q_BLOCKS[0], j < A_Q_BLOCKS[1])
    is_k = j == A_K_BLOCK

    @pl.when(is_q)
    def _():
        cos, sin = cos_ref[...], sin_ref[...]
        p_ref[:, :LANES] = _rope128(acc[:, :LANES], cos, sin)
        p_ref[:, LANES:] = _rope128(acc[:, LANES:], cos, sin)

    @pl.when(is_k)
    def _():
        p_ref[:, :LANES] = _rope128(acc[:, :LANES], cos_ref[...], sin_ref[...])
        p_ref[:, LANES:] = acc[:, LANES:]

    @pl.when(jnp.logical_not(jnp.logical_or(is_q, is_k)))
    def _():
        p_ref[...] = acc


def _in_proj(x, g, b, w_p, cos, sin, tm, table_blocks):
    n = x.shape[0]
    grid = (n // tm, P_COLS // A_TN)
    return pl.pallas_call(
        _in_proj_kernel,
        grid=grid,
        in_specs=[
            pl.BlockSpec((tm, D_MODEL), lambda i, j: (i, 0)),
            pl.BlockSpec((1, D_MODEL), lambda i, j: (0, 0)),
            pl.BlockSpec((1, D_MODEL), lambda i, j: (0, 0)),
            pl.BlockSpec((D_MODEL, A_TN), lambda i, j: (0, j)),
            pl.BlockSpec((tm, LANES), lambda i, j: (i % table_blocks, 0)),
            pl.BlockSpec((tm, LANES), lambda i, j: (i % table_blocks, 0)),
        ],
        out_specs=[
            pl.BlockSpec((tm, A_TN), lambda i, j: (i, j)),
            pl.BlockSpec((tm, D_MODEL), lambda i, j: (i, 0)),
        ],
        out_shape=[
            jax.ShapeDtypeStruct((n, P_COLS), F32),
            jax.ShapeDtypeStruct((n, D_MODEL), F32),
        ],
        scratch_shapes=[pltpu.VMEM((tm, D_MODEL), BF16)],
        compiler_params=_cparams(("arbitrary", "arbitrary")),
        name="in_proj",
    )(x, g, b, w_p, cos, sin)


def _attend(q, k, v, valid, sink_ref):
    r = q.shape[0]
    outs = []
    for kv in range(ATT_KV_HEADS):
        kh = k[:, kv * HEAD_DIM:(kv + 1) * HEAD_DIM].astype(BF16)
        vh = v[:, kv * HEAD_DIM:(kv + 1) * HEAD_DIM].astype(BF16)
        heads = [kv * ATT_GROUP + g for g in range(ATT_GROUP)]
        qs = jnp.concatenate([q[:, h * HEAD_DIM:(h + 1) * HEAD_DIM] for h in heads], axis=0)
        sink = jnp.concatenate([jnp.full((r, 1), sink_ref[h], F32) for h in heads], axis=0)
        s = lax.dot_general(qs.astype(BF16), kh, (((1,), (1,)), ((), ())), preferred_element_type=F32) * ATT_SCALE
        if valid is not None:
            s = jnp.where(valid, s, -jnp.inf)
        m = jnp.maximum(jnp.max(s, -1, keepdims=True), sink)
        p = jnp.exp(s - m)
        den = jnp.sum(p, -1, keepdims=True) + jnp.exp(sink - m)
        o = jnp.dot(p.astype(BF16), vh, preferred_element_type=F32) / den
        outs.extend(o[g * r:(g + 1) * r] for g in range(ATT_GROUP))
    return jnp.concatenate(outs, axis=1)


def _attn_prompt_kernel(sink_ref, q_ref, k0_ref, k1_ref, k2_ref, v0_ref, v1_ref, v2_ref, o_ref):
    n = pl.program_id(1)
    k = jnp.concatenate([k0_ref[...], k1_ref[...], k2_ref[...]], axis=0)
    v = jnp.concatenate([v0_ref[...], v1_ref[...], v2_ref[...]], axis=0)
    slot = lax.broadcasted_iota(jnp.int32, (1, 3 * CHUNK), 1) // CHUNK
    valid = (n - 2 + slot) >= 0
    o_ref[...] = _attend(q_ref[...], k, v, valid, sink_ref).astype(o_ref.dtype)


def _attn_prompt(p, sinks, batch, seq):
    nc = seq // CHUNK
    kcol, vcol = P_AK // KV_WIDTH, P_AV // KV_WIDTH

    def kv_spec(col, back):
        return pl.BlockSpec((CHUNK, KV_WIDTH), lambda b, n: (b * nc + jnp.maximum(n - back, 0), col))

    return pl.pallas_call(
        _attn_prompt_kernel,
        grid=(batch, nc),
        in_specs=[
            pl.BlockSpec(memory_space=pltpu.SMEM),
            pl.BlockSpec((CHUNK, ATT_WIDTH), lambda b, n: (b * nc + n, P_Q // ATT_WIDTH)),
            kv_spec(kcol, 2), kv_spec(kcol, 1), kv_spec(kcol, 0),
            kv_spec(vcol, 2), kv_spec(vcol, 1), kv_spec(vcol, 0),
        ],
        out_specs=pl.BlockSpec((CHUNK, ATT_WIDTH), lambda b, n: (b * nc + n, 0)),
        out_shape=jax.ShapeDtypeStruct((batch * seq, ATT_WIDTH), BF16),
        compiler_params=_cparams(("arbitrary", "arbitrary")),
        name="attn_prompt",
    )(sinks, p, p, p, p, p, p, p)


def _attn_sample_kernel(sink_ref, q_ref, kn_ref, vn_ref, kc_ref, vc_ref, o_ref, kw_ref, vw_ref):
    t = q_ref.shape[0]
    k = jnp.concatenate([kc_ref[...], kn_ref[...]], axis=0)
    v = jnp.concatenate([vc_ref[...], vn_ref[...]], axis=0)
    o_ref[...] = _attend(q_ref[...], k, v, None, sink_ref).astype(o_ref.dtype)
    kw_ref[...] = k[t:]
    vw_ref[...] = v[t:]


def _attn_sample(p, sinks, cache_k, cache_v, batch, seq):
    kcol, vcol = P_AK // KV_WIDTH, P_AV // KV_WIDTH
    return pl.pallas_call(
        _attn_sample_kernel,
        grid=(batch,),
        in_specs=[
            pl.BlockSpec(memory_space=pltpu.SMEM),
            pl.BlockSpec((seq, ATT_WIDTH), lambda b: (b, P_Q // ATT_WIDTH)),
            pl.BlockSpec((seq, KV_WIDTH), lambda b: (b, kcol)),
            pl.BlockSpec((seq, KV_WIDTH), lambda b: (b, vcol)),
            pl.BlockSpec((WINDOW, KV_WIDTH), lambda b: (b, 0)),
            pl.BlockSpec((WINDOW, KV_WIDTH), lambda b: (b, 0)),
        ],
        out_specs=[
            pl.BlockSpec((seq, ATT_WIDTH), lambda b: (b, 0)),
            pl.BlockSpec((WINDOW, KV_WIDTH), lambda b: (b, 0)),
            pl.BlockSpec((WINDOW, KV_WIDTH), lambda b: (b, 0)),
        ],
        out_shape=[
            jax.ShapeDtypeStruct((batch * seq, ATT_WIDTH), BF16),
            jax.ShapeDtypeStruct((batch * WINDOW, KV_WIDTH), F32),
            jax.ShapeDtypeStruct((batch * WINDOW, KV_WIDTH), F32),
        ],
        compiler_params=_cparams(("arbitrary",)),
        name="attn_sample",
    )(sinks, p, p, p, cache_k, cache_v)


SEG = 256


def _seg_sum(x, bd):
    parts = [jnp.dot(x[:, s:s + SEG], bd, precision=HI, preferred_element_type=F32)
             for s in range(0, RW_WIDTH, SEG)]
    return jnp.concatenate(parts, axis=1)


def _shift_rows(x, prev_row):
    row = lax.broadcasted_iota(jnp.int32, x.shape, 0)
    return jnp.where(row == 0, prev_row, pltpu.roll(x, 1, 0))


def _dot(a, b):
    return jnp.dot(a, b, precision=HI, preferred_element_type=F32)


def _dot_nt(a, b):
    return lax.dot_general(a, b, (((1,), (1,)), ((), ())), precision=HI, preferred_element_type=F32)


def _dot_tn(a, b):
    return lax.dot_general(a, b, (((0,), (0,)), ((), ())), precision=HI, preferred_element_type=F32)


def _rwkv_head(ab, rb, bb, kb, v, gam_c, s0, strict, incl, n_double):
    l_ab = jnp.where(strict, _dot_nt(ab, bb), 0.0)
    l_ak = jnp.where(strict, _dot_nt(ab, kb), 0.0)
    m_rb = jnp.where(incl, _dot_nt(rb, bb), 0.0)
    m_rk = jnp.where(incl, _dot_nt(rb, kb), 0.0)
    y = jnp.concatenate([ab, _dot(l_ak, v)], axis=1)
    lp = l_ab
    for i in range(n_double):
        y = y + _dot(lp, y)
        if i + 1 < n_double:
            lp = _dot(lp, lp)
    at, ut = y[:, :HEAD_DIM], y[:, HEAD_DIM:]
    r_t = rb + _dot(m_rb, at)
    o_t = _dot(m_rb, ut) + _dot(m_rk, v)
    o = _dot_nt(r_t, s0) + o_t
    q_t = _dot_tn(at, bb)
    d_t = _dot_tn(ut, bb) + _dot_tn(v, kb)
    s_new = (s0 + _dot(s0, q_t) + d_t) * gam_c
    return o, s_new


def _rwkv_kernel(xr_ref, xk_ref, xv_ref, xl_ref, sr_ref, sk_ref, sv_ref, sl_ref,
                 mur_ref, muk_ref, muv_ref, mul_ref, w0_ref, a0_ref, kk_ref, ka_ref, rk_ref,
                 lg_ref, lb_ref, w2_ref, a2_ref, g2_ref, bd_ref, st0_ref,
                 out_ref, st_ref,
                 s_scr, pr_scr, pk_scr, pv_scr, pl_scr, ab_scr, rb_scr, bb_scr, kb_scr, v_scr, o_scr):
    c = pl.program_id(1)
    nc = pl.num_programs(1)
    C = xr_ref.shape[0]
    n_double = int(math.log2(C))

    @pl.when(c == 0)
    def _():
        s_scr[...] = st0_ref[0]
        pr_scr[0:1, :] = sr_ref[0]
        pk_scr[0:1, :] = sk_ref[0]
        pv_scr[0:1, :] = sv_ref[0]
        pl_scr[0:1, :] = sl_ref[0]

    def shifted(x_ref, p_scr, mu_ref):
        x = x_ref[...]
        prev = _shift_rows(x, p_scr[0:1, :])
        p_scr[0:1, :] = x[C - 1:C, :]
        return x + (prev - x) * mu_ref[...]

    r = shifted(xr_ref, pr_scr, mur_ref)
    kr = shifted(xk_ref, pk_scr, muk_ref)
    v = shifted(xv_ref, pv_scr, muv_ref)
    xl = shifted(xl_ref, pl_scr, mul_ref)
    wd = xl[:, 0:DECAY_LORA]
    ad = xl[:, DECAY_LORA:DECAY_LORA + AAA_LORA]
    gd = xl[:, DECAY_LORA + AAA_LORA:DECAY_LORA + AAA_LORA + GATE_LORA]

    w_log = -jax.nn.softplus(-(w0_ref[...] + _dot(jnp.tanh(wd), w2_ref[...]))) - 0.5
    lw = -jnp.exp(w_log)
    a = jax.nn.sigmoid(a0_ref[...] + _dot(ad, a2_ref[...]))
    g = _dot(jax.nn.sigmoid(gd), g2_ref[...])
    bd = bd_ref[...]
    kk = kr * kk_ref[...]
    kk = kk / jnp.maximum(jnp.sqrt(_seg_sum(kk * kk, bd)), 1e-12)
    k2 = kr * (1.0 + (a - 1.0) * ka_ref[...])

    ti = lax.broadcasted_iota(jnp.int32, (C, C), 0)
    si = lax.broadcasted_iota(jnp.int32, (C, C), 1)
    strict = si < ti
    incl = si <= ti
    cs = _dot(incl.astype(F32), lw)
    gam = jnp.exp(cs)
    ginv = jnp.exp(-cs)
    ab_scr[...] = -kk * jnp.exp(cs - lw)
    rb_scr[...] = r * gam
    bb_scr[...] = kk * a * ginv
    kb_scr[...] = k2 * ginv
    v_scr[...] = v
    gam_c = gam[C - 1:C, :]

    for h in range(RW_HEADS):
        sl = slice(h * HEAD_DIM, (h + 1) * HEAD_DIM)
        o_h, s_new = _rwkv_head(ab_scr[:, sl], rb_scr[:, sl], bb_scr[:, sl], kb_scr[:, sl], v_scr[:, sl],
                                gam_c[:, sl], s_scr[h], strict, incl, n_double)
        o_scr[:, sl] = o_h
        s_scr[h] = s_new

    o = o_scr[...]
    mo = _seg_sum(o, bd) * (1.0 / HEAD_DIM)
    d = o - mo
    vo = _seg_sum(d * d, bd) * (1.0 / HEAD_DIM)
    on = d * lax.rsqrt(vo + LNX_EPS) * lg_ref[...] + lb_ref[...]
    bonus = _seg_sum(r * k2 * rk_ref[...], bd) * v
    out_ref[...] = ((on + bonus) * g).astype(out_ref.dtype)

    @pl.when(c == nc - 1)
    def _():
        st_ref[0] = s_scr[...]


def _rwkv(p, shift_p, state0, prm, batch, seq, chunk):
    nc = seq // chunk
    row = lambda w: pl.BlockSpec((1, w), lambda b, c: (0, 0))
    xspec = lambda w, col: pl.BlockSpec((chunk, w), lambda b, c: (b * nc + c, col))
    sspec = lambda w: pl.BlockSpec((1, 1, w), lambda b, c: (b, 0, 0))
    full = lambda a: pl.BlockSpec(a.shape, lambda b, c: (0,) * a.ndim)
    sr, sk, sv, sl = shift_p
    wide = lambda: pltpu.VMEM((chunk, RW_WIDTH), F32)
    return pl.pallas_call(
        _rwkv_kernel,
        grid=(batch, nc),
        in_specs=[
            xspec(RW_WIDTH, P_R // RW_WIDTH), xspec(RW_WIDTH, P_K // RW_WIDTH), xspec(RW_WIDTH, P_V // RW_WIDTH),
            xspec(LORA_W, P_LORA // LORA_W),
            sspec(RW_WIDTH), sspec(RW_WIDTH), sspec(RW_WIDTH), sspec(LORA_W),
            row(RW_WIDTH), row(RW_WIDTH), row(RW_WIDTH), row(LORA_W),
            row(RW_WIDTH), row(RW_WIDTH), row(RW_WIDTH), row(RW_WIDTH), row(RW_WIDTH),
            row(RW_WIDTH), row(RW_WIDTH),
            full(prm["w2"]), full(prm["a2"]), full(prm["g2"]), full(prm["bd"]),
            pl.BlockSpec((1, RW_HEADS, HEAD_DIM, HEAD_DIM), lambda b, c: (b, 0, 0, 0)),
        ],
        out_specs=[
            pl.BlockSpec((chunk, RW_WIDTH), lambda b, c: (b * nc + c, 0)),
            pl.BlockSpec((1, RW_HEADS, HEAD_DIM, HEAD_DIM), lambda b, c: (b, 0, 0, 0)),
        ],
        out_shape=[
            jax.ShapeDtypeStruct((batch * seq, RW_WIDTH), BF16),
            jax.ShapeDtypeStruct((batch, RW_HEADS, HEAD_DIM, HEAD_DIM), F32),
        ],
        scratch_shapes=[
            pltpu.VMEM((RW_HEADS, HEAD_DIM, HEAD_DIM), F32),
            pltpu.VMEM((SUBLANES, RW_WIDTH), F32), pltpu.VMEM((SUBLANES, RW_WIDTH), F32),
            pltpu.VMEM((SUBLANES, RW_WIDTH), F32), pltpu.VMEM((SUBLANES, LORA_W), F32),
            wide(), wide(), wide(), wide(), wide(), wide(),
        ],
        compiler_params=_cparams(("arbitrary", "arbitrary")),
        name="rwkv",
    )(p, p, p, p, sr, sk, sv, sl,
      prm["mu_r"], prm["mu_k"], prm["mu_v"], prm["mu_l"], prm["w0"], prm["a0"], prm["k_k"], prm["k_a"], prm["r_k"],
      prm["lnx_g"], prm["lnx_b"], prm["w2"], prm["a2"], prm["g2"], prm["bd"], state0)


def _out_proj_kernel(att_ref, rw_ref, wa_ref, wb_ref, h_ref, g_ref, b_ref, h1_ref, h1b_ref):
    m = jnp.dot(att_ref[...], wa_ref[...], preferred_element_type=F32)
    m = m + jnp.dot(rw_ref[...], wb_ref[...], preferred_element_type=F32)
    y = _layer_norm(ALPHA * h_ref[...] + m, g_ref[...], b_ref[...])
    h1_ref[...] = y
    h1b_ref[...] = y.astype(BF16)


def _out_proj(att, rw, w_out, h, g, b, tm):
    n = att.shape[0]
    return pl.pallas_call(
        _out_proj_kernel,
        grid=(n // tm,),
        in_specs=[
            pl.BlockSpec((tm, ATT_WIDTH), lambda i: (i, 0)),
            pl.BlockSpec((tm, RW_WIDTH), lambda i: (i, 0)),
            pl.BlockSpec((ATT_WIDTH, D_MODEL), lambda i: (0, 0)),
            pl.BlockSpec((RW_WIDTH, D_MODEL), lambda i: (1, 0)),
            pl.BlockSpec((tm, D_MODEL), lambda i: (i, 0)),
            pl.BlockSpec((1, D_MODEL), lambda i: (0, 0)),
            pl.BlockSpec((1, D_MODEL), lambda i: (0, 0)),
        ],
        out_specs=[
            pl.BlockSpec((tm, D_MODEL), lambda i: (i, 0)),
            pl.BlockSpec((tm, D_MODEL), lambda i: (i, 0)),
        ],
        out_shape=[
            jax.ShapeDtypeStruct((n, D_MODEL), F32),
            jax.ShapeDtypeStruct((n, D_MODEL), BF16),
        ],
        compiler_params=_cparams(("arbitrary",)),
        name="out_proj",
    )(att, rw, w_out, w_out, h, g, b)


E_TN = 512
E_NCB = D_FF // E_TN
SQRT_HALF = math.sqrt(0.5)


def _ffn_up_kernel(bps, h_ref, wg_ref, wv_ref, cwg_ref, cwv_ref, cbg_ref, cbv_ref, sg_ref, sv_ref,
                   act_ref, og_ref, ov_ref, cg_scr, cv_scr):
    i = pl.program_id(1)
    first = (i % bps) == 0
    h = h_ref[...]
    tm = h.shape[0]
    row = lax.broadcasted_iota(jnp.int32, (tm, E_TN), 0)

    def conv(w_ref, cw_ref, cb_ref, st_ref, c_scr, o_ref):
        u = jnp.dot(h, w_ref[...], preferred_element_type=F32)
        prev = jnp.where(first, st_ref[0], c_scr[SUBLANES - 2:SUBLANES, :])
        p2, p1 = prev[0:1, :], prev[1:2, :]
        sh1 = jnp.where(row == 0, p1, pltpu.roll(u, 1, 0))
        sh2 = jnp.where(row == 0, p2, jnp.where(row == 1, p1, pltpu.roll(u, 2, 0)))
        cw = cw_ref[...]
        c = cb_ref[...] + sh2 * cw[0:1, :]
        c = c + sh1 * cw[1:2, :]
        c = c + u * cw[2:3, :]
        tail = u[tm - SUBLANES:tm, :]
        c_scr[...] = tail
        o_ref[0] = tail
        return c

    gate = conv(wg_ref, cwg_ref, cbg_ref, sg_ref, cg_scr, og_ref)
    val = conv(wv_ref, cwv_ref, cbv_ref, sv_ref, cv_scr, ov_ref)
    gelu = 0.5 * gate * (1.0 + lax.erf(gate * SQRT_HALF))
    act_ref[...] = (gelu * val).astype(act_ref.dtype)


def _ffn_up(h1b, w_up, conv_w, conv_b, conv_prev, batch, seq, tm):
    n = h1b.shape[0]
    bps = seq // tm
    nrb = n // tm
    return pl.pallas_call(
        functools.partial(_ffn_up_kernel, bps),
        grid=(E_NCB, nrb),
        in_specs=[
            pl.BlockSpec((tm, D_MODEL), lambda j, i: (i, 0)),
            pl.BlockSpec((D_MODEL, E_TN), lambda j, i: (0, j)),
            pl.BlockSpec((D_MODEL, E_TN), lambda j, i: (0, j + E_NCB)),
            pl.BlockSpec((CONV_W, E_TN), lambda j, i: (0, j)),
            pl.BlockSpec((CONV_W, E_TN), lambda j, i: (0, j + E_NCB)),
            pl.BlockSpec((1, E_TN), lambda j, i: (0, j)),
            pl.BlockSpec((1, E_TN), lambda j, i: (0, j + E_NCB)),
            pl.BlockSpec((1, CONV_W - 1, E_TN), lambda j, i: (i // bps, 0, j)),
            pl.BlockSpec((1, CONV_W - 1, E_TN), lambda j, i: (i // bps, 0, j + E_NCB)),
        ],
        out_specs=[
            pl.BlockSpec((tm, E_TN), lambda j, i: (i, j)),
            pl.BlockSpec((1, SUBLANES, E_TN), lambda j, i: (i // bps, 0, j)),
            pl.BlockSpec((1, SUBLANES, E_TN), lambda j, i: (i // bps, 0, j)),
        ],
        out_shape=[
            jax.ShapeDtypeStruct((n, D_FF), BF16),
            jax.ShapeDtypeStruct((batch, SUBLANES, D_FF), F32),
            jax.ShapeDtypeStruct((batch, SUBLANES, D_FF), F32),
        ],
        scratch_shapes=[pltpu.VMEM((SUBLANES, E_TN), F32), pltpu.VMEM((SUBLANES, E_TN), F32)],
        compiler_params=_cparams(("arbitrary", "arbitrary")),
        name="ffn_up",
    )(h1b, w_up, w_up, conv_w, conv_w, conv_b, conv_b, conv_prev, conv_prev)


F_TK = 1408
F_NKB = D_FF // F_TK


def _ffn_down_kernel(act_ref, w_ref, h1_ref, g_ref, b_ref, y_ref, acc_ref):
    k = pl.program_id(1)
    part = jnp.dot(act_ref[...], w_ref[...], preferred_element_type=F32)

    @pl.when(k == 0)
    def _():
        acc_ref[...] = part

    @pl.when(k > 0)
    def _():
        acc_ref[...] += part

    @pl.when(k == F_NKB - 1)
    def _():
        y_ref[...] = _layer_norm(ALPHA * h1_ref[...] + acc_ref[...], g_ref[...], b_ref[...])


def _ffn_down(act, w_down, h1, g, b, tm):
    n = act.shape[0]
    return pl.pallas_call(
        _ffn_down_kernel,
        grid=(n // tm, F_NKB),
        in_specs=[
            pl.BlockSpec((tm, F_TK), lambda i, k: (i, k)),
            pl.BlockSpec((F_TK, D_MODEL), lambda i, k: (k, 0)),
            pl.BlockSpec((tm, D_MODEL), lambda i, k: (i, 0)),
            pl.BlockSpec((1, D_MODEL), lambda i, k: (0, 0)),
            pl.BlockSpec((1, D_MODEL), lambda i, k: (0, 0)),
        ],
        out_specs=pl.BlockSpec((tm, D_MODEL), lambda i, k: (i, 0)),
        out_shape=jax.ShapeDtypeStruct((n, D_MODEL), F32),
        scratch_shapes=[pltpu.VMEM((tm, D_MODEL), F32)],
        compiler_params=_cparams(("arbitrary", "arbitrary")),
        name="ffn_down",
    )(act, w_down, h1, g, b)


def _rope_tables(pos, reps):
    half = HEAD_DIM // 2
    inv = ROPE_THETA ** (-jnp.arange(half, dtype=F32) / half)
    ang = pos.astype(F32)[:, None] * inv[None, :]
    cos, sin = jnp.cos(ang), jnp.sin(ang)
    cos128 = jnp.concatenate([cos, cos, cos, cos], axis=1)
    sin128 = jnp.concatenate([-sin, sin, -sin, sin], axis=1)
    return jnp.tile(cos128, (reps, 1)), jnp.tile(sin128, (reps, 1))


def _permute_rw(a):
    o_wd, o_k, o_v = RW_WIDTH, RW_WIDTH + DECAY_LORA, 2 * RW_WIDTH + DECAY_LORA
    o_ad = 3 * RW_WIDTH + DECAY_LORA
    o_gd = o_ad + AAA_LORA
    pad = jnp.zeros(a.shape[:-1] + (LORA_W - DECAY_LORA - AAA_LORA - GATE_LORA,), a.dtype)
    return jnp.concatenate([a[..., :RW_WIDTH], a[..., o_k:o_k + RW_WIDTH], a[..., o_v:o_v + RW_WIDTH],
                            a[..., o_wd:o_wd + DECAY_LORA], a[..., o_ad:o_ad + AAA_LORA],
                            a[..., o_gd:o_gd + GATE_LORA], pad], axis=-1)


def _split_rw(a):
    return (a[..., :RW_WIDTH], a[..., RW_WIDTH:2 * RW_WIDTH], a[..., 2 * RW_WIDTH:3 * RW_WIDTH],
            a[..., 3 * RW_WIDTH:])


def _unpermute_shift(p_row):
    lo = P_LORA
    return jnp.concatenate([
        p_row[..., P_R:P_R + RW_WIDTH], p_row[..., lo:lo + DECAY_LORA],
        p_row[..., P_K:P_K + RW_WIDTH], p_row[..., P_V:P_V + RW_WIDTH],
        p_row[..., lo + DECAY_LORA:lo + DECAY_LORA + AAA_LORA],
        p_row[..., lo + DECAY_LORA + AAA_LORA:lo + DECAY_LORA + AAA_LORA + GATE_LORA]], axis=-1)


def _run(x, pos_tables, table_blocks, attn_fn, state_wkv, state_shift, state_conv, wts, tiles):
    batch, seq = x.shape[0], x.shape[1]
    n = batch * seq
    cos, sin = pos_tables
    p, h = _in_proj(x.reshape(n, D_MODEL), wts["ln_in_g"], wts["ln_in_b"], wts["w_in_p"], cos, sin,
                    tiles["a_tm"], table_blocks)
    att, k_win, v_win = attn_fn(p)
    shift_p = tuple(s for s in _split_rw(_permute_rw(state_shift)))
    rw, wkv = _rwkv(p, shift_p, state_wkv, wts["rw"], batch, seq, tiles["chunk"])
    h1, h1b = _out_proj(att, rw, wts["w_out"], h, wts["ln1_g"], wts["ln1_b"], tiles["d_tm"])
    act, cg, cv = _ffn_up(h1b, wts["w_up"], wts["conv_w"], wts["conv_b"], state_conv, batch, seq, tiles["e_tm"])
    y = _ffn_down(act, wts["w_down"], h1, wts["ln2_g"], wts["ln2_b"], tiles["f_tm"])
    p3 = p.reshape(batch, seq, P_COLS)
    shift_out = _unpermute_shift(p3[:, seq - 1:seq, :])
    conv_out = jnp.concatenate([cg[:, SUBLANES - 2:, :], cv[:, SUBLANES - 2:, :]], axis=-1)
    if k_win is None:
        k_win = p3[:, seq - WINDOW:, P_AK:P_AK + KV_WIDTH]
        v_win = p3[:, seq - WINDOW:, P_AV:P_AV + KV_WIDTH]
    kv_shape = (1, batch, WINDOW, ATT_KV_HEADS, HEAD_DIM)
    return (y.reshape(batch, seq, D_MODEL), k_win.reshape(kv_shape), v_win.reshape(kv_shape),
            wkv[None], shift_out[None], conv_out[None])


def kernel(x_prompt, x_sample, cache_k, cache_v, state_wkv, state_shift, state_ffn_conv, ln_in_g, ln_in_b, w_in, attn_sinks, rw_mu, rw_w0, rw_w2, rw_a0, rw_a2, rw_g2, rw_k_k, rw_k_a, rw_r_k, rw_lnx_g, rw_lnx_b, w_out, ln1_g, ln1_b, ffn_w_up, ffn_conv_w, ffn_conv_b, ffn_w_down, ln2_g, ln2_b):
    l = 0
    w = w_in[l]
    w_att, w_rw = w[:, :ATT_COLS], _permute_rw(w[:, ATT_COLS:])
    w_in_p = jnp.concatenate([w_rw[:, :3 * RW_WIDTH], w_att[:, :ATT_WIDTH], w_rw[:, 3 * RW_WIDTH:],
                              w_att[:, ATT_WIDTH:]], axis=1).astype(BF16)
    mu_r, mu_k, mu_v, mu_l = _split_rw(_permute_rw(rw_mu[l])[None])
    seg = lax.broadcasted_iota(jnp.int32, (SEG, SEG), 0) // HEAD_DIM
    bd = (seg == seg.T).astype(F32)
    row = lambda a: a.reshape(1, -1)
    wts = {
        "ln_in_g": row(ln_in_g), "ln_in_b": row(ln_in_b), "w_in_p": w_in_p,
        "rw": {"mu_r": mu_r, "mu_k": mu_k, "mu_v": mu_v, "mu_l": mu_l,
               "w0": row(rw_w0[l]), "a0": row(rw_a0[l]), "k_k": row(rw_k_k[l]), "k_a": row(rw_k_a[l]),
               "r_k": row(rw_r_k[l]), "lnx_g": row(rw_lnx_g[l]), "lnx_b": row(rw_lnx_b[l]),
               "w2": rw_w2[l], "a2": rw_a2[l], "g2": rw_g2[l], "bd": bd},
        "w_out": w_out[l].astype(BF16), "ln1_g": row(ln1_g[l]), "ln1_b": row(ln1_b[l]),
        "w_up": ffn_w_up[l].astype(BF16), "conv_w": ffn_conv_w[l], "conv_b": row(ffn_conv_b[l]),
        "w_down": ffn_w_down[l].astype(BF16), "ln2_g": row(ln2_g[l]), "ln2_b": row(ln2_b[l]),
    }
    sinks = attn_sinks[l]

    bp, tp = x_prompt.shape[0], x_prompt.shape[1]
    p_tiles = {"a_tm": 512, "chunk": CHUNK, "d_tm": 256, "e_tm": 512, "f_tm": 512}
    p_tables = _rope_tables(jnp.arange(tp, dtype=jnp.int32), 1)
    zeros = lambda *s: jnp.zeros(s, F32)
    y_p, p_k, p_v, p_wkv, p_shift, p_conv = _run(
        x_prompt, p_tables, tp // p_tiles["a_tm"],
        lambda p: (_attn_prompt(p, sinks, bp, tp), None, None),
        zeros(bp, RW_HEADS, HEAD_DIM, HEAD_DIM), zeros(bp, 1, RW_COLS), zeros(bp, CONV_W - 1, 2 * D_FF),
        wts, p_tiles)

    bs, ts = x_sample.shape[0], x_sample.shape[1]
    s_tiles = {"a_tm": bs * ts, "chunk": ts, "d_tm": bs * ts, "e_tm": ts, "f_tm": bs * ts}
    s_tables = _rope_tables(PAST_LEN + jnp.arange(ts, dtype=jnp.int32), bs)
    ck = cache_k[l].reshape(bs * WINDOW, KV_WIDTH)
    cv = cache_v[l].reshape(bs * WINDOW, KV_WIDTH)
    y_s, s_k, s_v, s_wkv, s_shift, s_conv = _run(
        x_sample, s_tables, 1,
        lambda p: _attn_sample(p, sinks, ck, cv, bs, ts),
        state_wkv[l], state_shift[l], state_ffn_conv[l], wts, s_tiles)

    return (y_p, y_s, p_k, p_v, p_wkv, p_shift, p_conv, s_k, s_v, s_wkv, s_shift, s_conv)
```

```python
import functools
import math

import jax
import jax.numpy as jnp
from jax import lax
from jax.experimental import pallas as pl
from jax.experimental.pallas import tpu as pltpu

D_MODEL = 2048
HEAD_DIM = 64
ATT_HEADS = 16
ATT_KV_HEADS = 2
ATT_GROUP = ATT_HEADS // ATT_KV_HEADS
ATT_WIDTH = ATT_HEADS * HEAD_DIM
KV_WIDTH = ATT_KV_HEADS * HEAD_DIM
CHUNK = 64
WINDOW = 128
ROPE_THETA = 10000.0
ATT_SCALE = HEAD_DIM ** -0.5
RW_HEADS = 16
RW_WIDTH = RW_HEADS * HEAD_DIM
DECAY_LORA = 64
AAA_LORA = 64
GATE_LORA = 160
LNX_EPS = 64e-5
ATT_COLS = ATT_WIDTH + 2 * KV_WIDTH
RW_COLS = 3 * RW_WIDTH + DECAY_LORA + AAA_LORA + GATE_LORA
D_FF = 5632
CONV_W = 3
LN_EPS = 1e-5
DEPTH = 1
ALPHA = (2 * DEPTH) ** 0.25
PAST_LEN = 1024

LORA_W = 512
P_R, P_K, P_V, P_Q = 0, 1024, 2048, 3072
P_LORA = 4096
P_AK = P_LORA + LORA_W
P_AV = P_AK + KV_WIDTH
P_COLS = P_AV + KV_WIDTH
LANES = 128
SUBLANES = 8
VMEM_LIMIT = 56 * 1024 * 1024

F32 = jnp.float32
BF16 = jnp.bfloat16


def _cparams(sem):
    return pltpu.CompilerParams(dimension_semantics=sem, vmem_limit_bytes=VMEM_LIMIT)


def _layer_norm(y, g, b):
    mu = jnp.mean(y, -1, keepdims=True)
    d = y - mu
    var = jnp.mean(d * d, -1, keepdims=True)
    return d * lax.rsqrt(var + LN_EPS) * g + b


A_TN = 256
A_Q_BLOCKS = (P_Q // A_TN, (P_Q + ATT_WIDTH) // A_TN)
A_K_BLOCK = P_AK // A_TN


def _rope128(x, cos, sin_signed):
    lane = lax.broadcasted_iota(jnp.int32, x.shape, 1)
    first_half = (lane % HEAD_DIM) < (HEAD_DIM // 2)
    rot = jnp.where(first_half, pltpu.roll(x, LANES - HEAD_DIM // 2, 1), pltpu.roll(x, HEAD_DIM // 2, 1))
    return x * cos + rot * sin_signed


def _in_proj_kernel(x_ref, g_ref, b_ref, w_ref, cos_ref, sin_ref, p_ref, h_ref, hb_ref):
    j = pl.program_id(1)

    @pl.when(j == 0)
    def _():
        h = _layer_norm(x_ref[...], g_ref[...], b_ref[...])
        h_ref[...] = h
        hb_ref[...] = h.astype(BF16)

    acc = jnp.dot(hb_ref[...], w_ref[...], preferred_element_type=F32)
    is_q = jnp.logical_and(j >= A_Q_BLOCKS[0], j < A_Q_BLOCKS[1])
    is_k = j == A_K_BLOCK

    @pl.when(is_q)
    def _():
        cos, sin = cos_ref[...], sin_ref[...]
        p_ref[:, :LANES] = _rope128(acc[:, :LANES], cos, sin)
        p_ref[:, LANES:] = _rope128(acc[:, LANES:], cos, sin)

    @pl.when(is_k)
    def _():
        p_ref[:, :LANES] = _rope128(acc[:, :LANES], cos_ref[...], sin_ref[...])
        p_ref[:, LANES:] = acc[:, LANES:]

    @pl.when(jnp.logical_not(jnp.logical_or(is_q, is_k)))
    def _():
        p_ref[...] = acc


def _in_proj(x, g, b, w_p, cos, sin, tm, table_blocks):
    n = x.shape[0]
    grid = (n // tm, P_COLS // A_TN)
    return pl.pallas_call(
        _in_proj_kernel,
        grid=grid,
        in_specs=[
            pl.BlockSpec((tm, D_MODEL), lambda i, j: (i, 0)),
            pl.BlockSpec((1, D_MODEL), lambda i, j: (0, 0)),
            pl.BlockSpec((1, D_MODEL), lambda i, j: (0, 0)),
            pl.BlockSpec((D_MODEL, A_TN), lambda i, j: (0, j)),
            pl.BlockSpec((tm, LANES), lambda i, j: (i % table_blocks, 0)),
            pl.BlockSpec((tm, LANES), lambda i, j: (i % table_blocks, 0)),
        ],
        out_specs=[
            pl.BlockSpec((tm, A_TN), lambda i, j: (i, j)),
            pl.BlockSpec((tm, D_MODEL), lambda i, j: (i, 0)),
        ],
        out_shape=[
            jax.ShapeDtypeStruct((n, P_COLS), F32),
            jax.ShapeDtypeStruct((n, D_MODEL), F32),
        ],
        scratch_shapes=[pltpu.VMEM((tm, D_MODEL), BF16)],
        compiler_params=_cparams(("arbitrary", "arbitrary")),
        name="in_proj",
    )(x, g, b, w_p, cos, sin)


def _attend(q, k, v, valid, sink_ref):
    r = q.shape[0]
    outs = []
    for kv in range(ATT_KV_HEADS):
        kh = k[:, kv * HEAD_DIM:(kv + 1) * HEAD_DIM].astype(BF16)
        vh = v[:, kv * HEAD_DIM:(kv + 1) * HEAD_DIM].astype(BF16)
        heads = [kv * ATT_GROUP + g for g in range(ATT_GROUP)]
        qs = jnp.concatenate([q[:, h * HEAD_DIM:(h + 1) * HEAD_DIM] for h in heads], axis=0)
        sink = jnp.concatenate([jnp.full((r, 1), sink_ref[h], F32) for h in heads], axis=0)
        s = lax.dot_general(qs.astype(BF16), kh, (((1,), (1,)), ((), ())), preferred_element_type=F32) * ATT_SCALE
        if valid is not None:
            s = jnp.where(valid, s, -jnp.inf)
        m = jnp.maximum(jnp.max(s, -1, keepdims=True), sink)
        p = jnp.exp(s - m)
        den = jnp.sum(p, -1, keepdims=True) + jnp.exp(sink - m)
        o = jnp.dot(p.astype(BF16), vh, preferred_element_type=F32) / den
        outs.extend(o[g * r:(g + 1) * r] for g in range(ATT_GROUP))
    return jnp.concatenate(outs, axis=1)


def _attn_prompt_kernel(sink_ref, q_ref, k0_ref, k1_ref, k2_ref, v0_ref, v1_ref, v2_ref, o_ref):
    n = pl.program_id(1)
    k = jnp.concatenate([k0_ref[...], k1_ref[...], k2_ref[...]], axis=0)
    v = jnp.concatenate([v0_ref[...], v1_ref[...], v2_ref[...]], axis=0)
    slot = lax.broadcasted_iota(jnp.int32, (1, 3 * CHUNK), 1) // CHUNK
    valid = (n - 2 + slot) >= 0
    o_ref[...] = _attend(q_ref[...], k, v, valid, sink_ref).astype(o_ref.dtype)


def _attn_prompt(p, sinks, batch, seq):
    nc = seq // CHUNK
    kcol, vcol = P_AK // KV_WIDTH, P_AV // KV_WIDTH

    def kv_spec(col, back):
        return pl.BlockSpec((CHUNK, KV_WIDTH), lambda b, n: (b * nc + jnp.maximum(n - back, 0), col))

    return pl.pallas_call(
        _attn_prompt_kernel,
        grid=(batch, nc),
        in_specs=[
            pl.BlockSpec(memory_space=pltpu.SMEM),
            pl.BlockSpec((CHUNK, ATT_WIDTH), lambda b, n: (b * nc + n, P_Q // ATT_WIDTH)),
            kv_spec(kcol, 2), kv_spec(kcol, 1), kv_spec(kcol, 0),
            kv_spec(vcol, 2), kv_spec(vcol, 1), kv_spec(vcol, 0),
        ],
        out_specs=pl.BlockSpec((CHUNK, ATT_WIDTH), lambda b, n: (b * nc + n, 0)),
        out_shape=jax.ShapeDtypeStruct((batch * seq, ATT_WIDTH), BF16),
        compiler_params=_cparams(("arbitrary", "arbitrary")),
        name="attn_prompt",
    )(sinks, p, p, p, p, p, p, p)


def _attn_sample_kernel(sink_ref, q_ref, kn_ref, vn_ref, kc_ref, vc_ref, o_ref, kw_ref, vw_ref):
    t = q_ref.shape[0]
    k = jnp.concatenate([kc_ref[...], kn_ref[...]], axis=0)
    v = jnp.concatenate([vc_ref[...], vn_ref[...]], axis=0)
    o_ref[...] = _attend(q_ref[...], k, v, None, sink_ref).astype(o_ref.dtype)
    kw_ref[...] = k[t:]
    vw_ref[...] = v[t:]


def _attn_sample(p, sinks, cache_k, cache_v, batch, seq):
    kcol, vcol = P_AK // KV_WIDTH, P_AV // KV_WIDTH
    return pl.pallas_call(
        _attn_sample_kernel,
        grid=(batch,),
        in_specs=[
            pl.BlockSpec(memory_space=pltpu.SMEM),
            pl.BlockSpec((seq, ATT_WIDTH), lambda b: (b, P_Q // ATT_WIDTH)),
            pl.BlockSpec((seq, KV_WIDTH), lambda b: (b, kcol)),
            pl.BlockSpec((seq, KV_WIDTH), lambda b: (b, vcol)),
            pl.BlockSpec((WINDOW, KV_WIDTH), lambda b: (b, 0)),
            pl.BlockSpec((WINDOW, KV_WIDTH), lambda b: (b, 0)),
        ],
        out_specs=[
            pl.BlockSpec((seq, ATT_WIDTH), lambda b: (b, 0)),
            pl.BlockSpec((WINDOW, KV_WIDTH), lambda b: (b, 0)),
            pl.BlockSpec((WINDOW, KV_WIDTH), lambda b: (b, 0)),
        ],
        out_shape=[
            jax.ShapeDtypeStruct((batch * seq, ATT_WIDTH), BF16),
            jax.ShapeDtypeStruct((batch * WINDOW, KV_WIDTH), F32),
            jax.ShapeDtypeStruct((batch * WINDOW, KV_WIDTH), F32),
        ],
        compiler_params=_cparams(("arbitrary",)),
        name="attn_sample",
    )(sinks, p, p, p, cache_k, cache_v)


SEG = 256
PAIR = 2 * HEAD_DIM
N_PAIRS = RW_HEADS // 2


def _split2(x):
    hi = x.astype(BF16)
    lo = (x - hi.astype(F32)).astype(BF16)
    return hi, lo


def _seg_sum(x, bd):
    c = x.shape[0]
    slabs = range(0, RW_WIDTH, SEG)
    stack = jnp.concatenate([part[:, s:s + SEG] for part in _split2(x) for s in slabs], axis=0)
    r = jnp.dot(stack, bd, preferred_element_type=F32)
    n = len(slabs)
    return jnp.concatenate([r[i * c:(i + 1) * c] + r[(n + i) * c:(n + i + 1) * c] for i in range(n)], axis=1)


def _shift_rows(x, prev_row):
    row = lax.broadcasted_iota(jnp.int32, x.shape, 0)
    return jnp.where(row == 0, prev_row, pltpu.roll(x, 1, 0))


def _dotb(a, b):
    return jnp.dot(a.astype(BF16), b.astype(BF16), preferred_element_type=F32)


def _dotb_nt(a, b):
    return lax.dot_general(a.astype(BF16), b.astype(BF16), (((1,), (1,)), ((), ())), preferred_element_type=F32)


def _dot3_tn(a, b):
    ah, al = _split2(a)
    bh, bl = _split2(b)
    return lax.dot_general(jnp.concatenate([ah, al, ah], axis=0), jnp.concatenate([bh, bh, bl], axis=0),
                           (((0,), (0,)), ((), ())), preferred_element_type=F32)


def _dot3(a, b):
    ah, al = _split2(a)
    bh, bl = _split2(b)
    return jnp.dot(jnp.concatenate([ah, al, ah], axis=1), jnp.concatenate([bh, bh, bl], axis=0),
                   preferred_element_type=F32)


def _stack_heads(x, half):
    lane = lax.broadcasted_iota(jnp.int32, x.shape, 1)
    first = (lane % (2 * half)) < half
    return jnp.concatenate([jnp.where(first, x, 0.0), jnp.where(first, 0.0, x)], axis=0)


def _rwkv_pairs(ab, rb, bb, kb, v, gam_c, s_bd, n_double):
    c = ab[0].shape[0]
    pairs = range(len(ab))
    t_idx = lax.broadcasted_iota(jnp.int32, (c, 2 * c), 0)
    s_idx = lax.broadcasted_iota(jnp.int32, (c, 2 * c), 1) % c
    strict, incl = s_idx < t_idx, s_idx <= t_idx
    g = [_dotb_nt(jnp.concatenate([ab[p], rb[p]], axis=0),
                  jnp.concatenate([_stack_heads(bb[p], HEAD_DIM), _stack_heads(kb[p], HEAD_DIM)], axis=0))
         for p in pairs]
    l_ak = [jnp.where(strict, g[p][:c, 2 * c:], 0.0) for p in pairs]
    m_rbk = [jnp.concatenate([jnp.where(incl, g[p][c:, :2 * c], 0.0), jnp.where(incl, g[p][c:, 2 * c:], 0.0)], axis=1)
             for p in pairs]
    lp = [jnp.where(strict, g[p][:c, :2 * c], 0.0) for p in pairs]
    v_st = [_stack_heads(v[p], HEAD_DIM) for p in pairs]
    ya = list(ab)
    yu = [_dotb(l_ak[p], v_st[p]) for p in pairs]
    for i in range(n_double):
        y_st = [_stack_heads(jnp.concatenate([ya[p], yu[p]], axis=1), HEAD_DIM) for p in pairs]
        if i + 1 < n_double:
            z = [_dotb(lp[p], jnp.concatenate([y_st[p], _stack_heads(lp[p], c)], axis=1)) for p in pairs]
            lp = [z[p][:, 2 * PAIR:] for p in pairs]
        else:
            z = [_dotb(lp[p], y_st[p]) for p in pairs]
        ya = [ya[p] + z[p][:, :PAIR] for p in pairs]
        yu = [yu[p] + z[p][:, PAIR:2 * PAIR] for p in pairs]
    z = []
    for p in pairs:
        top = jnp.concatenate([_stack_heads(ya[p], HEAD_DIM), _stack_heads(yu[p], HEAD_DIM)], axis=1)
        bot = jnp.concatenate([jnp.zeros_like(v_st[p]), v_st[p]], axis=1)
        z.append(_dotb(m_rbk[p], jnp.concatenate([top, bot], axis=0)))
    o = [_dotb_nt(rb[p] + z[p][:, :PAIR], s_bd[p]) + z[p][:, PAIR:] for p in pairs]
    row_h = lax.broadcasted_iota(jnp.int32, (PAIR, PAIR), 0) // HEAD_DIM
    col_h = lax.broadcasted_iota(jnp.int32, (PAIR, PAIR), 1) // HEAD_DIM
    same = row_h == col_h
    q_t = [jnp.where(same, _dot3_tn(ya[p], bb[p]), 0.0) for p in pairs]
    d_t = [jnp.where(same, _dot3_tn(jnp.concatenate([yu[p], v[p]], axis=0),
                                    jnp.concatenate([bb[p], kb[p]], axis=0)), 0.0) for p in pairs]
    s_new = [(s_bd[p] + _dot3(s_bd[p], q_t[p]) + d_t[p]) * gam_c[p] for p in pairs]
    return o, s_new


def _rwkv_kernel(xr_ref, xk_ref, xv_ref, xl_ref, sr_ref, sk_ref, sv_ref, sl_ref,
                 mur_ref, muk_ref, muv_ref, mul_ref, w0_ref, a0_ref, kk_ref, ka_ref, rk_ref,
                 lg_ref, lb_ref, w2_ref, a2_ref, g2_ref, bd_ref, st0_ref,
                 out_ref, st_ref,
                 s_scr, pr_scr, pk_scr, pv_scr, pl_scr):
    c = pl.program_id(1)
    nc = pl.num_programs(1)
    C = xr_ref.shape[0]
    n_double = int(math.log2(C))

    @pl.when(c == 0)
    def _():
        s_scr[...] = jnp.zeros_like(s_scr)
        for h in range(RW_HEADS):
            off = (h % 2) * HEAD_DIM
            s_scr[h // 2, off:off + HEAD_DIM, off:off + HEAD_DIM] = st0_ref[0, h]
        pr_scr[0:1, :] = sr_ref[0]
        pk_scr[0:1, :] = sk_ref[0]
        pv_scr[0:1, :] = sv_ref[0]
        pl_scr[0:1, :] = sl_ref[0]

    def shifted(x_ref, p_scr, mu_ref):
        x = x_ref[...]
        prev = _shift_rows(x, p_scr[0:1, :])
        p_scr[0:1, :] = x[C - 1:C, :]
        return x + (prev - x) * mu_ref[...]

    r = shifted(xr_ref, pr_scr, mur_ref)
    kr = shifted(xk_ref, pk_scr, muk_ref)
    v = shifted(xv_ref, pv_scr, muv_ref)
    xl = shifted(xl_ref, pl_scr, mul_ref)
    wd = xl[:, 0:DECAY_LORA]
    ad = xl[:, DECAY_LORA:DECAY_LORA + AAA_LORA]
    gd = xl[:, DECAY_LORA + AAA_LORA:DECAY_LORA + AAA_LORA + GATE_LORA]

    w_log = -jax.nn.softplus(-(w0_ref[...] + _dotb(jnp.tanh(wd), w2_ref[...]))) - 0.5
    lw = -jnp.exp(w_log)
    a = jax.nn.sigmoid(a0_ref[...] + _dotb(ad, a2_ref[...]))
    g = _dotb(jax.nn.sigmoid(gd), g2_ref[...])
    bd = bd_ref[...]
    kk = kr * kk_ref[...]
    kk = kk / jnp.maximum(jnp.sqrt(_seg_sum(kk * kk, bd)), 1e-12)
    k2 = kr * (1.0 + (a - 1.0) * ka_ref[...])

    ti = lax.broadcasted_iota(jnp.int32, (C, C), 0)
    si = lax.broadcasted_iota(jnp.int32, (C, C), 1)
    tri = (si <= ti).astype(BF16)
    lw_hi = lw.astype(BF16)
    lw_mid, lw_lo = _split2(lw - lw_hi.astype(F32))
    cs3 = jnp.dot(tri, jnp.concatenate([lw_hi, lw_mid, lw_lo], axis=1), preferred_element_type=F32)
    cs = cs3[:, :RW_WIDTH] + cs3[:, RW_WIDTH:2 * RW_WIDTH] + cs3[:, 2 * RW_WIDTH:]
    gam = jnp.exp(cs)
    ginv = jnp.exp(-cs)
    ab = -kk * jnp.exp(cs - lw)
    rb = r * gam
    bb = kk * a * ginv
    kb = k2 * ginv
    gam_c = gam[C - 1:C, :]

    per_pair = lambda x: [x[:, p * PAIR:(p + 1) * PAIR] for p in range(N_PAIRS)]
    outs, s_new = _rwkv_pairs(per_pair(ab), per_pair(rb), per_pair(bb), per_pair(kb), per_pair(v), per_pair(gam_c),
                              [s_scr[p] for p in range(N_PAIRS)], n_double)
    for p in range(N_PAIRS):
        s_scr[p] = s_new[p]
    o = jnp.concatenate(outs, axis=1)

    mo = _seg_sum(o, bd) * (1.0 / HEAD_DIM)
    d = o - mo
    vo = _seg_sum(d * d, bd) * (1.0 / HEAD_DIM)
    on = d * lax.rsqrt(vo + LNX_EPS) * lg_ref[...] + lb_ref[...]
    bonus = _seg_sum(r * k2 * rk_ref[...], bd) * v
    out_ref[...] = ((on + bonus) * g).astype(out_ref.dtype)

    @pl.when(c == nc - 1)
    def _():
        for h in range(RW_HEADS):
            off = (h % 2) * HEAD_DIM
            st_ref[0, h] = s_scr[h // 2, off:off + HEAD_DIM, off:off + HEAD_DIM]


def _rwkv(p, shift_p, state0, prm, batch, seq, chunk):
    nc = seq // chunk
    row = lambda w: pl.BlockSpec((1, w), lambda b, c: (0, 0))
    xspec = lambda w, col: pl.BlockSpec((chunk, w), lambda b, c: (b * nc + c, col))
    sspec = lambda w: pl.BlockSpec((1, 1, w), lambda b, c: (b, 0, 0))
    full = lambda a: pl.BlockSpec(a.shape, lambda b, c: (0,) * a.ndim)
    sr, sk, sv, sl = shift_p
    return pl.pallas_call(
        _rwkv_kernel,
        grid=(batch, nc),
        in_specs=[
            xspec(RW_WIDTH, P_R // RW_WIDTH), xspec(RW_WIDTH, P_K // RW_WIDTH), xspec(RW_WIDTH, P_V // RW_WIDTH),
            xspec(LORA_W, P_LORA // LORA_W),
            sspec(RW_WIDTH), sspec(RW_WIDTH), sspec(RW_WIDTH), sspec(LORA_W),
            row(RW_WIDTH), row(RW_WIDTH), row(RW_WIDTH), row(LORA_W),
            row(RW_WIDTH), row(RW_WIDTH), row(RW_WIDTH), row(RW_WIDTH), row(RW_WIDTH),
            row(RW_WIDTH), row(RW_WIDTH),
            full(prm["w2"]), full(prm["a2"]), full(prm["g2"]), full(prm["bd"]),
            pl.BlockSpec((1, RW_HEADS, HEAD_DIM, HEAD_DIM), lambda b, c: (b, 0, 0, 0)),
        ],
        out_specs=[
            pl.BlockSpec((chunk, RW_WIDTH), lambda b, c: (b * nc + c, 0)),
            pl.BlockSpec((1, RW_HEADS, HEAD_DIM, HEAD_DIM), lambda b, c: (b, 0, 0, 0)),
        ],
        out_shape=[
            jax.ShapeDtypeStruct((batch * seq, RW_WIDTH), BF16),
            jax.ShapeDtypeStruct((batch, RW_HEADS, HEAD_DIM, HEAD_DIM), F32),
        ],
        scratch_shapes=[
            pltpu.VMEM((N_PAIRS, PAIR, PAIR), F32),
            pltpu.VMEM((SUBLANES, RW_WIDTH), F32), pltpu.VMEM((SUBLANES, RW_WIDTH), F32),
            pltpu.VMEM((SUBLANES, RW_WIDTH), F32), pltpu.VMEM((SUBLANES, LORA_W), F32),
        ],
        compiler_params=_cparams(("arbitrary", "arbitrary")),
        name="rwkv",
    )(p, p, p, p, sr, sk, sv, sl,
      prm["mu_r"], prm["mu_k"], prm["mu_v"], prm["mu_l"], prm["w0"], prm["a0"], prm["k_k"], prm["k_a"], prm["r_k"],
      prm["lnx_g"], prm["lnx_b"], prm["w2"], prm["a2"], prm["g2"], prm["bd"], state0)


def _out_proj_kernel(att_ref, rw_ref, wa_ref, wb_ref, h_ref, g_ref, b_ref, h1_ref, h1b_ref):
    m = jnp.dot(att_ref[...], wa_ref[...], preferred_element_type=F32)
    m = m + jnp.dot(rw_ref[...], wb_ref[...], preferred_element_type=F32)
    y = _layer_norm(ALPHA * h_ref[...] + m, g_ref[...], b_ref[...])
    h1_ref[...] = y
    h1b_ref[...] = y.astype(BF16)


def _out_proj(att, rw, w_out, h, g, b, tm):
    n = att.shape[0]
    return pl.pallas_call(
        _out_proj_kernel,
        grid=(n // tm,),
        in_specs=[
            pl.BlockSpec((tm, ATT_WIDTH), lambda i: (i, 0)),
            pl.BlockSpec((tm, RW_WIDTH), lambda i: (i, 0)),
            pl.BlockSpec((ATT_WIDTH, D_MODEL), lambda i: (0, 0)),
            pl.BlockSpec((RW_WIDTH, D_MODEL), lambda i: (1, 0)),
            pl.BlockSpec((tm, D_MODEL), lambda i: (i, 0)),
            pl.BlockSpec((1, D_MODEL), lambda i: (0, 0)),
            pl.BlockSpec((1, D_MODEL), lambda i: (0, 0)),
        ],
        out_specs=[
            pl.BlockSpec((tm, D_MODEL), lambda i: (i, 0)),
            pl.BlockSpec((tm, D_MODEL), lambda i: (i, 0)),
        ],
        out_shape=[
            jax.ShapeDtypeStruct((n, D_MODEL), F32),
            jax.ShapeDtypeStruct((n, D_MODEL), BF16),
        ],
        compiler_params=_cparams(("arbitrary",)),
        name="out_proj",
    )(att, rw, w_out, w_out, h, g, b)


E_TN = 512
E_NCB = D_FF // E_TN
SQRT_HALF = math.sqrt(0.5)


def _ffn_up_kernel(bps, h_ref, wg_ref, wv_ref, cwg_ref, cwv_ref, cbg_ref, cbv_ref, sg_ref, sv_ref,
                   act_ref, og_ref, ov_ref, cg_scr, cv_scr):
    i = pl.program_id(1)
    first = (i % bps) == 0
    h = h_ref[...]
    tm = h.shape[0]
    row = lax.broadcasted_iota(jnp.int32, (tm, E_TN), 0)

    def conv(w_ref, cw_ref, cb_ref, st_ref, c_scr, o_ref):
        u = jnp.dot(h, w_ref[...], preferred_element_type=F32)
        prev = jnp.where(first, st_ref[0], c_scr[SUBLANES - 2:SUBLANES, :])
        p2, p1 = prev[0:1, :], prev[1:2, :]
        sh1 = jnp.where(row == 0, p1, pltpu.roll(u, 1, 0))
        sh2 = jnp.where(row == 0, p2, jnp.where(row == 1, p1, pltpu.roll(u, 2, 0)))
        cw = cw_ref[...]
        c = cb_ref[...] + sh2 * cw[0:1, :]
        c = c + sh1 * cw[1:2, :]
        c = c + u * cw[2:3, :]
        tail = u[tm - SUBLANES:tm, :]
        c_scr[...] = tail
        o_ref[0] = tail
        return c

    gate = conv(wg_ref, cwg_ref, cbg_ref, sg_ref, cg_scr, og_ref)
    val = conv(wv_ref, cwv_ref, cbv_ref, sv_ref, cv_scr, ov_ref)
    gelu = 0.5 * gate * (1.0 + lax.erf(gate * SQRT_HALF))
    act_ref[...] = (gelu * val).astype(act_ref.dtype)


def _ffn_up(h1b, w_up, conv_w, conv_b, conv_prev, batch, seq, tm):
    n = h1b.shape[0]
    bps = seq // tm
    nrb = n // tm
    return pl.pallas_call(
        functools.partial(_ffn_up_kernel, bps),
        grid=(E_NCB, nrb),
        in_specs=[
            pl.BlockSpec((tm, D_MODEL), lambda j, i: (i, 0)),
            pl.BlockSpec((D_MODEL, E_TN), lambda j, i: (0, j)),
            pl.BlockSpec((D_MODEL, E_TN), lambda j, i: (0, j + E_NCB)),
            pl.BlockSpec((CONV_W, E_TN), lambda j, i: (0, j)),
            pl.BlockSpec((CONV_W, E_TN), lambda j, i: (0, j + E_NCB)),
            pl.BlockSpec((1, E_TN), lambda j, i: (0, j)),
            pl.BlockSpec((1, E_TN), lambda j, i: (0, j + E_NCB)),
            pl.BlockSpec((1, CONV_W - 1, E_TN), lambda j, i: (i // bps, 0, j)),
            pl.BlockSpec((1, CONV_W - 1, E_TN), lambda j, i: (i // bps, 0, j + E_NCB)),
        ],
        out_specs=[
            pl.BlockSpec((tm, E_TN), lambda j, i: (i, j)),
            pl.BlockSpec((1, SUBLANES, E_TN), lambda j, i: (i // bps, 0, j)),
            pl.BlockSpec((1, SUBLANES, E_TN), lambda j, i: (i // bps, 0, j)),
        ],
        out_shape=[
            jax.ShapeDtypeStruct((n, D_FF), BF16),
            jax.ShapeDtypeStruct((batch, SUBLANES, D_FF), F32),
            jax.ShapeDtypeStruct((batch, SUBLANES, D_FF), F32),
        ],
        scratch_shapes=[pltpu.VMEM((SUBLANES, E_TN), F32), pltpu.VMEM((SUBLANES, E_TN), F32)],
        compiler_params=_cparams(("arbitrary", "arbitrary")),
        name="ffn_up",
    )(h1b, w_up, w_up, conv_w, conv_w, conv_b, conv_b, conv_prev, conv_prev)


F_TK = 1408
F_NKB = D_FF // F_TK


def _ffn_down_kernel(act_ref, w_ref, h1_ref, g_ref, b_ref, y_ref, acc_ref):
    k = pl.program_id(1)
    part = jnp.dot(act_ref[...], w_ref[...], preferred_element_type=F32)

    @pl.when(k == 0)
    def _():
        acc_ref[...] = part

    @pl.when(k > 0)
    def _():
        acc_ref[...] += part

    @pl.when(k == F_NKB - 1)
    def _():
        y_ref[...] = _layer_norm(ALPHA * h1_ref[...] + acc_ref[...], g_ref[...], b_ref[...])


def _ffn_down(act, w_down, h1, g, b, tm):
    n = act.shape[0]
    return pl.pallas_call(
        _ffn_down_kernel,
        grid=(n // tm, F_NKB),
        in_specs=[
            pl.BlockSpec((tm, F_TK), lambda i, k: (i, k)),
            pl.BlockSpec((F_TK, D_MODEL), lambda i, k: (k, 0)),
            pl.BlockSpec((tm, D_MODEL), lambda i, k: (i, 0)),
            pl.BlockSpec((1, D_MODEL), lambda i, k: (0, 0)),
            pl.BlockSpec((1, D_MODEL), lambda i, k: (0, 0)),
        ],
        out_specs=pl.BlockSpec((tm, D_MODEL), lambda i, k: (i, 0)),
        out_shape=jax.ShapeDtypeStruct((n, D_MODEL), F32),
        scratch_shapes=[pltpu.VMEM((tm, D_MODEL), F32)],
        compiler_params=_cparams(("arbitrary", "arbitrary")),
        name="ffn_down",
    )(act, w_down, h1, g, b)


def _rope_tables(pos, reps):
    half = HEAD_DIM // 2
    inv = ROPE_THETA ** (-jnp.arange(half, dtype=F32) / half)
    ang = pos.astype(F32)[:, None] * inv[None, :]
    cos, sin = jnp.cos(ang), jnp.sin(ang)
    cos128 = jnp.concatenate([cos, cos, cos, cos], axis=1)
    sin128 = jnp.concatenate([-sin, sin, -sin, sin], axis=1)
    return jnp.tile(cos128, (reps, 1)), jnp.tile(sin128, (reps, 1))


def _permute_rw(a):
    o_wd, o_k, o_v = RW_WIDTH, RW_WIDTH + DECAY_LORA, 2 * RW_WIDTH + DECAY_LORA
    o_ad = 3 * RW_WIDTH + DECAY_LORA
    o_gd = o_ad + AAA_LORA
    pad = jnp.zeros(a.shape[:-1] + (LORA_W - DECAY_LORA - AAA_LORA - GATE_LORA,), a.dtype)
    return jnp.concatenate([a[..., :RW_WIDTH], a[..., o_k:o_k + RW_WIDTH], a[..., o_v:o_v + RW_WIDTH],
                            a[..., o_wd:o_wd + DECAY_LORA], a[..., o_ad:o_ad + AAA_LORA],
                            a[..., o_gd:o_gd + GATE_LORA], pad], axis=-1)


def _split_rw(a):
    return (a[..., :RW_WIDTH], a[..., RW_WIDTH:2 * RW_WIDTH], a[..., 2 * RW_WIDTH:3 * RW_WIDTH],
            a[..., 3 * RW_WIDTH:])


def _unpermute_shift(p_row):
    lo = P_LORA
    return jnp.concatenate([
        p_row[..., P_R:P_R + RW_WIDTH], p_row[..., lo:lo + DECAY_LORA],
        p_row[..., P_K:P_K + RW_WIDTH], p_row[..., P_V:P_V + RW_WIDTH],
        p_row[..., lo + DECAY_LORA:lo + DECAY_LORA + AAA_LORA],
        p_row[..., lo + DECAY_LORA + AAA_LORA:lo + DECAY_LORA + AAA_LORA + GATE_LORA]], axis=-1)


def _run(x, pos_tables, table_blocks, attn_fn, state_wkv, state_shift, state_conv, wts, tiles):
    batch, seq = x.shape[0], x.shape[1]
    n = batch * seq
    cos, sin = pos_tables
    p, h = _in_proj(x.reshape(n, D_MODEL), wts["ln_in_g"], wts["ln_in_b"], wts["w_in_p"], cos, sin,
                    tiles["a_tm"], table_blocks)
    att, k_win, v_win = attn_fn(p)
    shift_p = tuple(s for s in _split_rw(_permute_rw(state_shift)))
    rw, wkv = _rwkv(p, shift_p, state_wkv, wts["rw"], batch, seq, tiles["chunk"])
    h1, h1b = _out_proj(att, rw, wts["w_out"], h, wts["ln1_g"], wts["ln1_b"], tiles["d_tm"])
    act, cg, cv = _ffn_up(h1b, wts["w_up"], wts["conv_w"], wts["conv_b"], state_conv, batch, seq, tiles["e_tm"])
    y = _ffn_down(act, wts["w_down"], h1, wts["ln2_g"], wts["ln2_b"], tiles["f_tm"])
    p3 = p.reshape(batch, seq, P_COLS)
    shift_out = _unpermute_shift(p3[:, seq - 1:seq, :])
    conv_out = jnp.concatenate([cg[:, SUBLANES - 2:, :], cv[:, SUBLANES - 2:, :]], axis=-1)
    if k_win is None:
        k_win = p3[:, seq - WINDOW:, P_AK:P_AK + KV_WIDTH]
        v_win = p3[:, seq - WINDOW:, P_AV:P_AV + KV_WIDTH]
    kv_shape = (1, batch, WINDOW, ATT_KV_HEADS, HEAD_DIM)
    return (y.reshape(batch, seq, D_MODEL), k_win.reshape(kv_shape), v_win.reshape(kv_shape),
            wkv[None], shift_out[None], conv_out[None])


def kernel(x_prompt, x_sample, cache_k, cache_v, state_wkv, state_shift, state_ffn_conv, ln_in_g, ln_in_b, w_in, attn_sinks, rw_mu, rw_w0, rw_w2, rw_a0, rw_a2, rw_g2, rw_k_k, rw_k_a, rw_r_k, rw_lnx_g, rw_lnx_b, w_out, ln1_g, ln1_b, ffn_w_up, ffn_conv_w, ffn_conv_b, ffn_w_down, ln2_g, ln2_b):
    l = 0
    w = w_in[l]
    w_att, w_rw = w[:, :ATT_COLS], _permute_rw(w[:, ATT_COLS:])
    w_in_p = jnp.concatenate([w_rw[:, :3 * RW_WIDTH], w_att[:, :ATT_WIDTH], w_rw[:, 3 * RW_WIDTH:],
                              w_att[:, ATT_WIDTH:]], axis=1).astype(BF16)
    mu_r, mu_k, mu_v, mu_l = _split_rw(_permute_rw(rw_mu[l])[None])
    seg = lax.broadcasted_iota(jnp.int32, (SEG, SEG), 0) // HEAD_DIM
    bd = (seg == seg.T).astype(BF16)
    row = lambda a: a.reshape(1, -1)
    wts = {
        "ln_in_g": row(ln_in_g), "ln_in_b": row(ln_in_b), "w_in_p": w_in_p,
        "rw": {"mu_r": mu_r, "mu_k": mu_k, "mu_v": mu_v, "mu_l": mu_l,
               "w0": row(rw_w0[l]), "a0": row(rw_a0[l]), "k_k": row(rw_k_k[l]), "k_a": row(rw_k_a[l]),
               "r_k": row(rw_r_k[l]), "lnx_g": row(rw_lnx_g[l]), "lnx_b": row(rw_lnx_b[l]),
               "w2": rw_w2[l].astype(BF16), "a2": rw_a2[l].astype(BF16), "g2": rw_g2[l].astype(BF16), "bd": bd},
        "w_out": w_out[l].astype(BF16), "ln1_g": row(ln1_g[l]), "ln1_b": row(ln1_b[l]),
        "w_up": ffn_w_up[l].astype(BF16), "conv_w": ffn_conv_w[l], "conv_b": row(ffn_conv_b[l]),
        "w_down": ffn_w_down[l].astype(BF16), "ln2_g": row(ln2_g[l]), "ln2_b": row(ln2_b[l]),
    }
    sinks = attn_sinks[l]

    bp, tp = x_prompt.shape[0], x_prompt.shape[1]
    p_tiles = {"a_tm": 512, "chunk": CHUNK, "d_tm": 256, "e_tm": 512, "f_tm": 512}
    p_tables = _rope_tables(jnp.arange(tp, dtype=jnp.int32), 1)
    zeros = lambda *s: jnp.zeros(s, F32)
    y_p, p_k, p_v, p_wkv, p_shift, p_conv = _run(
        x_prompt, p_tables, tp // p_tiles["a_tm"],
        lambda p: (_attn_prompt(p, sinks, bp, tp), None, None),
        zeros(bp, RW_HEADS, HEAD_DIM, HEAD_DIM), zeros(bp, 1, RW_COLS), zeros(bp, CONV_W - 1, 2 * D_FF),
        wts, p_tiles)

    bs, ts = x_sample.shape[0], x_sample.shape[1]
    s_tiles = {"a_tm": bs * ts, "chunk": ts, "d_tm": bs * ts, "e_tm": ts, "f_tm": bs * ts}
    s_tables = _rope_tables(PAST_LEN + jnp.arange(ts, dtype=jnp.int32), bs)
    ck = cache_k[l].reshape(bs * WINDOW, KV_WIDTH)
    cv = cache_v[l].reshape(bs * WINDOW, KV_WIDTH)
    y_s, s_k, s_v, s_wkv, s_shift, s_conv = _run(
        x_sample, s_tables, 1,
        lambda p: _attn_sample(p, sinks, ck, cv, bs, ts),
        state_wkv[l], state_shift[l], state_ffn_conv[l], wts, s_tiles)

    return (y_p, y_s, p_k, p_v, p_wkv, p_shift, p_conv, s_k, s_v, s_wkv, s_shift, s_conv)
```

```python
import functools
import math

import jax
import jax.numpy as jnp
from jax import lax
from jax.experimental import pallas as pl
from jax.experimental.pallas import tpu as pltpu

D_MODEL = 2048
HEAD_DIM = 64
ATT_HEADS = 16
ATT_KV_HEADS = 2
ATT_GROUP = ATT_HEADS // ATT_KV_HEADS
ATT_WIDTH = ATT_HEADS * HEAD_DIM
KV_WIDTH = ATT_KV_HEADS * HEAD_DIM
CHUNK = 64
WINDOW = 128
ROPE_THETA = 10000.0
ATT_SCALE = HEAD_DIM ** -0.5
RW_HEADS = 16
RW_WIDTH = RW_HEADS * HEAD_DIM
DECAY_LORA = 64
AAA_LORA = 64
GATE_LORA = 160
LNX_EPS = 64e-5
ATT_COLS = ATT_WIDTH + 2 * KV_WIDTH
RW_COLS = 3 * RW_WIDTH + DECAY_LORA + AAA_LORA + GATE_LORA
D_FF = 5632
CONV_W = 3
LN_EPS = 1e-5
DEPTH = 1
ALPHA = (2 * DEPTH) ** 0.25
PAST_LEN = 1024

LORA_W = 512
P_R, P_K, P_V, P_Q = 0, 1024, 2048, 3072
P_LORA = 4096
P_AK = P_LORA + LORA_W
P_AV = P_AK + KV_WIDTH
P_COLS = P_AV + KV_WIDTH
LANES = 128
SUBLANES = 8
VMEM_LIMIT = 56 * 1024 * 1024

F32 = jnp.float32
BF16 = jnp.bfloat16


def _cparams(sem):
    return pltpu.CompilerParams(dimension_semantics=sem, vmem_limit_bytes=VMEM_LIMIT)


def _layer_norm(y, g, b):
    mu = jnp.mean(y, -1, keepdims=True)
    d = y - mu
    var = jnp.mean(d * d, -1, keepdims=True)
    return d * lax.rsqrt(var + LN_EPS) * g + b


A_TN = 512


def _rope128(x, cos, sin_signed):
    lane = lax.broadcasted_iota(jnp.int32, x.shape, 1)
    first_half = (lane % HEAD_DIM) < (HEAD_DIM // 2)
    rot = jnp.where(first_half, pltpu.roll(x, LANES - HEAD_DIM // 2, 1), pltpu.roll(x, HEAD_DIM // 2, 1))
    return x * cos + rot * sin_signed


def _is_rope_col(col):
    return (P_Q <= col < P_Q + ATT_WIDTH) or (P_AK <= col < P_AK + KV_WIDTH)


def _in_proj_kernel(x_ref, g_ref, b_ref, w_ref, cos_ref, sin_ref, p_ref, h_ref, hb_ref):
    h = _layer_norm(x_ref[...], g_ref[...], b_ref[...])
    h_ref[...] = h
    hb_ref[...] = h.astype(BF16)
    for c0 in range(0, P_COLS, A_TN):
        width = min(A_TN, P_COLS - c0)
        acc = jnp.dot(hb_ref[...], w_ref[:, c0:c0 + width], preferred_element_type=F32)
        for l0 in range(0, width, LANES):
            col = c0 + l0
            part = acc[:, l0:l0 + LANES]
            if _is_rope_col(col):
                part = _rope128(part, cos_ref[...], sin_ref[...])
            p_ref[:, col:col + LANES] = part


def _in_proj(x, g, b, w_p, cos, sin, tm, table_blocks):
    n = x.shape[0]
    return pl.pallas_call(
        _in_proj_kernel,
        grid=(n // tm,),
        in_specs=[
            pl.BlockSpec((tm, D_MODEL), lambda i: (i, 0)),
            pl.BlockSpec((1, D_MODEL), lambda i: (0, 0)),
            pl.BlockSpec((1, D_MODEL), lambda i: (0, 0)),
            pl.BlockSpec((D_MODEL, P_COLS), lambda i: (0, 0), pipeline_mode=pl.Buffered(1)),
            pl.BlockSpec((tm, LANES), lambda i: (i % table_blocks, 0)),
            pl.BlockSpec((tm, LANES), lambda i: (i % table_blocks, 0)),
        ],
        out_specs=[
            pl.BlockSpec((tm, P_COLS), lambda i: (i, 0)),
            pl.BlockSpec((tm, D_MODEL), lambda i: (i, 0)),
        ],
        out_shape=[
            jax.ShapeDtypeStruct((n, P_COLS), F32),
            jax.ShapeDtypeStruct((n, D_MODEL), F32),
        ],
        scratch_shapes=[pltpu.VMEM((tm, D_MODEL), BF16)],
        compiler_params=_cparams(("arbitrary",)),
        name="in_proj",
    )(x, g, b, w_p, cos, sin)


def _attend(q, k, v, valid, sink_ref):
    r = q.shape[0]
    outs = []
    for kv in range(ATT_KV_HEADS):
        kh = k[:, kv * HEAD_DIM:(kv + 1) * HEAD_DIM].astype(BF16)
        vh = v[:, kv * HEAD_DIM:(kv + 1) * HEAD_DIM].astype(BF16)
        heads = [kv * ATT_GROUP + g for g in range(ATT_GROUP)]
        qs = jnp.concatenate([q[:, h * HEAD_DIM:(h + 1) * HEAD_DIM] for h in heads], axis=0)
        sink = jnp.concatenate([jnp.full((r, 1), sink_ref[h], F32) for h in heads], axis=0)
        s = lax.dot_general(qs.astype(BF16), kh, (((1,), (1,)), ((), ())), preferred_element_type=F32) * ATT_SCALE
        if valid is not None:
            s = jnp.where(valid, s, -jnp.inf)
        m = jnp.maximum(jnp.max(s, -1, keepdims=True), sink)
        p = jnp.exp(s - m)
        den = jnp.sum(p, -1, keepdims=True) + jnp.exp(sink - m)
        o = jnp.dot(p.astype(BF16), vh, preferred_element_type=F32) / den
        outs.extend(o[g * r:(g + 1) * r] for g in range(ATT_GROUP))
    return jnp.concatenate(outs, axis=1)


QB = 2 * CHUNK


def _attn_prompt_kernel(sink_ref, q_ref, kp_ref, kc_ref, vp_ref, vc_ref, o_ref):
    m = pl.program_id(1)
    span = 2 * QB
    k_all = jnp.concatenate([kp_ref[...], kc_ref[...]], axis=0)
    v_all = jnp.concatenate([vp_ref[...], vc_ref[...]], axis=0)
    lane = lax.broadcasted_iota(jnp.int32, (span, KV_WIDTH), 1)
    row_c = 2 + lax.broadcasted_iota(jnp.int32, (QB, span), 0) // CHUNK
    col_c = lax.broadcasted_iota(jnp.int32, (QB, span), 1) // CHUNK
    back = row_c - col_c
    valid = jnp.logical_and(jnp.logical_and(back >= 0, back <= 2), 2 * m - 2 + col_c >= 0)
    out_lane = lax.broadcasted_iota(jnp.int32, (QB, PAIR), 1)

    def dup(x, kv):
        sw = pltpu.roll(x, HEAD_DIM, 1)
        return jnp.where(lane < HEAD_DIM, x, sw) if kv == 0 else jnp.where(lane < HEAD_DIM, sw, x)

    k_st = [_stack_heads(dup(k_all, kv), HEAD_DIM).astype(BF16) for kv in range(ATT_KV_HEADS)]
    v_st = [_stack_heads(dup(v_all, kv), HEAD_DIM).astype(BF16) for kv in range(ATT_KV_HEADS)]
    pairs = range(ATT_HEADS // 2)
    kv_of = lambda j: (2 * j) // ATT_GROUP
    s = [lax.dot_general(q_ref[:, j * PAIR:(j + 1) * PAIR].astype(BF16), k_st[kv_of(j)],
                         (((1,), (1,)), ((), ())), preferred_element_type=F32) * ATT_SCALE for j in pairs]
    probs, dens = [], []
    for j in pairs:
        halves, den = [], []
        for e in range(2):
            sink = sink_ref[2 * j + e]
            sh = jnp.where(valid, s[j][:, e * span:(e + 1) * span], -jnp.inf)
            mx = jnp.maximum(jnp.max(sh, -1, keepdims=True), sink)
            ph = jnp.exp(sh - mx)
            halves.append(ph)
            den.append(jnp.sum(ph, -1, keepdims=True) + jnp.exp(sink - mx))
        probs.append(jnp.concatenate(halves, axis=1).astype(BF16))
        dens.append(jnp.where(out_lane < HEAD_DIM, den[0], den[1]))
    o = [jnp.dot(probs[j], v_st[kv_of(j)], preferred_element_type=F32) / dens[j] for j in pairs]
    o_ref[...] = jnp.concatenate(o, axis=1).astype(o_ref.dtype)


def _attn_prompt(p, sinks, batch, seq):
    nb = seq // QB
    kcol, vcol = P_AK // KV_WIDTH, P_AV // KV_WIDTH
    prev = lambda col: pl.BlockSpec((QB, KV_WIDTH), lambda b, m: (b * nb + jnp.maximum(m - 1, 0), col))
    cur = lambda col: pl.BlockSpec((QB, KV_WIDTH), lambda b, m: (b * nb + m, col))
    return pl.pallas_call(
        _attn_prompt_kernel,
        grid=(batch, nb),
        in_specs=[
            pl.BlockSpec(memory_space=pltpu.SMEM),
            pl.BlockSpec((QB, ATT_WIDTH), lambda b, m: (b * nb + m, P_Q // ATT_WIDTH)),
            prev(kcol), cur(kcol), prev(vcol), cur(vcol),
        ],
        out_specs=pl.BlockSpec((QB, ATT_WIDTH), lambda b, m: (b * nb + m, 0)),
        out_shape=jax.ShapeDtypeStruct((batch * seq, ATT_WIDTH), BF16),
        compiler_params=_cparams(("arbitrary", "arbitrary")),
        name="attn_prompt",
    )(sinks, p, p, p, p, p)


def _attn_sample_kernel(sink_ref, q_ref, kn_ref, vn_ref, kc_ref, vc_ref, o_ref, kw_ref, vw_ref):
    t = q_ref.shape[0]
    k = jnp.concatenate([kc_ref[...], kn_ref[...]], axis=0)
    v = jnp.concatenate([vc_ref[...], vn_ref[...]], axis=0)
    o_ref[...] = _attend(q_ref[...], k, v, None, sink_ref).astype(o_ref.dtype)
    kw_ref[...] = k[t:]
    vw_ref[...] = v[t:]


def _attn_sample(p, sinks, cache_k, cache_v, batch, seq):
    kcol, vcol = P_AK // KV_WIDTH, P_AV // KV_WIDTH
    return pl.pallas_call(
        _attn_sample_kernel,
        grid=(batch,),
        in_specs=[
            pl.BlockSpec(memory_space=pltpu.SMEM),
            pl.BlockSpec((seq, ATT_WIDTH), lambda b: (b, P_Q // ATT_WIDTH)),
            pl.BlockSpec((seq, KV_WIDTH), lambda b: (b, kcol)),
            pl.BlockSpec((seq, KV_WIDTH), lambda b: (b, vcol)),
            pl.BlockSpec((WINDOW, KV_WIDTH), lambda b: (b, 0)),
            pl.BlockSpec((WINDOW, KV_WIDTH), lambda b: (b, 0)),
        ],
        out_specs=[
            pl.BlockSpec((seq, ATT_WIDTH), lambda b: (b, 0)),
            pl.BlockSpec((WINDOW, KV_WIDTH), lambda b: (b, 0)),
            pl.BlockSpec((WINDOW, KV_WIDTH), lambda b: (b, 0)),
        ],
        out_shape=[
            jax.ShapeDtypeStruct((batch * seq, ATT_WIDTH), BF16),
            jax.ShapeDtypeStruct((batch * WINDOW, KV_WIDTH), F32),
            jax.ShapeDtypeStruct((batch * WINDOW, KV_WIDTH), F32),
        ],
        compiler_params=_cparams(("arbitrary",)),
        name="attn_sample",
    )(sinks, p, p, p, cache_k, cache_v)


SEG = 256
PAIR = 2 * HEAD_DIM
N_PAIRS = RW_HEADS // 2


def _split2(x):
    hi = x.astype(BF16)
    lo = (x - hi.astype(F32)).astype(BF16)
    return hi, lo


def _seg_sum(x, bd):
    c = x.shape[0]
    slabs = range(0, RW_WIDTH, SEG)
    stack = jnp.concatenate([part[:, s:s + SEG] for part in _split2(x) for s in slabs], axis=0)
    r = jnp.dot(stack, bd, preferred_element_type=F32)
    n = len(slabs)
    return jnp.concatenate([r[i * c:(i + 1) * c] + r[(n + i) * c:(n + i + 1) * c] for i in range(n)], axis=1)


def _shift_rows(x, prev_row):
    row = lax.broadcasted_iota(jnp.int32, x.shape, 0)
    return jnp.where(row == 0, prev_row, pltpu.roll(x, 1, 0))


def _dotb(a, b):
    return jnp.dot(a.astype(BF16), b.astype(BF16), preferred_element_type=F32)


def _dotb_nt(a, b):
    return lax.dot_general(a.astype(BF16), b.astype(BF16), (((1,), (1,)), ((), ())), preferred_element_type=F32)


def _dot3_tn(a, b):
    ah, al = _split2(a)
    bh, bl = _split2(b)
    return lax.dot_general(jnp.concatenate([ah, al, ah], axis=0), jnp.concatenate([bh, bh, bl], axis=0),
                           (((0,), (0,)), ((), ())), preferred_element_type=F32)


def _dot3(a, b):
    ah, al = _split2(a)
    bh, bl = _split2(b)
    return jnp.dot(jnp.concatenate([ah, al, ah], axis=1), jnp.concatenate([bh, bh, bl], axis=0),
                   preferred_element_type=F32)


def _stack_heads(x, half):
    lane = lax.broadcasted_iota(jnp.int32, x.shape, 1)
    first = (lane % (2 * half)) < half
    return jnp.concatenate([jnp.where(first, x, 0.0), jnp.where(first, 0.0, x)], axis=0)


def _rwkv_pairs(ab, rb, bb, kb, v, gam_c, s_bd, n_double):
    c = ab[0].shape[0]
    pairs = range(len(ab))
    t_idx = lax.broadcasted_iota(jnp.int32, (c, 2 * c), 0)
    s_idx = lax.broadcasted_iota(jnp.int32, (c, 2 * c), 1) % c
    strict, incl = s_idx < t_idx, s_idx <= t_idx
    g = [_dotb_nt(jnp.concatenate([ab[p], rb[p]], axis=0),
                  jnp.concatenate([_stack_heads(bb[p], HEAD_DIM), _stack_heads(kb[p], HEAD_DIM)], axis=0))
         for p in pairs]
    l_ak = [jnp.where(strict, g[p][:c, 2 * c:], 0.0) for p in pairs]
    m_rbk = [jnp.concatenate([jnp.where(incl, g[p][c:, :2 * c], 0.0), jnp.where(incl, g[p][c:, 2 * c:], 0.0)], axis=1)
             for p in pairs]
    lp = [jnp.where(strict, g[p][:c, :2 * c], 0.0) for p in pairs]
    v_st = [_stack_heads(v[p], HEAD_DIM) for p in pairs]
    ya = list(ab)
    yu = [_dotb(l_ak[p], v_st[p]) for p in pairs]
    for i in range(n_double):
        y_st = [_stack_heads(jnp.concatenate([ya[p], yu[p]], axis=1), HEAD_DIM) for p in pairs]
        if i + 1 < n_double:
            z = [_dotb(lp[p], jnp.concatenate([y_st[p], _stack_heads(lp[p], c)], axis=1)) for p in pairs]
            lp = [z[p][:, 2 * PAIR:] for p in pairs]
        else:
            z = [_dotb(lp[p], y_st[p]) for p in pairs]
        ya = [ya[p] + z[p][:, :PAIR] for p in pairs]
        yu = [yu[p] + z[p][:, PAIR:2 * PAIR] for p in pairs]
    z = []
    for p in pairs:
        top = jnp.concatenate([_stack_heads(ya[p], HEAD_DIM), _stack_heads(yu[p], HEAD_DIM)], axis=1)
        bot = jnp.concatenate([jnp.zeros_like(v_st[p]), v_st[p]], axis=1)
        z.append(_dotb(m_rbk[p], jnp.concatenate([top, bot], axis=0)))
    o = [_dotb_nt(rb[p] + z[p][:, :PAIR], s_bd[p]) + z[p][:, PAIR:] for p in pairs]
    row_h = lax.broadcasted_iota(jnp.int32, (PAIR, PAIR), 0) // HEAD_DIM
    col_h = lax.broadcasted_iota(jnp.int32, (PAIR, PAIR), 1) // HEAD_DIM
    same = row_h == col_h
    q_t = [jnp.where(same, _dot3_tn(ya[p], bb[p]), 0.0) for p in pairs]
    d_t = [jnp.where(same, _dot3_tn(jnp.concatenate([yu[p], v[p]], axis=0),
                                    jnp.concatenate([bb[p], kb[p]], axis=0)), 0.0) for p in pairs]
    s_new = [(s_bd[p] + _dot3(s_bd[p], q_t[p]) + d_t[p]) * gam_c[p] for p in pairs]
    return o, s_new


def _rwkv_kernel(xr_ref, xk_ref, xv_ref, xl_ref, sr_ref, sk_ref, sv_ref, sl_ref,
                 mur_ref, muk_ref, muv_ref, mul_ref, w0_ref, a0_ref, kk_ref, ka_ref, rk_ref,
                 lg_ref, lb_ref, w2_ref, a2_ref, g2_ref, bd_ref, st0_ref,
                 out_ref, st_ref,
                 s_scr, pr_scr, pk_scr, pv_scr, pl_scr):
    c = pl.program_id(1)
    nc = pl.num_programs(1)
    C = xr_ref.shape[0]
    n_double = int(math.log2(C))

    @pl.when(c == 0)
    def _():
        s_scr[...] = jnp.zeros_like(s_scr)
        for h in range(RW_HEADS):
            off = (h % 2) * HEAD_DIM
            s_scr[h // 2, off:off + HEAD_DIM, off:off + HEAD_DIM] = st0_ref[0, h]
        pr_scr[0:1, :] = sr_ref[0]
        pk_scr[0:1, :] = sk_ref[0]
        pv_scr[0:1, :] = sv_ref[0]
        pl_scr[0:1, :] = sl_ref[0]

    def shifted(x_ref, p_scr, mu_ref):
        x = x_ref[...]
        prev = _shift_rows(x, p_scr[0:1, :])
        p_scr[0:1, :] = x[C - 1:C, :]
        return x + (prev - x) * mu_ref[...]

    r = shifted(xr_ref, pr_scr, mur_ref)
    kr = shifted(xk_ref, pk_scr, muk_ref)
    v = shifted(xv_ref, pv_scr, muv_ref)
    xl = shifted(xl_ref, pl_scr, mul_ref)
    wd = xl[:, 0:DECAY_LORA]
    ad = xl[:, DECAY_LORA:DECAY_LORA + AAA_LORA]
    gd = xl[:, DECAY_LORA + AAA_LORA:DECAY_LORA + AAA_LORA + GATE_LORA]

    w_log = -jax.nn.softplus(-(w0_ref[...] + _dotb(jnp.tanh(wd), w2_ref[...]))) - 0.5
    lw = -jnp.exp(w_log)
    a = jax.nn.sigmoid(a0_ref[...] + _dotb(ad, a2_ref[...]))
    g = _dotb(jax.nn.sigmoid(gd), g2_ref[...])
    bd = bd_ref[...]
    kk = kr * kk_ref[...]
    kk = kk / jnp.maximum(jnp.sqrt(_seg_sum(kk * kk, bd)), 1e-12)
    k2 = kr * (1.0 + (a - 1.0) * ka_ref[...])

    ti = lax.broadcasted_iota(jnp.int32, (C, C), 0)
    si = lax.broadcasted_iota(jnp.int32, (C, C), 1)
    tri = (si <= ti).astype(BF16)
    lw_hi = lw.astype(BF16)
    lw_mid, lw_lo = _split2(lw - lw_hi.astype(F32))
    cs3 = jnp.dot(tri, jnp.concatenate([lw_hi, lw_mid, lw_lo], axis=1), preferred_element_type=F32)
    cs = cs3[:, :RW_WIDTH] + cs3[:, RW_WIDTH:2 * RW_WIDTH] + cs3[:, 2 * RW_WIDTH:]
    gam = jnp.exp(cs)
    ginv = jnp.exp(-cs)
    ab = -kk * jnp.exp(cs - lw)
    rb = r * gam
    bb = kk * a * ginv
    kb = k2 * ginv
    gam_c = gam[C - 1:C, :]

    per_pair = lambda x: [x[:, p * PAIR:(p + 1) * PAIR] for p in range(N_PAIRS)]
    outs, s_new = _rwkv_pairs(per_pair(ab), per_pair(rb), per_pair(bb), per_pair(kb), per_pair(v), per_pair(gam_c),
                              [s_scr[p] for p in range(N_PAIRS)], n_double)
    for p in range(N_PAIRS):
        s_scr[p] = s_new[p]
    o = jnp.concatenate(outs, axis=1)

    mo = _seg_sum(o, bd) * (1.0 / HEAD_DIM)
    d = o - mo
    vo = _seg_sum(d * d, bd) * (1.0 / HEAD_DIM)
    on = d * lax.rsqrt(vo + LNX_EPS) * lg_ref[...] + lb_ref[...]
    bonus = _seg_sum(r * k2 * rk_ref[...], bd) * v
    out_ref[...] = ((on + bonus) * g).astype(out_ref.dtype)

    @pl.when(c == nc - 1)
    def _():
        for h in range(RW_HEADS):
            off = (h % 2) * HEAD_DIM
            st_ref[0, h] = s_scr[h // 2, off:off + HEAD_DIM, off:off + HEAD_DIM]


def _rwkv(p, shift_p, state0, prm, batch, seq, chunk):
    nc = seq // chunk
    row = lambda w: pl.BlockSpec((1, w), lambda b, c: (0, 0))
    xspec = lambda w, col: pl.BlockSpec((chunk, w), lambda b, c: (b * nc + c, col))
    sspec = lambda w: pl.BlockSpec((1, 1, w), lambda b, c: (b, 0, 0))
    full = lambda a: pl.BlockSpec(a.shape, lambda b, c: (0,) * a.ndim)
    sr, sk, sv, sl = shift_p
    return pl.pallas_call(
        _rwkv_kernel,
        grid=(batch, nc),
        in_specs=[
            xspec(RW_WIDTH, P_R // RW_WIDTH), xspec(RW_WIDTH, P_K // RW_WIDTH), xspec(RW_WIDTH, P_V // RW_WIDTH),
            xspec(LORA_W, P_LORA // LORA_W),
            sspec(RW_WIDTH), sspec(RW_WIDTH), sspec(RW_WIDTH), sspec(LORA_W),
            row(RW_WIDTH), row(RW_WIDTH), row(RW_WIDTH), row(LORA_W),
            row(RW_WIDTH), row(RW_WIDTH), row(RW_WIDTH), row(RW_WIDTH), row(RW_WIDTH),
            row(RW_WIDTH), row(RW_WIDTH),
            full(prm["w2"]), full(prm["a2"]), full(prm["g2"]), full(prm["bd"]),
            pl.BlockSpec((1, RW_HEADS, HEAD_DIM, HEAD_DIM), lambda b, c: (b, 0, 0, 0)),
        ],
        out_specs=[
            pl.BlockSpec((chunk, RW_WIDTH), lambda b, c: (b * nc + c, 0)),
            pl.BlockSpec((1, RW_HEADS, HEAD_DIM, HEAD_DIM), lambda b, c: (b, 0, 0, 0)),
        ],
        out_shape=[
            jax.ShapeDtypeStruct((batch * seq, RW_WIDTH), BF16),
            jax.ShapeDtypeStruct((batch, RW_HEADS, HEAD_DIM, HEAD_DIM), F32),
        ],
        scratch_shapes=[
            pltpu.VMEM((N_PAIRS, PAIR, PAIR), F32),
            pltpu.VMEM((SUBLANES, RW_WIDTH), F32), pltpu.VMEM((SUBLANES, RW_WIDTH), F32),
            pltpu.VMEM((SUBLANES, RW_WIDTH), F32), pltpu.VMEM((SUBLANES, LORA_W), F32),
        ],
        compiler_params=_cparams(("arbitrary", "arbitrary")),
        name="rwkv",
    )(p, p, p, p, sr, sk, sv, sl,
      prm["mu_r"], prm["mu_k"], prm["mu_v"], prm["mu_l"], prm["w0"], prm["a0"], prm["k_k"], prm["k_a"], prm["r_k"],
      prm["lnx_g"], prm["lnx_b"], prm["w2"], prm["a2"], prm["g2"], prm["bd"], state0)


def _out_proj_kernel(att_ref, rw_ref, wa_ref, wb_ref, h_ref, g_ref, b_ref, h1_ref, h1b_ref):
    m = jnp.dot(att_ref[...], wa_ref[...], preferred_element_type=F32)
    m = m + jnp.dot(rw_ref[...], wb_ref[...], preferred_element_type=F32)
    y = _layer_norm(ALPHA * h_ref[...] + m, g_ref[...], b_ref[...])
    h1_ref[...] = y
    h1b_ref[...] = y.astype(BF16)


def _out_proj(att, rw, w_out, h, g, b, tm):
    n = att.shape[0]
    return pl.pallas_call(
        _out_proj_kernel,
        grid=(n // tm,),
        in_specs=[
            pl.BlockSpec((tm, ATT_WIDTH), lambda i: (i, 0)),
            pl.BlockSpec((tm, RW_WIDTH), lambda i: (i, 0)),
            pl.BlockSpec((ATT_WIDTH, D_MODEL), lambda i: (0, 0)),
            pl.BlockSpec((RW_WIDTH, D_MODEL), lambda i: (1, 0)),
            pl.BlockSpec((tm, D_MODEL), lambda i: (i, 0)),
            pl.BlockSpec((1, D_MODEL), lambda i: (0, 0)),
            pl.BlockSpec((1, D_MODEL), lambda i: (0, 0)),
        ],
        out_specs=[
            pl.BlockSpec((tm, D_MODEL), lambda i: (i, 0)),
            pl.BlockSpec((tm, D_MODEL), lambda i: (i, 0)),
        ],
        out_shape=[
            jax.ShapeDtypeStruct((n, D_MODEL), F32),
            jax.ShapeDtypeStruct((n, D_MODEL), BF16),
        ],
        compiler_params=_cparams(("arbitrary",)),
        name="out_proj",
    )(att, rw, w_out, w_out, h, g, b)


E_TN = 512
E_NCB = D_FF // E_TN
SQRT_HALF = math.sqrt(0.5)


def _ffn_up_kernel(bps, h_ref, wg_ref, wv_ref, cwg_ref, cwv_ref, cbg_ref, cbv_ref, sg_ref, sv_ref,
                   act_ref, og_ref, ov_ref, cg_scr, cv_scr):
    i = pl.program_id(1)
    first = (i % bps) == 0
    h = h_ref[...]
    tm = h.shape[0]
    row = lax.broadcasted_iota(jnp.int32, (tm, E_TN), 0)

    def conv(w_ref, cw_ref, cb_ref, st_ref, c_scr, o_ref):
        u = jnp.dot(h, w_ref[...], preferred_element_type=F32)
        prev = jnp.where(first, st_ref[0], c_scr[SUBLANES - 2:SUBLANES, :])
        p2, p1 = prev[0:1, :], prev[1:2, :]
        sh1 = jnp.where(row == 0, p1, pltpu.roll(u, 1, 0))
        sh2 = jnp.where(row == 0, p2, jnp.where(row == 1, p1, pltpu.roll(u, 2, 0)))
        cw = cw_ref[...]
        c = cb_ref[...] + sh2 * cw[0:1, :]
        c = c + sh1 * cw[1:2, :]
        c = c + u * cw[2:3, :]
        tail = u[tm - SUBLANES:tm, :]
        c_scr[...] = tail
        o_ref[0] = tail
        return c

    gate = conv(wg_ref, cwg_ref, cbg_ref, sg_ref, cg_scr, og_ref)
    val = conv(wv_ref, cwv_ref, cbv_ref, sv_ref, cv_scr, ov_ref)
    gelu = 0.5 * gate * (1.0 + lax.erf(gate * SQRT_HALF))
    act_ref[...] = (gelu * val).astype(act_ref.dtype)


def _ffn_up(h1b, w_up, conv_w, conv_b, conv_prev, batch, seq, tm):
    n = h1b.shape[0]
    bps = seq // tm
    nrb = n // tm
    return pl.pallas_call(
        functools.partial(_ffn_up_kernel, bps),
        grid=(E_NCB, nrb),
        in_specs=[
            pl.BlockSpec((tm, D_MODEL), lambda j, i: (i, 0)),
            pl.BlockSpec((D_MODEL, E_TN), lambda j, i: (0, j)),
            pl.BlockSpec((D_MODEL, E_TN), lambda j, i: (0, j + E_NCB)),
            pl.BlockSpec((CONV_W, E_TN), lambda j, i: (0, j)),
            pl.BlockSpec((CONV_W, E_TN), lambda j, i: (0, j + E_NCB)),
            pl.BlockSpec((1, E_TN), lambda j, i: (0, j)),
            pl.BlockSpec((1, E_TN), lambda j, i: (0, j + E_NCB)),
            pl.BlockSpec((1, CONV_W - 1, E_TN), lambda j, i: (i // bps, 0, j)),
            pl.BlockSpec((1, CONV_W - 1, E_TN), lambda j, i: (i // bps, 0, j + E_NCB)),
        ],
        out_specs=[
            pl.BlockSpec((tm, E_TN), lambda j, i: (i, j)),
            pl.BlockSpec((1, SUBLANES, E_TN), lambda j, i: (i // bps, 0, j)),
            pl.BlockSpec((1, SUBLANES, E_TN), lambda j, i: (i // bps, 0, j)),
        ],
        out_shape=[
            jax.ShapeDtypeStruct((n, D_FF), BF16),
            jax.ShapeDtypeStruct((batch, SUBLANES, D_FF), F32),
            jax.ShapeDtypeStruct((batch, SUBLANES, D_FF), F32),
        ],
        scratch_shapes=[pltpu.VMEM((SUBLANES, E_TN), F32), pltpu.VMEM((SUBLANES, E_TN), F32)],
        compiler_params=_cparams(("arbitrary", "arbitrary")),
        name="ffn_up",
    )(h1b, w_up, w_up, conv_w, conv_w, conv_b, conv_b, conv_prev, conv_prev)


F_TN = 512


def _ffn_down_kernel(act_ref, w_ref, h1_ref, g_ref, b_ref, y_ref):
    for c0 in range(0, D_MODEL, F_TN):
        f = jnp.dot(act_ref[...], w_ref[:, c0:c0 + F_TN], preferred_element_type=F32)
        y_ref[:, c0:c0 + F_TN] = ALPHA * h1_ref[:, c0:c0 + F_TN] + f
    y_ref[...] = _layer_norm(y_ref[...], g_ref[...], b_ref[...])


def _ffn_down(act, w_down, h1, g, b, tm):
    n = act.shape[0]
    return pl.pallas_call(
        _ffn_down_kernel,
        grid=(n // tm,),
        in_specs=[
            pl.BlockSpec((tm, D_FF), lambda i: (i, 0)),
            pl.BlockSpec((D_FF, D_MODEL), lambda i: (0, 0), pipeline_mode=pl.Buffered(1)),
            pl.BlockSpec((tm, D_MODEL), lambda i: (i, 0)),
            pl.BlockSpec((1, D_MODEL), lambda i: (0, 0)),
            pl.BlockSpec((1, D_MODEL), lambda i: (0, 0)),
        ],
        out_specs=pl.BlockSpec((tm, D_MODEL), lambda i: (i, 0)),
        out_shape=jax.ShapeDtypeStruct((n, D_MODEL), F32),
        compiler_params=_cparams(("arbitrary",)),
        name="ffn_down",
    )(act, w_down, h1, g, b)


def _rope_tables(pos, reps):
    half = HEAD_DIM // 2
    inv = ROPE_THETA ** (-jnp.arange(half, dtype=F32) / half)
    ang = pos.astype(F32)[:, None] * inv[None, :]
    cos, sin = jnp.cos(ang), jnp.sin(ang)
    cos128 = jnp.concatenate([cos, cos, cos, cos], axis=1)
    sin128 = jnp.concatenate([-sin, sin, -sin, sin], axis=1)
    return jnp.tile(cos128, (reps, 1)), jnp.tile(sin128, (reps, 1))


def _permute_rw(a):
    o_wd, o_k, o_v = RW_WIDTH, RW_WIDTH + DECAY_LORA, 2 * RW_WIDTH + DECAY_LORA
    o_ad = 3 * RW_WIDTH + DECAY_LORA
    o_gd = o_ad + AAA_LORA
    pad = jnp.zeros(a.shape[:-1] + (LORA_W - DECAY_LORA - AAA_LORA - GATE_LORA,), a.dtype)
    return jnp.concatenate([a[..., :RW_WIDTH], a[..., o_k:o_k + RW_WIDTH], a[..., o_v:o_v + RW_WIDTH],
                            a[..., o_wd:o_wd + DECAY_LORA], a[..., o_ad:o_ad + AAA_LORA],
                            a[..., o_gd:o_gd + GATE_LORA], pad], axis=-1)


def _split_rw(a):
    return (a[..., :RW_WIDTH], a[..., RW_WIDTH:2 * RW_WIDTH], a[..., 2 * RW_WIDTH:3 * RW_WIDTH],
            a[..., 3 * RW_WIDTH:])


def _unpermute_shift(p_row):
    lo = P_LORA
    return jnp.concatenate([
        p_row[..., P_R:P_R + RW_WIDTH], p_row[..., lo:lo + DECAY_LORA],
        p_row[..., P_K:P_K + RW_WIDTH], p_row[..., P_V:P_V + RW_WIDTH],
        p_row[..., lo + DECAY_LORA:lo + DECAY_LORA + AAA_LORA],
        p_row[..., lo + DECAY_LORA + AAA_LORA:lo + DECAY_LORA + AAA_LORA + GATE_LORA]], axis=-1)


def _run(x, pos_tables, table_blocks, attn_fn, state_wkv, state_shift, state_conv, wts, tiles):
    batch, seq = x.shape[0], x.shape[1]
    n = batch * seq
    cos, sin = pos_tables
    p, h = _in_proj(x.reshape(n, D_MODEL), wts["ln_in_g"], wts["ln_in_b"], wts["w_in_p"], cos, sin,
                    tiles["a_tm"], table_blocks)
    att, k_win, v_win = attn_fn(p)
    shift_p = tuple(s for s in _split_rw(_permute_rw(state_shift)))
    rw, wkv = _rwkv(p, shift_p, state_wkv, wts["rw"], batch, seq, tiles["chunk"])
    h1, h1b = _out_proj(att, rw, wts["w_out"], h, wts["ln1_g"], wts["ln1_b"], tiles["d_tm"])
    act, cg, cv = _ffn_up(h1b, wts["w_up"], wts["conv_w"], wts["conv_b"], state_conv, batch, seq, tiles["e_tm"])
    y = _ffn_down(act, wts["w_down"], h1, wts["ln2_g"], wts["ln2_b"], tiles["f_tm"])
    p3 = p.reshape(batch, seq, P_COLS)
    shift_out = _unpermute_shift(p3[:, seq - 1:seq, :])
    conv_out = jnp.concatenate([cg[:, SUBLANES - 2:, :], cv[:, SUBLANES - 2:, :]], axis=-1)
    if k_win is None:
        k_win = p3[:, seq - WINDOW:, P_AK:P_AK + KV_WIDTH]
        v_win = p3[:, seq - WINDOW:, P_AV:P_AV + KV_WIDTH]
    kv_shape = (1, batch, WINDOW, ATT_KV_HEADS, HEAD_DIM)
    return (y.reshape(batch, seq, D_MODEL), k_win.reshape(kv_shape), v_win.reshape(kv_shape),
            wkv[None], shift_out[None], conv_out[None])


def kernel(x_prompt, x_sample, cache_k, cache_v, state_wkv, state_shift, state_ffn_conv, ln_in_g, ln_in_b, w_in, attn_sinks, rw_mu, rw_w0, rw_w2, rw_a0, rw_a2, rw_g2, rw_k_k, rw_k_a, rw_r_k, rw_lnx_g, rw_lnx_b, w_out, ln1_g, ln1_b, ffn_w_up, ffn_conv_w, ffn_conv_b, ffn_w_down, ln2_g, ln2_b):
    l = 0
    w = w_in[l]
    w_att, w_rw = w[:, :ATT_COLS], _permute_rw(w[:, ATT_COLS:])
    w_in_p = jnp.concatenate([w_rw[:, :3 * RW_WIDTH], w_att[:, :ATT_WIDTH], w_rw[:, 3 * RW_WIDTH:],
                              w_att[:, ATT_WIDTH:]], axis=1).astype(BF16)
    mu_r, mu_k, mu_v, mu_l = _split_rw(_permute_rw(rw_mu[l])[None])
    seg = lax.broadcasted_iota(jnp.int32, (SEG, SEG), 0) // HEAD_DIM
    bd = (seg == seg.T).astype(BF16)
    row = lambda a: a.reshape(1, -1)
    wts = {
        "ln_in_g": row(ln_in_g), "ln_in_b": row(ln_in_b), "w_in_p": w_in_p,
        "rw": {"mu_r": mu_r, "mu_k": mu_k, "mu_v": mu_v, "mu_l": mu_l,
               "w0": row(rw_w0[l]), "a0": row(rw_a0[l]), "k_k": row(rw_k_k[l]), "k_a": row(rw_k_a[l]),
               "r_k": row(rw_r_k[l]), "lnx_g": row(rw_lnx_g[l]), "lnx_b": row(rw_lnx_b[l]),
               "w2": rw_w2[l].astype(BF16), "a2": rw_a2[l].astype(BF16), "g2": rw_g2[l].astype(BF16), "bd": bd},
        "w_out": w_out[l].astype(BF16), "ln1_g": row(ln1_g[l]), "ln1_b": row(ln1_b[l]),
        "w_up": ffn_w_up[l].astype(BF16), "conv_w": ffn_conv_w[l], "conv_b": row(ffn_conv_b[l]),
        "w_down": ffn_w_down[l].astype(BF16), "ln2_g": row(ln2_g[l]), "ln2_b": row(ln2_b[l]),
    }
    sinks = attn_sinks[l]

    bp, tp = x_prompt.shape[0], x_prompt.shape[1]
    p_tiles = {"a_tm": 256, "chunk": CHUNK, "d_tm": 256, "e_tm": 512, "f_tm": 256}
    p_tables = _rope_tables(jnp.arange(tp, dtype=jnp.int32), 1)
    zeros = lambda *s: jnp.zeros(s, F32)
    y_p, p_k, p_v, p_wkv, p_shift, p_conv = _run(
        x_prompt, p_tables, tp // p_tiles["a_tm"],
        lambda p: (_attn_prompt(p, sinks, bp, tp), None, None),
        zeros(bp, RW_HEADS, HEAD_DIM, HEAD_DIM), zeros(bp, 1, RW_COLS), zeros(bp, CONV_W - 1, 2 * D_FF),
        wts, p_tiles)

    bs, ts = x_sample.shape[0], x_sample.shape[1]
    s_tiles = {"a_tm": bs * ts, "chunk": ts, "d_tm": bs * ts, "e_tm": ts, "f_tm": bs * ts}
    s_tables = _rope_tables(PAST_LEN + jnp.arange(ts, dtype=jnp.int32), bs)
    ck = cache_k[l].reshape(bs * WINDOW, KV_WIDTH)
    cv = cache_v[l].reshape(bs * WINDOW, KV_WIDTH)
    y_s, s_k, s_v, s_wkv, s_shift, s_conv = _run(
        x_sample, s_tables, 1,
        lambda p: _attn_sample(p, sinks, ck, cv, bs, ts),
        state_wkv[l], state_shift[l], state_ffn_conv[l], wts, s_tiles)

    return (y_p, y_s, p_k, p_v, p_wkv, p_shift, p_conv, s_k, s_v, s_wkv, s_shift, s_conv)
```

```python
import functools
import math

import jax
import jax.numpy as jnp
from jax import lax
from jax.experimental import pallas as pl
from jax.experimental.pallas import tpu as pltpu

D_MODEL = 2048
HEAD_DIM = 64
ATT_HEADS = 16
ATT_KV_HEADS = 2
ATT_GROUP = ATT_HEADS // ATT_KV_HEADS
ATT_WIDTH = ATT_HEADS * HEAD_DIM
KV_WIDTH = ATT_KV_HEADS * HEAD_DIM
CHUNK = 64
WINDOW = 128
ROPE_THETA = 10000.0
ATT_SCALE = HEAD_DIM ** -0.5
RW_HEADS = 16
RW_WIDTH = RW_HEADS * HEAD_DIM
DECAY_LORA = 64
AAA_LORA = 64
GATE_LORA = 160
LNX_EPS = 64e-5
ATT_COLS = ATT_WIDTH + 2 * KV_WIDTH
RW_COLS = 3 * RW_WIDTH + DECAY_LORA + AAA_LORA + GATE_LORA
D_FF = 5632
CONV_W = 3
LN_EPS = 1e-5
DEPTH = 1
ALPHA = (2 * DEPTH) ** 0.25
PAST_LEN = 1024

LORA_W = 512
P_R, P_K, P_V, P_Q = 0, 1024, 2048, 3072
P_LORA = 4096
P_AK = P_LORA + LORA_W
P_AV = P_AK + KV_WIDTH
P_COLS = P_AV + KV_WIDTH
LANES = 128
SUBLANES = 8
VMEM_LIMIT = 56 * 1024 * 1024

F32 = jnp.float32
BF16 = jnp.bfloat16


def _cparams(sem):
    return pltpu.CompilerParams(dimension_semantics=sem, vmem_limit_bytes=VMEM_LIMIT)


def _layer_norm(y, g, b):
    mu = jnp.mean(y, -1, keepdims=True)
    d = y - mu
    var = jnp.mean(d * d, -1, keepdims=True)
    return d * lax.rsqrt(var + LN_EPS) * g + b


A_TN = 512


def _rope128(x, cos, sin_signed):
    lane = lax.broadcasted_iota(jnp.int32, x.shape, 1)
    first_half = (lane % HEAD_DIM) < (HEAD_DIM // 2)
    rot = jnp.where(first_half, pltpu.roll(x, LANES - HEAD_DIM // 2, 1), pltpu.roll(x, HEAD_DIM // 2, 1))
    return x * cos + rot * sin_signed


def _is_rope_col(col):
    return (P_Q <= col < P_Q + ATT_WIDTH) or (P_AK <= col < P_AK + KV_WIDTH)


def _in_proj_kernel(x_ref, g_ref, b_ref, w_ref, cos_ref, sin_ref, p_ref, h_ref, hb_ref):
    h = _layer_norm(x_ref[...], g_ref[...], b_ref[...])
    h_ref[...] = h
    hb_ref[...] = h.astype(BF16)
    for c0 in range(0, P_COLS, A_TN):
        width = min(A_TN, P_COLS - c0)
        acc = jnp.dot(hb_ref[...], w_ref[:, c0:c0 + width], preferred_element_type=F32)
        for l0 in range(0, width, LANES):
            col = c0 + l0
            part = acc[:, l0:l0 + LANES]
            if _is_rope_col(col):
                part = _rope128(part, cos_ref[...], sin_ref[...])
            p_ref[:, col:col + LANES] = part


def _in_proj(x, g, b, w_p, cos, sin, tm, table_blocks):
    n = x.shape[0]
    return pl.pallas_call(
        _in_proj_kernel,
        grid=(n // tm,),
        in_specs=[
            pl.BlockSpec((tm, D_MODEL), lambda i: (i, 0)),
            pl.BlockSpec((1, D_MODEL), lambda i: (0, 0)),
            pl.BlockSpec((1, D_MODEL), lambda i: (0, 0)),
            pl.BlockSpec((D_MODEL, P_COLS), lambda i: (0, 0), pipeline_mode=pl.Buffered(1)),
            pl.BlockSpec((tm, LANES), lambda i: (i % table_blocks, 0)),
            pl.BlockSpec((tm, LANES), lambda i: (i % table_blocks, 0)),
        ],
        out_specs=[
            pl.BlockSpec((tm, P_COLS), lambda i: (i, 0)),
            pl.BlockSpec((tm, D_MODEL), lambda i: (i, 0)),
        ],
        out_shape=[
            jax.ShapeDtypeStruct((n, P_COLS), F32),
            jax.ShapeDtypeStruct((n, D_MODEL), F32),
        ],
        scratch_shapes=[pltpu.VMEM((tm, D_MODEL), BF16)],
        compiler_params=_cparams(("arbitrary",)),
        name="in_proj",
    )(x, g, b, w_p, cos, sin)


def _attend(q, k, v, valid, sink_ref):
    r = q.shape[0]
    outs = []
    for kv in range(ATT_KV_HEADS):
        kh = k[:, kv * HEAD_DIM:(kv + 1) * HEAD_DIM].astype(BF16)
        vh = v[:, kv * HEAD_DIM:(kv + 1) * HEAD_DIM].astype(BF16)
        heads = [kv * ATT_GROUP + g for g in range(ATT_GROUP)]
        qs = jnp.concatenate([q[:, h * HEAD_DIM:(h + 1) * HEAD_DIM] for h in heads], axis=0)
        sink = jnp.concatenate([jnp.full((r, 1), sink_ref[h], F32) for h in heads], axis=0)
        s = lax.dot_general(qs.astype(BF16), kh, (((1,), (1,)), ((), ())), preferred_element_type=F32) * ATT_SCALE
        if valid is not None:
            s = jnp.where(valid, s, -jnp.inf)
        m = jnp.maximum(jnp.max(s, -1, keepdims=True), sink)
        p = jnp.exp(s - m)
        den = jnp.sum(p, -1, keepdims=True) + jnp.exp(sink - m)
        o = jnp.dot(p.astype(BF16), vh, preferred_element_type=F32) / den
        outs.extend(o[g * r:(g + 1) * r] for g in range(ATT_GROUP))
    return jnp.concatenate(outs, axis=1)


QB = 2 * CHUNK


def _attn_prompt_kernel(sink_ref, q_ref, kp_ref, kc_ref, vp_ref, vc_ref, o_ref):
    m = pl.program_id(1)
    span = 2 * QB
    k_all = jnp.concatenate([kp_ref[...], kc_ref[...]], axis=0)
    v_all = jnp.concatenate([vp_ref[...], vc_ref[...]], axis=0)
    lane = lax.broadcasted_iota(jnp.int32, (span, KV_WIDTH), 1)
    row_c = 2 + lax.broadcasted_iota(jnp.int32, (QB, span), 0) // CHUNK
    col_c = lax.broadcasted_iota(jnp.int32, (QB, span), 1) // CHUNK
    back = row_c - col_c
    valid = jnp.logical_and(jnp.logical_and(back >= 0, back <= 2), 2 * m - 2 + col_c >= 0)
    out_lane = lax.broadcasted_iota(jnp.int32, (QB, PAIR), 1)

    def dup(x, kv):
        sw = pltpu.roll(x, HEAD_DIM, 1)
        return jnp.where(lane < HEAD_DIM, x, sw) if kv == 0 else jnp.where(lane < HEAD_DIM, sw, x)

    k_st = [_stack_heads(dup(k_all, kv), HEAD_DIM).astype(BF16) for kv in range(ATT_KV_HEADS)]
    v_st = [_stack_heads(dup(v_all, kv), HEAD_DIM).astype(BF16) for kv in range(ATT_KV_HEADS)]
    pairs = range(ATT_HEADS // 2)
    kv_of = lambda j: (2 * j) // ATT_GROUP
    s = [lax.dot_general(q_ref[:, j * PAIR:(j + 1) * PAIR].astype(BF16), k_st[kv_of(j)],
                         (((1,), (1,)), ((), ())), preferred_element_type=F32) * ATT_SCALE for j in pairs]
    probs, dens = [], []
    for j in pairs:
        halves, den = [], []
        for e in range(2):
            sink = sink_ref[2 * j + e]
            sh = jnp.where(valid, s[j][:, e * span:(e + 1) * span], -jnp.inf)
            mx = jnp.maximum(jnp.max(sh, -1, keepdims=True), sink)
            ph = jnp.exp(sh - mx)
            halves.append(ph)
            den.append(jnp.sum(ph, -1, keepdims=True) + jnp.exp(sink - mx))
        probs.append(jnp.concatenate(halves, axis=1).astype(BF16))
        dens.append(jnp.where(out_lane < HEAD_DIM, den[0], den[1]))
    o = [jnp.dot(probs[j], v_st[kv_of(j)], preferred_element_type=F32) / dens[j] for j in pairs]
    o_ref[...] = jnp.concatenate(o, axis=1).astype(o_ref.dtype)


def _attn_prompt(p, sinks, batch, seq):
    nb = seq // QB
    kcol, vcol = P_AK // KV_WIDTH, P_AV // KV_WIDTH
    prev = lambda col: pl.BlockSpec((QB, KV_WIDTH), lambda b, m: (b * nb + jnp.maximum(m - 1, 0), col))
    cur = lambda col: pl.BlockSpec((QB, KV_WIDTH), lambda b, m: (b * nb + m, col))
    return pl.pallas_call(
        _attn_prompt_kernel,
        grid=(batch, nb),
        in_specs=[
            pl.BlockSpec(memory_space=pltpu.SMEM),
            pl.BlockSpec((QB, ATT_WIDTH), lambda b, m: (b * nb + m, P_Q // ATT_WIDTH)),
            prev(kcol), cur(kcol), prev(vcol), cur(vcol),
        ],
        out_specs=pl.BlockSpec((QB, ATT_WIDTH), lambda b, m: (b * nb + m, 0)),
        out_shape=jax.ShapeDtypeStruct((batch * seq, ATT_WIDTH), BF16),
        compiler_params=_cparams(("arbitrary", "arbitrary")),
        name="attn_prompt",
    )(sinks, p, p, p, p, p)


def _attn_sample_kernel(sink_ref, q_ref, kn_ref, vn_ref, kc_ref, vc_ref, o_ref, kw_ref, vw_ref):
    t = q_ref.shape[0]
    k = jnp.concatenate([kc_ref[...], kn_ref[...]], axis=0)
    v = jnp.concatenate([vc_ref[...], vn_ref[...]], axis=0)
    o_ref[...] = _attend(q_ref[...], k, v, None, sink_ref).astype(o_ref.dtype)
    kw_ref[...] = k[t:]
    vw_ref[...] = v[t:]


def _attn_sample(p, sinks, cache_k, cache_v, batch, seq):
    kcol, vcol = P_AK // KV_WIDTH, P_AV // KV_WIDTH
    return pl.pallas_call(
        _attn_sample_kernel,
        grid=(batch,),
        in_specs=[
            pl.BlockSpec(memory_space=pltpu.SMEM),
            pl.BlockSpec((seq, ATT_WIDTH), lambda b: (b, P_Q // ATT_WIDTH)),
            pl.BlockSpec((seq, KV_WIDTH), lambda b: (b, kcol)),
            pl.BlockSpec((seq, KV_WIDTH), lambda b: (b, vcol)),
            pl.BlockSpec((WINDOW, KV_WIDTH), lambda b: (b, 0)),
            pl.BlockSpec((WINDOW, KV_WIDTH), lambda b: (b, 0)),
        ],
        out_specs=[
            pl.BlockSpec((seq, ATT_WIDTH), lambda b: (b, 0)),
            pl.BlockSpec((WINDOW, KV_WIDTH), lambda b: (b, 0)),
            pl.BlockSpec((WINDOW, KV_WIDTH), lambda b: (b, 0)),
        ],
        out_shape=[
            jax.ShapeDtypeStruct((batch * seq, ATT_WIDTH), BF16),
            jax.ShapeDtypeStruct((batch * WINDOW, KV_WIDTH), F32),
            jax.ShapeDtypeStruct((batch * WINDOW, KV_WIDTH), F32),
        ],
        compiler_params=_cparams(("arbitrary",)),
        name="attn_sample",
    )(sinks, p, p, p, cache_k, cache_v)


SEG = 256
PAIR = 2 * HEAD_DIM
N_PAIRS = RW_HEADS // 2


def _split2(x):
    hi = x.astype(BF16)
    lo = (x - hi.astype(F32)).astype(BF16)
    return hi, lo


def _seg_sum(x, bd):
    c = x.shape[0]
    slabs = range(0, RW_WIDTH, SEG)
    stack = jnp.concatenate([part[:, s:s + SEG] for part in _split2(x) for s in slabs], axis=0)
    r = jnp.dot(stack, bd, preferred_element_type=F32)
    n = len(slabs)
    return jnp.concatenate([r[i * c:(i + 1) * c] + r[(n + i) * c:(n + i + 1) * c] for i in range(n)], axis=1)


def _shift_rows(x, prev_row):
    row = lax.broadcasted_iota(jnp.int32, x.shape, 0)
    return jnp.where(row == 0, prev_row, pltpu.roll(x, 1, 0))


def _dotb(a, b):
    return jnp.dot(a.astype(BF16), b.astype(BF16), preferred_element_type=F32)


def _dotb_nt(a, b):
    return lax.dot_general(a.astype(BF16), b.astype(BF16), (((1,), (1,)), ((), ())), preferred_element_type=F32)


def _dot3_tn(a, b):
    ah, al = _split2(a)
    bh, bl = _split2(b)
    return lax.dot_general(jnp.concatenate([ah, al, ah], axis=0), jnp.concatenate([bh, bh, bl], axis=0),
                           (((0,), (0,)), ((), ())), preferred_element_type=F32)


def _dot3(a, b):
    ah, al = _split2(a)
    bh, bl = _split2(b)
    return jnp.dot(jnp.concatenate([ah, al, ah], axis=1), jnp.concatenate([bh, bh, bl], axis=0),
                   preferred_element_type=F32)


def _stack_heads(x, half):
    lane = lax.broadcasted_iota(jnp.int32, x.shape, 1)
    first = (lane % (2 * half)) < half
    return jnp.concatenate([jnp.where(first, x, 0.0), jnp.where(first, 0.0, x)], axis=0)


def _rwkv_pairs(ab, rb, bb, kb, v, gam_c, s_bd, n_double):
    c = ab[0].shape[0]
    pairs = range(len(ab))
    t_idx = lax.broadcasted_iota(jnp.int32, (c, 2 * c), 0)
    s_idx = lax.broadcasted_iota(jnp.int32, (c, 2 * c), 1) % c
    strict, incl = s_idx < t_idx, s_idx <= t_idx
    g = [_dotb_nt(jnp.concatenate([ab[p], rb[p]], axis=0),
                  jnp.concatenate([_stack_heads(bb[p], HEAD_DIM), _stack_heads(kb[p], HEAD_DIM)], axis=0))
         for p in pairs]
    l_ak = [jnp.where(strict, g[p][:c, 2 * c:], 0.0) for p in pairs]
    m_rbk = [jnp.concatenate([jnp.where(incl, g[p][c:, :2 * c], 0.0), jnp.where(incl, g[p][c:, 2 * c:], 0.0)], axis=1)
             for p in pairs]
    lp = [jnp.where(strict, g[p][:c, :2 * c], 0.0) for p in pairs]
    v_st = [_stack_heads(v[p], HEAD_DIM) for p in pairs]
    ya = list(ab)
    yu = [_dotb(l_ak[p], v_st[p]) for p in pairs]
    for i in range(n_double):
        y_st = [_stack_heads(jnp.concatenate([ya[p], yu[p]], axis=1), HEAD_DIM) for p in pairs]
        if i + 1 < n_double:
            z = [_dotb(lp[p], jnp.concatenate([y_st[p], _stack_heads(lp[p], c)], axis=1)) for p in pairs]
            lp = [z[p][:, 2 * PAIR:] for p in pairs]
        else:
            z = [_dotb(lp[p], y_st[p]) for p in pairs]
        ya = [ya[p] + z[p][:, :PAIR] for p in pairs]
        yu = [yu[p] + z[p][:, PAIR:2 * PAIR] for p in pairs]
    z = []
    for p in pairs:
        top = jnp.concatenate([_stack_heads(ya[p], HEAD_DIM), _stack_heads(yu[p], HEAD_DIM)], axis=1)
        bot = jnp.concatenate([jnp.zeros_like(v_st[p]), v_st[p]], axis=1)
        z.append(_dotb(m_rbk[p], jnp.concatenate([top, bot], axis=0)))
    o = [_dotb_nt(rb[p] + z[p][:, :PAIR], s_bd[p]) + z[p][:, PAIR:] for p in pairs]
    row_h = lax.broadcasted_iota(jnp.int32, (PAIR, PAIR), 0) // HEAD_DIM
    col_h = lax.broadcasted_iota(jnp.int32, (PAIR, PAIR), 1) // HEAD_DIM
    same = row_h == col_h
    q_t = [jnp.where(same, _dot3_tn(ya[p], bb[p]), 0.0) for p in pairs]
    d_t = [jnp.where(same, _dot3_tn(jnp.concatenate([yu[p], v[p]], axis=0),
                                    jnp.concatenate([bb[p], kb[p]], axis=0)), 0.0) for p in pairs]
    s_new = [(s_bd[p] + _dot3(s_bd[p], q_t[p]) + d_t[p]) * gam_c[p] for p in pairs]
    return o, s_new


def _rwkv_kernel(xr_ref, xk_ref, xv_ref, xl_ref, sr_ref, sk_ref, sv_ref, sl_ref,
                 mur_ref, muk_ref, muv_ref, mul_ref, w0_ref, a0_ref, kk_ref, ka_ref, rk_ref,
                 lg_ref, lb_ref, w2_ref, a2_ref, g2_ref, bd_ref, st0_ref,
                 out_ref, st_ref,
                 s_scr, pr_scr, pk_scr, pv_scr, pl_scr):
    c = pl.program_id(1)
    nc = pl.num_programs(1)
    C = xr_ref.shape[0]
    n_double = int(math.log2(C))

    @pl.when(c == 0)
    def _():
        s_scr[...] = jnp.zeros_like(s_scr)
        for h in range(RW_HEADS):
            off = (h % 2) * HEAD_DIM
            s_scr[h // 2, off:off + HEAD_DIM, off:off + HEAD_DIM] = st0_ref[0, h]
        pr_scr[0:1, :] = sr_ref[0]
        pk_scr[0:1, :] = sk_ref[0]
        pv_scr[0:1, :] = sv_ref[0]
        pl_scr[0:1, :] = sl_ref[0]

    def shifted(x_ref, p_scr, mu_ref):
        x = x_ref[...]
        prev = _shift_rows(x, p_scr[0:1, :])
        p_scr[0:1, :] = x[C - 1:C, :]
        return x + (prev - x) * mu_ref[...]

    r = shifted(xr_ref, pr_scr, mur_ref)
    kr = shifted(xk_ref, pk_scr, muk_ref)
    v = shifted(xv_ref, pv_scr, muv_ref)
    xl = shifted(xl_ref, pl_scr, mul_ref)
    wd = xl[:, 0:DECAY_LORA]
    ad = xl[:, DECAY_LORA:DECAY_LORA + AAA_LORA]
    gd = xl[:, DECAY_LORA + AAA_LORA:DECAY_LORA + AAA_LORA + GATE_LORA]

    w_log = -jax.nn.softplus(-(w0_ref[...] + _dotb(jnp.tanh(wd), w2_ref[...]))) - 0.5
    lw = -jnp.exp(w_log)
    a = jax.nn.sigmoid(a0_ref[...] + _dotb(ad, a2_ref[...]))
    g = _dotb(jax.nn.sigmoid(gd), g2_ref[...])
    bd = bd_ref[...]
    kk = kr * kk_ref[...]
    kk = kk / jnp.maximum(jnp.sqrt(_seg_sum(kk * kk, bd)), 1e-12)
    k2 = kr * (1.0 + (a - 1.0) * ka_ref[...])

    ti = lax.broadcasted_iota(jnp.int32, (C, C), 0)
    si = lax.broadcasted_iota(jnp.int32, (C, C), 1)
    tri = (si <= ti).astype(BF16)
    lw_hi = lw.astype(BF16)
    lw_mid, lw_lo = _split2(lw - lw_hi.astype(F32))
    cs3 = jnp.dot(tri, jnp.concatenate([lw_hi, lw_mid, lw_lo], axis=1), preferred_element_type=F32)
    cs = cs3[:, :RW_WIDTH] + cs3[:, RW_WIDTH:2 * RW_WIDTH] + cs3[:, 2 * RW_WIDTH:]
    gam = jnp.exp(cs)
    ginv = jnp.exp(-cs)
    ab = -kk * jnp.exp(cs - lw)
    rb = r * gam
    bb = kk * a * ginv
    kb = k2 * ginv
    gam_c = gam[C - 1:C, :]

    per_pair = lambda x: [x[:, p * PAIR:(p + 1) * PAIR] for p in range(N_PAIRS)]
    outs, s_new = _rwkv_pairs(per_pair(ab), per_pair(rb), per_pair(bb), per_pair(kb), per_pair(v), per_pair(gam_c),
                              [s_scr[p] for p in range(N_PAIRS)], n_double)
    for p in range(N_PAIRS):
        s_scr[p] = s_new[p]
    o = jnp.concatenate(outs, axis=1)

    mo = _seg_sum(o, bd) * (1.0 / HEAD_DIM)
    d = o - mo
    vo = _seg_sum(d * d, bd) * (1.0 / HEAD_DIM)
    on = d * lax.rsqrt(vo + LNX_EPS) * lg_ref[...] + lb_ref[...]
    bonus = _seg_sum(r * k2 * rk_ref[...], bd) * v
    out_ref[...] = ((on + bonus) * g).astype(out_ref.dtype)

    @pl.when(c == nc - 1)
    def _():
        for h in range(RW_HEADS):
            off = (h % 2) * HEAD_DIM
            st_ref[0, h] = s_scr[h // 2, off:off + HEAD_DIM, off:off + HEAD_DIM]


def _rwkv(p, shift_p, state0, prm, batch, seq, chunk):
    nc = seq // chunk
    row = lambda w: pl.BlockSpec((1, w), lambda b, c: (0, 0))
    xspec = lambda w, col: pl.BlockSpec((chunk, w), lambda b, c: (b * nc + c, col))
    sspec = lambda w: pl.BlockSpec((1, 1, w), lambda b, c: (b, 0, 0))
    full = lambda a: pl.BlockSpec(a.shape, lambda b, c: (0,) * a.ndim)
    sr, sk, sv, sl = shift_p
    return pl.pallas_call(
        _rwkv_kernel,
        grid=(batch, nc),
        in_specs=[
            xspec(RW_WIDTH, P_R // RW_WIDTH), xspec(RW_WIDTH, P_K // RW_WIDTH), xspec(RW_WIDTH, P_V // RW_WIDTH),
            xspec(LORA_W, P_LORA // LORA_W),
            sspec(RW_WIDTH), sspec(RW_WIDTH), sspec(RW_WIDTH), sspec(LORA_W),
            row(RW_WIDTH), row(RW_WIDTH), row(RW_WIDTH), row(LORA_W),
            row(RW_WIDTH), row(RW_WIDTH), row(RW_WIDTH), row(RW_WIDTH), row(RW_WIDTH),
            row(RW_WIDTH), row(RW_WIDTH),
            full(prm["w2"]), full(prm["a2"]), full(prm["g2"]), full(prm["bd"]),
            pl.BlockSpec((1, RW_HEADS, HEAD_DIM, HEAD_DIM), lambda b, c: (b, 0, 0, 0)),
        ],
        out_specs=[
            pl.BlockSpec((chunk, RW_WIDTH), lambda b, c: (b * nc + c, 0)),
            pl.BlockSpec((1, RW_HEADS, HEAD_DIM, HEAD_DIM), lambda b, c: (b, 0, 0, 0)),
        ],
        out_shape=[
            jax.ShapeDtypeStruct((batch * seq, RW_WIDTH), BF16),
            jax.ShapeDtypeStruct((batch, RW_HEADS, HEAD_DIM, HEAD_DIM), F32),
        ],
        scratch_shapes=[
            pltpu.VMEM((N_PAIRS, PAIR, PAIR), F32),
            pltpu.VMEM((SUBLANES, RW_WIDTH), F32), pltpu.VMEM((SUBLANES, RW_WIDTH), F32),
            pltpu.VMEM((SUBLANES, RW_WIDTH), F32), pltpu.VMEM((SUBLANES, LORA_W), F32),
        ],
        compiler_params=_cparams(("arbitrary", "arbitrary")),
        name="rwkv",
    )(p, p, p, p, sr, sk, sv, sl,
      prm["mu_r"], prm["mu_k"], prm["mu_v"], prm["mu_l"], prm["w0"], prm["a0"], prm["k_k"], prm["k_a"], prm["r_k"],
      prm["lnx_g"], prm["lnx_b"], prm["w2"], prm["a2"], prm["g2"], prm["bd"], state0)


def _out_proj_kernel(att_ref, rw_ref, wa_ref, wb_ref, h_ref, g_ref, b_ref, h1_ref, h1b_ref):
    m = jnp.dot(att_ref[...], wa_ref[...], preferred_element_type=F32)
    m = m + jnp.dot(rw_ref[...], wb_ref[...], preferred_element_type=F32)
    y = _layer_norm(ALPHA * h_ref[...] + m, g_ref[...], b_ref[...])
    h1_ref[...] = y
    h1b_ref[...] = y.astype(BF16)


def _out_proj(att, rw, w_out, h, g, b, tm):
    n = att.shape[0]
    return pl.pallas_call(
        _out_proj_kernel,
        grid=(n // tm,),
        in_specs=[
            pl.BlockSpec((tm, ATT_WIDTH), lambda i: (i, 0)),
            pl.BlockSpec((tm, RW_WIDTH), lambda i: (i, 0)),
            pl.BlockSpec((ATT_WIDTH, D_MODEL), lambda i: (0, 0)),
            pl.BlockSpec((RW_WIDTH, D_MODEL), lambda i: (1, 0)),
            pl.BlockSpec((tm, D_MODEL), lambda i: (i, 0)),
            pl.BlockSpec((1, D_MODEL), lambda i: (0, 0)),
            pl.BlockSpec((1, D_MODEL), lambda i: (0, 0)),
        ],
        out_specs=[
            pl.BlockSpec((tm, D_MODEL), lambda i: (i, 0)),
            pl.BlockSpec((tm, D_MODEL), lambda i: (i, 0)),
        ],
        out_shape=[
            jax.ShapeDtypeStruct((n, D_MODEL), F32),
            jax.ShapeDtypeStruct((n, D_MODEL), BF16),
        ],
        compiler_params=_cparams(("arbitrary",)),
        name="out_proj",
    )(att, rw, w_out, w_out, h, g, b)


E_TN = 512
E_NCB = D_FF // E_TN
E_SUB = 512
SQRT_HALF = math.sqrt(0.5)


def _ffn_up_kernel(bps, nseq, h_ref, wg_ref, wv_ref, cwg_ref, cwv_ref, cbg_ref, cbv_ref, sg_ref, sv_ref,
                   act_ref, og_ref, ov_ref, w_scr, cg_scr, cv_scr):
    i = pl.program_id(1)
    tm = h_ref.shape[0]
    sub = min(tm, E_SUB)
    n_sub = tm // sub
    t_seq = tm // nseq
    row = lax.broadcasted_iota(jnp.int32, (sub, E_TN), 0)

    @pl.when(i == 0)
    def _():
        w_scr[:, :E_TN] = wg_ref[...].astype(BF16)
        w_scr[:, E_TN:] = wv_ref[...].astype(BF16)

    def products(r0):
        u = jnp.dot(h_ref[r0:r0 + sub, :], w_scr[...], preferred_element_type=F32)
        return u[:, :E_TN], u[:, E_TN:]

    def conv(u, prevs, cw_ref, cb_ref):
        sh1, sh2 = pltpu.roll(u, 1, 0), pltpu.roll(u, 2, 0)
        for r, prev in prevs:
            p2, p1 = prev[0:1, :], prev[1:2, :]
            sh1 = jnp.where(row == r, p1, sh1)
            sh2 = jnp.where(row == r, p2, jnp.where(row == r + 1, p1, sh2))
        cw = cw_ref[...]
        c = cb_ref[...] + sh2 * cw[0:1, :]
        c = c + sh1 * cw[1:2, :]
        return c + u * cw[2:3, :]

    def epilogue(r0, ug, uv, prev_g, prev_v):
        gate = conv(ug, prev_g, cwg_ref, cbg_ref)
        val = conv(uv, prev_v, cwv_ref, cbv_ref)
        gelu = 0.5 * gate * (1.0 + lax.erf(gate * SQRT_HALF))
        act_ref[r0:r0 + sub, :] = (gelu * val).astype(act_ref.dtype)

    if nseq > 1:
        ug, uv = products(0)
        epilogue(0, ug, uv, [(s * t_seq, sg_ref[s]) for s in range(nseq)],
                 [(s * t_seq, sv_ref[s]) for s in range(nseq)])
        for s in range(nseq):
            og_ref[s] = ug[(s + 1) * t_seq - SUBLANES:(s + 1) * t_seq, :]
            ov_ref[s] = uv[(s + 1) * t_seq - SUBLANES:(s + 1) * t_seq, :]
    else:
        first = (i % bps) == 0
        prev_g = jnp.where(first, sg_ref[0], cg_scr[SUBLANES - 2:SUBLANES, :])
        prev_v = jnp.where(first, sv_ref[0], cv_scr[SUBLANES - 2:SUBLANES, :])
        nxt = products(0)
        for k in range(n_sub):
            ug, uv = nxt
            if k + 1 < n_sub:
                nxt = products((k + 1) * sub)
            epilogue(k * sub, ug, uv, [(0, prev_g)], [(0, prev_v)])
            prev_g, prev_v = ug[sub - 2:sub, :], uv[sub - 2:sub, :]
        cg_scr[...] = ug[sub - SUBLANES:sub, :]
        cv_scr[...] = uv[sub - SUBLANES:sub, :]
        og_ref[0] = ug[sub - SUBLANES:sub, :]
        ov_ref[0] = uv[sub - SUBLANES:sub, :]


def _ffn_up(h1b, w_up, conv_w, conv_b, conv_prev, batch, seq, tm):
    n = h1b.shape[0]
    nseq = max(tm // seq, 1)
    bps = max(seq // tm, 1)
    nrb = n // tm
    return pl.pallas_call(
        functools.partial(_ffn_up_kernel, bps, nseq),
        grid=(E_NCB, nrb),
        in_specs=[
            pl.BlockSpec((tm, D_MODEL), lambda j, i: (i, 0)),
            pl.BlockSpec((D_MODEL, E_TN), lambda j, i: (0, j)),
            pl.BlockSpec((D_MODEL, E_TN), lambda j, i: (0, j + E_NCB)),
            pl.BlockSpec((CONV_W, E_TN), lambda j, i: (0, j)),
            pl.BlockSpec((CONV_W, E_TN), lambda j, i: (0, j + E_NCB)),
            pl.BlockSpec((1, E_TN), lambda j, i: (0, j)),
            pl.BlockSpec((1, E_TN), lambda j, i: (0, j + E_NCB)),
            pl.BlockSpec((nseq, CONV_W - 1, E_TN), lambda j, i: (i // bps, 0, j)),
            pl.BlockSpec((nseq, CONV_W - 1, E_TN), lambda j, i: (i // bps, 0, j + E_NCB)),
        ],
        out_specs=[
            pl.BlockSpec((tm, E_TN), lambda j, i: (i, j)),
            pl.BlockSpec((nseq, SUBLANES, E_TN), lambda j, i: (i // bps, 0, j)),
            pl.BlockSpec((nseq, SUBLANES, E_TN), lambda j, i: (i // bps, 0, j)),
        ],
        out_shape=[
            jax.ShapeDtypeStruct((n, D_FF), BF16),
            jax.ShapeDtypeStruct((batch, SUBLANES, D_FF), F32),
            jax.ShapeDtypeStruct((batch, SUBLANES, D_FF), F32),
        ],
        scratch_shapes=[pltpu.VMEM((D_MODEL, 2 * E_TN), BF16),
                        pltpu.VMEM((SUBLANES, E_TN), F32), pltpu.VMEM((SUBLANES, E_TN), F32)],
        compiler_params=_cparams(("arbitrary", "arbitrary")),
        name="ffn_up",
    )(h1b, w_up, w_up, conv_w, conv_w, conv_b, conv_b, conv_prev, conv_prev)


F_TN = 512


def _ffn_down_kernel(act_ref, w_ref, h1_ref, g_ref, b_ref, y_ref):
    for c0 in range(0, D_MODEL, F_TN):
        f = jnp.dot(act_ref[...], w_ref[:, c0:c0 + F_TN], preferred_element_type=F32)
        y_ref[:, c0:c0 + F_TN] = ALPHA * h1_ref[:, c0:c0 + F_TN] + f
    y_ref[...] = _layer_norm(y_ref[...], g_ref[...], b_ref[...])


def _ffn_down(act, w_down, h1, g, b, tm):
    n = act.shape[0]
    return pl.pallas_call(
        _ffn_down_kernel,
        grid=(n // tm,),
        in_specs=[
            pl.BlockSpec((tm, D_FF), lambda i: (i, 0)),
            pl.BlockSpec((D_FF, D_MODEL), lambda i: (0, 0), pipeline_mode=pl.Buffered(1)),
            pl.BlockSpec((tm, D_MODEL), lambda i: (i, 0)),
            pl.BlockSpec((1, D_MODEL), lambda i: (0, 0)),
            pl.BlockSpec((1, D_MODEL), lambda i: (0, 0)),
        ],
        out_specs=pl.BlockSpec((tm, D_MODEL), lambda i: (i, 0)),
        out_shape=jax.ShapeDtypeStruct((n, D_MODEL), F32),
        compiler_params=_cparams(("arbitrary",)),
        name="ffn_down",
    )(act, w_down, h1, g, b)


def _rope_tables(pos, reps):
    half = HEAD_DIM // 2
    inv = ROPE_THETA ** (-jnp.arange(half, dtype=F32) / half)
    ang = pos.astype(F32)[:, None] * inv[None, :]
    cos, sin = jnp.cos(ang), jnp.sin(ang)
    cos128 = jnp.concatenate([cos, cos, cos, cos], axis=1)
    sin128 = jnp.concatenate([-sin, sin, -sin, sin], axis=1)
    return jnp.tile(cos128, (reps, 1)), jnp.tile(sin128, (reps, 1))


def _permute_rw(a):
    o_wd, o_k, o_v = RW_WIDTH, RW_WIDTH + DECAY_LORA, 2 * RW_WIDTH + DECAY_LORA
    o_ad = 3 * RW_WIDTH + DECAY_LORA
    o_gd = o_ad + AAA_LORA
    pad = jnp.zeros(a.shape[:-1] + (LORA_W - DECAY_LORA - AAA_LORA - GATE_LORA,), a.dtype)
    return jnp.concatenate([a[..., :RW_WIDTH], a[..., o_k:o_k + RW_WIDTH], a[..., o_v:o_v + RW_WIDTH],
                            a[..., o_wd:o_wd + DECAY_LORA], a[..., o_ad:o_ad + AAA_LORA],
                            a[..., o_gd:o_gd + GATE_LORA], pad], axis=-1)


def _split_rw(a):
    return (a[..., :RW_WIDTH], a[..., RW_WIDTH:2 * RW_WIDTH], a[..., 2 * RW_WIDTH:3 * RW_WIDTH],
            a[..., 3 * RW_WIDTH:])


def _unpermute_shift(p_row):
    lo = P_LORA
    return jnp.concatenate([
        p_row[..., P_R:P_R + RW_WIDTH], p_row[..., lo:lo + DECAY_LORA],
        p_row[..., P_K:P_K + RW_WIDTH], p_row[..., P_V:P_V + RW_WIDTH],
        p_row[..., lo + DECAY_LORA:lo + DECAY_LORA + AAA_LORA],
        p_row[..., lo + DECAY_LORA + AAA_LORA:lo + DECAY_LORA + AAA_LORA + GATE_LORA]], axis=-1)


def _run(x, pos_tables, table_blocks, attn_fn, state_wkv, state_shift, state_conv, wts, tiles):
    batch, seq = x.shape[0], x.shape[1]
    n = batch * seq
    cos, sin = pos_tables
    p, h = _in_proj(x.reshape(n, D_MODEL), wts["ln_in_g"], wts["ln_in_b"], wts["w_in_p"], cos, sin,
                    tiles["a_tm"], table_blocks)
    att, k_win, v_win = attn_fn(p)
    shift_p = tuple(s for s in _split_rw(_permute_rw(state_shift)))
    rw, wkv = _rwkv(p, shift_p, state_wkv, wts["rw"], batch, seq, tiles["chunk"])
    h1, h1b = _out_proj(att, rw, wts["w_out"], h, wts["ln1_g"], wts["ln1_b"], tiles["d_tm"])
    act, cg, cv = _ffn_up(h1b, wts["w_up"], wts["conv_w"], wts["conv_b"], state_conv, batch, seq, tiles["e_tm"])
    y = _ffn_down(act, wts["w_down"], h1, wts["ln2_g"], wts["ln2_b"], tiles["f_tm"])
    p3 = p.reshape(batch, seq, P_COLS)
    shift_out = _unpermute_shift(p3[:, seq - 1:seq, :])
    conv_out = jnp.concatenate([cg[:, SUBLANES - 2:, :], cv[:, SUBLANES - 2:, :]], axis=-1)
    if k_win is None:
        k_win = p3[:, seq - WINDOW:, P_AK:P_AK + KV_WIDTH]
        v_win = p3[:, seq - WINDOW:, P_AV:P_AV + KV_WIDTH]
    kv_shape = (1, batch, WINDOW, ATT_KV_HEADS, HEAD_DIM)
    return (y.reshape(batch, seq, D_MODEL), k_win.reshape(kv_shape), v_win.reshape(kv_shape),
            wkv[None], shift_out[None], conv_out[None])


def kernel(x_prompt, x_sample, cache_k, cache_v, state_wkv, state_shift, state_ffn_conv, ln_in_g, ln_in_b, w_in, attn_sinks, rw_mu, rw_w0, rw_w2, rw_a0, rw_a2, rw_g2, rw_k_k, rw_k_a, rw_r_k, rw_lnx_g, rw_lnx_b, w_out, ln1_g, ln1_b, ffn_w_up, ffn_conv_w, ffn_conv_b, ffn_w_down, ln2_g, ln2_b):
    l = 0
    w = w_in[l]
    w_att, w_rw = w[:, :ATT_COLS], _permute_rw(w[:, ATT_COLS:])
    w_in_p = jnp.concatenate([w_rw[:, :3 * RW_WIDTH], w_att[:, :ATT_WIDTH], w_rw[:, 3 * RW_WIDTH:],
                              w_att[:, ATT_WIDTH:]], axis=1).astype(BF16)
    mu_r, mu_k, mu_v, mu_l = _split_rw(_permute_rw(rw_mu[l])[None])
    seg = lax.broadcasted_iota(jnp.int32, (SEG, SEG), 0) // HEAD_DIM
    bd = (seg == seg.T).astype(BF16)
    row = lambda a: a.reshape(1, -1)
    wts = {
        "ln_in_g": row(ln_in_g), "ln_in_b": row(ln_in_b), "w_in_p": w_in_p,
        "rw": {"mu_r": mu_r, "mu_k": mu_k, "mu_v": mu_v, "mu_l": mu_l,
               "w0": row(rw_w0[l]), "a0": row(rw_a0[l]), "k_k": row(rw_k_k[l]), "k_a": row(rw_k_a[l]),
               "r_k": row(rw_r_k[l]), "lnx_g": row(rw_lnx_g[l]), "lnx_b": row(rw_lnx_b[l]),
               "w2": rw_w2[l].astype(BF16), "a2": rw_a2[l].astype(BF16), "g2": rw_g2[l].astype(BF16), "bd": bd},
        "w_out": w_out[l].astype(BF16), "ln1_g": row(ln1_g[l]), "ln1_b": row(ln1_b[l]),
        "w_up": ffn_w_up[l], "conv_w": ffn_conv_w[l], "conv_b": row(ffn_conv_b[l]),
        "w_down": ffn_w_down[l].astype(BF16), "ln2_g": row(ln2_g[l]), "ln2_b": row(ln2_b[l]),
    }
    sinks = attn_sinks[l]

    bp, tp = x_prompt.shape[0], x_prompt.shape[1]
    p_tiles = {"a_tm": 256, "chunk": CHUNK, "d_tm": 256, "e_tm": 1024, "f_tm": 256}
    p_tables = _rope_tables(jnp.arange(tp, dtype=jnp.int32), 1)
    zeros = lambda *s: jnp.zeros(s, F32)
    y_p, p_k, p_v, p_wkv, p_shift, p_conv = _run(
        x_prompt, p_tables, tp // p_tiles["a_tm"],
        lambda p: (_attn_prompt(p, sinks, bp, tp), None, None),
        zeros(bp, RW_HEADS, HEAD_DIM, HEAD_DIM), zeros(bp, 1, RW_COLS), zeros(bp, CONV_W - 1, 2 * D_FF),
        wts, p_tiles)

    bs, ts = x_sample.shape[0], x_sample.shape[1]
    s_tiles = {"a_tm": bs * ts, "chunk": ts, "d_tm": bs * ts, "e_tm": bs * ts, "f_tm": bs * ts}
    s_tables = _rope_tables(PAST_LEN + jnp.arange(ts, dtype=jnp.int32), bs)
    ck = cache_k[l].reshape(bs * WINDOW, KV_WIDTH)
    cv = cache_v[l].reshape(bs * WINDOW, KV_WIDTH)
    y_s, s_k, s_v, s_wkv, s_shift, s_conv = _run(
        x_sample, s_tables, 1,
        lambda p: _attn_sample(p, sinks, ck, cv, bs, ts),
        state_wkv[l], state_shift[l], state_ffn_conv[l], wts, s_tiles)

    return (y_p, y_s, p_k, p_v, p_wkv, p_shift, p_conv, s_k, s_v, s_wkv, s_shift, s_conv)
```

```python
import functools
import math

import jax
import jax.numpy as jnp
from jax import lax
from jax.experimental import pallas as pl
from jax.experimental.pallas import tpu as pltpu

D_MODEL = 2048
HEAD_DIM = 64
ATT_HEADS = 16
ATT_KV_HEADS = 2
ATT_GROUP = ATT_HEADS // ATT_KV_HEADS
ATT_WIDTH = ATT_HEADS * HEAD_DIM
KV_WIDTH = ATT_KV_HEADS * HEAD_DIM
CHUNK = 64
WINDOW = 128
ROPE_THETA = 10000.0
ATT_SCALE = HEAD_DIM ** -0.5
RW_HEADS = 16
RW_WIDTH = RW_HEADS * HEAD_DIM
DECAY_LORA = 64
AAA_LORA = 64
GATE_LORA = 160
LNX_EPS = 64e-5
ATT_COLS = ATT_WIDTH + 2 * KV_WIDTH
RW_COLS = 3 * RW_WIDTH + DECAY_LORA + AAA_LORA + GATE_LORA
D_FF = 5632
CONV_W = 3
LN_EPS = 1e-5
DEPTH = 1
ALPHA = (2 * DEPTH) ** 0.25
PAST_LEN = 1024

LORA_W = 512
P_R, P_K, P_V, P_Q = 0, 1024, 2048, 3072
P_LORA = 4096
P_AK = P_LORA + LORA_W
P_AV = P_AK + KV_WIDTH
P_COLS = P_AV + KV_WIDTH
LANES = 128
SUBLANES = 8
VMEM_LIMIT = 56 * 1024 * 1024

F32 = jnp.float32
BF16 = jnp.bfloat16


def _cparams(sem):
    return pltpu.CompilerParams(dimension_semantics=sem, vmem_limit_bytes=VMEM_LIMIT)


def _layer_norm(y, g, b):
    mu = jnp.mean(y, -1, keepdims=True)
    d = y - mu
    var = jnp.mean(d * d, -1, keepdims=True)
    return d * lax.rsqrt(var + LN_EPS) * g + b


A_TN = 512


def _rope128(x, cos, sin_signed):
    lane = lax.broadcasted_iota(jnp.int32, x.shape, 1)
    first_half = (lane % HEAD_DIM) < (HEAD_DIM // 2)
    rot = jnp.where(first_half, pltpu.roll(x, LANES - HEAD_DIM // 2, 1), pltpu.roll(x, HEAD_DIM // 2, 1))
    return x * cos + rot * sin_signed


def _is_rope_col(col):
    return (P_Q <= col < P_Q + ATT_WIDTH) or (P_AK <= col < P_AK + KV_WIDTH)


def _in_proj_kernel(x_ref, g_ref, b_ref, w_ref, cos_ref, sin_ref, p_ref, h_ref, hb_ref):
    h = _layer_norm(x_ref[...], g_ref[...], b_ref[...])
    h_ref[...] = h
    hb_ref[...] = h.astype(BF16)
    for c0 in range(0, P_COLS, A_TN):
        width = min(A_TN, P_COLS - c0)
        acc = jnp.dot(hb_ref[...], w_ref[:, c0:c0 + width], preferred_element_type=F32)
        for l0 in range(0, width, LANES):
            col = c0 + l0
            part = acc[:, l0:l0 + LANES]
            if _is_rope_col(col):
                part = _rope128(part, cos_ref[...], sin_ref[...])
            p_ref[:, col:col + LANES] = part


def _in_proj(x, g, b, w_p, cos, sin, tm, table_blocks):
    n = x.shape[0]
    return pl.pallas_call(
        _in_proj_kernel,
        grid=(n // tm,),
        in_specs=[
            pl.BlockSpec((tm, D_MODEL), lambda i: (i, 0)),
            pl.BlockSpec((1, D_MODEL), lambda i: (0, 0)),
            pl.BlockSpec((1, D_MODEL), lambda i: (0, 0)),
            pl.BlockSpec((D_MODEL, P_COLS), lambda i: (0, 0), pipeline_mode=pl.Buffered(1)),
            pl.BlockSpec((tm, LANES), lambda i: (i % table_blocks, 0)),
            pl.BlockSpec((tm, LANES), lambda i: (i % table_blocks, 0)),
        ],
        out_specs=[
            pl.BlockSpec((tm, P_COLS), lambda i: (i, 0)),
            pl.BlockSpec((tm, D_MODEL), lambda i: (i, 0)),
        ],
        out_shape=[
            jax.ShapeDtypeStruct((n, P_COLS), F32),
            jax.ShapeDtypeStruct((n, D_MODEL), F32),
        ],
        scratch_shapes=[pltpu.VMEM((tm, D_MODEL), BF16)],
        compiler_params=_cparams(("arbitrary",)),
        name="in_proj",
    )(x, g, b, w_p, cos, sin)


def _attend(q, k, v, valid, sink_ref):
    r = q.shape[0]
    outs = []
    for kv in range(ATT_KV_HEADS):
        kh = k[:, kv * HEAD_DIM:(kv + 1) * HEAD_DIM].astype(BF16)
        vh = v[:, kv * HEAD_DIM:(kv + 1) * HEAD_DIM].astype(BF16)
        heads = [kv * ATT_GROUP + g for g in range(ATT_GROUP)]
        qs = jnp.concatenate([q[:, h * HEAD_DIM:(h + 1) * HEAD_DIM] for h in heads], axis=0)
        sink = jnp.concatenate([jnp.full((r, 1), sink_ref[h], F32) for h in heads], axis=0)
        s = lax.dot_general(qs.astype(BF16), kh, (((1,), (1,)), ((), ())), preferred_element_type=F32) * ATT_SCALE
        if valid is not None:
            s = jnp.where(valid, s, -jnp.inf)
        m = jnp.maximum(jnp.max(s, -1, keepdims=True), sink)
        p = jnp.exp(s - m)
        den = jnp.sum(p, -1, keepdims=True) + jnp.exp(sink - m)
        o = jnp.dot(p.astype(BF16), vh, preferred_element_type=F32) / den
        outs.extend(o[g * r:(g + 1) * r] for g in range(ATT_GROUP))
    return jnp.concatenate(outs, axis=1)


QB = 2 * CHUNK


def _attn_prompt_kernel(sink_ref, q_ref, kp_ref, kc_ref, vp_ref, vc_ref, o_ref):
    m = pl.program_id(1)
    span = 2 * QB
    k_all = jnp.concatenate([kp_ref[...], kc_ref[...]], axis=0)
    v_all = jnp.concatenate([vp_ref[...], vc_ref[...]], axis=0)
    lane = lax.broadcasted_iota(jnp.int32, (span, KV_WIDTH), 1)
    row_c = 2 + lax.broadcasted_iota(jnp.int32, (QB, span), 0) // CHUNK
    col_c = lax.broadcasted_iota(jnp.int32, (QB, span), 1) // CHUNK
    back = row_c - col_c
    valid = jnp.logical_and(jnp.logical_and(back >= 0, back <= 2), 2 * m - 2 + col_c >= 0)
    out_lane = lax.broadcasted_iota(jnp.int32, (QB, PAIR), 1)

    def dup(x, kv):
        sw = pltpu.roll(x, HEAD_DIM, 1)
        return jnp.where(lane < HEAD_DIM, x, sw) if kv == 0 else jnp.where(lane < HEAD_DIM, sw, x)

    k_st = [_stack_heads(dup(k_all, kv), HEAD_DIM).astype(BF16) for kv in range(ATT_KV_HEADS)]
    v_st = [_stack_heads(dup(v_all, kv), HEAD_DIM).astype(BF16) for kv in range(ATT_KV_HEADS)]
    pairs = range(ATT_HEADS // 2)
    kv_of = lambda j: (2 * j) // ATT_GROUP
    s = [lax.dot_general(q_ref[:, j * PAIR:(j + 1) * PAIR].astype(BF16), k_st[kv_of(j)],
                         (((1,), (1,)), ((), ())), preferred_element_type=F32) * ATT_SCALE for j in pairs]
    probs, dens = [], []
    for j in pairs:
        halves, den = [], []
        for e in range(2):
            sink = sink_ref[2 * j + e]
            sh = jnp.where(valid, s[j][:, e * span:(e + 1) * span], -jnp.inf)
            mx = jnp.maximum(jnp.max(sh, -1, keepdims=True), sink)
            ph = jnp.exp(sh - mx)
            halves.append(ph)
            den.append(jnp.sum(ph, -1, keepdims=True) + jnp.exp(sink - mx))
        probs.append(jnp.concatenate(halves, axis=1).astype(BF16))
        dens.append(jnp.where(out_lane < HEAD_DIM, den[0], den[1]))
    o = [jnp.dot(probs[j], v_st[kv_of(j)], preferred_element_type=F32) / dens[j] for j in pairs]
    o_ref[...] = jnp.concatenate(o, axis=1).astype(o_ref.dtype)


def _attn_prompt(p, sinks, batch, seq):
    nb = seq // QB
    kcol, vcol = P_AK // KV_WIDTH, P_AV // KV_WIDTH
    prev = lambda col: pl.BlockSpec((QB, KV_WIDTH), lambda b, m: (b * nb + jnp.maximum(m - 1, 0), col))
    cur = lambda col: pl.BlockSpec((QB, KV_WIDTH), lambda b, m: (b * nb + m, col))
    return pl.pallas_call(
        _attn_prompt_kernel,
        grid=(batch, nb),
        in_specs=[
            pl.BlockSpec(memory_space=pltpu.SMEM),
            pl.BlockSpec((QB, ATT_WIDTH), lambda b, m: (b * nb + m, P_Q // ATT_WIDTH)),
            prev(kcol), cur(kcol), prev(vcol), cur(vcol),
        ],
        out_specs=pl.BlockSpec((QB, ATT_WIDTH), lambda b, m: (b * nb + m, 0)),
        out_shape=jax.ShapeDtypeStruct((batch * seq, ATT_WIDTH), BF16),
        compiler_params=_cparams(("arbitrary", "arbitrary")),
        name="attn_prompt",
    )(sinks, p, p, p, p, p)


def _attn_sample_kernel(sink_ref, q_ref, kn_ref, vn_ref, kc_ref, vc_ref, o_ref, kw_ref, vw_ref):
    t = q_ref.shape[0]
    k = jnp.concatenate([kc_ref[...], kn_ref[...]], axis=0)
    v = jnp.concatenate([vc_ref[...], vn_ref[...]], axis=0)
    o_ref[...] = _attend(q_ref[...], k, v, None, sink_ref).astype(o_ref.dtype)
    kw_ref[...] = k[t:]
    vw_ref[...] = v[t:]


def _attn_sample(p, sinks, cache_k, cache_v, batch, seq):
    kcol, vcol = P_AK // KV_WIDTH, P_AV // KV_WIDTH
    return pl.pallas_call(
        _attn_sample_kernel,
        grid=(batch,),
        in_specs=[
            pl.BlockSpec(memory_space=pltpu.SMEM),
            pl.BlockSpec((seq, ATT_WIDTH), lambda b: (b, P_Q // ATT_WIDTH)),
            pl.BlockSpec((seq, KV_WIDTH), lambda b: (b, kcol)),
            pl.BlockSpec((seq, KV_WIDTH), lambda b: (b, vcol)),
            pl.BlockSpec((WINDOW, KV_WIDTH), lambda b: (b, 0)),
            pl.BlockSpec((WINDOW, KV_WIDTH), lambda b: (b, 0)),
        ],
        out_specs=[
            pl.BlockSpec((seq, ATT_WIDTH), lambda b: (b, 0)),
            pl.BlockSpec((WINDOW, KV_WIDTH), lambda b: (b, 0)),
            pl.BlockSpec((WINDOW, KV_WIDTH), lambda b: (b, 0)),
        ],
        out_shape=[
            jax.ShapeDtypeStruct((batch * seq, ATT_WIDTH), BF16),
            jax.ShapeDtypeStruct((batch * WINDOW, KV_WIDTH), F32),
            jax.ShapeDtypeStruct((batch * WINDOW, KV_WIDTH), F32),
        ],
        compiler_params=_cparams(("arbitrary",)),
        name="attn_sample",
    )(sinks, p, p, p, cache_k, cache_v)


SEG = 256
EXP_NEG_HALF = math.exp(-0.5)
PAIR = 2 * HEAD_DIM
N_PAIRS = RW_HEADS // 2


def _split2(x):
    hi = x.astype(BF16)
    lo = (x - hi.astype(F32)).astype(BF16)
    return hi, lo


def _seg_sum(x, bd):
    c = x.shape[0]
    xb = x.astype(BF16)
    stack = jnp.concatenate([xb[:, s:s + SEG] for s in range(0, RW_WIDTH, SEG)], axis=0)
    r = jnp.dot(stack, bd, preferred_element_type=F32)
    return jnp.concatenate([r[i * c:(i + 1) * c] for i in range(RW_WIDTH // SEG)], axis=1)


def _shift_rows(x, prev_row):
    row = lax.broadcasted_iota(jnp.int32, x.shape, 0)
    return jnp.where(row == 0, prev_row, pltpu.roll(x, 1, 0))


def _dotb(a, b):
    return jnp.dot(a.astype(BF16), b.astype(BF16), preferred_element_type=F32)


def _dotb_nt(a, b):
    return lax.dot_general(a.astype(BF16), b.astype(BF16), (((1,), (1,)), ((), ())), preferred_element_type=F32)


def _dot3_tn(a, b):
    ah, al = _split2(a)
    bh, bl = _split2(b)
    return lax.dot_general(jnp.concatenate([ah, al, ah], axis=0), jnp.concatenate([bh, bh, bl], axis=0),
                           (((0,), (0,)), ((), ())), preferred_element_type=F32)


def _dot2_tn(a, b):
    ah, al = _split2(a)
    bh = b.astype(BF16)
    return lax.dot_general(jnp.concatenate([ah, al], axis=0), jnp.concatenate([bh, bh], axis=0),
                           (((0,), (0,)), ((), ())), preferred_element_type=F32)


def _dot2(a, b):
    ah, al = _split2(a)
    bh = b.astype(BF16)
    return jnp.dot(jnp.concatenate([ah, al], axis=1), jnp.concatenate([bh, bh], axis=0), preferred_element_type=F32)


def _stack_heads(x, half):
    lane = lax.broadcasted_iota(jnp.int32, x.shape, 1)
    first = (lane % (2 * half)) < half
    zero = jnp.zeros_like(x)
    return jnp.concatenate([jnp.where(first, x, zero), jnp.where(first, zero, x)], axis=0)


def _stack_heads_bf16(x, half):
    return _stack_heads(x.astype(BF16), half)


def _rwkv_pairs(ab, rb, bb, kb, v, gam_c, s_bd, n_double):
    c = ab[0].shape[0]
    pairs = range(len(ab))
    t_idx = lax.broadcasted_iota(jnp.int32, (c, 2 * c), 0)
    s_idx = lax.broadcasted_iota(jnp.int32, (c, 2 * c), 1) % c
    strict, incl = s_idx < t_idx, s_idx <= t_idx
    g = [_dotb_nt(jnp.concatenate([ab[p], rb[p]], axis=0),
                  jnp.concatenate([_stack_heads_bf16(bb[p], HEAD_DIM), _stack_heads_bf16(kb[p], HEAD_DIM)], axis=0))
         for p in pairs]
    l_ak = [jnp.where(strict, g[p][:c, 2 * c:], 0.0) for p in pairs]
    m_rbk = [jnp.concatenate([jnp.where(incl, g[p][c:, :2 * c], 0.0), jnp.where(incl, g[p][c:, 2 * c:], 0.0)], axis=1)
             for p in pairs]
    lp = [jnp.where(strict, g[p][:c, :2 * c], 0.0) for p in pairs]
    v_st = [_stack_heads_bf16(v[p], HEAD_DIM) for p in pairs]
    ya = list(ab)
    yu = [_dotb(l_ak[p], v_st[p]) for p in pairs]
    for i in range(n_double):
        y_st = [_stack_heads_bf16(jnp.concatenate([ya[p], yu[p]], axis=1), HEAD_DIM) for p in pairs]
        if i + 1 < n_double:
            z = [_dotb(lp[p], jnp.concatenate([y_st[p], _stack_heads_bf16(lp[p], c)], axis=1)) for p in pairs]
            lp = [z[p][:, 2 * PAIR:] for p in pairs]
        else:
            z = [_dotb(lp[p], y_st[p]) for p in pairs]
        ya = [ya[p] + z[p][:, :PAIR] for p in pairs]
        yu = [yu[p] + z[p][:, PAIR:2 * PAIR] for p in pairs]
    z = []
    for p in pairs:
        top = _stack_heads_bf16(jnp.concatenate([ya[p], yu[p]], axis=1), HEAD_DIM)
        bot = jnp.concatenate([jnp.zeros_like(v_st[p]), v_st[p]], axis=1)
        z.append(_dotb(m_rbk[p], jnp.concatenate([top, bot], axis=0)))
    o = [_dotb_nt(rb[p] + z[p][:, :PAIR], s_bd[p]) + z[p][:, PAIR:] for p in pairs]
    row_h = lax.broadcasted_iota(jnp.int32, (PAIR, PAIR), 0) // HEAD_DIM
    col_h = lax.broadcasted_iota(jnp.int32, (PAIR, PAIR), 1) // HEAD_DIM
    same = row_h == col_h
    q_t = [jnp.where(same, _dot3_tn(ya[p], bb[p]), 0.0) for p in pairs]
    d_t = [jnp.where(same, _dot2_tn(jnp.concatenate([yu[p], v[p]], axis=0),
                                    jnp.concatenate([bb[p], kb[p]], axis=0)), 0.0) for p in pairs]
    s_new = [(s_bd[p] + _dot2(s_bd[p], q_t[p]) + d_t[p]) * gam_c[p] for p in pairs]
    return o, s_new


def _rwkv_kernel(xr_ref, xk_ref, xv_ref, xl_ref, sr_ref, sk_ref, sv_ref, sl_ref,
                 mur_ref, muk_ref, muv_ref, mul_ref, w0_ref, a0_ref, kk_ref, ka_ref, rk_ref,
                 lg_ref, lb_ref, w2_ref, a2_ref, g2_ref, bd_ref, st0_ref,
                 out_ref, st_ref,
                 s_scr, pr_scr, pk_scr, pv_scr, pl_scr):
    c = pl.program_id(1)
    nc = pl.num_programs(1)
    nb, C = xr_ref.shape[0], xr_ref.shape[1]
    n_double = int(math.log2(C))

    @pl.when(c == 0)
    def _():
        s_scr[...] = jnp.zeros_like(s_scr)
        for b in range(nb):
            for h in range(RW_HEADS):
                off = (h % 2) * HEAD_DIM
                s_scr[b * N_PAIRS + h // 2, off:off + HEAD_DIM, off:off + HEAD_DIM] = st0_ref[b, h]
            pr_scr[b, 0:1, :] = sr_ref[b]
            pk_scr[b, 0:1, :] = sk_ref[b]
            pv_scr[b, 0:1, :] = sv_ref[b]
            pl_scr[b, 0:1, :] = sl_ref[b]

    def shifted(x_ref, p_scr, mu_ref):
        parts = []
        for b in range(nb):
            x = x_ref[b]
            prev = _shift_rows(x, p_scr[b, 0:1, :])
            p_scr[b, 0:1, :] = x[C - 1:C, :]
            parts.append(x + (prev - x) * mu_ref[...])
        return jnp.concatenate(parts, axis=0)

    r = shifted(xr_ref, pr_scr, mur_ref)
    kr = shifted(xk_ref, pk_scr, muk_ref)
    v = shifted(xv_ref, pv_scr, muv_ref)
    xl = shifted(xl_ref, pl_scr, mul_ref)
    wd = xl[:, 0:DECAY_LORA]
    ad = xl[:, DECAY_LORA:DECAY_LORA + AAA_LORA]
    gd = xl[:, DECAY_LORA + AAA_LORA:DECAY_LORA + AAA_LORA + GATE_LORA]

    lw = -EXP_NEG_HALF * jax.nn.sigmoid(w0_ref[...] + _dotb(jnp.tanh(wd), w2_ref[...]))
    a = jax.nn.sigmoid(a0_ref[...] + _dotb(ad, a2_ref[...]))
    g = _dotb(jax.nn.sigmoid(gd), g2_ref[...])
    bd = bd_ref[...]
    kk = kr * kk_ref[...]
    kk = kk * lax.rsqrt(jnp.maximum(_seg_sum(kk * kk, bd), 1e-24))
    k2 = kr * (1.0 + (a - 1.0) * ka_ref[...])

    rows = nb * C
    ti = lax.broadcasted_iota(jnp.int32, (rows, rows), 0)
    si = lax.broadcasted_iota(jnp.int32, (rows, rows), 1)
    tri = jnp.logical_and(si <= ti, si // C == ti // C).astype(BF16)
    cs2 = jnp.dot(tri, jnp.concatenate(_split2(lw), axis=1), preferred_element_type=F32)
    cs = cs2[:, :RW_WIDTH] + cs2[:, RW_WIDTH:]
    gam = jnp.exp(cs)
    ginv = jnp.exp(-cs)
    ab = -kk * jnp.exp(cs - lw)
    rb = r * gam
    bb = kk * a * ginv
    kb = k2 * ginv

    def per_pair(x, r0, r1):
        return [x[b * C + r0:b * C + r1, p * PAIR:(p + 1) * PAIR] for b in range(nb) for p in range(N_PAIRS)]

    chains = nb * N_PAIRS
    outs, s_new = _rwkv_pairs(per_pair(ab, 0, C), per_pair(rb, 0, C), per_pair(bb, 0, C), per_pair(kb, 0, C),
                              per_pair(v, 0, C), per_pair(gam, C - 1, C), [s_scr[q] for q in range(chains)], n_double)
    for q in range(chains):
        s_scr[q] = s_new[q]
    o = jnp.concatenate([jnp.concatenate(outs[b * N_PAIRS:(b + 1) * N_PAIRS], axis=1) for b in range(nb)], axis=0)

    mo = _seg_sum(o, bd) * (1.0 / HEAD_DIM)
    d = o - mo
    vo = _seg_sum(d * d, bd) * (1.0 / HEAD_DIM)
    on = d * lax.rsqrt(vo + LNX_EPS) * lg_ref[...] + lb_ref[...]
    bonus = _seg_sum(r * k2 * rk_ref[...], bd) * v
    res = ((on + bonus) * g).astype(out_ref.dtype)
    for b in range(nb):
        out_ref[b] = res[b * C:(b + 1) * C, :]

    @pl.when(c == nc - 1)
    def _():
        for b in range(nb):
            for h in range(RW_HEADS):
                off = (h % 2) * HEAD_DIM
                st_ref[b, h] = s_scr[b * N_PAIRS + h // 2, off:off + HEAD_DIM, off:off + HEAD_DIM]


def _rwkv(p3, shift_p, state0, prm, chunk, nb):
    batch, seq = p3.shape[0], p3.shape[1]
    nc = seq // chunk
    row = lambda w: pl.BlockSpec((1, w), lambda g, c: (0, 0))
    xspec = lambda w, col: pl.BlockSpec((nb, chunk, w), lambda g, c: (g, c, col))
    sspec = lambda w: pl.BlockSpec((nb, 1, w), lambda g, c: (g, 0, 0))
    full = lambda a: pl.BlockSpec(a.shape, lambda g, c: (0,) * a.ndim)
    stspec = pl.BlockSpec((nb, RW_HEADS, HEAD_DIM, HEAD_DIM), lambda g, c: (g, 0, 0, 0))
    sr, sk, sv, sl = shift_p
    return pl.pallas_call(
        _rwkv_kernel,
        grid=(batch // nb, nc),
        in_specs=[
            xspec(RW_WIDTH, P_R // RW_WIDTH), xspec(RW_WIDTH, P_K // RW_WIDTH), xspec(RW_WIDTH, P_V // RW_WIDTH),
            xspec(LORA_W, P_LORA // LORA_W),
            sspec(RW_WIDTH), sspec(RW_WIDTH), sspec(RW_WIDTH), sspec(LORA_W),
            row(RW_WIDTH), row(RW_WIDTH), row(RW_WIDTH), row(LORA_W),
            row(RW_WIDTH), row(RW_WIDTH), row(RW_WIDTH), row(RW_WIDTH), row(RW_WIDTH),
            row(RW_WIDTH), row(RW_WIDTH),
            full(prm["w2"]), full(prm["a2"]), full(prm["g2"]), full(prm["bd"]),
            stspec,
        ],
        out_specs=[
            pl.BlockSpec((nb, chunk, RW_WIDTH), lambda g, c: (g, c, 0)),
            stspec,
        ],
        out_shape=[
            jax.ShapeDtypeStruct((batch, seq, RW_WIDTH), BF16),
            jax.ShapeDtypeStruct((batch, RW_HEADS, HEAD_DIM, HEAD_DIM), F32),
        ],
        scratch_shapes=[
            pltpu.VMEM((nb * N_PAIRS, PAIR, PAIR), F32),
            pltpu.VMEM((nb, SUBLANES, RW_WIDTH), F32), pltpu.VMEM((nb, SUBLANES, RW_WIDTH), F32),
            pltpu.VMEM((nb, SUBLANES, RW_WIDTH), F32), pltpu.VMEM((nb, SUBLANES, LORA_W), F32),
        ],
        compiler_params=_cparams(("arbitrary", "arbitrary")),
        name="rwkv",
    )(p3, p3, p3, p3, sr, sk, sv, sl,
      prm["mu_r"], prm["mu_k"], prm["mu_v"], prm["mu_l"], prm["w0"], prm["a0"], prm["k_k"], prm["k_a"], prm["r_k"],
      prm["lnx_g"], prm["lnx_b"], prm["w2"], prm["a2"], prm["g2"], prm["bd"], state0)


def _out_proj_kernel(att_ref, rw_ref, wa_ref, wb_ref, h_ref, g_ref, b_ref, h1_ref, h1b_ref):
    m = jnp.dot(att_ref[...], wa_ref[...], preferred_element_type=F32)
    m = m + jnp.dot(rw_ref[...], wb_ref[...], preferred_element_type=F32)
    y = _layer_norm(ALPHA * h_ref[...] + m, g_ref[...], b_ref[...])
    h1_ref[...] = y
    h1b_ref[...] = y.astype(BF16)


def _out_proj(att, rw, w_out, h, g, b, tm):
    n = att.shape[0]
    return pl.pallas_call(
        _out_proj_kernel,
        grid=(n // tm,),
        in_specs=[
            pl.BlockSpec((tm, ATT_WIDTH), lambda i: (i, 0)),
            pl.BlockSpec((tm, RW_WIDTH), lambda i: (i, 0)),
            pl.BlockSpec((ATT_WIDTH, D_MODEL), lambda i: (0, 0)),
            pl.BlockSpec((RW_WIDTH, D_MODEL), lambda i: (1, 0)),
            pl.BlockSpec((tm, D_MODEL), lambda i: (i, 0)),
            pl.BlockSpec((1, D_MODEL), lambda i: (0, 0)),
            pl.BlockSpec((1, D_MODEL), lambda i: (0, 0)),
        ],
        out_specs=[
            pl.BlockSpec((tm, D_MODEL), lambda i: (i, 0)),
            pl.BlockSpec((tm, D_MODEL), lambda i: (i, 0)),
        ],
        out_shape=[
            jax.ShapeDtypeStruct((n, D_MODEL), F32),
            jax.ShapeDtypeStruct((n, D_MODEL), BF16),
        ],
        compiler_params=_cparams(("arbitrary",)),
        name="out_proj",
    )(att, rw, w_out, w_out, h, g, b)


E_TN = 512
E_NCB = D_FF // E_TN
E_SUB = 512
SQRT_HALF = math.sqrt(0.5)


def _ffn_up_kernel(bps, nseq, h_ref, wg_ref, wv_ref, cwg_ref, cwv_ref, cbg_ref, cbv_ref, sg_ref, sv_ref,
                   act_ref, og_ref, ov_ref, w_scr, cg_scr, cv_scr):
    i = pl.program_id(1)
    tm = h_ref.shape[0]
    sub = min(tm, E_SUB)
    n_sub = tm // sub
    t_seq = tm // nseq
    row = lax.broadcasted_iota(jnp.int32, (sub, E_TN), 0)

    @pl.when(i == 0)
    def _():
        w_scr[:, :E_TN] = wg_ref[...].astype(BF16)
        w_scr[:, E_TN:] = wv_ref[...].astype(BF16)

    def products(r0):
        u = jnp.dot(h_ref[r0:r0 + sub, :], w_scr[...], preferred_element_type=F32)
        return u[:, :E_TN], u[:, E_TN:]

    def conv(u, prevs, cw_ref, cb_ref):
        sh1, sh2 = pltpu.roll(u, 1, 0), pltpu.roll(u, 2, 0)
        for r, prev in prevs:
            p2, p1 = prev[0:1, :], prev[1:2, :]
            sh1 = jnp.where(row == r, p1, sh1)
            sh2 = jnp.where(row == r, p2, jnp.where(row == r + 1, p1, sh2))
        cw = cw_ref[...]
        c = cb_ref[...] + sh2 * cw[0:1, :]
        c = c + sh1 * cw[1:2, :]
        return c + u * cw[2:3, :]

    def epilogue(r0, ug, uv, prev_g, prev_v):
        gate = conv(ug, prev_g, cwg_ref, cbg_ref)
        val = conv(uv, prev_v, cwv_ref, cbv_ref)
        gelu = 0.5 * gate * (1.0 + lax.erf(gate * SQRT_HALF))
        act_ref[r0:r0 + sub, :] = (gelu * val).astype(act_ref.dtype)

    if nseq > 1:
        ug, uv = products(0)
        epilogue(0, ug, uv, [(s * t_seq, sg_ref[s]) for s in range(nseq)],
                 [(s * t_seq, sv_ref[s]) for s in range(nseq)])
        for s in range(nseq):
            og_ref[s] = ug[(s + 1) * t_seq - SUBLANES:(s + 1) * t_seq, :]
            ov_ref[s] = uv[(s + 1) * t_seq - SUBLANES:(s + 1) * t_seq, :]
    else:
        first = (i % bps) == 0
        prev_g = jnp.where(first, sg_ref[0], cg_scr[SUBLANES - 2:SUBLANES, :])
        prev_v = jnp.where(first, sv_ref[0], cv_scr[SUBLANES - 2:SUBLANES, :])
        nxt = products(0)
        for k in range(n_sub):
            ug, uv = nxt
            if k + 1 < n_sub:
                nxt = products((k + 1) * sub)
            epilogue(k * sub, ug, uv, [(0, prev_g)], [(0, prev_v)])
            prev_g, prev_v = ug[sub - 2:sub, :], uv[sub - 2:sub, :]
        cg_scr[...] = ug[sub - SUBLANES:sub, :]
        cv_scr[...] = uv[sub - SUBLANES:sub, :]
        og_ref[0] = ug[sub - SUBLANES:sub, :]
        ov_ref[0] = uv[sub - SUBLANES:sub, :]


def _ffn_up(h1b, w_up, conv_w, conv_b, conv_prev, batch, seq, tm):
    n = h1b.shape[0]
    nseq = max(tm // seq, 1)
    bps = max(seq // tm, 1)
    nrb = n // tm
    return pl.pallas_call(
        functools.partial(_ffn_up_kernel, bps, nseq),
        grid=(E_NCB, nrb),
        in_specs=[
            pl.BlockSpec((tm, D_MODEL), lambda j, i: (i, 0)),
            pl.BlockSpec((D_MODEL, E_TN), lambda j, i: (0, j)),
            pl.BlockSpec((D_MODEL, E_TN), lambda j, i: (0, j + E_NCB)),
            pl.BlockSpec((CONV_W, E_TN), lambda j, i: (0, j)),
            pl.BlockSpec((CONV_W, E_TN), lambda j, i: (0, j + E_NCB)),
            pl.BlockSpec((1, E_TN), lambda j, i: (0, j)),
            pl.BlockSpec((1, E_TN), lambda j, i: (0, j + E_NCB)),
            pl.BlockSpec((nseq, CONV_W - 1, E_TN), lambda j, i: (i // bps, 0, j)),
            pl.BlockSpec((nseq, CONV_W - 1, E_TN), lambda j, i: (i // bps, 0, j + E_NCB)),
        ],
        out_specs=[
            pl.BlockSpec((tm, E_TN), lambda j, i: (i, j)),
            pl.BlockSpec((nseq, SUBLANES, E_TN), lambda j, i: (i // bps, 0, j)),
            pl.BlockSpec((nseq, SUBLANES, E_TN), lambda j, i: (i // bps, 0, j)),
        ],
        out_shape=[
            jax.ShapeDtypeStruct((n, D_FF), BF16),
            jax.ShapeDtypeStruct((batch, SUBLANES, D_FF), F32),
            jax.ShapeDtypeStruct((batch, SUBLANES, D_FF), F32),
        ],
        scratch_shapes=[pltpu.VMEM((D_MODEL, 2 * E_TN), BF16),
                        pltpu.VMEM((SUBLANES, E_TN), F32), pltpu.VMEM((SUBLANES, E_TN), F32)],
        compiler_params=_cparams(("arbitrary", "arbitrary")),
        name="ffn_up",
    )(h1b, w_up, w_up, conv_w, conv_w, conv_b, conv_b, conv_prev, conv_prev)


F_TN = 512


def _ffn_down_kernel(act_ref, w_ref, h1_ref, g_ref, b_ref, y_ref):
    for c0 in range(0, D_MODEL, F_TN):
        f = jnp.dot(act_ref[...], w_ref[:, c0:c0 + F_TN], preferred_element_type=F32)
        y_ref[:, c0:c0 + F_TN] = ALPHA * h1_ref[:, c0:c0 + F_TN] + f
    y_ref[...] = _layer_norm(y_ref[...], g_ref[...], b_ref[...])


def _ffn_down(act, w_down, h1, g, b, tm):
    n = act.shape[0]
    return pl.pallas_call(
        _ffn_down_kernel,
        grid=(n // tm,),
        in_specs=[
            pl.BlockSpec((tm, D_FF), lambda i: (i, 0)),
            pl.BlockSpec((D_FF, D_MODEL), lambda i: (0, 0), pipeline_mode=pl.Buffered(1)),
            pl.BlockSpec((tm, D_MODEL), lambda i: (i, 0)),
            pl.BlockSpec((1, D_MODEL), lambda i: (0, 0)),
            pl.BlockSpec((1, D_MODEL), lambda i: (0, 0)),
        ],
        out_specs=pl.BlockSpec((tm, D_MODEL), lambda i: (i, 0)),
        out_shape=jax.ShapeDtypeStruct((n, D_MODEL), F32),
        compiler_params=_cparams(("arbitrary",)),
        name="ffn_down",
    )(act, w_down, h1, g, b)


def _rope_tables(pos, reps):
    half = HEAD_DIM // 2
    inv = ROPE_THETA ** (-jnp.arange(half, dtype=F32) / half)
    ang = pos.astype(F32)[:, None] * inv[None, :]
    cos, sin = jnp.cos(ang), jnp.sin(ang)
    cos128 = jnp.concatenate([cos, cos, cos, cos], axis=1)
    sin128 = jnp.concatenate([-sin, sin, -sin, sin], axis=1)
    return jnp.tile(cos128, (reps, 1)), jnp.tile(sin128, (reps, 1))


def _permute_rw(a):
    o_wd, o_k, o_v = RW_WIDTH, RW_WIDTH + DECAY_LORA, 2 * RW_WIDTH + DECAY_LORA
    o_ad = 3 * RW_WIDTH + DECAY_LORA
    o_gd = o_ad + AAA_LORA
    pad = jnp.zeros(a.shape[:-1] + (LORA_W - DECAY_LORA - AAA_LORA - GATE_LORA,), a.dtype)
    return jnp.concatenate([a[..., :RW_WIDTH], a[..., o_k:o_k + RW_WIDTH], a[..., o_v:o_v + RW_WIDTH],
                            a[..., o_wd:o_wd + DECAY_LORA], a[..., o_ad:o_ad + AAA_LORA],
                            a[..., o_gd:o_gd + GATE_LORA], pad], axis=-1)


def _split_rw(a):
    return (a[..., :RW_WIDTH], a[..., RW_WIDTH:2 * RW_WIDTH], a[..., 2 * RW_WIDTH:3 * RW_WIDTH],
            a[..., 3 * RW_WIDTH:])


def _unpermute_shift(p_row):
    lo = P_LORA
    return jnp.concatenate([
        p_row[..., P_R:P_R + RW_WIDTH], p_row[..., lo:lo + DECAY_LORA],
        p_row[..., P_K:P_K + RW_WIDTH], p_row[..., P_V:P_V + RW_WIDTH],
        p_row[..., lo + DECAY_LORA:lo + DECAY_LORA + AAA_LORA],
        p_row[..., lo + DECAY_LORA + AAA_LORA:lo + DECAY_LORA + AAA_LORA + GATE_LORA]], axis=-1)


def _run(x, pos_tables, table_blocks, attn_fn, state_wkv, state_shift, state_conv, wts, tiles):
    batch, seq = x.shape[0], x.shape[1]
    n = batch * seq
    cos, sin = pos_tables
    p, h = _in_proj(x.reshape(n, D_MODEL), wts["ln_in_g"], wts["ln_in_b"], wts["w_in_p"], cos, sin,
                    tiles["a_tm"], table_blocks)
    att, k_win, v_win = attn_fn(p)
    shift_p = tuple(s for s in _split_rw(_permute_rw(state_shift)))
    p3 = p.reshape(batch, seq, P_COLS)
    rw, wkv = _rwkv(p3, shift_p, state_wkv, wts["rw"], tiles["chunk"], tiles["c_nb"])
    h1, h1b = _out_proj(att, rw.reshape(n, RW_WIDTH), wts["w_out"], h, wts["ln1_g"], wts["ln1_b"], tiles["d_tm"])
    act, cg, cv = _ffn_up(h1b, wts["w_up"], wts["conv_w"], wts["conv_b"], state_conv, batch, seq, tiles["e_tm"])
    y = _ffn_down(act, wts["w_down"], h1, wts["ln2_g"], wts["ln2_b"], tiles["f_tm"])
    shift_out = _unpermute_shift(p3[:, seq - 1:seq, :])
    conv_out = jnp.concatenate([cg[:, SUBLANES - 2:, :], cv[:, SUBLANES - 2:, :]], axis=-1)
    if k_win is None:
        k_win = p3[:, seq - WINDOW:, P_AK:P_AK + KV_WIDTH]
        v_win = p3[:, seq - WINDOW:, P_AV:P_AV + KV_WIDTH]
    kv_shape = (1, batch, WINDOW, ATT_KV_HEADS, HEAD_DIM)
    return (y.reshape(batch, seq, D_MODEL), k_win.reshape(kv_shape), v_win.reshape(kv_shape),
            wkv[None], shift_out[None], conv_out[None])


def kernel(x_prompt, x_sample, cache_k, cache_v, state_wkv, state_shift, state_ffn_conv, ln_in_g, ln_in_b, w_in, attn_sinks, rw_mu, rw_w0, rw_w2, rw_a0, rw_a2, rw_g2, rw_k_k, rw_k_a, rw_r_k, rw_lnx_g, rw_lnx_b, w_out, ln1_g, ln1_b, ffn_w_up, ffn_conv_w, ffn_conv_b, ffn_w_down, ln2_g, ln2_b):
    l = 0
    w = w_in[l]
    w_att, w_rw = w[:, :ATT_COLS], _permute_rw(w[:, ATT_COLS:])
    w_in_p = jnp.concatenate([w_rw[:, :3 * RW_WIDTH], w_att[:, :ATT_WIDTH], w_rw[:, 3 * RW_WIDTH:],
                              w_att[:, ATT_WIDTH:]], axis=1).astype(BF16)
    mu_r, mu_k, mu_v, mu_l = _split_rw(_permute_rw(rw_mu[l])[None])
    seg = lax.broadcasted_iota(jnp.int32, (SEG, SEG), 0) // HEAD_DIM
    bd = (seg == seg.T).astype(BF16)
    row = lambda a: a.reshape(1, -1)
    wts = {
        "ln_in_g": row(ln_in_g), "ln_in_b": row(ln_in_b), "w_in_p": w_in_p,
        "rw": {"mu_r": mu_r, "mu_k": mu_k, "mu_v": mu_v, "mu_l": mu_l,
               "w0": row(rw_w0[l]), "a0": row(rw_a0[l]), "k_k": row(rw_k_k[l]), "k_a": row(rw_k_a[l]),
               "r_k": row(rw_r_k[l]), "lnx_g": row(rw_lnx_g[l]), "lnx_b": row(rw_lnx_b[l]),
               "w2": rw_w2[l].astype(BF16), "a2": rw_a2[l].astype(BF16), "g2": rw_g2[l].astype(BF16), "bd": bd},
        "w_out": w_out[l].astype(BF16), "ln1_g": row(ln1_g[l]), "ln1_b": row(ln1_b[l]),
        "w_up": ffn_w_up[l], "conv_w": ffn_conv_w[l], "conv_b": row(ffn_conv_b[l]),
        "w_down": ffn_w_down[l].astype(BF16), "ln2_g": row(ln2_g[l]), "ln2_b": row(ln2_b[l]),
    }
    sinks = attn_sinks[l]

    bp, tp = x_prompt.shape[0], x_prompt.shape[1]
    p_tiles = {"a_tm": 256, "chunk": CHUNK, "c_nb": 4, "d_tm": 256, "e_tm": 1024, "f_tm": 256}
    p_tables = _rope_tables(jnp.arange(tp, dtype=jnp.int32), 1)
    zeros = lambda *s: jnp.zeros(s, F32)
    y_p, p_k, p_v, p_wkv, p_shift, p_conv = _run(
        x_prompt, p_tables, tp // p_tiles["a_tm"],
        lambda p: (_attn_prompt(p, sinks, bp, tp), None, None),
        zeros(bp, RW_HEADS, HEAD_DIM, HEAD_DIM), zeros(bp, 1, RW_COLS), zeros(bp, CONV_W - 1, 2 * D_FF),
        wts, p_tiles)

    bs, ts = x_sample.shape[0], x_sample.shape[1]
    s_tiles = {"a_tm": bs * ts, "chunk": ts, "c_nb": 4, "d_tm": bs * ts, "e_tm": bs * ts, "f_tm": bs * ts}
    s_tables = _rope_tables(PAST_LEN + jnp.arange(ts, dtype=jnp.int32), bs)
    ck = cache_k[l].reshape(bs * WINDOW, KV_WIDTH)
    cv = cache_v[l].reshape(bs * WINDOW, KV_WIDTH)
    y_s, s_k, s_v, s_wkv, s_shift, s_conv = _run(
        x_sample, s_tables, 1,
        lambda p: _attn_sample(p, sinks, ck, cv, bs, ts),
        state_wkv[l], state_shift[l], state_ffn_conv[l], wts, s_tiles)

    return (y_p, y_s, p_k, p_v, p_wkv, p_shift, p_conv, s_k, s_v, s_wkv, s_shift, s_conv)
```

```python
import functools
import math

import jax
import jax.numpy as jnp
from jax import lax
from jax.experimental import pallas as pl
from jax.experimental.pallas import tpu as pltpu

D_MODEL = 2048
HEAD_DIM = 64
ATT_HEADS = 16
ATT_KV_HEADS = 2
ATT_GROUP = ATT_HEADS // ATT_KV_HEADS
ATT_WIDTH = ATT_HEADS * HEAD_DIM
KV_WIDTH = ATT_KV_HEADS * HEAD_DIM
CHUNK = 64
WINDOW = 128
ROPE_THETA = 10000.0
ATT_SCALE = HEAD_DIM ** -0.5
RW_HEADS = 16
RW_WIDTH = RW_HEADS * HEAD_DIM
DECAY_LORA = 64
AAA_LORA = 64
GATE_LORA = 160
LNX_EPS = 64e-5
ATT_COLS = ATT_WIDTH + 2 * KV_WIDTH
RW_COLS = 3 * RW_WIDTH + DECAY_LORA + AAA_LORA + GATE_LORA
D_FF = 5632
CONV_W = 3
LN_EPS = 1e-5
DEPTH = 1
ALPHA = (2 * DEPTH) ** 0.25
PAST_LEN = 1024

LORA_W = 512
P_R, P_K, P_V, P_Q = 0, 1024, 2048, 3072
P_LORA = 4096
P_AK = P_LORA + LORA_W
P_AV = P_AK + KV_WIDTH
P_COLS = P_AV + KV_WIDTH
LANES = 128
SUBLANES = 8
VMEM_LIMIT = 56 * 1024 * 1024

F32 = jnp.float32
BF16 = jnp.bfloat16


def _cparams(sem):
    return pltpu.CompilerParams(dimension_semantics=sem, vmem_limit_bytes=VMEM_LIMIT)


def _layer_norm(y, g, b):
    mu = jnp.mean(y, -1, keepdims=True)
    d = y - mu
    var = jnp.mean(d * d, -1, keepdims=True)
    return d * lax.rsqrt(var + LN_EPS) * g + b


A_TN = 512


def _rope128(x, cos, sin_signed):
    lane = lax.broadcasted_iota(jnp.int32, x.shape, 1)
    first_half = (lane % HEAD_DIM) < (HEAD_DIM // 2)
    rot = jnp.where(first_half, pltpu.roll(x, LANES - HEAD_DIM // 2, 1), pltpu.roll(x, HEAD_DIM // 2, 1))
    return x * cos + rot * sin_signed


def _is_rope_col(col):
    return (P_Q <= col < P_Q + ATT_WIDTH) or (P_AK <= col < P_AK + KV_WIDTH)


O_R = ATT_COLS
O_WD = O_R + RW_WIDTH
O_K = O_WD + DECAY_LORA
O_V = O_K + RW_WIDTH
O_AD = O_V + RW_WIDTH
O_GD = O_AD + AAA_LORA
IN_COLS = O_GD + GATE_LORA


def _in_proj_kernel(x_ref, g_ref, b_ref, w_ref, cos_ref, sin_ref, p_ref, h_ref, tail_ref, hb_ref):
    h = _layer_norm(x_ref[...], g_ref[...], b_ref[...])
    h_ref[...] = h
    hb_ref[...] = h.astype(BF16)
    nseq = tail_ref.shape[0]
    t_seq = x_ref.shape[0] // nseq

    def product(c0, width):
        acc = jnp.dot(hb_ref[...], w_ref[:, c0:c0 + width], preferred_element_type=F32)
        if c0 >= O_R:
            for s in range(nseq):
                tail_ref[s, :, c0 - O_R:c0 - O_R + width] = acc[(s + 1) * t_seq - SUBLANES:(s + 1) * t_seq, :]
        return acc

    for src, width, dst in ((0, ATT_WIDTH, P_Q), (ATT_WIDTH, 2 * KV_WIDTH, P_AK), (O_R, RW_WIDTH, P_R)):
        for c0 in range(0, width, A_TN):
            acc = product(src + c0, min(A_TN, width - c0))
            for l0 in range(0, acc.shape[1], LANES):
                col = dst + c0 + l0
                part = acc[:, l0:l0 + LANES]
                if _is_rope_col(col):
                    part = _rope128(part, cos_ref[...], sin_ref[...])
                p_ref[:, col:col + LANES] = part

    low = lax.broadcasted_iota(jnp.int32, (x_ref.shape[0], LANES), 1) < HEAD_DIM
    span = O_GD - O_WD
    rolled, first_piece = [], None
    for c0 in range(0, span, A_TN):
        acc = product(O_WD + c0, min(A_TN, span - c0))
        for l0 in range(0, acc.shape[1], LANES):
            piece = acc[:, l0:l0 + LANES]
            j = (c0 + l0) // LANES
            if j == 0:
                first_piece = piece
            rolled.append(pltpu.roll(piece, HEAD_DIM, 1))
            if j >= 1:
                m = j - 1
                dst = P_K + m * LANES if m < RW_WIDTH // LANES else P_V + (m - RW_WIDTH // LANES) * LANES
                p_ref[:, dst:dst + LANES] = jnp.where(low, rolled[m], rolled[m + 1])
            if j == span // LANES - 1:
                p_ref[:, P_LORA:P_LORA + LANES] = jnp.where(low, first_piece, piece)
    gd0 = P_LORA + DECAY_LORA + AAA_LORA
    p_ref[:, gd0:gd0 + GATE_LORA] = product(O_GD, GATE_LORA)
    p_ref[:, gd0 + GATE_LORA:P_LORA + LORA_W] = jnp.zeros((x_ref.shape[0], LORA_W - DECAY_LORA - AAA_LORA - GATE_LORA), F32)


def _in_proj(x, g, b, w_p, cos, sin, tm, table_blocks, seq):
    n = x.shape[0]
    nseq = max(tm // seq, 1)
    return pl.pallas_call(
        _in_proj_kernel,
        grid=(n // tm,),
        in_specs=[
            pl.BlockSpec((tm, D_MODEL), lambda i: (i, 0)),
            pl.BlockSpec((1, D_MODEL), lambda i: (0, 0)),
            pl.BlockSpec((1, D_MODEL), lambda i: (0, 0)),
            pl.BlockSpec((D_MODEL, IN_COLS), lambda i: (0, 0), pipeline_mode=pl.Buffered(1)),
            pl.BlockSpec((tm, LANES), lambda i: (i % table_blocks, 0)),
            pl.BlockSpec((tm, LANES), lambda i: (i % table_blocks, 0)),
        ],
        out_specs=[
            pl.BlockSpec((tm, P_COLS), lambda i: (i, 0)),
            pl.BlockSpec((tm, D_MODEL), lambda i: (i, 0)),
            pl.BlockSpec((nseq, SUBLANES, RW_COLS), lambda i: (i, 0, 0)),
        ],
        out_shape=[
            jax.ShapeDtypeStruct((n, P_COLS), F32),
            jax.ShapeDtypeStruct((n, D_MODEL), F32),
            jax.ShapeDtypeStruct((n // tm * nseq, SUBLANES, RW_COLS), F32),
        ],
        scratch_shapes=[pltpu.VMEM((tm, D_MODEL), BF16)],
        compiler_params=_cparams(("arbitrary",)),
        name="in_proj",
    )(x, g, b, w_p, cos, sin)


def _attend(q, k, v, valid, sink_ref):
    r = q.shape[0]
    outs = []
    for kv in range(ATT_KV_HEADS):
        kh = k[:, kv * HEAD_DIM:(kv + 1) * HEAD_DIM].astype(BF16)
        vh = v[:, kv * HEAD_DIM:(kv + 1) * HEAD_DIM].astype(BF16)
        heads = [kv * ATT_GROUP + g for g in range(ATT_GROUP)]
        qs = jnp.concatenate([q[:, h * HEAD_DIM:(h + 1) * HEAD_DIM] for h in heads], axis=0)
        sink = jnp.concatenate([jnp.full((r, 1), sink_ref[h], F32) for h in heads], axis=0)
        s = lax.dot_general(qs.astype(BF16), kh, (((1,), (1,)), ((), ())), preferred_element_type=F32) * ATT_SCALE
        if valid is not None:
            s = jnp.where(valid, s, -jnp.inf)
        m = jnp.maximum(jnp.max(s, -1, keepdims=True), sink)
        p = jnp.exp(s - m)
        den = jnp.sum(p, -1, keepdims=True) + jnp.exp(sink - m)
        o = jnp.dot(p.astype(BF16), vh, preferred_element_type=F32) / den
        outs.extend(o[g * r:(g + 1) * r] for g in range(ATT_GROUP))
    return jnp.concatenate(outs, axis=1)


QB = 2 * CHUNK


def _attn_prompt_kernel(sink_ref, q_ref, kp_ref, kc_ref, vp_ref, vc_ref, o_ref):
    m = pl.program_id(1)
    span = 2 * QB
    k_all = jnp.concatenate([kp_ref[...], kc_ref[...]], axis=0)
    v_all = jnp.concatenate([vp_ref[...], vc_ref[...]], axis=0)
    lane = lax.broadcasted_iota(jnp.int32, (span, KV_WIDTH), 1)
    row_c = 2 + lax.broadcasted_iota(jnp.int32, (QB, span), 0) // CHUNK
    col_c = lax.broadcasted_iota(jnp.int32, (QB, span), 1) // CHUNK
    back = row_c - col_c
    valid = jnp.logical_and(jnp.logical_and(back >= 0, back <= 2), 2 * m - 2 + col_c >= 0)
    out_lane = lax.broadcasted_iota(jnp.int32, (QB, PAIR), 1)

    def dup(x, kv):
        sw = pltpu.roll(x, HEAD_DIM, 1)
        return jnp.where(lane < HEAD_DIM, x, sw) if kv == 0 else jnp.where(lane < HEAD_DIM, sw, x)

    k_st = [_stack_heads(dup(k_all, kv), HEAD_DIM).astype(BF16) for kv in range(ATT_KV_HEADS)]
    v_st = [_stack_heads(dup(v_all, kv), HEAD_DIM).astype(BF16) for kv in range(ATT_KV_HEADS)]
    pairs = range(ATT_HEADS // 2)
    kv_of = lambda j: (2 * j) // ATT_GROUP
    s = [lax.dot_general(q_ref[:, j * PAIR:(j + 1) * PAIR].astype(BF16), k_st[kv_of(j)],
                         (((1,), (1,)), ((), ())), preferred_element_type=F32) * ATT_SCALE for j in pairs]
    probs, dens = [], []
    for j in pairs:
        halves, den = [], []
        for e in range(2):
            sink = sink_ref[2 * j + e]
            sh = jnp.where(valid, s[j][:, e * span:(e + 1) * span], -jnp.inf)
            mx = jnp.maximum(jnp.max(sh, -1, keepdims=True), sink)
            ph = jnp.exp(sh - mx)
            halves.append(ph)
            den.append(jnp.sum(ph, -1, keepdims=True) + jnp.exp(sink - mx))
        probs.append(jnp.concatenate(halves, axis=1).astype(BF16))
        dens.append(jnp.where(out_lane < HEAD_DIM, den[0], den[1]))
    o = [jnp.dot(probs[j], v_st[kv_of(j)], preferred_element_type=F32) / dens[j] for j in pairs]
    o_ref[...] = jnp.concatenate(o, axis=1).astype(o_ref.dtype)


def _attn_prompt(p, sinks, batch, seq):
    nb = seq // QB
    kcol, vcol = P_AK // KV_WIDTH, P_AV // KV_WIDTH
    prev = lambda col: pl.BlockSpec((QB, KV_WIDTH), lambda b, m: (b * nb + jnp.maximum(m - 1, 0), col))
    cur = lambda col: pl.BlockSpec((QB, KV_WIDTH), lambda b, m: (b * nb + m, col))
    return pl.pallas_call(
        _attn_prompt_kernel,
        grid=(batch, nb),
        in_specs=[
            pl.BlockSpec(memory_space=pltpu.SMEM),
            pl.BlockSpec((QB, ATT_WIDTH), lambda b, m: (b * nb + m, P_Q // ATT_WIDTH)),
            prev(kcol), cur(kcol), prev(vcol), cur(vcol),
        ],
        out_specs=pl.BlockSpec((QB, ATT_WIDTH), lambda b, m: (b * nb + m, 0)),
        out_shape=jax.ShapeDtypeStruct((batch * seq, ATT_WIDTH), BF16),
        compiler_params=_cparams(("arbitrary", "arbitrary")),
        name="attn_prompt",
    )(sinks, p, p, p, p, p)


def _attn_sample_kernel(sink_ref, q_ref, kn_ref, vn_ref, kc_ref, vc_ref, o_ref, kw_ref, vw_ref):
    t = q_ref.shape[0]
    k = jnp.concatenate([kc_ref[...], kn_ref[...]], axis=0)
    v = jnp.concatenate([vc_ref[...], vn_ref[...]], axis=0)
    o_ref[...] = _attend(q_ref[...], k, v, None, sink_ref).astype(o_ref.dtype)
    kw_ref[...] = k[t:]
    vw_ref[...] = v[t:]


def _attn_sample(p, sinks, cache_k, cache_v, batch, seq):
    kcol, vcol = P_AK // KV_WIDTH, P_AV // KV_WIDTH
    return pl.pallas_call(
        _attn_sample_kernel,
        grid=(batch,),
        in_specs=[
            pl.BlockSpec(memory_space=pltpu.SMEM),
            pl.BlockSpec((seq, ATT_WIDTH), lambda b: (b, P_Q // ATT_WIDTH)),
            pl.BlockSpec((seq, KV_WIDTH), lambda b: (b, kcol)),
            pl.BlockSpec((seq, KV_WIDTH), lambda b: (b, vcol)),
            pl.BlockSpec((WINDOW, KV_WIDTH), lambda b: (b, 0)),
            pl.BlockSpec((WINDOW, KV_WIDTH), lambda b: (b, 0)),
        ],
        out_specs=[
            pl.BlockSpec((seq, ATT_WIDTH), lambda b: (b, 0)),
            pl.BlockSpec((WINDOW, KV_WIDTH), lambda b: (b, 0)),
            pl.BlockSpec((WINDOW, KV_WIDTH), lambda b: (b, 0)),
        ],
        out_shape=[
            jax.ShapeDtypeStruct((batch * seq, ATT_WIDTH), BF16),
            jax.ShapeDtypeStruct((batch * WINDOW, KV_WIDTH), F32),
            jax.ShapeDtypeStruct((batch * WINDOW, KV_WIDTH), F32),
        ],
        compiler_params=_cparams(("arbitrary",)),
        name="attn_sample",
    )(sinks, p, p, p, cache_k, cache_v)


SEG = 256
EXP_NEG_HALF = math.exp(-0.5)
PAIR = 2 * HEAD_DIM
N_PAIRS = RW_HEADS // 2


def _split2(x):
    hi = x.astype(BF16)
    lo = (x - hi.astype(F32)).astype(BF16)
    return hi, lo


def _seg_sum(x, bd):
    c = x.shape[0]
    xb = x.astype(BF16)
    stack = jnp.concatenate([xb[:, s:s + SEG] for s in range(0, RW_WIDTH, SEG)], axis=0)
    r = jnp.dot(stack, bd, preferred_element_type=F32)
    return jnp.concatenate([r[i * c:(i + 1) * c] for i in range(RW_WIDTH // SEG)], axis=1)


def _shift_rows(x, prev_row):
    row = lax.broadcasted_iota(jnp.int32, x.shape, 0)
    return jnp.where(row == 0, prev_row, pltpu.roll(x, 1, 0))


def _dotb(a, b):
    return jnp.dot(a.astype(BF16), b.astype(BF16), preferred_element_type=F32)


def _dotb_nt(a, b):
    return lax.dot_general(a.astype(BF16), b.astype(BF16), (((1,), (1,)), ((), ())), preferred_element_type=F32)


def _dot3_tn(a, b):
    ah, al = _split2(a)
    bh, bl = _split2(b)
    return lax.dot_general(jnp.concatenate([ah, al, ah], axis=0), jnp.concatenate([bh, bh, bl], axis=0),
                           (((0,), (0,)), ((), ())), preferred_element_type=F32)


def _dot2_tn(a, b):
    ah, al = _split2(a)
    bh = b.astype(BF16)
    return lax.dot_general(jnp.concatenate([ah, al], axis=0), jnp.concatenate([bh, bh], axis=0),
                           (((0,), (0,)), ((), ())), preferred_element_type=F32)


def _dot2(a, b):
    ah, al = _split2(a)
    bh = b.astype(BF16)
    return jnp.dot(jnp.concatenate([ah, al], axis=1), jnp.concatenate([bh, bh], axis=0), preferred_element_type=F32)


def _stack_heads(x, half):
    lane = lax.broadcasted_iota(jnp.int32, x.shape, 1)
    first = (lane % (2 * half)) < half
    zero = jnp.zeros_like(x)
    return jnp.concatenate([jnp.where(first, x, zero), jnp.where(first, zero, x)], axis=0)


def _stack_heads_bf16(x, half):
    return _stack_heads(x.astype(BF16), half)


def _rwkv_pairs(ab, rb, bb, kb, v, gam_c, s_bd, n_double):
    c = ab[0].shape[0]
    pairs = range(len(ab))
    t_idx = lax.broadcasted_iota(jnp.int32, (c, 2 * c), 0)
    s_idx = lax.broadcasted_iota(jnp.int32, (c, 2 * c), 1) % c
    strict, incl = s_idx < t_idx, s_idx <= t_idx
    g = [_dotb_nt(jnp.concatenate([ab[p], rb[p]], axis=0),
                  jnp.concatenate([_stack_heads_bf16(bb[p], HEAD_DIM), _stack_heads_bf16(kb[p], HEAD_DIM)], axis=0))
         for p in pairs]
    l_ak = [jnp.where(strict, g[p][:c, 2 * c:], 0.0) for p in pairs]
    m_rbk = [jnp.concatenate([jnp.where(incl, g[p][c:, :2 * c], 0.0), jnp.where(incl, g[p][c:, 2 * c:], 0.0)], axis=1)
             for p in pairs]
    lp = [jnp.where(strict, g[p][:c, :2 * c], 0.0) for p in pairs]
    v_st = [_stack_heads_bf16(v[p], HEAD_DIM) for p in pairs]
    ya = list(ab)
    yu = [_dotb(l_ak[p], v_st[p]) for p in pairs]
    for i in range(n_double):
        y_st = [_stack_heads_bf16(jnp.concatenate([ya[p], yu[p]], axis=1), HEAD_DIM) for p in pairs]
        if i + 1 < n_double:
            z = [_dotb(lp[p], jnp.concatenate([y_st[p], _stack_heads_bf16(lp[p], c)], axis=1)) for p in pairs]
            lp = [z[p][:, 2 * PAIR:] for p in pairs]
        else:
            z = [_dotb(lp[p], y_st[p]) for p in pairs]
        ya = [ya[p] + z[p][:, :PAIR] for p in pairs]
        yu = [yu[p] + z[p][:, PAIR:2 * PAIR] for p in pairs]
    z = []
    for p in pairs:
        top = _stack_heads_bf16(jnp.concatenate([ya[p], yu[p]], axis=1), HEAD_DIM)
        bot = jnp.concatenate([jnp.zeros_like(v_st[p]), v_st[p]], axis=1)
        z.append(_dotb(m_rbk[p], jnp.concatenate([top, bot], axis=0)))
    o = [_dotb_nt(rb[p] + z[p][:, :PAIR], s_bd[p]) + z[p][:, PAIR:] for p in pairs]
    row_h = lax.broadcasted_iota(jnp.int32, (PAIR, PAIR), 0) // HEAD_DIM
    col_h = lax.broadcasted_iota(jnp.int32, (PAIR, PAIR), 1) // HEAD_DIM
    same = row_h == col_h
    q_t = [jnp.where(same, _dot3_tn(ya[p], bb[p]), 0.0) for p in pairs]
    d_t = [jnp.where(same, _dot2_tn(jnp.concatenate([yu[p], v[p]], axis=0),
                                    jnp.concatenate([bb[p], kb[p]], axis=0)), 0.0) for p in pairs]
    s_new = [(s_bd[p] + _dot2(s_bd[p], q_t[p]) + d_t[p]) * gam_c[p] for p in pairs]
    return o, s_new


def _split_rw_rows(x):
    o_wd, o_gd = RW_WIDTH, RW_COLS - GATE_LORA
    low = lax.broadcasted_iota(jnp.int32, (x.shape[0], LANES), 1) < HEAD_DIM
    pieces = [x[:, c0:c0 + LANES] for c0 in range(o_wd, o_gd, LANES)]
    rolled = [pltpu.roll(p, HEAD_DIM, 1) for p in pieces]
    merged = [jnp.where(low, rolled[m], rolled[m + 1]) for m in range(len(pieces) - 1)]
    n = RW_WIDTH // LANES
    return (x[:, :RW_WIDTH], jnp.concatenate(merged[:n], axis=1), jnp.concatenate(merged[n:], axis=1),
            jnp.where(low, pieces[0], pieces[-1]), x[:, o_gd:])


def _rwkv_kernel(xr_ref, xk_ref, xv_ref, xl_ref, sh_ref, mu_ref, w0_ref, a0_ref, kk_ref, ka_ref, rk_ref,
                 lg_ref, lb_ref, w2_ref, a2_ref, g2_ref, bd_ref, st0_ref,
                 out_ref, st_ref,
                 s_scr, pr_scr, pk_scr, pv_scr, pw_scr, pg_scr):
    c = pl.program_id(1)
    nc = pl.num_programs(1)
    nb, C = xr_ref.shape[0], xr_ref.shape[1]
    n_double = int(math.log2(C))
    prev_scr = (pr_scr, pk_scr, pv_scr, pw_scr, pg_scr)

    @pl.when(c == 0)
    def _():
        s_scr[...] = jnp.zeros_like(s_scr)
        for b in range(nb):
            for h in range(RW_HEADS):
                off = (h % 2) * HEAD_DIM
                s_scr[b * N_PAIRS + h // 2, off:off + HEAD_DIM, off:off + HEAD_DIM] = st0_ref[b, h]
            for scr, row in zip(prev_scr, _split_rw_rows(sh_ref[b])):
                scr[b, 0:1, :] = row

    def shifted(load, p_scr, mu):
        parts = []
        for b in range(nb):
            x = load(b)
            prev = _shift_rows(x, p_scr[b, 0:1, :])
            p_scr[b, 0:1, :] = x[C - 1:C, :]
            parts.append(x + (prev - x) * mu)
        return jnp.concatenate(parts, axis=0)

    mu_r, mu_k, mu_v, mu_w, mu_g = _split_rw_rows(mu_ref[...])
    wa_w = DECAY_LORA + AAA_LORA
    r = shifted(lambda b: xr_ref[b], pr_scr, mu_r)
    kr = shifted(lambda b: xk_ref[b], pk_scr, mu_k)
    v = shifted(lambda b: xv_ref[b], pv_scr, mu_v)
    xw = shifted(lambda b: xl_ref[b, :, :wa_w], pw_scr, mu_w)
    gd = shifted(lambda b: xl_ref[b, :, wa_w:wa_w + GATE_LORA], pg_scr, mu_g)
    wd = xw[:, 0:DECAY_LORA]
    ad = xw[:, DECAY_LORA:wa_w]

    lw = -EXP_NEG_HALF * jax.nn.sigmoid(w0_ref[...] + _dotb(jnp.tanh(wd), w2_ref[...]))
    a = jax.nn.sigmoid(a0_ref[...] + _dotb(ad, a2_ref[...]))
    g = _dotb(jax.nn.sigmoid(gd), g2_ref[...])
    bd = bd_ref[...]
    kk = kr * kk_ref[...]
    kk = kk * lax.rsqrt(jnp.maximum(_seg_sum(kk * kk, bd), 1e-24))
    k2 = kr * (1.0 + (a - 1.0) * ka_ref[...])

    rows = nb * C
    ti = lax.broadcasted_iota(jnp.int32, (rows, rows), 0)
    si = lax.broadcasted_iota(jnp.int32, (rows, rows), 1)
    tri = jnp.logical_and(si <= ti, si // C == ti // C).astype(BF16)
    cs2 = jnp.dot(tri, jnp.concatenate(_split2(lw), axis=1), preferred_element_type=F32)
    cs = cs2[:, :RW_WIDTH] + cs2[:, RW_WIDTH:]
    gam = jnp.exp(cs)
    ginv = jnp.exp(-cs)
    ab = -kk * jnp.exp(cs - lw)
    rb = r * gam
    bb = kk * a * ginv
    kb = k2 * ginv

    def per_pair(x, r0, r1):
        return [x[b * C + r0:b * C + r1, p * PAIR:(p + 1) * PAIR] for b in range(nb) for p in range(N_PAIRS)]

    chains = nb * N_PAIRS
    outs, s_new = _rwkv_pairs(per_pair(ab, 0, C), per_pair(rb, 0, C), per_pair(bb, 0, C), per_pair(kb, 0, C),
                              per_pair(v, 0, C), per_pair(gam, C - 1, C), [s_scr[q] for q in range(chains)], n_double)
    for q in range(chains):
        s_scr[q] = s_new[q]
    o = jnp.concatenate([jnp.concatenate(outs[b * N_PAIRS:(b + 1) * N_PAIRS], axis=1) for b in range(nb)], axis=0)

    mo = _seg_sum(o, bd) * (1.0 / HEAD_DIM)
    d = o - mo
    vo = _seg_sum(d * d, bd) * (1.0 / HEAD_DIM)
    on = d * lax.rsqrt(vo + LNX_EPS) * lg_ref[...] + lb_ref[...]
    bonus = _seg_sum(r * k2 * rk_ref[...], bd) * v
    res = ((on + bonus) * g).astype(out_ref.dtype)
    for b in range(nb):
        out_ref[b] = res[b * C:(b + 1) * C, :]

    @pl.when(c == nc - 1)
    def _():
        for b in range(nb):
            for h in range(RW_HEADS):
                off = (h % 2) * HEAD_DIM
                st_ref[b, h] = s_scr[b * N_PAIRS + h // 2, off:off + HEAD_DIM, off:off + HEAD_DIM]


def _rwkv(p3, state_shift, state0, prm, chunk, nb):
    batch, seq = p3.shape[0], p3.shape[1]
    nc = seq // chunk
    row = lambda w: pl.BlockSpec((1, w), lambda g, c: (0, 0))
    xspec = lambda w, col: pl.BlockSpec((nb, chunk, w), lambda g, c: (g, c, col))
    full = lambda a: pl.BlockSpec(a.shape, lambda g, c: (0,) * a.ndim)
    stspec = pl.BlockSpec((nb, RW_HEADS, HEAD_DIM, HEAD_DIM), lambda g, c: (g, 0, 0, 0))
    return pl.pallas_call(
        _rwkv_kernel,
        grid=(batch // nb, nc),
        in_specs=[
            xspec(RW_WIDTH, P_R // RW_WIDTH), xspec(RW_WIDTH, P_K // RW_WIDTH), xspec(RW_WIDTH, P_V // RW_WIDTH),
            xspec(LORA_W, P_LORA // LORA_W),
            pl.BlockSpec((nb, 1, RW_COLS), lambda g, c: (g, 0, 0)),
            row(RW_COLS),
            row(RW_WIDTH), row(RW_WIDTH), row(RW_WIDTH), row(RW_WIDTH), row(RW_WIDTH),
            row(RW_WIDTH), row(RW_WIDTH),
            full(prm["w2"]), full(prm["a2"]), full(prm["g2"]), full(prm["bd"]),
            stspec,
        ],
        out_specs=[
            pl.BlockSpec((nb, chunk, RW_WIDTH), lambda g, c: (g, c, 0)),
            stspec,
        ],
        out_shape=[
            jax.ShapeDtypeStruct((batch, seq, RW_WIDTH), BF16),
            jax.ShapeDtypeStruct((batch, RW_HEADS, HEAD_DIM, HEAD_DIM), F32),
        ],
        scratch_shapes=[
            pltpu.VMEM((nb * N_PAIRS, PAIR, PAIR), F32),
            pltpu.VMEM((nb, SUBLANES, RW_WIDTH), F32), pltpu.VMEM((nb, SUBLANES, RW_WIDTH), F32),
            pltpu.VMEM((nb, SUBLANES, RW_WIDTH), F32),
            pltpu.VMEM((nb, SUBLANES, DECAY_LORA + AAA_LORA), F32), pltpu.VMEM((nb, SUBLANES, GATE_LORA), F32),
        ],
        compiler_params=_cparams(("arbitrary", "arbitrary")),
        name="rwkv",
    )(p3, p3, p3, p3, state_shift, prm["mu"], prm["w0"], prm["a0"], prm["k_k"], prm["k_a"], prm["r_k"],
      prm["lnx_g"], prm["lnx_b"], prm["w2"], prm["a2"], prm["g2"], prm["bd"], state0)


D_TN = 512


def _out_proj_kernel(att_ref, rw_ref, wa_ref, wb_ref, h_ref, g_ref, b_ref, h1_ref, h1b_ref, w_scr):
    @pl.when(pl.program_id(0) == 0)
    def _():
        w_scr[:ATT_WIDTH, :] = wa_ref[...].astype(BF16)
        w_scr[ATT_WIDTH:, :] = wb_ref[...].astype(BF16)

    for c0 in range(0, D_MODEL, D_TN):
        m = jnp.dot(att_ref[...], w_scr[:ATT_WIDTH, c0:c0 + D_TN], preferred_element_type=F32)
        m = m + jnp.dot(rw_ref[...], w_scr[ATT_WIDTH:, c0:c0 + D_TN], preferred_element_type=F32)
        h1_ref[:, c0:c0 + D_TN] = ALPHA * h_ref[:, c0:c0 + D_TN] + m
    y = _layer_norm(h1_ref[...], g_ref[...], b_ref[...])
    h1_ref[...] = y
    h1b_ref[...] = y.astype(BF16)


def _out_proj(att, rw, w_out, h, g, b, tm):
    n = att.shape[0]
    return pl.pallas_call(
        _out_proj_kernel,
        grid=(n // tm,),
        in_specs=[
            pl.BlockSpec((tm, ATT_WIDTH), lambda i: (i, 0)),
            pl.BlockSpec((tm, RW_WIDTH), lambda i: (i, 0)),
            pl.BlockSpec((ATT_WIDTH, D_MODEL), lambda i: (0, 0), pipeline_mode=pl.Buffered(1)),
            pl.BlockSpec((RW_WIDTH, D_MODEL), lambda i: (1, 0), pipeline_mode=pl.Buffered(1)),
            pl.BlockSpec((tm, D_MODEL), lambda i: (i, 0)),
            pl.BlockSpec((1, D_MODEL), lambda i: (0, 0)),
            pl.BlockSpec((1, D_MODEL), lambda i: (0, 0)),
        ],
        out_specs=[
            pl.BlockSpec((tm, D_MODEL), lambda i: (i, 0)),
            pl.BlockSpec((tm, D_MODEL), lambda i: (i, 0)),
        ],
        out_shape=[
            jax.ShapeDtypeStruct((n, D_MODEL), F32),
            jax.ShapeDtypeStruct((n, D_MODEL), BF16),
        ],
        scratch_shapes=[pltpu.VMEM((ATT_WIDTH + RW_WIDTH, D_MODEL), BF16)],
        compiler_params=_cparams(("arbitrary",)),
        name="out_proj",
    )(att, rw, w_out, w_out, h, g, b)


E_TN = 512
E_NCB = D_FF // E_TN
E_SUB = 512
SQRT_HALF = math.sqrt(0.5)


def _ffn_up_kernel(bps, nseq, h_ref, wg_ref, wv_ref, cwg_ref, cwv_ref, cbg_ref, cbv_ref, sg_ref, sv_ref,
                   act_ref, og_ref, ov_ref, w_scr, cg_scr, cv_scr):
    i = pl.program_id(1)
    tm = h_ref.shape[0]
    sub = min(tm, E_SUB)
    n_sub = tm // sub
    t_seq = tm // nseq
    row = lax.broadcasted_iota(jnp.int32, (sub, E_TN), 0)

    @pl.when(i == 0)
    def _():
        w_scr[:, :E_TN] = wg_ref[...].astype(BF16)
        w_scr[:, E_TN:] = wv_ref[...].astype(BF16)

    def products(r0):
        u = jnp.dot(h_ref[r0:r0 + sub, :], w_scr[...], preferred_element_type=F32)
        return u[:, :E_TN], u[:, E_TN:]

    def conv(u, prevs, cw_ref, cb_ref):
        sh1, sh2 = pltpu.roll(u, 1, 0), pltpu.roll(u, 2, 0)
        for r, prev in prevs:
            p2, p1 = prev[0:1, :], prev[1:2, :]
            sh1 = jnp.where(row == r, p1, sh1)
            sh2 = jnp.where(row == r, p2, jnp.where(row == r + 1, p1, sh2))
        cw = cw_ref[...]
        c = cb_ref[...] + sh2 * cw[0:1, :]
        c = c + sh1 * cw[1:2, :]
        return c + u * cw[2:3, :]

    def epilogue(r0, ug, uv, prev_g, prev_v):
        gate = conv(ug, prev_g, cwg_ref, cbg_ref)
        val = conv(uv, prev_v, cwv_ref, cbv_ref)
        gelu = 0.5 * gate * (1.0 + lax.erf(gate * SQRT_HALF))
        act_ref[r0:r0 + sub, :] = (gelu * val).astype(act_ref.dtype)

    if nseq > 1:
        ug, uv = products(0)
        epilogue(0, ug, uv, [(s * t_seq, sg_ref[s]) for s in range(nseq)],
                 [(s * t_seq, sv_ref[s]) for s in range(nseq)])
        for s in range(nseq):
            og_ref[s] = ug[(s + 1) * t_seq - SUBLANES:(s + 1) * t_seq, :]
            ov_ref[s] = uv[(s + 1) * t_seq - SUBLANES:(s + 1) * t_seq, :]
    else:
        first = (i % bps) == 0
        prev_g = jnp.where(first, sg_ref[0], cg_scr[SUBLANES - 2:SUBLANES, :])
        prev_v = jnp.where(first, sv_ref[0], cv_scr[SUBLANES - 2:SUBLANES, :])
        nxt = products(0)
        for k in range(n_sub):
            ug, uv = nxt
            if k + 1 < n_sub:
                nxt = products((k + 1) * sub)
            epilogue(k * sub, ug, uv, [(0, prev_g)], [(0, prev_v)])
            prev_g, prev_v = ug[sub - 2:sub, :], uv[sub - 2:sub, :]
        cg_scr[...] = ug[sub - SUBLANES:sub, :]
        cv_scr[...] = uv[sub - SUBLANES:sub, :]
        og_ref[0] = ug[sub - SUBLANES:sub, :]
        ov_ref[0] = uv[sub - SUBLANES:sub, :]


def _ffn_up(h1b, w_up, conv_w, conv_b, conv_prev, batch, seq, tm):
    n = h1b.shape[0]
    nseq = max(tm // seq, 1)
    bps = max(seq // tm, 1)
    nrb = n // tm
    return pl.pallas_call(
        functools.partial(_ffn_up_kernel, bps, nseq),
        grid=(E_NCB, nrb),
        in_specs=[
            pl.BlockSpec((tm, D_MODEL), lambda j, i: (i, 0)),
            pl.BlockSpec((D_MODEL, E_TN), lambda j, i: (0, j)),
            pl.BlockSpec((D_MODEL, E_TN), lambda j, i: (0, j + E_NCB)),
            pl.BlockSpec((CONV_W, E_TN), lambda j, i: (0, j)),
            pl.BlockSpec((CONV_W, E_TN), lambda j, i: (0, j + E_NCB)),
            pl.BlockSpec((1, E_TN), lambda j, i: (0, j)),
            pl.BlockSpec((1, E_TN), lambda j, i: (0, j + E_NCB)),
            pl.BlockSpec((nseq, CONV_W - 1, E_TN), lambda j, i: (i // bps, 0, j)),
            pl.BlockSpec((nseq, CONV_W - 1, E_TN), lambda j, i: (i // bps, 0, j + E_NCB)),
        ],
        out_specs=[
            pl.BlockSpec((tm, E_TN), lambda j, i: (i, j)),
            pl.BlockSpec((nseq, SUBLANES, E_TN), lambda j, i: (i // bps, 0, j)),
            pl.BlockSpec((nseq, SUBLANES, E_TN), lambda j, i: (i // bps, 0, j)),
        ],
        out_shape=[
            jax.ShapeDtypeStruct((n, D_FF), BF16),
            jax.ShapeDtypeStruct((batch, SUBLANES, D_FF), F32),
            jax.ShapeDtypeStruct((batch, SUBLANES, D_FF), F32),
        ],
        scratch_shapes=[pltpu.VMEM((D_MODEL, 2 * E_TN), BF16),
                        pltpu.VMEM((SUBLANES, E_TN), F32), pltpu.VMEM((SUBLANES, E_TN), F32)],
        compiler_params=_cparams(("arbitrary", "arbitrary")),
        name="ffn_up",
    )(h1b, w_up, w_up, conv_w, conv_w, conv_b, conv_b, conv_prev, conv_prev)


F_TN = 512


def _ffn_down_kernel(act_ref, w_ref, h1_ref, g_ref, b_ref, y_ref):
    for c0 in range(0, D_MODEL, F_TN):
        f = jnp.dot(act_ref[...], w_ref[:, c0:c0 + F_TN], preferred_element_type=F32)
        y_ref[:, c0:c0 + F_TN] = ALPHA * h1_ref[:, c0:c0 + F_TN] + f
    y_ref[...] = _layer_norm(y_ref[...], g_ref[...], b_ref[...])


def _ffn_down(act, w_down, h1, g, b, tm):
    n = act.shape[0]
    return pl.pallas_call(
        _ffn_down_kernel,
        grid=(n // tm,),
        in_specs=[
            pl.BlockSpec((tm, D_FF), lambda i: (i, 0)),
            pl.BlockSpec((D_FF, D_MODEL), lambda i: (0, 0), pipeline_mode=pl.Buffered(1)),
            pl.BlockSpec((tm, D_MODEL), lambda i: (i, 0)),
            pl.BlockSpec((1, D_MODEL), lambda i: (0, 0)),
            pl.BlockSpec((1, D_MODEL), lambda i: (0, 0)),
        ],
        out_specs=pl.BlockSpec((tm, D_MODEL), lambda i: (i, 0)),
        out_shape=jax.ShapeDtypeStruct((n, D_MODEL), F32),
        compiler_params=_cparams(("arbitrary",)),
        name="ffn_down",
    )(act, w_down, h1, g, b)


def _rope_tables(pos, reps):
    half = HEAD_DIM // 2
    inv = ROPE_THETA ** (-jnp.arange(half, dtype=F32) / half)
    ang = pos.astype(F32)[:, None] * inv[None, :]
    cos, sin = jnp.cos(ang), jnp.sin(ang)
    cos128 = jnp.concatenate([cos, cos, cos, cos], axis=1)
    sin128 = jnp.concatenate([-sin, sin, -sin, sin], axis=1)
    return jnp.tile(cos128, (reps, 1)), jnp.tile(sin128, (reps, 1))


def _run(x, pos_tables, table_blocks, attn_fn, state_wkv, state_shift, state_conv, wts, tiles):
    batch, seq = x.shape[0], x.shape[1]
    n = batch * seq
    cos, sin = pos_tables
    p, h, tails = _in_proj(x.reshape(n, D_MODEL), wts["ln_in_g"], wts["ln_in_b"], wts["w_in_p"], cos, sin,
                           tiles["a_tm"], table_blocks, seq)
    att, k_win, v_win = attn_fn(p)
    p3 = p.reshape(batch, seq, P_COLS)
    rw, wkv = _rwkv(p3, state_shift, state_wkv, wts["rw"], tiles["chunk"], tiles["c_nb"])
    h1, h1b = _out_proj(att, rw.reshape(n, RW_WIDTH), wts["w_out"], h, wts["ln1_g"], wts["ln1_b"], tiles["d_tm"])
    act, cg, cv = _ffn_up(h1b, wts["w_up"], wts["conv_w"], wts["conv_b"], state_conv, batch, seq, tiles["e_tm"])
    y = _ffn_down(act, wts["w_down"], h1, wts["ln2_g"], wts["ln2_b"], tiles["f_tm"])
    shift_out = tails.reshape(batch, -1, RW_COLS)[:, -1:, :]
    conv_out = jnp.concatenate([cg[:, SUBLANES - 2:, :], cv[:, SUBLANES - 2:, :]], axis=-1)
    if k_win is None:
        k_win = p3[:, seq - WINDOW:, P_AK:P_AK + KV_WIDTH]
        v_win = p3[:, seq - WINDOW:, P_AV:P_AV + KV_WIDTH]
    kv_shape = (1, batch, WINDOW, ATT_KV_HEADS, HEAD_DIM)
    return (y.reshape(batch, seq, D_MODEL), k_win.reshape(kv_shape), v_win.reshape(kv_shape),
            wkv[None], shift_out[None], conv_out[None])


def kernel(x_prompt, x_sample, cache_k, cache_v, state_wkv, state_shift, state_ffn_conv, ln_in_g, ln_in_b, w_in, attn_sinks, rw_mu, rw_w0, rw_w2, rw_a0, rw_a2, rw_g2, rw_k_k, rw_k_a, rw_r_k, rw_lnx_g, rw_lnx_b, w_out, ln1_g, ln1_b, ffn_w_up, ffn_conv_w, ffn_conv_b, ffn_w_down, ln2_g, ln2_b):
    l = 0
    w_in_p = w_in[l].astype(BF16)
    seg = lax.broadcasted_iota(jnp.int32, (SEG, SEG), 0) // HEAD_DIM
    bd = (seg == seg.T).astype(BF16)
    row = lambda a: a.reshape(1, -1)
    wts = {
        "ln_in_g": row(ln_in_g), "ln_in_b": row(ln_in_b), "w_in_p": w_in_p,
        "rw": {"mu": row(rw_mu[l]),
               "w0": row(rw_w0[l]), "a0": row(rw_a0[l]), "k_k": row(rw_k_k[l]), "k_a": row(rw_k_a[l]),
               "r_k": row(rw_r_k[l]), "lnx_g": row(rw_lnx_g[l]), "lnx_b": row(rw_lnx_b[l]),
               "w2": rw_w2[l].astype(BF16), "a2": rw_a2[l].astype(BF16), "g2": rw_g2[l].astype(BF16), "bd": bd},
        "w_out": w_out[l], "ln1_g": row(ln1_g[l]), "ln1_b": row(ln1_b[l]),
        "w_up": ffn_w_up[l], "conv_w": ffn_conv_w[l], "conv_b": row(ffn_conv_b[l]),
        "w_down": ffn_w_down[l].astype(BF16), "ln2_g": row(ln2_g[l]), "ln2_b": row(ln2_b[l]),
    }
    sinks = attn_sinks[l]

    bp, tp = x_prompt.shape[0], x_prompt.shape[1]
    p_tiles = {"a_tm": 256, "chunk": CHUNK, "c_nb": 4, "d_tm": 256, "e_tm": 1024, "f_tm": 256}
    p_tables = _rope_tables(jnp.arange(tp, dtype=jnp.int32), 1)
    zeros = lambda *s: jnp.zeros(s, F32)
    y_p, p_k, p_v, p_wkv, p_shift, p_conv = _run(
        x_prompt, p_tables, tp // p_tiles["a_tm"],
        lambda p: (_attn_prompt(p, sinks, bp, tp), None, None),
        zeros(bp, RW_HEADS, HEAD_DIM, HEAD_DIM), zeros(bp, 1, RW_COLS), zeros(bp, CONV_W - 1, 2 * D_FF),
        wts, p_tiles)

    bs, ts = x_sample.shape[0], x_sample.shape[1]
    s_tiles = {"a_tm": bs * ts, "chunk": ts, "c_nb": 4, "d_tm": bs * ts, "e_tm": bs * ts, "f_tm": bs * ts}
    s_tables = _rope_tables(PAST_LEN + jnp.arange(ts, dtype=jnp.int32), bs)
    ck = cache_k[l].reshape(bs * WINDOW, KV_WIDTH)
    cv = cache_v[l].reshape(bs * WINDOW, KV_WIDTH)
    y_s, s_k, s_v, s_wkv, s_shift, s_conv = _run(
        x_sample, s_tables, 1,
        lambda p: _attn_sample(p, sinks, ck, cv, bs, ts),
        state_wkv[l], state_shift[l], state_ffn_conv[l], wts, s_tiles)

    return (y_p, y_s, p_k, p_v, p_wkv, p_shift, p_conv, s_k, s_v, s_wkv, s_shift, s_conv)
```

```python
import functools
import math

import jax
import jax.numpy as jnp
from jax import lax
from jax.experimental import pallas as pl
from jax.experimental.pallas import tpu as pltpu

D_MODEL = 2048
HEAD_DIM = 64
ATT_HEADS = 16
ATT_KV_HEADS = 2
ATT_GROUP = ATT_HEADS // ATT_KV_HEADS
ATT_WIDTH = ATT_HEADS * HEAD_DIM
KV_WIDTH = ATT_KV_HEADS * HEAD_DIM
CHUNK = 64
WINDOW = 128
ROPE_THETA = 10000.0
ATT_SCALE = HEAD_DIM ** -0.5
RW_HEADS = 16
RW_WIDTH = RW_HEADS * HEAD_DIM
DECAY_LORA = 64
AAA_LORA = 64
GATE_LORA = 160
LNX_EPS = 64e-5
ATT_COLS = ATT_WIDTH + 2 * KV_WIDTH
RW_COLS = 3 * RW_WIDTH + DECAY_LORA + AAA_LORA + GATE_LORA
D_FF = 5632
CONV_W = 3
LN_EPS = 1e-5
DEPTH = 1
ALPHA = (2 * DEPTH) ** 0.25
PAST_LEN = 1024

LORA_W = 512
P_R, P_K, P_V, P_Q = 0, 1024, 2048, 3072
P_LORA = 4096
P_AK = P_LORA + LORA_W
P_AV = P_AK + KV_WIDTH
P_COLS = P_AV + KV_WIDTH
LANES = 128
SUBLANES = 8
VMEM_LIMIT = 56 * 1024 * 1024

F32 = jnp.float32
BF16 = jnp.bfloat16


def _cparams(sem):
    return pltpu.CompilerParams(dimension_semantics=sem, vmem_limit_bytes=VMEM_LIMIT)


def _layer_norm(y, g, b):
    mu = jnp.mean(y, -1, keepdims=True)
    d = y - mu
    var = jnp.mean(d * d, -1, keepdims=True)
    return d * lax.rsqrt(var + LN_EPS) * g + b


A_TN = 512


def _rope128(x, cos, sin_signed):
    lane = lax.broadcasted_iota(jnp.int32, x.shape, 1)
    first_half = (lane % HEAD_DIM) < (HEAD_DIM // 2)
    rot = jnp.where(first_half, pltpu.roll(x, LANES - HEAD_DIM // 2, 1), pltpu.roll(x, HEAD_DIM // 2, 1))
    return x * cos + rot * sin_signed


def _is_rope_col(col):
    return (P_Q <= col < P_Q + ATT_WIDTH) or (P_AK <= col < P_AK + KV_WIDTH)


O_R = ATT_COLS
O_WD = O_R + RW_WIDTH
O_K = O_WD + DECAY_LORA
O_V = O_K + RW_WIDTH
O_AD = O_V + RW_WIDTH
O_GD = O_AD + AAA_LORA
IN_COLS = O_GD + GATE_LORA


def _in_proj_kernel(x_ref, g_ref, b_ref, w_ref, cos_ref, sin_ref, p_ref, h_ref, tail_ref, hb_ref):
    h = _layer_norm(x_ref[...], g_ref[...], b_ref[...])
    h_ref[...] = h
    hb_ref[...] = h.astype(BF16)
    nseq = tail_ref.shape[0]
    t_seq = x_ref.shape[0] // nseq

    def product(c0, width):
        acc = jnp.dot(hb_ref[...], w_ref[:, c0:c0 + width], preferred_element_type=F32)
        if c0 >= O_R:
            for s in range(nseq):
                tail_ref[s, :, c0 - O_R:c0 - O_R + width] = acc[(s + 1) * t_seq - SUBLANES:(s + 1) * t_seq, :]
        return acc

    for src, width, dst in ((0, ATT_WIDTH, P_Q), (ATT_WIDTH, 2 * KV_WIDTH, P_AK), (O_R, RW_WIDTH, P_R)):
        for c0 in range(0, width, A_TN):
            acc = product(src + c0, min(A_TN, width - c0))
            for l0 in range(0, acc.shape[1], LANES):
                col = dst + c0 + l0
                part = acc[:, l0:l0 + LANES]
                if _is_rope_col(col):
                    part = _rope128(part, cos_ref[...], sin_ref[...])
                p_ref[:, col:col + LANES] = part

    low = lax.broadcasted_iota(jnp.int32, (x_ref.shape[0], LANES), 1) < HEAD_DIM
    span = O_GD - O_WD
    rolled, first_piece = [], None
    for c0 in range(0, span, A_TN):
        acc = product(O_WD + c0, min(A_TN, span - c0))
        for l0 in range(0, acc.shape[1], LANES):
            piece = acc[:, l0:l0 + LANES]
            j = (c0 + l0) // LANES
            if j == 0:
                first_piece = piece
            rolled.append(pltpu.roll(piece, HEAD_DIM, 1))
            if j >= 1:
                m = j - 1
                dst = P_K + m * LANES if m < RW_WIDTH // LANES else P_V + (m - RW_WIDTH // LANES) * LANES
                p_ref[:, dst:dst + LANES] = jnp.where(low, rolled[m], rolled[m + 1])
            if j == span // LANES - 1:
                p_ref[:, P_LORA:P_LORA + LANES] = jnp.where(low, first_piece, piece)
    gd0 = P_LORA + DECAY_LORA + AAA_LORA
    p_ref[:, gd0:gd0 + GATE_LORA] = product(O_GD, GATE_LORA)
    p_ref[:, gd0 + GATE_LORA:P_LORA + LORA_W] = jnp.zeros((x_ref.shape[0], LORA_W - DECAY_LORA - AAA_LORA - GATE_LORA), F32)


def _in_proj(x, g, b, w_p, cos, sin, tm, table_blocks, seq):
    n = x.shape[0]
    nseq = max(tm // seq, 1)
    return pl.pallas_call(
        _in_proj_kernel,
        grid=(n // tm,),
        in_specs=[
            pl.BlockSpec((tm, D_MODEL), lambda i: (i, 0)),
            pl.BlockSpec((1, D_MODEL), lambda i: (0, 0)),
            pl.BlockSpec((1, D_MODEL), lambda i: (0, 0)),
            pl.BlockSpec((D_MODEL, IN_COLS), lambda i: (0, 0), pipeline_mode=pl.Buffered(1)),
            pl.BlockSpec((tm, LANES), lambda i: (i % table_blocks, 0)),
            pl.BlockSpec((tm, LANES), lambda i: (i % table_blocks, 0)),
        ],
        out_specs=[
            pl.BlockSpec((tm, P_COLS), lambda i: (i, 0)),
            pl.BlockSpec((tm, D_MODEL), lambda i: (i, 0)),
            pl.BlockSpec((nseq, SUBLANES, RW_COLS), lambda i: (i, 0, 0)),
        ],
        out_shape=[
            jax.ShapeDtypeStruct((n, P_COLS), F32),
            jax.ShapeDtypeStruct((n, D_MODEL), F32),
            jax.ShapeDtypeStruct((n // tm * nseq, SUBLANES, RW_COLS), F32),
        ],
        scratch_shapes=[pltpu.VMEM((tm, D_MODEL), BF16)],
        compiler_params=_cparams(("arbitrary",)),
        name="in_proj",
    )(x, g, b, w_p, cos, sin)


def _attend(q, k, v, valid, sink_ref):
    r = q.shape[0]
    outs = []
    for kv in range(ATT_KV_HEADS):
        kh = k[:, kv * HEAD_DIM:(kv + 1) * HEAD_DIM].astype(BF16)
        vh = v[:, kv * HEAD_DIM:(kv + 1) * HEAD_DIM].astype(BF16)
        heads = [kv * ATT_GROUP + g for g in range(ATT_GROUP)]
        qs = jnp.concatenate([q[:, h * HEAD_DIM:(h + 1) * HEAD_DIM] for h in heads], axis=0)
        sink = jnp.concatenate([jnp.full((r, 1), sink_ref[h], F32) for h in heads], axis=0)
        s = lax.dot_general(qs.astype(BF16), kh, (((1,), (1,)), ((), ())), preferred_element_type=F32) * ATT_SCALE
        if valid is not None:
            s = jnp.where(valid, s, -jnp.inf)
        m = jnp.maximum(jnp.max(s, -1, keepdims=True), sink)
        p = jnp.exp(s - m)
        den = jnp.sum(p, -1, keepdims=True) + jnp.exp(sink - m)
        o = jnp.dot(p.astype(BF16), vh, preferred_element_type=F32) / den
        outs.extend(o[g * r:(g + 1) * r] for g in range(ATT_GROUP))
    return jnp.concatenate(outs, axis=1)


QB = 2 * CHUNK


def _attn_prompt_kernel(sink_ref, q_ref, kp_ref, kc_ref, vp_ref, vc_ref, o_ref):
    m = pl.program_id(1)
    nseq = q_ref.shape[0]
    span = 2 * QB
    lane = lax.broadcasted_iota(jnp.int32, (span, KV_WIDTH), 1)
    row_c = 2 + lax.broadcasted_iota(jnp.int32, (QB, span), 0) // CHUNK
    col_c = lax.broadcasted_iota(jnp.int32, (QB, span), 1) // CHUNK
    back = row_c - col_c
    valid = jnp.logical_and(jnp.logical_and(back >= 0, back <= 2), 2 * m - 2 + col_c >= 0)
    out_lane = lax.broadcasted_iota(jnp.int32, (QB, PAIR), 1)

    def dup(x, kv):
        sw = pltpu.roll(x, HEAD_DIM, 1)
        return jnp.where(lane < HEAD_DIM, x, sw) if kv == 0 else jnp.where(lane < HEAD_DIM, sw, x)

    k_st, v_st = [], []
    for b in range(nseq):
        k_all = jnp.concatenate([kp_ref[b], kc_ref[b]], axis=0)
        v_all = jnp.concatenate([vp_ref[b], vc_ref[b]], axis=0)
        k_st.append([_stack_heads(dup(k_all, kv), HEAD_DIM).astype(BF16) for kv in range(ATT_KV_HEADS)])
        v_st.append([_stack_heads(dup(v_all, kv), HEAD_DIM).astype(BF16) for kv in range(ATT_KV_HEADS)])
    pairs = [(b, j) for b in range(nseq) for j in range(ATT_HEADS // 2)]
    kv_of = lambda j: (2 * j) // ATT_GROUP
    s = [lax.dot_general(q_ref[b, :, j * PAIR:(j + 1) * PAIR].astype(BF16), k_st[b][kv_of(j)],
                         (((1,), (1,)), ((), ())), preferred_element_type=F32) * ATT_SCALE for b, j in pairs]
    probs, dens = [], []
    for idx, (b, j) in enumerate(pairs):
        halves, den = [], []
        for e in range(2):
            sink = sink_ref[2 * j + e]
            sh = jnp.where(valid, s[idx][:, e * span:(e + 1) * span], -jnp.inf)
            mx = jnp.maximum(jnp.max(sh, -1, keepdims=True), sink)
            ph = jnp.exp(sh - mx)
            halves.append(ph)
            den.append(jnp.sum(ph, -1, keepdims=True) + jnp.exp(sink - mx))
        probs.append(jnp.concatenate(halves, axis=1).astype(BF16))
        dens.append(jnp.where(out_lane < HEAD_DIM, den[0], den[1]))
    o = [jnp.dot(probs[idx], v_st[b][kv_of(j)], preferred_element_type=F32) / dens[idx]
         for idx, (b, j) in enumerate(pairs)]
    n_pairs = ATT_HEADS // 2
    for b in range(nseq):
        o_ref[b] = jnp.concatenate(o[b * n_pairs:(b + 1) * n_pairs], axis=1).astype(o_ref.dtype)


def _attn_prompt(p3, sinks, nseq):
    batch, seq = p3.shape[0], p3.shape[1]
    nb = seq // QB
    kcol, vcol = P_AK // KV_WIDTH, P_AV // KV_WIDTH
    prev = lambda col: pl.BlockSpec((nseq, QB, KV_WIDTH), lambda g, m: (g, jnp.maximum(m - 1, 0), col))
    cur = lambda col: pl.BlockSpec((nseq, QB, KV_WIDTH), lambda g, m: (g, m, col))
    return pl.pallas_call(
        _attn_prompt_kernel,
        grid=(batch // nseq, nb),
        in_specs=[
            pl.BlockSpec(memory_space=pltpu.SMEM),
            pl.BlockSpec((nseq, QB, ATT_WIDTH), lambda g, m: (g, m, P_Q // ATT_WIDTH)),
            prev(kcol), cur(kcol), prev(vcol), cur(vcol),
        ],
        out_specs=pl.BlockSpec((nseq, QB, ATT_WIDTH), lambda g, m: (g, m, 0)),
        out_shape=jax.ShapeDtypeStruct((batch, seq, ATT_WIDTH), BF16),
        compiler_params=_cparams(("arbitrary", "arbitrary")),
        name="attn_prompt",
    )(sinks, p3, p3, p3, p3, p3)


def _attn_sample_kernel(sink_ref, q_ref, kn_ref, vn_ref, kc_ref, vc_ref, o_ref, kw_ref, vw_ref):
    t = q_ref.shape[0]
    k = jnp.concatenate([kc_ref[...], kn_ref[...]], axis=0)
    v = jnp.concatenate([vc_ref[...], vn_ref[...]], axis=0)
    o_ref[...] = _attend(q_ref[...], k, v, None, sink_ref).astype(o_ref.dtype)
    kw_ref[...] = k[t:]
    vw_ref[...] = v[t:]


def _attn_sample(p, sinks, cache_k, cache_v, batch, seq):
    kcol, vcol = P_AK // KV_WIDTH, P_AV // KV_WIDTH
    return pl.pallas_call(
        _attn_sample_kernel,
        grid=(batch,),
        in_specs=[
            pl.BlockSpec(memory_space=pltpu.SMEM),
            pl.BlockSpec((seq, ATT_WIDTH), lambda b: (b, P_Q // ATT_WIDTH)),
            pl.BlockSpec((seq, KV_WIDTH), lambda b: (b, kcol)),
            pl.BlockSpec((seq, KV_WIDTH), lambda b: (b, vcol)),
            pl.BlockSpec((WINDOW, KV_WIDTH), lambda b: (b, 0)),
            pl.BlockSpec((WINDOW, KV_WIDTH), lambda b: (b, 0)),
        ],
        out_specs=[
            pl.BlockSpec((seq, ATT_WIDTH), lambda b: (b, 0)),
            pl.BlockSpec((WINDOW, KV_WIDTH), lambda b: (b, 0)),
            pl.BlockSpec((WINDOW, KV_WIDTH), lambda b: (b, 0)),
        ],
        out_shape=[
            jax.ShapeDtypeStruct((batch * seq, ATT_WIDTH), BF16),
            jax.ShapeDtypeStruct((batch * WINDOW, KV_WIDTH), F32),
            jax.ShapeDtypeStruct((batch * WINDOW, KV_WIDTH), F32),
        ],
        compiler_params=_cparams(("arbitrary",)),
        name="attn_sample",
    )(sinks, p, p, p, cache_k, cache_v)


SEG = 256
EXP_NEG_HALF = math.exp(-0.5)
PAIR = 2 * HEAD_DIM
N_PAIRS = RW_HEADS // 2


def _split2(x):
    hi = x.astype(BF16)
    lo = (x - hi.astype(F32)).astype(BF16)
    return hi, lo


def _seg_sum(x, bd):
    c = x.shape[0]
    xb = x.astype(BF16)
    stack = jnp.concatenate([xb[:, s:s + SEG] for s in range(0, RW_WIDTH, SEG)], axis=0)
    r = jnp.dot(stack, bd, preferred_element_type=F32)
    return jnp.concatenate([r[i * c:(i + 1) * c] for i in range(RW_WIDTH // SEG)], axis=1)


def _shift_rows(x, prev_row):
    row = lax.broadcasted_iota(jnp.int32, x.shape, 0)
    return jnp.where(row == 0, prev_row, pltpu.roll(x, 1, 0))


def _dotb(a, b):
    return jnp.dot(a.astype(BF16), b.astype(BF16), preferred_element_type=F32)


def _dotb_nt(a, b):
    return lax.dot_general(a.astype(BF16), b.astype(BF16), (((1,), (1,)), ((), ())), preferred_element_type=F32)


def _dot3_tn(a, b):
    ah, al = _split2(a)
    bh, bl = _split2(b)
    return lax.dot_general(jnp.concatenate([ah, al, ah], axis=0), jnp.concatenate([bh, bh, bl], axis=0),
                           (((0,), (0,)), ((), ())), preferred_element_type=F32)


def _dot2_tn(a, b):
    ah, al = _split2(a)
    bh = b.astype(BF16)
    return lax.dot_general(jnp.concatenate([ah, al], axis=0), jnp.concatenate([bh, bh], axis=0),
                           (((0,), (0,)), ((), ())), preferred_element_type=F32)


def _dot2(a, b):
    ah, al = _split2(a)
    bh = b.astype(BF16)
    return jnp.dot(jnp.concatenate([ah, al], axis=1), jnp.concatenate([bh, bh], axis=0), preferred_element_type=F32)


def _stack_heads(x, half):
    lane = lax.broadcasted_iota(jnp.int32, x.shape, 1)
    first = (lane % (2 * half)) < half
    zero = jnp.zeros_like(x)
    return jnp.concatenate([jnp.where(first, x, zero), jnp.where(first, zero, x)], axis=0)


def _stack_heads_bf16(x, half):
    return _stack_heads(x.astype(BF16), half)


def _rwkv_pairs(ab, rb, bb, kb, v, gam_c, s_bd, n_double):
    c = ab[0].shape[0]
    pairs = range(len(ab))
    t_idx = lax.broadcasted_iota(jnp.int32, (c, 2 * c), 0)
    s_idx = lax.broadcasted_iota(jnp.int32, (c, 2 * c), 1) % c
    strict, incl = s_idx < t_idx, s_idx <= t_idx
    g = [_dotb_nt(jnp.concatenate([ab[p], rb[p]], axis=0),
                  jnp.concatenate([_stack_heads_bf16(bb[p], HEAD_DIM), _stack_heads_bf16(kb[p], HEAD_DIM)], axis=0))
         for p in pairs]
    l_ak = [jnp.where(strict, g[p][:c, 2 * c:], 0.0) for p in pairs]
    m_rbk = [jnp.concatenate([jnp.where(incl, g[p][c:, :2 * c], 0.0), jnp.where(incl, g[p][c:, 2 * c:], 0.0)], axis=1)
             for p in pairs]
    lp = [jnp.where(strict, g[p][:c, :2 * c], 0.0) for p in pairs]
    v_st = [_stack_heads_bf16(v[p], HEAD_DIM) for p in pairs]
    ya = list(ab)
    yu = [_dotb(l_ak[p], v_st[p]) for p in pairs]
    for i in range(n_double):
        y_st = [_stack_heads_bf16(jnp.concatenate([ya[p], yu[p]], axis=1), HEAD_DIM) for p in pairs]
        if i + 1 < n_double:
            z = [_dotb(lp[p], jnp.concatenate([y_st[p], _stack_heads_bf16(lp[p], c)], axis=1)) for p in pairs]
            lp = [z[p][:, 2 * PAIR:] for p in pairs]
        else:
            z = [_dotb(lp[p], y_st[p]) for p in pairs]
        ya = [ya[p] + z[p][:, :PAIR] for p in pairs]
        yu = [yu[p] + z[p][:, PAIR:2 * PAIR] for p in pairs]
    z = []
    for p in pairs:
        top = _stack_heads_bf16(jnp.concatenate([ya[p], yu[p]], axis=1), HEAD_DIM)
        bot = jnp.concatenate([jnp.zeros_like(v_st[p]), v_st[p]], axis=1)
        z.append(_dotb(m_rbk[p], jnp.concatenate([top, bot], axis=0)))
    o = [_dotb_nt(rb[p] + z[p][:, :PAIR], s_bd[p]) + z[p][:, PAIR:] for p in pairs]
    row_h = lax.broadcasted_iota(jnp.int32, (PAIR, PAIR), 0) // HEAD_DIM
    col_h = lax.broadcasted_iota(jnp.int32, (PAIR, PAIR), 1) // HEAD_DIM
    same = row_h == col_h
    q_t = [jnp.where(same, _dot3_tn(ya[p], bb[p]), 0.0) for p in pairs]
    d_t = [jnp.where(same, _dot2_tn(jnp.concatenate([yu[p], v[p]], axis=0),
                                    jnp.concatenate([bb[p], kb[p]], axis=0)), 0.0) for p in pairs]
    s_new = [(s_bd[p] + _dot2(s_bd[p], q_t[p]) + d_t[p]) * gam_c[p] for p in pairs]
    return o, s_new


def _split_rw_rows(x):
    o_wd, o_gd = RW_WIDTH, RW_COLS - GATE_LORA
    low = lax.broadcasted_iota(jnp.int32, (x.shape[0], LANES), 1) < HEAD_DIM
    pieces = [x[:, c0:c0 + LANES] for c0 in range(o_wd, o_gd, LANES)]
    rolled = [pltpu.roll(p, HEAD_DIM, 1) for p in pieces]
    merged = [jnp.where(low, rolled[m], rolled[m + 1]) for m in range(len(pieces) - 1)]
    n = RW_WIDTH // LANES
    return (x[:, :RW_WIDTH], jnp.concatenate(merged[:n], axis=1), jnp.concatenate(merged[n:], axis=1),
            jnp.where(low, pieces[0], pieces[-1]), x[:, o_gd:])


def _rwkv_kernel(xr_ref, xk_ref, xv_ref, xl_ref, sh_ref, mu_ref, w0_ref, a0_ref, kk_ref, ka_ref, rk_ref,
                 lg_ref, lb_ref, w2_ref, a2_ref, g2_ref, bd_ref, st0_ref,
                 out_ref, st_ref,
                 s_scr, pr_scr, pk_scr, pv_scr, pw_scr, pg_scr):
    c = pl.program_id(1)
    nc = pl.num_programs(1)
    nb, C = xr_ref.shape[0], xr_ref.shape[1]
    n_double = int(math.log2(C))
    prev_scr = (pr_scr, pk_scr, pv_scr, pw_scr, pg_scr)

    @pl.when(c == 0)
    def _():
        s_scr[...] = jnp.zeros_like(s_scr)
        for b in range(nb):
            for h in range(RW_HEADS):
                off = (h % 2) * HEAD_DIM
                s_scr[b * N_PAIRS + h // 2, off:off + HEAD_DIM, off:off + HEAD_DIM] = st0_ref[b, h]
            for scr, row in zip(prev_scr, _split_rw_rows(sh_ref[b])):
                scr[b, 0:1, :] = row

    def shifted(load, p_scr, mu):
        parts = []
        for b in range(nb):
            x = load(b)
            prev = _shift_rows(x, p_scr[b, 0:1, :])
            p_scr[b, 0:1, :] = x[C - 1:C, :]
            parts.append(x + (prev - x) * mu)
        return jnp.concatenate(parts, axis=0)

    mu_r, mu_k, mu_v, mu_w, mu_g = _split_rw_rows(mu_ref[...])
    wa_w = DECAY_LORA + AAA_LORA
    r = shifted(lambda b: xr_ref[b], pr_scr, mu_r)
    kr = shifted(lambda b: xk_ref[b], pk_scr, mu_k)
    v = shifted(lambda b: xv_ref[b], pv_scr, mu_v)
    xw = shifted(lambda b: xl_ref[b, :, :wa_w], pw_scr, mu_w)
    gd = shifted(lambda b: xl_ref[b, :, wa_w:wa_w + GATE_LORA], pg_scr, mu_g)
    wd = xw[:, 0:DECAY_LORA]
    ad = xw[:, DECAY_LORA:wa_w]

    lw = -EXP_NEG_HALF * jax.nn.sigmoid(w0_ref[...] + _dotb(jnp.tanh(wd), w2_ref[...]))
    a = jax.nn.sigmoid(a0_ref[...] + _dotb(ad, a2_ref[...]))
    g = _dotb(jax.nn.sigmoid(gd), g2_ref[...])
    bd = bd_ref[...]
    kk = kr * kk_ref[...]
    kk = kk * lax.rsqrt(jnp.maximum(_seg_sum(kk * kk, bd), 1e-24))
    k2 = kr * (1.0 + (a - 1.0) * ka_ref[...])

    rows = nb * C
    ti = lax.broadcasted_iota(jnp.int32, (rows, rows), 0)
    si = lax.broadcasted_iota(jnp.int32, (rows, rows), 1)
    tri = jnp.logical_and(si <= ti, si // C == ti // C).astype(BF16)
    cs2 = jnp.dot(tri, jnp.concatenate(_split2(lw), axis=1), preferred_element_type=F32)
    cs = cs2[:, :RW_WIDTH] + cs2[:, RW_WIDTH:]
    gam = jnp.exp(cs)
    ginv = jnp.exp(-cs)
    ab = -kk * jnp.exp(cs - lw)
    rb = r * gam
    bb = kk * a * ginv
    kb = k2 * ginv

    def per_pair(x, r0, r1):
        return [x[b * C + r0:b * C + r1, p * PAIR:(p + 1) * PAIR] for b in range(nb) for p in range(N_PAIRS)]

    chains = nb * N_PAIRS
    outs, s_new = _rwkv_pairs(per_pair(ab, 0, C), per_pair(rb, 0, C), per_pair(bb, 0, C), per_pair(kb, 0, C),
                              per_pair(v, 0, C), per_pair(gam, C - 1, C), [s_scr[q] for q in range(chains)], n_double)
    for q in range(chains):
        s_scr[q] = s_new[q]
    o = jnp.concatenate([jnp.concatenate(outs[b * N_PAIRS:(b + 1) * N_PAIRS], axis=1) for b in range(nb)], axis=0)

    mo = _seg_sum(o, bd) * (1.0 / HEAD_DIM)
    d = o - mo
    vo = _seg_sum(d * d, bd) * (1.0 / HEAD_DIM)
    on = d * lax.rsqrt(vo + LNX_EPS) * lg_ref[...] + lb_ref[...]
    bonus = _seg_sum(r * k2 * rk_ref[...], bd) * v
    res = ((on + bonus) * g).astype(out_ref.dtype)
    for b in range(nb):
        out_ref[b] = res[b * C:(b + 1) * C, :]

    @pl.when(c == nc - 1)
    def _():
        for b in range(nb):
            for h in range(RW_HEADS):
                off = (h % 2) * HEAD_DIM
                st_ref[b, h] = s_scr[b * N_PAIRS + h // 2, off:off + HEAD_DIM, off:off + HEAD_DIM]


def _rwkv(p3, state_shift, state0, prm, chunk, nb):
    batch, seq = p3.shape[0], p3.shape[1]
    nc = seq // chunk
    row = lambda w: pl.BlockSpec((1, w), lambda g, c: (0, 0))
    xspec = lambda w, col: pl.BlockSpec((nb, chunk, w), lambda g, c: (g, c, col))
    full = lambda a: pl.BlockSpec(a.shape, lambda g, c: (0,) * a.ndim)
    stspec = pl.BlockSpec((nb, RW_HEADS, HEAD_DIM, HEAD_DIM), lambda g, c: (g, 0, 0, 0))
    return pl.pallas_call(
        _rwkv_kernel,
        grid=(batch // nb, nc),
        in_specs=[
            xspec(RW_WIDTH, P_R // RW_WIDTH), xspec(RW_WIDTH, P_K // RW_WIDTH), xspec(RW_WIDTH, P_V // RW_WIDTH),
            xspec(LORA_W, P_LORA // LORA_W),
            pl.BlockSpec((nb, 1, RW_COLS), lambda g, c: (g, 0, 0)),
            row(RW_COLS),
            row(RW_WIDTH), row(RW_WIDTH), row(RW_WIDTH), row(RW_WIDTH), row(RW_WIDTH),
            row(RW_WIDTH), row(RW_WIDTH),
            full(prm["w2"]), full(prm["a2"]), full(prm["g2"]), full(prm["bd"]),
            stspec,
        ],
        out_specs=[
            pl.BlockSpec((nb, chunk, RW_WIDTH), lambda g, c: (g, c, 0)),
            stspec,
        ],
        out_shape=[
            jax.ShapeDtypeStruct((batch, seq, RW_WIDTH), BF16),
            jax.ShapeDtypeStruct((batch, RW_HEADS, HEAD_DIM, HEAD_DIM), F32),
        ],
        scratch_shapes=[
            pltpu.VMEM((nb * N_PAIRS, PAIR, PAIR), F32),
            pltpu.VMEM((nb, SUBLANES, RW_WIDTH), F32), pltpu.VMEM((nb, SUBLANES, RW_WIDTH), F32),
            pltpu.VMEM((nb, SUBLANES, RW_WIDTH), F32),
            pltpu.VMEM((nb, SUBLANES, DECAY_LORA + AAA_LORA), F32), pltpu.VMEM((nb, SUBLANES, GATE_LORA), F32),
        ],
        compiler_params=_cparams(("arbitrary", "arbitrary")),
        name="rwkv",
    )(p3, p3, p3, p3, state_shift, prm["mu"], prm["w0"], prm["a0"], prm["k_k"], prm["k_a"], prm["r_k"],
      prm["lnx_g"], prm["lnx_b"], prm["w2"], prm["a2"], prm["g2"], prm["bd"], state0)


D_TN = 512


def _out_proj_kernel(att_ref, rw_ref, wa_ref, wb_ref, h_ref, g_ref, b_ref, h1_ref, h1b_ref, w_scr):
    @pl.when(pl.program_id(0) == 0)
    def _():
        w_scr[:ATT_WIDTH, :] = wa_ref[...].astype(BF16)
        w_scr[ATT_WIDTH:, :] = wb_ref[...].astype(BF16)

    for c0 in range(0, D_MODEL, D_TN):
        m = jnp.dot(att_ref[...], w_scr[:ATT_WIDTH, c0:c0 + D_TN], preferred_element_type=F32)
        m = m + jnp.dot(rw_ref[...], w_scr[ATT_WIDTH:, c0:c0 + D_TN], preferred_element_type=F32)
        h1_ref[:, c0:c0 + D_TN] = ALPHA * h_ref[:, c0:c0 + D_TN] + m
    y = _layer_norm(h1_ref[...], g_ref[...], b_ref[...])
    h1_ref[...] = y
    h1b_ref[...] = y.astype(BF16)


def _out_proj(att, rw, w_out, h, g, b, tm):
    n = att.shape[0]
    return pl.pallas_call(
        _out_proj_kernel,
        grid=(n // tm,),
        in_specs=[
            pl.BlockSpec((tm, ATT_WIDTH), lambda i: (i, 0)),
            pl.BlockSpec((tm, RW_WIDTH), lambda i: (i, 0)),
            pl.BlockSpec((ATT_WIDTH, D_MODEL), lambda i: (0, 0), pipeline_mode=pl.Buffered(1)),
            pl.BlockSpec((RW_WIDTH, D_MODEL), lambda i: (1, 0), pipeline_mode=pl.Buffered(1)),
            pl.BlockSpec((tm, D_MODEL), lambda i: (i, 0)),
            pl.BlockSpec((1, D_MODEL), lambda i: (0, 0)),
            pl.BlockSpec((1, D_MODEL), lambda i: (0, 0)),
        ],
        out_specs=[
            pl.BlockSpec((tm, D_MODEL), lambda i: (i, 0)),
            pl.BlockSpec((tm, D_MODEL), lambda i: (i, 0)),
        ],
        out_shape=[
            jax.ShapeDtypeStruct((n, D_MODEL), F32),
            jax.ShapeDtypeStruct((n, D_MODEL), BF16),
        ],
        scratch_shapes=[pltpu.VMEM((ATT_WIDTH + RW_WIDTH, D_MODEL), BF16)],
        compiler_params=_cparams(("arbitrary",)),
        name="out_proj",
    )(att, rw, w_out, w_out, h, g, b)


E_TN = 512
E_NCB = D_FF // E_TN
E_SUB = 512
SQRT_HALF = math.sqrt(0.5)


def _ffn_up_kernel(bps, nseq, cast_down, h_ref, wg_ref, wv_ref, cwg_ref, cwv_ref, cbg_ref, cbv_ref, sg_ref, sv_ref,
                   *rest):
    if cast_down:
        wd_ref, act_ref, og_ref, ov_ref, wdb_ref, w_scr, cg_scr, cv_scr = rest
        wdb_ref[...] = wd_ref[...].astype(BF16)
    else:
        act_ref, og_ref, ov_ref, w_scr, cg_scr, cv_scr = rest
    i = pl.program_id(1)
    tm = h_ref.shape[0]
    sub = min(tm, E_SUB)
    n_sub = tm // sub
    t_seq = tm // nseq
    row8 = lax.broadcasted_iota(jnp.int32, (SUBLANES, E_TN), 0)

    @pl.when(i == 0)
    def _():
        w_scr[:, :E_TN] = wg_ref[...].astype(BF16)
        w_scr[:, E_TN:] = wv_ref[...].astype(BF16)

    def products(r0):
        u = jnp.dot(h_ref[r0:r0 + sub, :], w_scr[...], preferred_element_type=F32)
        return u[:, :E_TN], u[:, E_TN:]

    def conv(u, prevs, cw_ref, cb_ref):
        sh1, sh2 = pltpu.roll(u, 1, 0), pltpu.roll(u, 2, 0)
        for r, prev in prevs:
            p2, p1 = prev[0:1, :], prev[1:2, :]
            fix1 = jnp.where(row8 == 0, p1, sh1[r:r + SUBLANES, :])
            fix2 = jnp.where(row8 == 0, p2, jnp.where(row8 == 1, p1, sh2[r:r + SUBLANES, :]))
            head = lambda x: [x[:r, :]] if r > 0 else []
            tail = lambda x: [x[r + SUBLANES:, :]] if r + SUBLANES < x.shape[0] else []
            sh1 = jnp.concatenate(head(sh1) + [fix1] + tail(sh1), axis=0)
            sh2 = jnp.concatenate(head(sh2) + [fix2] + tail(sh2), axis=0)
        cw = cw_ref[...]
        c = cb_ref[...] + sh2 * cw[0:1, :]
        c = c + sh1 * cw[1:2, :]
        return c + u * cw[2:3, :]

    def epilogue(r0, ug, uv, prev_g, prev_v):
        gate = conv(ug, prev_g, cwg_ref, cbg_ref)
        val = conv(uv, prev_v, cwv_ref, cbv_ref)
        gelu = 0.5 * gate * (1.0 + lax.erf(gate * SQRT_HALF))
        act_ref[r0:r0 + sub, :] = (gelu * val).astype(act_ref.dtype)

    if nseq > 1:
        ug, uv = products(0)
        epilogue(0, ug, uv, [(s * t_seq, sg_ref[s]) for s in range(nseq)],
                 [(s * t_seq, sv_ref[s]) for s in range(nseq)])
        for s in range(nseq):
            og_ref[s] = ug[(s + 1) * t_seq - SUBLANES:(s + 1) * t_seq, :]
            ov_ref[s] = uv[(s + 1) * t_seq - SUBLANES:(s + 1) * t_seq, :]
    else:
        first = (i % bps) == 0
        prev_g = jnp.where(first, sg_ref[0], cg_scr[SUBLANES - 2:SUBLANES, :])
        prev_v = jnp.where(first, sv_ref[0], cv_scr[SUBLANES - 2:SUBLANES, :])
        nxt = products(0)
        for k in range(n_sub):
            ug, uv = nxt
            if k + 1 < n_sub:
                nxt = products((k + 1) * sub)
            epilogue(k * sub, ug, uv, [(0, prev_g)], [(0, prev_v)])
            prev_g, prev_v = ug[sub - 2:sub, :], uv[sub - 2:sub, :]
        cg_scr[...] = ug[sub - SUBLANES:sub, :]
        cv_scr[...] = uv[sub - SUBLANES:sub, :]
        og_ref[0] = ug[sub - SUBLANES:sub, :]
        ov_ref[0] = uv[sub - SUBLANES:sub, :]


def _ffn_up(h1b, w_up, conv_w, conv_b, conv_prev, batch, seq, tm, w_down=None):
    n = h1b.shape[0]
    nseq = max(tm // seq, 1)
    bps = max(seq // tm, 1)
    nrb = n // tm
    in_specs = [
        pl.BlockSpec((tm, D_MODEL), lambda j, i: (i, 0)),
        pl.BlockSpec((D_MODEL, E_TN), lambda j, i: (0, j)),
        pl.BlockSpec((D_MODEL, E_TN), lambda j, i: (0, j + E_NCB)),
        pl.BlockSpec((CONV_W, E_TN), lambda j, i: (0, j)),
        pl.BlockSpec((CONV_W, E_TN), lambda j, i: (0, j + E_NCB)),
        pl.BlockSpec((1, E_TN), lambda j, i: (0, j)),
        pl.BlockSpec((1, E_TN), lambda j, i: (0, j + E_NCB)),
        pl.BlockSpec((nseq, CONV_W - 1, E_TN), lambda j, i: (i // bps, 0, j)),
        pl.BlockSpec((nseq, CONV_W - 1, E_TN), lambda j, i: (i // bps, 0, j + E_NCB)),
    ]
    out_specs = [
        pl.BlockSpec((tm, E_TN), lambda j, i: (i, j)),
        pl.BlockSpec((nseq, SUBLANES, E_TN), lambda j, i: (i // bps, 0, j)),
        pl.BlockSpec((nseq, SUBLANES, E_TN), lambda j, i: (i // bps, 0, j)),
    ]
    out_shape = [
        jax.ShapeDtypeStruct((n, D_FF), BF16),
        jax.ShapeDtypeStruct((batch, SUBLANES, D_FF), F32),
        jax.ShapeDtypeStruct((batch, SUBLANES, D_FF), F32),
    ]
    args = [h1b, w_up, w_up, conv_w, conv_w, conv_b, conv_b, conv_prev, conv_prev]
    if w_down is not None:
        slab = D_FF // (E_NCB * nrb)
        spec = pl.BlockSpec((slab, D_MODEL), lambda j, i: (j * nrb + i, 0))
        in_specs.append(spec)
        out_specs.append(spec)
        out_shape.append(jax.ShapeDtypeStruct((D_FF, D_MODEL), BF16))
        args.append(w_down)
    return pl.pallas_call(
        functools.partial(_ffn_up_kernel, bps, nseq, w_down is not None),
        grid=(E_NCB, nrb),
        in_specs=in_specs,
        out_specs=out_specs,
        out_shape=out_shape,
        scratch_shapes=[pltpu.VMEM((D_MODEL, 2 * E_TN), BF16),
                        pltpu.VMEM((SUBLANES, E_TN), F32), pltpu.VMEM((SUBLANES, E_TN), F32)],
        compiler_params=_cparams(("arbitrary", "arbitrary")),
        name="ffn_up",
    )(*args)


F_TN = 512


def _ffn_down_kernel(act_ref, w_ref, h1_ref, g_ref, b_ref, y_ref):
    for c0 in range(0, D_MODEL, F_TN):
        f = jnp.dot(act_ref[...], w_ref[:, c0:c0 + F_TN], preferred_element_type=F32)
        y_ref[:, c0:c0 + F_TN] = ALPHA * h1_ref[:, c0:c0 + F_TN] + f
    y_ref[...] = _layer_norm(y_ref[...], g_ref[...], b_ref[...])


def _ffn_down(act, w_down, h1, g, b, tm):
    n = act.shape[0]
    return pl.pallas_call(
        _ffn_down_kernel,
        grid=(n // tm,),
        in_specs=[
            pl.BlockSpec((tm, D_FF), lambda i: (i, 0)),
            pl.BlockSpec((D_FF, D_MODEL), lambda i: (0, 0), pipeline_mode=pl.Buffered(1)),
            pl.BlockSpec((tm, D_MODEL), lambda i: (i, 0)),
            pl.BlockSpec((1, D_MODEL), lambda i: (0, 0)),
            pl.BlockSpec((1, D_MODEL), lambda i: (0, 0)),
        ],
        out_specs=pl.BlockSpec((tm, D_MODEL), lambda i: (i, 0)),
        out_shape=jax.ShapeDtypeStruct((n, D_MODEL), F32),
        compiler_params=_cparams(("arbitrary",)),
        name="ffn_down",
    )(act, w_down, h1, g, b)


def _rope_tables(pos, reps):
    half = HEAD_DIM // 2
    inv = ROPE_THETA ** (-jnp.arange(half, dtype=F32) / half)
    ang = pos.astype(F32)[:, None] * inv[None, :]
    cos, sin = jnp.cos(ang), jnp.sin(ang)
    cos128 = jnp.concatenate([cos, cos, cos, cos], axis=1)
    sin128 = jnp.concatenate([-sin, sin, -sin, sin], axis=1)
    return jnp.tile(cos128, (reps, 1)), jnp.tile(sin128, (reps, 1))


def _run(x, pos_tables, table_blocks, attn_fn, state_wkv, state_shift, state_conv, wts, tiles, w_down_b=None):
    batch, seq = x.shape[0], x.shape[1]
    n = batch * seq
    cos, sin = pos_tables
    p, h, tails = _in_proj(x.reshape(n, D_MODEL), wts["ln_in_g"], wts["ln_in_b"], wts["w_in_p"], cos, sin,
                           tiles["a_tm"], table_blocks, seq)
    att, k_win, v_win = attn_fn(p)
    p3 = p.reshape(batch, seq, P_COLS)
    rw, wkv = _rwkv(p3, state_shift, state_wkv, wts["rw"], tiles["chunk"], tiles["c_nb"])
    h1, h1b = _out_proj(att, rw.reshape(n, RW_WIDTH), wts["w_out"], h, wts["ln1_g"], wts["ln1_b"], tiles["d_tm"])
    if w_down_b is None:
        act, cg, cv, w_down_b = _ffn_up(h1b, wts["w_up"], wts["conv_w"], wts["conv_b"], state_conv, batch, seq,
                                        tiles["e_tm"], wts["w_down"])
    else:
        act, cg, cv = _ffn_up(h1b, wts["w_up"], wts["conv_w"], wts["conv_b"], state_conv, batch, seq, tiles["e_tm"])
    y = _ffn_down(act, w_down_b, h1, wts["ln2_g"], wts["ln2_b"], tiles["f_tm"])
    shift_out = tails.reshape(batch, -1, RW_COLS)[:, -1:, :]
    conv_out = jnp.concatenate([cg[:, SUBLANES - 2:, :], cv[:, SUBLANES - 2:, :]], axis=-1)
    if k_win is None:
        k_win = p3[:, seq - WINDOW:, P_AK:P_AK + KV_WIDTH]
        v_win = p3[:, seq - WINDOW:, P_AV:P_AV + KV_WIDTH]
    kv_shape = (1, batch, WINDOW, ATT_KV_HEADS, HEAD_DIM)
    return (y.reshape(batch, seq, D_MODEL), k_win.reshape(kv_shape), v_win.reshape(kv_shape),
            wkv[None], shift_out[None], conv_out[None]), w_down_b


def kernel(x_prompt, x_sample, cache_k, cache_v, state_wkv, state_shift, state_ffn_conv, ln_in_g, ln_in_b, w_in, attn_sinks, rw_mu, rw_w0, rw_w2, rw_a0, rw_a2, rw_g2, rw_k_k, rw_k_a, rw_r_k, rw_lnx_g, rw_lnx_b, w_out, ln1_g, ln1_b, ffn_w_up, ffn_conv_w, ffn_conv_b, ffn_w_down, ln2_g, ln2_b):
    l = 0
    w_in_p = w_in[l].astype(BF16)
    seg = lax.broadcasted_iota(jnp.int32, (SEG, SEG), 0) // HEAD_DIM
    bd = (seg == seg.T).astype(BF16)
    row = lambda a: a.reshape(1, -1)
    wts = {
        "ln_in_g": row(ln_in_g), "ln_in_b": row(ln_in_b), "w_in_p": w_in_p,
        "rw": {"mu": row(rw_mu[l]),
               "w0": row(rw_w0[l]), "a0": row(rw_a0[l]), "k_k": row(rw_k_k[l]), "k_a": row(rw_k_a[l]),
               "r_k": row(rw_r_k[l]), "lnx_g": row(rw_lnx_g[l]), "lnx_b": row(rw_lnx_b[l]),
               "w2": rw_w2[l].astype(BF16), "a2": rw_a2[l].astype(BF16), "g2": rw_g2[l].astype(BF16), "bd": bd},
        "w_out": w_out[l], "ln1_g": row(ln1_g[l]), "ln1_b": row(ln1_b[l]),
        "w_up": ffn_w_up[l], "conv_w": ffn_conv_w[l], "conv_b": row(ffn_conv_b[l]),
        "w_down": ffn_w_down[l], "ln2_g": row(ln2_g[l]), "ln2_b": row(ln2_b[l]),
    }
    sinks = attn_sinks[l]

    bp, tp = x_prompt.shape[0], x_prompt.shape[1]
    p_tiles = {"a_tm": 256, "chunk": CHUNK, "c_nb": 4, "d_tm": 256, "e_tm": 1024, "f_tm": 256}
    p_tables = _rope_tables(jnp.arange(tp, dtype=jnp.int32), 1)
    zeros = lambda *s: jnp.zeros(s, F32)
    (y_p, p_k, p_v, p_wkv, p_shift, p_conv), w_down_b = _run(
        x_prompt, p_tables, tp // p_tiles["a_tm"],
        lambda p: (_attn_prompt(p.reshape(bp, tp, P_COLS), sinks, 2).reshape(bp * tp, ATT_WIDTH), None, None),
        zeros(bp, RW_HEADS, HEAD_DIM, HEAD_DIM), zeros(bp, 1, RW_COLS), zeros(bp, CONV_W - 1, 2 * D_FF),
        wts, p_tiles)

    bs, ts = x_sample.shape[0], x_sample.shape[1]
    s_tiles = {"a_tm": bs * ts, "chunk": ts, "c_nb": 4, "d_tm": bs * ts, "e_tm": bs * ts, "f_tm": bs * ts}
    s_tables = _rope_tables(PAST_LEN + jnp.arange(ts, dtype=jnp.int32), bs)
    ck = cache_k[l].reshape(bs * WINDOW, KV_WIDTH)
    cv = cache_v[l].reshape(bs * WINDOW, KV_WIDTH)
    (y_s, s_k, s_v, s_wkv, s_shift, s_conv), _ = _run(
        x_sample, s_tables, 1,
        lambda p: _attn_sample(p, sinks, ck, cv, bs, ts),
        state_wkv[l], state_shift[l], state_ffn_conv[l], wts, s_tiles, w_down_b)

    return (y_p, y_s, p_k, p_v, p_wkv, p_shift, p_conv, s_k, s_v, s_wkv, s_shift, s_conv)
```

```python
import functools
import math

import jax
import jax.numpy as jnp
from jax import lax
from jax.experimental import pallas as pl
from jax.experimental.pallas import tpu as pltpu

D_MODEL = 2048
HEAD_DIM = 64
ATT_HEADS = 16
ATT_KV_HEADS = 2
ATT_GROUP = ATT_HEADS // ATT_KV_HEADS
ATT_WIDTH = ATT_HEADS * HEAD_DIM
KV_WIDTH = ATT_KV_HEADS * HEAD_DIM
CHUNK = 64
WINDOW = 128
ROPE_THETA = 10000.0
ATT_SCALE = HEAD_DIM ** -0.5
RW_HEADS = 16
RW_WIDTH = RW_HEADS * HEAD_DIM
DECAY_LORA = 64
AAA_LORA = 64
GATE_LORA = 160
LNX_EPS = 64e-5
ATT_COLS = ATT_WIDTH + 2 * KV_WIDTH
RW_COLS = 3 * RW_WIDTH + DECAY_LORA + AAA_LORA + GATE_LORA
D_FF = 5632
CONV_W = 3
LN_EPS = 1e-5
DEPTH = 1
ALPHA = (2 * DEPTH) ** 0.25
PAST_LEN = 1024

LORA_W = 512
P_R, P_K, P_V, P_Q = 0, 1024, 2048, 3072
P_LORA = 4096
P_AK = P_LORA + LORA_W
P_AV = P_AK + KV_WIDTH
P_COLS = P_AV + KV_WIDTH
LANES = 128
SUBLANES = 8
VMEM_LIMIT = 56 * 1024 * 1024

F32 = jnp.float32
BF16 = jnp.bfloat16


def _cparams(sem):
    return pltpu.CompilerParams(dimension_semantics=sem, vmem_limit_bytes=VMEM_LIMIT)


def _layer_norm(y, g, b):
    mu = jnp.mean(y, -1, keepdims=True)
    d = y - mu
    var = jnp.mean(d * d, -1, keepdims=True)
    return d * lax.rsqrt(var + LN_EPS) * g + b


A_TN = 512


def _rope128(x, cos, sin_signed):
    lane = lax.broadcasted_iota(jnp.int32, x.shape, 1)
    first_half = (lane % HEAD_DIM) < (HEAD_DIM // 2)
    rot = jnp.where(first_half, pltpu.roll(x, LANES - HEAD_DIM // 2, 1), pltpu.roll(x, HEAD_DIM // 2, 1))
    return x * cos + rot * sin_signed


def _is_rope_col(col):
    return (P_Q <= col < P_Q + ATT_WIDTH) or (P_AK <= col < P_AK + KV_WIDTH)


O_R = ATT_COLS
O_WD = O_R + RW_WIDTH
O_K = O_WD + DECAY_LORA
O_V = O_K + RW_WIDTH
O_AD = O_V + RW_WIDTH
O_GD = O_AD + AAA_LORA
IN_COLS = O_GD + GATE_LORA


def _in_proj_kernel(x_ref, g_ref, b_ref, w_ref, cos_ref, sin_ref, p_ref, h_ref, tail_ref, hb_ref):
    h = _layer_norm(x_ref[...], g_ref[...], b_ref[...])
    h_ref[...] = h
    hb_ref[...] = h.astype(BF16)
    nseq = tail_ref.shape[0]
    t_seq = x_ref.shape[0] // nseq

    def product(c0, width):
        acc = jnp.dot(hb_ref[...], w_ref[:, c0:c0 + width], preferred_element_type=F32)
        if c0 >= O_R:
            for s in range(nseq):
                tail_ref[s, :, c0 - O_R:c0 - O_R + width] = acc[(s + 1) * t_seq - SUBLANES:(s + 1) * t_seq, :]
        return acc

    for src, width, dst in ((0, ATT_WIDTH, P_Q), (ATT_WIDTH, 2 * KV_WIDTH, P_AK), (O_R, RW_WIDTH, P_R)):
        for c0 in range(0, width, A_TN):
            acc = product(src + c0, min(A_TN, width - c0))
            for l0 in range(0, acc.shape[1], LANES):
                col = dst + c0 + l0
                part = acc[:, l0:l0 + LANES]
                if _is_rope_col(col):
                    part = _rope128(part, cos_ref[...], sin_ref[...])
                p_ref[:, col:col + LANES] = part

    low = lax.broadcasted_iota(jnp.int32, (x_ref.shape[0], LANES), 1) < HEAD_DIM
    span = O_GD - O_WD
    rolled, first_piece = [], None
    for c0 in range(0, span, A_TN):
        acc = product(O_WD + c0, min(A_TN, span - c0))
        for l0 in range(0, acc.shape[1], LANES):
            piece = acc[:, l0:l0 + LANES]
            j = (c0 + l0) // LANES
            if j == 0:
                first_piece = piece
            rolled.append(pltpu.roll(piece, HEAD_DIM, 1))
            if j >= 1:
                m = j - 1
                dst = P_K + m * LANES if m < RW_WIDTH // LANES else P_V + (m - RW_WIDTH // LANES) * LANES
                p_ref[:, dst:dst + LANES] = jnp.where(low, rolled[m], rolled[m + 1])
            if j == span // LANES - 1:
                p_ref[:, P_LORA:P_LORA + LANES] = jnp.where(low, first_piece, piece)
    gd0 = P_LORA + DECAY_LORA + AAA_LORA
    p_ref[:, gd0:gd0 + GATE_LORA] = product(O_GD, GATE_LORA)
    p_ref[:, gd0 + GATE_LORA:P_LORA + LORA_W] = jnp.zeros((x_ref.shape[0], LORA_W - DECAY_LORA - AAA_LORA - GATE_LORA), F32)


def _in_proj(xp, xs, g, b, w_p, tables_p, tables_s, tm, seq_p, seq_s):
    n_p, n_s = xp.shape[0], xs.shape[0]
    assert n_s == tm
    nrb = n_p // tm
    nseq_s = tm // seq_s
    tb = seq_p // tm
    last = lambda i: jnp.minimum(i, nrb - 1)

    def body(ins, shared, outs, scratch):
        x_ref, cos_ref, sin_ref = ins
        g_ref, b_ref, w_ref = shared
        _in_proj_kernel(x_ref, g_ref, b_ref, w_ref, cos_ref, sin_ref, *outs, *scratch)

    def kern(xp_ref, cp_ref, sp_ref, xs_ref, cs_ref, ss_ref, g_ref, b_ref, w_ref,
             pp_ref, hp_ref, tp_ref, ps_ref, hs_ref, ts_ref, hb_ref):
        i = pl.program_id(0)
        pl.when(i < nrb)(lambda: body((xp_ref, cp_ref, sp_ref), (g_ref, b_ref, w_ref), (pp_ref, hp_ref, tp_ref), (hb_ref,)))
        pl.when(i == nrb)(lambda: body((xs_ref, cs_ref, ss_ref), (g_ref, b_ref, w_ref), (ps_ref, hs_ref, ts_ref), (hb_ref,)))

    const2 = lambda shape: pl.BlockSpec(shape, lambda i: (0, 0))
    return pl.pallas_call(
        kern,
        grid=(nrb + 1,),
        in_specs=[
            pl.BlockSpec((tm, D_MODEL), lambda i: (last(i), 0)),
            pl.BlockSpec((tm, LANES), lambda i: (last(i) % tb, 0)),
            pl.BlockSpec((tm, LANES), lambda i: (last(i) % tb, 0)),
            const2((tm, D_MODEL)), const2((tm, LANES)), const2((tm, LANES)),
            const2((1, D_MODEL)), const2((1, D_MODEL)),
            pl.BlockSpec((D_MODEL, IN_COLS), lambda i: (0, 0), pipeline_mode=pl.Buffered(1)),
        ],
        out_specs=[
            pl.BlockSpec((tm, P_COLS), lambda i: (last(i), 0)),
            pl.BlockSpec((tm, D_MODEL), lambda i: (last(i), 0)),
            pl.BlockSpec((1, SUBLANES, RW_COLS), lambda i: (last(i), 0, 0)),
            const2((tm, P_COLS)), const2((tm, D_MODEL)),
            pl.BlockSpec((nseq_s, SUBLANES, RW_COLS), lambda i: (0, 0, 0)),
        ],
        out_shape=[
            jax.ShapeDtypeStruct((n_p, P_COLS), F32),
            jax.ShapeDtypeStruct((n_p, D_MODEL), F32),
            jax.ShapeDtypeStruct((nrb, SUBLANES, RW_COLS), F32),
            jax.ShapeDtypeStruct((n_s, P_COLS), F32),
            jax.ShapeDtypeStruct((n_s, D_MODEL), F32),
            jax.ShapeDtypeStruct((nseq_s, SUBLANES, RW_COLS), F32),
        ],
        scratch_shapes=[pltpu.VMEM((tm, D_MODEL), BF16)],
        compiler_params=_cparams(("arbitrary",)),
        name="in_proj",
    )(xp, *tables_p, xs, *tables_s, g, b, w_p)


def _attend(q, k, v, valid, sink_ref):
    r = q.shape[0]
    outs = []
    for kv in range(ATT_KV_HEADS):
        kh = k[:, kv * HEAD_DIM:(kv + 1) * HEAD_DIM].astype(BF16)
        vh = v[:, kv * HEAD_DIM:(kv + 1) * HEAD_DIM].astype(BF16)
        heads = [kv * ATT_GROUP + g for g in range(ATT_GROUP)]
        qs = jnp.concatenate([q[:, h * HEAD_DIM:(h + 1) * HEAD_DIM] for h in heads], axis=0)
        sink = jnp.concatenate([jnp.full((r, 1), sink_ref[h], F32) for h in heads], axis=0)
        s = lax.dot_general(qs.astype(BF16), kh, (((1,), (1,)), ((), ())), preferred_element_type=F32) * ATT_SCALE
        if valid is not None:
            s = jnp.where(valid, s, -jnp.inf)
        m = jnp.maximum(jnp.max(s, -1, keepdims=True), sink)
        p = jnp.exp(s - m)
        den = jnp.sum(p, -1, keepdims=True) + jnp.exp(sink - m)
        o = jnp.dot(p.astype(BF16), vh, preferred_element_type=F32) / den
        outs.extend(o[g * r:(g + 1) * r] for g in range(ATT_GROUP))
    return jnp.concatenate(outs, axis=1)


QB = 2 * CHUNK


def _attn_prompt_kernel(sink_ref, q_ref, kp_ref, kc_ref, vp_ref, vc_ref, o_ref):
    m = pl.program_id(1)
    nseq = q_ref.shape[0]
    span = 2 * QB
    lane = lax.broadcasted_iota(jnp.int32, (span, KV_WIDTH), 1)
    row_c = 2 + lax.broadcasted_iota(jnp.int32, (QB, span), 0) // CHUNK
    col_c = lax.broadcasted_iota(jnp.int32, (QB, span), 1) // CHUNK
    back = row_c - col_c
    valid = jnp.logical_and(jnp.logical_and(back >= 0, back <= 2), 2 * m - 2 + col_c >= 0)
    out_lane = lax.broadcasted_iota(jnp.int32, (QB, PAIR), 1)

    def dup(x, kv):
        sw = pltpu.roll(x, HEAD_DIM, 1)
        return jnp.where(lane < HEAD_DIM, x, sw) if kv == 0 else jnp.where(lane < HEAD_DIM, sw, x)

    k_st, v_st = [], []
    for b in range(nseq):
        k_all = jnp.concatenate([kp_ref[b], kc_ref[b]], axis=0)
        v_all = jnp.concatenate([vp_ref[b], vc_ref[b]], axis=0)
        k_st.append([_stack_heads(dup(k_all, kv), HEAD_DIM).astype(BF16) for kv in range(ATT_KV_HEADS)])
        v_st.append([_stack_heads(dup(v_all, kv), HEAD_DIM).astype(BF16) for kv in range(ATT_KV_HEADS)])
    pairs = [(b, j) for b in range(nseq) for j in range(ATT_HEADS // 2)]
    kv_of = lambda j: (2 * j) // ATT_GROUP
    s = [lax.dot_general(q_ref[b, :, j * PAIR:(j + 1) * PAIR].astype(BF16), k_st[b][kv_of(j)],
                         (((1,), (1,)), ((), ())), preferred_element_type=F32) * ATT_SCALE for b, j in pairs]
    probs, dens = [], []
    for idx, (b, j) in enumerate(pairs):
        halves, den = [], []
        for e in range(2):
            sink = sink_ref[2 * j + e]
            sh = jnp.where(valid, s[idx][:, e * span:(e + 1) * span], -jnp.inf)
            mx = jnp.maximum(jnp.max(sh, -1, keepdims=True), sink)
            ph = jnp.exp(sh - mx)
            halves.append(ph)
            den.append(jnp.sum(ph, -1, keepdims=True) + jnp.exp(sink - mx))
        probs.append(jnp.concatenate(halves, axis=1).astype(BF16))
        dens.append(jnp.where(out_lane < HEAD_DIM, den[0], den[1]))
    o = [jnp.dot(probs[idx], v_st[b][kv_of(j)], preferred_element_type=F32) / dens[idx]
         for idx, (b, j) in enumerate(pairs)]
    n_pairs = ATT_HEADS // 2
    for b in range(nseq):
        o_ref[b] = jnp.concatenate(o[b * n_pairs:(b + 1) * n_pairs], axis=1).astype(o_ref.dtype)


def _attn_prompt(p3, sinks, nseq):
    batch, seq = p3.shape[0], p3.shape[1]
    nb = seq // QB
    kcol, vcol = P_AK // KV_WIDTH, P_AV // KV_WIDTH
    prev = lambda col: pl.BlockSpec((nseq, QB, KV_WIDTH), lambda g, m: (g, jnp.maximum(m - 1, 0), col))
    cur = lambda col: pl.BlockSpec((nseq, QB, KV_WIDTH), lambda g, m: (g, m, col))
    return pl.pallas_call(
        _attn_prompt_kernel,
        grid=(batch // nseq, nb),
        in_specs=[
            pl.BlockSpec(memory_space=pltpu.SMEM),
            pl.BlockSpec((nseq, QB, ATT_WIDTH), lambda g, m: (g, m, P_Q // ATT_WIDTH)),
            prev(kcol), cur(kcol), prev(vcol), cur(vcol),
        ],
        out_specs=pl.BlockSpec((nseq, QB, ATT_WIDTH), lambda g, m: (g, m, 0)),
        out_shape=jax.ShapeDtypeStruct((batch, seq, ATT_WIDTH), BF16),
        compiler_params=_cparams(("arbitrary", "arbitrary")),
        name="attn_prompt",
    )(sinks, p3, p3, p3, p3, p3)


def _attn_sample_kernel(sink_ref, q_ref, kn_ref, vn_ref, kc_ref, vc_ref, o_ref, kw_ref, vw_ref):
    t = q_ref.shape[0]
    k = jnp.concatenate([kc_ref[...], kn_ref[...]], axis=0)
    v = jnp.concatenate([vc_ref[...], vn_ref[...]], axis=0)
    o_ref[...] = _attend(q_ref[...], k, v, None, sink_ref).astype(o_ref.dtype)
    kw_ref[...] = k[t:]
    vw_ref[...] = v[t:]


def _attn_sample(p, sinks, cache_k, cache_v, batch, seq):
    kcol, vcol = P_AK // KV_WIDTH, P_AV // KV_WIDTH
    return pl.pallas_call(
        _attn_sample_kernel,
        grid=(batch,),
        in_specs=[
            pl.BlockSpec(memory_space=pltpu.SMEM),
            pl.BlockSpec((seq, ATT_WIDTH), lambda b: (b, P_Q // ATT_WIDTH)),
            pl.BlockSpec((seq, KV_WIDTH), lambda b: (b, kcol)),
            pl.BlockSpec((seq, KV_WIDTH), lambda b: (b, vcol)),
            pl.BlockSpec((WINDOW, KV_WIDTH), lambda b: (b, 0)),
            pl.BlockSpec((WINDOW, KV_WIDTH), lambda b: (b, 0)),
        ],
        out_specs=[
            pl.BlockSpec((seq, ATT_WIDTH), lambda b: (b, 0)),
            pl.BlockSpec((WINDOW, KV_WIDTH), lambda b: (b, 0)),
            pl.BlockSpec((WINDOW, KV_WIDTH), lambda b: (b, 0)),
        ],
        out_shape=[
            jax.ShapeDtypeStruct((batch * seq, ATT_WIDTH), BF16),
            jax.ShapeDtypeStruct((batch * WINDOW, KV_WIDTH), F32),
            jax.ShapeDtypeStruct((batch * WINDOW, KV_WIDTH), F32),
        ],
        compiler_params=_cparams(("arbitrary",)),
        name="attn_sample",
    )(sinks, p, p, p, cache_k, cache_v)


SEG = 256
EXP_NEG_HALF = math.exp(-0.5)
PAIR = 2 * HEAD_DIM
N_PAIRS = RW_HEADS // 2


def _split2(x):
    hi = x.astype(BF16)
    lo = (x - hi.astype(F32)).astype(BF16)
    return hi, lo


def _seg_sum(x, bd):
    c = x.shape[0]
    xb = x.astype(BF16)
    stack = jnp.concatenate([xb[:, s:s + SEG] for s in range(0, RW_WIDTH, SEG)], axis=0)
    r = jnp.dot(stack, bd, preferred_element_type=F32)
    return jnp.concatenate([r[i * c:(i + 1) * c] for i in range(RW_WIDTH // SEG)], axis=1)


def _shift_rows(x, prev_row):
    row = lax.broadcasted_iota(jnp.int32, x.shape, 0)
    return jnp.where(row == 0, prev_row, pltpu.roll(x, 1, 0))


def _dotb(a, b):
    return jnp.dot(a.astype(BF16), b.astype(BF16), preferred_element_type=F32)


def _dotb_nt(a, b):
    return lax.dot_general(a.astype(BF16), b.astype(BF16), (((1,), (1,)), ((), ())), preferred_element_type=F32)


def _dot3_tn(a, b):
    ah, al = _split2(a)
    bh, bl = _split2(b)
    return lax.dot_general(jnp.concatenate([ah, al, ah], axis=0), jnp.concatenate([bh, bh, bl], axis=0),
                           (((0,), (0,)), ((), ())), preferred_element_type=F32)


def _dot2_tn(a, b):
    ah, al = _split2(a)
    bh = b.astype(BF16)
    return lax.dot_general(jnp.concatenate([ah, al], axis=0), jnp.concatenate([bh, bh], axis=0),
                           (((0,), (0,)), ((), ())), preferred_element_type=F32)


def _dot2(a, b):
    ah, al = _split2(a)
    bh = b.astype(BF16)
    return jnp.dot(jnp.concatenate([ah, al], axis=1), jnp.concatenate([bh, bh], axis=0), preferred_element_type=F32)


def _stack_heads(x, half):
    lane = lax.broadcasted_iota(jnp.int32, x.shape, 1)
    first = (lane % (2 * half)) < half
    zero = jnp.zeros_like(x)
    return jnp.concatenate([jnp.where(first, x, zero), jnp.where(first, zero, x)], axis=0)


def _stack_heads_bf16(x, half):
    return _stack_heads(x.astype(BF16), half)


def _rwkv_pairs(ab, rb, bb, kb, v, gam_c, s_bd, n_double):
    c = ab[0].shape[0]
    pairs = range(len(ab))
    t_idx = lax.broadcasted_iota(jnp.int32, (c, 2 * c), 0)
    s_idx = lax.broadcasted_iota(jnp.int32, (c, 2 * c), 1) % c
    strict, incl = s_idx < t_idx, s_idx <= t_idx
    g = [_dotb_nt(jnp.concatenate([ab[p], rb[p]], axis=0),
                  jnp.concatenate([_stack_heads_bf16(bb[p], HEAD_DIM), _stack_heads_bf16(kb[p], HEAD_DIM)], axis=0))
         for p in pairs]
    l_ak = [jnp.where(strict, g[p][:c, 2 * c:], 0.0) for p in pairs]
    m_rbk = [jnp.concatenate([jnp.where(incl, g[p][c:, :2 * c], 0.0), jnp.where(incl, g[p][c:, 2 * c:], 0.0)], axis=1)
             for p in pairs]
    lp = [jnp.where(strict, g[p][:c, :2 * c], 0.0) for p in pairs]
    v_st = [_stack_heads_bf16(v[p], HEAD_DIM) for p in pairs]
    ya = list(ab)
    yu = [_dotb(l_ak[p], v_st[p]) for p in pairs]
    for i in range(n_double):
        y_st = [_stack_heads_bf16(jnp.concatenate([ya[p], yu[p]], axis=1), HEAD_DIM) for p in pairs]
        if i + 1 < n_double:
            z = [_dotb(lp[p], jnp.concatenate([y_st[p], _stack_heads_bf16(lp[p], c)], axis=1)) for p in pairs]
            lp = [z[p][:, 2 * PAIR:] for p in pairs]
        else:
            z = [_dotb(lp[p], y_st[p]) for p in pairs]
        ya = [ya[p] + z[p][:, :PAIR] for p in pairs]
        yu = [yu[p] + z[p][:, PAIR:2 * PAIR] for p in pairs]
    z = []
    for p in pairs:
        top = _stack_heads_bf16(jnp.concatenate([ya[p], yu[p]], axis=1), HEAD_DIM)
        bot = jnp.concatenate([jnp.zeros_like(v_st[p]), v_st[p]], axis=1)
        z.append(_dotb(m_rbk[p], jnp.concatenate([top, bot], axis=0)))
    o = [_dotb_nt(rb[p] + z[p][:, :PAIR], s_bd[p]) + z[p][:, PAIR:] for p in pairs]
    row_h = lax.broadcasted_iota(jnp.int32, (PAIR, PAIR), 0) // HEAD_DIM
    col_h = lax.broadcasted_iota(jnp.int32, (PAIR, PAIR), 1) // HEAD_DIM
    same = row_h == col_h
    q_t = [jnp.where(same, _dot3_tn(ya[p], bb[p]), 0.0) for p in pairs]
    d_t = [jnp.where(same, _dot2_tn(jnp.concatenate([yu[p], v[p]], axis=0),
                                    jnp.concatenate([bb[p], kb[p]], axis=0)), 0.0) for p in pairs]
    s_new = [(s_bd[p] + _dot2(s_bd[p], q_t[p]) + d_t[p]) * gam_c[p] for p in pairs]
    return o, s_new


def _split_rw_rows(x):
    o_wd, o_gd = RW_WIDTH, RW_COLS - GATE_LORA
    low = lax.broadcasted_iota(jnp.int32, (x.shape[0], LANES), 1) < HEAD_DIM
    pieces = [x[:, c0:c0 + LANES] for c0 in range(o_wd, o_gd, LANES)]
    rolled = [pltpu.roll(p, HEAD_DIM, 1) for p in pieces]
    merged = [jnp.where(low, rolled[m], rolled[m + 1]) for m in range(len(pieces) - 1)]
    n = RW_WIDTH // LANES
    return (x[:, :RW_WIDTH], jnp.concatenate(merged[:n], axis=1), jnp.concatenate(merged[n:], axis=1),
            jnp.where(low, pieces[0], pieces[-1]), x[:, o_gd:])


def _rwkv_kernel(xr_ref, xk_ref, xv_ref, xl_ref, sh_ref, mu_ref, w0_ref, a0_ref, kk_ref, ka_ref, rk_ref,
                 lg_ref, lb_ref, w2_ref, a2_ref, g2_ref, bd_ref, st0_ref,
                 out_ref, st_ref,
                 s_scr, pr_scr, pk_scr, pv_scr, pw_scr, pg_scr):
    c = pl.program_id(1)
    nc = pl.num_programs(1)
    nb, C = xr_ref.shape[0], xr_ref.shape[1]
    n_double = int(math.log2(C))
    prev_scr = (pr_scr, pk_scr, pv_scr, pw_scr, pg_scr)

    @pl.when(c == 0)
    def _():
        s_scr[...] = jnp.zeros_like(s_scr)
        for b in range(nb):
            for h in range(RW_HEADS):
                off = (h % 2) * HEAD_DIM
                s_scr[b * N_PAIRS + h // 2, off:off + HEAD_DIM, off:off + HEAD_DIM] = st0_ref[b, h]
            for scr, row in zip(prev_scr, _split_rw_rows(sh_ref[b])):
                scr[b, 0:1, :] = row

    def shifted(load, p_scr, mu):
        parts = []
        for b in range(nb):
            x = load(b)
            prev = _shift_rows(x, p_scr[b, 0:1, :])
            p_scr[b, 0:1, :] = x[C - 1:C, :]
            parts.append(x + (prev - x) * mu)
        return jnp.concatenate(parts, axis=0)

    mu_r, mu_k, mu_v, mu_w, mu_g = _split_rw_rows(mu_ref[...])
    wa_w = DECAY_LORA + AAA_LORA
    r = shifted(lambda b: xr_ref[b], pr_scr, mu_r)
    kr = shifted(lambda b: xk_ref[b], pk_scr, mu_k)
    v = shifted(lambda b: xv_ref[b], pv_scr, mu_v)
    xw = shifted(lambda b: xl_ref[b, :, :wa_w], pw_scr, mu_w)
    gd = shifted(lambda b: xl_ref[b, :, wa_w:wa_w + GATE_LORA], pg_scr, mu_g)
    wd = xw[:, 0:DECAY_LORA]
    ad = xw[:, DECAY_LORA:wa_w]

    lw = -EXP_NEG_HALF * jax.nn.sigmoid(w0_ref[...] + _dotb(jnp.tanh(wd), w2_ref[...]))
    a = jax.nn.sigmoid(a0_ref[...] + _dotb(ad, a2_ref[...]))
    g = _dotb(jax.nn.sigmoid(gd), g2_ref[...])
    bd = bd_ref[...]
    kk = kr * kk_ref[...]
    kk = kk * lax.rsqrt(jnp.maximum(_seg_sum(kk * kk, bd), 1e-24))
    k2 = kr * (1.0 + (a - 1.0) * ka_ref[...])

    rows = nb * C
    ti = lax.broadcasted_iota(jnp.int32, (rows, rows), 0)
    si = lax.broadcasted_iota(jnp.int32, (rows, rows), 1)
    tri = jnp.logical_and(si <= ti, si // C == ti // C).astype(BF16)
    cs2 = jnp.dot(tri, jnp.concatenate(_split2(lw), axis=1), preferred_element_type=F32)
    cs = cs2[:, :RW_WIDTH] + cs2[:, RW_WIDTH:]
    gam = jnp.exp(cs)
    ginv = jnp.exp(-cs)
    ab = -kk * jnp.exp(cs - lw)
    rb = r * gam
    bb = kk * a * ginv
    kb = k2 * ginv

    def per_pair(x, r0, r1):
        return [x[b * C + r0:b * C + r1, p * PAIR:(p + 1) * PAIR] for b in range(nb) for p in range(N_PAIRS)]

    chains = nb * N_PAIRS
    outs, s_new = _rwkv_pairs(per_pair(ab, 0, C), per_pair(rb, 0, C), per_pair(bb, 0, C), per_pair(kb, 0, C),
                              per_pair(v, 0, C), per_pair(gam, C - 1, C), [s_scr[q] for q in range(chains)], n_double)
    for q in range(chains):
        s_scr[q] = s_new[q]
    o = jnp.concatenate([jnp.concatenate(outs[b * N_PAIRS:(b + 1) * N_PAIRS], axis=1) for b in range(nb)], axis=0)

    mo = _seg_sum(o, bd) * (1.0 / HEAD_DIM)
    d = o - mo
    vo = _seg_sum(d * d, bd) * (1.0 / HEAD_DIM)
    on = d * lax.rsqrt(vo + LNX_EPS) * lg_ref[...] + lb_ref[...]
    bonus = _seg_sum(r * k2 * rk_ref[...], bd) * v
    res = ((on + bonus) * g).astype(out_ref.dtype)
    for b in range(nb):
        out_ref[b] = res[b * C:(b + 1) * C, :]

    @pl.when(c == nc - 1)
    def _():
        for b in range(nb):
            for h in range(RW_HEADS):
                off = (h % 2) * HEAD_DIM
                st_ref[b, h] = s_scr[b * N_PAIRS + h // 2, off:off + HEAD_DIM, off:off + HEAD_DIM]


def _rwkv(p3, state_shift, state0, prm, chunk, nb):
    batch, seq = p3.shape[0], p3.shape[1]
    nc = seq // chunk
    row = lambda w: pl.BlockSpec((1, w), lambda g, c: (0, 0))
    xspec = lambda w, col: pl.BlockSpec((nb, chunk, w), lambda g, c: (g, c, col))
    full = lambda a: pl.BlockSpec(a.shape, lambda g, c: (0,) * a.ndim)
    stspec = pl.BlockSpec((nb, RW_HEADS, HEAD_DIM, HEAD_DIM), lambda g, c: (g, 0, 0, 0))
    return pl.pallas_call(
        _rwkv_kernel,
        grid=(batch // nb, nc),
        in_specs=[
            xspec(RW_WIDTH, P_R // RW_WIDTH), xspec(RW_WIDTH, P_K // RW_WIDTH), xspec(RW_WIDTH, P_V // RW_WIDTH),
            xspec(LORA_W, P_LORA // LORA_W),
            pl.BlockSpec((nb, 1, RW_COLS), lambda g, c: (g, 0, 0)),
            row(RW_COLS),
            row(RW_WIDTH), row(RW_WIDTH), row(RW_WIDTH), row(RW_WIDTH), row(RW_WIDTH),
            row(RW_WIDTH), row(RW_WIDTH),
            full(prm["w2"]), full(prm["a2"]), full(prm["g2"]), full(prm["bd"]),
            stspec,
        ],
        out_specs=[
            pl.BlockSpec((nb, chunk, RW_WIDTH), lambda g, c: (g, c, 0)),
            stspec,
        ],
        out_shape=[
            jax.ShapeDtypeStruct((batch, seq, RW_WIDTH), BF16),
            jax.ShapeDtypeStruct((batch, RW_HEADS, HEAD_DIM, HEAD_DIM), F32),
        ],
        scratch_shapes=[
            pltpu.VMEM((nb * N_PAIRS, PAIR, PAIR), F32),
            pltpu.VMEM((nb, SUBLANES, RW_WIDTH), F32), pltpu.VMEM((nb, SUBLANES, RW_WIDTH), F32),
            pltpu.VMEM((nb, SUBLANES, RW_WIDTH), F32),
            pltpu.VMEM((nb, SUBLANES, DECAY_LORA + AAA_LORA), F32), pltpu.VMEM((nb, SUBLANES, GATE_LORA), F32),
        ],
        compiler_params=_cparams(("arbitrary", "arbitrary")),
        name="rwkv",
    )(p3, p3, p3, p3, state_shift, prm["mu"], prm["w0"], prm["a0"], prm["k_k"], prm["k_a"], prm["r_k"],
      prm["lnx_g"], prm["lnx_b"], prm["w2"], prm["a2"], prm["g2"], prm["bd"], state0)


D_TN = 512


def _out_proj_body(att_ref, rw_ref, h_ref, g_ref, b_ref, h1_ref, h1b_ref, w_scr):
    for c0 in range(0, D_MODEL, D_TN):
        m = jnp.dot(att_ref[...], w_scr[:ATT_WIDTH, c0:c0 + D_TN], preferred_element_type=F32)
        m = m + jnp.dot(rw_ref[...], w_scr[ATT_WIDTH:, c0:c0 + D_TN], preferred_element_type=F32)
        h1_ref[:, c0:c0 + D_TN] = ALPHA * h_ref[:, c0:c0 + D_TN] + m
    y = _layer_norm(h1_ref[...], g_ref[...], b_ref[...])
    h1_ref[...] = y
    h1b_ref[...] = y.astype(BF16)


def _out_proj(att_p, rw_p, h_p, att_s, rw_s, h_s, w_out, g, b, tm):
    n_p, n_s = att_p.shape[0], att_s.shape[0]
    assert n_s == tm
    nrb = n_p // tm
    last = lambda i: jnp.minimum(i, nrb - 1)

    def kern(ap_ref, rp_ref, hp_ref, as_ref, rs_ref, hs_ref, wa_ref, wb_ref, g_ref, b_ref,
             h1p_ref, h1bp_ref, h1s_ref, h1bs_ref, w_scr):
        i = pl.program_id(0)

        @pl.when(i == 0)
        def _():
            w_scr[:ATT_WIDTH, :] = wa_ref[...].astype(BF16)
            w_scr[ATT_WIDTH:, :] = wb_ref[...].astype(BF16)

        pl.when(i < nrb)(lambda: _out_proj_body(ap_ref, rp_ref, hp_ref, g_ref, b_ref, h1p_ref, h1bp_ref, w_scr))
        pl.when(i == nrb)(lambda: _out_proj_body(as_ref, rs_ref, hs_ref, g_ref, b_ref, h1s_ref, h1bs_ref, w_scr))

    rows_p = lambda w: pl.BlockSpec((tm, w), lambda i: (last(i), 0))
    rows_s = lambda w: pl.BlockSpec((tm, w), lambda i: (0, 0))
    return pl.pallas_call(
        kern,
        grid=(nrb + 1,),
        in_specs=[
            rows_p(ATT_WIDTH), rows_p(RW_WIDTH), rows_p(D_MODEL),
            rows_s(ATT_WIDTH), rows_s(RW_WIDTH), rows_s(D_MODEL),
            pl.BlockSpec((ATT_WIDTH, D_MODEL), lambda i: (0, 0), pipeline_mode=pl.Buffered(1)),
            pl.BlockSpec((RW_WIDTH, D_MODEL), lambda i: (1, 0), pipeline_mode=pl.Buffered(1)),
            pl.BlockSpec((1, D_MODEL), lambda i: (0, 0)),
            pl.BlockSpec((1, D_MODEL), lambda i: (0, 0)),
        ],
        out_specs=[rows_p(D_MODEL), rows_p(D_MODEL), rows_s(D_MODEL), rows_s(D_MODEL)],
        out_shape=[
            jax.ShapeDtypeStruct((n_p, D_MODEL), F32), jax.ShapeDtypeStruct((n_p, D_MODEL), BF16),
            jax.ShapeDtypeStruct((n_s, D_MODEL), F32), jax.ShapeDtypeStruct((n_s, D_MODEL), BF16),
        ],
        scratch_shapes=[pltpu.VMEM((ATT_WIDTH + RW_WIDTH, D_MODEL), BF16)],
        compiler_params=_cparams(("arbitrary",)),
        name="out_proj",
    )(att_p, rw_p, h_p, att_s, rw_s, h_s, w_out, w_out, g, b)


E_TN = 512
E_NCB = D_FF // E_TN
E_SUB = 512
SQRT_HALF = math.sqrt(0.5)


def _ffn_up_body(bps, nseq, i, h_ref, w_scr, cwg_ref, cwv_ref, cbg_ref, cbv_ref, sg_ref, sv_ref,
                 act_ref, og_ref, ov_ref, cg_scr, cv_scr):
    tm = h_ref.shape[0]
    sub = min(tm, E_SUB)
    n_sub = tm // sub
    t_seq = tm // nseq
    row8 = lax.broadcasted_iota(jnp.int32, (SUBLANES, E_TN), 0)

    def products(r0):
        u = jnp.dot(h_ref[r0:r0 + sub, :], w_scr[...], preferred_element_type=F32)
        return u[:, :E_TN], u[:, E_TN:]

    def conv(u, prevs, cw_ref, cb_ref):
        sh1, sh2 = pltpu.roll(u, 1, 0), pltpu.roll(u, 2, 0)
        for r, prev in prevs:
            p2, p1 = prev[0:1, :], prev[1:2, :]
            fix1 = jnp.where(row8 == 0, p1, sh1[r:r + SUBLANES, :])
            fix2 = jnp.where(row8 == 0, p2, jnp.where(row8 == 1, p1, sh2[r:r + SUBLANES, :]))
            head = lambda x: [x[:r, :]] if r > 0 else []
            tail = lambda x: [x[r + SUBLANES:, :]] if r + SUBLANES < x.shape[0] else []
            sh1 = jnp.concatenate(head(sh1) + [fix1] + tail(sh1), axis=0)
            sh2 = jnp.concatenate(head(sh2) + [fix2] + tail(sh2), axis=0)
        cw = cw_ref[...]
        c = cb_ref[...] + sh2 * cw[0:1, :]
        c = c + sh1 * cw[1:2, :]
        return c + u * cw[2:3, :]

    def epilogue(r0, ug, uv, prev_g, prev_v):
        gate = conv(ug, prev_g, cwg_ref, cbg_ref)
        val = conv(uv, prev_v, cwv_ref, cbv_ref)
        gelu = 0.5 * gate * (1.0 + lax.erf(gate * SQRT_HALF))
        act_ref[r0:r0 + sub, :] = (gelu * val).astype(act_ref.dtype)

    if nseq > 1:
        ug, uv = products(0)
        epilogue(0, ug, uv, [(s * t_seq, sg_ref[s]) for s in range(nseq)],
                 [(s * t_seq, sv_ref[s]) for s in range(nseq)])
        for s in range(nseq):
            og_ref[s] = ug[(s + 1) * t_seq - SUBLANES:(s + 1) * t_seq, :]
            ov_ref[s] = uv[(s + 1) * t_seq - SUBLANES:(s + 1) * t_seq, :]
    else:
        first = (i % bps) == 0
        prev_g = jnp.where(first, sg_ref[0], cg_scr[SUBLANES - 2:SUBLANES, :])
        prev_v = jnp.where(first, sv_ref[0], cv_scr[SUBLANES - 2:SUBLANES, :])
        nxt = products(0)
        for k in range(n_sub):
            ug, uv = nxt
            if k + 1 < n_sub:
                nxt = products((k + 1) * sub)
            epilogue(k * sub, ug, uv, [(0, prev_g)], [(0, prev_v)])
            prev_g, prev_v = ug[sub - 2:sub, :], uv[sub - 2:sub, :]
        cg_scr[...] = ug[sub - SUBLANES:sub, :]
        cv_scr[...] = uv[sub - SUBLANES:sub, :]
        og_ref[0] = ug[sub - SUBLANES:sub, :]
        ov_ref[0] = uv[sub - SUBLANES:sub, :]


def _ffn_up(h_p, conv_prev_p, seq_p, tm_p, h_s, conv_prev_s, seq_s, w_up, conv_w, conv_b, w_down):
    n_p, n_s = h_p.shape[0], h_s.shape[0]
    batch_p, batch_s = n_p // seq_p, n_s // seq_s
    bps = seq_p // tm_p
    nrb = n_p // tm_p
    slab = D_FF // (E_NCB * nrb)
    last = lambda i: jnp.minimum(i, nrb - 1)

    def kern(hp_ref, sgp_ref, svp_ref, hs_ref, sgs_ref, svs_ref, wg_ref, wv_ref, cwg_ref, cwv_ref, cbg_ref, cbv_ref,
             wd_ref, actp_ref, ogp_ref, ovp_ref, acts_ref, ogs_ref, ovs_ref, wdb_ref, w_scr, cg_scr, cv_scr):
        i = pl.program_id(1)

        @pl.when(i == 0)
        def _():
            w_scr[:, :E_TN] = wg_ref[...].astype(BF16)
            w_scr[:, E_TN:] = wv_ref[...].astype(BF16)

        @pl.when(i < nrb)
        def _():
            wdb_ref[...] = wd_ref[...].astype(BF16)
            _ffn_up_body(bps, 1, i, hp_ref, w_scr, cwg_ref, cwv_ref, cbg_ref, cbv_ref, sgp_ref, svp_ref,
                         actp_ref, ogp_ref, ovp_ref, cg_scr, cv_scr)

        @pl.when(i == nrb)
        def _():
            _ffn_up_body(1, batch_s, i, hs_ref, w_scr, cwg_ref, cwv_ref, cbg_ref, cbv_ref, sgs_ref, svs_ref,
                         acts_ref, ogs_ref, ovs_ref, cg_scr, cv_scr)

    col = lambda shape, off: pl.BlockSpec(shape, lambda j, i: (0,) * (len(shape) - 1) + (j + off,))
    st_p = lambda off: pl.BlockSpec((1, CONV_W - 1, E_TN), lambda j, i: (last(i) // bps, 0, j + off))
    st_s = lambda off: pl.BlockSpec((batch_s, CONV_W - 1, E_TN), lambda j, i: (0, 0, j + off))
    tail_p = pl.BlockSpec((1, SUBLANES, E_TN), lambda j, i: (last(i) // bps, 0, j))
    tail_s = pl.BlockSpec((batch_s, SUBLANES, E_TN), lambda j, i: (0, 0, j))
    wd_spec = pl.BlockSpec((slab, D_MODEL), lambda j, i: (j * nrb + last(i), 0))
    return pl.pallas_call(
        kern,
        grid=(E_NCB, nrb + 1),
        in_specs=[
            pl.BlockSpec((tm_p, D_MODEL), lambda j, i: (last(i), 0)), st_p(0), st_p(E_NCB),
            pl.BlockSpec((n_s, D_MODEL), lambda j, i: (0, 0)), st_s(0), st_s(E_NCB),
            col((D_MODEL, E_TN), 0), col((D_MODEL, E_TN), E_NCB),
            col((CONV_W, E_TN), 0), col((CONV_W, E_TN), E_NCB),
            col((1, E_TN), 0), col((1, E_TN), E_NCB),
            wd_spec,
        ],
        out_specs=[
            pl.BlockSpec((tm_p, E_TN), lambda j, i: (last(i), j)), tail_p, tail_p,
            pl.BlockSpec((n_s, E_TN), lambda j, i: (0, j)), tail_s, tail_s,
            wd_spec,
        ],
        out_shape=[
            jax.ShapeDtypeStruct((n_p, D_FF), BF16),
            jax.ShapeDtypeStruct((batch_p, SUBLANES, D_FF), F32), jax.ShapeDtypeStruct((batch_p, SUBLANES, D_FF), F32),
            jax.ShapeDtypeStruct((n_s, D_FF), BF16),
            jax.ShapeDtypeStruct((batch_s, SUBLANES, D_FF), F32), jax.ShapeDtypeStruct((batch_s, SUBLANES, D_FF), F32),
            jax.ShapeDtypeStruct((D_FF, D_MODEL), BF16),
        ],
        scratch_shapes=[pltpu.VMEM((D_MODEL, 2 * E_TN), BF16),
                        pltpu.VMEM((SUBLANES, E_TN), F32), pltpu.VMEM((SUBLANES, E_TN), F32)],
        compiler_params=_cparams(("arbitrary", "arbitrary")),
        name="ffn_up",
    )(h_p, conv_prev_p, conv_prev_p, h_s, conv_prev_s, conv_prev_s, w_up, w_up, conv_w, conv_w, conv_b, conv_b, w_down)


F_TN = 512


def _ffn_down_kernel(act_ref, w_ref, h1_ref, g_ref, b_ref, y_ref):
    for c0 in range(0, D_MODEL, F_TN):
        f = jnp.dot(act_ref[...], w_ref[:, c0:c0 + F_TN], preferred_element_type=F32)
        y_ref[:, c0:c0 + F_TN] = ALPHA * h1_ref[:, c0:c0 + F_TN] + f
    y_ref[...] = _layer_norm(y_ref[...], g_ref[...], b_ref[...])


def _ffn_down(act_p, h1_p, act_s, h1_s, w_down, g, b, tm):
    n_p, n_s = act_p.shape[0], act_s.shape[0]
    assert n_s == tm
    nrb = n_p // tm
    last = lambda i: jnp.minimum(i, nrb - 1)

    def kern(ap_ref, hp_ref, as_ref, hs_ref, w_ref, g_ref, b_ref, yp_ref, ys_ref):
        i = pl.program_id(0)
        pl.when(i < nrb)(lambda: _ffn_down_kernel(ap_ref, w_ref, hp_ref, g_ref, b_ref, yp_ref))
        pl.when(i == nrb)(lambda: _ffn_down_kernel(as_ref, w_ref, hs_ref, g_ref, b_ref, ys_ref))

    rows_p = lambda w: pl.BlockSpec((tm, w), lambda i: (last(i), 0))
    rows_s = lambda w: pl.BlockSpec((tm, w), lambda i: (0, 0))
    return pl.pallas_call(
        kern,
        grid=(nrb + 1,),
        in_specs=[
            rows_p(D_FF), rows_p(D_MODEL), rows_s(D_FF), rows_s(D_MODEL),
            pl.BlockSpec((D_FF, D_MODEL), lambda i: (0, 0), pipeline_mode=pl.Buffered(1)),
            pl.BlockSpec((1, D_MODEL), lambda i: (0, 0)),
            pl.BlockSpec((1, D_MODEL), lambda i: (0, 0)),
        ],
        out_specs=[rows_p(D_MODEL), rows_s(D_MODEL)],
        out_shape=[jax.ShapeDtypeStruct((n_p, D_MODEL), F32), jax.ShapeDtypeStruct((n_s, D_MODEL), F32)],
        compiler_params=_cparams(("arbitrary",)),
        name="ffn_down",
    )(act_p, h1_p, act_s, h1_s, w_down, g, b)


def _rope_tables(pos, reps):
    half = HEAD_DIM // 2
    inv = ROPE_THETA ** (-jnp.arange(half, dtype=F32) / half)
    ang = pos.astype(F32)[:, None] * inv[None, :]
    cos, sin = jnp.cos(ang), jnp.sin(ang)
    cos128 = jnp.concatenate([cos, cos, cos, cos], axis=1)
    sin128 = jnp.concatenate([-sin, sin, -sin, sin], axis=1)
    return jnp.tile(cos128, (reps, 1)), jnp.tile(sin128, (reps, 1))


TM = 256
E_TM = 1024
RWKV_NB = 4
ATTN_NB = 2


def kernel(x_prompt, x_sample, cache_k, cache_v, state_wkv, state_shift, state_ffn_conv, ln_in_g, ln_in_b, w_in, attn_sinks, rw_mu, rw_w0, rw_w2, rw_a0, rw_a2, rw_g2, rw_k_k, rw_k_a, rw_r_k, rw_lnx_g, rw_lnx_b, w_out, ln1_g, ln1_b, ffn_w_up, ffn_conv_w, ffn_conv_b, ffn_w_down, ln2_g, ln2_b):
    l = 0
    seg = lax.broadcasted_iota(jnp.int32, (SEG, SEG), 0) // HEAD_DIM
    bd = (seg == seg.T).astype(BF16)
    row = lambda a: a.reshape(1, -1)
    rw_prm = {"mu": row(rw_mu[l]), "w0": row(rw_w0[l]), "a0": row(rw_a0[l]), "k_k": row(rw_k_k[l]),
              "k_a": row(rw_k_a[l]), "r_k": row(rw_r_k[l]), "lnx_g": row(rw_lnx_g[l]), "lnx_b": row(rw_lnx_b[l]),
              "w2": rw_w2[l].astype(BF16), "a2": rw_a2[l].astype(BF16), "g2": rw_g2[l].astype(BF16), "bd": bd}
    sinks = attn_sinks[l]
    bp, tp = x_prompt.shape[0], x_prompt.shape[1]
    bs, ts = x_sample.shape[0], x_sample.shape[1]
    n_p, n_s = bp * tp, bs * ts
    zeros = lambda *s: jnp.zeros(s, F32)

    p_p, h_p, tails_p, p_s, h_s, tails_s = _in_proj(
        x_prompt.reshape(n_p, D_MODEL), x_sample.reshape(n_s, D_MODEL), row(ln_in_g), row(ln_in_b),
        w_in[l].astype(BF16), _rope_tables(jnp.arange(tp, dtype=jnp.int32), 1),
        _rope_tables(PAST_LEN + jnp.arange(ts, dtype=jnp.int32), bs), TM, tp, ts)
    p3_p, p3_s = p_p.reshape(bp, tp, P_COLS), p_s.reshape(bs, ts, P_COLS)

    att_p = _attn_prompt(p3_p, sinks, ATTN_NB).reshape(n_p, ATT_WIDTH)
    att_s, s_k, s_v = _attn_sample(p_s, sinks, cache_k[l].reshape(bs * WINDOW, KV_WIDTH),
                                   cache_v[l].reshape(bs * WINDOW, KV_WIDTH), bs, ts)
    rw_p, p_wkv = _rwkv(p3_p, zeros(bp, 1, RW_COLS), zeros(bp, RW_HEADS, HEAD_DIM, HEAD_DIM), rw_prm, CHUNK, RWKV_NB)
    rw_s, s_wkv = _rwkv(p3_s, state_shift[l], state_wkv[l], rw_prm, ts, RWKV_NB)

    h1_p, h1b_p, h1_s, h1b_s = _out_proj(att_p, rw_p.reshape(n_p, RW_WIDTH), h_p, att_s, rw_s.reshape(n_s, RW_WIDTH), h_s,
                                         w_out[l], row(ln1_g[l]), row(ln1_b[l]), TM)
    act_p, cg_p, cv_p, act_s, cg_s, cv_s, w_down_b = _ffn_up(
        h1b_p, zeros(bp, CONV_W - 1, 2 * D_FF), tp, E_TM, h1b_s, state_ffn_conv[l], ts,
        ffn_w_up[l], ffn_conv_w[l], row(ffn_conv_b[l]), ffn_w_down[l])
    y_p, y_s = _ffn_down(act_p, h1_p, act_s, h1_s, w_down_b, row(ln2_g[l]), row(ln2_b[l]), TM)

    kv_shape = lambda b: (1, b, WINDOW, ATT_KV_HEADS, HEAD_DIM)
    p_k = p3_p[:, tp - WINDOW:, P_AK:P_AK + KV_WIDTH].reshape(kv_shape(bp))
    p_v = p3_p[:, tp - WINDOW:, P_AV:P_AV + KV_WIDTH].reshape(kv_shape(bp))
    shift_of = lambda tails, b: tails.reshape(b, -1, RW_COLS)[:, -1:, :][None]
    conv_of = lambda cg, cv: jnp.concatenate([cg[:, SUBLANES - 2:, :], cv[:, SUBLANES - 2:, :]], axis=-1)[None]
    return (y_p.reshape(bp, tp, D_MODEL), y_s.reshape(bs, ts, D_MODEL), p_k, p_v, p_wkv[None],
            shift_of(tails_p, bp), conv_of(cg_p, cv_p),
            s_k.reshape(kv_shape(bs)), s_v.reshape(kv_shape(bs)), s_wkv[None],
            shift_of(tails_s, bs), conv_of(cg_s, cv_s))
```

```python
import functools
import math

import jax
import jax.numpy as jnp
from jax import lax
from jax.experimental import pallas as pl
from jax.experimental.pallas import tpu as pltpu

D_MODEL = 2048
HEAD_DIM = 64
ATT_HEADS = 16
ATT_KV_HEADS = 2
ATT_GROUP = ATT_HEADS // ATT_KV_HEADS
ATT_WIDTH = ATT_HEADS * HEAD_DIM
KV_WIDTH = ATT_KV_HEADS * HEAD_DIM
CHUNK = 64
WINDOW = 128
ROPE_THETA = 10000.0
ATT_SCALE = HEAD_DIM ** -0.5
RW_HEADS = 16
RW_WIDTH = RW_HEADS * HEAD_DIM
DECAY_LORA = 64
AAA_LORA = 64
GATE_LORA = 160
LNX_EPS = 64e-5
ATT_COLS = ATT_WIDTH + 2 * KV_WIDTH
RW_COLS = 3 * RW_WIDTH + DECAY_LORA + AAA_LORA + GATE_LORA
D_FF = 5632
CONV_W = 3
LN_EPS = 1e-5
DEPTH = 1
ALPHA = (2 * DEPTH) ** 0.25
PAST_LEN = 1024

LORA_W = 512
P_R, P_K, P_V, P_Q = 0, 1024, 2048, 3072
P_LORA = 4096
P_AK = P_LORA + LORA_W
P_AV = P_AK + KV_WIDTH
P_COLS = P_AV + KV_WIDTH
LANES = 128
SUBLANES = 8
VMEM_LIMIT = 56 * 1024 * 1024

F32 = jnp.float32
BF16 = jnp.bfloat16


def _cparams(sem):
    return pltpu.CompilerParams(dimension_semantics=sem, vmem_limit_bytes=VMEM_LIMIT)


def _layer_norm(y, g, b):
    mu = jnp.mean(y, -1, keepdims=True)
    d = y - mu
    var = jnp.mean(d * d, -1, keepdims=True)
    return d * lax.rsqrt(var + LN_EPS) * g + b


A_TN = 512


def _rope128(x, cos, sin_signed):
    lane = lax.broadcasted_iota(jnp.int32, x.shape, 1)
    first_half = (lane % HEAD_DIM) < (HEAD_DIM // 2)
    rot = jnp.where(first_half, pltpu.roll(x, LANES - HEAD_DIM // 2, 1), pltpu.roll(x, HEAD_DIM // 2, 1))
    return x * cos + rot * sin_signed


def _is_rope_col(col):
    return (P_Q <= col < P_Q + ATT_WIDTH) or (P_AK <= col < P_AK + KV_WIDTH)


O_R = ATT_COLS
O_WD = O_R + RW_WIDTH
O_K = O_WD + DECAY_LORA
O_V = O_K + RW_WIDTH
O_AD = O_V + RW_WIDTH
O_GD = O_AD + AAA_LORA
IN_COLS = O_GD + GATE_LORA


def _in_proj_kernel(x_ref, g_ref, b_ref, w_ref, cos_ref, sin_ref, p_ref, h_ref, tail_ref, hb_ref):
    h = _layer_norm(x_ref[...], g_ref[...], b_ref[...])
    h_ref[...] = h
    hb_ref[...] = h.astype(BF16)
    nseq = tail_ref.shape[0]
    t_seq = x_ref.shape[0] // nseq

    def product(c0, width):
        acc = jnp.dot(hb_ref[...], w_ref[:, c0:c0 + width], preferred_element_type=F32)
        if c0 >= O_R:
            for s in range(nseq):
                tail_ref[s, :, c0 - O_R:c0 - O_R + width] = acc[(s + 1) * t_seq - SUBLANES:(s + 1) * t_seq, :]
        return acc

    for src, width, dst in ((0, ATT_WIDTH, P_Q), (ATT_WIDTH, 2 * KV_WIDTH, P_AK), (O_R, RW_WIDTH, P_R)):
        for c0 in range(0, width, A_TN):
            acc = product(src + c0, min(A_TN, width - c0))
            for l0 in range(0, acc.shape[1], LANES):
                col = dst + c0 + l0
                part = acc[:, l0:l0 + LANES]
                if _is_rope_col(col):
                    part = _rope128(part, cos_ref[...], sin_ref[...])
                p_ref[:, col:col + LANES] = part

    low = lax.broadcasted_iota(jnp.int32, (x_ref.shape[0], LANES), 1) < HEAD_DIM
    span = O_GD - O_WD
    rolled, first_piece = [], None
    for c0 in range(0, span, A_TN):
        acc = product(O_WD + c0, min(A_TN, span - c0))
        for l0 in range(0, acc.shape[1], LANES):
            piece = acc[:, l0:l0 + LANES]
            j = (c0 + l0) // LANES
            if j == 0:
                first_piece = piece
            rolled.append(pltpu.roll(piece, HEAD_DIM, 1))
            if j >= 1:
                m = j - 1
                dst = P_K + m * LANES if m < RW_WIDTH // LANES else P_V + (m - RW_WIDTH // LANES) * LANES
                p_ref[:, dst:dst + LANES] = jnp.where(low, rolled[m], rolled[m + 1])
            if j == span // LANES - 1:
                p_ref[:, P_LORA:P_LORA + LANES] = jnp.where(low, first_piece, piece)
    gd0 = P_LORA + DECAY_LORA + AAA_LORA
    p_ref[:, gd0:gd0 + GATE_LORA] = product(O_GD, GATE_LORA)
    p_ref[:, gd0 + GATE_LORA:P_LORA + LORA_W] = jnp.zeros((x_ref.shape[0], LORA_W - DECAY_LORA - AAA_LORA - GATE_LORA), F32)


def _in_proj(xp, xs, g, b, w_p, tables_p, tables_s, tm, seq_p, seq_s):
    n_p, n_s = xp.shape[0], xs.shape[0]
    assert n_s == tm
    nrb = n_p // tm
    nseq_s = tm // seq_s
    tb = seq_p // tm
    last = lambda i: jnp.minimum(i, nrb - 1)

    def body(ins, shared, outs, scratch):
        x_ref, cos_ref, sin_ref = ins
        g_ref, b_ref, w_ref = shared
        _in_proj_kernel(x_ref, g_ref, b_ref, w_ref, cos_ref, sin_ref, *outs, *scratch)

    def kern(xp_ref, cp_ref, sp_ref, xs_ref, cs_ref, ss_ref, g_ref, b_ref, w_ref,
             pp_ref, hp_ref, tp_ref, ps_ref, hs_ref, ts_ref, hb_ref):
        i = pl.program_id(0)
        pl.when(i < nrb)(lambda: body((xp_ref, cp_ref, sp_ref), (g_ref, b_ref, w_ref), (pp_ref, hp_ref, tp_ref), (hb_ref,)))
        pl.when(i == nrb)(lambda: body((xs_ref, cs_ref, ss_ref), (g_ref, b_ref, w_ref), (ps_ref, hs_ref, ts_ref), (hb_ref,)))

    const2 = lambda shape: pl.BlockSpec(shape, lambda i: (0, 0))
    return pl.pallas_call(
        kern,
        grid=(nrb + 1,),
        in_specs=[
            pl.BlockSpec((tm, D_MODEL), lambda i: (last(i), 0)),
            pl.BlockSpec((tm, LANES), lambda i: (last(i) % tb, 0)),
            pl.BlockSpec((tm, LANES), lambda i: (last(i) % tb, 0)),
            const2((tm, D_MODEL)), const2((tm, LANES)), const2((tm, LANES)),
            const2((1, D_MODEL)), const2((1, D_MODEL)),
            pl.BlockSpec((D_MODEL, IN_COLS), lambda i: (0, 0), pipeline_mode=pl.Buffered(1)),
        ],
        out_specs=[
            pl.BlockSpec((tm, P_COLS), lambda i: (last(i), 0)),
            pl.BlockSpec((tm, D_MODEL), lambda i: (last(i), 0)),
            pl.BlockSpec((1, SUBLANES, RW_COLS), lambda i: (last(i), 0, 0)),
            const2((tm, P_COLS)), const2((tm, D_MODEL)),
            pl.BlockSpec((nseq_s, SUBLANES, RW_COLS), lambda i: (0, 0, 0)),
        ],
        out_shape=[
            jax.ShapeDtypeStruct((n_p, P_COLS), F32),
            jax.ShapeDtypeStruct((n_p, D_MODEL), F32),
            jax.ShapeDtypeStruct((nrb, SUBLANES, RW_COLS), F32),
            jax.ShapeDtypeStruct((n_s, P_COLS), F32),
            jax.ShapeDtypeStruct((n_s, D_MODEL), F32),
            jax.ShapeDtypeStruct((nseq_s, SUBLANES, RW_COLS), F32),
        ],
        scratch_shapes=[pltpu.VMEM((tm, D_MODEL), BF16)],
        compiler_params=_cparams(("arbitrary",)),
        name="in_proj",
    )(xp, *tables_p, xs, *tables_s, g, b, w_p)


def _attend(q, k, v, valid, sink_ref):
    r = q.shape[0]
    outs = []
    for kv in range(ATT_KV_HEADS):
        kh = k[:, kv * HEAD_DIM:(kv + 1) * HEAD_DIM].astype(BF16)
        vh = v[:, kv * HEAD_DIM:(kv + 1) * HEAD_DIM].astype(BF16)
        heads = [kv * ATT_GROUP + g for g in range(ATT_GROUP)]
        qs = jnp.concatenate([q[:, h * HEAD_DIM:(h + 1) * HEAD_DIM] for h in heads], axis=0)
        sink = jnp.concatenate([jnp.full((r, 1), sink_ref[h], F32) for h in heads], axis=0)
        s = lax.dot_general(qs.astype(BF16), kh, (((1,), (1,)), ((), ())), preferred_element_type=F32) * ATT_SCALE
        if valid is not None:
            s = jnp.where(valid, s, -jnp.inf)
        m = jnp.maximum(jnp.max(s, -1, keepdims=True), sink)
        p = jnp.exp(s - m)
        den = jnp.sum(p, -1, keepdims=True) + jnp.exp(sink - m)
        o = jnp.dot(p.astype(BF16), vh, preferred_element_type=F32) / den
        outs.extend(o[g * r:(g + 1) * r] for g in range(ATT_GROUP))
    return jnp.concatenate(outs, axis=1)


QB = 2 * CHUNK


def _attn_prompt_kernel(sink_ref, q_ref, kp_ref, kc_ref, vp_ref, vc_ref, o_ref):
    m = pl.program_id(1)
    nseq = q_ref.shape[0]
    span = 2 * QB
    lane = lax.broadcasted_iota(jnp.int32, (span, KV_WIDTH), 1)
    row_c = 2 + lax.broadcasted_iota(jnp.int32, (QB, span), 0) // CHUNK
    col_c = lax.broadcasted_iota(jnp.int32, (QB, span), 1) // CHUNK
    back = row_c - col_c
    valid = jnp.logical_and(jnp.logical_and(back >= 0, back <= 2), 2 * m - 2 + col_c >= 0)
    out_lane = lax.broadcasted_iota(jnp.int32, (QB, PAIR), 1)

    def dup(x, kv):
        sw = pltpu.roll(x, HEAD_DIM, 1)
        return jnp.where(lane < HEAD_DIM, x, sw) if kv == 0 else jnp.where(lane < HEAD_DIM, sw, x)

    k_st, v_st = [], []
    for b in range(nseq):
        k_all = jnp.concatenate([kp_ref[b], kc_ref[b]], axis=0)
        v_all = jnp.concatenate([vp_ref[b], vc_ref[b]], axis=0)
        k_st.append([_stack_heads(dup(k_all, kv), HEAD_DIM).astype(BF16) for kv in range(ATT_KV_HEADS)])
        v_st.append([_stack_heads(dup(v_all, kv), HEAD_DIM).astype(BF16) for kv in range(ATT_KV_HEADS)])
    pairs = [(b, j) for b in range(nseq) for j in range(ATT_HEADS // 2)]
    kv_of = lambda j: (2 * j) // ATT_GROUP
    s = [lax.dot_general(q_ref[b, :, j * PAIR:(j + 1) * PAIR].astype(BF16), k_st[b][kv_of(j)],
                         (((1,), (1,)), ((), ())), preferred_element_type=F32) * ATT_SCALE for b, j in pairs]
    probs, dens = [], []
    for idx, (b, j) in enumerate(pairs):
        halves, den = [], []
        for e in range(2):
            sink = sink_ref[2 * j + e]
            sh = jnp.where(valid, s[idx][:, e * span:(e + 1) * span], -jnp.inf)
            mx = jnp.maximum(jnp.max(sh, -1, keepdims=True), sink)
            ph = jnp.exp(sh - mx)
            halves.append(ph)
            den.append(jnp.sum(ph, -1, keepdims=True) + jnp.exp(sink - mx))
        probs.append(jnp.concatenate(halves, axis=1).astype(BF16))
        dens.append(jnp.where(out_lane < HEAD_DIM, den[0], den[1]))
    o = [jnp.dot(probs[idx], v_st[b][kv_of(j)], preferred_element_type=F32) / dens[idx]
         for idx, (b, j) in enumerate(pairs)]
    n_pairs = ATT_HEADS // 2
    for b in range(nseq):
        o_ref[b] = jnp.concatenate(o[b * n_pairs:(b + 1) * n_pairs], axis=1).astype(o_ref.dtype)


def _attn_prompt(p3, sinks, nseq):
    batch, seq = p3.shape[0], p3.shape[1]
    nb = seq // QB
    kcol, vcol = P_AK // KV_WIDTH, P_AV // KV_WIDTH
    prev = lambda col: pl.BlockSpec((nseq, QB, KV_WIDTH), lambda g, m: (g, jnp.maximum(m - 1, 0), col))
    cur = lambda col: pl.BlockSpec((nseq, QB, KV_WIDTH), lambda g, m: (g, m, col))
    return pl.pallas_call(
        _attn_prompt_kernel,
        grid=(batch // nseq, nb),
        in_specs=[
            pl.BlockSpec(memory_space=pltpu.SMEM),
            pl.BlockSpec((nseq, QB, ATT_WIDTH), lambda g, m: (g, m, P_Q // ATT_WIDTH)),
            prev(kcol), cur(kcol), prev(vcol), cur(vcol),
        ],
        out_specs=pl.BlockSpec((nseq, QB, ATT_WIDTH), lambda g, m: (g, m, 0)),
        out_shape=jax.ShapeDtypeStruct((batch, seq, ATT_WIDTH), BF16),
        compiler_params=_cparams(("arbitrary", "arbitrary")),
        name="attn_prompt",
    )(sinks, p3, p3, p3, p3, p3)


def _attn_sample_kernel(sink_ref, q_ref, kn_ref, vn_ref, kc_ref, vc_ref, o_ref, kw_ref, vw_ref):
    t = q_ref.shape[0]
    k = jnp.concatenate([kc_ref[...], kn_ref[...]], axis=0)
    v = jnp.concatenate([vc_ref[...], vn_ref[...]], axis=0)
    o_ref[...] = _attend(q_ref[...], k, v, None, sink_ref).astype(o_ref.dtype)
    kw_ref[...] = k[t:]
    vw_ref[...] = v[t:]


def _attn_sample(p, sinks, cache_k, cache_v, batch, seq):
    kcol, vcol = P_AK // KV_WIDTH, P_AV // KV_WIDTH
    return pl.pallas_call(
        _attn_sample_kernel,
        grid=(batch,),
        in_specs=[
            pl.BlockSpec(memory_space=pltpu.SMEM),
            pl.BlockSpec((seq, ATT_WIDTH), lambda b: (b, P_Q // ATT_WIDTH)),
            pl.BlockSpec((seq, KV_WIDTH), lambda b: (b, kcol)),
            pl.BlockSpec((seq, KV_WIDTH), lambda b: (b, vcol)),
            pl.BlockSpec((WINDOW, KV_WIDTH), lambda b: (b, 0)),
            pl.BlockSpec((WINDOW, KV_WIDTH), lambda b: (b, 0)),
        ],
        out_specs=[
            pl.BlockSpec((seq, ATT_WIDTH), lambda b: (b, 0)),
            pl.BlockSpec((WINDOW, KV_WIDTH), lambda b: (b, 0)),
            pl.BlockSpec((WINDOW, KV_WIDTH), lambda b: (b, 0)),
        ],
        out_shape=[
            jax.ShapeDtypeStruct((batch * seq, ATT_WIDTH), BF16),
            jax.ShapeDtypeStruct((batch * WINDOW, KV_WIDTH), F32),
            jax.ShapeDtypeStruct((batch * WINDOW, KV_WIDTH), F32),
        ],
        compiler_params=_cparams(("arbitrary",)),
        name="attn_sample",
    )(sinks, p, p, p, cache_k, cache_v)


SEG = 256
EXP_NEG_HALF = math.exp(-0.5)
PAIR = 2 * HEAD_DIM
N_PAIRS = RW_HEADS // 2


def _split2(x):
    hi = x.astype(BF16)
    lo = (x - hi.astype(F32)).astype(BF16)
    return hi, lo


def _seg_sum(x, bd):
    c = x.shape[0]
    xb = x.astype(BF16)
    stack = jnp.concatenate([xb[:, s:s + SEG] for s in range(0, RW_WIDTH, SEG)], axis=0)
    r = jnp.dot(stack, bd, preferred_element_type=F32)
    return jnp.concatenate([r[i * c:(i + 1) * c] for i in range(RW_WIDTH // SEG)], axis=1)


def _shift_rows(x, prev_row):
    row = lax.broadcasted_iota(jnp.int32, x.shape, 0)
    return jnp.where(row == 0, prev_row, pltpu.roll(x, 1, 0))


def _dotb(a, b):
    return jnp.dot(a.astype(BF16), b.astype(BF16), preferred_element_type=F32)


def _dotb_nt(a, b):
    return lax.dot_general(a.astype(BF16), b.astype(BF16), (((1,), (1,)), ((), ())), preferred_element_type=F32)


def _dot3_tn(a, b):
    ah, al = _split2(a)
    bh, bl = _split2(b)
    return lax.dot_general(jnp.concatenate([ah, al, ah], axis=0), jnp.concatenate([bh, bh, bl], axis=0),
                           (((0,), (0,)), ((), ())), preferred_element_type=F32)


def _dot2_tn(a, b):
    ah, al = _split2(a)
    bh = b.astype(BF16)
    return lax.dot_general(jnp.concatenate([ah, al], axis=0), jnp.concatenate([bh, bh], axis=0),
                           (((0,), (0,)), ((), ())), preferred_element_type=F32)


def _dot2(a, b):
    ah, al = _split2(a)
    bh = b.astype(BF16)
    return jnp.dot(jnp.concatenate([ah, al], axis=1), jnp.concatenate([bh, bh], axis=0), preferred_element_type=F32)


def _stack_heads(x, half):
    lane = lax.broadcasted_iota(jnp.int32, x.shape, 1)
    first = (lane % (2 * half)) < half
    zero = jnp.zeros_like(x)
    return jnp.concatenate([jnp.where(first, x, zero), jnp.where(first, zero, x)], axis=0)


def _stack_heads_bf16(x, half):
    return _stack_heads(x.astype(BF16), half)


def _rwkv_pairs(ab, rb, bb, kb, v, gam_c, s_bd, n_double):
    c = ab[0].shape[0]
    pairs = range(len(ab))
    t_idx = lax.broadcasted_iota(jnp.int32, (c, 2 * c), 0)
    s_idx = lax.broadcasted_iota(jnp.int32, (c, 2 * c), 1) % c
    strict, incl = s_idx < t_idx, s_idx <= t_idx
    g = [_dotb_nt(jnp.concatenate([ab[p], rb[p]], axis=0),
                  jnp.concatenate([_stack_heads_bf16(bb[p], HEAD_DIM), _stack_heads_bf16(kb[p], HEAD_DIM)], axis=0))
         for p in pairs]
    l_ak = [jnp.where(strict, g[p][:c, 2 * c:], 0.0) for p in pairs]
    m_rbk = [jnp.concatenate([jnp.where(incl, g[p][c:, :2 * c], 0.0), jnp.where(incl, g[p][c:, 2 * c:], 0.0)], axis=1)
             for p in pairs]
    lp = [jnp.where(strict, g[p][:c, :2 * c], 0.0) for p in pairs]
    v_st = [_stack_heads_bf16(v[p], HEAD_DIM) for p in pairs]
    ya = list(ab)
    yu = [_dotb(l_ak[p], v_st[p]) for p in pairs]
    for i in range(n_double):
        y_st = [_stack_heads_bf16(jnp.concatenate([ya[p], yu[p]], axis=1), HEAD_DIM) for p in pairs]
        if i + 1 < n_double:
            z = [_dotb(lp[p], jnp.concatenate([y_st[p], _stack_heads_bf16(lp[p], c)], axis=1)) for p in pairs]
            lp = [z[p][:, 2 * PAIR:] for p in pairs]
        else:
            z = [_dotb(lp[p], y_st[p]) for p in pairs]
        ya = [ya[p] + z[p][:, :PAIR] for p in pairs]
        yu = [yu[p] + z[p][:, PAIR:2 * PAIR] for p in pairs]
    z = []
    for p in pairs:
        top = _stack_heads_bf16(jnp.concatenate([ya[p], yu[p]], axis=1), HEAD_DIM)
        bot = jnp.concatenate([jnp.zeros_like(v_st[p]), v_st[p]], axis=1)
        z.append(_dotb(m_rbk[p], jnp.concatenate([top, bot], axis=0)))
    o = [_dotb_nt(rb[p] + z[p][:, :PAIR], s_bd[p]) + z[p][:, PAIR:] for p in pairs]
    row_h = lax.broadcasted_iota(jnp.int32, (PAIR, PAIR), 0) // HEAD_DIM
    col_h = lax.broadcasted_iota(jnp.int32, (PAIR, PAIR), 1) // HEAD_DIM
    same = row_h == col_h
    q_t = [jnp.where(same, _dot3_tn(ya[p], bb[p]), 0.0) for p in pairs]
    d_t = [jnp.where(same, _dot2_tn(jnp.concatenate([yu[p], v[p]], axis=0),
                                    jnp.concatenate([bb[p], kb[p]], axis=0)), 0.0) for p in pairs]
    s_new = [(s_bd[p] + _dot2(s_bd[p], q_t[p]) + d_t[p]) * gam_c[p] for p in pairs]
    return o, s_new


def _split_rw_rows(x):
    o_wd, o_gd = RW_WIDTH, RW_COLS - GATE_LORA
    low = lax.broadcasted_iota(jnp.int32, (x.shape[0], LANES), 1) < HEAD_DIM
    pieces = [x[:, c0:c0 + LANES] for c0 in range(o_wd, o_gd, LANES)]
    rolled = [pltpu.roll(p, HEAD_DIM, 1) for p in pieces]
    merged = [jnp.where(low, rolled[m], rolled[m + 1]) for m in range(len(pieces) - 1)]
    n = RW_WIDTH // LANES
    return (x[:, :RW_WIDTH], jnp.concatenate(merged[:n], axis=1), jnp.concatenate(merged[n:], axis=1),
            jnp.where(low, pieces[0], pieces[-1]), x[:, o_gd:])


def _rwkv_kernel(xr_ref, xk_ref, xv_ref, xl_ref, sh_ref, mu_ref, w0_ref, a0_ref, kk_ref, ka_ref, rk_ref,
                 lg_ref, lb_ref, w2_ref, a2_ref, g2_ref, bd_ref, st0_ref,
                 out_ref, st_ref,
                 s_scr, pr_scr, pk_scr, pv_scr, pw_scr, pg_scr):
    c = pl.program_id(1)
    nc = pl.num_programs(1)
    nb, C = xr_ref.shape[0], xr_ref.shape[1]
    n_double = int(math.log2(C))
    prev_scr = (pr_scr, pk_scr, pv_scr, pw_scr, pg_scr)

    @pl.when(c == 0)
    def _():
        s_scr[...] = jnp.zeros_like(s_scr)
        for b in range(nb):
            for h in range(RW_HEADS):
                off = (h % 2) * HEAD_DIM
                s_scr[b * N_PAIRS + h // 2, off:off + HEAD_DIM, off:off + HEAD_DIM] = st0_ref[b, h]
            for scr, row in zip(prev_scr, _split_rw_rows(sh_ref[b])):
                scr[b, 0:1, :] = row

    def shifted(load, p_scr, mu):
        parts = []
        for b in range(nb):
            x = load(b)
            prev = _shift_rows(x, p_scr[b, 0:1, :])
            p_scr[b, 0:1, :] = x[C - 1:C, :]
            parts.append(x + (prev - x) * mu)
        return jnp.concatenate(parts, axis=0)

    mu_r, mu_k, mu_v, mu_w, mu_g = _split_rw_rows(mu_ref[...])
    wa_w = DECAY_LORA + AAA_LORA
    r = shifted(lambda b: xr_ref[b], pr_scr, mu_r)
    kr = shifted(lambda b: xk_ref[b], pk_scr, mu_k)
    v = shifted(lambda b: xv_ref[b], pv_scr, mu_v)
    xw = shifted(lambda b: xl_ref[b, :, :wa_w], pw_scr, mu_w)
    gd = shifted(lambda b: xl_ref[b, :, wa_w:wa_w + GATE_LORA], pg_scr, mu_g)
    wd = xw[:, 0:DECAY_LORA]
    ad = xw[:, DECAY_LORA:wa_w]

    lw = -EXP_NEG_HALF * jax.nn.sigmoid(w0_ref[...] + _dotb(jnp.tanh(wd), w2_ref[...]))
    a = jax.nn.sigmoid(a0_ref[...] + _dotb(ad, a2_ref[...]))
    g = _dotb(jax.nn.sigmoid(gd), g2_ref[...])
    bd = bd_ref[...]
    kk = kr * kk_ref[...]
    kk = kk * lax.rsqrt(jnp.maximum(_seg_sum(kk * kk, bd), 1e-24))
    k2 = kr * (1.0 + (a - 1.0) * ka_ref[...])

    rows = nb * C
    ti = lax.broadcasted_iota(jnp.int32, (rows, rows), 0)
    si = lax.broadcasted_iota(jnp.int32, (rows, rows), 1)
    tri = jnp.logical_and(si <= ti, si // C == ti // C).astype(BF16)
    cs2 = jnp.dot(tri, jnp.concatenate(_split2(lw), axis=1), preferred_element_type=F32)
    cs = cs2[:, :RW_WIDTH] + cs2[:, RW_WIDTH:]
    gam = jnp.exp(cs)
    ginv = jnp.exp(-cs)
    ab = -kk * jnp.exp(cs - lw)
    rb = r * gam
    bb = kk * a * ginv
    kb = k2 * ginv

    def per_pair(x, r0, r1):
        return [x[b * C + r0:b * C + r1, p * PAIR:(p + 1) * PAIR] for b in range(nb) for p in range(N_PAIRS)]

    chains = nb * N_PAIRS
    outs, s_new = _rwkv_pairs(per_pair(ab, 0, C), per_pair(rb, 0, C), per_pair(bb, 0, C), per_pair(kb, 0, C),
                              per_pair(v, 0, C), per_pair(gam, C - 1, C), [s_scr[q] for q in range(chains)], n_double)
    for q in range(chains):
        s_scr[q] = s_new[q]
    o = jnp.concatenate([jnp.concatenate(outs[b * N_PAIRS:(b + 1) * N_PAIRS], axis=1) for b in range(nb)], axis=0)

    mo = _seg_sum(o, bd) * (1.0 / HEAD_DIM)
    d = o - mo
    vo = _seg_sum(d * d, bd) * (1.0 / HEAD_DIM)
    on = d * lax.rsqrt(vo + LNX_EPS) * lg_ref[...] + lb_ref[...]
    bonus = _seg_sum(r * k2 * rk_ref[...], bd) * v
    res = ((on + bonus) * g).astype(out_ref.dtype)
    for b in range(nb):
        out_ref[b] = res[b * C:(b + 1) * C, :]

    @pl.when(c == nc - 1)
    def _():
        for b in range(nb):
            for h in range(RW_HEADS):
                off = (h % 2) * HEAD_DIM
                st_ref[b, h] = s_scr[b * N_PAIRS + h // 2, off:off + HEAD_DIM, off:off + HEAD_DIM]


def _rwkv(p3, state_shift, state0, prm, chunk, nb):
    batch, seq = p3.shape[0], p3.shape[1]
    nc = seq // chunk
    row = lambda w: pl.BlockSpec((1, w), lambda g, c: (0, 0))
    xspec = lambda w, col: pl.BlockSpec((nb, chunk, w), lambda g, c: (g, c, col))
    full = lambda a: pl.BlockSpec(a.shape, lambda g, c: (0,) * a.ndim)
    stspec = pl.BlockSpec((nb, RW_HEADS, HEAD_DIM, HEAD_DIM), lambda g, c: (g, 0, 0, 0))
    return pl.pallas_call(
        _rwkv_kernel,
        grid=(batch // nb, nc),
        in_specs=[
            xspec(RW_WIDTH, P_R // RW_WIDTH), xspec(RW_WIDTH, P_K // RW_WIDTH), xspec(RW_WIDTH, P_V // RW_WIDTH),
            xspec(LORA_W, P_LORA // LORA_W),
            pl.BlockSpec((nb, 1, RW_COLS), lambda g, c: (g, 0, 0)),
            row(RW_COLS),
            row(RW_WIDTH), row(RW_WIDTH), row(RW_WIDTH), row(RW_WIDTH), row(RW_WIDTH),
            row(RW_WIDTH), row(RW_WIDTH),
            full(prm["w2"]), full(prm["a2"]), full(prm["g2"]), full(prm["bd"]),
            stspec,
        ],
        out_specs=[
            pl.BlockSpec((nb, chunk, RW_WIDTH), lambda g, c: (g, c, 0)),
            stspec,
        ],
        out_shape=[
            jax.ShapeDtypeStruct((batch, seq, RW_WIDTH), BF16),
            jax.ShapeDtypeStruct((batch, RW_HEADS, HEAD_DIM, HEAD_DIM), F32),
        ],
        scratch_shapes=[
            pltpu.VMEM((nb * N_PAIRS, PAIR, PAIR), F32),
            pltpu.VMEM((nb, SUBLANES, RW_WIDTH), F32), pltpu.VMEM((nb, SUBLANES, RW_WIDTH), F32),
            pltpu.VMEM((nb, SUBLANES, RW_WIDTH), F32),
            pltpu.VMEM((nb, SUBLANES, DECAY_LORA + AAA_LORA), F32), pltpu.VMEM((nb, SUBLANES, GATE_LORA), F32),
        ],
        compiler_params=_cparams(("arbitrary", "arbitrary")),
        name="rwkv",
    )(p3, p3, p3, p3, state_shift, prm["mu"], prm["w0"], prm["a0"], prm["k_k"], prm["k_a"], prm["r_k"],
      prm["lnx_g"], prm["lnx_b"], prm["w2"], prm["a2"], prm["g2"], prm["bd"], state0)


D_TN = 512


def _out_proj_body(att_ref, rw_ref, h_ref, g_ref, b_ref, h1_ref, h1b_ref, w_scr):
    for c0 in range(0, D_MODEL, D_TN):
        m = jnp.dot(att_ref[...], w_scr[:ATT_WIDTH, c0:c0 + D_TN], preferred_element_type=F32)
        m = m + jnp.dot(rw_ref[...], w_scr[ATT_WIDTH:, c0:c0 + D_TN], preferred_element_type=F32)
        h1_ref[:, c0:c0 + D_TN] = ALPHA * h_ref[:, c0:c0 + D_TN] + m
    y = _layer_norm(h1_ref[...], g_ref[...], b_ref[...])
    h1_ref[...] = y
    h1b_ref[...] = y.astype(BF16)


def _out_proj(att_p, rw_p, h_p, att_s, rw_s, h_s, w_out, g, b, tm):
    n_p, n_s = att_p.shape[0], att_s.shape[0]
    assert n_s == tm
    nrb = n_p // tm
    last = lambda i: jnp.minimum(i, nrb - 1)

    def kern(ap_ref, rp_ref, hp_ref, as_ref, rs_ref, hs_ref, wa_ref, wb_ref, g_ref, b_ref,
             h1p_ref, h1bp_ref, h1s_ref, h1bs_ref, w_scr):
        i = pl.program_id(0)

        @pl.when(i == 0)
        def _():
            w_scr[:ATT_WIDTH, :] = wa_ref[...].astype(BF16)
            w_scr[ATT_WIDTH:, :] = wb_ref[...].astype(BF16)

        pl.when(i < nrb)(lambda: _out_proj_body(ap_ref, rp_ref, hp_ref, g_ref, b_ref, h1p_ref, h1bp_ref, w_scr))
        pl.when(i == nrb)(lambda: _out_proj_body(as_ref, rs_ref, hs_ref, g_ref, b_ref, h1s_ref, h1bs_ref, w_scr))

    rows_p = lambda w: pl.BlockSpec((tm, w), lambda i: (last(i), 0))
    rows_s = lambda w: pl.BlockSpec((tm, w), lambda i: (0, 0))
    return pl.pallas_call(
        kern,
        grid=(nrb + 1,),
        in_specs=[
            rows_p(ATT_WIDTH), rows_p(RW_WIDTH), rows_p(D_MODEL),
            rows_s(ATT_WIDTH), rows_s(RW_WIDTH), rows_s(D_MODEL),
            pl.BlockSpec((ATT_WIDTH, D_MODEL), lambda i: (0, 0), pipeline_mode=pl.Buffered(1)),
            pl.BlockSpec((RW_WIDTH, D_MODEL), lambda i: (1, 0), pipeline_mode=pl.Buffered(1)),
            pl.BlockSpec((1, D_MODEL), lambda i: (0, 0)),
            pl.BlockSpec((1, D_MODEL), lambda i: (0, 0)),
        ],
        out_specs=[rows_p(D_MODEL), rows_p(D_MODEL), rows_s(D_MODEL), rows_s(D_MODEL)],
        out_shape=[
            jax.ShapeDtypeStruct((n_p, D_MODEL), F32), jax.ShapeDtypeStruct((n_p, D_MODEL), BF16),
            jax.ShapeDtypeStruct((n_s, D_MODEL), F32), jax.ShapeDtypeStruct((n_s, D_MODEL), BF16),
        ],
        scratch_shapes=[pltpu.VMEM((ATT_WIDTH + RW_WIDTH, D_MODEL), BF16)],
        compiler_params=_cparams(("arbitrary",)),
        name="out_proj",
    )(att_p, rw_p, h_p, att_s, rw_s, h_s, w_out, w_out, g, b)


E_TN = 512
E_NCB = D_FF // E_TN
E_SUB = 512
SQRT_HALF = math.sqrt(0.5)


def _ffn_up_body(bps, nseq, i, h_ref, w_scr, cwg_ref, cwv_ref, cbg_ref, cbv_ref, sg_ref, sv_ref,
                 act_ref, og_ref, ov_ref, cg_scr, cv_scr):
    tm = h_ref.shape[0]
    sub = min(tm, E_SUB)
    n_sub = tm // sub
    t_seq = tm // nseq
    row8 = lax.broadcasted_iota(jnp.int32, (SUBLANES, E_TN), 0)

    def products(r0):
        u = jnp.dot(h_ref[r0:r0 + sub, :], w_scr[...], preferred_element_type=F32)
        return u[:, :E_TN], u[:, E_TN:]

    def conv(u, prevs, cw_ref, cb_ref):
        sh1, sh2 = pltpu.roll(u, 1, 0), pltpu.roll(u, 2, 0)
        for r, prev in prevs:
            p2, p1 = prev[0:1, :], prev[1:2, :]
            fix1 = jnp.where(row8 == 0, p1, sh1[r:r + SUBLANES, :])
            fix2 = jnp.where(row8 == 0, p2, jnp.where(row8 == 1, p1, sh2[r:r + SUBLANES, :]))
            head = lambda x: [x[:r, :]] if r > 0 else []
            tail = lambda x: [x[r + SUBLANES:, :]] if r + SUBLANES < x.shape[0] else []
            sh1 = jnp.concatenate(head(sh1) + [fix1] + tail(sh1), axis=0)
            sh2 = jnp.concatenate(head(sh2) + [fix2] + tail(sh2), axis=0)
        cw = cw_ref[...]
        c = cb_ref[...] + sh2 * cw[0:1, :]
        c = c + sh1 * cw[1:2, :]
        return c + u * cw[2:3, :]

    def epilogue(r0, ug, uv, prev_g, prev_v):
        gate = conv(ug, prev_g, cwg_ref, cbg_ref)
        val = conv(uv, prev_v, cwv_ref, cbv_ref)
        gelu = 0.5 * gate * (1.0 + lax.erf(gate * SQRT_HALF))
        act_ref[r0:r0 + sub, :] = (gelu * val).astype(act_ref.dtype)

    if nseq > 1:
        ug, uv = products(0)
        epilogue(0, ug, uv, [(s * t_seq, sg_ref[s]) for s in range(nseq)],
                 [(s * t_seq, sv_ref[s]) for s in range(nseq)])
        for s in range(nseq):
            og_ref[s] = ug[(s + 1) * t_seq - SUBLANES:(s + 1) * t_seq, :]
            ov_ref[s] = uv[(s + 1) * t_seq - SUBLANES:(s + 1) * t_seq, :]
    else:
        first = (i % bps) == 0
        prev_g = jnp.where(first, sg_ref[0], cg_scr[SUBLANES - 2:SUBLANES, :])
        prev_v = jnp.where(first, sv_ref[0], cv_scr[SUBLANES - 2:SUBLANES, :])
        nxt = products(0)
        for k in range(n_sub):
            ug, uv = nxt
            if k + 1 < n_sub:
                nxt = products((k + 1) * sub)
            epilogue(k * sub, ug, uv, [(0, prev_g)], [(0, prev_v)])
            prev_g, prev_v = ug[sub - 2:sub, :], uv[sub - 2:sub, :]
        cg_scr[...] = ug[sub - SUBLANES:sub, :]
        cv_scr[...] = uv[sub - SUBLANES:sub, :]
        og_ref[0] = ug[sub - SUBLANES:sub, :]
        ov_ref[0] = uv[sub - SUBLANES:sub, :]


def _ffn_up(h_p, conv_prev_p, seq_p, tm_p, h_s, conv_prev_s, seq_s, w_up, conv_w, conv_b, w_down):
    n_p, n_s = h_p.shape[0], h_s.shape[0]
    batch_p, batch_s = n_p // seq_p, n_s // seq_s
    bps = seq_p // tm_p
    nrb = n_p // tm_p
    slab = D_FF // (E_NCB * nrb)
    last = lambda i: jnp.maximum(i - 1, 0)

    def kern(hp_ref, sgp_ref, svp_ref, hs_ref, sgs_ref, svs_ref, wg_ref, wv_ref, cwg_ref, cwv_ref, cbg_ref, cbv_ref,
             wd_ref, actp_ref, ogp_ref, ovp_ref, acts_ref, ogs_ref, ovs_ref, wdb_ref, w_scr, cg_scr, cv_scr):
        i = pl.program_id(1)

        @pl.when(i == 0)
        def _():
            w_scr[:, :E_TN] = wg_ref[...].astype(BF16)
            w_scr[:, E_TN:] = wv_ref[...].astype(BF16)
            _ffn_up_body(1, batch_s, i, hs_ref, w_scr, cwg_ref, cwv_ref, cbg_ref, cbv_ref, sgs_ref, svs_ref,
                         acts_ref, ogs_ref, ovs_ref, cg_scr, cv_scr)

        @pl.when(i > 0)
        def _():
            wdb_ref[...] = wd_ref[...].astype(BF16)
            _ffn_up_body(bps, 1, i - 1, hp_ref, w_scr, cwg_ref, cwv_ref, cbg_ref, cbv_ref, sgp_ref, svp_ref,
                         actp_ref, ogp_ref, ovp_ref, cg_scr, cv_scr)

    col = lambda shape, off: pl.BlockSpec(shape, lambda j, i: (0,) * (len(shape) - 1) + (j + off,))
    st_p = lambda off: pl.BlockSpec((1, CONV_W - 1, E_TN), lambda j, i: (last(i) // bps, 0, j + off))
    st_s = lambda off: pl.BlockSpec((batch_s, CONV_W - 1, E_TN), lambda j, i: (0, 0, j + off))
    tail_p = pl.BlockSpec((1, SUBLANES, E_TN), lambda j, i: (last(i) // bps, 0, j))
    tail_s = pl.BlockSpec((batch_s, SUBLANES, E_TN), lambda j, i: (0, 0, j))
    wd_spec = pl.BlockSpec((slab, D_MODEL), lambda j, i: (j * nrb + last(i), 0))
    return pl.pallas_call(
        kern,
        grid=(E_NCB, nrb + 1),
        in_specs=[
            pl.BlockSpec((tm_p, D_MODEL), lambda j, i: (last(i), 0)), st_p(0), st_p(E_NCB),
            pl.BlockSpec((n_s, D_MODEL), lambda j, i: (0, 0)), st_s(0), st_s(E_NCB),
            col((D_MODEL, E_TN), 0), col((D_MODEL, E_TN), E_NCB),
            col((CONV_W, E_TN), 0), col((CONV_W, E_TN), E_NCB),
            col((1, E_TN), 0), col((1, E_TN), E_NCB),
            wd_spec,
        ],
        out_specs=[
            pl.BlockSpec((tm_p, E_TN), lambda j, i: (last(i), j)), tail_p, tail_p,
            pl.BlockSpec((n_s, E_TN), lambda j, i: (0, j)), tail_s, tail_s,
            wd_spec,
        ],
        out_shape=[
            jax.ShapeDtypeStruct((n_p, D_FF), BF16),
            jax.ShapeDtypeStruct((batch_p, SUBLANES, D_FF), F32), jax.ShapeDtypeStruct((batch_p, SUBLANES, D_FF), F32),
            jax.ShapeDtypeStruct((n_s, D_FF), BF16),
            jax.ShapeDtypeStruct((batch_s, SUBLANES, D_FF), F32), jax.ShapeDtypeStruct((batch_s, SUBLANES, D_FF), F32),
            jax.ShapeDtypeStruct((D_FF, D_MODEL), BF16),
        ],
        scratch_shapes=[pltpu.VMEM((D_MODEL, 2 * E_TN), BF16),
                        pltpu.VMEM((SUBLANES, E_TN), F32), pltpu.VMEM((SUBLANES, E_TN), F32)],
        compiler_params=_cparams(("arbitrary", "arbitrary")),
        name="ffn_up",
    )(h_p, conv_prev_p, conv_prev_p, h_s, conv_prev_s, conv_prev_s, w_up, w_up, conv_w, conv_w, conv_b, conv_b, w_down)


F_TN = 512


def _ffn_down_kernel(act_ref, w_ref, h1_ref, g_ref, b_ref, y_ref):
    for c0 in range(0, D_MODEL, F_TN):
        f = jnp.dot(act_ref[...], w_ref[:, c0:c0 + F_TN], preferred_element_type=F32)
        y_ref[:, c0:c0 + F_TN] = ALPHA * h1_ref[:, c0:c0 + F_TN] + f
    y_ref[...] = _layer_norm(y_ref[...], g_ref[...], b_ref[...])


def _ffn_down(act_p, h1_p, act_s, h1_s, w_down, g, b, tm):
    n_p, n_s = act_p.shape[0], act_s.shape[0]
    assert n_s == tm
    nrb = n_p // tm
    last = lambda i: jnp.minimum(i, nrb - 1)

    def kern(ap_ref, hp_ref, as_ref, hs_ref, w_ref, g_ref, b_ref, yp_ref, ys_ref):
        i = pl.program_id(0)
        pl.when(i < nrb)(lambda: _ffn_down_kernel(ap_ref, w_ref, hp_ref, g_ref, b_ref, yp_ref))
        pl.when(i == nrb)(lambda: _ffn_down_kernel(as_ref, w_ref, hs_ref, g_ref, b_ref, ys_ref))

    rows_p = lambda w: pl.BlockSpec((tm, w), lambda i: (last(i), 0))
    rows_s = lambda w: pl.BlockSpec((tm, w), lambda i: (0, 0))
    return pl.pallas_call(
        kern,
        grid=(nrb + 1,),
        in_specs=[
            rows_p(D_FF), rows_p(D_MODEL), rows_s(D_FF), rows_s(D_MODEL),
            pl.BlockSpec((D_FF, D_MODEL), lambda i: (0, 0), pipeline_mode=pl.Buffered(1)),
            pl.BlockSpec((1, D_MODEL), lambda i: (0, 0)),
            pl.BlockSpec((1, D_MODEL), lambda i: (0, 0)),
        ],
        out_specs=[rows_p(D_MODEL), rows_s(D_MODEL)],
        out_shape=[jax.ShapeDtypeStruct((n_p, D_MODEL), F32), jax.ShapeDtypeStruct((n_s, D_MODEL), F32)],
        compiler_params=_cparams(("arbitrary",)),
        name="ffn_down",
    )(act_p, h1_p, act_s, h1_s, w_down, g, b)


def _rope_tables(pos, reps):
    half = HEAD_DIM // 2
    inv = ROPE_THETA ** (-jnp.arange(half, dtype=F32) / half)
    ang = pos.astype(F32)[:, None] * inv[None, :]
    cos, sin = jnp.cos(ang), jnp.sin(ang)
    cos128 = jnp.concatenate([cos, cos, cos, cos], axis=1)
    sin128 = jnp.concatenate([-sin, sin, -sin, sin], axis=1)
    return jnp.tile(cos128, (reps, 1)), jnp.tile(sin128, (reps, 1))


TM = 256
E_TM = 1024
RWKV_NB = 4
ATTN_NB = 2


def kernel(x_prompt, x_sample, cache_k, cache_v, state_wkv, state_shift, state_ffn_conv, ln_in_g, ln_in_b, w_in, attn_sinks, rw_mu, rw_w0, rw_w2, rw_a0, rw_a2, rw_g2, rw_k_k, rw_k_a, rw_r_k, rw_lnx_g, rw_lnx_b, w_out, ln1_g, ln1_b, ffn_w_up, ffn_conv_w, ffn_conv_b, ffn_w_down, ln2_g, ln2_b):
    l = 0
    seg = lax.broadcasted_iota(jnp.int32, (SEG, SEG), 0) // HEAD_DIM
    bd = (seg == seg.T).astype(BF16)
    row = lambda a: a.reshape(1, -1)
    rw_prm = {"mu": row(rw_mu[l]), "w0": row(rw_w0[l]), "a0": row(rw_a0[l]), "k_k": row(rw_k_k[l]),
              "k_a": row(rw_k_a[l]), "r_k": row(rw_r_k[l]), "lnx_g": row(rw_lnx_g[l]), "lnx_b": row(rw_lnx_b[l]),
              "w2": rw_w2[l].astype(BF16), "a2": rw_a2[l].astype(BF16), "g2": rw_g2[l].astype(BF16), "bd": bd}
    sinks = attn_sinks[l]
    bp, tp = x_prompt.shape[0], x_prompt.shape[1]
    bs, ts = x_sample.shape[0], x_sample.shape[1]
    n_p, n_s = bp * tp, bs * ts
    zeros = lambda *s: jnp.zeros(s, F32)

    p_p, h_p, tails_p, p_s, h_s, tails_s = _in_proj(
        x_prompt.reshape(n_p, D_MODEL), x_sample.reshape(n_s, D_MODEL), row(ln_in_g), row(ln_in_b),
        w_in[l].astype(BF16), _rope_tables(jnp.arange(tp, dtype=jnp.int32), 1),
        _rope_tables(PAST_LEN + jnp.arange(ts, dtype=jnp.int32), bs), TM, tp, ts)
    p3_p, p3_s = p_p.reshape(bp, tp, P_COLS), p_s.reshape(bs, ts, P_COLS)

    att_p = _attn_prompt(p3_p, sinks, ATTN_NB).reshape(n_p, ATT_WIDTH)
    att_s, s_k, s_v = _attn_sample(p_s, sinks, cache_k[l].reshape(bs * WINDOW, KV_WIDTH),
                                   cache_v[l].reshape(bs * WINDOW, KV_WIDTH), bs, ts)
    rw_p, p_wkv = _rwkv(p3_p, zeros(bp, 1, RW_COLS), zeros(bp, RW_HEADS, HEAD_DIM, HEAD_DIM), rw_prm, CHUNK, RWKV_NB)
    rw_s, s_wkv = _rwkv(p3_s, state_shift[l], state_wkv[l], rw_prm, ts, RWKV_NB)

    h1_p, h1b_p, h1_s, h1b_s = _out_proj(att_p, rw_p.reshape(n_p, RW_WIDTH), h_p, att_s, rw_s.reshape(n_s, RW_WIDTH), h_s,
                                         w_out[l], row(ln1_g[l]), row(ln1_b[l]), TM)
    act_p, cg_p, cv_p, act_s, cg_s, cv_s, w_down_b = _ffn_up(
        h1b_p, zeros(bp, CONV_W - 1, 2 * D_FF), tp, E_TM, h1b_s, state_ffn_conv[l], ts,
        ffn_w_up[l], ffn_conv_w[l], row(ffn_conv_b[l]), ffn_w_down[l])
    y_p, y_s = _ffn_down(act_p, h1_p, act_s, h1_s, w_down_b, row(ln2_g[l]), row(ln2_b[l]), TM)

    kv_shape = lambda b: (1, b, WINDOW, ATT_KV_HEADS, HEAD_DIM)
    p_k = p3_p[:, tp - WINDOW:, P_AK:P_AK + KV_WIDTH].reshape(kv_shape(bp))
    p_v = p3_p[:, tp - WINDOW:, P_AV:P_AV + KV_WIDTH].reshape(kv_shape(bp))
    shift_of = lambda tails, b: tails.reshape(b, -1, RW_COLS)[:, -1:, :][None]
    conv_of = lambda cg, cv: jnp.concatenate([cg[:, SUBLANES - 2:, :], cv[:, SUBLANES - 2:, :]], axis=-1)[None]
    return (y_p.reshape(bp, tp, D_MODEL), y_s.reshape(bs, ts, D_MODEL), p_k, p_v, p_wkv[None],
            shift_of(tails_p, bp), conv_of(cg_p, cv_p),
            s_k.reshape(kv_shape(bs)), s_v.reshape(kv_shape(bs)), s_wkv[None],
            shift_of(tails_s, bs), conv_of(cg_s, cv_s))
```

```python
import functools
import math

import jax
import jax.numpy as jnp
from jax import lax
from jax.experimental import pallas as pl
from jax.experimental.pallas import tpu as pltpu

D_MODEL = 2048
HEAD_DIM = 64
ATT_HEADS = 16
ATT_KV_HEADS = 2
ATT_GROUP = ATT_HEADS // ATT_KV_HEADS
ATT_WIDTH = ATT_HEADS * HEAD_DIM
KV_WIDTH = ATT_KV_HEADS * HEAD_DIM
CHUNK = 64
WINDOW = 128
ROPE_THETA = 10000.0
ATT_SCALE = HEAD_DIM ** -0.5
RW_HEADS = 16
RW_WIDTH = RW_HEADS * HEAD_DIM
DECAY_LORA = 64
AAA_LORA = 64
GATE_LORA = 160
LNX_EPS = 64e-5
ATT_COLS = ATT_WIDTH + 2 * KV_WIDTH
RW_COLS = 3 * RW_WIDTH + DECAY_LORA + AAA_LORA + GATE_LORA
D_FF = 5632
CONV_W = 3
LN_EPS = 1e-5
DEPTH = 1
ALPHA = (2 * DEPTH) ** 0.25
PAST_LEN = 1024

LORA_W = 512
P_R, P_K, P_V, P_Q = 0, 1024, 2048, 3072
P_LORA = 4096
P_AK = P_LORA + LORA_W
P_AV = P_AK + KV_WIDTH
P_COLS = P_AV + KV_WIDTH
LANES = 128
SUBLANES = 8
VMEM_LIMIT = 56 * 1024 * 1024

F32 = jnp.float32
BF16 = jnp.bfloat16


def _cparams(sem):
    return pltpu.CompilerParams(dimension_semantics=sem, vmem_limit_bytes=VMEM_LIMIT)


def _layer_norm(y, g, b):
    mu = jnp.mean(y, -1, keepdims=True)
    d = y - mu
    var = jnp.mean(d * d, -1, keepdims=True)
    return d * lax.rsqrt(var + LN_EPS) * g + b


A_TN = 512
A_SLAB = 64


def _rope128(x, cos, sin_signed):
    lane = lax.broadcasted_iota(jnp.int32, x.shape, 1)
    first_half = (lane % HEAD_DIM) < (HEAD_DIM // 2)
    rot = jnp.where(first_half, pltpu.roll(x, LANES - HEAD_DIM // 2, 1), pltpu.roll(x, HEAD_DIM // 2, 1))
    return x * cos + rot * sin_signed


def _is_rope_col(col):
    return (P_Q <= col < P_Q + ATT_WIDTH) or (P_AK <= col < P_AK + KV_WIDTH)


O_R = ATT_COLS
O_WD = O_R + RW_WIDTH
O_K = O_WD + DECAY_LORA
O_V = O_K + RW_WIDTH
O_AD = O_V + RW_WIDTH
O_GD = O_AD + AAA_LORA
IN_COLS = O_GD + GATE_LORA


def _in_proj_kernel(x_ref, g_ref, b_ref, w_ref, cos_ref, sin_ref, p_ref, h_ref, tail_ref, hb_ref):
    h = _layer_norm(x_ref[...], g_ref[...], b_ref[...])
    h_ref[...] = h
    hb_ref[...] = h.astype(BF16)
    nseq = tail_ref.shape[0]
    t_seq = x_ref.shape[0] // nseq

    def product(c0, width):
        acc = jnp.dot(hb_ref[...], w_ref[:, c0:c0 + width], preferred_element_type=F32)
        if c0 >= O_R:
            for s in range(nseq):
                tail_ref[s, :, c0 - O_R:c0 - O_R + width] = acc[(s + 1) * t_seq - SUBLANES:(s + 1) * t_seq, :]
        return acc

    for src, width, dst in ((0, ATT_WIDTH, P_Q), (ATT_WIDTH, 2 * KV_WIDTH, P_AK), (O_R, RW_WIDTH, P_R)):
        for c0 in range(0, width, A_TN):
            acc = product(src + c0, min(A_TN, width - c0))
            for l0 in range(0, acc.shape[1], LANES):
                col = dst + c0 + l0
                part = acc[:, l0:l0 + LANES]
                if _is_rope_col(col):
                    part = _rope128(part, cos_ref[...], sin_ref[...])
                p_ref[:, col:col + LANES] = part

    low = lax.broadcasted_iota(jnp.int32, (x_ref.shape[0], LANES), 1) < HEAD_DIM
    span = O_GD - O_WD
    rolled, first_piece = [], None
    for c0 in range(0, span, A_TN):
        acc = product(O_WD + c0, min(A_TN, span - c0))
        for l0 in range(0, acc.shape[1], LANES):
            piece = acc[:, l0:l0 + LANES]
            j = (c0 + l0) // LANES
            if j == 0:
                first_piece = piece
            rolled.append(pltpu.roll(piece, HEAD_DIM, 1))
            if j >= 1:
                m = j - 1
                dst = P_K + m * LANES if m < RW_WIDTH // LANES else P_V + (m - RW_WIDTH // LANES) * LANES
                p_ref[:, dst:dst + LANES] = jnp.where(low, rolled[m], rolled[m + 1])
            if j == span // LANES - 1:
                p_ref[:, P_LORA:P_LORA + LANES] = jnp.where(low, first_piece, piece)
    gd0 = P_LORA + DECAY_LORA + AAA_LORA
    p_ref[:, gd0:gd0 + GATE_LORA] = product(O_GD, GATE_LORA)
    p_ref[:, gd0 + GATE_LORA:P_LORA + LORA_W] = jnp.zeros((x_ref.shape[0], LORA_W - DECAY_LORA - AAA_LORA - GATE_LORA), F32)


def _in_proj(xp, xs, g, b, w_p, tables_p, tables_s, tm, seq_p, seq_s):
    n_p, n_s = xp.shape[0], xs.shape[0]
    assert n_s == tm
    nrb = n_p // tm
    nseq_s = tm // seq_s
    tb = seq_p // tm
    last = lambda i: jnp.minimum(i, nrb - 1)

    def body(ins, shared, outs, scratch):
        x_ref, cos_ref, sin_ref = ins
        g_ref, b_ref, w_ref = shared
        _in_proj_kernel(x_ref, g_ref, b_ref, w_ref, cos_ref, sin_ref, *outs, *scratch)

    n_slab = D_MODEL // A_SLAB

    def kern(xp_ref, cp_ref, sp_ref, xs_ref, cs_ref, ss_ref, g_ref, b_ref, w_hbm,
             pp_ref, hp_ref, tp_ref, ps_ref, hs_ref, ts_ref, hb_ref, w_scr, stage, sem):
        i = pl.program_id(0)

        def fetch(k):
            return pltpu.make_async_copy(w_hbm.at[pl.ds(k * A_SLAB, A_SLAB), :], stage.at[k % 2], sem.at[k % 2])

        @pl.when(i == 0)
        def _():
            fetch(0).start()
            for k in range(n_slab):
                if k + 1 < n_slab:
                    fetch(k + 1).start()
                fetch(k).wait()
                w_scr[k * A_SLAB:(k + 1) * A_SLAB, :] = stage[k % 2].astype(BF16)

        pl.when(i < nrb)(lambda: body((xp_ref, cp_ref, sp_ref), (g_ref, b_ref, w_scr), (pp_ref, hp_ref, tp_ref), (hb_ref,)))
        pl.when(i == nrb)(lambda: body((xs_ref, cs_ref, ss_ref), (g_ref, b_ref, w_scr), (ps_ref, hs_ref, ts_ref), (hb_ref,)))

    const2 = lambda shape: pl.BlockSpec(shape, lambda i: (0, 0))
    return pl.pallas_call(
        kern,
        grid=(nrb + 1,),
        in_specs=[
            pl.BlockSpec((tm, D_MODEL), lambda i: (last(i), 0)),
            pl.BlockSpec((tm, LANES), lambda i: (last(i) % tb, 0)),
            pl.BlockSpec((tm, LANES), lambda i: (last(i) % tb, 0)),
            const2((tm, D_MODEL)), const2((tm, LANES)), const2((tm, LANES)),
            const2((1, D_MODEL)), const2((1, D_MODEL)),
            pl.BlockSpec(memory_space=pl.ANY),
        ],
        out_specs=[
            pl.BlockSpec((tm, P_COLS), lambda i: (last(i), 0)),
            pl.BlockSpec((tm, D_MODEL), lambda i: (last(i), 0)),
            pl.BlockSpec((1, SUBLANES, RW_COLS), lambda i: (last(i), 0, 0)),
            const2((tm, P_COLS)), const2((tm, D_MODEL)),
            pl.BlockSpec((nseq_s, SUBLANES, RW_COLS), lambda i: (0, 0, 0)),
        ],
        out_shape=[
            jax.ShapeDtypeStruct((n_p, P_COLS), F32),
            jax.ShapeDtypeStruct((n_p, D_MODEL), F32),
            jax.ShapeDtypeStruct((nrb, SUBLANES, RW_COLS), F32),
            jax.ShapeDtypeStruct((n_s, P_COLS), F32),
            jax.ShapeDtypeStruct((n_s, D_MODEL), F32),
            jax.ShapeDtypeStruct((nseq_s, SUBLANES, RW_COLS), F32),
        ],
        scratch_shapes=[pltpu.VMEM((tm, D_MODEL), BF16), pltpu.VMEM((D_MODEL, IN_COLS), BF16),
                        pltpu.VMEM((2, A_SLAB, IN_COLS), F32), pltpu.SemaphoreType.DMA((2,))],
        compiler_params=_cparams(("arbitrary",)),
        name="in_proj",
    )(xp, *tables_p, xs, *tables_s, g, b, w_p)


def _attend(q, k, v, valid, sink_ref):
    r = q.shape[0]
    outs = []
    for kv in range(ATT_KV_HEADS):
        kh = k[:, kv * HEAD_DIM:(kv + 1) * HEAD_DIM].astype(BF16)
        vh = v[:, kv * HEAD_DIM:(kv + 1) * HEAD_DIM].astype(BF16)
        heads = [kv * ATT_GROUP + g for g in range(ATT_GROUP)]
        qs = jnp.concatenate([q[:, h * HEAD_DIM:(h + 1) * HEAD_DIM] for h in heads], axis=0)
        sink = jnp.concatenate([jnp.full((r, 1), sink_ref[h], F32) for h in heads], axis=0)
        s = lax.dot_general(qs.astype(BF16), kh, (((1,), (1,)), ((), ())), preferred_element_type=F32) * ATT_SCALE
        if valid is not None:
            s = jnp.where(valid, s, -jnp.inf)
        m = jnp.maximum(jnp.max(s, -1, keepdims=True), sink)
        p = jnp.exp(s - m)
        den = jnp.sum(p, -1, keepdims=True) + jnp.exp(sink - m)
        o = jnp.dot(p.astype(BF16), vh, preferred_element_type=F32) / den
        outs.extend(o[g * r:(g + 1) * r] for g in range(ATT_GROUP))
    return jnp.concatenate(outs, axis=1)


QB = 2 * CHUNK


def _attn_prompt_kernel(sink_ref, q_ref, kp_ref, kc_ref, vp_ref, vc_ref, o_ref):
    m = pl.program_id(1)
    nseq = q_ref.shape[0]
    span = 2 * QB
    lane = lax.broadcasted_iota(jnp.int32, (span, KV_WIDTH), 1)
    row_c = 2 + lax.broadcasted_iota(jnp.int32, (QB, span), 0) // CHUNK
    col_c = lax.broadcasted_iota(jnp.int32, (QB, span), 1) // CHUNK
    back = row_c - col_c
    valid = jnp.logical_and(jnp.logical_and(back >= 0, back <= 2), 2 * m - 2 + col_c >= 0)
    out_lane = lax.broadcasted_iota(jnp.int32, (QB, PAIR), 1)

    def dup(x, kv):
        sw = pltpu.roll(x, HEAD_DIM, 1)
        return jnp.where(lane < HEAD_DIM, x, sw) if kv == 0 else jnp.where(lane < HEAD_DIM, sw, x)

    k_st, v_st = [], []
    for b in range(nseq):
        k_all = jnp.concatenate([kp_ref[b], kc_ref[b]], axis=0)
        v_all = jnp.concatenate([vp_ref[b], vc_ref[b]], axis=0)
        k_st.append([_stack_heads(dup(k_all, kv), HEAD_DIM).astype(BF16) for kv in range(ATT_KV_HEADS)])
        v_st.append([_stack_heads(dup(v_all, kv), HEAD_DIM).astype(BF16) for kv in range(ATT_KV_HEADS)])
    pairs = [(b, j) for b in range(nseq) for j in range(ATT_HEADS // 2)]
    kv_of = lambda j: (2 * j) // ATT_GROUP
    s = [lax.dot_general(q_ref[b, :, j * PAIR:(j + 1) * PAIR].astype(BF16), k_st[b][kv_of(j)],
                         (((1,), (1,)), ((), ())), preferred_element_type=F32) * ATT_SCALE for b, j in pairs]
    probs, dens = [], []
    for idx, (b, j) in enumerate(pairs):
        halves, den = [], []
        for e in range(2):
            sink = sink_ref[2 * j + e]
            sh = jnp.where(valid, s[idx][:, e * span:(e + 1) * span], -jnp.inf)
            mx = jnp.maximum(jnp.max(sh, -1, keepdims=True), sink)
            ph = jnp.exp(sh - mx)
            halves.append(ph)
            den.append(jnp.sum(ph, -1, keepdims=True) + jnp.exp(sink - mx))
        probs.append(jnp.concatenate(halves, axis=1).astype(BF16))
        dens.append(jnp.where(out_lane < HEAD_DIM, den[0], den[1]))
    o = [jnp.dot(probs[idx], v_st[b][kv_of(j)], preferred_element_type=F32) / dens[idx]
         for idx, (b, j) in enumerate(pairs)]
    n_pairs = ATT_HEADS // 2
    for b in range(nseq):
        o_ref[b] = jnp.concatenate(o[b * n_pairs:(b + 1) * n_pairs], axis=1).astype(o_ref.dtype)


def _attn_prompt(p3, sinks, nseq):
    batch, seq = p3.shape[0], p3.shape[1]
    nb = seq // QB
    kcol, vcol = P_AK // KV_WIDTH, P_AV // KV_WIDTH
    prev = lambda col: pl.BlockSpec((nseq, QB, KV_WIDTH), lambda g, m: (g, jnp.maximum(m - 1, 0), col))
    cur = lambda col: pl.BlockSpec((nseq, QB, KV_WIDTH), lambda g, m: (g, m, col))
    return pl.pallas_call(
        _attn_prompt_kernel,
        grid=(batch // nseq, nb),
        in_specs=[
            pl.BlockSpec(memory_space=pltpu.SMEM),
            pl.BlockSpec((nseq, QB, ATT_WIDTH), lambda g, m: (g, m, P_Q // ATT_WIDTH)),
            prev(kcol), cur(kcol), prev(vcol), cur(vcol),
        ],
        out_specs=pl.BlockSpec((nseq, QB, ATT_WIDTH), lambda g, m: (g, m, 0)),
        out_shape=jax.ShapeDtypeStruct((batch, seq, ATT_WIDTH), BF16),
        compiler_params=_cparams(("arbitrary", "arbitrary")),
        name="attn_prompt",
    )(sinks, p3, p3, p3, p3, p3)


def _attn_sample_kernel(sink_ref, q_ref, kn_ref, vn_ref, kc_ref, vc_ref, o_ref, kw_ref, vw_ref):
    t = q_ref.shape[0]
    k = jnp.concatenate([kc_ref[...], kn_ref[...]], axis=0)
    v = jnp.concatenate([vc_ref[...], vn_ref[...]], axis=0)
    o_ref[...] = _attend(q_ref[...], k, v, None, sink_ref).astype(o_ref.dtype)
    kw_ref[...] = k[t:]
    vw_ref[...] = v[t:]


def _attn_sample(p, sinks, cache_k, cache_v, batch, seq):
    kcol, vcol = P_AK // KV_WIDTH, P_AV // KV_WIDTH
    return pl.pallas_call(
        _attn_sample_kernel,
        grid=(batch,),
        in_specs=[
            pl.BlockSpec(memory_space=pltpu.SMEM),
            pl.BlockSpec((seq, ATT_WIDTH), lambda b: (b, P_Q // ATT_WIDTH)),
            pl.BlockSpec((seq, KV_WIDTH), lambda b: (b, kcol)),
            pl.BlockSpec((seq, KV_WIDTH), lambda b: (b, vcol)),
            pl.BlockSpec((WINDOW, KV_WIDTH), lambda b: (b, 0)),
            pl.BlockSpec((WINDOW, KV_WIDTH), lambda b: (b, 0)),
        ],
        out_specs=[
            pl.BlockSpec((seq, ATT_WIDTH), lambda b: (b, 0)),
            pl.BlockSpec((WINDOW, KV_WIDTH), lambda b: (b, 0)),
            pl.BlockSpec((WINDOW, KV_WIDTH), lambda b: (b, 0)),
        ],
        out_shape=[
            jax.ShapeDtypeStruct((batch * seq, ATT_WIDTH), BF16),
            jax.ShapeDtypeStruct((batch * WINDOW, KV_WIDTH), F32),
            jax.ShapeDtypeStruct((batch * WINDOW, KV_WIDTH), F32),
        ],
        compiler_params=_cparams(("arbitrary",)),
        name="attn_sample",
    )(sinks, p, p, p, cache_k, cache_v)


SEG = 256
EXP_NEG_HALF = math.exp(-0.5)
PAIR = 2 * HEAD_DIM
N_PAIRS = RW_HEADS // 2


def _split2(x):
    hi = x.astype(BF16)
    lo = (x - hi.astype(F32)).astype(BF16)
    return hi, lo


def _seg_sum(x, bd):
    c = x.shape[0]
    xb = x.astype(BF16)
    stack = jnp.concatenate([xb[:, s:s + SEG] for s in range(0, RW_WIDTH, SEG)], axis=0)
    r = jnp.dot(stack, bd, preferred_element_type=F32)
    return jnp.concatenate([r[i * c:(i + 1) * c] for i in range(RW_WIDTH // SEG)], axis=1)


def _shift_rows(x, prev_row):
    row = lax.broadcasted_iota(jnp.int32, x.shape, 0)
    return jnp.where(row == 0, prev_row, pltpu.roll(x, 1, 0))


def _dotb(a, b):
    return jnp.dot(a.astype(BF16), b.astype(BF16), preferred_element_type=F32)


def _dotb_nt(a, b):
    return lax.dot_general(a.astype(BF16), b.astype(BF16), (((1,), (1,)), ((), ())), preferred_element_type=F32)


def _dot3_tn(a, b):
    ah, al = _split2(a)
    bh, bl = _split2(b)
    return lax.dot_general(jnp.concatenate([ah, al, ah], axis=0), jnp.concatenate([bh, bh, bl], axis=0),
                           (((0,), (0,)), ((), ())), preferred_element_type=F32)


def _dot2_tn(a, b):
    ah, al = _split2(a)
    bh = b.astype(BF16)
    return lax.dot_general(jnp.concatenate([ah, al], axis=0), jnp.concatenate([bh, bh], axis=0),
                           (((0,), (0,)), ((), ())), preferred_element_type=F32)


def _dot2(a, b):
    ah, al = _split2(a)
    bh = b.astype(BF16)
    return jnp.dot(jnp.concatenate([ah, al], axis=1), jnp.concatenate([bh, bh], axis=0), preferred_element_type=F32)


def _stack_heads(x, half):
    lane = lax.broadcasted_iota(jnp.int32, x.shape, 1)
    first = (lane % (2 * half)) < half
    zero = jnp.zeros_like(x)
    return jnp.concatenate([jnp.where(first, x, zero), jnp.where(first, zero, x)], axis=0)


def _stack_heads_bf16(x, half):
    return _stack_heads(x.astype(BF16), half)


def _rwkv_pairs(ab, rb, bb, kb, v, gam_c, s_bd, n_double):
    c = ab[0].shape[0]
    pairs = range(len(ab))
    t_idx = lax.broadcasted_iota(jnp.int32, (c, 2 * c), 0)
    s_idx = lax.broadcasted_iota(jnp.int32, (c, 2 * c), 1) % c
    strict, incl = s_idx < t_idx, s_idx <= t_idx
    g = [_dotb_nt(jnp.concatenate([ab[p], rb[p]], axis=0),
                  jnp.concatenate([_stack_heads_bf16(bb[p], HEAD_DIM), _stack_heads_bf16(kb[p], HEAD_DIM)], axis=0))
         for p in pairs]
    l_ak = [jnp.where(strict, g[p][:c, 2 * c:], 0.0) for p in pairs]
    m_rbk = [jnp.concatenate([jnp.where(incl, g[p][c:, :2 * c], 0.0), jnp.where(incl, g[p][c:, 2 * c:], 0.0)], axis=1)
             for p in pairs]
    lp = [jnp.where(strict, g[p][:c, :2 * c], 0.0) for p in pairs]
    v_st = [_stack_heads_bf16(v[p], HEAD_DIM) for p in pairs]
    ya = list(ab)
    yu = [_dotb(l_ak[p], v_st[p]) for p in pairs]
    for i in range(n_double):
        y_st = [_stack_heads_bf16(jnp.concatenate([ya[p], yu[p]], axis=1), HEAD_DIM) for p in pairs]
        if i + 1 < n_double:
            z = [_dotb(lp[p], jnp.concatenate([y_st[p], _stack_heads_bf16(lp[p], c)], axis=1)) for p in pairs]
            lp = [z[p][:, 2 * PAIR:] for p in pairs]
        else:
            z = [_dotb(lp[p], y_st[p]) for p in pairs]
        ya = [ya[p] + z[p][:, :PAIR] for p in pairs]
        yu = [yu[p] + z[p][:, PAIR:2 * PAIR] for p in pairs]
    z = []
    for p in pairs:
        top = _stack_heads_bf16(jnp.concatenate([ya[p], yu[p]], axis=1), HEAD_DIM)
        bot = jnp.concatenate([jnp.zeros_like(v_st[p]), v_st[p]], axis=1)
        z.append(_dotb(m_rbk[p], jnp.concatenate([top, bot], axis=0)))
    o = [_dotb_nt(rb[p] + z[p][:, :PAIR], s_bd[p]) + z[p][:, PAIR:] for p in pairs]
    row_h = lax.broadcasted_iota(jnp.int32, (PAIR, PAIR), 0) // HEAD_DIM
    col_h = lax.broadcasted_iota(jnp.int32, (PAIR, PAIR), 1) // HEAD_DIM
    same = row_h == col_h
    q_t = [jnp.where(same, _dot3_tn(ya[p], bb[p]), 0.0) for p in pairs]
    d_t = [jnp.where(same, _dot2_tn(jnp.concatenate([yu[p], v[p]], axis=0),
                                    jnp.concatenate([bb[p], kb[p]], axis=0)), 0.0) for p in pairs]
    s_new = [(s_bd[p] + _dot2(s_bd[p], q_t[p]) + d_t[p]) * gam_c[p] for p in pairs]
    return o, s_new


def _split_rw_rows(x):
    o_wd, o_gd = RW_WIDTH, RW_COLS - GATE_LORA
    low = lax.broadcasted_iota(jnp.int32, (x.shape[0], LANES), 1) < HEAD_DIM
    pieces = [x[:, c0:c0 + LANES] for c0 in range(o_wd, o_gd, LANES)]
    rolled = [pltpu.roll(p, HEAD_DIM, 1) for p in pieces]
    merged = [jnp.where(low, rolled[m], rolled[m + 1]) for m in range(len(pieces) - 1)]
    n = RW_WIDTH // LANES
    return (x[:, :RW_WIDTH], jnp.concatenate(merged[:n], axis=1), jnp.concatenate(merged[n:], axis=1),
            jnp.where(low, pieces[0], pieces[-1]), x[:, o_gd:])


def _rwkv_kernel(xr_ref, xk_ref, xv_ref, xl_ref, sh_ref, mu_ref, w0_ref, a0_ref, kk_ref, ka_ref, rk_ref,
                 lg_ref, lb_ref, w2_ref, a2_ref, g2_ref, bd_ref, st0_ref,
                 out_ref, st_ref,
                 s_scr, pr_scr, pk_scr, pv_scr, pw_scr, pg_scr):
    c = pl.program_id(1)
    nc = pl.num_programs(1)
    nb, C = xr_ref.shape[0], xr_ref.shape[1]
    n_double = int(math.log2(C))
    prev_scr = (pr_scr, pk_scr, pv_scr, pw_scr, pg_scr)

    @pl.when(c == 0)
    def _():
        s_scr[...] = jnp.zeros_like(s_scr)
        for b in range(nb):
            for h in range(RW_HEADS):
                off = (h % 2) * HEAD_DIM
                s_scr[b * N_PAIRS + h // 2, off:off + HEAD_DIM, off:off + HEAD_DIM] = st0_ref[b, h]
            for scr, row in zip(prev_scr, _split_rw_rows(sh_ref[b])):
                scr[b, 0:1, :] = row

    def shifted(load, p_scr, mu):
        parts = []
        for b in range(nb):
            x = load(b)
            prev = _shift_rows(x, p_scr[b, 0:1, :])
            p_scr[b, 0:1, :] = x[C - 1:C, :]
            parts.append(x + (prev - x) * mu)
        return jnp.concatenate(parts, axis=0)

    mu_r, mu_k, mu_v, mu_w, mu_g = _split_rw_rows(mu_ref[...])
    wa_w = DECAY_LORA + AAA_LORA
    r = shifted(lambda b: xr_ref[b], pr_scr, mu_r)
    kr = shifted(lambda b: xk_ref[b], pk_scr, mu_k)
    v = shifted(lambda b: xv_ref[b], pv_scr, mu_v)
    xw = shifted(lambda b: xl_ref[b, :, :wa_w], pw_scr, mu_w)
    gd = shifted(lambda b: xl_ref[b, :, wa_w:wa_w + GATE_LORA], pg_scr, mu_g)
    wd = xw[:, 0:DECAY_LORA]
    ad = xw[:, DECAY_LORA:wa_w]

    lw = -EXP_NEG_HALF * jax.nn.sigmoid(w0_ref[...] + _dotb(jnp.tanh(wd), w2_ref[...]))
    a = jax.nn.sigmoid(a0_ref[...] + _dotb(ad, a2_ref[...]))
    g = _dotb(jax.nn.sigmoid(gd), g2_ref[...])
    bd = bd_ref[...]
    kk = kr * kk_ref[...]
    kk = kk * lax.rsqrt(jnp.maximum(_seg_sum(kk * kk, bd), 1e-24))
    k2 = kr * (1.0 + (a - 1.0) * ka_ref[...])

    rows = nb * C
    ti = lax.broadcasted_iota(jnp.int32, (rows, rows), 0)
    si = lax.broadcasted_iota(jnp.int32, (rows, rows), 1)
    tri = jnp.logical_and(si <= ti, si // C == ti // C).astype(BF16)
    cs2 = jnp.dot(tri, jnp.concatenate(_split2(lw), axis=1), preferred_element_type=F32)
    cs = cs2[:, :RW_WIDTH] + cs2[:, RW_WIDTH:]
    gam = jnp.exp(cs)
    ginv = jnp.exp(-cs)
    ab = -kk * jnp.exp(cs - lw)
    rb = r * gam
    bb = kk * a * ginv
    kb = k2 * ginv

    def per_pair(x, r0, r1):
        return [x[b * C + r0:b * C + r1, p * PAIR:(p + 1) * PAIR] for b in range(nb) for p in range(N_PAIRS)]

    chains = nb * N_PAIRS
    outs, s_new = _rwkv_pairs(per_pair(ab, 0, C), per_pair(rb, 0, C), per_pair(bb, 0, C), per_pair(kb, 0, C),
                              per_pair(v, 0, C), per_pair(gam, C - 1, C), [s_scr[q] for q in range(chains)], n_double)
    for q in range(chains):
        s_scr[q] = s_new[q]
    o = jnp.concatenate([jnp.concatenate(outs[b * N_PAIRS:(b + 1) * N_PAIRS], axis=1) for b in range(nb)], axis=0)

    mo = _seg_sum(o, bd) * (1.0 / HEAD_DIM)
    d = o - mo
    vo = _seg_sum(d * d, bd) * (1.0 / HEAD_DIM)
    on = d * lax.rsqrt(vo + LNX_EPS) * lg_ref[...] + lb_ref[...]
    bonus = _seg_sum(r * k2 * rk_ref[...], bd) * v
    res = ((on + bonus) * g).astype(out_ref.dtype)
    for b in range(nb):
        out_ref[b] = res[b * C:(b + 1) * C, :]

    @pl.when(c == nc - 1)
    def _():
        for b in range(nb):
            for h in range(RW_HEADS):
                off = (h % 2) * HEAD_DIM
                st_ref[b, h] = s_scr[b * N_PAIRS + h // 2, off:off + HEAD_DIM, off:off + HEAD_DIM]


def _rwkv(p3, state_shift, state0, prm, chunk, nb):
    batch, seq = p3.shape[0], p3.shape[1]
    nc = seq // chunk
    row = lambda w: pl.BlockSpec((1, w), lambda g, c: (0, 0))
    xspec = lambda w, col: pl.BlockSpec((nb, chunk, w), lambda g, c: (g, c, col))
    full = lambda a: pl.BlockSpec(a.shape, lambda g, c: (0,) * a.ndim)
    stspec = pl.BlockSpec((nb, RW_HEADS, HEAD_DIM, HEAD_DIM), lambda g, c: (g, 0, 0, 0))
    return pl.pallas_call(
        _rwkv_kernel,
        grid=(batch // nb, nc),
        in_specs=[
            xspec(RW_WIDTH, P_R // RW_WIDTH), xspec(RW_WIDTH, P_K // RW_WIDTH), xspec(RW_WIDTH, P_V // RW_WIDTH),
            xspec(LORA_W, P_LORA // LORA_W),
            pl.BlockSpec((nb, 1, RW_COLS), lambda g, c: (g, 0, 0)),
            row(RW_COLS),
            row(RW_WIDTH), row(RW_WIDTH), row(RW_WIDTH), row(RW_WIDTH), row(RW_WIDTH),
            row(RW_WIDTH), row(RW_WIDTH),
            full(prm["w2"]), full(prm["a2"]), full(prm["g2"]), full(prm["bd"]),
            stspec,
        ],
        out_specs=[
            pl.BlockSpec((nb, chunk, RW_WIDTH), lambda g, c: (g, c, 0)),
            stspec,
        ],
        out_shape=[
            jax.ShapeDtypeStruct((batch, seq, RW_WIDTH), BF16),
            jax.ShapeDtypeStruct((batch, RW_HEADS, HEAD_DIM, HEAD_DIM), F32),
        ],
        scratch_shapes=[
            pltpu.VMEM((nb * N_PAIRS, PAIR, PAIR), F32),
            pltpu.VMEM((nb, SUBLANES, RW_WIDTH), F32), pltpu.VMEM((nb, SUBLANES, RW_WIDTH), F32),
            pltpu.VMEM((nb, SUBLANES, RW_WIDTH), F32),
            pltpu.VMEM((nb, SUBLANES, DECAY_LORA + AAA_LORA), F32), pltpu.VMEM((nb, SUBLANES, GATE_LORA), F32),
        ],
        compiler_params=_cparams(("arbitrary", "arbitrary")),
        name="rwkv",
    )(p3, p3, p3, p3, state_shift, prm["mu"], prm["w0"], prm["a0"], prm["k_k"], prm["k_a"], prm["r_k"],
      prm["lnx_g"], prm["lnx_b"], prm["w2"], prm["a2"], prm["g2"], prm["bd"], state0)


D_TN = 512


def _out_proj_body(att_ref, rw_ref, h_ref, g_ref, b_ref, h1_ref, h1b_ref, w_scr):
    for c0 in range(0, D_MODEL, D_TN):
        m = jnp.dot(att_ref[...], w_scr[:ATT_WIDTH, c0:c0 + D_TN], preferred_element_type=F32)
        m = m + jnp.dot(rw_ref[...], w_scr[ATT_WIDTH:, c0:c0 + D_TN], preferred_element_type=F32)
        h1_ref[:, c0:c0 + D_TN] = ALPHA * h_ref[:, c0:c0 + D_TN] + m
    y = _layer_norm(h1_ref[...], g_ref[...], b_ref[...])
    h1_ref[...] = y
    h1b_ref[...] = y.astype(BF16)


def _out_proj(att_p, rw_p, h_p, att_s, rw_s, h_s, w_out, g, b, tm):
    n_p, n_s = att_p.shape[0], att_s.shape[0]
    assert n_s == tm
    nrb = n_p // tm
    last = lambda i: jnp.minimum(i, nrb - 1)

    def kern(ap_ref, rp_ref, hp_ref, as_ref, rs_ref, hs_ref, wa_ref, wb_ref, g_ref, b_ref,
             h1p_ref, h1bp_ref, h1s_ref, h1bs_ref, w_scr):
        i = pl.program_id(0)

        @pl.when(i == 0)
        def _():
            w_scr[:ATT_WIDTH, :] = wa_ref[...].astype(BF16)
            w_scr[ATT_WIDTH:, :] = wb_ref[...].astype(BF16)

        pl.when(i < nrb)(lambda: _out_proj_body(ap_ref, rp_ref, hp_ref, g_ref, b_ref, h1p_ref, h1bp_ref, w_scr))
        pl.when(i == nrb)(lambda: _out_proj_body(as_ref, rs_ref, hs_ref, g_ref, b_ref, h1s_ref, h1bs_ref, w_scr))

    rows_p = lambda w: pl.BlockSpec((tm, w), lambda i: (last(i), 0))
    rows_s = lambda w: pl.BlockSpec((tm, w), lambda i: (0, 0))
    return pl.pallas_call(
        kern,
        grid=(nrb + 1,),
        in_specs=[
            rows_p(ATT_WIDTH), rows_p(RW_WIDTH), rows_p(D_MODEL),
            rows_s(ATT_WIDTH), rows_s(RW_WIDTH), rows_s(D_MODEL),
            pl.BlockSpec((ATT_WIDTH, D_MODEL), lambda i: (0, 0), pipeline_mode=pl.Buffered(1)),
            pl.BlockSpec((RW_WIDTH, D_MODEL), lambda i: (1, 0), pipeline_mode=pl.Buffered(1)),
            pl.BlockSpec((1, D_MODEL), lambda i: (0, 0)),
            pl.BlockSpec((1, D_MODEL), lambda i: (0, 0)),
        ],
        out_specs=[rows_p(D_MODEL), rows_p(D_MODEL), rows_s(D_MODEL), rows_s(D_MODEL)],
        out_shape=[
            jax.ShapeDtypeStruct((n_p, D_MODEL), F32), jax.ShapeDtypeStruct((n_p, D_MODEL), BF16),
            jax.ShapeDtypeStruct((n_s, D_MODEL), F32), jax.ShapeDtypeStruct((n_s, D_MODEL), BF16),
        ],
        scratch_shapes=[pltpu.VMEM((ATT_WIDTH + RW_WIDTH, D_MODEL), BF16)],
        compiler_params=_cparams(("arbitrary",)),
        name="out_proj",
    )(att_p, rw_p, h_p, att_s, rw_s, h_s, w_out, w_out, g, b)


E_TN = 512
E_NCB = D_FF // E_TN
E_SUB = 512
SQRT_HALF = math.sqrt(0.5)


def _ffn_up_body(bps, nseq, i, h_ref, w_scr, cwg_ref, cwv_ref, cbg_ref, cbv_ref, sg_ref, sv_ref,
                 act_ref, og_ref, ov_ref, cg_scr, cv_scr):
    tm = h_ref.shape[0]
    sub = min(tm, E_SUB)
    n_sub = tm // sub
    t_seq = tm // nseq
    row8 = lax.broadcasted_iota(jnp.int32, (SUBLANES, E_TN), 0)

    def products(r0):
        u = jnp.dot(h_ref[r0:r0 + sub, :], w_scr[...], preferred_element_type=F32)
        return u[:, :E_TN], u[:, E_TN:]

    def conv(u, prevs, cw_ref, cb_ref):
        sh1, sh2 = pltpu.roll(u, 1, 0), pltpu.roll(u, 2, 0)
        for r, prev in prevs:
            p2, p1 = prev[0:1, :], prev[1:2, :]
            fix1 = jnp.where(row8 == 0, p1, sh1[r:r + SUBLANES, :])
            fix2 = jnp.where(row8 == 0, p2, jnp.where(row8 == 1, p1, sh2[r:r + SUBLANES, :]))
            head = lambda x: [x[:r, :]] if r > 0 else []
            tail = lambda x: [x[r + SUBLANES:, :]] if r + SUBLANES < x.shape[0] else []
            sh1 = jnp.concatenate(head(sh1) + [fix1] + tail(sh1), axis=0)
            sh2 = jnp.concatenate(head(sh2) + [fix2] + tail(sh2), axis=0)
        cw = cw_ref[...]
        c = cb_ref[...] + sh2 * cw[0:1, :]
        c = c + sh1 * cw[1:2, :]
        return c + u * cw[2:3, :]

    def epilogue(r0, ug, uv, prev_g, prev_v):
        gate = conv(ug, prev_g, cwg_ref, cbg_ref)
        val = conv(uv, prev_v, cwv_ref, cbv_ref)
        gelu = 0.5 * gate * (1.0 + lax.erf(gate * SQRT_HALF))
        act_ref[r0:r0 + sub, :] = (gelu * val).astype(act_ref.dtype)

    if nseq > 1:
        ug, uv = products(0)
        epilogue(0, ug, uv, [(s * t_seq, sg_ref[s]) for s in range(nseq)],
                 [(s * t_seq, sv_ref[s]) for s in range(nseq)])
        for s in range(nseq):
            og_ref[s] = ug[(s + 1) * t_seq - SUBLANES:(s + 1) * t_seq, :]
            ov_ref[s] = uv[(s + 1) * t_seq - SUBLANES:(s + 1) * t_seq, :]
    else:
        first = (i % bps) == 0
        prev_g = jnp.where(first, sg_ref[0], cg_scr[SUBLANES - 2:SUBLANES, :])
        prev_v = jnp.where(first, sv_ref[0], cv_scr[SUBLANES - 2:SUBLANES, :])
        nxt = products(0)
        for k in range(n_sub):
            ug, uv = nxt
            if k + 1 < n_sub:
                nxt = products((k + 1) * sub)
            epilogue(k * sub, ug, uv, [(0, prev_g)], [(0, prev_v)])
            prev_g, prev_v = ug[sub - 2:sub, :], uv[sub - 2:sub, :]
        cg_scr[...] = ug[sub - SUBLANES:sub, :]
        cv_scr[...] = uv[sub - SUBLANES:sub, :]
        og_ref[0] = ug[sub - SUBLANES:sub, :]
        ov_ref[0] = uv[sub - SUBLANES:sub, :]


def _ffn_up(h_p, conv_prev_p, seq_p, tm_p, h_s, conv_prev_s, seq_s, w_up, conv_w, conv_b, w_down):
    n_p, n_s = h_p.shape[0], h_s.shape[0]
    batch_p, batch_s = n_p // seq_p, n_s // seq_s
    bps = seq_p // tm_p
    nrb = n_p // tm_p
    slab = D_FF // (E_NCB * nrb)
    last = lambda i: jnp.maximum(i - 1, 0)

    def kern(hp_ref, sgp_ref, svp_ref, hs_ref, sgs_ref, svs_ref, wg_ref, wv_ref, cwg_ref, cwv_ref, cbg_ref, cbv_ref,
             wd_ref, actp_ref, ogp_ref, ovp_ref, acts_ref, ogs_ref, ovs_ref, wdb_ref, w_scr, cg_scr, cv_scr):
        i = pl.program_id(1)

        @pl.when(i == 0)
        def _():
            w_scr[:, :E_TN] = wg_ref[...].astype(BF16)
            w_scr[:, E_TN:] = wv_ref[...].astype(BF16)
            _ffn_up_body(1, batch_s, i, hs_ref, w_scr, cwg_ref, cwv_ref, cbg_ref, cbv_ref, sgs_ref, svs_ref,
                         acts_ref, ogs_ref, ovs_ref, cg_scr, cv_scr)

        @pl.when(i > 0)
        def _():
            wdb_ref[...] = wd_ref[...].astype(BF16)
            _ffn_up_body(bps, 1, i - 1, hp_ref, w_scr, cwg_ref, cwv_ref, cbg_ref, cbv_ref, sgp_ref, svp_ref,
                         actp_ref, ogp_ref, ovp_ref, cg_scr, cv_scr)

    col = lambda shape, off: pl.BlockSpec(shape, lambda j, i: (0,) * (len(shape) - 1) + (j + off,))
    st_p = lambda off: pl.BlockSpec((1, CONV_W - 1, E_TN), lambda j, i: (last(i) // bps, 0, j + off))
    st_s = lambda off: pl.BlockSpec((batch_s, CONV_W - 1, E_TN), lambda j, i: (0, 0, j + off))
    tail_p = pl.BlockSpec((1, SUBLANES, E_TN), lambda j, i: (last(i) // bps, 0, j))
    tail_s = pl.BlockSpec((batch_s, SUBLANES, E_TN), lambda j, i: (0, 0, j))
    wd_spec = pl.BlockSpec((slab, D_MODEL), lambda j, i: (j * nrb + last(i), 0))
    return pl.pallas_call(
        kern,
        grid=(E_NCB, nrb + 1),
        in_specs=[
            pl.BlockSpec((tm_p, D_MODEL), lambda j, i: (last(i), 0)), st_p(0), st_p(E_NCB),
            pl.BlockSpec((n_s, D_MODEL), lambda j, i: (0, 0)), st_s(0), st_s(E_NCB),
            col((D_MODEL, E_TN), 0), col((D_MODEL, E_TN), E_NCB),
            col((CONV_W, E_TN), 0), col((CONV_W, E_TN), E_NCB),
            col((1, E_TN), 0), col((1, E_TN), E_NCB),
            wd_spec,
        ],
        out_specs=[
            pl.BlockSpec((tm_p, E_TN), lambda j, i: (last(i), j)), tail_p, tail_p,
            pl.BlockSpec((n_s, E_TN), lambda j, i: (0, j)), tail_s, tail_s,
            wd_spec,
        ],
        out_shape=[
            jax.ShapeDtypeStruct((n_p, D_FF), BF16),
            jax.ShapeDtypeStruct((batch_p, SUBLANES, D_FF), F32), jax.ShapeDtypeStruct((batch_p, SUBLANES, D_FF), F32),
            jax.ShapeDtypeStruct((n_s, D_FF), BF16),
            jax.ShapeDtypeStruct((batch_s, SUBLANES, D_FF), F32), jax.ShapeDtypeStruct((batch_s, SUBLANES, D_FF), F32),
            jax.ShapeDtypeStruct((D_FF, D_MODEL), BF16),
        ],
        scratch_shapes=[pltpu.VMEM((D_MODEL, 2 * E_TN), BF16),
                        pltpu.VMEM((SUBLANES, E_TN), F32), pltpu.VMEM((SUBLANES, E_TN), F32)],
        compiler_params=_cparams(("arbitrary", "arbitrary")),
        name="ffn_up",
    )(h_p, conv_prev_p, conv_prev_p, h_s, conv_prev_s, conv_prev_s, w_up, w_up, conv_w, conv_w, conv_b, conv_b, w_down)


F_TN = 512


def _ffn_down_kernel(act_ref, w_ref, h1_ref, g_ref, b_ref, y_ref):
    for c0 in range(0, D_MODEL, F_TN):
        f = jnp.dot(act_ref[...], w_ref[:, c0:c0 + F_TN], preferred_element_type=F32)
        y_ref[:, c0:c0 + F_TN] = ALPHA * h1_ref[:, c0:c0 + F_TN] + f
    y_ref[...] = _layer_norm(y_ref[...], g_ref[...], b_ref[...])


def _ffn_down(act_p, h1_p, act_s, h1_s, w_down, g, b, tm):
    n_p, n_s = act_p.shape[0], act_s.shape[0]
    assert n_s == tm
    nrb = n_p // tm
    last = lambda i: jnp.minimum(i, nrb - 1)

    def kern(ap_ref, hp_ref, as_ref, hs_ref, w_ref, g_ref, b_ref, yp_ref, ys_ref):
        i = pl.program_id(0)
        pl.when(i < nrb)(lambda: _ffn_down_kernel(ap_ref, w_ref, hp_ref, g_ref, b_ref, yp_ref))
        pl.when(i == nrb)(lambda: _ffn_down_kernel(as_ref, w_ref, hs_ref, g_ref, b_ref, ys_ref))

    rows_p = lambda w: pl.BlockSpec((tm, w), lambda i: (last(i), 0))
    rows_s = lambda w: pl.BlockSpec((tm, w), lambda i: (0, 0))
    return pl.pallas_call(
        kern,
        grid=(nrb + 1,),
        in_specs=[
            rows_p(D_FF), rows_p(D_MODEL), rows_s(D_FF), rows_s(D_MODEL),
            pl.BlockSpec((D_FF, D_MODEL), lambda i: (0, 0), pipeline_mode=pl.Buffered(1)),
            pl.BlockSpec((1, D_MODEL), lambda i: (0, 0)),
            pl.BlockSpec((1, D_MODEL), lambda i: (0, 0)),
        ],
        out_specs=[rows_p(D_MODEL), rows_s(D_MODEL)],
        out_shape=[jax.ShapeDtypeStruct((n_p, D_MODEL), F32), jax.ShapeDtypeStruct((n_s, D_MODEL), F32)],
        compiler_params=_cparams(("arbitrary",)),
        name="ffn_down",
    )(act_p, h1_p, act_s, h1_s, w_down, g, b)


def _rope_tables(pos, reps):
    half = HEAD_DIM // 2
    inv = ROPE_THETA ** (-jnp.arange(half, dtype=F32) / half)
    ang = pos.astype(F32)[:, None] * inv[None, :]
    cos, sin = jnp.cos(ang), jnp.sin(ang)
    cos128 = jnp.concatenate([cos, cos, cos, cos], axis=1)
    sin128 = jnp.concatenate([-sin, sin, -sin, sin], axis=1)
    return jnp.tile(cos128, (reps, 1)), jnp.tile(sin128, (reps, 1))


TM = 256
E_TM = 1024
RWKV_NB = 4
ATTN_NB = 2


def kernel(x_prompt, x_sample, cache_k, cache_v, state_wkv, state_shift, state_ffn_conv, ln_in_g, ln_in_b, w_in, attn_sinks, rw_mu, rw_w0, rw_w2, rw_a0, rw_a2, rw_g2, rw_k_k, rw_k_a, rw_r_k, rw_lnx_g, rw_lnx_b, w_out, ln1_g, ln1_b, ffn_w_up, ffn_conv_w, ffn_conv_b, ffn_w_down, ln2_g, ln2_b):
    l = 0
    seg = lax.broadcasted_iota(jnp.int32, (SEG, SEG), 0) // HEAD_DIM
    bd = (seg == seg.T).astype(BF16)
    row = lambda a: a.reshape(1, -1)
    rw_prm = {"mu": row(rw_mu[l]), "w0": row(rw_w0[l]), "a0": row(rw_a0[l]), "k_k": row(rw_k_k[l]),
              "k_a": row(rw_k_a[l]), "r_k": row(rw_r_k[l]), "lnx_g": row(rw_lnx_g[l]), "lnx_b": row(rw_lnx_b[l]),
              "w2": rw_w2[l].astype(BF16), "a2": rw_a2[l].astype(BF16), "g2": rw_g2[l].astype(BF16), "bd": bd}
    sinks = attn_sinks[l]
    bp, tp = x_prompt.shape[0], x_prompt.shape[1]
    bs, ts = x_sample.shape[0], x_sample.shape[1]
    n_p, n_s = bp * tp, bs * ts
    zeros = lambda *s: jnp.zeros(s, F32)

    p_p, h_p, tails_p, p_s, h_s, tails_s = _in_proj(
        x_prompt.reshape(n_p, D_MODEL), x_sample.reshape(n_s, D_MODEL), row(ln_in_g), row(ln_in_b),
        w_in[l], _rope_tables(jnp.arange(tp, dtype=jnp.int32), 1),
        _rope_tables(PAST_LEN + jnp.arange(ts, dtype=jnp.int32), bs), TM, tp, ts)
    p3_p, p3_s = p_p.reshape(bp, tp, P_COLS), p_s.reshape(bs, ts, P_COLS)

    att_p = _attn_prompt(p3_p, sinks, ATTN_NB).reshape(n_p, ATT_WIDTH)
    att_s, s_k, s_v = _attn_sample(p_s, sinks, cache_k[l].reshape(bs * WINDOW, KV_WIDTH),
                                   cache_v[l].reshape(bs * WINDOW, KV_WIDTH), bs, ts)
    rw_p, p_wkv = _rwkv(p3_p, zeros(bp, 1, RW_COLS), zeros(bp, RW_HEADS, HEAD_DIM, HEAD_DIM), rw_prm, CHUNK, RWKV_NB)
    rw_s, s_wkv = _rwkv(p3_s, state_shift[l], state_wkv[l], rw_prm, ts, RWKV_NB)

    h1_p, h1b_p, h1_s, h1b_s = _out_proj(att_p, rw_p.reshape(n_p, RW_WIDTH), h_p, att_s, rw_s.reshape(n_s, RW_WIDTH), h_s,
                                         w_out[l], row(ln1_g[l]), row(ln1_b[l]), TM)
    act_p, cg_p, cv_p, act_s, cg_s, cv_s, w_down_b = _ffn_up(
        h1b_p, zeros(bp, CONV_W - 1, 2 * D_FF), tp, E_TM, h1b_s, state_ffn_conv[l], ts,
        ffn_w_up[l], ffn_conv_w[l], row(ffn_conv_b[l]), ffn_w_down[l])
    y_p, y_s = _ffn_down(act_p, h1_p, act_s, h1_s, w_down_b, row(ln2_g[l]), row(ln2_b[l]), TM)

    kv_shape = lambda b: (1, b, WINDOW, ATT_KV_HEADS, HEAD_DIM)
    p_k = p3_p[:, tp - WINDOW:, P_AK:P_AK + KV_WIDTH].reshape(kv_shape(bp))
    p_v = p3_p[:, tp - WINDOW:, P_AV:P_AV + KV_WIDTH].reshape(kv_shape(bp))
    shift_of = lambda tails, b: tails.reshape(b, -1, RW_COLS)[:, -1:, :][None]
    conv_of = lambda cg, cv: jnp.concatenate([cg[:, SUBLANES - 2:, :], cv[:, SUBLANES - 2:, :]], axis=-1)[None]
    return (y_p.reshape(bp, tp, D_MODEL), y_s.reshape(bs, ts, D_MODEL), p_k, p_v, p_wkv[None],
            shift_of(tails_p, bp), conv_of(cg_p, cv_p),
            s_k.reshape(kv_shape(bs)), s_v.reshape(kv_shape(bs)), s_wkv[None],
            shift_of(tails_s, bs), conv_of(cg_s, cv_s))
```

```python
import functools
import math

import jax
import jax.numpy as jnp
from jax import lax
from jax.experimental import pallas as pl
from jax.experimental.pallas import tpu as pltpu

D_MODEL = 2048
HEAD_DIM = 64
ATT_HEADS = 16
ATT_KV_HEADS = 2
ATT_GROUP = ATT_HEADS // ATT_KV_HEADS
ATT_WIDTH = ATT_HEADS * HEAD_DIM
KV_WIDTH = ATT_KV_HEADS * HEAD_DIM
CHUNK = 64
WINDOW = 128
ROPE_THETA = 10000.0
ATT_SCALE = HEAD_DIM ** -0.5
RW_HEADS = 16
RW_WIDTH = RW_HEADS * HEAD_DIM
DECAY_LORA = 64
AAA_LORA = 64
GATE_LORA = 160
LNX_EPS = 64e-5
ATT_COLS = ATT_WIDTH + 2 * KV_WIDTH
RW_COLS = 3 * RW_WIDTH + DECAY_LORA + AAA_LORA + GATE_LORA
D_FF = 5632
CONV_W = 3
LN_EPS = 1e-5
DEPTH = 1
ALPHA = (2 * DEPTH) ** 0.25
PAST_LEN = 1024

LORA_W = 512
P_R, P_K, P_V, P_Q = 0, 1024, 2048, 3072
P_LORA = 4096
P_AK = P_LORA + LORA_W
P_AV = P_AK + KV_WIDTH
P_COLS = P_AV + KV_WIDTH
LANES = 128
SUBLANES = 8
VMEM_LIMIT = 56 * 1024 * 1024

F32 = jnp.float32
BF16 = jnp.bfloat16


def _cparams(sem):
    return pltpu.CompilerParams(dimension_semantics=sem, vmem_limit_bytes=VMEM_LIMIT)


def _layer_norm(y, g, b):
    mu = jnp.mean(y, -1, keepdims=True)
    d = y - mu
    var = jnp.mean(d * d, -1, keepdims=True)
    return d * lax.rsqrt(var + LN_EPS) * g + b


A_TN = 512


def _rope128(x, cos, sin_signed):
    lane = lax.broadcasted_iota(jnp.int32, x.shape, 1)
    first_half = (lane % HEAD_DIM) < (HEAD_DIM // 2)
    rot = jnp.where(first_half, pltpu.roll(x, LANES - HEAD_DIM // 2, 1), pltpu.roll(x, HEAD_DIM // 2, 1))
    return x * cos + rot * sin_signed


def _is_rope_col(col):
    return (P_Q <= col < P_Q + ATT_WIDTH) or (P_AK <= col < P_AK + KV_WIDTH)


O_R = ATT_COLS
O_WD = O_R + RW_WIDTH
O_K = O_WD + DECAY_LORA
O_V = O_K + RW_WIDTH
O_AD = O_V + RW_WIDTH
O_GD = O_AD + AAA_LORA
IN_COLS = O_GD + GATE_LORA


def _in_proj_kernel(x_ref, g_ref, b_ref, w_ref, cos_ref, sin_ref, p_ref, h_ref, tail_ref, hb_ref):
    h = _layer_norm(x_ref[...], g_ref[...], b_ref[...])
    h_ref[...] = h
    hb_ref[...] = h.astype(BF16)
    nseq = tail_ref.shape[0]
    t_seq = x_ref.shape[0] // nseq

    def product(c0, width):
        acc = jnp.dot(hb_ref[...], w_ref[:, c0:c0 + width], preferred_element_type=F32)
        if c0 >= O_R:
            for s in range(nseq):
                tail_ref[s, :, c0 - O_R:c0 - O_R + width] = acc[(s + 1) * t_seq - SUBLANES:(s + 1) * t_seq, :]
        return acc

    for src, width, dst in ((0, ATT_WIDTH, P_Q), (ATT_WIDTH, 2 * KV_WIDTH, P_AK), (O_R, RW_WIDTH, P_R)):
        for c0 in range(0, width, A_TN):
            acc = product(src + c0, min(A_TN, width - c0))
            for l0 in range(0, acc.shape[1], LANES):
                col = dst + c0 + l0
                part = acc[:, l0:l0 + LANES]
                if _is_rope_col(col):
                    part = _rope128(part, cos_ref[...], sin_ref[...])
                p_ref[:, col:col + LANES] = part

    low = lax.broadcasted_iota(jnp.int32, (x_ref.shape[0], LANES), 1) < HEAD_DIM
    span = O_GD - O_WD
    rolled, first_piece = [], None
    for c0 in range(0, span, A_TN):
        acc = product(O_WD + c0, min(A_TN, span - c0))
        for l0 in range(0, acc.shape[1], LANES):
            piece = acc[:, l0:l0 + LANES]
            j = (c0 + l0) // LANES
            if j == 0:
                first_piece = piece
            rolled.append(pltpu.roll(piece, HEAD_DIM, 1))
            if j >= 1:
                m = j - 1
                dst = P_K + m * LANES if m < RW_WIDTH // LANES else P_V + (m - RW_WIDTH // LANES) * LANES
                p_ref[:, dst:dst + LANES] = jnp.where(low, rolled[m], rolled[m + 1])
            if j == span // LANES - 1:
                p_ref[:, P_LORA:P_LORA + LANES] = jnp.where(low, first_piece, piece)
    gd0 = P_LORA + DECAY_LORA + AAA_LORA
    p_ref[:, gd0:gd0 + GATE_LORA] = product(O_GD, GATE_LORA)
    p_ref[:, gd0 + GATE_LORA:P_LORA + LORA_W] = jnp.zeros((x_ref.shape[0], LORA_W - DECAY_LORA - AAA_LORA - GATE_LORA), F32)


def _in_proj(xp, xs, g, b, w_p, tables_p, tables_s, tm, seq_p, seq_s):
    n_p, n_s = xp.shape[0], xs.shape[0]
    assert n_s == tm
    nrb = n_p // tm
    nseq_s = tm // seq_s
    tb = seq_p // tm
    last = lambda i: jnp.minimum(i, nrb - 1)

    def body(ins, shared, outs, scratch):
        x_ref, cos_ref, sin_ref = ins
        g_ref, b_ref, w_ref = shared
        _in_proj_kernel(x_ref, g_ref, b_ref, w_ref, cos_ref, sin_ref, *outs, *scratch)

    def kern(xp_ref, cp_ref, sp_ref, xs_ref, cs_ref, ss_ref, g_ref, b_ref, w_ref,
             pp_ref, hp_ref, tp_ref, ps_ref, hs_ref, ts_ref, hb_ref):
        i = pl.program_id(0)
        pl.when(i < nrb)(lambda: body((xp_ref, cp_ref, sp_ref), (g_ref, b_ref, w_ref), (pp_ref, hp_ref, tp_ref), (hb_ref,)))
        pl.when(i == nrb)(lambda: body((xs_ref, cs_ref, ss_ref), (g_ref, b_ref, w_ref), (ps_ref, hs_ref, ts_ref), (hb_ref,)))

    const2 = lambda shape: pl.BlockSpec(shape, lambda i: (0, 0))
    return pl.pallas_call(
        kern,
        grid=(nrb + 1,),
        in_specs=[
            pl.BlockSpec((tm, D_MODEL), lambda i: (last(i), 0)),
            pl.BlockSpec((tm, LANES), lambda i: (last(i) % tb, 0)),
            pl.BlockSpec((tm, LANES), lambda i: (last(i) % tb, 0)),
            const2((tm, D_MODEL)), const2((tm, LANES)), const2((tm, LANES)),
            const2((1, D_MODEL)), const2((1, D_MODEL)),
            pl.BlockSpec((D_MODEL, IN_COLS), lambda i: (0, 0), pipeline_mode=pl.Buffered(1)),
        ],
        out_specs=[
            pl.BlockSpec((tm, P_COLS), lambda i: (last(i), 0)),
            pl.BlockSpec((tm, D_MODEL), lambda i: (last(i), 0)),
            pl.BlockSpec((1, SUBLANES, RW_COLS), lambda i: (last(i), 0, 0)),
            const2((tm, P_COLS)), const2((tm, D_MODEL)),
            pl.BlockSpec((nseq_s, SUBLANES, RW_COLS), lambda i: (0, 0, 0)),
        ],
        out_shape=[
            jax.ShapeDtypeStruct((n_p, P_COLS), F32),
            jax.ShapeDtypeStruct((n_p, D_MODEL), F32),
            jax.ShapeDtypeStruct((nrb, SUBLANES, RW_COLS), F32),
            jax.ShapeDtypeStruct((n_s, P_COLS), F32),
            jax.ShapeDtypeStruct((n_s, D_MODEL), F32),
            jax.ShapeDtypeStruct((nseq_s, SUBLANES, RW_COLS), F32),
        ],
        scratch_shapes=[pltpu.VMEM((tm, D_MODEL), BF16)],
        compiler_params=_cparams(("arbitrary",)),
        name="in_proj",
    )(xp, *tables_p, xs, *tables_s, g, b, w_p)


def _attend(q, k, v, valid, sink_ref):
    r = q.shape[0]
    outs = []
    for kv in range(ATT_KV_HEADS):
        kh = k[:, kv * HEAD_DIM:(kv + 1) * HEAD_DIM].astype(BF16)
        vh = v[:, kv * HEAD_DIM:(kv + 1) * HEAD_DIM].astype(BF16)
        heads = [kv * ATT_GROUP + g for g in range(ATT_GROUP)]
        qs = jnp.concatenate([q[:, h * HEAD_DIM:(h + 1) * HEAD_DIM] for h in heads], axis=0)
        sink = jnp.concatenate([jnp.full((r, 1), sink_ref[h], F32) for h in heads], axis=0)
        s = lax.dot_general(qs.astype(BF16), kh, (((1,), (1,)), ((), ())), preferred_element_type=F32) * ATT_SCALE
        if valid is not None:
            s = jnp.where(valid, s, -jnp.inf)
        m = jnp.maximum(jnp.max(s, -1, keepdims=True), sink)
        p = jnp.exp(s - m)
        den = jnp.sum(p, -1, keepdims=True) + jnp.exp(sink - m)
        o = jnp.dot(p.astype(BF16), vh, preferred_element_type=F32) / den
        outs.extend(o[g * r:(g + 1) * r] for g in range(ATT_GROUP))
    return jnp.concatenate(outs, axis=1)


QB = 2 * CHUNK


def _attn_prompt_kernel(sink_ref, q_ref, kp_ref, kc_ref, vp_ref, vc_ref, o_ref):
    m = pl.program_id(1)
    nseq = q_ref.shape[0]
    span = 2 * QB
    lane = lax.broadcasted_iota(jnp.int32, (span, KV_WIDTH), 1)
    row_c = 2 + lax.broadcasted_iota(jnp.int32, (QB, span), 0) // CHUNK
    col_c = lax.broadcasted_iota(jnp.int32, (QB, span), 1) // CHUNK
    back = row_c - col_c
    valid = jnp.logical_and(jnp.logical_and(back >= 0, back <= 2), 2 * m - 2 + col_c >= 0)
    out_lane = lax.broadcasted_iota(jnp.int32, (QB, PAIR), 1)

    def dup(x, kv):
        sw = pltpu.roll(x, HEAD_DIM, 1)
        return jnp.where(lane < HEAD_DIM, x, sw) if kv == 0 else jnp.where(lane < HEAD_DIM, sw, x)

    k_st, v_st = [], []
    for b in range(nseq):
        k_all = jnp.concatenate([kp_ref[b], kc_ref[b]], axis=0)
        v_all = jnp.concatenate([vp_ref[b], vc_ref[b]], axis=0)
        k_st.append([_stack_heads(dup(k_all, kv), HEAD_DIM).astype(BF16) for kv in range(ATT_KV_HEADS)])
        v_st.append([_stack_heads(dup(v_all, kv), HEAD_DIM).astype(BF16) for kv in range(ATT_KV_HEADS)])
    pairs = [(b, j) for b in range(nseq) for j in range(ATT_HEADS // 2)]
    kv_of = lambda j: (2 * j) // ATT_GROUP
    s = [lax.dot_general(q_ref[b, :, j * PAIR:(j + 1) * PAIR].astype(BF16), k_st[b][kv_of(j)],
                         (((1,), (1,)), ((), ())), preferred_element_type=F32) * ATT_SCALE for b, j in pairs]
    probs, dens = [], []
    for idx, (b, j) in enumerate(pairs):
        halves, den = [], []
        for e in range(2):
            sink = sink_ref[2 * j + e]
            sh = jnp.where(valid, s[idx][:, e * span:(e + 1) * span], -jnp.inf)
            mx = jnp.maximum(jnp.max(sh, -1, keepdims=True), sink)
            ph = jnp.exp(sh - mx)
            halves.append(ph)
            den.append(jnp.sum(ph, -1, keepdims=True) + jnp.exp(sink - mx))
        probs.append(jnp.concatenate(halves, axis=1).astype(BF16))
        dens.append(jnp.where(out_lane < HEAD_DIM, den[0], den[1]))
    o = [jnp.dot(probs[idx], v_st[b][kv_of(j)], preferred_element_type=F32) / dens[idx]
         for idx, (b, j) in enumerate(pairs)]
    n_pairs = ATT_HEADS // 2
    for b in range(nseq):
        o_ref[b] = jnp.concatenate(o[b * n_pairs:(b + 1) * n_pairs], axis=1).astype(o_ref.dtype)


def _attn_prompt(p3, sinks, nseq):
    batch, seq = p3.shape[0], p3.shape[1]
    nb = seq // QB
    kcol, vcol = P_AK // KV_WIDTH, P_AV // KV_WIDTH
    prev = lambda col: pl.BlockSpec((nseq, QB, KV_WIDTH), lambda g, m: (g, jnp.maximum(m - 1, 0), col))
    cur = lambda col: pl.BlockSpec((nseq, QB, KV_WIDTH), lambda g, m: (g, m, col))
    return pl.pallas_call(
        _attn_prompt_kernel,
        grid=(batch // nseq, nb),
        in_specs=[
            pl.BlockSpec(memory_space=pltpu.SMEM),
            pl.BlockSpec((nseq, QB, ATT_WIDTH), lambda g, m: (g, m, P_Q // ATT_WIDTH)),
            prev(kcol), cur(kcol), prev(vcol), cur(vcol),
        ],
        out_specs=pl.BlockSpec((nseq, QB, ATT_WIDTH), lambda g, m: (g, m, 0)),
        out_shape=jax.ShapeDtypeStruct((batch, seq, ATT_WIDTH), BF16),
        compiler_params=_cparams(("arbitrary", "arbitrary")),
        name="attn_prompt",
    )(sinks, p3, p3, p3, p3, p3)


def _attn_sample_kernel(sink_ref, q_ref, kn_ref, vn_ref, kc_ref, vc_ref, o_ref, kw_ref, vw_ref):
    t = q_ref.shape[0]
    k = jnp.concatenate([kc_ref[...], kn_ref[...]], axis=0)
    v = jnp.concatenate([vc_ref[...], vn_ref[...]], axis=0)
    o_ref[...] = _attend(q_ref[...], k, v, None, sink_ref).astype(o_ref.dtype)
    kw_ref[...] = k[t:]
    vw_ref[...] = v[t:]


def _attn_sample(p, sinks, cache_k, cache_v, batch, seq):
    kcol, vcol = P_AK // KV_WIDTH, P_AV // KV_WIDTH
    return pl.pallas_call(
        _attn_sample_kernel,
        grid=(batch,),
        in_specs=[
            pl.BlockSpec(memory_space=pltpu.SMEM),
            pl.BlockSpec((seq, ATT_WIDTH), lambda b: (b, P_Q // ATT_WIDTH)),
            pl.BlockSpec((seq, KV_WIDTH), lambda b: (b, kcol)),
            pl.BlockSpec((seq, KV_WIDTH), lambda b: (b, vcol)),
            pl.BlockSpec((WINDOW, KV_WIDTH), lambda b: (b, 0)),
            pl.BlockSpec((WINDOW, KV_WIDTH), lambda b: (b, 0)),
        ],
        out_specs=[
            pl.BlockSpec((seq, ATT_WIDTH), lambda b: (b, 0)),
            pl.BlockSpec((WINDOW, KV_WIDTH), lambda b: (b, 0)),
            pl.BlockSpec((WINDOW, KV_WIDTH), lambda b: (b, 0)),
        ],
        out_shape=[
            jax.ShapeDtypeStruct((batch * seq, ATT_WIDTH), BF16),
            jax.ShapeDtypeStruct((batch * WINDOW, KV_WIDTH), F32),
            jax.ShapeDtypeStruct((batch * WINDOW, KV_WIDTH), F32),
        ],
        compiler_params=_cparams(("arbitrary",)),
        name="attn_sample",
    )(sinks, p, p, p, cache_k, cache_v)


SEG = 256
EXP_NEG_HALF = math.exp(-0.5)
PAIR = 2 * HEAD_DIM
N_PAIRS = RW_HEADS // 2


def _split2(x):
    hi = x.astype(BF16)
    lo = (x - hi.astype(F32)).astype(BF16)
    return hi, lo


def _seg_sum(x, bd):
    c = x.shape[0]
    xb = x.astype(BF16)
    stack = jnp.concatenate([xb[:, s:s + SEG] for s in range(0, RW_WIDTH, SEG)], axis=0)
    r = jnp.dot(stack, bd, preferred_element_type=F32)
    return jnp.concatenate([r[i * c:(i + 1) * c] for i in range(RW_WIDTH // SEG)], axis=1)


def _shift_rows(x, prev_row):
    row = lax.broadcasted_iota(jnp.int32, x.shape, 0)
    return jnp.where(row == 0, prev_row, pltpu.roll(x, 1, 0))


def _dotb(a, b):
    return jnp.dot(a.astype(BF16), b.astype(BF16), preferred_element_type=F32)


def _dotb_nt(a, b):
    return lax.dot_general(a.astype(BF16), b.astype(BF16), (((1,), (1,)), ((), ())), preferred_element_type=F32)


def _dot3_tn(a, b):
    ah, al = _split2(a)
    bh, bl = _split2(b)
    return lax.dot_general(jnp.concatenate([ah, al, ah], axis=0), jnp.concatenate([bh, bh, bl], axis=0),
                           (((0,), (0,)), ((), ())), preferred_element_type=F32)


def _dot2_tn(a, b):
    ah, al = _split2(a)
    bh = b.astype(BF16)
    return lax.dot_general(jnp.concatenate([ah, al], axis=0), jnp.concatenate([bh, bh], axis=0),
                           (((0,), (0,)), ((), ())), preferred_element_type=F32)


def _dot2(a, b):
    ah, al = _split2(a)
    bh = b.astype(BF16)
    return jnp.dot(jnp.concatenate([ah, al], axis=1), jnp.concatenate([bh, bh], axis=0), preferred_element_type=F32)


def _stack_heads(x, half):
    lane = lax.broadcasted_iota(jnp.int32, x.shape, 1)
    first = (lane % (2 * half)) < half
    zero = jnp.zeros_like(x)
    return jnp.concatenate([jnp.where(first, x, zero), jnp.where(first, zero, x)], axis=0)


def _stack_heads_bf16(x, half):
    return _stack_heads(x.astype(BF16), half)


def _rwkv_pairs(ab, rb, bb, kb, v, gam_c, s_bd, n_double):
    c = ab[0].shape[0]
    pairs = range(len(ab))
    t_idx = lax.broadcasted_iota(jnp.int32, (c, 2 * c), 0)
    s_idx = lax.broadcasted_iota(jnp.int32, (c, 2 * c), 1) % c
    strict, incl = s_idx < t_idx, s_idx <= t_idx
    g = [_dotb_nt(jnp.concatenate([ab[p], rb[p]], axis=0),
                  jnp.concatenate([_stack_heads_bf16(bb[p], HEAD_DIM), _stack_heads_bf16(kb[p], HEAD_DIM)], axis=0))
         for p in pairs]
    l_ak = [jnp.where(strict, g[p][:c, 2 * c:], 0.0) for p in pairs]
    m_rbk = [jnp.concatenate([jnp.where(incl, g[p][c:, :2 * c], 0.0), jnp.where(incl, g[p][c:, 2 * c:], 0.0)], axis=1)
             for p in pairs]
    lp = [jnp.where(strict, g[p][:c, :2 * c], 0.0) for p in pairs]
    v_st = [_stack_heads_bf16(v[p], HEAD_DIM) for p in pairs]
    ya = list(ab)
    yu = [_dotb(l_ak[p], v_st[p]) for p in pairs]
    for i in range(n_double):
        y_st = [_stack_heads_bf16(jnp.concatenate([ya[p], yu[p]], axis=1), HEAD_DIM) for p in pairs]
        if i + 1 < n_double:
            z = [_dotb(lp[p], jnp.concatenate([y_st[p], _stack_heads_bf16(lp[p], c)], axis=1)) for p in pairs]
            lp = [z[p][:, 2 * PAIR:] for p in pairs]
        else:
            z = [_dotb(lp[p], y_st[p]) for p in pairs]
        ya = [ya[p] + z[p][:, :PAIR] for p in pairs]
        yu = [yu[p] + z[p][:, PAIR:2 * PAIR] for p in pairs]
    z = []
    for p in pairs:
        top = _stack_heads_bf16(jnp.concatenate([ya[p], yu[p]], axis=1), HEAD_DIM)
        bot = jnp.concatenate([jnp.zeros_like(v_st[p]), v_st[p]], axis=1)
        z.append(_dotb(m_rbk[p], jnp.concatenate([top, bot], axis=0)))
    o = [_dotb_nt(rb[p] + z[p][:, :PAIR], s_bd[p]) + z[p][:, PAIR:] for p in pairs]
    row_h = lax.broadcasted_iota(jnp.int32, (PAIR, PAIR), 0) // HEAD_DIM
    col_h = lax.broadcasted_iota(jnp.int32, (PAIR, PAIR), 1) // HEAD_DIM
    same = row_h == col_h
    q_t = [jnp.where(same, _dot3_tn(ya[p], bb[p]), 0.0) for p in pairs]
    d_t = [jnp.where(same, _dot2_tn(jnp.concatenate([yu[p], v[p]], axis=0),
                                    jnp.concatenate([bb[p], kb[p]], axis=0)), 0.0) for p in pairs]
    s_new = [(s_bd[p] + _dot2(s_bd[p], q_t[p]) + d_t[p]) * gam_c[p] for p in pairs]
    return o, s_new


def _split_rw_rows(x):
    o_wd, o_gd = RW_WIDTH, RW_COLS - GATE_LORA
    low = lax.broadcasted_iota(jnp.int32, (x.shape[0], LANES), 1) < HEAD_DIM
    pieces = [x[:, c0:c0 + LANES] for c0 in range(o_wd, o_gd, LANES)]
    rolled = [pltpu.roll(p, HEAD_DIM, 1) for p in pieces]
    merged = [jnp.where(low, rolled[m], rolled[m + 1]) for m in range(len(pieces) - 1)]
    n = RW_WIDTH // LANES
    return (x[:, :RW_WIDTH], jnp.concatenate(merged[:n], axis=1), jnp.concatenate(merged[n:], axis=1),
            jnp.where(low, pieces[0], pieces[-1]), x[:, o_gd:])


def _rwkv_kernel(xr_ref, xk_ref, xv_ref, xl_ref, sh_ref, mu_ref, w0_ref, a0_ref, kk_ref, ka_ref, rk_ref,
                 lg_ref, lb_ref, w2_ref, a2_ref, g2_ref, bd_ref, st0_ref,
                 out_ref, st_ref,
                 s_scr, pr_scr, pk_scr, pv_scr, pw_scr, pg_scr):
    c = pl.program_id(1)
    nc = pl.num_programs(1)
    nb, C = xr_ref.shape[0], xr_ref.shape[1]
    n_double = int(math.log2(C))
    prev_scr = (pr_scr, pk_scr, pv_scr, pw_scr, pg_scr)

    @pl.when(c == 0)
    def _():
        s_scr[...] = jnp.zeros_like(s_scr)
        for b in range(nb):
            for h in range(RW_HEADS):
                off = (h % 2) * HEAD_DIM
                s_scr[b * N_PAIRS + h // 2, off:off + HEAD_DIM, off:off + HEAD_DIM] = st0_ref[b, h]
            for scr, row in zip(prev_scr, _split_rw_rows(sh_ref[b])):
                scr[b, 0:1, :] = row

    def shifted(load, p_scr, mu):
        parts = []
        for b in range(nb):
            x = load(b)
            prev = _shift_rows(x, p_scr[b, 0:1, :])
            p_scr[b, 0:1, :] = x[C - 1:C, :]
            parts.append(x + (prev - x) * mu)
        return jnp.concatenate(parts, axis=0)

    mu_r, mu_k, mu_v, mu_w, mu_g = _split_rw_rows(mu_ref[...])
    wa_w = DECAY_LORA + AAA_LORA
    r = shifted(lambda b: xr_ref[b], pr_scr, mu_r)
    kr = shifted(lambda b: xk_ref[b], pk_scr, mu_k)
    v = shifted(lambda b: xv_ref[b], pv_scr, mu_v)
    xw = shifted(lambda b: xl_ref[b, :, :wa_w], pw_scr, mu_w)
    gd = shifted(lambda b: xl_ref[b, :, wa_w:wa_w + GATE_LORA], pg_scr, mu_g)
    wd = xw[:, 0:DECAY_LORA]
    ad = xw[:, DECAY_LORA:wa_w]

    lw = -EXP_NEG_HALF * jax.nn.sigmoid(w0_ref[...] + _dotb(jnp.tanh(wd), w2_ref[...]))
    a = jax.nn.sigmoid(a0_ref[...] + _dotb(ad, a2_ref[...]))
    g = _dotb(jax.nn.sigmoid(gd), g2_ref[...])
    bd = bd_ref[...]
    kk = kr * kk_ref[...]
    kk = kk * lax.rsqrt(jnp.maximum(_seg_sum(kk * kk, bd), 1e-24))
    k2 = kr * (1.0 + (a - 1.0) * ka_ref[...])

    rows = nb * C
    ti = lax.broadcasted_iota(jnp.int32, (rows, rows), 0)
    si = lax.broadcasted_iota(jnp.int32, (rows, rows), 1)
    tri = jnp.logical_and(si <= ti, si // C == ti // C).astype(BF16)
    cs2 = jnp.dot(tri, jnp.concatenate(_split2(lw), axis=1), preferred_element_type=F32)
    cs = cs2[:, :RW_WIDTH] + cs2[:, RW_WIDTH:]
    gam = jnp.exp(cs)
    ginv = jnp.exp(-cs)
    ab = -kk * jnp.exp(cs - lw)
    rb = r * gam
    bb = kk * a * ginv
    kb = k2 * ginv

    def per_pair(x, r0, r1):
        return [x[b * C + r0:b * C + r1, p * PAIR:(p + 1) * PAIR] for b in range(nb) for p in range(N_PAIRS)]

    chains = nb * N_PAIRS
    outs, s_new = _rwkv_pairs(per_pair(ab, 0, C), per_pair(rb, 0, C), per_pair(bb, 0, C), per_pair(kb, 0, C),
                              per_pair(v, 0, C), per_pair(gam, C - 1, C), [s_scr[q] for q in range(chains)], n_double)
    for q in range(chains):
        s_scr[q] = s_new[q]
    o = jnp.concatenate([jnp.concatenate(outs[b * N_PAIRS:(b + 1) * N_PAIRS], axis=1) for b in range(nb)], axis=0)

    mo = _seg_sum(o, bd) * (1.0 / HEAD_DIM)
    d = o - mo
    vo = _seg_sum(d * d, bd) * (1.0 / HEAD_DIM)
    on = d * lax.rsqrt(vo + LNX_EPS) * lg_ref[...] + lb_ref[...]
    bonus = _seg_sum(r * k2 * rk_ref[...], bd) * v
    res = ((on + bonus) * g).astype(out_ref.dtype)
    for b in range(nb):
        out_ref[b] = res[b * C:(b + 1) * C, :]

    @pl.when(c == nc - 1)
    def _():
        for b in range(nb):
            for h in range(RW_HEADS):
                off = (h % 2) * HEAD_DIM
                st_ref[b, h] = s_scr[b * N_PAIRS + h // 2, off:off + HEAD_DIM, off:off + HEAD_DIM]


def _rwkv(p3, state_shift, state0, prm, chunk, nb):
    batch, seq = p3.shape[0], p3.shape[1]
    nc = seq // chunk
    row = lambda w: pl.BlockSpec((1, w), lambda g, c: (0, 0))
    xspec = lambda w, col: pl.BlockSpec((nb, chunk, w), lambda g, c: (g, c, col))
    full = lambda a: pl.BlockSpec(a.shape, lambda g, c: (0,) * a.ndim)
    stspec = pl.BlockSpec((nb, RW_HEADS, HEAD_DIM, HEAD_DIM), lambda g, c: (g, 0, 0, 0))
    return pl.pallas_call(
        _rwkv_kernel,
        grid=(batch // nb, nc),
        in_specs=[
            xspec(RW_WIDTH, P_R // RW_WIDTH), xspec(RW_WIDTH, P_K // RW_WIDTH), xspec(RW_WIDTH, P_V // RW_WIDTH),
            xspec(LORA_W, P_LORA // LORA_W),
            pl.BlockSpec((nb, 1, RW_COLS), lambda g, c: (g, 0, 0)),
            row(RW_COLS),
            row(RW_WIDTH), row(RW_WIDTH), row(RW_WIDTH), row(RW_WIDTH), row(RW_WIDTH),
            row(RW_WIDTH), row(RW_WIDTH),
            full(prm["w2"]), full(prm["a2"]), full(prm["g2"]), full(prm["bd"]),
            stspec,
        ],
        out_specs=[
            pl.BlockSpec((nb, chunk, RW_WIDTH), lambda g, c: (g, c, 0)),
            stspec,
        ],
        out_shape=[
            jax.ShapeDtypeStruct((batch, seq, RW_WIDTH), BF16),
            jax.ShapeDtypeStruct((batch, RW_HEADS, HEAD_DIM, HEAD_DIM), F32),
        ],
        scratch_shapes=[
            pltpu.VMEM((nb * N_PAIRS, PAIR, PAIR), F32),
            pltpu.VMEM((nb, SUBLANES, RW_WIDTH), F32), pltpu.VMEM((nb, SUBLANES, RW_WIDTH), F32),
            pltpu.VMEM((nb, SUBLANES, RW_WIDTH), F32),
            pltpu.VMEM((nb, SUBLANES, DECAY_LORA + AAA_LORA), F32), pltpu.VMEM((nb, SUBLANES, GATE_LORA), F32),
        ],
        compiler_params=_cparams(("arbitrary", "arbitrary")),
        name="rwkv",
    )(p3, p3, p3, p3, state_shift, prm["mu"], prm["w0"], prm["a0"], prm["k_k"], prm["k_a"], prm["r_k"],
      prm["lnx_g"], prm["lnx_b"], prm["w2"], prm["a2"], prm["g2"], prm["bd"], state0)


D_TN = 512


def _out_proj_body(att_ref, rw_ref, h_ref, g_ref, b_ref, h1_ref, h1b_ref, w_scr):
    for c0 in range(0, D_MODEL, D_TN):
        m = jnp.dot(att_ref[...], w_scr[:ATT_WIDTH, c0:c0 + D_TN], preferred_element_type=F32)
        m = m + jnp.dot(rw_ref[...], w_scr[ATT_WIDTH:, c0:c0 + D_TN], preferred_element_type=F32)
        h1_ref[:, c0:c0 + D_TN] = ALPHA * h_ref[:, c0:c0 + D_TN] + m
    y = _layer_norm(h1_ref[...], g_ref[...], b_ref[...])
    h1_ref[...] = y
    h1b_ref[...] = y.astype(BF16)


def _out_proj(att_p, rw_p, h_p, att_s, rw_s, h_s, w_out, g, b, tm):
    n_p, n_s = att_p.shape[0], att_s.shape[0]
    assert n_s == tm
    nrb = n_p // tm
    last = lambda i: jnp.minimum(i, nrb - 1)

    def kern(ap_ref, rp_ref, hp_ref, as_ref, rs_ref, hs_ref, wa_ref, wb_ref, g_ref, b_ref,
             h1p_ref, h1bp_ref, h1s_ref, h1bs_ref, w_scr):
        i = pl.program_id(0)

        @pl.when(i == 0)
        def _():
            w_scr[:ATT_WIDTH, :] = wa_ref[...].astype(BF16)
            w_scr[ATT_WIDTH:, :] = wb_ref[...].astype(BF16)

        pl.when(i < nrb)(lambda: _out_proj_body(ap_ref, rp_ref, hp_ref, g_ref, b_ref, h1p_ref, h1bp_ref, w_scr))
        pl.when(i == nrb)(lambda: _out_proj_body(as_ref, rs_ref, hs_ref, g_ref, b_ref, h1s_ref, h1bs_ref, w_scr))

    rows_p = lambda w: pl.BlockSpec((tm, w), lambda i: (last(i), 0))
    rows_s = lambda w: pl.BlockSpec((tm, w), lambda i: (0, 0))
    return pl.pallas_call(
        kern,
        grid=(nrb + 1,),
        in_specs=[
            rows_p(ATT_WIDTH), rows_p(RW_WIDTH), rows_p(D_MODEL),
            rows_s(ATT_WIDTH), rows_s(RW_WIDTH), rows_s(D_MODEL),
            pl.BlockSpec((ATT_WIDTH, D_MODEL), lambda i: (0, 0), pipeline_mode=pl.Buffered(1)),
            pl.BlockSpec((RW_WIDTH, D_MODEL), lambda i: (1, 0), pipeline_mode=pl.Buffered(1)),
            pl.BlockSpec((1, D_MODEL), lambda i: (0, 0)),
            pl.BlockSpec((1, D_MODEL), lambda i: (0, 0)),
        ],
        out_specs=[rows_p(D_MODEL), rows_p(D_MODEL), rows_s(D_MODEL), rows_s(D_MODEL)],
        out_shape=[
            jax.ShapeDtypeStruct((n_p, D_MODEL), F32), jax.ShapeDtypeStruct((n_p, D_MODEL), BF16),
            jax.ShapeDtypeStruct((n_s, D_MODEL), F32), jax.ShapeDtypeStruct((n_s, D_MODEL), BF16),
        ],
        scratch_shapes=[pltpu.VMEM((ATT_WIDTH + RW_WIDTH, D_MODEL), BF16)],
        compiler_params=_cparams(("arbitrary",)),
        name="out_proj",
    )(att_p, rw_p, h_p, att_s, rw_s, h_s, w_out, w_out, g, b)


E_TN = 512
E_NCB = D_FF // E_TN
E_SUB = 1024
SQRT_HALF = math.sqrt(0.5)


def _ffn_up_body(bps, nseq, i, h_ref, w_scr, cwg_ref, cwv_ref, cbg_ref, cbv_ref, sg_ref, sv_ref,
                 act_ref, og_ref, ov_ref, cg_scr, cv_scr):
    tm = h_ref.shape[0]
    sub = min(tm, E_SUB)
    n_sub = tm // sub
    t_seq = tm // nseq
    row8 = lax.broadcasted_iota(jnp.int32, (SUBLANES, E_TN), 0)

    def products(r0):
        u = jnp.dot(h_ref[r0:r0 + sub, :], w_scr[...], preferred_element_type=F32)
        return u[:, :E_TN], u[:, E_TN:]

    def conv(u, prevs, cw_ref, cb_ref):
        sh1, sh2 = pltpu.roll(u, 1, 0), pltpu.roll(u, 2, 0)
        for r, prev in prevs:
            p2, p1 = prev[0:1, :], prev[1:2, :]
            fix1 = jnp.where(row8 == 0, p1, sh1[r:r + SUBLANES, :])
            fix2 = jnp.where(row8 == 0, p2, jnp.where(row8 == 1, p1, sh2[r:r + SUBLANES, :]))
            head = lambda x: [x[:r, :]] if r > 0 else []
            tail = lambda x: [x[r + SUBLANES:, :]] if r + SUBLANES < x.shape[0] else []
            sh1 = jnp.concatenate(head(sh1) + [fix1] + tail(sh1), axis=0)
            sh2 = jnp.concatenate(head(sh2) + [fix2] + tail(sh2), axis=0)
        cw = cw_ref[...]
        c = cb_ref[...] + sh2 * cw[0:1, :]
        c = c + sh1 * cw[1:2, :]
        return c + u * cw[2:3, :]

    def epilogue(r0, ug, uv, prev_g, prev_v):
        gate = conv(ug, prev_g, cwg_ref, cbg_ref)
        val = conv(uv, prev_v, cwv_ref, cbv_ref)
        gelu = 0.5 * gate * (1.0 + lax.erf(gate * SQRT_HALF))
        act_ref[r0:r0 + sub, :] = (gelu * val).astype(act_ref.dtype)

    if nseq > 1:
        ug, uv = products(0)
        epilogue(0, ug, uv, [(s * t_seq, sg_ref[s]) for s in range(nseq)],
                 [(s * t_seq, sv_ref[s]) for s in range(nseq)])
        for s in range(nseq):
            og_ref[s] = ug[(s + 1) * t_seq - SUBLANES:(s + 1) * t_seq, :]
            ov_ref[s] = uv[(s + 1) * t_seq - SUBLANES:(s + 1) * t_seq, :]
    else:
        first = (i % bps) == 0
        prev_g = jnp.where(first, sg_ref[0], cg_scr[SUBLANES - 2:SUBLANES, :])
        prev_v = jnp.where(first, sv_ref[0], cv_scr[SUBLANES - 2:SUBLANES, :])
        nxt = products(0)
        for k in range(n_sub):
            ug, uv = nxt
            if k + 1 < n_sub:
                nxt = products((k + 1) * sub)
            epilogue(k * sub, ug, uv, [(0, prev_g)], [(0, prev_v)])
            prev_g, prev_v = ug[sub - 2:sub, :], uv[sub - 2:sub, :]
        cg_scr[...] = ug[sub - SUBLANES:sub, :]
        cv_scr[...] = uv[sub - SUBLANES:sub, :]
        og_ref[0] = ug[sub - SUBLANES:sub, :]
        ov_ref[0] = uv[sub - SUBLANES:sub, :]


def _ffn_up(h_p, conv_prev_p, seq_p, tm_p, h_s, conv_prev_s, seq_s, w_up, conv_w, conv_b, w_down):
    n_p, n_s = h_p.shape[0], h_s.shape[0]
    batch_p, batch_s = n_p // seq_p, n_s // seq_s
    bps = seq_p // tm_p
    nrb = n_p // tm_p
    slab = D_FF // (E_NCB * nrb)
    last = lambda i: jnp.maximum(i - 1, 0)

    def kern(hp_ref, sgp_ref, svp_ref, hs_ref, sgs_ref, svs_ref, wg_ref, wv_ref, cwg_ref, cwv_ref, cbg_ref, cbv_ref,
             wd_ref, actp_ref, ogp_ref, ovp_ref, acts_ref, ogs_ref, ovs_ref, wdb_ref, w_scr, cg_scr, cv_scr):
        i = pl.program_id(1)

        @pl.when(i == 0)
        def _():
            w_scr[:, :E_TN] = wg_ref[...].astype(BF16)
            w_scr[:, E_TN:] = wv_ref[...].astype(BF16)
            _ffn_up_body(1, batch_s, i, hs_ref, w_scr, cwg_ref, cwv_ref, cbg_ref, cbv_ref, sgs_ref, svs_ref,
                         acts_ref, ogs_ref, ovs_ref, cg_scr, cv_scr)

        @pl.when(i > 0)
        def _():
            wdb_ref[...] = wd_ref[...].astype(BF16)
            _ffn_up_body(bps, 1, i - 1, hp_ref, w_scr, cwg_ref, cwv_ref, cbg_ref, cbv_ref, sgp_ref, svp_ref,
                         actp_ref, ogp_ref, ovp_ref, cg_scr, cv_scr)

    col = lambda shape, off: pl.BlockSpec(shape, lambda j, i: (0,) * (len(shape) - 1) + (j + off,))
    st_p = lambda off: pl.BlockSpec((1, CONV_W - 1, E_TN), lambda j, i: (last(i) // bps, 0, j + off))
    st_s = lambda off: pl.BlockSpec((batch_s, CONV_W - 1, E_TN), lambda j, i: (0, 0, j + off))
    tail_p = pl.BlockSpec((1, SUBLANES, E_TN), lambda j, i: (last(i) // bps, 0, j))
    tail_s = pl.BlockSpec((batch_s, SUBLANES, E_TN), lambda j, i: (0, 0, j))
    wd_spec = pl.BlockSpec((slab, D_MODEL), lambda j, i: (j * nrb + last(i), 0))
    return pl.pallas_call(
        kern,
        grid=(E_NCB, nrb + 1),
        in_specs=[
            pl.BlockSpec((tm_p, D_MODEL), lambda j, i: (last(i), 0)), st_p(0), st_p(E_NCB),
            pl.BlockSpec((n_s, D_MODEL), lambda j, i: (0, 0)), st_s(0), st_s(E_NCB),
            col((D_MODEL, E_TN), 0), col((D_MODEL, E_TN), E_NCB),
            col((CONV_W, E_TN), 0), col((CONV_W, E_TN), E_NCB),
            col((1, E_TN), 0), col((1, E_TN), E_NCB),
            wd_spec,
        ],
        out_specs=[
            pl.BlockSpec((tm_p, E_TN), lambda j, i: (last(i), j)), tail_p, tail_p,
            pl.BlockSpec((n_s, E_TN), lambda j, i: (0, j)), tail_s, tail_s,
            wd_spec,
        ],
        out_shape=[
            jax.ShapeDtypeStruct((n_p, D_FF), BF16),
            jax.ShapeDtypeStruct((batch_p, SUBLANES, D_FF), F32), jax.ShapeDtypeStruct((batch_p, SUBLANES, D_FF), F32),
            jax.ShapeDtypeStruct((n_s, D_FF), BF16),
            jax.ShapeDtypeStruct((batch_s, SUBLANES, D_FF), F32), jax.ShapeDtypeStruct((batch_s, SUBLANES, D_FF), F32),
            jax.ShapeDtypeStruct((D_FF, D_MODEL), BF16),
        ],
        scratch_shapes=[pltpu.VMEM((D_MODEL, 2 * E_TN), BF16),
                        pltpu.VMEM((SUBLANES, E_TN), F32), pltpu.VMEM((SUBLANES, E_TN), F32)],
        compiler_params=_cparams(("arbitrary", "arbitrary")),
        name="ffn_up",
    )(h_p, conv_prev_p, conv_prev_p, h_s, conv_prev_s, conv_prev_s, w_up, w_up, conv_w, conv_w, conv_b, conv_b, w_down)


F_TN = 512


def _ffn_down_kernel(act_ref, w_ref, h1_ref, g_ref, b_ref, y_ref):
    for c0 in range(0, D_MODEL, F_TN):
        f = jnp.dot(act_ref[...], w_ref[:, c0:c0 + F_TN], preferred_element_type=F32)
        y_ref[:, c0:c0 + F_TN] = ALPHA * h1_ref[:, c0:c0 + F_TN] + f
    y_ref[...] = _layer_norm(y_ref[...], g_ref[...], b_ref[...])


def _ffn_down(act_p, h1_p, act_s, h1_s, w_down, g, b, tm):
    n_p, n_s = act_p.shape[0], act_s.shape[0]
    assert n_s == tm
    nrb = n_p // tm
    last = lambda i: jnp.minimum(i, nrb - 1)

    def kern(ap_ref, hp_ref, as_ref, hs_ref, w_ref, g_ref, b_ref, yp_ref, ys_ref):
        i = pl.program_id(0)
        pl.when(i < nrb)(lambda: _ffn_down_kernel(ap_ref, w_ref, hp_ref, g_ref, b_ref, yp_ref))
        pl.when(i == nrb)(lambda: _ffn_down_kernel(as_ref, w_ref, hs_ref, g_ref, b_ref, ys_ref))

    rows_p = lambda w: pl.BlockSpec((tm, w), lambda i: (last(i), 0))
    rows_s = lambda w: pl.BlockSpec((tm, w), lambda i: (0, 0))
    return pl.pallas_call(
        kern,
        grid=(nrb + 1,),
        in_specs=[
            rows_p(D_FF), rows_p(D_MODEL), rows_s(D_FF), rows_s(D_MODEL),
            pl.BlockSpec((D_FF, D_MODEL), lambda i: (0, 0), pipeline_mode=pl.Buffered(1)),
            pl.BlockSpec((1, D_MODEL), lambda i: (0, 0)),
            pl.BlockSpec((1, D_MODEL), lambda i: (0, 0)),
        ],
        out_specs=[rows_p(D_MODEL), rows_s(D_MODEL)],
        out_shape=[jax.ShapeDtypeStruct((n_p, D_MODEL), F32), jax.ShapeDtypeStruct((n_s, D_MODEL), F32)],
        compiler_params=_cparams(("arbitrary",)),
        name="ffn_down",
    )(act_p, h1_p, act_s, h1_s, w_down, g, b)


def _rope_tables(pos, reps):
    half = HEAD_DIM // 2
    inv = ROPE_THETA ** (-jnp.arange(half, dtype=F32) / half)
    ang = pos.astype(F32)[:, None] * inv[None, :]
    cos, sin = jnp.cos(ang), jnp.sin(ang)
    cos128 = jnp.concatenate([cos, cos, cos, cos], axis=1)
    sin128 = jnp.concatenate([-sin, sin, -sin, sin], axis=1)
    return jnp.tile(cos128, (reps, 1)), jnp.tile(sin128, (reps, 1))


TM = 256
E_TM = 1024
RWKV_NB = 4
ATTN_NB = 2


def kernel(x_prompt, x_sample, cache_k, cache_v, state_wkv, state_shift, state_ffn_conv, ln_in_g, ln_in_b, w_in, attn_sinks, rw_mu, rw_w0, rw_w2, rw_a0, rw_a2, rw_g2, rw_k_k, rw_k_a, rw_r_k, rw_lnx_g, rw_lnx_b, w_out, ln1_g, ln1_b, ffn_w_up, ffn_conv_w, ffn_conv_b, ffn_w_down, ln2_g, ln2_b):
    l = 0
    seg = lax.broadcasted_iota(jnp.int32, (SEG, SEG), 0) // HEAD_DIM
    bd = (seg == seg.T).astype(BF16)
    row = lambda a: a.reshape(1, -1)
    rw_prm = {"mu": row(rw_mu[l]), "w0": row(rw_w0[l]), "a0": row(rw_a0[l]), "k_k": row(rw_k_k[l]),
              "k_a": row(rw_k_a[l]), "r_k": row(rw_r_k[l]), "lnx_g": row(rw_lnx_g[l]), "lnx_b": row(rw_lnx_b[l]),
              "w2": rw_w2[l].astype(BF16), "a2": rw_a2[l].astype(BF16), "g2": rw_g2[l].astype(BF16), "bd": bd}
    sinks = attn_sinks[l]
    bp, tp = x_prompt.shape[0], x_prompt.shape[1]
    bs, ts = x_sample.shape[0], x_sample.shape[1]
    n_p, n_s = bp * tp, bs * ts
    zeros = lambda *s: jnp.zeros(s, F32)

    p_p, h_p, tails_p, p_s, h_s, tails_s = _in_proj(
        x_prompt.reshape(n_p, D_MODEL), x_sample.reshape(n_s, D_MODEL), row(ln_in_g), row(ln_in_b),
        w_in[l].astype(BF16), _rope_tables(jnp.arange(tp, dtype=jnp.int32), 1),
        _rope_tables(PAST_LEN + jnp.arange(ts, dtype=jnp.int32), bs), TM, tp, ts)
    p3_p, p3_s = p_p.reshape(bp, tp, P_COLS), p_s.reshape(bs, ts, P_COLS)

    att_p = _attn_prompt(p3_p, sinks, ATTN_NB).reshape(n_p, ATT_WIDTH)
    att_s, s_k, s_v = _attn_sample(p_s, sinks, cache_k[l].reshape(bs * WINDOW, KV_WIDTH),
                                   cache_v[l].reshape(bs * WINDOW, KV_WIDTH), bs, ts)
    rw_p, p_wkv = _rwkv(p3_p, zeros(bp, 1, RW_COLS), zeros(bp, RW_HEADS, HEAD_DIM, HEAD_DIM), rw_prm, CHUNK, RWKV_NB)
    rw_s, s_wkv = _rwkv(p3_s, state_shift[l], state_wkv[l], rw_prm, ts, RWKV_NB)

    h1_p, h1b_p, h1_s, h1b_s = _out_proj(att_p, rw_p.reshape(n_p, RW_WIDTH), h_p, att_s, rw_s.reshape(n_s, RW_WIDTH), h_s,
                                         w_out[l], row(ln1_g[l]), row(ln1_b[l]), TM)
    act_p, cg_p, cv_p, act_s, cg_s, cv_s, w_down_b = _ffn_up(
        h1b_p, zeros(bp, CONV_W - 1, 2 * D_FF), tp, E_TM, h1b_s, state_ffn_conv[l], ts,
        ffn_w_up[l], ffn_conv_w[l], row(ffn_conv_b[l]), ffn_w_down[l])
    y_p, y_s = _ffn_down(act_p, h1_p, act_s, h1_s, w_down_b, row(ln2_g[l]), row(ln2_b[l]), TM)

    kv_shape = lambda b: (1, b, WINDOW, ATT_KV_HEADS, HEAD_DIM)
    p_k = p3_p[:, tp - WINDOW:, P_AK:P_AK + KV_WIDTH].reshape(kv_shape(bp))
    p_v = p3_p[:, tp - WINDOW:, P_AV:P_AV + KV_WIDTH].reshape(kv_shape(bp))
    shift_of = lambda tails, b: tails.reshape(b, -1, RW_COLS)[:, -1:, :][None]
    conv_of = lambda cg, cv: jnp.concatenate([cg[:, SUBLANES - 2:, :], cv[:, SUBLANES - 2:, :]], axis=-1)[None]
    return (y_p.reshape(bp, tp, D_MODEL), y_s.reshape(bs, ts, D_MODEL), p_k, p_v, p_wkv[None],
            shift_of(tails_p, bp), conv_of(cg_p, cv_p),
            s_k.reshape(kv_shape(bs)), s_v.reshape(kv_shape(bs)), s_wkv[None],
            shift_of(tails_s, bs), conv_of(cg_s, cv_s))
```

```python
import functools
import math

import jax
import jax.numpy as jnp
from jax import lax
from jax.experimental import pallas as pl
from jax.experimental.pallas import tpu as pltpu

D_MODEL = 2048
HEAD_DIM = 64
ATT_HEADS = 16
ATT_KV_HEADS = 2
ATT_GROUP = ATT_HEADS // ATT_KV_HEADS
ATT_WIDTH = ATT_HEADS * HEAD_DIM
KV_WIDTH = ATT_KV_HEADS * HEAD_DIM
CHUNK = 64
WINDOW = 128
ROPE_THETA = 10000.0
ATT_SCALE = HEAD_DIM ** -0.5
RW_HEADS = 16
RW_WIDTH = RW_HEADS * HEAD_DIM
DECAY_LORA = 64
AAA_LORA = 64
GATE_LORA = 160
LNX_EPS = 64e-5
ATT_COLS = ATT_WIDTH + 2 * KV_WIDTH
RW_COLS = 3 * RW_WIDTH + DECAY_LORA + AAA_LORA + GATE_LORA
D_FF = 5632
CONV_W = 3
LN_EPS = 1e-5
DEPTH = 1
ALPHA = (2 * DEPTH) ** 0.25
PAST_LEN = 1024

LORA_W = 512
P_R, P_K, P_V, P_Q = 0, 1024, 2048, 3072
P_LORA = 4096
P_AK = P_LORA + LORA_W
P_AV = P_AK + KV_WIDTH
P_COLS = P_AV + KV_WIDTH
LANES = 128
SUBLANES = 8
VMEM_LIMIT = 56 * 1024 * 1024

F32 = jnp.float32
BF16 = jnp.bfloat16


def _cparams(sem):
    return pltpu.CompilerParams(dimension_semantics=sem, vmem_limit_bytes=VMEM_LIMIT)


def _layer_norm(y, g, b):
    mu = jnp.mean(y, -1, keepdims=True)
    d = y - mu
    var = jnp.mean(d * d, -1, keepdims=True)
    return d * lax.rsqrt(var + LN_EPS) * g + b


A_TN = 512


def _rope128(x, cos, sin_signed):
    lane = lax.broadcasted_iota(jnp.int32, x.shape, 1)
    first_half = (lane % HEAD_DIM) < (HEAD_DIM // 2)
    rot = jnp.where(first_half, pltpu.roll(x, LANES - HEAD_DIM // 2, 1), pltpu.roll(x, HEAD_DIM // 2, 1))
    return x * cos + rot * sin_signed


def _is_rope_col(col):
    return (P_Q <= col < P_Q + ATT_WIDTH) or (P_AK <= col < P_AK + KV_WIDTH)


O_R = ATT_COLS
O_WD = O_R + RW_WIDTH
O_K = O_WD + DECAY_LORA
O_V = O_K + RW_WIDTH
O_AD = O_V + RW_WIDTH
O_GD = O_AD + AAA_LORA
IN_COLS = O_GD + GATE_LORA


def _in_proj_kernel(x_ref, g_ref, b_ref, w_ref, cos_ref, sin_ref, p_ref, h_ref, tail_ref, hb_ref):
    h = _layer_norm(x_ref[...], g_ref[...], b_ref[...])
    h_ref[...] = h
    hb_ref[...] = h.astype(BF16)
    nseq = tail_ref.shape[0]
    t_seq = x_ref.shape[0] // nseq

    def product(c0, width):
        acc = jnp.dot(hb_ref[...], w_ref[:, c0:c0 + width], preferred_element_type=F32)
        if c0 >= O_R:
            for s in range(nseq):
                tail_ref[s, :, c0 - O_R:c0 - O_R + width] = acc[(s + 1) * t_seq - SUBLANES:(s + 1) * t_seq, :]
        return acc

    for src, width, dst in ((0, ATT_WIDTH, P_Q), (ATT_WIDTH, 2 * KV_WIDTH, P_AK), (O_R, RW_WIDTH, P_R)):
        for c0 in range(0, width, A_TN):
            acc = product(src + c0, min(A_TN, width - c0))
            for l0 in range(0, acc.shape[1], LANES):
                col = dst + c0 + l0
                part = acc[:, l0:l0 + LANES]
                if _is_rope_col(col):
                    part = _rope128(part, cos_ref[...], sin_ref[...])
                p_ref[:, col:col + LANES] = part

    low = lax.broadcasted_iota(jnp.int32, (x_ref.shape[0], LANES), 1) < HEAD_DIM
    span = O_GD - O_WD
    rolled, first_piece = [], None
    for c0 in range(0, span, A_TN):
        acc = product(O_WD + c0, min(A_TN, span - c0))
        for l0 in range(0, acc.shape[1], LANES):
            piece = acc[:, l0:l0 + LANES]
            j = (c0 + l0) // LANES
            if j == 0:
                first_piece = piece
            rolled.append(pltpu.roll(piece, HEAD_DIM, 1))
            if j >= 1:
                m = j - 1
                dst = P_K + m * LANES if m < RW_WIDTH // LANES else P_V + (m - RW_WIDTH // LANES) * LANES
                p_ref[:, dst:dst + LANES] = jnp.where(low, rolled[m], rolled[m + 1])
            if j == span // LANES - 1:
                p_ref[:, P_LORA:P_LORA + LANES] = jnp.where(low, first_piece, piece)
    gd0 = P_LORA + DECAY_LORA + AAA_LORA
    p_ref[:, gd0:gd0 + GATE_LORA] = product(O_GD, GATE_LORA)
    p_ref[:, gd0 + GATE_LORA:P_LORA + LORA_W] = jnp.zeros((x_ref.shape[0], LORA_W - DECAY_LORA - AAA_LORA - GATE_LORA), F32)


def _to_bf16(w3, layer, rows):
    _, r, c = w3.shape

    def kern(w_ref, o_ref):
        o_ref[...] = w_ref[0].astype(BF16)

    return pl.pallas_call(
        kern,
        grid=(r // rows,),
        in_specs=[pl.BlockSpec((1, rows, c), lambda i: (layer, i, 0))],
        out_specs=pl.BlockSpec((rows, c), lambda i: (i, 0)),
        out_shape=jax.ShapeDtypeStruct((r, c), BF16),
        compiler_params=_cparams(("arbitrary",)),
        name="to_bf16",
    )(w3)


def _in_proj(xp, xs, g, b, w_p, tables_p, tables_s, tm, seq_p, seq_s):
    n_p, n_s = xp.shape[0], xs.shape[0]
    assert n_s == tm
    nrb = n_p // tm
    nseq_s = tm // seq_s
    tb = seq_p // tm
    last = lambda i: jnp.minimum(i, nrb - 1)

    def body(ins, shared, outs, scratch):
        x_ref, cos_ref, sin_ref = ins
        g_ref, b_ref, w_ref = shared
        _in_proj_kernel(x_ref, g_ref, b_ref, w_ref, cos_ref, sin_ref, *outs, *scratch)

    def kern(xp_ref, cp_ref, sp_ref, xs_ref, cs_ref, ss_ref, g_ref, b_ref, w_ref,
             pp_ref, hp_ref, tp_ref, ps_ref, hs_ref, ts_ref, hb_ref):
        i = pl.program_id(0)
        pl.when(i < nrb)(lambda: body((xp_ref, cp_ref, sp_ref), (g_ref, b_ref, w_ref), (pp_ref, hp_ref, tp_ref), (hb_ref,)))
        pl.when(i == nrb)(lambda: body((xs_ref, cs_ref, ss_ref), (g_ref, b_ref, w_ref), (ps_ref, hs_ref, ts_ref), (hb_ref,)))

    const2 = lambda shape: pl.BlockSpec(shape, lambda i: (0, 0))
    return pl.pallas_call(
        kern,
        grid=(nrb + 1,),
        in_specs=[
            pl.BlockSpec((tm, D_MODEL), lambda i: (last(i), 0)),
            pl.BlockSpec((tm, LANES), lambda i: (last(i) % tb, 0)),
            pl.BlockSpec((tm, LANES), lambda i: (last(i) % tb, 0)),
            const2((tm, D_MODEL)), const2((tm, LANES)), const2((tm, LANES)),
            const2((1, D_MODEL)), const2((1, D_MODEL)),
            pl.BlockSpec((D_MODEL, IN_COLS), lambda i: (0, 0), pipeline_mode=pl.Buffered(1)),
        ],
        out_specs=[
            pl.BlockSpec((tm, P_COLS), lambda i: (last(i), 0)),
            pl.BlockSpec((tm, D_MODEL), lambda i: (last(i), 0)),
            pl.BlockSpec((1, SUBLANES, RW_COLS), lambda i: (last(i), 0, 0)),
            const2((tm, P_COLS)), const2((tm, D_MODEL)),
            pl.BlockSpec((nseq_s, SUBLANES, RW_COLS), lambda i: (0, 0, 0)),
        ],
        out_shape=[
            jax.ShapeDtypeStruct((n_p, P_COLS), F32),
            jax.ShapeDtypeStruct((n_p, D_MODEL), F32),
            jax.ShapeDtypeStruct((nrb, SUBLANES, RW_COLS), F32),
            jax.ShapeDtypeStruct((n_s, P_COLS), F32),
            jax.ShapeDtypeStruct((n_s, D_MODEL), F32),
            jax.ShapeDtypeStruct((nseq_s, SUBLANES, RW_COLS), F32),
        ],
        scratch_shapes=[pltpu.VMEM((tm, D_MODEL), BF16)],
        compiler_params=_cparams(("arbitrary",)),
        name="in_proj",
    )(xp, *tables_p, xs, *tables_s, g, b, w_p)


def _attend(q, k, v, valid, sink_ref):
    r = q.shape[0]
    outs = []
    for kv in range(ATT_KV_HEADS):
        kh = k[:, kv * HEAD_DIM:(kv + 1) * HEAD_DIM].astype(BF16)
        vh = v[:, kv * HEAD_DIM:(kv + 1) * HEAD_DIM].astype(BF16)
        heads = [kv * ATT_GROUP + g for g in range(ATT_GROUP)]
        qs = jnp.concatenate([q[:, h * HEAD_DIM:(h + 1) * HEAD_DIM] for h in heads], axis=0)
        sink = jnp.concatenate([jnp.full((r, 1), sink_ref[h], F32) for h in heads], axis=0)
        s = lax.dot_general(qs.astype(BF16), kh, (((1,), (1,)), ((), ())), preferred_element_type=F32) * ATT_SCALE
        if valid is not None:
            s = jnp.where(valid, s, -jnp.inf)
        m = jnp.maximum(jnp.max(s, -1, keepdims=True), sink)
        p = jnp.exp(s - m)
        den = jnp.sum(p, -1, keepdims=True) + jnp.exp(sink - m)
        o = jnp.dot(p.astype(BF16), vh, preferred_element_type=F32) / den
        outs.extend(o[g * r:(g + 1) * r] for g in range(ATT_GROUP))
    return jnp.concatenate(outs, axis=1)


LOG2_E = math.log2(math.e)
QB = 2 * CHUNK


def _attn_prompt_kernel(sink_ref, q_ref, kp_ref, kc_ref, vp_ref, vc_ref, o_ref):
    m = pl.program_id(1)
    nseq = q_ref.shape[0]
    span = 2 * QB
    lane = lax.broadcasted_iota(jnp.int32, (span, KV_WIDTH), 1)
    row_c = 2 + lax.broadcasted_iota(jnp.int32, (QB, span), 0) // CHUNK
    col_c = lax.broadcasted_iota(jnp.int32, (QB, span), 1) // CHUNK
    back = row_c - col_c
    valid = jnp.logical_and(jnp.logical_and(back >= 0, back <= 2), 2 * m - 2 + col_c >= 0)
    out_lane = lax.broadcasted_iota(jnp.int32, (QB, PAIR), 1)

    def dup(x, kv):
        sw = pltpu.roll(x, HEAD_DIM, 1)
        return jnp.where(lane < HEAD_DIM, x, sw) if kv == 0 else jnp.where(lane < HEAD_DIM, sw, x)

    k_st, v_st = [], []
    for b in range(nseq):
        k_all = jnp.concatenate([kp_ref[b], kc_ref[b]], axis=0)
        v_all = jnp.concatenate([vp_ref[b], vc_ref[b]], axis=0)
        k_st.append([_stack_heads(dup(k_all, kv), HEAD_DIM).astype(BF16) for kv in range(ATT_KV_HEADS)])
        v_st.append([_stack_heads(dup(v_all, kv), HEAD_DIM).astype(BF16) for kv in range(ATT_KV_HEADS)])
    pairs = [(b, j) for b in range(nseq) for j in range(ATT_HEADS // 2)]
    kv_of = lambda j: (2 * j) // ATT_GROUP
    s = [lax.dot_general((q_ref[b, :, j * PAIR:(j + 1) * PAIR] * (ATT_SCALE * LOG2_E)).astype(BF16), k_st[b][kv_of(j)],
                         (((1,), (1,)), ((), ())), preferred_element_type=F32) for b, j in pairs]
    probs, dens = [], []
    for idx, (b, j) in enumerate(pairs):
        halves, den = [], []
        for e in range(2):
            sink = sink_ref[2 * j + e] * LOG2_E
            sh = jnp.where(valid, s[idx][:, e * span:(e + 1) * span], -jnp.inf)
            mx = jnp.maximum(jnp.max(sh, -1, keepdims=True), sink)
            ph = jnp.exp2(sh - mx)
            halves.append(ph)
            den.append(jnp.sum(ph, -1, keepdims=True) + jnp.exp2(sink - mx))
        probs.append(jnp.concatenate(halves, axis=1).astype(BF16))
        dens.append(jnp.where(out_lane < HEAD_DIM, den[0], den[1]))
    o = [jnp.dot(probs[idx], v_st[b][kv_of(j)], preferred_element_type=F32) / dens[idx]
         for idx, (b, j) in enumerate(pairs)]
    n_pairs = ATT_HEADS // 2
    for b in range(nseq):
        o_ref[b] = jnp.concatenate(o[b * n_pairs:(b + 1) * n_pairs], axis=1).astype(o_ref.dtype)


def _attn_prompt(p3, sinks, nseq):
    batch, seq = p3.shape[0], p3.shape[1]
    nb = seq // QB
    kcol, vcol = P_AK // KV_WIDTH, P_AV // KV_WIDTH
    prev = lambda col: pl.BlockSpec((nseq, QB, KV_WIDTH), lambda g, m: (g, jnp.maximum(m - 1, 0), col))
    cur = lambda col: pl.BlockSpec((nseq, QB, KV_WIDTH), lambda g, m: (g, m, col))
    return pl.pallas_call(
        _attn_prompt_kernel,
        grid=(batch // nseq, nb),
        in_specs=[
            pl.BlockSpec(memory_space=pltpu.SMEM),
            pl.BlockSpec((nseq, QB, ATT_WIDTH), lambda g, m: (g, m, P_Q // ATT_WIDTH)),
            prev(kcol), cur(kcol), prev(vcol), cur(vcol),
        ],
        out_specs=pl.BlockSpec((nseq, QB, ATT_WIDTH), lambda g, m: (g, m, 0)),
        out_shape=jax.ShapeDtypeStruct((batch, seq, ATT_WIDTH), BF16),
        compiler_params=_cparams(("arbitrary", "arbitrary")),
        name="attn_prompt",
    )(sinks, p3, p3, p3, p3, p3)


def _attn_sample_kernel(sink_ref, q_ref, kn_ref, vn_ref, kc_ref, vc_ref, o_ref, kw_ref, vw_ref):
    t = q_ref.shape[0]
    k = jnp.concatenate([kc_ref[...], kn_ref[...]], axis=0)
    v = jnp.concatenate([vc_ref[...], vn_ref[...]], axis=0)
    o_ref[...] = _attend(q_ref[...], k, v, None, sink_ref).astype(o_ref.dtype)
    kw_ref[...] = k[t:]
    vw_ref[...] = v[t:]


def _attn_sample(p, sinks, cache_k, cache_v, batch, seq):
    kcol, vcol = P_AK // KV_WIDTH, P_AV // KV_WIDTH
    return pl.pallas_call(
        _attn_sample_kernel,
        grid=(batch,),
        in_specs=[
            pl.BlockSpec(memory_space=pltpu.SMEM),
            pl.BlockSpec((seq, ATT_WIDTH), lambda b: (b, P_Q // ATT_WIDTH)),
            pl.BlockSpec((seq, KV_WIDTH), lambda b: (b, kcol)),
            pl.BlockSpec((seq, KV_WIDTH), lambda b: (b, vcol)),
            pl.BlockSpec((WINDOW, KV_WIDTH), lambda b: (b, 0)),
            pl.BlockSpec((WINDOW, KV_WIDTH), lambda b: (b, 0)),
        ],
        out_specs=[
            pl.BlockSpec((seq, ATT_WIDTH), lambda b: (b, 0)),
            pl.BlockSpec((WINDOW, KV_WIDTH), lambda b: (b, 0)),
            pl.BlockSpec((WINDOW, KV_WIDTH), lambda b: (b, 0)),
        ],
        out_shape=[
            jax.ShapeDtypeStruct((batch * seq, ATT_WIDTH), BF16),
            jax.ShapeDtypeStruct((batch * WINDOW, KV_WIDTH), F32),
            jax.ShapeDtypeStruct((batch * WINDOW, KV_WIDTH), F32),
        ],
        compiler_params=_cparams(("arbitrary",)),
        name="attn_sample",
    )(sinks, p, p, p, cache_k, cache_v)


SEG = 256
EXP_NEG_HALF = math.exp(-0.5)
PAIR = 2 * HEAD_DIM
N_PAIRS = RW_HEADS // 2


def _split2(x):
    hi = x.astype(BF16)
    lo = (x - hi.astype(F32)).astype(BF16)
    return hi, lo


def _seg_sum(x, bd):
    c = x.shape[0]
    xb = x.astype(BF16)
    stack = jnp.concatenate([xb[:, s:s + SEG] for s in range(0, RW_WIDTH, SEG)], axis=0)
    r = jnp.dot(stack, bd, preferred_element_type=F32)
    return jnp.concatenate([r[i * c:(i + 1) * c] for i in range(RW_WIDTH // SEG)], axis=1)


def _shift_rows(x, prev_row):
    row = lax.broadcasted_iota(jnp.int32, x.shape, 0)
    return jnp.where(row == 0, prev_row, pltpu.roll(x, 1, 0))


def _dotb(a, b):
    return jnp.dot(a.astype(BF16), b.astype(BF16), preferred_element_type=F32)


def _dotb_nt(a, b):
    return lax.dot_general(a.astype(BF16), b.astype(BF16), (((1,), (1,)), ((), ())), preferred_element_type=F32)


def _dot3_tn(a, b):
    ah, al = _split2(a)
    bh, bl = _split2(b)
    return lax.dot_general(jnp.concatenate([ah, al, ah], axis=0), jnp.concatenate([bh, bh, bl], axis=0),
                           (((0,), (0,)), ((), ())), preferred_element_type=F32)


def _dot2_tn(a, b):
    ah, al = _split2(a)
    bh = b.astype(BF16)
    return lax.dot_general(jnp.concatenate([ah, al], axis=0), jnp.concatenate([bh, bh], axis=0),
                           (((0,), (0,)), ((), ())), preferred_element_type=F32)


def _dot2(a, b):
    ah, al = _split2(a)
    bh = b.astype(BF16)
    return jnp.dot(jnp.concatenate([ah, al], axis=1), jnp.concatenate([bh, bh], axis=0), preferred_element_type=F32)


def _stack_heads(x, half):
    lane = lax.broadcasted_iota(jnp.int32, x.shape, 1)
    first = (lane % (2 * half)) < half
    zero = jnp.zeros_like(x)
    return jnp.concatenate([jnp.where(first, x, zero), jnp.where(first, zero, x)], axis=0)


def _stack_heads_bf16(x, half):
    return _stack_heads(x.astype(BF16), half)


def _rwkv_pairs(ab, rb, bb, kb, v, gam_c, s_bd, n_double):
    c = ab[0].shape[0]
    pairs = range(len(ab))
    t_idx = lax.broadcasted_iota(jnp.int32, (c, 2 * c), 0)
    s_idx = lax.broadcasted_iota(jnp.int32, (c, 2 * c), 1) % c
    strict, incl = s_idx < t_idx, s_idx <= t_idx
    g = [_dotb_nt(jnp.concatenate([ab[p], rb[p]], axis=0),
                  jnp.concatenate([_stack_heads_bf16(bb[p], HEAD_DIM), _stack_heads_bf16(kb[p], HEAD_DIM)], axis=0))
         for p in pairs]
    l_ak = [jnp.where(strict, g[p][:c, 2 * c:], 0.0) for p in pairs]
    m_rbk = [jnp.concatenate([jnp.where(incl, g[p][c:, :2 * c], 0.0), jnp.where(incl, g[p][c:, 2 * c:], 0.0)], axis=1)
             for p in pairs]
    lp = [jnp.where(strict, g[p][:c, :2 * c], 0.0) for p in pairs]
    v_st = [_stack_heads_bf16(v[p], HEAD_DIM) for p in pairs]
    ya = list(ab)
    yu = [_dotb(l_ak[p], v_st[p]) for p in pairs]
    for i in range(n_double):
        y_st = [_stack_heads_bf16(jnp.concatenate([ya[p], yu[p]], axis=1), HEAD_DIM) for p in pairs]
        if i + 1 < n_double:
            z = [_dotb(lp[p], jnp.concatenate([y_st[p], _stack_heads_bf16(lp[p], c)], axis=1)) for p in pairs]
            lp = [z[p][:, 2 * PAIR:] for p in pairs]
        else:
            z = [_dotb(lp[p], y_st[p]) for p in pairs]
        ya = [ya[p] + z[p][:, :PAIR] for p in pairs]
        yu = [yu[p] + z[p][:, PAIR:2 * PAIR] for p in pairs]
    z = []
    for p in pairs:
        top = _stack_heads_bf16(jnp.concatenate([ya[p], yu[p]], axis=1), HEAD_DIM)
        bot = jnp.concatenate([jnp.zeros_like(v_st[p]), v_st[p]], axis=1)
        z.append(_dotb(m_rbk[p], jnp.concatenate([top, bot], axis=0)))
    o = [_dotb_nt(rb[p] + z[p][:, :PAIR], s_bd[p]) + z[p][:, PAIR:] for p in pairs]
    row_h = lax.broadcasted_iota(jnp.int32, (PAIR, PAIR), 0) // HEAD_DIM
    col_h = lax.broadcasted_iota(jnp.int32, (PAIR, PAIR), 1) // HEAD_DIM
    same = row_h == col_h
    q_t = [jnp.where(same, _dot3_tn(ya[p], bb[p]), 0.0) for p in pairs]
    d_t = [jnp.where(same, _dot2_tn(jnp.concatenate([yu[p], v[p]], axis=0),
                                    jnp.concatenate([bb[p], kb[p]], axis=0)), 0.0) for p in pairs]
    s_new = [(s_bd[p] + _dot2(s_bd[p], q_t[p]) + d_t[p]) * gam_c[p] for p in pairs]
    return o, s_new


def _split_rw_rows(x):
    o_wd, o_gd = RW_WIDTH, RW_COLS - GATE_LORA
    low = lax.broadcasted_iota(jnp.int32, (x.shape[0], LANES), 1) < HEAD_DIM
    pieces = [x[:, c0:c0 + LANES] for c0 in range(o_wd, o_gd, LANES)]
    rolled = [pltpu.roll(p, HEAD_DIM, 1) for p in pieces]
    merged = [jnp.where(low, rolled[m], rolled[m + 1]) for m in range(len(pieces) - 1)]
    n = RW_WIDTH // LANES
    return (x[:, :RW_WIDTH], jnp.concatenate(merged[:n], axis=1), jnp.concatenate(merged[n:], axis=1),
            jnp.where(low, pieces[0], pieces[-1]), x[:, o_gd:])


def _rwkv_kernel(xr_ref, xk_ref, xv_ref, xl_ref, sh_ref, mu_ref, w0_ref, a0_ref, kk_ref, ka_ref, rk_ref,
                 lg_ref, lb_ref, w2_ref, a2_ref, g2_ref, bd_ref, st0_ref,
                 out_ref, st_ref,
                 s_scr, pr_scr, pk_scr, pv_scr, pw_scr, pg_scr):
    c = pl.program_id(1)
    nc = pl.num_programs(1)
    nb, C = xr_ref.shape[0], xr_ref.shape[1]
    n_double = int(math.log2(C))
    prev_scr = (pr_scr, pk_scr, pv_scr, pw_scr, pg_scr)

    @pl.when(c == 0)
    def _():
        s_scr[...] = jnp.zeros_like(s_scr)
        for b in range(nb):
            for h in range(RW_HEADS):
                off = (h % 2) * HEAD_DIM
                s_scr[b * N_PAIRS + h // 2, off:off + HEAD_DIM, off:off + HEAD_DIM] = st0_ref[b, h]
            for scr, row in zip(prev_scr, _split_rw_rows(sh_ref[b])):
                scr[b, 0:1, :] = row

    def shifted(load, p_scr, mu):
        parts = []
        for b in range(nb):
            x = load(b)
            prev = _shift_rows(x, p_scr[b, 0:1, :])
            p_scr[b, 0:1, :] = x[C - 1:C, :]
            parts.append(x + (prev - x) * mu)
        return jnp.concatenate(parts, axis=0)

    mu_r, mu_k, mu_v, mu_w, mu_g = _split_rw_rows(mu_ref[...])
    wa_w = DECAY_LORA + AAA_LORA
    r = shifted(lambda b: xr_ref[b], pr_scr, mu_r)
    kr = shifted(lambda b: xk_ref[b], pk_scr, mu_k)
    v = shifted(lambda b: xv_ref[b], pv_scr, mu_v)
    xw = shifted(lambda b: xl_ref[b, :, :wa_w], pw_scr, mu_w)
    gd = shifted(lambda b: xl_ref[b, :, wa_w:wa_w + GATE_LORA], pg_scr, mu_g)
    wd = xw[:, 0:DECAY_LORA]
    ad = xw[:, DECAY_LORA:wa_w]

    lw = -EXP_NEG_HALF * jax.nn.sigmoid(w0_ref[...] + _dotb(jnp.tanh(wd), w2_ref[...]))
    a = jax.nn.sigmoid(a0_ref[...] + _dotb(ad, a2_ref[...]))
    g = _dotb(jax.nn.sigmoid(gd), g2_ref[...])
    bd = bd_ref[...]
    kk = kr * kk_ref[...]
    kk = kk * lax.rsqrt(jnp.maximum(_seg_sum(kk * kk, bd), 1e-24))
    k2 = kr * (1.0 + (a - 1.0) * ka_ref[...])

    rows = nb * C
    ti = lax.broadcasted_iota(jnp.int32, (rows, rows), 0)
    si = lax.broadcasted_iota(jnp.int32, (rows, rows), 1)
    tri = jnp.logical_and(si <= ti, si // C == ti // C).astype(BF16)
    cs2 = jnp.dot(tri, jnp.concatenate(_split2(lw), axis=1), preferred_element_type=F32)
    cs = cs2[:, :RW_WIDTH] + cs2[:, RW_WIDTH:]
    gam = jnp.exp(cs)
    ginv = jnp.exp(-cs)
    ab = -kk * jnp.exp(cs - lw)
    rb = r * gam
    bb = kk * a * ginv
    kb = k2 * ginv

    def per_pair(x, r0, r1):
        return [x[b * C + r0:b * C + r1, p * PAIR:(p + 1) * PAIR] for b in range(nb) for p in range(N_PAIRS)]

    chains = nb * N_PAIRS
    outs, s_new = _rwkv_pairs(per_pair(ab, 0, C), per_pair(rb, 0, C), per_pair(bb, 0, C), per_pair(kb, 0, C),
                              per_pair(v, 0, C), per_pair(gam, C - 1, C), [s_scr[q] for q in range(chains)], n_double)
    for q in range(chains):
        s_scr[q] = s_new[q]
    o = jnp.concatenate([jnp.concatenate(outs[b * N_PAIRS:(b + 1) * N_PAIRS], axis=1) for b in range(nb)], axis=0)

    mo = _seg_sum(o, bd) * (1.0 / HEAD_DIM)
    d = o - mo
    vo = _seg_sum(d * d, bd) * (1.0 / HEAD_DIM)
    on = d * lax.rsqrt(vo + LNX_EPS) * lg_ref[...] + lb_ref[...]
    bonus = _seg_sum(r * k2 * rk_ref[...], bd) * v
    res = ((on + bonus) * g).astype(out_ref.dtype)
    for b in range(nb):
        out_ref[b] = res[b * C:(b + 1) * C, :]

    @pl.when(c == nc - 1)
    def _():
        for b in range(nb):
            for h in range(RW_HEADS):
                off = (h % 2) * HEAD_DIM
                st_ref[b, h] = s_scr[b * N_PAIRS + h // 2, off:off + HEAD_DIM, off:off + HEAD_DIM]


def _rwkv(p3, state_shift, state0, prm, chunk, nb):
    batch, seq = p3.shape[0], p3.shape[1]
    nc = seq // chunk
    row = lambda w: pl.BlockSpec((1, w), lambda g, c: (0, 0))
    xspec = lambda w, col: pl.BlockSpec((nb, chunk, w), lambda g, c: (g, c, col))
    full = lambda a: pl.BlockSpec(a.shape, lambda g, c: (0,) * a.ndim)
    stspec = pl.BlockSpec((nb, RW_HEADS, HEAD_DIM, HEAD_DIM), lambda g, c: (g, 0, 0, 0))
    return pl.pallas_call(
        _rwkv_kernel,
        grid=(batch // nb, nc),
        in_specs=[
            xspec(RW_WIDTH, P_R // RW_WIDTH), xspec(RW_WIDTH, P_K // RW_WIDTH), xspec(RW_WIDTH, P_V // RW_WIDTH),
            xspec(LORA_W, P_LORA // LORA_W),
            pl.BlockSpec((nb, 1, RW_COLS), lambda g, c: (g, 0, 0)),
            row(RW_COLS),
            row(RW_WIDTH), row(RW_WIDTH), row(RW_WIDTH), row(RW_WIDTH), row(RW_WIDTH),
            row(RW_WIDTH), row(RW_WIDTH),
            full(prm["w2"]), full(prm["a2"]), full(prm["g2"]), full(prm["bd"]),
            stspec,
        ],
        out_specs=[
            pl.BlockSpec((nb, chunk, RW_WIDTH), lambda g, c: (g, c, 0)),
            stspec,
        ],
        out_shape=[
            jax.ShapeDtypeStruct((batch, seq, RW_WIDTH), BF16),
            jax.ShapeDtypeStruct((batch, RW_HEADS, HEAD_DIM, HEAD_DIM), F32),
        ],
        scratch_shapes=[
            pltpu.VMEM((nb * N_PAIRS, PAIR, PAIR), F32),
            pltpu.VMEM((nb, SUBLANES, RW_WIDTH), F32), pltpu.VMEM((nb, SUBLANES, RW_WIDTH), F32),
            pltpu.VMEM((nb, SUBLANES, RW_WIDTH), F32),
            pltpu.VMEM((nb, SUBLANES, DECAY_LORA + AAA_LORA), F32), pltpu.VMEM((nb, SUBLANES, GATE_LORA), F32),
        ],
        compiler_params=_cparams(("arbitrary", "arbitrary")),
        name="rwkv",
    )(p3, p3, p3, p3, state_shift, prm["mu"], prm["w0"], prm["a0"], prm["k_k"], prm["k_a"], prm["r_k"],
      prm["lnx_g"], prm["lnx_b"], prm["w2"], prm["a2"], prm["g2"], prm["bd"], state0)


D_TN = 512


def _out_proj_body(att_ref, rw_ref, h_ref, g_ref, b_ref, h1_ref, h1b_ref, w_scr):
    for c0 in range(0, D_MODEL, D_TN):
        m = jnp.dot(att_ref[...], w_scr[:ATT_WIDTH, c0:c0 + D_TN], preferred_element_type=F32)
        m = m + jnp.dot(rw_ref[...], w_scr[ATT_WIDTH:, c0:c0 + D_TN], preferred_element_type=F32)
        h1_ref[:, c0:c0 + D_TN] = ALPHA * h_ref[:, c0:c0 + D_TN] + m
    y = _layer_norm(h1_ref[...], g_ref[...], b_ref[...])
    h1_ref[...] = y
    h1b_ref[...] = y.astype(BF16)


def _out_proj(att_p, rw_p, h_p, att_s, rw_s, h_s, w_out, g, b, tm):
    n_p, n_s = att_p.shape[0], att_s.shape[0]
    nrb = n_p // tm
    last = lambda i: jnp.minimum(i, nrb - 1)

    def kern(ap_ref, rp_ref, hp_ref, as_ref, rs_ref, hs_ref, wa_ref, wb_ref, g_ref, b_ref,
             h1p_ref, h1bp_ref, h1s_ref, h1bs_ref, w_scr):
        i = pl.program_id(0)

        @pl.when(i == 0)
        def _():
            w_scr[:ATT_WIDTH, :] = wa_ref[...].astype(BF16)
            w_scr[ATT_WIDTH:, :] = wb_ref[...].astype(BF16)

        pl.when(i < nrb)(lambda: _out_proj_body(ap_ref, rp_ref, hp_ref, g_ref, b_ref, h1p_ref, h1bp_ref, w_scr))
        pl.when(i == nrb)(lambda: _out_proj_body(as_ref, rs_ref, hs_ref, g_ref, b_ref, h1s_ref, h1bs_ref, w_scr))

    rows_p = lambda w: pl.BlockSpec((tm, w), lambda i: (last(i), 0))
    rows_s = lambda w: pl.BlockSpec((n_s, w), lambda i: (0, 0))
    return pl.pallas_call(
        kern,
        grid=(nrb + 1,),
        in_specs=[
            rows_p(ATT_WIDTH), rows_p(RW_WIDTH), rows_p(D_MODEL),
            rows_s(ATT_WIDTH), rows_s(RW_WIDTH), rows_s(D_MODEL),
            pl.BlockSpec((ATT_WIDTH, D_MODEL), lambda i: (0, 0), pipeline_mode=pl.Buffered(1)),
            pl.BlockSpec((RW_WIDTH, D_MODEL), lambda i: (1, 0), pipeline_mode=pl.Buffered(1)),
            pl.BlockSpec((1, D_MODEL), lambda i: (0, 0)),
            pl.BlockSpec((1, D_MODEL), lambda i: (0, 0)),
        ],
        out_specs=[rows_p(D_MODEL), rows_p(D_MODEL), rows_s(D_MODEL), rows_s(D_MODEL)],
        out_shape=[
            jax.ShapeDtypeStruct((n_p, D_MODEL), F32), jax.ShapeDtypeStruct((n_p, D_MODEL), BF16),
            jax.ShapeDtypeStruct((n_s, D_MODEL), F32), jax.ShapeDtypeStruct((n_s, D_MODEL), BF16),
        ],
        scratch_shapes=[pltpu.VMEM((ATT_WIDTH + RW_WIDTH, D_MODEL), BF16)],
        compiler_params=_cparams(("arbitrary",)),
        name="out_proj",
    )(att_p, rw_p, h_p, att_s, rw_s, h_s, w_out, w_out, g, b)


E_TN = 512
E_NCB = D_FF // E_TN
E_SUB = 1024
SQRT_HALF = math.sqrt(0.5)


def _ffn_up_body(bps, nseq, i, h_ref, w_scr, cwg_ref, cwv_ref, cbg_ref, cbv_ref, sg_ref, sv_ref,
                 act_ref, og_ref, ov_ref, cg_scr, cv_scr):
    tm = h_ref.shape[0]
    sub = min(tm, E_SUB)
    n_sub = tm // sub
    t_seq = tm // nseq
    row8 = lax.broadcasted_iota(jnp.int32, (SUBLANES, E_TN), 0)

    def products(r0):
        u = jnp.dot(h_ref[r0:r0 + sub, :], w_scr[...], preferred_element_type=F32)
        return u[:, :E_TN], u[:, E_TN:]

    def conv(u, prevs, cw_ref, cb_ref):
        sh1, sh2 = pltpu.roll(u, 1, 0), pltpu.roll(u, 2, 0)
        for r, prev in prevs:
            p2, p1 = prev[0:1, :], prev[1:2, :]
            fix1 = jnp.where(row8 == 0, p1, sh1[r:r + SUBLANES, :])
            fix2 = jnp.where(row8 == 0, p2, jnp.where(row8 == 1, p1, sh2[r:r + SUBLANES, :]))
            head = lambda x: [x[:r, :]] if r > 0 else []
            tail = lambda x: [x[r + SUBLANES:, :]] if r + SUBLANES < x.shape[0] else []
            sh1 = jnp.concatenate(head(sh1) + [fix1] + tail(sh1), axis=0)
            sh2 = jnp.concatenate(head(sh2) + [fix2] + tail(sh2), axis=0)
        cw = cw_ref[...]
        c = cb_ref[...] + sh2 * cw[0:1, :]
        c = c + sh1 * cw[1:2, :]
        return c + u * cw[2:3, :]

    def epilogue(r0, ug, uv, prev_g, prev_v):
        gate = conv(ug, prev_g, cwg_ref, cbg_ref)
        val = conv(uv, prev_v, cwv_ref, cbv_ref)
        gelu = 0.5 * gate * (1.0 + lax.erf(gate * SQRT_HALF))
        act_ref[r0:r0 + sub, :] = (gelu * val).astype(act_ref.dtype)

    if nseq > 1:
        ug, uv = products(0)
        epilogue(0, ug, uv, [(s * t_seq, sg_ref[s]) for s in range(nseq)],
                 [(s * t_seq, sv_ref[s]) for s in range(nseq)])
        for s in range(nseq):
            og_ref[s] = ug[(s + 1) * t_seq - SUBLANES:(s + 1) * t_seq, :]
            ov_ref[s] = uv[(s + 1) * t_seq - SUBLANES:(s + 1) * t_seq, :]
    else:
        first = (i % bps) == 0
        prev_g = jnp.where(first, sg_ref[0], cg_scr[SUBLANES - 2:SUBLANES, :])
        prev_v = jnp.where(first, sv_ref[0], cv_scr[SUBLANES - 2:SUBLANES, :])
        nxt = products(0)
        for k in range(n_sub):
            ug, uv = nxt
            if k + 1 < n_sub:
                nxt = products((k + 1) * sub)
            epilogue(k * sub, ug, uv, [(0, prev_g)], [(0, prev_v)])
            prev_g, prev_v = ug[sub - 2:sub, :], uv[sub - 2:sub, :]
        cg_scr[...] = ug[sub - SUBLANES:sub, :]
        cv_scr[...] = uv[sub - SUBLANES:sub, :]
        og_ref[0] = ug[sub - SUBLANES:sub, :]
        ov_ref[0] = uv[sub - SUBLANES:sub, :]


def _ffn_up(h_p, conv_prev_p, seq_p, tm_p, h_s, conv_prev_s, seq_s, w_up, conv_w, conv_b, w_down):
    n_p, n_s = h_p.shape[0], h_s.shape[0]
    batch_p, batch_s = n_p // seq_p, n_s // seq_s
    bps = seq_p // tm_p
    nrb = n_p // tm_p
    slab = D_FF // (E_NCB * nrb)
    last = lambda i: jnp.maximum(i - 1, 0)

    def kern(hp_ref, sgp_ref, svp_ref, hs_ref, sgs_ref, svs_ref, wg_ref, wv_ref, cwg_ref, cwv_ref, cbg_ref, cbv_ref,
             wd_ref, actp_ref, ogp_ref, ovp_ref, acts_ref, ogs_ref, ovs_ref, wdb_ref, w_scr, cg_scr, cv_scr):
        i = pl.program_id(1)

        @pl.when(i == 0)
        def _():
            w_scr[:, :E_TN] = wg_ref[...].astype(BF16)
            w_scr[:, E_TN:] = wv_ref[...].astype(BF16)
            _ffn_up_body(1, batch_s, i, hs_ref, w_scr, cwg_ref, cwv_ref, cbg_ref, cbv_ref, sgs_ref, svs_ref,
                         acts_ref, ogs_ref, ovs_ref, cg_scr, cv_scr)

        @pl.when(i > 0)
        def _():
            wdb_ref[...] = wd_ref[...].astype(BF16)
            _ffn_up_body(bps, 1, i - 1, hp_ref, w_scr, cwg_ref, cwv_ref, cbg_ref, cbv_ref, sgp_ref, svp_ref,
                         actp_ref, ogp_ref, ovp_ref, cg_scr, cv_scr)

    col = lambda shape, off: pl.BlockSpec(shape, lambda j, i: (0,) * (len(shape) - 1) + (j + off,))
    st_p = lambda off: pl.BlockSpec((1, CONV_W - 1, E_TN), lambda j, i: (last(i) // bps, 0, j + off))
    st_s = lambda off: pl.BlockSpec((batch_s, CONV_W - 1, E_TN), lambda j, i: (0, 0, j + off))
    tail_p = pl.BlockSpec((1, SUBLANES, E_TN), lambda j, i: (last(i) // bps, 0, j))
    tail_s = pl.BlockSpec((batch_s, SUBLANES, E_TN), lambda j, i: (0, 0, j))
    wd_spec = pl.BlockSpec((slab, D_MODEL), lambda j, i: (j * nrb + last(i), 0))
    return pl.pallas_call(
        kern,
        grid=(E_NCB, nrb + 1),
        in_specs=[
            pl.BlockSpec((tm_p, D_MODEL), lambda j, i: (last(i), 0)), st_p(0), st_p(E_NCB),
            pl.BlockSpec((n_s, D_MODEL), lambda j, i: (0, 0)), st_s(0), st_s(E_NCB),
            col((D_MODEL, E_TN), 0), col((D_MODEL, E_TN), E_NCB),
            col((CONV_W, E_TN), 0), col((CONV_W, E_TN), E_NCB),
            col((1, E_TN), 0), col((1, E_TN), E_NCB),
            wd_spec,
        ],
        out_specs=[
            pl.BlockSpec((tm_p, E_TN), lambda j, i: (last(i), j)), tail_p, tail_p,
            pl.BlockSpec((n_s, E_TN), lambda j, i: (0, j)), tail_s, tail_s,
            wd_spec,
        ],
        out_shape=[
            jax.ShapeDtypeStruct((n_p, D_FF), BF16),
            jax.ShapeDtypeStruct((batch_p, SUBLANES, D_FF), F32), jax.ShapeDtypeStruct((batch_p, SUBLANES, D_FF), F32),
            jax.ShapeDtypeStruct((n_s, D_FF), BF16),
            jax.ShapeDtypeStruct((batch_s, SUBLANES, D_FF), F32), jax.ShapeDtypeStruct((batch_s, SUBLANES, D_FF), F32),
            jax.ShapeDtypeStruct((D_FF, D_MODEL), BF16),
        ],
        scratch_shapes=[pltpu.VMEM((D_MODEL, 2 * E_TN), BF16),
                        pltpu.VMEM((SUBLANES, E_TN), F32), pltpu.VMEM((SUBLANES, E_TN), F32)],
        compiler_params=_cparams(("arbitrary", "arbitrary")),
        name="ffn_up",
    )(h_p, conv_prev_p, conv_prev_p, h_s, conv_prev_s, conv_prev_s, w_up, w_up, conv_w, conv_w, conv_b, conv_b, w_down)


F_TN = 512


def _ffn_down_kernel(act_ref, w_ref, h1_ref, g_ref, b_ref, y_ref):
    for c0 in range(0, D_MODEL, F_TN):
        f = jnp.dot(act_ref[...], w_ref[:, c0:c0 + F_TN], preferred_element_type=F32)
        y_ref[:, c0:c0 + F_TN] = ALPHA * h1_ref[:, c0:c0 + F_TN] + f
    y_ref[...] = _layer_norm(y_ref[...], g_ref[...], b_ref[...])


def _ffn_down(act_p, h1_p, act_s, h1_s, w_down, g, b, tm):
    n_p, n_s = act_p.shape[0], act_s.shape[0]
    nrb = n_p // tm
    last = lambda i: jnp.minimum(i, nrb - 1)

    def kern(ap_ref, hp_ref, as_ref, hs_ref, w_ref, g_ref, b_ref, yp_ref, ys_ref):
        i = pl.program_id(0)
        pl.when(i < nrb)(lambda: _ffn_down_kernel(ap_ref, w_ref, hp_ref, g_ref, b_ref, yp_ref))
        pl.when(i == nrb)(lambda: _ffn_down_kernel(as_ref, w_ref, hs_ref, g_ref, b_ref, ys_ref))

    rows_p = lambda w: pl.BlockSpec((tm, w), lambda i: (last(i), 0))
    rows_s = lambda w: pl.BlockSpec((n_s, w), lambda i: (0, 0))
    return pl.pallas_call(
        kern,
        grid=(nrb + 1,),
        in_specs=[
            rows_p(D_FF), rows_p(D_MODEL), rows_s(D_FF), rows_s(D_MODEL),
            pl.BlockSpec((D_FF, D_MODEL), lambda i: (0, 0), pipeline_mode=pl.Buffered(1)),
            pl.BlockSpec((1, D_MODEL), lambda i: (0, 0)),
            pl.BlockSpec((1, D_MODEL), lambda i: (0, 0)),
        ],
        out_specs=[rows_p(D_MODEL), rows_s(D_MODEL)],
        out_shape=[jax.ShapeDtypeStruct((n_p, D_MODEL), F32), jax.ShapeDtypeStruct((n_s, D_MODEL), F32)],
        compiler_params=_cparams(("arbitrary",)),
        name="ffn_down",
    )(act_p, h1_p, act_s, h1_s, w_down, g, b)


def _rope_tables(pos, reps):
    half = HEAD_DIM // 2
    inv = ROPE_THETA ** (-jnp.arange(half, dtype=F32) / half)
    ang = pos.astype(F32)[:, None] * inv[None, :]
    cos, sin = jnp.cos(ang), jnp.sin(ang)
    cos128 = jnp.concatenate([cos, cos, cos, cos], axis=1)
    sin128 = jnp.concatenate([-sin, sin, -sin, sin], axis=1)
    return jnp.tile(cos128, (reps, 1)), jnp.tile(sin128, (reps, 1))


TM = 256
D_TM = 256
E_TM = 1024
F_TM = 256
RWKV_NB = 4
ATTN_NB = 2


def kernel(x_prompt, x_sample, cache_k, cache_v, state_wkv, state_shift, state_ffn_conv, ln_in_g, ln_in_b, w_in, attn_sinks, rw_mu, rw_w0, rw_w2, rw_a0, rw_a2, rw_g2, rw_k_k, rw_k_a, rw_r_k, rw_lnx_g, rw_lnx_b, w_out, ln1_g, ln1_b, ffn_w_up, ffn_conv_w, ffn_conv_b, ffn_w_down, ln2_g, ln2_b):
    l = 0
    seg = lax.broadcasted_iota(jnp.int32, (SEG, SEG), 0) // HEAD_DIM
    bd = (seg == seg.T).astype(BF16)
    row = lambda a: a.reshape(1, -1)
    rw_prm = {"mu": row(rw_mu[l]), "w0": row(rw_w0[l]), "a0": row(rw_a0[l]), "k_k": row(rw_k_k[l]),
              "k_a": row(rw_k_a[l]), "r_k": row(rw_r_k[l]), "lnx_g": row(rw_lnx_g[l]), "lnx_b": row(rw_lnx_b[l]),
              "w2": rw_w2[l].astype(BF16), "a2": rw_a2[l].astype(BF16), "g2": rw_g2[l].astype(BF16), "bd": bd}
    sinks = attn_sinks[l]
    bp, tp = x_prompt.shape[0], x_prompt.shape[1]
    bs, ts = x_sample.shape[0], x_sample.shape[1]
    n_p, n_s = bp * tp, bs * ts
    zeros = lambda *s: jnp.zeros(s, F32)

    p_p, h_p, tails_p, p_s, h_s, tails_s = _in_proj(
        x_prompt.reshape(n_p, D_MODEL), x_sample.reshape(n_s, D_MODEL), row(ln_in_g), row(ln_in_b),
        _to_bf16(w_in, l, TM), _rope_tables(jnp.arange(tp, dtype=jnp.int32), 1),
        _rope_tables(PAST_LEN + jnp.arange(ts, dtype=jnp.int32), bs), TM, tp, ts)
    p3_p, p3_s = p_p.reshape(bp, tp, P_COLS), p_s.reshape(bs, ts, P_COLS)

    att_p = _attn_prompt(p3_p, sinks, ATTN_NB).reshape(n_p, ATT_WIDTH)
    att_s, s_k, s_v = _attn_sample(p_s, sinks, cache_k[l].reshape(bs * WINDOW, KV_WIDTH),
                                   cache_v[l].reshape(bs * WINDOW, KV_WIDTH), bs, ts)
    rw_p, p_wkv = _rwkv(p3_p, zeros(bp, 1, RW_COLS), zeros(bp, RW_HEADS, HEAD_DIM, HEAD_DIM), rw_prm, CHUNK, RWKV_NB)
    rw_s, s_wkv = _rwkv(p3_s, state_shift[l], state_wkv[l], rw_prm, ts, RWKV_NB)

    h1_p, h1b_p, h1_s, h1b_s = _out_proj(att_p, rw_p.reshape(n_p, RW_WIDTH), h_p, att_s, rw_s.reshape(n_s, RW_WIDTH), h_s,
                                         w_out[l], row(ln1_g[l]), row(ln1_b[l]), D_TM)
    act_p, cg_p, cv_p, act_s, cg_s, cv_s, w_down_b = _ffn_up(
        h1b_p, zeros(bp, CONV_W - 1, 2 * D_FF), tp, E_TM, h1b_s, state_ffn_conv[l], ts,
        ffn_w_up[l], ffn_conv_w[l], row(ffn_conv_b[l]), ffn_w_down[l])
    y_p, y_s = _ffn_down(act_p, h1_p, act_s, h1_s, w_down_b, row(ln2_g[l]), row(ln2_b[l]), F_TM)

    kv_shape = lambda b: (1, b, WINDOW, ATT_KV_HEADS, HEAD_DIM)
    p_k = p3_p[:, tp - WINDOW:, P_AK:P_AK + KV_WIDTH].reshape(kv_shape(bp))
    p_v = p3_p[:, tp - WINDOW:, P_AV:P_AV + KV_WIDTH].reshape(kv_shape(bp))
    shift_of = lambda tails, b: tails.reshape(b, -1, RW_COLS)[:, -1:, :][None]
    conv_of = lambda cg, cv: jnp.concatenate([cg[:, SUBLANES - 2:, :], cv[:, SUBLANES - 2:, :]], axis=-1)[None]
    return (y_p.reshape(bp, tp, D_MODEL), y_s.reshape(bs, ts, D_MODEL), p_k, p_v, p_wkv[None],
            shift_of(tails_p, bp), conv_of(cg_p, cv_p),
            s_k.reshape(kv_shape(bs)), s_v.reshape(kv_shape(bs)), s_wkv[None],
            shift_of(tails_s, bs), conv_of(cg_s, cv_s))
```

```python
import functools
import math

import jax
import jax.numpy as jnp
from jax import lax
from jax.experimental import pallas as pl
from jax.experimental.pallas import tpu as pltpu

D_MODEL = 2048
HEAD_DIM = 64
ATT_HEADS = 16
ATT_KV_HEADS = 2
ATT_GROUP = ATT_HEADS // ATT_KV_HEADS
ATT_WIDTH = ATT_HEADS * HEAD_DIM
KV_WIDTH = ATT_KV_HEADS * HEAD_DIM
CHUNK = 64
WINDOW = 128
ROPE_THETA = 10000.0
ATT_SCALE = HEAD_DIM ** -0.5
RW_HEADS = 16
RW_WIDTH = RW_HEADS * HEAD_DIM
DECAY_LORA = 64
AAA_LORA = 64
GATE_LORA = 160
LNX_EPS = 64e-5
ATT_COLS = ATT_WIDTH + 2 * KV_WIDTH
RW_COLS = 3 * RW_WIDTH + DECAY_LORA + AAA_LORA + GATE_LORA
D_FF = 5632
CONV_W = 3
LN_EPS = 1e-5
DEPTH = 1
ALPHA = (2 * DEPTH) ** 0.25
PAST_LEN = 1024

LORA_W = 512
P_R, P_K, P_V, P_Q = 0, 1024, 2048, 3072
P_LORA = 4096
P_AK = P_LORA + LORA_W
P_AV = P_AK + KV_WIDTH
P_COLS = P_AV + KV_WIDTH
LANES = 128
SUBLANES = 8
VMEM_LIMIT = 56 * 1024 * 1024

F32 = jnp.float32
BF16 = jnp.bfloat16


def _cparams(sem):
    return pltpu.CompilerParams(dimension_semantics=sem, vmem_limit_bytes=VMEM_LIMIT)


def _layer_norm(y, g, b):
    mu = jnp.mean(y, -1, keepdims=True)
    d = y - mu
    var = jnp.mean(d * d, -1, keepdims=True)
    return d * lax.rsqrt(var + LN_EPS) * g + b


A_TN = 512


def _rope128(x, cos, sin_signed):
    lane = lax.broadcasted_iota(jnp.int32, x.shape, 1)
    first_half = (lane % HEAD_DIM) < (HEAD_DIM // 2)
    rot = jnp.where(first_half, pltpu.roll(x, LANES - HEAD_DIM // 2, 1), pltpu.roll(x, HEAD_DIM // 2, 1))
    return x * cos + rot * sin_signed


def _is_rope_col(col):
    return (P_Q <= col < P_Q + ATT_WIDTH) or (P_AK <= col < P_AK + KV_WIDTH)


O_R = ATT_COLS
O_WD = O_R + RW_WIDTH
O_K = O_WD + DECAY_LORA
O_V = O_K + RW_WIDTH
O_AD = O_V + RW_WIDTH
O_GD = O_AD + AAA_LORA
IN_COLS = O_GD + GATE_LORA


def _in_proj_kernel(x_ref, g_ref, b_ref, w_ref, cos_ref, sin_ref, p_ref, h_ref, tail_ref, hb_ref):
    h = _layer_norm(x_ref[...], g_ref[...], b_ref[...])
    h_ref[...] = h
    hb_ref[...] = h.astype(BF16)
    nseq = tail_ref.shape[0]
    t_seq = x_ref.shape[0] // nseq

    def product(c0, width):
        acc = jnp.dot(hb_ref[...], w_ref[:, c0:c0 + width], preferred_element_type=F32)
        if c0 >= O_R:
            for s in range(nseq):
                tail_ref[s, :, c0 - O_R:c0 - O_R + width] = acc[(s + 1) * t_seq - SUBLANES:(s + 1) * t_seq, :]
        return acc

    for src, width, dst in ((0, ATT_WIDTH, P_Q), (ATT_WIDTH, 2 * KV_WIDTH, P_AK), (O_R, RW_WIDTH, P_R)):
        for c0 in range(0, width, A_TN):
            acc = product(src + c0, min(A_TN, width - c0))
            for l0 in range(0, acc.shape[1], LANES):
                col = dst + c0 + l0
                part = acc[:, l0:l0 + LANES]
                if _is_rope_col(col):
                    part = _rope128(part, cos_ref[...], sin_ref[...])
                p_ref[:, col:col + LANES] = part

    low = lax.broadcasted_iota(jnp.int32, (x_ref.shape[0], LANES), 1) < HEAD_DIM
    span = O_GD - O_WD
    rolled, first_piece = [], None
    for c0 in range(0, span, A_TN):
        acc = product(O_WD + c0, min(A_TN, span - c0))
        for l0 in range(0, acc.shape[1], LANES):
            piece = acc[:, l0:l0 + LANES]
            j = (c0 + l0) // LANES
            if j == 0:
                first_piece = piece
            rolled.append(pltpu.roll(piece, HEAD_DIM, 1))
            if j >= 1:
                m = j - 1
                dst = P_K + m * LANES if m < RW_WIDTH // LANES else P_V + (m - RW_WIDTH // LANES) * LANES
                p_ref[:, dst:dst + LANES] = jnp.where(low, rolled[m], rolled[m + 1])
            if j == span // LANES - 1:
                p_ref[:, P_LORA:P_LORA + LANES] = jnp.where(low, first_piece, piece)
    gd0 = P_LORA + DECAY_LORA + AAA_LORA
    p_ref[:, gd0:gd0 + GATE_LORA] = product(O_GD, GATE_LORA)
    p_ref[:, gd0 + GATE_LORA:P_LORA + LORA_W] = jnp.zeros((x_ref.shape[0], LORA_W - DECAY_LORA - AAA_LORA - GATE_LORA), F32)


def _in_proj(xp, xs, g, b, w_p, tables_p, tables_s, tm, seq_p, seq_s):
    n_p, n_s = xp.shape[0], xs.shape[0]
    assert n_s == tm
    nrb = n_p // tm
    nseq_s = tm // seq_s
    tb = seq_p // tm
    last = lambda i: jnp.minimum(i, nrb - 1)

    def body(ins, shared, outs, scratch):
        x_ref, cos_ref, sin_ref = ins
        g_ref, b_ref, w_ref = shared
        _in_proj_kernel(x_ref, g_ref, b_ref, w_ref, cos_ref, sin_ref, *outs, *scratch)

    def kern(xp_ref, cp_ref, sp_ref, xs_ref, cs_ref, ss_ref, g_ref, b_ref, w_ref,
             pp_ref, hp_ref, tp_ref, ps_ref, hs_ref, ts_ref, hb_ref):
        i = pl.program_id(0)
        pl.when(i < nrb)(lambda: body((xp_ref, cp_ref, sp_ref), (g_ref, b_ref, w_ref), (pp_ref, hp_ref, tp_ref), (hb_ref,)))
        pl.when(i == nrb)(lambda: body((xs_ref, cs_ref, ss_ref), (g_ref, b_ref, w_ref), (ps_ref, hs_ref, ts_ref), (hb_ref,)))

    const2 = lambda shape: pl.BlockSpec(shape, lambda i: (0, 0))
    return pl.pallas_call(
        kern,
        grid=(nrb + 1,),
        in_specs=[
            pl.BlockSpec((tm, D_MODEL), lambda i: (last(i), 0)),
            pl.BlockSpec((tm, LANES), lambda i: (last(i) % tb, 0)),
            pl.BlockSpec((tm, LANES), lambda i: (last(i) % tb, 0)),
            const2((tm, D_MODEL)), const2((tm, LANES)), const2((tm, LANES)),
            const2((1, D_MODEL)), const2((1, D_MODEL)),
            pl.BlockSpec((D_MODEL, IN_COLS), lambda i: (0, 0), pipeline_mode=pl.Buffered(1)),
        ],
        out_specs=[
            pl.BlockSpec((tm, P_COLS), lambda i: (last(i), 0)),
            pl.BlockSpec((tm, D_MODEL), lambda i: (last(i), 0)),
            pl.BlockSpec((1, SUBLANES, RW_COLS), lambda i: (last(i), 0, 0)),
            const2((tm, P_COLS)), const2((tm, D_MODEL)),
            pl.BlockSpec((nseq_s, SUBLANES, RW_COLS), lambda i: (0, 0, 0)),
        ],
        out_shape=[
            jax.ShapeDtypeStruct((n_p, P_COLS), F32),
            jax.ShapeDtypeStruct((n_p, D_MODEL), F32),
            jax.ShapeDtypeStruct((nrb, SUBLANES, RW_COLS), F32),
            jax.ShapeDtypeStruct((n_s, P_COLS), F32),
            jax.ShapeDtypeStruct((n_s, D_MODEL), F32),
            jax.ShapeDtypeStruct((nseq_s, SUBLANES, RW_COLS), F32),
        ],
        scratch_shapes=[pltpu.VMEM((tm, D_MODEL), BF16)],
        compiler_params=_cparams(("arbitrary",)),
        name="in_proj",
    )(xp, *tables_p, xs, *tables_s, g, b, w_p)


def _attend(q, k, v, valid, sink_ref):
    r = q.shape[0]
    outs = []
    for kv in range(ATT_KV_HEADS):
        kh = k[:, kv * HEAD_DIM:(kv + 1) * HEAD_DIM].astype(BF16)
        vh = v[:, kv * HEAD_DIM:(kv + 1) * HEAD_DIM].astype(BF16)
        heads = [kv * ATT_GROUP + g for g in range(ATT_GROUP)]
        qs = jnp.concatenate([q[:, h * HEAD_DIM:(h + 1) * HEAD_DIM] for h in heads], axis=0)
        sink = jnp.concatenate([jnp.full((r, 1), sink_ref[h], F32) for h in heads], axis=0)
        s = lax.dot_general(qs.astype(BF16), kh, (((1,), (1,)), ((), ())), preferred_element_type=F32) * ATT_SCALE
        if valid is not None:
            s = jnp.where(valid, s, -jnp.inf)
        m = jnp.maximum(jnp.max(s, -1, keepdims=True), sink)
        p = jnp.exp(s - m)
        den = jnp.sum(p, -1, keepdims=True) + jnp.exp(sink - m)
        o = jnp.dot(p.astype(BF16), vh, preferred_element_type=F32) / den
        outs.extend(o[g * r:(g + 1) * r] for g in range(ATT_GROUP))
    return jnp.concatenate(outs, axis=1)


LOG2_E = math.log2(math.e)
QB = 2 * CHUNK


def _attn_prompt_kernel(sink_ref, q_ref, kp_ref, kc_ref, vp_ref, vc_ref, o_ref):
    m = pl.program_id(1)
    nseq = q_ref.shape[0]
    span = 2 * QB
    lane = lax.broadcasted_iota(jnp.int32, (span, KV_WIDTH), 1)
    row_c = 2 + lax.broadcasted_iota(jnp.int32, (QB, span), 0) // CHUNK
    col_c = lax.broadcasted_iota(jnp.int32, (QB, span), 1) // CHUNK
    back = row_c - col_c
    valid = jnp.logical_and(jnp.logical_and(back >= 0, back <= 2), 2 * m - 2 + col_c >= 0)
    out_lane = lax.broadcasted_iota(jnp.int32, (QB, PAIR), 1)

    def dup(x, kv):
        sw = pltpu.roll(x, HEAD_DIM, 1)
        return jnp.where(lane < HEAD_DIM, x, sw) if kv == 0 else jnp.where(lane < HEAD_DIM, sw, x)

    k_st, v_st = [], []
    for b in range(nseq):
        k_all = jnp.concatenate([kp_ref[b], kc_ref[b]], axis=0)
        v_all = jnp.concatenate([vp_ref[b], vc_ref[b]], axis=0)
        k_st.append([_stack_heads(dup(k_all, kv), HEAD_DIM).astype(BF16) for kv in range(ATT_KV_HEADS)])
        v_st.append([_stack_heads(dup(v_all, kv), HEAD_DIM).astype(BF16) for kv in range(ATT_KV_HEADS)])
    pairs = [(b, j) for b in range(nseq) for j in range(ATT_HEADS // 2)]
    kv_of = lambda j: (2 * j) // ATT_GROUP
    s = [lax.dot_general((q_ref[b, :, j * PAIR:(j + 1) * PAIR] * (ATT_SCALE * LOG2_E)).astype(BF16), k_st[b][kv_of(j)],
                         (((1,), (1,)), ((), ())), preferred_element_type=F32) for b, j in pairs]
    probs, dens = [], []
    for idx, (b, j) in enumerate(pairs):
        halves, den = [], []
        for e in range(2):
            sink = sink_ref[2 * j + e] * LOG2_E
            sh = jnp.where(valid, s[idx][:, e * span:(e + 1) * span], -jnp.inf)
            mx = jnp.maximum(jnp.max(sh, -1, keepdims=True), sink)
            ph = jnp.exp2(sh - mx)
            halves.append(ph)
            den.append(jnp.sum(ph, -1, keepdims=True) + jnp.exp2(sink - mx))
        probs.append(jnp.concatenate(halves, axis=1).astype(BF16))
        dens.append(jnp.where(out_lane < HEAD_DIM, den[0], den[1]))
    o = [jnp.dot(probs[idx], v_st[b][kv_of(j)], preferred_element_type=F32) / dens[idx]
         for idx, (b, j) in enumerate(pairs)]
    n_pairs = ATT_HEADS // 2
    for b in range(nseq):
        o_ref[b] = jnp.concatenate(o[b * n_pairs:(b + 1) * n_pairs], axis=1).astype(o_ref.dtype)


def _attn_prompt(p3, sinks, nseq):
    batch, seq = p3.shape[0], p3.shape[1]
    nb = seq // QB
    kcol, vcol = P_AK // KV_WIDTH, P_AV // KV_WIDTH
    prev = lambda col: pl.BlockSpec((nseq, QB, KV_WIDTH), lambda g, m: (g, jnp.maximum(m - 1, 0), col))
    cur = lambda col: pl.BlockSpec((nseq, QB, KV_WIDTH), lambda g, m: (g, m, col))
    return pl.pallas_call(
        _attn_prompt_kernel,
        grid=(batch // nseq, nb),
        in_specs=[
            pl.BlockSpec(memory_space=pltpu.SMEM),
            pl.BlockSpec((nseq, QB, ATT_WIDTH), lambda g, m: (g, m, P_Q // ATT_WIDTH)),
            prev(kcol), cur(kcol), prev(vcol), cur(vcol),
        ],
        out_specs=pl.BlockSpec((nseq, QB, ATT_WIDTH), lambda g, m: (g, m, 0)),
        out_shape=jax.ShapeDtypeStruct((batch, seq, ATT_WIDTH), BF16),
        compiler_params=_cparams(("arbitrary", "arbitrary")),
        name="attn_prompt",
    )(sinks, p3, p3, p3, p3, p3)


def _attn_sample_kernel(sink_ref, q_ref, kn_ref, vn_ref, kc_ref, vc_ref, o_ref, kw_ref, vw_ref):
    t = q_ref.shape[0]
    k = jnp.concatenate([kc_ref[...], kn_ref[...]], axis=0)
    v = jnp.concatenate([vc_ref[...], vn_ref[...]], axis=0)
    o_ref[...] = _attend(q_ref[...], k, v, None, sink_ref).astype(o_ref.dtype)
    kw_ref[...] = k[t:]
    vw_ref[...] = v[t:]


def _attn_sample(p, sinks, cache_k, cache_v, batch, seq):
    kcol, vcol = P_AK // KV_WIDTH, P_AV // KV_WIDTH
    return pl.pallas_call(
        _attn_sample_kernel,
        grid=(batch,),
        in_specs=[
            pl.BlockSpec(memory_space=pltpu.SMEM),
            pl.BlockSpec((seq, ATT_WIDTH), lambda b: (b, P_Q // ATT_WIDTH)),
            pl.BlockSpec((seq, KV_WIDTH), lambda b: (b, kcol)),
            pl.BlockSpec((seq, KV_WIDTH), lambda b: (b, vcol)),
            pl.BlockSpec((WINDOW, KV_WIDTH), lambda b: (b, 0)),
            pl.BlockSpec((WINDOW, KV_WIDTH), lambda b: (b, 0)),
        ],
        out_specs=[
            pl.BlockSpec((seq, ATT_WIDTH), lambda b: (b, 0)),
            pl.BlockSpec((WINDOW, KV_WIDTH), lambda b: (b, 0)),
            pl.BlockSpec((WINDOW, KV_WIDTH), lambda b: (b, 0)),
        ],
        out_shape=[
            jax.ShapeDtypeStruct((batch * seq, ATT_WIDTH), BF16),
            jax.ShapeDtypeStruct((batch * WINDOW, KV_WIDTH), F32),
            jax.ShapeDtypeStruct((batch * WINDOW, KV_WIDTH), F32),
        ],
        compiler_params=_cparams(("arbitrary",)),
        name="attn_sample",
    )(sinks, p, p, p, cache_k, cache_v)


SEG = 256
EXP_NEG_HALF = math.exp(-0.5)
PAIR = 2 * HEAD_DIM
N_PAIRS = RW_HEADS // 2


def _split2(x):
    hi = x.astype(BF16)
    lo = (x - hi.astype(F32)).astype(BF16)
    return hi, lo


def _seg_sum(x, bd):
    c = x.shape[0]
    xb = x.astype(BF16)
    stack = jnp.concatenate([xb[:, s:s + SEG] for s in range(0, RW_WIDTH, SEG)], axis=0)
    r = jnp.dot(stack, bd, preferred_element_type=F32)
    return jnp.concatenate([r[i * c:(i + 1) * c] for i in range(RW_WIDTH // SEG)], axis=1)


def _shift_rows(x, prev_row):
    row = lax.broadcasted_iota(jnp.int32, x.shape, 0)
    return jnp.where(row == 0, prev_row, pltpu.roll(x, 1, 0))


def _dotb(a, b):
    return jnp.dot(a.astype(BF16), b.astype(BF16), preferred_element_type=F32)


def _dotb_nt(a, b):
    return lax.dot_general(a.astype(BF16), b.astype(BF16), (((1,), (1,)), ((), ())), preferred_element_type=F32)


def _dot3_tn(a, b):
    ah, al = _split2(a)
    bh, bl = _split2(b)
    return lax.dot_general(jnp.concatenate([ah, al, ah], axis=0), jnp.concatenate([bh, bh, bl], axis=0),
                           (((0,), (0,)), ((), ())), preferred_element_type=F32)


def _dot2_tn(a, b):
    ah, al = _split2(a)
    bh = b.astype(BF16)
    return lax.dot_general(jnp.concatenate([ah, al], axis=0), jnp.concatenate([bh, bh], axis=0),
                           (((0,), (0,)), ((), ())), preferred_element_type=F32)


def _dot2(a, b):
    ah, al = _split2(a)
    bh = b.astype(BF16)
    return jnp.dot(jnp.concatenate([ah, al], axis=1), jnp.concatenate([bh, bh], axis=0), preferred_element_type=F32)


def _stack_heads(x, half):
    lane = lax.broadcasted_iota(jnp.int32, x.shape, 1)
    first = (lane % (2 * half)) < half
    zero = jnp.zeros_like(x)
    return jnp.concatenate([jnp.where(first, x, zero), jnp.where(first, zero, x)], axis=0)


def _stack_heads_bf16(x, half):
    return _stack_heads(x.astype(BF16), half)


def _rwkv_pairs(ab, rb, bb, kb, v, gam_c, s_bd, n_double):
    c = ab[0].shape[0]
    pairs = range(len(ab))
    t_idx = lax.broadcasted_iota(jnp.int32, (c, 2 * c), 0)
    s_idx = lax.broadcasted_iota(jnp.int32, (c, 2 * c), 1) % c
    strict, incl = s_idx < t_idx, s_idx <= t_idx
    g = [_dotb_nt(jnp.concatenate([ab[p], rb[p]], axis=0),
                  jnp.concatenate([_stack_heads_bf16(bb[p], HEAD_DIM), _stack_heads_bf16(kb[p], HEAD_DIM)], axis=0))
         for p in pairs]
    l_ak = [jnp.where(strict, g[p][:c, 2 * c:], 0.0) for p in pairs]
    m_rbk = [jnp.concatenate([jnp.where(incl, g[p][c:, :2 * c], 0.0), jnp.where(incl, g[p][c:, 2 * c:], 0.0)], axis=1)
             for p in pairs]
    lp = [jnp.where(strict, g[p][:c, :2 * c], 0.0) for p in pairs]
    v_st = [_stack_heads_bf16(v[p], HEAD_DIM) for p in pairs]
    ya = list(ab)
    yu = [_dotb(l_ak[p], v_st[p]) for p in pairs]
    for i in range(n_double):
        y_st = [_stack_heads_bf16(jnp.concatenate([ya[p], yu[p]], axis=1), HEAD_DIM) for p in pairs]
        if i + 1 < n_double:
            z = [_dotb(lp[p], jnp.concatenate([y_st[p], _stack_heads_bf16(lp[p], c)], axis=1)) for p in pairs]
            lp = [z[p][:, 2 * PAIR:] for p in pairs]
        else:
            z = [_dotb(lp[p], y_st[p]) for p in pairs]
        ya = [ya[p] + z[p][:, :PAIR] for p in pairs]
        yu = [yu[p] + z[p][:, PAIR:2 * PAIR] for p in pairs]
    z = []
    for p in pairs:
        top = _stack_heads_bf16(jnp.concatenate([ya[p], yu[p]], axis=1), HEAD_DIM)
        bot = jnp.concatenate([jnp.zeros_like(v_st[p]), v_st[p]], axis=1)
        z.append(_dotb(m_rbk[p], jnp.concatenate([top, bot], axis=0)))
    o = [_dotb_nt(rb[p] + z[p][:, :PAIR], s_bd[p]) + z[p][:, PAIR:] for p in pairs]
    row_h = lax.broadcasted_iota(jnp.int32, (PAIR, PAIR), 0) // HEAD_DIM
    col_h = lax.broadcasted_iota(jnp.int32, (PAIR, PAIR), 1) // HEAD_DIM
    same = row_h == col_h
    q_t = [jnp.where(same, _dot3_tn(ya[p], bb[p]), 0.0) for p in pairs]
    d_t = [jnp.where(same, _dot2_tn(jnp.concatenate([yu[p], v[p]], axis=0),
                                    jnp.concatenate([bb[p], kb[p]], axis=0)), 0.0) for p in pairs]
    s_new = [(s_bd[p] + _dot2(s_bd[p], q_t[p]) + d_t[p]) * gam_c[p] for p in pairs]
    return o, s_new


def _split_rw_rows(x):
    o_wd, o_gd = RW_WIDTH, RW_COLS - GATE_LORA
    low = lax.broadcasted_iota(jnp.int32, (x.shape[0], LANES), 1) < HEAD_DIM
    pieces = [x[:, c0:c0 + LANES] for c0 in range(o_wd, o_gd, LANES)]
    rolled = [pltpu.roll(p, HEAD_DIM, 1) for p in pieces]
    merged = [jnp.where(low, rolled[m], rolled[m + 1]) for m in range(len(pieces) - 1)]
    n = RW_WIDTH // LANES
    return (x[:, :RW_WIDTH], jnp.concatenate(merged[:n], axis=1), jnp.concatenate(merged[n:], axis=1),
            jnp.where(low, pieces[0], pieces[-1]), x[:, o_gd:])


def _rwkv_kernel(xr_ref, xk_ref, xv_ref, xl_ref, sh_ref, mu_ref, w0_ref, a0_ref, kk_ref, ka_ref, rk_ref,
                 lg_ref, lb_ref, w2_ref, a2_ref, g2_ref, bd_ref, st0_ref,
                 out_ref, st_ref,
                 s_scr, pr_scr, pk_scr, pv_scr, pw_scr, pg_scr):
    c = pl.program_id(1)
    nc = pl.num_programs(1)
    nb, C = xr_ref.shape[0], xr_ref.shape[1]
    n_double = int(math.log2(C))
    prev_scr = (pr_scr, pk_scr, pv_scr, pw_scr, pg_scr)

    @pl.when(c == 0)
    def _():
        s_scr[...] = jnp.zeros_like(s_scr)
        for b in range(nb):
            for h in range(RW_HEADS):
                off = (h % 2) * HEAD_DIM
                s_scr[b * N_PAIRS + h // 2, off:off + HEAD_DIM, off:off + HEAD_DIM] = st0_ref[b, h]
            for scr, row in zip(prev_scr, _split_rw_rows(sh_ref[b])):
                scr[b, 0:1, :] = row

    def shifted(load, p_scr, mu):
        parts = []
        for b in range(nb):
            x = load(b)
            prev = _shift_rows(x, p_scr[b, 0:1, :])
            p_scr[b, 0:1, :] = x[C - 1:C, :]
            parts.append(x + (prev - x) * mu)
        return jnp.concatenate(parts, axis=0)

    mu_r, mu_k, mu_v, mu_w, mu_g = _split_rw_rows(mu_ref[...])
    wa_w = DECAY_LORA + AAA_LORA
    r = shifted(lambda b: xr_ref[b], pr_scr, mu_r)
    kr = shifted(lambda b: xk_ref[b], pk_scr, mu_k)
    v = shifted(lambda b: xv_ref[b], pv_scr, mu_v)
    xw = shifted(lambda b: xl_ref[b, :, :wa_w], pw_scr, mu_w)
    gd = shifted(lambda b: xl_ref[b, :, wa_w:wa_w + GATE_LORA], pg_scr, mu_g)
    wd = xw[:, 0:DECAY_LORA]
    ad = xw[:, DECAY_LORA:wa_w]

    lw = -EXP_NEG_HALF * jax.nn.sigmoid(w0_ref[...] + _dotb(jnp.tanh(wd), w2_ref[...]))
    a = jax.nn.sigmoid(a0_ref[...] + _dotb(ad, a2_ref[...]))
    g = _dotb(jax.nn.sigmoid(gd), g2_ref[...])
    bd = bd_ref[...]
    kk = kr * kk_ref[...]
    kk = kk * lax.rsqrt(jnp.maximum(_seg_sum(kk * kk, bd), 1e-24))
    k2 = kr * (1.0 + (a - 1.0) * ka_ref[...])

    rows = nb * C
    ti = lax.broadcasted_iota(jnp.int32, (rows, rows), 0)
    si = lax.broadcasted_iota(jnp.int32, (rows, rows), 1)
    tri = jnp.logical_and(si <= ti, si // C == ti // C).astype(BF16)
    cs2 = jnp.dot(tri, jnp.concatenate(_split2(lw), axis=1), preferred_element_type=F32)
    cs = cs2[:, :RW_WIDTH] + cs2[:, RW_WIDTH:]
    gam = jnp.exp(cs)
    ginv = jnp.exp(-cs)
    ab = -kk * jnp.exp(cs - lw)
    rb = r * gam
    bb = kk * a * ginv
    kb = k2 * ginv

    def per_pair(x, r0, r1):
        return [x[b * C + r0:b * C + r1, p * PAIR:(p + 1) * PAIR] for b in range(nb) for p in range(N_PAIRS)]

    chains = nb * N_PAIRS
    outs, s_new = _rwkv_pairs(per_pair(ab, 0, C), per_pair(rb, 0, C), per_pair(bb, 0, C), per_pair(kb, 0, C),
                              per_pair(v, 0, C), per_pair(gam, C - 1, C), [s_scr[q] for q in range(chains)], n_double)
    for q in range(chains):
        s_scr[q] = s_new[q]
    o = jnp.concatenate([jnp.concatenate(outs[b * N_PAIRS:(b + 1) * N_PAIRS], axis=1) for b in range(nb)], axis=0)

    mo = _seg_sum(o, bd) * (1.0 / HEAD_DIM)
    d = o - mo
    vo = _seg_sum(d * d, bd) * (1.0 / HEAD_DIM)
    on = d * lax.rsqrt(vo + LNX_EPS) * lg_ref[...] + lb_ref[...]
    bonus = _seg_sum(r * k2 * rk_ref[...], bd) * v
    res = ((on + bonus) * g).astype(out_ref.dtype)
    for b in range(nb):
        out_ref[b] = res[b * C:(b + 1) * C, :]

    @pl.when(c == nc - 1)
    def _():
        for b in range(nb):
            for h in range(RW_HEADS):
                off = (h % 2) * HEAD_DIM
                st_ref[b, h] = s_scr[b * N_PAIRS + h // 2, off:off + HEAD_DIM, off:off + HEAD_DIM]


def _rwkv(p3, state_shift, state0, prm, chunk, nb):
    batch, seq = p3.shape[0], p3.shape[1]
    nc = seq // chunk
    row = lambda w: pl.BlockSpec((1, w), lambda g, c: (0, 0))
    xspec = lambda w, col: pl.BlockSpec((nb, chunk, w), lambda g, c: (g, c, col))
    full = lambda a: pl.BlockSpec(a.shape, lambda g, c: (0,) * a.ndim)
    stspec = pl.BlockSpec((nb, RW_HEADS, HEAD_DIM, HEAD_DIM), lambda g, c: (g, 0, 0, 0))
    return pl.pallas_call(
        _rwkv_kernel,
        grid=(batch // nb, nc),
        in_specs=[
            xspec(RW_WIDTH, P_R // RW_WIDTH), xspec(RW_WIDTH, P_K // RW_WIDTH), xspec(RW_WIDTH, P_V // RW_WIDTH),
            xspec(LORA_W, P_LORA // LORA_W),
            pl.BlockSpec((nb, 1, RW_COLS), lambda g, c: (g, 0, 0)),
            row(RW_COLS),
            row(RW_WIDTH), row(RW_WIDTH), row(RW_WIDTH), row(RW_WIDTH), row(RW_WIDTH),
            row(RW_WIDTH), row(RW_WIDTH),
            full(prm["w2"]), full(prm["a2"]), full(prm["g2"]), full(prm["bd"]),
            stspec,
        ],
        out_specs=[
            pl.BlockSpec((nb, chunk, RW_WIDTH), lambda g, c: (g, c, 0)),
            stspec,
        ],
        out_shape=[
            jax.ShapeDtypeStruct((batch, seq, RW_WIDTH), BF16),
            jax.ShapeDtypeStruct((batch, RW_HEADS, HEAD_DIM, HEAD_DIM), F32),
        ],
        scratch_shapes=[
            pltpu.VMEM((nb * N_PAIRS, PAIR, PAIR), F32),
            pltpu.VMEM((nb, SUBLANES, RW_WIDTH), F32), pltpu.VMEM((nb, SUBLANES, RW_WIDTH), F32),
            pltpu.VMEM((nb, SUBLANES, RW_WIDTH), F32),
            pltpu.VMEM((nb, SUBLANES, DECAY_LORA + AAA_LORA), F32), pltpu.VMEM((nb, SUBLANES, GATE_LORA), F32),
        ],
        compiler_params=_cparams(("arbitrary", "arbitrary")),
        name="rwkv",
    )(p3, p3, p3, p3, state_shift, prm["mu"], prm["w0"], prm["a0"], prm["k_k"], prm["k_a"], prm["r_k"],
      prm["lnx_g"], prm["lnx_b"], prm["w2"], prm["a2"], prm["g2"], prm["bd"], state0)


D_TN = 512


def _out_proj_body(att_ref, rw_ref, h_ref, g_ref, b_ref, h1_ref, h1b_ref, w_scr):
    for c0 in range(0, D_MODEL, D_TN):
        m = jnp.dot(att_ref[...], w_scr[:ATT_WIDTH, c0:c0 + D_TN], preferred_element_type=F32)
        m = m + jnp.dot(rw_ref[...], w_scr[ATT_WIDTH:, c0:c0 + D_TN], preferred_element_type=F32)
        h1_ref[:, c0:c0 + D_TN] = ALPHA * h_ref[:, c0:c0 + D_TN] + m
    y = _layer_norm(h1_ref[...], g_ref[...], b_ref[...])
    h1_ref[...] = y
    h1b_ref[...] = y.astype(BF16)


def _out_proj(att_p, rw_p, h_p, att_s, rw_s, h_s, w_out, g, b, tm):
    n_p, n_s = att_p.shape[0], att_s.shape[0]
    nrb = n_p // tm
    last = lambda i: jnp.minimum(i, nrb - 1)

    def kern(ap_ref, rp_ref, hp_ref, as_ref, rs_ref, hs_ref, wa_ref, wb_ref, g_ref, b_ref,
             h1p_ref, h1bp_ref, h1s_ref, h1bs_ref, w_scr):
        i = pl.program_id(0)

        @pl.when(i == 0)
        def _():
            w_scr[:ATT_WIDTH, :] = wa_ref[...].astype(BF16)
            w_scr[ATT_WIDTH:, :] = wb_ref[...].astype(BF16)

        pl.when(i < nrb)(lambda: _out_proj_body(ap_ref, rp_ref, hp_ref, g_ref, b_ref, h1p_ref, h1bp_ref, w_scr))
        pl.when(i == nrb)(lambda: _out_proj_body(as_ref, rs_ref, hs_ref, g_ref, b_ref, h1s_ref, h1bs_ref, w_scr))

    rows_p = lambda w: pl.BlockSpec((tm, w), lambda i: (last(i), 0))
    rows_s = lambda w: pl.BlockSpec((n_s, w), lambda i: (0, 0))
    return pl.pallas_call(
        kern,
        grid=(nrb + 1,),
        in_specs=[
            rows_p(ATT_WIDTH), rows_p(RW_WIDTH), rows_p(D_MODEL),
            rows_s(ATT_WIDTH), rows_s(RW_WIDTH), rows_s(D_MODEL),
            pl.BlockSpec((ATT_WIDTH, D_MODEL), lambda i: (0, 0), pipeline_mode=pl.Buffered(1)),
            pl.BlockSpec((RW_WIDTH, D_MODEL), lambda i: (1, 0), pipeline_mode=pl.Buffered(1)),
            pl.BlockSpec((1, D_MODEL), lambda i: (0, 0)),
            pl.BlockSpec((1, D_MODEL), lambda i: (0, 0)),
        ],
        out_specs=[rows_p(D_MODEL), rows_p(D_MODEL), rows_s(D_MODEL), rows_s(D_MODEL)],
        out_shape=[
            jax.ShapeDtypeStruct((n_p, D_MODEL), F32), jax.ShapeDtypeStruct((n_p, D_MODEL), BF16),
            jax.ShapeDtypeStruct((n_s, D_MODEL), F32), jax.ShapeDtypeStruct((n_s, D_MODEL), BF16),
        ],
        scratch_shapes=[pltpu.VMEM((ATT_WIDTH + RW_WIDTH, D_MODEL), BF16)],
        compiler_params=_cparams(("arbitrary",)),
        name="out_proj",
    )(att_p, rw_p, h_p, att_s, rw_s, h_s, w_out, w_out, g, b)


E_TN = 512
E_NCB = D_FF // E_TN
E_SUB = 1024
SQRT_HALF = math.sqrt(0.5)


def _ffn_up_body(bps, nseq, i, h_ref, w_scr, cwg_ref, cwv_ref, cbg_ref, cbv_ref, sg_ref, sv_ref,
                 act_ref, og_ref, ov_ref, cg_scr, cv_scr):
    tm = h_ref.shape[0]
    sub = min(tm, E_SUB)
    n_sub = tm // sub
    t_seq = tm // nseq
    row8 = lax.broadcasted_iota(jnp.int32, (SUBLANES, E_TN), 0)

    def products(r0):
        u = jnp.dot(h_ref[r0:r0 + sub, :], w_scr[...], preferred_element_type=F32)
        return u[:, :E_TN], u[:, E_TN:]

    def conv(u, prevs, cw_ref, cb_ref):
        sh1, sh2 = pltpu.roll(u, 1, 0), pltpu.roll(u, 2, 0)
        for r, prev in prevs:
            p2, p1 = prev[0:1, :], prev[1:2, :]
            fix1 = jnp.where(row8 == 0, p1, sh1[r:r + SUBLANES, :])
            fix2 = jnp.where(row8 == 0, p2, jnp.where(row8 == 1, p1, sh2[r:r + SUBLANES, :]))
            head = lambda x: [x[:r, :]] if r > 0 else []
            tail = lambda x: [x[r + SUBLANES:, :]] if r + SUBLANES < x.shape[0] else []
            sh1 = jnp.concatenate(head(sh1) + [fix1] + tail(sh1), axis=0)
            sh2 = jnp.concatenate(head(sh2) + [fix2] + tail(sh2), axis=0)
        cw = cw_ref[...]
        c = cb_ref[...] + sh2 * cw[0:1, :]
        c = c + sh1 * cw[1:2, :]
        return c + u * cw[2:3, :]

    def epilogue(r0, ug, uv, prev_g, prev_v):
        gate = conv(ug, prev_g, cwg_ref, cbg_ref)
        val = conv(uv, prev_v, cwv_ref, cbv_ref)
        gelu = 0.5 * gate * (1.0 + lax.erf(gate * SQRT_HALF))
        act_ref[r0:r0 + sub, :] = (gelu * val).astype(act_ref.dtype)

    if nseq > 1:
        ug, uv = products(0)
        epilogue(0, ug, uv, [(s * t_seq, sg_ref[s]) for s in range(nseq)],
                 [(s * t_seq, sv_ref[s]) for s in range(nseq)])
        for s in range(nseq):
            og_ref[s] = ug[(s + 1) * t_seq - SUBLANES:(s + 1) * t_seq, :]
            ov_ref[s] = uv[(s + 1) * t_seq - SUBLANES:(s + 1) * t_seq, :]
    else:
        first = (i % bps) == 0
        prev_g = jnp.where(first, sg_ref[0], cg_scr[SUBLANES - 2:SUBLANES, :])
        prev_v = jnp.where(first, sv_ref[0], cv_scr[SUBLANES - 2:SUBLANES, :])
        nxt = products(0)
        for k in range(n_sub):
            ug, uv = nxt
            if k + 1 < n_sub:
                nxt = products((k + 1) * sub)
            epilogue(k * sub, ug, uv, [(0, prev_g)], [(0, prev_v)])
            prev_g, prev_v = ug[sub - 2:sub, :], uv[sub - 2:sub, :]
        cg_scr[...] = ug[sub - SUBLANES:sub, :]
        cv_scr[...] = uv[sub - SUBLANES:sub, :]
        og_ref[0] = ug[sub - SUBLANES:sub, :]
        ov_ref[0] = uv[sub - SUBLANES:sub, :]


def _ffn_up(h_p, conv_prev_p, seq_p, tm_p, h_s, conv_prev_s, seq_s, w_up, conv_w, conv_b, w_down):
    n_p, n_s = h_p.shape[0], h_s.shape[0]
    batch_p, batch_s = n_p // seq_p, n_s // seq_s
    bps = seq_p // tm_p
    nrb = n_p // tm_p
    slab = D_FF // (E_NCB * nrb)
    last = lambda i: jnp.maximum(i - 1, 0)

    def kern(hp_ref, sgp_ref, svp_ref, hs_ref, sgs_ref, svs_ref, wg_ref, wv_ref, cwg_ref, cwv_ref, cbg_ref, cbv_ref,
             wd_ref, actp_ref, ogp_ref, ovp_ref, acts_ref, ogs_ref, ovs_ref, wdb_ref, w_scr, cg_scr, cv_scr):
        i = pl.program_id(1)

        @pl.when(i == 0)
        def _():
            w_scr[:, :E_TN] = wg_ref[...].astype(BF16)
            w_scr[:, E_TN:] = wv_ref[...].astype(BF16)
            _ffn_up_body(1, batch_s, i, hs_ref, w_scr, cwg_ref, cwv_ref, cbg_ref, cbv_ref, sgs_ref, svs_ref,
                         acts_ref, ogs_ref, ovs_ref, cg_scr, cv_scr)

        @pl.when(i > 0)
        def _():
            wdb_ref[...] = wd_ref[...].astype(BF16)
            _ffn_up_body(bps, 1, i - 1, hp_ref, w_scr, cwg_ref, cwv_ref, cbg_ref, cbv_ref, sgp_ref, svp_ref,
                         actp_ref, ogp_ref, ovp_ref, cg_scr, cv_scr)

    col = lambda shape, off: pl.BlockSpec(shape, lambda j, i: (0,) * (len(shape) - 1) + (j + off,))
    st_p = lambda off: pl.BlockSpec((1, CONV_W - 1, E_TN), lambda j, i: (last(i) // bps, 0, j + off))
    st_s = lambda off: pl.BlockSpec((batch_s, CONV_W - 1, E_TN), lambda j, i: (0, 0, j + off))
    tail_p = pl.BlockSpec((1, SUBLANES, E_TN), lambda j, i: (last(i) // bps, 0, j))
    tail_s = pl.BlockSpec((batch_s, SUBLANES, E_TN), lambda j, i: (0, 0, j))
    wd_spec = pl.BlockSpec((slab, D_MODEL), lambda j, i: (j * nrb + last(i), 0))
    return pl.pallas_call(
        kern,
        grid=(E_NCB, nrb + 1),
        in_specs=[
            pl.BlockSpec((tm_p, D_MODEL), lambda j, i: (last(i), 0)), st_p(0), st_p(E_NCB),
            pl.BlockSpec((n_s, D_MODEL), lambda j, i: (0, 0)), st_s(0), st_s(E_NCB),
            col((D_MODEL, E_TN), 0), col((D_MODEL, E_TN), E_NCB),
            col((CONV_W, E_TN), 0), col((CONV_W, E_TN), E_NCB),
            col((1, E_TN), 0), col((1, E_TN), E_NCB),
            wd_spec,
        ],
        out_specs=[
            pl.BlockSpec((tm_p, E_TN), lambda j, i: (last(i), j)), tail_p, tail_p,
            pl.BlockSpec((n_s, E_TN), lambda j, i: (0, j)), tail_s, tail_s,
            wd_spec,
        ],
        out_shape=[
            jax.ShapeDtypeStruct((n_p, D_FF), BF16),
            jax.ShapeDtypeStruct((batch_p, SUBLANES, D_FF), F32), jax.ShapeDtypeStruct((batch_p, SUBLANES, D_FF), F32),
            jax.ShapeDtypeStruct((n_s, D_FF), BF16),
            jax.ShapeDtypeStruct((batch_s, SUBLANES, D_FF), F32), jax.ShapeDtypeStruct((batch_s, SUBLANES, D_FF), F32),
            jax.ShapeDtypeStruct((D_FF, D_MODEL), BF16),
        ],
        scratch_shapes=[pltpu.VMEM((D_MODEL, 2 * E_TN), BF16),
                        pltpu.VMEM((SUBLANES, E_TN), F32), pltpu.VMEM((SUBLANES, E_TN), F32)],
        compiler_params=_cparams(("arbitrary", "arbitrary")),
        name="ffn_up",
    )(h_p, conv_prev_p, conv_prev_p, h_s, conv_prev_s, conv_prev_s, w_up, w_up, conv_w, conv_w, conv_b, conv_b, w_down)


F_TN = 512


def _ffn_down_kernel(act_ref, w_ref, h1_ref, g_ref, b_ref, y_ref):
    for c0 in range(0, D_MODEL, F_TN):
        f = jnp.dot(act_ref[...], w_ref[:, c0:c0 + F_TN], preferred_element_type=F32)
        y_ref[:, c0:c0 + F_TN] = ALPHA * h1_ref[:, c0:c0 + F_TN] + f
    y_ref[...] = _layer_norm(y_ref[...], g_ref[...], b_ref[...])


def _ffn_down(act_p, h1_p, act_s, h1_s, w_down, g, b, tm):
    n_p, n_s = act_p.shape[0], act_s.shape[0]
    nrb = n_p // tm
    last = lambda i: jnp.minimum(i, nrb - 1)

    def kern(ap_ref, hp_ref, as_ref, hs_ref, w_ref, g_ref, b_ref, yp_ref, ys_ref):
        i = pl.program_id(0)
        pl.when(i < nrb)(lambda: _ffn_down_kernel(ap_ref, w_ref, hp_ref, g_ref, b_ref, yp_ref))
        pl.when(i == nrb)(lambda: _ffn_down_kernel(as_ref, w_ref, hs_ref, g_ref, b_ref, ys_ref))

    rows_p = lambda w: pl.BlockSpec((tm, w), lambda i: (last(i), 0))
    rows_s = lambda w: pl.BlockSpec((n_s, w), lambda i: (0, 0))
    return pl.pallas_call(
        kern,
        grid=(nrb + 1,),
        in_specs=[
            rows_p(D_FF), rows_p(D_MODEL), rows_s(D_FF), rows_s(D_MODEL),
            pl.BlockSpec((D_FF, D_MODEL), lambda i: (0, 0), pipeline_mode=pl.Buffered(1)),
            pl.BlockSpec((1, D_MODEL), lambda i: (0, 0)),
            pl.BlockSpec((1, D_MODEL), lambda i: (0, 0)),
        ],
        out_specs=[rows_p(D_MODEL), rows_s(D_MODEL)],
        out_shape=[jax.ShapeDtypeStruct((n_p, D_MODEL), F32), jax.ShapeDtypeStruct((n_s, D_MODEL), F32)],
        compiler_params=_cparams(("arbitrary",)),
        name="ffn_down",
    )(act_p, h1_p, act_s, h1_s, w_down, g, b)


def _rope_tables(pos, reps):
    half = HEAD_DIM // 2
    inv = ROPE_THETA ** (-jnp.arange(half, dtype=F32) / half)
    ang = pos.astype(F32)[:, None] * inv[None, :]
    cos, sin = jnp.cos(ang), jnp.sin(ang)
    cos128 = jnp.concatenate([cos, cos, cos, cos], axis=1)
    sin128 = jnp.concatenate([-sin, sin, -sin, sin], axis=1)
    return jnp.tile(cos128, (reps, 1)), jnp.tile(sin128, (reps, 1))


TM = 256
D_TM = 256
E_TM = 1024
F_TM = 256
RWKV_NB = 4
ATTN_NB = 2


def kernel(x_prompt, x_sample, cache_k, cache_v, state_wkv, state_shift, state_ffn_conv, ln_in_g, ln_in_b, w_in, attn_sinks, rw_mu, rw_w0, rw_w2, rw_a0, rw_a2, rw_g2, rw_k_k, rw_k_a, rw_r_k, rw_lnx_g, rw_lnx_b, w_out, ln1_g, ln1_b, ffn_w_up, ffn_conv_w, ffn_conv_b, ffn_w_down, ln2_g, ln2_b):
    l = 0
    seg = lax.broadcasted_iota(jnp.int32, (SEG, SEG), 0) // HEAD_DIM
    bd = (seg == seg.T).astype(BF16)
    row = lambda a: a.reshape(1, -1)
    rw_prm = {"mu": row(rw_mu[l]), "w0": row(rw_w0[l]), "a0": row(rw_a0[l]), "k_k": row(rw_k_k[l]),
              "k_a": row(rw_k_a[l]), "r_k": row(rw_r_k[l]), "lnx_g": row(rw_lnx_g[l]), "lnx_b": row(rw_lnx_b[l]),
              "w2": rw_w2[l].astype(BF16), "a2": rw_a2[l].astype(BF16), "g2": rw_g2[l].astype(BF16), "bd": bd}
    sinks = attn_sinks[l]
    bp, tp = x_prompt.shape[0], x_prompt.shape[1]
    bs, ts = x_sample.shape[0], x_sample.shape[1]
    n_p, n_s = bp * tp, bs * ts
    zeros = lambda *s: jnp.zeros(s, F32)

    p_p, h_p, tails_p, p_s, h_s, tails_s = _in_proj(
        x_prompt.reshape(n_p, D_MODEL), x_sample.reshape(n_s, D_MODEL), row(ln_in_g), row(ln_in_b),
        w_in[l].astype(BF16), _rope_tables(jnp.arange(tp, dtype=jnp.int32), 1),
        _rope_tables(PAST_LEN + jnp.arange(ts, dtype=jnp.int32), bs), TM, tp, ts)
    p3_p, p3_s = p_p.reshape(bp, tp, P_COLS), p_s.reshape(bs, ts, P_COLS)

    att_p = _attn_prompt(p3_p, sinks, ATTN_NB).reshape(n_p, ATT_WIDTH)
    att_s, s_k, s_v = _attn_sample(p_s, sinks, cache_k[l].reshape(bs * WINDOW, KV_WIDTH),
                                   cache_v[l].reshape(bs * WINDOW, KV_WIDTH), bs, ts)
    rw_p, p_wkv = _rwkv(p3_p, zeros(bp, 1, RW_COLS), zeros(bp, RW_HEADS, HEAD_DIM, HEAD_DIM), rw_prm, CHUNK, RWKV_NB)
    rw_s, s_wkv = _rwkv(p3_s, state_shift[l], state_wkv[l], rw_prm, ts, RWKV_NB)

    h1_p, h1b_p, h1_s, h1b_s = _out_proj(att_p, rw_p.reshape(n_p, RW_WIDTH), h_p, att_s, rw_s.reshape(n_s, RW_WIDTH), h_s,
                                         w_out[l], row(ln1_g[l]), row(ln1_b[l]), D_TM)
    act_p, cg_p, cv_p, act_s, cg_s, cv_s, w_down_b = _ffn_up(
        h1b_p, zeros(bp, CONV_W - 1, 2 * D_FF), tp, E_TM, h1b_s, state_ffn_conv[l], ts,
        ffn_w_up[l], ffn_conv_w[l], row(ffn_conv_b[l]), ffn_w_down[l])
    y_p, y_s = _ffn_down(act_p, h1_p, act_s, h1_s, w_down_b, row(ln2_g[l]), row(ln2_b[l]), F_TM)

    kv_shape = lambda b: (1, b, WINDOW, ATT_KV_HEADS, HEAD_DIM)
    p_k = p3_p[:, tp - WINDOW:, P_AK:P_AK + KV_WIDTH].reshape(kv_shape(bp))
    p_v = p3_p[:, tp - WINDOW:, P_AV:P_AV + KV_WIDTH].reshape(kv_shape(bp))
    shift_of = lambda tails, b: tails.reshape(b, -1, RW_COLS)[:, -1:, :][None]
    conv_of = lambda cg, cv: jnp.concatenate([cg[:, SUBLANES - 2:, :], cv[:, SUBLANES - 2:, :]], axis=-1)[None]
    return (y_p.reshape(bp, tp, D_MODEL), y_s.reshape(bs, ts, D_MODEL), p_k, p_v, p_wkv[None],
            shift_of(tails_p, bp), conv_of(cg_p, cv_p),
            s_k.reshape(kv_shape(bs)), s_v.reshape(kv_shape(bs)), s_wkv[None],
            shift_of(tails_s, bs), conv_of(cg_s, cv_s))
```

```python
import functools
import math

import jax
import jax.numpy as jnp
from jax import lax
from jax.experimental import pallas as pl
from jax.experimental.pallas import tpu as pltpu

D_MODEL = 2048
HEAD_DIM = 64
ATT_HEADS = 16
ATT_KV_HEADS = 2
ATT_GROUP = ATT_HEADS // ATT_KV_HEADS
ATT_WIDTH = ATT_HEADS * HEAD_DIM
KV_WIDTH = ATT_KV_HEADS * HEAD_DIM
CHUNK = 64
WINDOW = 128
ROPE_THETA = 10000.0
ATT_SCALE = HEAD_DIM ** -0.5
RW_HEADS = 16
RW_WIDTH = RW_HEADS * HEAD_DIM
DECAY_LORA = 64
AAA_LORA = 64
GATE_LORA = 160
LNX_EPS = 64e-5
ATT_COLS = ATT_WIDTH + 2 * KV_WIDTH
RW_COLS = 3 * RW_WIDTH + DECAY_LORA + AAA_LORA + GATE_LORA
D_FF = 5632
CONV_W = 3
LN_EPS = 1e-5
DEPTH = 1
ALPHA = (2 * DEPTH) ** 0.25
PAST_LEN = 1024

LORA_W = 512
P_R, P_K, P_V, P_Q = 0, 1024, 2048, 3072
P_LORA = 4096
P_AK = P_LORA + LORA_W
P_AV = P_AK + KV_WIDTH
P_COLS = P_AV + KV_WIDTH
LANES = 128
SUBLANES = 8
VMEM_LIMIT = 56 * 1024 * 1024

F32 = jnp.float32
BF16 = jnp.bfloat16


def _cparams(sem):
    return pltpu.CompilerParams(dimension_semantics=sem, vmem_limit_bytes=VMEM_LIMIT)


def _layer_norm(y, g, b):
    mu = jnp.mean(y, -1, keepdims=True)
    d = y - mu
    var = jnp.mean(d * d, -1, keepdims=True)
    return d * lax.rsqrt(var + LN_EPS) * g + b


A_TN = 512


def _rope128(x, cos, sin_signed):
    lane = lax.broadcasted_iota(jnp.int32, x.shape, 1)
    first_half = (lane % HEAD_DIM) < (HEAD_DIM // 2)
    rot = jnp.where(first_half, pltpu.roll(x, LANES - HEAD_DIM // 2, 1), pltpu.roll(x, HEAD_DIM // 2, 1))
    return x * cos + rot * sin_signed


def _is_rope_col(col):
    return (P_Q <= col < P_Q + ATT_WIDTH) or (P_AK <= col < P_AK + KV_WIDTH)


O_R = ATT_COLS
O_WD = O_R + RW_WIDTH
O_K = O_WD + DECAY_LORA
O_V = O_K + RW_WIDTH
O_AD = O_V + RW_WIDTH
O_GD = O_AD + AAA_LORA
IN_COLS = O_GD + GATE_LORA


def _in_proj_kernel(x_ref, g_ref, b_ref, w_ref, cos_ref, sin_ref, p_ref, h_ref, tail_ref, hb_ref):
    h = _layer_norm(x_ref[...], g_ref[...], b_ref[...])
    h_ref[...] = h
    hb_ref[...] = h.astype(BF16)
    nseq = tail_ref.shape[0]
    t_seq = x_ref.shape[0] // nseq

    def product(c0, width):
        acc = jnp.dot(hb_ref[...], w_ref[:, c0:c0 + width], preferred_element_type=F32)
        if c0 >= O_R:
            for s in range(nseq):
                tail_ref[s, :, c0 - O_R:c0 - O_R + width] = acc[(s + 1) * t_seq - SUBLANES:(s + 1) * t_seq, :]
        return acc

    for src, width, dst in ((0, ATT_WIDTH, P_Q), (ATT_WIDTH, 2 * KV_WIDTH, P_AK), (O_R, RW_WIDTH, P_R)):
        for c0 in range(0, width, A_TN):
            acc = product(src + c0, min(A_TN, width - c0))
            for l0 in range(0, acc.shape[1], LANES):
                col = dst + c0 + l0
                part = acc[:, l0:l0 + LANES]
                if _is_rope_col(col):
                    part = _rope128(part, cos_ref[...], sin_ref[...])
                p_ref[:, col:col + LANES] = part

    low = lax.broadcasted_iota(jnp.int32, (x_ref.shape[0], LANES), 1) < HEAD_DIM
    span = O_GD - O_WD
    rolled, first_piece = [], None
    for c0 in range(0, span, A_TN):
        acc = product(O_WD + c0, min(A_TN, span - c0))
        for l0 in range(0, acc.shape[1], LANES):
            piece = acc[:, l0:l0 + LANES]
            j = (c0 + l0) // LANES
            if j == 0:
                first_piece = piece
            rolled.append(pltpu.roll(piece, HEAD_DIM, 1))
            if j >= 1:
                m = j - 1
                dst = P_K + m * LANES if m < RW_WIDTH // LANES else P_V + (m - RW_WIDTH // LANES) * LANES
                p_ref[:, dst:dst + LANES] = jnp.where(low, rolled[m], rolled[m + 1])
            if j == span // LANES - 1:
                p_ref[:, P_LORA:P_LORA + LANES] = jnp.where(low, first_piece, piece)
    gd0 = P_LORA + DECAY_LORA + AAA_LORA
    p_ref[:, gd0:gd0 + GATE_LORA] = product(O_GD, GATE_LORA)
    p_ref[:, gd0 + GATE_LORA:P_LORA + LORA_W] = jnp.zeros((x_ref.shape[0], LORA_W - DECAY_LORA - AAA_LORA - GATE_LORA), F32)


def _in_proj(xp, xs, g, b, w_p, tables_p, tables_s, tm, seq_p, seq_s):
    n_p, n_s = xp.shape[0], xs.shape[0]
    assert n_s == tm
    nrb = n_p // tm
    nseq_s = tm // seq_s
    tb = seq_p // tm
    last = lambda i: jnp.minimum(i, nrb - 1)

    def body(ins, shared, outs, scratch):
        x_ref, cos_ref, sin_ref = ins
        g_ref, b_ref, w_ref = shared
        _in_proj_kernel(x_ref, g_ref, b_ref, w_ref, cos_ref, sin_ref, *outs, *scratch)

    def kern(xp_ref, cp_ref, sp_ref, xs_ref, cs_ref, ss_ref, g_ref, b_ref, w_ref,
             pp_ref, hp_ref, tp_ref, ps_ref, hs_ref, ts_ref, hb_ref):
        i = pl.program_id(0)
        pl.when(i < nrb)(lambda: body((xp_ref, cp_ref, sp_ref), (g_ref, b_ref, w_ref), (pp_ref, hp_ref, tp_ref), (hb_ref,)))
        pl.when(i == nrb)(lambda: body((xs_ref, cs_ref, ss_ref), (g_ref, b_ref, w_ref), (ps_ref, hs_ref, ts_ref), (hb_ref,)))

    const2 = lambda shape: pl.BlockSpec(shape, lambda i: (0, 0))
    return pl.pallas_call(
        kern,
        grid=(nrb + 1,),
        in_specs=[
            pl.BlockSpec((tm, D_MODEL), lambda i: (last(i), 0)),
            pl.BlockSpec((tm, LANES), lambda i: (last(i) % tb, 0)),
            pl.BlockSpec((tm, LANES), lambda i: (last(i) % tb, 0)),
            const2((tm, D_MODEL)), const2((tm, LANES)), const2((tm, LANES)),
            const2((1, D_MODEL)), const2((1, D_MODEL)),
            pl.BlockSpec((D_MODEL, IN_COLS), lambda i: (0, 0), pipeline_mode=pl.Buffered(1)),
        ],
        out_specs=[
            pl.BlockSpec((tm, P_COLS), lambda i: (last(i), 0)),
            pl.BlockSpec((tm, D_MODEL), lambda i: (last(i), 0)),
            pl.BlockSpec((1, SUBLANES, RW_COLS), lambda i: (last(i), 0, 0)),
            const2((tm, P_COLS)), const2((tm, D_MODEL)),
            pl.BlockSpec((nseq_s, SUBLANES, RW_COLS), lambda i: (0, 0, 0)),
        ],
        out_shape=[
            jax.ShapeDtypeStruct((n_p, P_COLS), F32),
            jax.ShapeDtypeStruct((n_p, D_MODEL), F32),
            jax.ShapeDtypeStruct((nrb, SUBLANES, RW_COLS), F32),
            jax.ShapeDtypeStruct((n_s, P_COLS), F32),
            jax.ShapeDtypeStruct((n_s, D_MODEL), F32),
            jax.ShapeDtypeStruct((nseq_s, SUBLANES, RW_COLS), F32),
        ],
        scratch_shapes=[pltpu.VMEM((tm, D_MODEL), BF16)],
        compiler_params=_cparams(("arbitrary",)),
        name="in_proj",
    )(xp, *tables_p, xs, *tables_s, g, b, w_p)


def _attend(q, k, v, valid, sink_ref):
    r = q.shape[0]
    outs = []
    for kv in range(ATT_KV_HEADS):
        kh = k[:, kv * HEAD_DIM:(kv + 1) * HEAD_DIM].astype(BF16)
        vh = v[:, kv * HEAD_DIM:(kv + 1) * HEAD_DIM].astype(BF16)
        heads = [kv * ATT_GROUP + g for g in range(ATT_GROUP)]
        qs = jnp.concatenate([q[:, h * HEAD_DIM:(h + 1) * HEAD_DIM] for h in heads], axis=0)
        sink = jnp.concatenate([jnp.full((r, 1), sink_ref[h], F32) for h in heads], axis=0)
        s = lax.dot_general(qs.astype(BF16), kh, (((1,), (1,)), ((), ())), preferred_element_type=F32) * ATT_SCALE
        if valid is not None:
            s = jnp.where(valid, s, -jnp.inf)
        m = jnp.maximum(jnp.max(s, -1, keepdims=True), sink)
        p = jnp.exp(s - m)
        den = jnp.sum(p, -1, keepdims=True) + jnp.exp(sink - m)
        o = jnp.dot(p.astype(BF16), vh, preferred_element_type=F32) / den
        outs.extend(o[g * r:(g + 1) * r] for g in range(ATT_GROUP))
    return jnp.concatenate(outs, axis=1)


LOG2_E = math.log2(math.e)
QB = 2 * CHUNK


def _attn_prompt_kernel(sink_ref, q_ref, kp_ref, kc_ref, vp_ref, vc_ref, o_ref):
    m = pl.program_id(1)
    nseq = q_ref.shape[0]
    span = 2 * QB
    lane = lax.broadcasted_iota(jnp.int32, (span, KV_WIDTH), 1)
    row_c = 2 + lax.broadcasted_iota(jnp.int32, (QB, span), 0) // CHUNK
    col_c = lax.broadcasted_iota(jnp.int32, (QB, span), 1) // CHUNK
    back = row_c - col_c
    valid = jnp.logical_and(jnp.logical_and(back >= 0, back <= 2), 2 * m - 2 + col_c >= 0)
    out_lane = lax.broadcasted_iota(jnp.int32, (QB, PAIR), 1)

    def dup(x, kv):
        sw = pltpu.roll(x, HEAD_DIM, 1)
        return jnp.where(lane < HEAD_DIM, x, sw) if kv == 0 else jnp.where(lane < HEAD_DIM, sw, x)

    k_st, v_st = [], []
    for b in range(nseq):
        k_all = jnp.concatenate([kp_ref[b], kc_ref[b]], axis=0)
        v_all = jnp.concatenate([vp_ref[b], vc_ref[b]], axis=0)
        k_st.append([_stack_heads(dup(k_all, kv), HEAD_DIM).astype(BF16) for kv in range(ATT_KV_HEADS)])
        v_st.append([_stack_heads(dup(v_all, kv), HEAD_DIM).astype(BF16) for kv in range(ATT_KV_HEADS)])
    pairs = [(b, j) for b in range(nseq) for j in range(ATT_HEADS // 2)]
    kv_of = lambda j: (2 * j) // ATT_GROUP
    s = [lax.dot_general((q_ref[b, :, j * PAIR:(j + 1) * PAIR] * (ATT_SCALE * LOG2_E)).astype(BF16), k_st[b][kv_of(j)],
                         (((1,), (1,)), ((), ())), preferred_element_type=F32) for b, j in pairs]
    probs, dens = [], []
    for idx, (b, j) in enumerate(pairs):
        halves, den = [], []
        for e in range(2):
            sink = sink_ref[2 * j + e] * LOG2_E
            sh = jnp.where(valid, s[idx][:, e * span:(e + 1) * span], -jnp.inf)
            mx = jnp.maximum(jnp.max(sh, -1, keepdims=True), sink)
            ph = jnp.exp2(sh - mx)
            halves.append(ph)
            den.append(jnp.sum(ph, -1, keepdims=True) + jnp.exp2(sink - mx))
        probs.append(jnp.concatenate(halves, axis=1).astype(BF16))
        dens.append(jnp.where(out_lane < HEAD_DIM, den[0], den[1]))
    o = [jnp.dot(probs[idx], v_st[b][kv_of(j)], preferred_element_type=F32) / dens[idx]
         for idx, (b, j) in enumerate(pairs)]
    n_pairs = ATT_HEADS // 2
    for b in range(nseq):
        o_ref[b] = jnp.concatenate(o[b * n_pairs:(b + 1) * n_pairs], axis=1).astype(o_ref.dtype)


def _attn_prompt(p3, sinks, nseq):
    batch, seq = p3.shape[0], p3.shape[1]
    nb = seq // QB
    kcol, vcol = P_AK // KV_WIDTH, P_AV // KV_WIDTH
    prev = lambda col: pl.BlockSpec((nseq, QB, KV_WIDTH), lambda g, m: (g, jnp.maximum(m - 1, 0), col))
    cur = lambda col: pl.BlockSpec((nseq, QB, KV_WIDTH), lambda g, m: (g, m, col))
    return pl.pallas_call(
        _attn_prompt_kernel,
        grid=(batch // nseq, nb),
        in_specs=[
            pl.BlockSpec(memory_space=pltpu.SMEM),
            pl.BlockSpec((nseq, QB, ATT_WIDTH), lambda g, m: (g, m, P_Q // ATT_WIDTH)),
            prev(kcol), cur(kcol), prev(vcol), cur(vcol),
        ],
        out_specs=pl.BlockSpec((nseq, QB, ATT_WIDTH), lambda g, m: (g, m, 0)),
        out_shape=jax.ShapeDtypeStruct((batch, seq, ATT_WIDTH), BF16),
        compiler_params=_cparams(("arbitrary", "arbitrary")),
        name="attn_prompt",
    )(sinks, p3, p3, p3, p3, p3)


def _attn_sample_kernel(sink_ref, q_ref, kn_ref, vn_ref, kc_ref, vc_ref, o_ref, kw_ref, vw_ref):
    t = q_ref.shape[0]
    k = jnp.concatenate([kc_ref[...], kn_ref[...]], axis=0)
    v = jnp.concatenate([vc_ref[...], vn_ref[...]], axis=0)
    o_ref[...] = _attend(q_ref[...], k, v, None, sink_ref).astype(o_ref.dtype)
    kw_ref[...] = k[t:]
    vw_ref[...] = v[t:]


def _attn_sample(p, sinks, cache_k, cache_v, batch, seq):
    kcol, vcol = P_AK // KV_WIDTH, P_AV // KV_WIDTH
    return pl.pallas_call(
        _attn_sample_kernel,
        grid=(batch,),
        in_specs=[
            pl.BlockSpec(memory_space=pltpu.SMEM),
            pl.BlockSpec((seq, ATT_WIDTH), lambda b: (b, P_Q // ATT_WIDTH)),
            pl.BlockSpec((seq, KV_WIDTH), lambda b: (b, kcol)),
            pl.BlockSpec((seq, KV_WIDTH), lambda b: (b, vcol)),
            pl.BlockSpec((WINDOW, KV_WIDTH), lambda b: (b, 0)),
            pl.BlockSpec((WINDOW, KV_WIDTH), lambda b: (b, 0)),
        ],
        out_specs=[
            pl.BlockSpec((seq, ATT_WIDTH), lambda b: (b, 0)),
            pl.BlockSpec((WINDOW, KV_WIDTH), lambda b: (b, 0)),
            pl.BlockSpec((WINDOW, KV_WIDTH), lambda b: (b, 0)),
        ],
        out_shape=[
            jax.ShapeDtypeStruct((batch * seq, ATT_WIDTH), BF16),
            jax.ShapeDtypeStruct((batch * WINDOW, KV_WIDTH), F32),
            jax.ShapeDtypeStruct((batch * WINDOW, KV_WIDTH), F32),
        ],
        compiler_params=_cparams(("arbitrary",)),
        name="attn_sample",
    )(sinks, p, p, p, cache_k, cache_v)


SEG = 256
EXP_NEG_HALF = math.exp(-0.5)
PAIR = 2 * HEAD_DIM
N_PAIRS = RW_HEADS // 2


def _split2(x):
    hi = x.astype(BF16)
    lo = (x - hi.astype(F32)).astype(BF16)
    return hi, lo


def _seg_sum(x, bd):
    c = x.shape[0]
    xb = x.astype(BF16)
    stack = jnp.concatenate([xb[:, s:s + SEG] for s in range(0, RW_WIDTH, SEG)], axis=0)
    r = jnp.dot(stack, bd, preferred_element_type=F32)
    return jnp.concatenate([r[i * c:(i + 1) * c] for i in range(RW_WIDTH // SEG)], axis=1)


def _shift_rows(x, prev_row):
    row = lax.broadcasted_iota(jnp.int32, x.shape, 0)
    return jnp.where(row == 0, prev_row, pltpu.roll(x, 1, 0))


def _dotb(a, b):
    return jnp.dot(a.astype(BF16), b.astype(BF16), preferred_element_type=F32)


def _dotb_nt(a, b):
    return lax.dot_general(a.astype(BF16), b.astype(BF16), (((1,), (1,)), ((), ())), preferred_element_type=F32)


def _dot3_tn(a, b):
    ah, al = _split2(a)
    bh, bl = _split2(b)
    return lax.dot_general(jnp.concatenate([ah, al, ah], axis=0), jnp.concatenate([bh, bh, bl], axis=0),
                           (((0,), (0,)), ((), ())), preferred_element_type=F32)


def _dot2_tn(a, b):
    ah, al = _split2(a)
    bh = b.astype(BF16)
    return lax.dot_general(jnp.concatenate([ah, al], axis=0), jnp.concatenate([bh, bh], axis=0),
                           (((0,), (0,)), ((), ())), preferred_element_type=F32)


def _dot2(a, b):
    ah, al = _split2(a)
    bh = b.astype(BF16)
    return jnp.dot(jnp.concatenate([ah, al], axis=1), jnp.concatenate([bh, bh], axis=0), preferred_element_type=F32)


def _stack_heads(x, half):
    lane = lax.broadcasted_iota(jnp.int32, x.shape, 1)
    first = (lane % (2 * half)) < half
    zero = jnp.zeros_like(x)
    return jnp.concatenate([jnp.where(first, x, zero), jnp.where(first, zero, x)], axis=0)


def _stack_heads_bf16(x, half):
    return _stack_heads(x.astype(BF16), half)


def _rwkv_pairs(ab, rb, bb, kb, v, gam_c, s_bd, n_double):
    c = ab[0].shape[0]
    pairs = range(len(ab))
    t_idx = lax.broadcasted_iota(jnp.int32, (c, 2 * c), 0)
    s_idx = lax.broadcasted_iota(jnp.int32, (c, 2 * c), 1) % c
    strict, incl = s_idx < t_idx, s_idx <= t_idx
    g = [_dotb_nt(jnp.concatenate([ab[p], rb[p]], axis=0),
                  jnp.concatenate([_stack_heads_bf16(bb[p], HEAD_DIM), _stack_heads_bf16(kb[p], HEAD_DIM)], axis=0))
         for p in pairs]
    l_ak = [jnp.where(strict, g[p][:c, 2 * c:], 0.0) for p in pairs]
    m_rbk = [jnp.concatenate([jnp.where(incl, g[p][c:, :2 * c], 0.0), jnp.where(incl, g[p][c:, 2 * c:], 0.0)], axis=1)
             for p in pairs]
    lp = [jnp.where(strict, g[p][:c, :2 * c], 0.0) for p in pairs]
    v_st = [_stack_heads_bf16(v[p], HEAD_DIM) for p in pairs]
    y0 = [jnp.concatenate([ab[p], _dotb(l_ak[p], v_st[p])], axis=1) for p in pairs]
    n_acc = list(lp)
    lk = [_dotb(lp[p], _stack_heads_bf16(lp[p], c)) for p in pairs]
    for k in range(1, n_double):
        n_st = [_stack_heads_bf16(n_acc[p], c) for p in pairs]
        if k + 1 < n_double:
            z = [_dotb(lk[p], jnp.concatenate([n_st[p], _stack_heads_bf16(lk[p], c)], axis=1)) for p in pairs]
            n_acc = [n_acc[p] + lk[p] + z[p][:, :2 * c] for p in pairs]
            lk = [z[p][:, 2 * c:] for p in pairs]
        else:
            n_acc = [n_acc[p] + lk[p] + _dotb(lk[p], n_st[p]) for p in pairs]
    y = [y0[p] + _dotb(n_acc[p], _stack_heads_bf16(y0[p], HEAD_DIM)) for p in pairs]
    ya = [y[p][:, :PAIR] for p in pairs]
    yu = [y[p][:, PAIR:] for p in pairs]
    z = []
    for p in pairs:
        top = _stack_heads_bf16(jnp.concatenate([ya[p], yu[p]], axis=1), HEAD_DIM)
        bot = jnp.concatenate([jnp.zeros_like(v_st[p]), v_st[p]], axis=1)
        z.append(_dotb(m_rbk[p], jnp.concatenate([top, bot], axis=0)))
    o = [_dotb_nt(rb[p] + z[p][:, :PAIR], s_bd[p]) + z[p][:, PAIR:] for p in pairs]
    row_h = lax.broadcasted_iota(jnp.int32, (PAIR, PAIR), 0) // HEAD_DIM
    col_h = lax.broadcasted_iota(jnp.int32, (PAIR, PAIR), 1) // HEAD_DIM
    same = row_h == col_h
    q_t = [jnp.where(same, _dot3_tn(ya[p], bb[p]), 0.0) for p in pairs]
    d_t = [jnp.where(same, _dot2_tn(jnp.concatenate([yu[p], v[p]], axis=0),
                                    jnp.concatenate([bb[p], kb[p]], axis=0)), 0.0) for p in pairs]
    s_new = [(s_bd[p] + _dot2(s_bd[p], q_t[p]) + d_t[p]) * gam_c[p] for p in pairs]
    return o, s_new


def _split_rw_rows(x):
    o_wd, o_gd = RW_WIDTH, RW_COLS - GATE_LORA
    low = lax.broadcasted_iota(jnp.int32, (x.shape[0], LANES), 1) < HEAD_DIM
    pieces = [x[:, c0:c0 + LANES] for c0 in range(o_wd, o_gd, LANES)]
    rolled = [pltpu.roll(p, HEAD_DIM, 1) for p in pieces]
    merged = [jnp.where(low, rolled[m], rolled[m + 1]) for m in range(len(pieces) - 1)]
    n = RW_WIDTH // LANES
    return (x[:, :RW_WIDTH], jnp.concatenate(merged[:n], axis=1), jnp.concatenate(merged[n:], axis=1),
            jnp.where(low, pieces[0], pieces[-1]), x[:, o_gd:])


def _rwkv_kernel(xr_ref, xk_ref, xv_ref, xl_ref, sh_ref, mu_ref, w0_ref, a0_ref, kk_ref, ka_ref, rk_ref,
                 lg_ref, lb_ref, w2_ref, a2_ref, g2_ref, bd_ref, st0_ref,
                 out_ref, st_ref,
                 s_scr, pr_scr, pk_scr, pv_scr, pw_scr, pg_scr):
    c = pl.program_id(1)
    nc = pl.num_programs(1)
    nb, C = xr_ref.shape[0], xr_ref.shape[1]
    n_double = int(math.log2(C))
    prev_scr = (pr_scr, pk_scr, pv_scr, pw_scr, pg_scr)

    @pl.when(c == 0)
    def _():
        s_scr[...] = jnp.zeros_like(s_scr)
        for b in range(nb):
            for h in range(RW_HEADS):
                off = (h % 2) * HEAD_DIM
                s_scr[b * N_PAIRS + h // 2, off:off + HEAD_DIM, off:off + HEAD_DIM] = st0_ref[b, h]
            for scr, row in zip(prev_scr, _split_rw_rows(sh_ref[b])):
                scr[b, 0:1, :] = row

    def shifted(load, p_scr, mu):
        parts = []
        for b in range(nb):
            x = load(b)
            prev = _shift_rows(x, p_scr[b, 0:1, :])
            p_scr[b, 0:1, :] = x[C - 1:C, :]
            parts.append(x + (prev - x) * mu)
        return jnp.concatenate(parts, axis=0)

    mu_r, mu_k, mu_v, mu_w, mu_g = _split_rw_rows(mu_ref[...])
    wa_w = DECAY_LORA + AAA_LORA
    r = shifted(lambda b: xr_ref[b], pr_scr, mu_r)
    kr = shifted(lambda b: xk_ref[b], pk_scr, mu_k)
    v = shifted(lambda b: xv_ref[b], pv_scr, mu_v)
    xw = shifted(lambda b: xl_ref[b, :, :wa_w], pw_scr, mu_w)
    gd = shifted(lambda b: xl_ref[b, :, wa_w:wa_w + GATE_LORA], pg_scr, mu_g)
    wd = xw[:, 0:DECAY_LORA]
    ad = xw[:, DECAY_LORA:wa_w]

    lw = -EXP_NEG_HALF * jax.nn.sigmoid(w0_ref[...] + _dotb(jnp.tanh(wd), w2_ref[...]))
    a = jax.nn.sigmoid(a0_ref[...] + _dotb(ad, a2_ref[...]))
    g = _dotb(jax.nn.sigmoid(gd), g2_ref[...])
    bd = bd_ref[...]
    kk = kr * kk_ref[...]
    kk = kk * lax.rsqrt(jnp.maximum(_seg_sum(kk * kk, bd), 1e-24))
    k2 = kr * (1.0 + (a - 1.0) * ka_ref[...])

    rows = nb * C
    ti = lax.broadcasted_iota(jnp.int32, (rows, rows), 0)
    si = lax.broadcasted_iota(jnp.int32, (rows, rows), 1)
    tri = jnp.logical_and(si <= ti, si // C == ti // C).astype(BF16)
    cs2 = jnp.dot(tri, jnp.concatenate(_split2(lw), axis=1), preferred_element_type=F32)
    cs = cs2[:, :RW_WIDTH] + cs2[:, RW_WIDTH:]
    gam = jnp.exp(cs)
    ginv = jnp.exp(-cs)
    ab = -kk * jnp.exp(cs - lw)
    rb = r * gam
    bb = kk * a * ginv
    kb = k2 * ginv

    def per_pair(x, r0, r1):
        return [x[b * C + r0:b * C + r1, p * PAIR:(p + 1) * PAIR] for b in range(nb) for p in range(N_PAIRS)]

    chains = nb * N_PAIRS
    outs, s_new = _rwkv_pairs(per_pair(ab, 0, C), per_pair(rb, 0, C), per_pair(bb, 0, C), per_pair(kb, 0, C),
                              per_pair(v, 0, C), per_pair(gam, C - 1, C), [s_scr[q] for q in range(chains)], n_double)
    for q in range(chains):
        s_scr[q] = s_new[q]
    o = jnp.concatenate([jnp.concatenate(outs[b * N_PAIRS:(b + 1) * N_PAIRS], axis=1) for b in range(nb)], axis=0)

    mo = _seg_sum(o, bd) * (1.0 / HEAD_DIM)
    d = o - mo
    vo = _seg_sum(d * d, bd) * (1.0 / HEAD_DIM)
    on = d * lax.rsqrt(vo + LNX_EPS) * lg_ref[...] + lb_ref[...]
    bonus = _seg_sum(r * k2 * rk_ref[...], bd) * v
    res = ((on + bonus) * g).astype(out_ref.dtype)
    for b in range(nb):
        out_ref[b] = res[b * C:(b + 1) * C, :]

    @pl.when(c == nc - 1)
    def _():
        for b in range(nb):
            for h in range(RW_HEADS):
                off = (h % 2) * HEAD_DIM
                st_ref[b, h] = s_scr[b * N_PAIRS + h // 2, off:off + HEAD_DIM, off:off + HEAD_DIM]


def _rwkv(p3, state_shift, state0, prm, chunk, nb):
    batch, seq = p3.shape[0], p3.shape[1]
    nc = seq // chunk
    row = lambda w: pl.BlockSpec((1, w), lambda g, c: (0, 0))
    xspec = lambda w, col: pl.BlockSpec((nb, chunk, w), lambda g, c: (g, c, col))
    full = lambda a: pl.BlockSpec(a.shape, lambda g, c: (0,) * a.ndim)
    stspec = pl.BlockSpec((nb, RW_HEADS, HEAD_DIM, HEAD_DIM), lambda g, c: (g, 0, 0, 0))
    return pl.pallas_call(
        _rwkv_kernel,
        grid=(batch // nb, nc),
        in_specs=[
            xspec(RW_WIDTH, P_R // RW_WIDTH), xspec(RW_WIDTH, P_K // RW_WIDTH), xspec(RW_WIDTH, P_V // RW_WIDTH),
            xspec(LORA_W, P_LORA // LORA_W),
            pl.BlockSpec((nb, 1, RW_COLS), lambda g, c: (g, 0, 0)),
            row(RW_COLS),
            row(RW_WIDTH), row(RW_WIDTH), row(RW_WIDTH), row(RW_WIDTH), row(RW_WIDTH),
            row(RW_WIDTH), row(RW_WIDTH),
            full(prm["w2"]), full(prm["a2"]), full(prm["g2"]), full(prm["bd"]),
            stspec,
        ],
        out_specs=[
            pl.BlockSpec((nb, chunk, RW_WIDTH), lambda g, c: (g, c, 0)),
            stspec,
        ],
        out_shape=[
            jax.ShapeDtypeStruct((batch, seq, RW_WIDTH), BF16),
            jax.ShapeDtypeStruct((batch, RW_HEADS, HEAD_DIM, HEAD_DIM), F32),
        ],
        scratch_shapes=[
            pltpu.VMEM((nb * N_PAIRS, PAIR, PAIR), F32),
            pltpu.VMEM((nb, SUBLANES, RW_WIDTH), F32), pltpu.VMEM((nb, SUBLANES, RW_WIDTH), F32),
            pltpu.VMEM((nb, SUBLANES, RW_WIDTH), F32),
            pltpu.VMEM((nb, SUBLANES, DECAY_LORA + AAA_LORA), F32), pltpu.VMEM((nb, SUBLANES, GATE_LORA), F32),
        ],
        compiler_params=_cparams(("arbitrary", "arbitrary")),
        name="rwkv",
    )(p3, p3, p3, p3, state_shift, prm["mu"], prm["w0"], prm["a0"], prm["k_k"], prm["k_a"], prm["r_k"],
      prm["lnx_g"], prm["lnx_b"], prm["w2"], prm["a2"], prm["g2"], prm["bd"], state0)


D_TN = 512


def _out_proj_body(att_ref, rw_ref, h_ref, g_ref, b_ref, h1_ref, h1b_ref, w_scr):
    for c0 in range(0, D_MODEL, D_TN):
        m = jnp.dot(att_ref[...], w_scr[:ATT_WIDTH, c0:c0 + D_TN], preferred_element_type=F32)
        m = m + jnp.dot(rw_ref[...], w_scr[ATT_WIDTH:, c0:c0 + D_TN], preferred_element_type=F32)
        h1_ref[:, c0:c0 + D_TN] = ALPHA * h_ref[:, c0:c0 + D_TN] + m
    y = _layer_norm(h1_ref[...], g_ref[...], b_ref[...])
    h1_ref[...] = y
    h1b_ref[...] = y.astype(BF16)


def _out_proj(att_p, rw_p, h_p, att_s, rw_s, h_s, w_out, g, b, tm):
    n_p, n_s = att_p.shape[0], att_s.shape[0]
    nrb = n_p // tm
    last = lambda i: jnp.minimum(i, nrb - 1)

    def kern(ap_ref, rp_ref, hp_ref, as_ref, rs_ref, hs_ref, wa_ref, wb_ref, g_ref, b_ref,
             h1p_ref, h1bp_ref, h1s_ref, h1bs_ref, w_scr):
        i = pl.program_id(0)

        @pl.when(i == 0)
        def _():
            w_scr[:ATT_WIDTH, :] = wa_ref[...].astype(BF16)
            w_scr[ATT_WIDTH:, :] = wb_ref[...].astype(BF16)

        pl.when(i < nrb)(lambda: _out_proj_body(ap_ref, rp_ref, hp_ref, g_ref, b_ref, h1p_ref, h1bp_ref, w_scr))
        pl.when(i == nrb)(lambda: _out_proj_body(as_ref, rs_ref, hs_ref, g_ref, b_ref, h1s_ref, h1bs_ref, w_scr))

    rows_p = lambda w: pl.BlockSpec((tm, w), lambda i: (last(i), 0))
    rows_s = lambda w: pl.BlockSpec((n_s, w), lambda i: (0, 0))
    return pl.pallas_call(
        kern,
        grid=(nrb + 1,),
        in_specs=[
            rows_p(ATT_WIDTH), rows_p(RW_WIDTH), rows_p(D_MODEL),
            rows_s(ATT_WIDTH), rows_s(RW_WIDTH), rows_s(D_MODEL),
            pl.BlockSpec((ATT_WIDTH, D_MODEL), lambda i: (0, 0), pipeline_mode=pl.Buffered(1)),
            pl.BlockSpec((RW_WIDTH, D_MODEL), lambda i: (1, 0), pipeline_mode=pl.Buffered(1)),
            pl.BlockSpec((1, D_MODEL), lambda i: (0, 0)),
            pl.BlockSpec((1, D_MODEL), lambda i: (0, 0)),
        ],
        out_specs=[rows_p(D_MODEL), rows_p(D_MODEL), rows_s(D_MODEL), rows_s(D_MODEL)],
        out_shape=[
            jax.ShapeDtypeStruct((n_p, D_MODEL), F32), jax.ShapeDtypeStruct((n_p, D_MODEL), BF16),
            jax.ShapeDtypeStruct((n_s, D_MODEL), F32), jax.ShapeDtypeStruct((n_s, D_MODEL), BF16),
        ],
        scratch_shapes=[pltpu.VMEM((ATT_WIDTH + RW_WIDTH, D_MODEL), BF16)],
        compiler_params=_cparams(("arbitrary",)),
        name="out_proj",
    )(att_p, rw_p, h_p, att_s, rw_s, h_s, w_out, w_out, g, b)


E_TN = 512
E_NCB = D_FF // E_TN
E_SUB = 1024
SQRT_HALF = math.sqrt(0.5)


def _ffn_up_body(bps, nseq, i, h_ref, w_scr, cwg_ref, cwv_ref, cbg_ref, cbv_ref, sg_ref, sv_ref,
                 act_ref, og_ref, ov_ref, cg_scr, cv_scr):
    tm = h_ref.shape[0]
    sub = min(tm, E_SUB)
    n_sub = tm // sub
    t_seq = tm // nseq
    row8 = lax.broadcasted_iota(jnp.int32, (SUBLANES, E_TN), 0)

    def products(r0):
        u = jnp.dot(h_ref[r0:r0 + sub, :], w_scr[...], preferred_element_type=F32)
        return u[:, :E_TN], u[:, E_TN:]

    def conv(u, prevs, cw_ref, cb_ref):
        sh1, sh2 = pltpu.roll(u, 1, 0), pltpu.roll(u, 2, 0)
        for r, prev in prevs:
            p2, p1 = prev[0:1, :], prev[1:2, :]
            fix1 = jnp.where(row8 == 0, p1, sh1[r:r + SUBLANES, :])
            fix2 = jnp.where(row8 == 0, p2, jnp.where(row8 == 1, p1, sh2[r:r + SUBLANES, :]))
            head = lambda x: [x[:r, :]] if r > 0 else []
            tail = lambda x: [x[r + SUBLANES:, :]] if r + SUBLANES < x.shape[0] else []
            sh1 = jnp.concatenate(head(sh1) + [fix1] + tail(sh1), axis=0)
            sh2 = jnp.concatenate(head(sh2) + [fix2] + tail(sh2), axis=0)
        cw = cw_ref[...]
        c = cb_ref[...] + sh2 * cw[0:1, :]
        c = c + sh1 * cw[1:2, :]
        return c + u * cw[2:3, :]

    def epilogue(r0, ug, uv, prev_g, prev_v):
        gate = conv(ug, prev_g, cwg_ref, cbg_ref)
        val = conv(uv, prev_v, cwv_ref, cbv_ref)
        gelu = 0.5 * gate * (1.0 + lax.erf(gate * SQRT_HALF))
        act_ref[r0:r0 + sub, :] = (gelu * val).astype(act_ref.dtype)

    if nseq > 1:
        ug, uv = products(0)
        epilogue(0, ug, uv, [(s * t_seq, sg_ref[s]) for s in range(nseq)],
                 [(s * t_seq, sv_ref[s]) for s in range(nseq)])
        for s in range(nseq):
            og_ref[s] = ug[(s + 1) * t_seq - SUBLANES:(s + 1) * t_seq, :]
            ov_ref[s] = uv[(s + 1) * t_seq - SUBLANES:(s + 1) * t_seq, :]
    else:
        first = (i % bps) == 0
        prev_g = jnp.where(first, sg_ref[0], cg_scr[SUBLANES - 2:SUBLANES, :])
        prev_v = jnp.where(first, sv_ref[0], cv_scr[SUBLANES - 2:SUBLANES, :])
        nxt = products(0)
        for k in range(n_sub):
            ug, uv = nxt
            if k + 1 < n_sub:
                nxt = products((k + 1) * sub)
            epilogue(k * sub, ug, uv, [(0, prev_g)], [(0, prev_v)])
            prev_g, prev_v = ug[sub - 2:sub, :], uv[sub - 2:sub, :]
        cg_scr[...] = ug[sub - SUBLANES:sub, :]
        cv_scr[...] = uv[sub - SUBLANES:sub, :]
        og_ref[0] = ug[sub - SUBLANES:sub, :]
        ov_ref[0] = uv[sub - SUBLANES:sub, :]


def _ffn_up(h_p, conv_prev_p, seq_p, tm_p, h_s, conv_prev_s, seq_s, w_up, conv_w, conv_b, w_down):
    n_p, n_s = h_p.shape[0], h_s.shape[0]
    batch_p, batch_s = n_p // seq_p, n_s // seq_s
    bps = seq_p // tm_p
    nrb = n_p // tm_p
    slab = D_FF // (E_NCB * nrb)
    last = lambda i: jnp.maximum(i - 1, 0)

    def kern(hp_ref, sgp_ref, svp_ref, hs_ref, sgs_ref, svs_ref, wg_ref, wv_ref, cwg_ref, cwv_ref, cbg_ref, cbv_ref,
             wd_ref, actp_ref, ogp_ref, ovp_ref, acts_ref, ogs_ref, ovs_ref, wdb_ref, w_scr, cg_scr, cv_scr):
        i = pl.program_id(1)

        @pl.when(i == 0)
        def _():
            w_scr[:, :E_TN] = wg_ref[...].astype(BF16)
            w_scr[:, E_TN:] = wv_ref[...].astype(BF16)
            _ffn_up_body(1, batch_s, i, hs_ref, w_scr, cwg_ref, cwv_ref, cbg_ref, cbv_ref, sgs_ref, svs_ref,
                         acts_ref, ogs_ref, ovs_ref, cg_scr, cv_scr)

        @pl.when(i > 0)
        def _():
            wdb_ref[...] = wd_ref[...].astype(BF16)
            _ffn_up_body(bps, 1, i - 1, hp_ref, w_scr, cwg_ref, cwv_ref, cbg_ref, cbv_ref, sgp_ref, svp_ref,
                         actp_ref, ogp_ref, ovp_ref, cg_scr, cv_scr)

    col = lambda shape, off: pl.BlockSpec(shape, lambda j, i: (0,) * (len(shape) - 1) + (j + off,))
    st_p = lambda off: pl.BlockSpec((1, CONV_W - 1, E_TN), lambda j, i: (last(i) // bps, 0, j + off))
    st_s = lambda off: pl.BlockSpec((batch_s, CONV_W - 1, E_TN), lambda j, i: (0, 0, j + off))
    tail_p = pl.BlockSpec((1, SUBLANES, E_TN), lambda j, i: (last(i) // bps, 0, j))
    tail_s = pl.BlockSpec((batch_s, SUBLANES, E_TN), lambda j, i: (0, 0, j))
    wd_spec = pl.BlockSpec((slab, D_MODEL), lambda j, i: (j * nrb + last(i), 0))
    return pl.pallas_call(
        kern,
        grid=(E_NCB, nrb + 1),
        in_specs=[
            pl.BlockSpec((tm_p, D_MODEL), lambda j, i: (last(i), 0)), st_p(0), st_p(E_NCB),
            pl.BlockSpec((n_s, D_MODEL), lambda j, i: (0, 0)), st_s(0), st_s(E_NCB),
            col((D_MODEL, E_TN), 0), col((D_MODEL, E_TN), E_NCB),
            col((CONV_W, E_TN), 0), col((CONV_W, E_TN), E_NCB),
            col((1, E_TN), 0), col((1, E_TN), E_NCB),
            wd_spec,
        ],
        out_specs=[
            pl.BlockSpec((tm_p, E_TN), lambda j, i: (last(i), j)), tail_p, tail_p,
            pl.BlockSpec((n_s, E_TN), lambda j, i: (0, j)), tail_s, tail_s,
            wd_spec,
        ],
        out_shape=[
            jax.ShapeDtypeStruct((n_p, D_FF), BF16),
            jax.ShapeDtypeStruct((batch_p, SUBLANES, D_FF), F32), jax.ShapeDtypeStruct((batch_p, SUBLANES, D_FF), F32),
            jax.ShapeDtypeStruct((n_s, D_FF), BF16),
            jax.ShapeDtypeStruct((batch_s, SUBLANES, D_FF), F32), jax.ShapeDtypeStruct((batch_s, SUBLANES, D_FF), F32),
            jax.ShapeDtypeStruct((D_FF, D_MODEL), BF16),
        ],
        scratch_shapes=[pltpu.VMEM((D_MODEL, 2 * E_TN), BF16),
                        pltpu.VMEM((SUBLANES, E_TN), F32), pltpu.VMEM((SUBLANES, E_TN), F32)],
        compiler_params=_cparams(("arbitrary", "arbitrary")),
        name="ffn_up",
    )(h_p, conv_prev_p, conv_prev_p, h_s, conv_prev_s, conv_prev_s, w_up, w_up, conv_w, conv_w, conv_b, conv_b, w_down)


F_TN = 512


def _ffn_down_kernel(act_ref, w_ref, h1_ref, g_ref, b_ref, y_ref):
    for c0 in range(0, D_MODEL, F_TN):
        f = jnp.dot(act_ref[...], w_ref[:, c0:c0 + F_TN], preferred_element_type=F32)
        y_ref[:, c0:c0 + F_TN] = ALPHA * h1_ref[:, c0:c0 + F_TN] + f
    y_ref[...] = _layer_norm(y_ref[...], g_ref[...], b_ref[...])


def _ffn_down(act_p, h1_p, act_s, h1_s, w_down, g, b, tm):
    n_p, n_s = act_p.shape[0], act_s.shape[0]
    nrb = n_p // tm
    last = lambda i: jnp.minimum(i, nrb - 1)

    def kern(ap_ref, hp_ref, as_ref, hs_ref, w_ref, g_ref, b_ref, yp_ref, ys_ref):
        i = pl.program_id(0)
        pl.when(i < nrb)(lambda: _ffn_down_kernel(ap_ref, w_ref, hp_ref, g_ref, b_ref, yp_ref))
        pl.when(i == nrb)(lambda: _ffn_down_kernel(as_ref, w_ref, hs_ref, g_ref, b_ref, ys_ref))

    rows_p = lambda w: pl.BlockSpec((tm, w), lambda i: (last(i), 0))
    rows_s = lambda w: pl.BlockSpec((n_s, w), lambda i: (0, 0))
    return pl.pallas_call(
        kern,
        grid=(nrb + 1,),
        in_specs=[
            rows_p(D_FF), rows_p(D_MODEL), rows_s(D_FF), rows_s(D_MODEL),
            pl.BlockSpec((D_FF, D_MODEL), lambda i: (0, 0), pipeline_mode=pl.Buffered(1)),
            pl.BlockSpec((1, D_MODEL), lambda i: (0, 0)),
            pl.BlockSpec((1, D_MODEL), lambda i: (0, 0)),
        ],
        out_specs=[rows_p(D_MODEL), rows_s(D_MODEL)],
        out_shape=[jax.ShapeDtypeStruct((n_p, D_MODEL), F32), jax.ShapeDtypeStruct((n_s, D_MODEL), F32)],
        compiler_params=_cparams(("arbitrary",)),
        name="ffn_down",
    )(act_p, h1_p, act_s, h1_s, w_down, g, b)


def _rope_tables(pos, reps):
    half = HEAD_DIM // 2
    inv = ROPE_THETA ** (-jnp.arange(half, dtype=F32) / half)
    ang = pos.astype(F32)[:, None] * inv[None, :]
    cos, sin = jnp.cos(ang), jnp.sin(ang)
    cos128 = jnp.concatenate([cos, cos, cos, cos], axis=1)
    sin128 = jnp.concatenate([-sin, sin, -sin, sin], axis=1)
    return jnp.tile(cos128, (reps, 1)), jnp.tile(sin128, (reps, 1))


TM = 256
D_TM = 256
E_TM = 1024
F_TM = 256
RWKV_NB = 4
ATTN_NB = 2


def kernel(x_prompt, x_sample, cache_k, cache_v, state_wkv, state_shift, state_ffn_conv, ln_in_g, ln_in_b, w_in, attn_sinks, rw_mu, rw_w0, rw_w2, rw_a0, rw_a2, rw_g2, rw_k_k, rw_k_a, rw_r_k, rw_lnx_g, rw_lnx_b, w_out, ln1_g, ln1_b, ffn_w_up, ffn_conv_w, ffn_conv_b, ffn_w_down, ln2_g, ln2_b):
    l = 0
    seg = lax.broadcasted_iota(jnp.int32, (SEG, SEG), 0) // HEAD_DIM
    bd = (seg == seg.T).astype(BF16)
    row = lambda a: a.reshape(1, -1)
    rw_prm = {"mu": row(rw_mu[l]), "w0": row(rw_w0[l]), "a0": row(rw_a0[l]), "k_k": row(rw_k_k[l]),
              "k_a": row(rw_k_a[l]), "r_k": row(rw_r_k[l]), "lnx_g": row(rw_lnx_g[l]), "lnx_b": row(rw_lnx_b[l]),
              "w2": rw_w2[l].astype(BF16), "a2": rw_a2[l].astype(BF16), "g2": rw_g2[l].astype(BF16), "bd": bd}
    sinks = attn_sinks[l]
    bp, tp = x_prompt.shape[0], x_prompt.shape[1]
    bs, ts = x_sample.shape[0], x_sample.shape[1]
    n_p, n_s = bp * tp, bs * ts
    zeros = lambda *s: jnp.zeros(s, F32)

    p_p, h_p, tails_p, p_s, h_s, tails_s = _in_proj(
        x_prompt.reshape(n_p, D_MODEL), x_sample.reshape(n_s, D_MODEL), row(ln_in_g), row(ln_in_b),
        w_in[l].astype(BF16), _rope_tables(jnp.arange(tp, dtype=jnp.int32), 1),
        _rope_tables(PAST_LEN + jnp.arange(ts, dtype=jnp.int32), bs), TM, tp, ts)
    p3_p, p3_s = p_p.reshape(bp, tp, P_COLS), p_s.reshape(bs, ts, P_COLS)

    att_p = _attn_prompt(p3_p, sinks, ATTN_NB).reshape(n_p, ATT_WIDTH)
    att_s, s_k, s_v = _attn_sample(p_s, sinks, cache_k[l].reshape(bs * WINDOW, KV_WIDTH),
                                   cache_v[l].reshape(bs * WINDOW, KV_WIDTH), bs, ts)
    rw_p, p_wkv = _rwkv(p3_p, zeros(bp, 1, RW_COLS), zeros(bp, RW_HEADS, HEAD_DIM, HEAD_DIM), rw_prm, CHUNK, RWKV_NB)
    rw_s, s_wkv = _rwkv(p3_s, state_shift[l], state_wkv[l], rw_prm, ts, RWKV_NB)

    h1_p, h1b_p, h1_s, h1b_s = _out_proj(att_p, rw_p.reshape(n_p, RW_WIDTH), h_p, att_s, rw_s.reshape(n_s, RW_WIDTH), h_s,
                                         w_out[l], row(ln1_g[l]), row(ln1_b[l]), D_TM)
    act_p, cg_p, cv_p, act_s, cg_s, cv_s, w_down_b = _ffn_up(
        h1b_p, zeros(bp, CONV_W - 1, 2 * D_FF), tp, E_TM, h1b_s, state_ffn_conv[l], ts,
        ffn_w_up[l], ffn_conv_w[l], row(ffn_conv_b[l]), ffn_w_down[l])
    y_p, y_s = _ffn_down(act_p, h1_p, act_s, h1_s, w_down_b, row(ln2_g[l]), row(ln2_b[l]), F_TM)

    kv_shape = lambda b: (1, b, WINDOW, ATT_KV_HEADS, HEAD_DIM)
    p_k = p3_p[:, tp - WINDOW:, P_AK:P_AK + KV_WIDTH].reshape(kv_shape(bp))
    p_v = p3_p[:, tp - WINDOW:, P_AV:P_AV + KV_WIDTH].reshape(kv_shape(bp))
    shift_of = lambda tails, b: tails.reshape(b, -1, RW_COLS)[:, -1:, :][None]
    conv_of = lambda cg, cv: jnp.concatenate([cg[:, SUBLANES - 2:, :], cv[:, SUBLANES - 2:, :]], axis=-1)[None]
    return (y_p.reshape(bp, tp, D_MODEL), y_s.reshape(bs, ts, D_MODEL), p_k, p_v, p_wkv[None],
            shift_of(tails_p, bp), conv_of(cg_p, cv_p),
            s_k.reshape(kv_shape(bs)), s_v.reshape(kv_shape(bs)), s_wkv[None],
            shift_of(tails_s, bs), conv_of(cg_s, cv_s))
```

```python
import functools
import math

import jax
import jax.numpy as jnp
from jax import lax
from jax.experimental import pallas as pl
from jax.experimental.pallas import tpu as pltpu

D_MODEL = 2048
HEAD_DIM = 64
ATT_HEADS = 16
ATT_KV_HEADS = 2
ATT_GROUP = ATT_HEADS // ATT_KV_HEADS
ATT_WIDTH = ATT_HEADS * HEAD_DIM
KV_WIDTH = ATT_KV_HEADS * HEAD_DIM
CHUNK = 64
WINDOW = 128
ROPE_THETA = 10000.0
ATT_SCALE = HEAD_DIM ** -0.5
RW_HEADS = 16
RW_WIDTH = RW_HEADS * HEAD_DIM
DECAY_LORA = 64
AAA_LORA = 64
GATE_LORA = 160
LNX_EPS = 64e-5
ATT_COLS = ATT_WIDTH + 2 * KV_WIDTH
RW_COLS = 3 * RW_WIDTH + DECAY_LORA + AAA_LORA + GATE_LORA
D_FF = 5632
CONV_W = 3
LN_EPS = 1e-5
DEPTH = 1
ALPHA = (2 * DEPTH) ** 0.25
PAST_LEN = 1024

LORA_W = 512
P_R, P_K, P_V, P_Q = 0, 1024, 2048, 3072
P_LORA = 4096
P_AK = P_LORA + LORA_W
P_AV = P_AK + KV_WIDTH
P_COLS = P_AV + KV_WIDTH
LANES = 128
SUBLANES = 8
VMEM_LIMIT = 56 * 1024 * 1024

F32 = jnp.float32
BF16 = jnp.bfloat16


def _cparams(sem):
    return pltpu.CompilerParams(dimension_semantics=sem, vmem_limit_bytes=VMEM_LIMIT)


def _layer_norm(y, g, b):
    mu = jnp.mean(y, -1, keepdims=True)
    d = y - mu
    var = jnp.mean(d * d, -1, keepdims=True)
    return d * lax.rsqrt(var + LN_EPS) * g + b


A_TN = 512


def _rope128(x, cos, sin_signed):
    lane = lax.broadcasted_iota(jnp.int32, x.shape, 1)
    first_half = (lane % HEAD_DIM) < (HEAD_DIM // 2)
    rot = jnp.where(first_half, pltpu.roll(x, LANES - HEAD_DIM // 2, 1), pltpu.roll(x, HEAD_DIM // 2, 1))
    return x * cos + rot * sin_signed


def _is_rope_col(col):
    return (P_Q <= col < P_Q + ATT_WIDTH) or (P_AK <= col < P_AK + KV_WIDTH)


O_R = ATT_COLS
O_WD = O_R + RW_WIDTH
O_K = O_WD + DECAY_LORA
O_V = O_K + RW_WIDTH
O_AD = O_V + RW_WIDTH
O_GD = O_AD + AAA_LORA
IN_COLS = O_GD + GATE_LORA


def _in_proj_kernel(x_ref, g_ref, b_ref, w_ref, cos_ref, sin_ref, p_ref, h_ref, tail_ref, hb_ref):
    h = _layer_norm(x_ref[...], g_ref[...], b_ref[...])
    h_ref[...] = h
    hb_ref[...] = h.astype(BF16)
    nseq = tail_ref.shape[0]
    t_seq = x_ref.shape[0] // nseq

    def product(c0, width):
        acc = jnp.dot(hb_ref[...], w_ref[:, c0:c0 + width], preferred_element_type=F32)
        if c0 >= O_R:
            for s in range(nseq):
                tail_ref[s, :, c0 - O_R:c0 - O_R + width] = acc[(s + 1) * t_seq - SUBLANES:(s + 1) * t_seq, :]
        return acc

    for src, width, dst in ((0, ATT_WIDTH, P_Q), (ATT_WIDTH, 2 * KV_WIDTH, P_AK), (O_R, RW_WIDTH, P_R)):
        for c0 in range(0, width, A_TN):
            acc = product(src + c0, min(A_TN, width - c0))
            for l0 in range(0, acc.shape[1], LANES):
                col = dst + c0 + l0
                part = acc[:, l0:l0 + LANES]
                if _is_rope_col(col):
                    part = _rope128(part, cos_ref[...], sin_ref[...])
                p_ref[:, col:col + LANES] = part

    low = lax.broadcasted_iota(jnp.int32, (x_ref.shape[0], LANES), 1) < HEAD_DIM
    span = O_GD - O_WD
    rolled, first_piece = [], None
    for c0 in range(0, span, A_TN):
        acc = product(O_WD + c0, min(A_TN, span - c0))
        for l0 in range(0, acc.shape[1], LANES):
            piece = acc[:, l0:l0 + LANES]
            j = (c0 + l0) // LANES
            if j == 0:
                first_piece = piece
            rolled.append(pltpu.roll(piece, HEAD_DIM, 1))
            if j >= 1:
                m = j - 1
                dst = P_K + m * LANES if m < RW_WIDTH // LANES else P_V + (m - RW_WIDTH // LANES) * LANES
                p_ref[:, dst:dst + LANES] = jnp.where(low, rolled[m], rolled[m + 1])
            if j == span // LANES - 1:
                p_ref[:, P_LORA:P_LORA + LANES] = jnp.where(low, first_piece, piece)
    gd0 = P_LORA + DECAY_LORA + AAA_LORA
    p_ref[:, gd0:gd0 + GATE_LORA] = product(O_GD, GATE_LORA)
    p_ref[:, gd0 + GATE_LORA:P_LORA + LORA_W] = jnp.zeros((x_ref.shape[0], LORA_W - DECAY_LORA - AAA_LORA - GATE_LORA), F32)


def _in_proj(xp, xs, g, b, w_p, tables_p, tables_s, tm, seq_p, seq_s):
    n_p, n_s = xp.shape[0], xs.shape[0]
    assert n_s == tm
    nrb = n_p // tm
    nseq_s = tm // seq_s
    tb = seq_p // tm
    last = lambda i: jnp.minimum(i, nrb - 1)

    def body(ins, shared, outs, scratch):
        x_ref, cos_ref, sin_ref = ins
        g_ref, b_ref, w_ref = shared
        _in_proj_kernel(x_ref, g_ref, b_ref, w_ref, cos_ref, sin_ref, *outs, *scratch)

    def kern(xp_ref, cp_ref, sp_ref, xs_ref, cs_ref, ss_ref, g_ref, b_ref, w_ref,
             pp_ref, hp_ref, tp_ref, ps_ref, hs_ref, ts_ref, hb_ref):
        i = pl.program_id(0)
        pl.when(i < nrb)(lambda: body((xp_ref, cp_ref, sp_ref), (g_ref, b_ref, w_ref), (pp_ref, hp_ref, tp_ref), (hb_ref,)))
        pl.when(i == nrb)(lambda: body((xs_ref, cs_ref, ss_ref), (g_ref, b_ref, w_ref), (ps_ref, hs_ref, ts_ref), (hb_ref,)))

    const2 = lambda shape: pl.BlockSpec(shape, lambda i: (0, 0))
    return pl.pallas_call(
        kern,
        grid=(nrb + 1,),
        in_specs=[
            pl.BlockSpec((tm, D_MODEL), lambda i: (last(i), 0)),
            pl.BlockSpec((tm, LANES), lambda i: (last(i) % tb, 0)),
            pl.BlockSpec((tm, LANES), lambda i: (last(i) % tb, 0)),
            const2((tm, D_MODEL)), const2((tm, LANES)), const2((tm, LANES)),
            const2((1, D_MODEL)), const2((1, D_MODEL)),
            pl.BlockSpec((D_MODEL, IN_COLS), lambda i: (0, 0), pipeline_mode=pl.Buffered(1)),
        ],
        out_specs=[
            pl.BlockSpec((tm, P_COLS), lambda i: (last(i), 0)),
            pl.BlockSpec((tm, D_MODEL), lambda i: (last(i), 0)),
            pl.BlockSpec((1, SUBLANES, RW_COLS), lambda i: (last(i), 0, 0)),
            const2((tm, P_COLS)), const2((tm, D_MODEL)),
            pl.BlockSpec((nseq_s, SUBLANES, RW_COLS), lambda i: (0, 0, 0)),
        ],
        out_shape=[
            jax.ShapeDtypeStruct((n_p, P_COLS), F32),
            jax.ShapeDtypeStruct((n_p, D_MODEL), F32),
            jax.ShapeDtypeStruct((nrb, SUBLANES, RW_COLS), F32),
            jax.ShapeDtypeStruct((n_s, P_COLS), F32),
            jax.ShapeDtypeStruct((n_s, D_MODEL), F32),
            jax.ShapeDtypeStruct((nseq_s, SUBLANES, RW_COLS), F32),
        ],
        scratch_shapes=[pltpu.VMEM((tm, D_MODEL), BF16)],
        compiler_params=_cparams(("arbitrary",)),
        name="in_proj",
    )(xp, *tables_p, xs, *tables_s, g, b, w_p)


def _attend(q, k, v, valid, sink_ref):
    r = q.shape[0]
    outs = []
    for kv in range(ATT_KV_HEADS):
        kh = k[:, kv * HEAD_DIM:(kv + 1) * HEAD_DIM].astype(BF16)
        vh = v[:, kv * HEAD_DIM:(kv + 1) * HEAD_DIM].astype(BF16)
        heads = [kv * ATT_GROUP + g for g in range(ATT_GROUP)]
        qs = jnp.concatenate([q[:, h * HEAD_DIM:(h + 1) * HEAD_DIM] for h in heads], axis=0)
        sink = jnp.concatenate([jnp.full((r, 1), sink_ref[h], F32) for h in heads], axis=0)
        s = lax.dot_general(qs.astype(BF16), kh, (((1,), (1,)), ((), ())), preferred_element_type=F32) * ATT_SCALE
        if valid is not None:
            s = jnp.where(valid, s, -jnp.inf)
        m = jnp.maximum(jnp.max(s, -1, keepdims=True), sink)
        p = jnp.exp(s - m)
        den = jnp.sum(p, -1, keepdims=True) + jnp.exp(sink - m)
        o = jnp.dot(p.astype(BF16), vh, preferred_element_type=F32) / den
        outs.extend(o[g * r:(g + 1) * r] for g in range(ATT_GROUP))
    return jnp.concatenate(outs, axis=1)


LOG2_E = math.log2(math.e)
QB = 2 * CHUNK


def _attn_prompt_kernel(sink_ref, q_ref, kp_ref, kc_ref, vp_ref, vc_ref, o_ref):
    m = pl.program_id(1)
    nseq = q_ref.shape[0]
    span = 2 * QB
    lane = lax.broadcasted_iota(jnp.int32, (span, KV_WIDTH), 1)
    row_c = 2 + lax.broadcasted_iota(jnp.int32, (QB, span), 0) // CHUNK
    col_c = lax.broadcasted_iota(jnp.int32, (QB, span), 1) // CHUNK
    back = row_c - col_c
    valid = jnp.logical_and(jnp.logical_and(back >= 0, back <= 2), 2 * m - 2 + col_c >= 0)
    out_lane = lax.broadcasted_iota(jnp.int32, (QB, PAIR), 1)

    def dup(x, kv):
        sw = pltpu.roll(x, HEAD_DIM, 1)
        return jnp.where(lane < HEAD_DIM, x, sw) if kv == 0 else jnp.where(lane < HEAD_DIM, sw, x)

    k_st, v_st = [], []
    for b in range(nseq):
        k_all = jnp.concatenate([kp_ref[b], kc_ref[b]], axis=0)
        v_all = jnp.concatenate([vp_ref[b], vc_ref[b]], axis=0)
        k_st.append([_stack_heads(dup(k_all, kv), HEAD_DIM).astype(BF16) for kv in range(ATT_KV_HEADS)])
        v_st.append([_stack_heads(dup(v_all, kv), HEAD_DIM).astype(BF16) for kv in range(ATT_KV_HEADS)])
    pairs = [(b, j) for b in range(nseq) for j in range(ATT_HEADS // 2)]
    kv_of = lambda j: (2 * j) // ATT_GROUP
    s = [lax.dot_general((q_ref[b, :, j * PAIR:(j + 1) * PAIR] * (ATT_SCALE * LOG2_E)).astype(BF16), k_st[b][kv_of(j)],
                         (((1,), (1,)), ((), ())), preferred_element_type=F32) for b, j in pairs]
    probs, dens = [], []
    for idx, (b, j) in enumerate(pairs):
        halves, den = [], []
        for e in range(2):
            sink = sink_ref[2 * j + e] * LOG2_E
            sh = jnp.where(valid, s[idx][:, e * span:(e + 1) * span], -jnp.inf)
            mx = jnp.maximum(jnp.max(sh, -1, keepdims=True), sink)
            ph = jnp.exp2(sh - mx)
            halves.append(ph)
            den.append(jnp.sum(ph, -1, keepdims=True) + jnp.exp2(sink - mx))
        probs.append(jnp.concatenate(halves, axis=1).astype(BF16))
        dens.append(jnp.where(out_lane < HEAD_DIM, den[0], den[1]))
    o = [jnp.dot(probs[idx], v_st[b][kv_of(j)], preferred_element_type=F32) / dens[idx]
         for idx, (b, j) in enumerate(pairs)]
    n_pairs = ATT_HEADS // 2
    for b in range(nseq):
        o_ref[b] = jnp.concatenate(o[b * n_pairs:(b + 1) * n_pairs], axis=1).astype(o_ref.dtype)


def _attn_prompt(p3, sinks, nseq):
    batch, seq = p3.shape[0], p3.shape[1]
    nb = seq // QB
    kcol, vcol = P_AK // KV_WIDTH, P_AV // KV_WIDTH
    prev = lambda col: pl.BlockSpec((nseq, QB, KV_WIDTH), lambda g, m: (g, jnp.maximum(m - 1, 0), col))
    cur = lambda col: pl.BlockSpec((nseq, QB, KV_WIDTH), lambda g, m: (g, m, col))
    return pl.pallas_call(
        _attn_prompt_kernel,
        grid=(batch // nseq, nb),
        in_specs=[
            pl.BlockSpec(memory_space=pltpu.SMEM),
            pl.BlockSpec((nseq, QB, ATT_WIDTH), lambda g, m: (g, m, P_Q // ATT_WIDTH)),
            prev(kcol), cur(kcol), prev(vcol), cur(vcol),
        ],
        out_specs=pl.BlockSpec((nseq, QB, ATT_WIDTH), lambda g, m: (g, m, 0)),
        out_shape=jax.ShapeDtypeStruct((batch, seq, ATT_WIDTH), BF16),
        compiler_params=_cparams(("arbitrary", "arbitrary")),
        name="attn_prompt",
    )(sinks, p3, p3, p3, p3, p3)


def _attn_sample_kernel(sink_ref, q_ref, kn_ref, vn_ref, kc_ref, vc_ref, o_ref, kw_ref, vw_ref):
    t = q_ref.shape[0]
    k = jnp.concatenate([kc_ref[...], kn_ref[...]], axis=0)
    v = jnp.concatenate([vc_ref[...], vn_ref[...]], axis=0)
    o_ref[...] = _attend(q_ref[...], k, v, None, sink_ref).astype(o_ref.dtype)
    kw_ref[...] = k[t:]
    vw_ref[...] = v[t:]


def _attn_sample(p, sinks, cache_k, cache_v, batch, seq):
    kcol, vcol = P_AK // KV_WIDTH, P_AV // KV_WIDTH
    return pl.pallas_call(
        _attn_sample_kernel,
        grid=(batch,),
        in_specs=[
            pl.BlockSpec(memory_space=pltpu.SMEM),
            pl.BlockSpec((seq, ATT_WIDTH), lambda b: (b, P_Q // ATT_WIDTH)),
            pl.BlockSpec((seq, KV_WIDTH), lambda b: (b, kcol)),
            pl.BlockSpec((seq, KV_WIDTH), lambda b: (b, vcol)),
            pl.BlockSpec((WINDOW, KV_WIDTH), lambda b: (b, 0)),
            pl.BlockSpec((WINDOW, KV_WIDTH), lambda b: (b, 0)),
        ],
        out_specs=[
            pl.BlockSpec((seq, ATT_WIDTH), lambda b: (b, 0)),
            pl.BlockSpec((WINDOW, KV_WIDTH), lambda b: (b, 0)),
            pl.BlockSpec((WINDOW, KV_WIDTH), lambda b: (b, 0)),
        ],
        out_shape=[
            jax.ShapeDtypeStruct((batch * seq, ATT_WIDTH), BF16),
            jax.ShapeDtypeStruct((batch * WINDOW, KV_WIDTH), F32),
            jax.ShapeDtypeStruct((batch * WINDOW, KV_WIDTH), F32),
        ],
        compiler_params=_cparams(("arbitrary",)),
        name="attn_sample",
    )(sinks, p, p, p, cache_k, cache_v)


SEG = 256
EXP_NEG_HALF = math.exp(-0.5)
PAIR = 2 * HEAD_DIM
N_PAIRS = RW_HEADS // 2


def _split2(x):
    hi = x.astype(BF16)
    lo = (x - hi.astype(F32)).astype(BF16)
    return hi, lo


def _seg_sum(x, bd):
    c = x.shape[0]
    xb = x.astype(BF16)
    stack = jnp.concatenate([xb[:, s:s + SEG] for s in range(0, RW_WIDTH, SEG)], axis=0)
    r = jnp.dot(stack, bd, preferred_element_type=F32)
    return jnp.concatenate([r[i * c:(i + 1) * c] for i in range(RW_WIDTH // SEG)], axis=1)


def _shift_rows(x, prev_row):
    row = lax.broadcasted_iota(jnp.int32, x.shape, 0)
    return jnp.where(row == 0, prev_row, pltpu.roll(x, 1, 0))


def _dotb(a, b):
    return jnp.dot(a.astype(BF16), b.astype(BF16), preferred_element_type=F32)


def _dotb_nt(a, b):
    return lax.dot_general(a.astype(BF16), b.astype(BF16), (((1,), (1,)), ((), ())), preferred_element_type=F32)


def _dot2_tn(a, b):
    ah, al = _split2(a)
    bh = b.astype(BF16)
    return lax.dot_general(jnp.concatenate([ah, al], axis=0), jnp.concatenate([bh, bh], axis=0),
                           (((0,), (0,)), ((), ())), preferred_element_type=F32)


def _stack_heads(x, half):
    lane = lax.broadcasted_iota(jnp.int32, x.shape, 1)
    first = (lane % (2 * half)) < half
    zero = jnp.zeros_like(x)
    return jnp.concatenate([jnp.where(first, x, zero), jnp.where(first, zero, x)], axis=0)


def _stack_heads_bf16(x, half):
    return _stack_heads(x.astype(BF16), half)


def _rwkv_pairs(ab, rb, bb, kb, v, gam_c, s_bd, n_double):
    c = ab[0].shape[0]
    pairs = range(len(ab))
    t_idx = lax.broadcasted_iota(jnp.int32, (c, 2 * c), 0)
    s_idx = lax.broadcasted_iota(jnp.int32, (c, 2 * c), 1) % c
    strict, incl = s_idx < t_idx, s_idx <= t_idx
    g = [_dotb_nt(jnp.concatenate([ab[p], rb[p]], axis=0),
                  jnp.concatenate([_stack_heads_bf16(bb[p], HEAD_DIM), _stack_heads_bf16(kb[p], HEAD_DIM)], axis=0))
         for p in pairs]
    l_ak = [jnp.where(strict, g[p][:c, 2 * c:], 0.0) for p in pairs]
    m_rbk = [jnp.concatenate([jnp.where(incl, g[p][c:, :2 * c], 0.0), jnp.where(incl, g[p][c:, 2 * c:], 0.0)], axis=1)
             for p in pairs]
    lp = [jnp.where(strict, g[p][:c, :2 * c], 0.0) for p in pairs]
    v_st = [_stack_heads_bf16(v[p], HEAD_DIM) for p in pairs]
    y0 = [jnp.concatenate([ab[p], _dotb(l_ak[p], v_st[p])], axis=1) for p in pairs]
    n_acc = list(lp)
    lk = [_dotb(lp[p], _stack_heads_bf16(lp[p], c)) for p in pairs]
    for k in range(1, n_double):
        n_st = [_stack_heads_bf16(n_acc[p], c) for p in pairs]
        if k + 1 < n_double:
            z = [_dotb(lk[p], jnp.concatenate([n_st[p], _stack_heads_bf16(lk[p], c)], axis=1)) for p in pairs]
            n_acc = [n_acc[p] + lk[p] + z[p][:, :2 * c] for p in pairs]
            lk = [z[p][:, 2 * c:] for p in pairs]
        else:
            n_acc = [n_acc[p] + lk[p] + _dotb(lk[p], n_st[p]) for p in pairs]
    y = [y0[p] + _dotb(n_acc[p], _stack_heads_bf16(y0[p], HEAD_DIM)) for p in pairs]
    ya = [y[p][:, :PAIR] for p in pairs]
    yu = [y[p][:, PAIR:] for p in pairs]
    ou = []
    for p in pairs:
        lhs = jnp.concatenate([rb[p], ya[p]], axis=0).astype(BF16)
        ou.append(lax.dot_general(jnp.concatenate([lhs, lhs], axis=1), jnp.concatenate(_split2(s_bd[p]), axis=1),
                                  (((1,), (1,)), ((), ())), preferred_element_type=F32))
    u = [ou[p][c:] + yu[p] for p in pairs]
    o = [ou[p][:c] + _dotb(m_rbk[p], jnp.concatenate([_stack_heads_bf16(u[p], HEAD_DIM), v_st[p]], axis=0))
         for p in pairs]
    row_h = lax.broadcasted_iota(jnp.int32, (PAIR, PAIR), 0) // HEAD_DIM
    col_h = lax.broadcasted_iota(jnp.int32, (PAIR, PAIR), 1) // HEAD_DIM
    same = row_h == col_h
    d_t = [jnp.where(same, _dot2_tn(jnp.concatenate([u[p], v[p]], axis=0),
                                    jnp.concatenate([bb[p], kb[p]], axis=0)), 0.0) for p in pairs]
    s_new = [(s_bd[p] + d_t[p]) * gam_c[p] for p in pairs]
    return o, s_new


def _split_rw_rows(x):
    o_wd, o_gd = RW_WIDTH, RW_COLS - GATE_LORA
    low = lax.broadcasted_iota(jnp.int32, (x.shape[0], LANES), 1) < HEAD_DIM
    pieces = [x[:, c0:c0 + LANES] for c0 in range(o_wd, o_gd, LANES)]
    rolled = [pltpu.roll(p, HEAD_DIM, 1) for p in pieces]
    merged = [jnp.where(low, rolled[m], rolled[m + 1]) for m in range(len(pieces) - 1)]
    n = RW_WIDTH // LANES
    return (x[:, :RW_WIDTH], jnp.concatenate(merged[:n], axis=1), jnp.concatenate(merged[n:], axis=1),
            jnp.where(low, pieces[0], pieces[-1]), x[:, o_gd:])


def _rwkv_kernel(xr_ref, xk_ref, xv_ref, xl_ref, sh_ref, mu_ref, w0_ref, a0_ref, kk_ref, ka_ref, rk_ref,
                 lg_ref, lb_ref, w2_ref, a2_ref, g2_ref, bd_ref, st0_ref,
                 out_ref, st_ref,
                 s_scr, pr_scr, pk_scr, pv_scr, pw_scr, pg_scr):
    c = pl.program_id(1)
    nc = pl.num_programs(1)
    nb, C = xr_ref.shape[0], xr_ref.shape[1]
    n_double = int(math.log2(C))
    prev_scr = (pr_scr, pk_scr, pv_scr, pw_scr, pg_scr)

    @pl.when(c == 0)
    def _():
        s_scr[...] = jnp.zeros_like(s_scr)
        for b in range(nb):
            for h in range(RW_HEADS):
                off = (h % 2) * HEAD_DIM
                s_scr[b * N_PAIRS + h // 2, off:off + HEAD_DIM, off:off + HEAD_DIM] = st0_ref[b, h]
            for scr, row in zip(prev_scr, _split_rw_rows(sh_ref[b])):
                scr[b, 0:1, :] = row

    def shifted(load, p_scr, mu):
        parts = []
        for b in range(nb):
            x = load(b)
            prev = _shift_rows(x, p_scr[b, 0:1, :])
            p_scr[b, 0:1, :] = x[C - 1:C, :]
            parts.append(x + (prev - x) * mu)
        return jnp.concatenate(parts, axis=0)

    mu_r, mu_k, mu_v, mu_w, mu_g = _split_rw_rows(mu_ref[...])
    wa_w = DECAY_LORA + AAA_LORA
    r = shifted(lambda b: xr_ref[b], pr_scr, mu_r)
    kr = shifted(lambda b: xk_ref[b], pk_scr, mu_k)
    v = shifted(lambda b: xv_ref[b], pv_scr, mu_v)
    xw = shifted(lambda b: xl_ref[b, :, :wa_w], pw_scr, mu_w)
    gd = shifted(lambda b: xl_ref[b, :, wa_w:wa_w + GATE_LORA], pg_scr, mu_g)
    wd = xw[:, 0:DECAY_LORA]
    ad = xw[:, DECAY_LORA:wa_w]

    lw = -EXP_NEG_HALF * jax.nn.sigmoid(w0_ref[...] + _dotb(jnp.tanh(wd), w2_ref[...]))
    a = jax.nn.sigmoid(a0_ref[...] + _dotb(ad, a2_ref[...]))
    g = _dotb(jax.nn.sigmoid(gd), g2_ref[...])
    bd = bd_ref[...]
    kk = kr * kk_ref[...]
    kk = kk * lax.rsqrt(jnp.maximum(_seg_sum(kk * kk, bd), 1e-24))
    k2 = kr * (1.0 + (a - 1.0) * ka_ref[...])

    rows = nb * C
    ti = lax.broadcasted_iota(jnp.int32, (rows, rows), 0)
    si = lax.broadcasted_iota(jnp.int32, (rows, rows), 1)
    tri = jnp.logical_and(si <= ti, si // C == ti // C).astype(BF16)
    cs2 = jnp.dot(tri, jnp.concatenate(_split2(lw), axis=1), preferred_element_type=F32)
    cs = cs2[:, :RW_WIDTH] + cs2[:, RW_WIDTH:]
    gam = jnp.exp(cs)
    ginv = jnp.exp(-cs)
    ab = -kk * jnp.exp(cs - lw)
    rb = r * gam
    bb = kk * a * ginv
    kb = k2 * ginv

    def per_pair(x, r0, r1):
        return [x[b * C + r0:b * C + r1, p * PAIR:(p + 1) * PAIR] for b in range(nb) for p in range(N_PAIRS)]

    chains = nb * N_PAIRS
    outs, s_new = _rwkv_pairs(per_pair(ab, 0, C), per_pair(rb, 0, C), per_pair(bb, 0, C), per_pair(kb, 0, C),
                              per_pair(v, 0, C), per_pair(gam, C - 1, C), [s_scr[q] for q in range(chains)], n_double)
    for q in range(chains):
        s_scr[q] = s_new[q]
    o = jnp.concatenate([jnp.concatenate(outs[b * N_PAIRS:(b + 1) * N_PAIRS], axis=1) for b in range(nb)], axis=0)

    mo = _seg_sum(o, bd) * (1.0 / HEAD_DIM)
    d = o - mo
    vo = _seg_sum(d * d, bd) * (1.0 / HEAD_DIM)
    on = d * lax.rsqrt(vo + LNX_EPS) * lg_ref[...] + lb_ref[...]
    bonus = _seg_sum(r * k2 * rk_ref[...], bd) * v
    res = ((on + bonus) * g).astype(out_ref.dtype)
    for b in range(nb):
        out_ref[b] = res[b * C:(b + 1) * C, :]

    @pl.when(c == nc - 1)
    def _():
        for b in range(nb):
            for h in range(RW_HEADS):
                off = (h % 2) * HEAD_DIM
                st_ref[b, h] = s_scr[b * N_PAIRS + h // 2, off:off + HEAD_DIM, off:off + HEAD_DIM]


def _rwkv(p3, state_shift, state0, prm, chunk, nb):
    batch, seq = p3.shape[0], p3.shape[1]
    nc = seq // chunk
    row = lambda w: pl.BlockSpec((1, w), lambda g, c: (0, 0))
    xspec = lambda w, col: pl.BlockSpec((nb, chunk, w), lambda g, c: (g, c, col))
    full = lambda a: pl.BlockSpec(a.shape, lambda g, c: (0,) * a.ndim)
    stspec = pl.BlockSpec((nb, RW_HEADS, HEAD_DIM, HEAD_DIM), lambda g, c: (g, 0, 0, 0))
    return pl.pallas_call(
        _rwkv_kernel,
        grid=(batch // nb, nc),
        in_specs=[
            xspec(RW_WIDTH, P_R // RW_WIDTH), xspec(RW_WIDTH, P_K // RW_WIDTH), xspec(RW_WIDTH, P_V // RW_WIDTH),
            xspec(LORA_W, P_LORA // LORA_W),
            pl.BlockSpec((nb, 1, RW_COLS), lambda g, c: (g, 0, 0)),
            row(RW_COLS),
            row(RW_WIDTH), row(RW_WIDTH), row(RW_WIDTH), row(RW_WIDTH), row(RW_WIDTH),
            row(RW_WIDTH), row(RW_WIDTH),
            full(prm["w2"]), full(prm["a2"]), full(prm["g2"]), full(prm["bd"]),
            stspec,
        ],
        out_specs=[
            pl.BlockSpec((nb, chunk, RW_WIDTH), lambda g, c: (g, c, 0)),
            stspec,
        ],
        out_shape=[
            jax.ShapeDtypeStruct((batch, seq, RW_WIDTH), BF16),
            jax.ShapeDtypeStruct((batch, RW_HEADS, HEAD_DIM, HEAD_DIM), F32),
        ],
        scratch_shapes=[
            pltpu.VMEM((nb * N_PAIRS, PAIR, PAIR), F32),
            pltpu.VMEM((nb, SUBLANES, RW_WIDTH), F32), pltpu.VMEM((nb, SUBLANES, RW_WIDTH), F32),
            pltpu.VMEM((nb, SUBLANES, RW_WIDTH), F32),
            pltpu.VMEM((nb, SUBLANES, DECAY_LORA + AAA_LORA), F32), pltpu.VMEM((nb, SUBLANES, GATE_LORA), F32),
        ],
        compiler_params=_cparams(("arbitrary", "arbitrary")),
        name="rwkv",
    )(p3, p3, p3, p3, state_shift, prm["mu"], prm["w0"], prm["a0"], prm["k_k"], prm["k_a"], prm["r_k"],
      prm["lnx_g"], prm["lnx_b"], prm["w2"], prm["a2"], prm["g2"], prm["bd"], state0)


D_TN = 512


def _out_proj_body(att_ref, rw_ref, h_ref, g_ref, b_ref, h1_ref, h1b_ref, w_scr):
    for c0 in range(0, D_MODEL, D_TN):
        m = jnp.dot(att_ref[...], w_scr[:ATT_WIDTH, c0:c0 + D_TN], preferred_element_type=F32)
        m = m + jnp.dot(rw_ref[...], w_scr[ATT_WIDTH:, c0:c0 + D_TN], preferred_element_type=F32)
        h1_ref[:, c0:c0 + D_TN] = ALPHA * h_ref[:, c0:c0 + D_TN] + m
    y = _layer_norm(h1_ref[...], g_ref[...], b_ref[...])
    h1_ref[...] = y
    h1b_ref[...] = y.astype(BF16)


def _out_proj(att_p, rw_p, h_p, att_s, rw_s, h_s, w_out, g, b, tm):
    n_p, n_s = att_p.shape[0], att_s.shape[0]
    nrb = n_p // tm
    last = lambda i: jnp.minimum(i, nrb - 1)

    def kern(ap_ref, rp_ref, hp_ref, as_ref, rs_ref, hs_ref, wa_ref, wb_ref, g_ref, b_ref,
             h1p_ref, h1bp_ref, h1s_ref, h1bs_ref, w_scr):
        i = pl.program_id(0)

        @pl.when(i == 0)
        def _():
            w_scr[:ATT_WIDTH, :] = wa_ref[...].astype(BF16)
            w_scr[ATT_WIDTH:, :] = wb_ref[...].astype(BF16)

        pl.when(i < nrb)(lambda: _out_proj_body(ap_ref, rp_ref, hp_ref, g_ref, b_ref, h1p_ref, h1bp_ref, w_scr))
        pl.when(i == nrb)(lambda: _out_proj_body(as_ref, rs_ref, hs_ref, g_ref, b_ref, h1s_ref, h1bs_ref, w_scr))

    rows_p = lambda w: pl.BlockSpec((tm, w), lambda i: (last(i), 0))
    rows_s = lambda w: pl.BlockSpec((n_s, w), lambda i: (0, 0))
    return pl.pallas_call(
        kern,
        grid=(nrb + 1,),
        in_specs=[
            rows_p(ATT_WIDTH), rows_p(RW_WIDTH), rows_p(D_MODEL),
            rows_s(ATT_WIDTH), rows_s(RW_WIDTH), rows_s(D_MODEL),
            pl.BlockSpec((ATT_WIDTH, D_MODEL), lambda i: (0, 0), pipeline_mode=pl.Buffered(1)),
            pl.BlockSpec((RW_WIDTH, D_MODEL), lambda i: (1, 0), pipeline_mode=pl.Buffered(1)),
            pl.BlockSpec((1, D_MODEL), lambda i: (0, 0)),
            pl.BlockSpec((1, D_MODEL), lambda i: (0, 0)),
        ],
        out_specs=[rows_p(D_MODEL), rows_p(D_MODEL), rows_s(D_MODEL), rows_s(D_MODEL)],
        out_shape=[
            jax.ShapeDtypeStruct((n_p, D_MODEL), F32), jax.ShapeDtypeStruct((n_p, D_MODEL), BF16),
            jax.ShapeDtypeStruct((n_s, D_MODEL), F32), jax.ShapeDtypeStruct((n_s, D_MODEL), BF16),
        ],
        scratch_shapes=[pltpu.VMEM((ATT_WIDTH + RW_WIDTH, D_MODEL), BF16)],
        compiler_params=_cparams(("arbitrary",)),
        name="out_proj",
    )(att_p, rw_p, h_p, att_s, rw_s, h_s, w_out, w_out, g, b)


E_TN = 512
E_NCB = D_FF // E_TN
E_SUB = 1024
SQRT_HALF = math.sqrt(0.5)


def _ffn_up_body(bps, nseq, i, h_ref, w_scr, cwg_ref, cwv_ref, cbg_ref, cbv_ref, sg_ref, sv_ref,
                 act_ref, og_ref, ov_ref, cg_scr, cv_scr):
    tm = h_ref.shape[0]
    sub = min(tm, E_SUB)
    n_sub = tm // sub
    t_seq = tm // nseq
    row8 = lax.broadcasted_iota(jnp.int32, (SUBLANES, E_TN), 0)

    def products(r0):
        u = jnp.dot(h_ref[r0:r0 + sub, :], w_scr[...], preferred_element_type=F32)
        return u[:, :E_TN], u[:, E_TN:]

    def conv(u, prevs, cw_ref, cb_ref):
        sh1, sh2 = pltpu.roll(u, 1, 0), pltpu.roll(u, 2, 0)
        for r, prev in prevs:
            p2, p1 = prev[0:1, :], prev[1:2, :]
            fix1 = jnp.where(row8 == 0, p1, sh1[r:r + SUBLANES, :])
            fix2 = jnp.where(row8 == 0, p2, jnp.where(row8 == 1, p1, sh2[r:r + SUBLANES, :]))
            head = lambda x: [x[:r, :]] if r > 0 else []
            tail = lambda x: [x[r + SUBLANES:, :]] if r + SUBLANES < x.shape[0] else []
            sh1 = jnp.concatenate(head(sh1) + [fix1] + tail(sh1), axis=0)
            sh2 = jnp.concatenate(head(sh2) + [fix2] + tail(sh2), axis=0)
        cw = cw_ref[...]
        c = cb_ref[...] + sh2 * cw[0:1, :]
        c = c + sh1 * cw[1:2, :]
        return c + u * cw[2:3, :]

    def epilogue(r0, ug, uv, prev_g, prev_v):
        gate = conv(ug, prev_g, cwg_ref, cbg_ref)
        val = conv(uv, prev_v, cwv_ref, cbv_ref)
        gelu = 0.5 * gate * (1.0 + lax.erf(gate * SQRT_HALF))
        act_ref[r0:r0 + sub, :] = (gelu * val).astype(act_ref.dtype)

    if nseq > 1:
        ug, uv = products(0)
        epilogue(0, ug, uv, [(s * t_seq, sg_ref[s]) for s in range(nseq)],
                 [(s * t_seq, sv_ref[s]) for s in range(nseq)])
        for s in range(nseq):
            og_ref[s] = ug[(s + 1) * t_seq - SUBLANES:(s + 1) * t_seq, :]
            ov_ref[s] = uv[(s + 1) * t_seq - SUBLANES:(s + 1) * t_seq, :]
    else:
        first = (i % bps) == 0
        prev_g = jnp.where(first, sg_ref[0], cg_scr[SUBLANES - 2:SUBLANES, :])
        prev_v = jnp.where(first, sv_ref[0], cv_scr[SUBLANES - 2:SUBLANES, :])
        nxt = products(0)
        for k in range(n_sub):
            ug, uv = nxt
            if k + 1 < n_sub:
                nxt = products((k + 1) * sub)
            epilogue(k * sub, ug, uv, [(0, prev_g)], [(0, prev_v)])
            prev_g, prev_v = ug[sub - 2:sub, :], uv[sub - 2:sub, :]
        cg_scr[...] = ug[sub - SUBLANES:sub, :]
        cv_scr[...] = uv[sub - SUBLANES:sub, :]
        og_ref[0] = ug[sub - SUBLANES:sub, :]
        ov_ref[0] = uv[sub - SUBLANES:sub, :]


def _ffn_up(h_p, conv_prev_p, seq_p, tm_p, h_s, conv_prev_s, seq_s, w_up, conv_w, conv_b, w_down):
    n_p, n_s = h_p.shape[0], h_s.shape[0]
    batch_p, batch_s = n_p // seq_p, n_s // seq_s
    bps = seq_p // tm_p
    nrb = n_p // tm_p
    slab = D_FF // (E_NCB * nrb)
    last = lambda i: jnp.maximum(i - 1, 0)

    def kern(hp_ref, sgp_ref, svp_ref, hs_ref, sgs_ref, svs_ref, wg_ref, wv_ref, cwg_ref, cwv_ref, cbg_ref, cbv_ref,
             wd_ref, actp_ref, ogp_ref, ovp_ref, acts_ref, ogs_ref, ovs_ref, wdb_ref, w_scr, cg_scr, cv_scr):
        i = pl.program_id(1)

        @pl.when(i == 0)
        def _():
            w_scr[:, :E_TN] = wg_ref[...].astype(BF16)
            w_scr[:, E_TN:] = wv_ref[...].astype(BF16)
            _ffn_up_body(1, batch_s, i, hs_ref, w_scr, cwg_ref, cwv_ref, cbg_ref, cbv_ref, sgs_ref, svs_ref,
                         acts_ref, ogs_ref, ovs_ref, cg_scr, cv_scr)

        @pl.when(i > 0)
        def _():
            wdb_ref[...] = wd_ref[...].astype(BF16)
            _ffn_up_body(bps, 1, i - 1, hp_ref, w_scr, cwg_ref, cwv_ref, cbg_ref, cbv_ref, sgp_ref, svp_ref,
                         actp_ref, ogp_ref, ovp_ref, cg_scr, cv_scr)

    col = lambda shape, off: pl.BlockSpec(shape, lambda j, i: (0,) * (len(shape) - 1) + (j + off,))
    st_p = lambda off: pl.BlockSpec((1, CONV_W - 1, E_TN), lambda j, i: (last(i) // bps, 0, j + off))
    st_s = lambda off: pl.BlockSpec((batch_s, CONV_W - 1, E_TN), lambda j, i: (0, 0, j + off))
    tail_p = pl.BlockSpec((1, SUBLANES, E_TN), lambda j, i: (last(i) // bps, 0, j))
    tail_s = pl.BlockSpec((batch_s, SUBLANES, E_TN), lambda j, i: (0, 0, j))
    wd_spec = pl.BlockSpec((slab, D_MODEL), lambda j, i: (j * nrb + last(i), 0))
    return pl.pallas_call(
        kern,
        grid=(E_NCB, nrb + 1),
        in_specs=[
            pl.BlockSpec((tm_p, D_MODEL), lambda j, i: (last(i), 0)), st_p(0), st_p(E_NCB),
            pl.BlockSpec((n_s, D_MODEL), lambda j, i: (0, 0)), st_s(0), st_s(E_NCB),
            col((D_MODEL, E_TN), 0), col((D_MODEL, E_TN), E_NCB),
            col((CONV_W, E_TN), 0), col((CONV_W, E_TN), E_NCB),
            col((1, E_TN), 0), col((1, E_TN), E_NCB),
            wd_spec,
        ],
        out_specs=[
            pl.BlockSpec((tm_p, E_TN), lambda j, i: (last(i), j)), tail_p, tail_p,
            pl.BlockSpec((n_s, E_TN), lambda j, i: (0, j)), tail_s, tail_s,
            wd_spec,
        ],
        out_shape=[
            jax.ShapeDtypeStruct((n_p, D_FF), BF16),
            jax.ShapeDtypeStruct((batch_p, SUBLANES, D_FF), F32), jax.ShapeDtypeStruct((batch_p, SUBLANES, D_FF), F32),
            jax.ShapeDtypeStruct((n_s, D_FF), BF16),
            jax.ShapeDtypeStruct((batch_s, SUBLANES, D_FF), F32), jax.ShapeDtypeStruct((batch_s, SUBLANES, D_FF), F32),
            jax.ShapeDtypeStruct((D_FF, D_MODEL), BF16),
        ],
        scratch_shapes=[pltpu.VMEM((D_MODEL, 2 * E_TN), BF16),
                        pltpu.VMEM((SUBLANES, E_TN), F32), pltpu.VMEM((SUBLANES, E_TN), F32)],
        compiler_params=_cparams(("arbitrary", "arbitrary")),
        name="ffn_up",
    )(h_p, conv_prev_p, conv_prev_p, h_s, conv_prev_s, conv_prev_s, w_up, w_up, conv_w, conv_w, conv_b, conv_b, w_down)


F_TN = 512


def _ffn_down_kernel(act_ref, w_ref, h1_ref, g_ref, b_ref, y_ref):
    for c0 in range(0, D_MODEL, F_TN):
        f = jnp.dot(act_ref[...], w_ref[:, c0:c0 + F_TN], preferred_element_type=F32)
        y_ref[:, c0:c0 + F_TN] = ALPHA * h1_ref[:, c0:c0 + F_TN] + f
    y_ref[...] = _layer_norm(y_ref[...], g_ref[...], b_ref[...])


def _ffn_down(act_p, h1_p, act_s, h1_s, w_down, g, b, tm):
    n_p, n_s = act_p.shape[0], act_s.shape[0]
    nrb = n_p // tm
    last = lambda i: jnp.minimum(i, nrb - 1)

    def kern(ap_ref, hp_ref, as_ref, hs_ref, w_ref, g_ref, b_ref, yp_ref, ys_ref):
        i = pl.program_id(0)
        pl.when(i < nrb)(lambda: _ffn_down_kernel(ap_ref, w_ref, hp_ref, g_ref, b_ref, yp_ref))
        pl.when(i == nrb)(lambda: _ffn_down_kernel(as_ref, w_ref, hs_ref, g_ref, b_ref, ys_ref))

    rows_p = lambda w: pl.BlockSpec((tm, w), lambda i: (last(i), 0))
    rows_s = lambda w: pl.BlockSpec((n_s, w), lambda i: (0, 0))
    return pl.pallas_call(
        kern,
        grid=(nrb + 1,),
        in_specs=[
            rows_p(D_FF), rows_p(D_MODEL), rows_s(D_FF), rows_s(D_MODEL),
            pl.BlockSpec((D_FF, D_MODEL), lambda i: (0, 0), pipeline_mode=pl.Buffered(1)),
            pl.BlockSpec((1, D_MODEL), lambda i: (0, 0)),
            pl.BlockSpec((1, D_MODEL), lambda i: (0, 0)),
        ],
        out_specs=[rows_p(D_MODEL), rows_s(D_MODEL)],
        out_shape=[jax.ShapeDtypeStruct((n_p, D_MODEL), F32), jax.ShapeDtypeStruct((n_s, D_MODEL), F32)],
        compiler_params=_cparams(("arbitrary",)),
        name="ffn_down",
    )(act_p, h1_p, act_s, h1_s, w_down, g, b)


def _rope_tables(pos, reps):
    half = HEAD_DIM // 2
    inv = ROPE_THETA ** (-jnp.arange(half, dtype=F32) / half)
    ang = pos.astype(F32)[:, None] * inv[None, :]
    cos, sin = jnp.cos(ang), jnp.sin(ang)
    cos128 = jnp.concatenate([cos, cos, cos, cos], axis=1)
    sin128 = jnp.concatenate([-sin, sin, -sin, sin], axis=1)
    return jnp.tile(cos128, (reps, 1)), jnp.tile(sin128, (reps, 1))


TM = 256
D_TM = 256
E_TM = 1024
F_TM = 256
RWKV_NB = 4
ATTN_NB = 2


def kernel(x_prompt, x_sample, cache_k, cache_v, state_wkv, state_shift, state_ffn_conv, ln_in_g, ln_in_b, w_in, attn_sinks, rw_mu, rw_w0, rw_w2, rw_a0, rw_a2, rw_g2, rw_k_k, rw_k_a, rw_r_k, rw_lnx_g, rw_lnx_b, w_out, ln1_g, ln1_b, ffn_w_up, ffn_conv_w, ffn_conv_b, ffn_w_down, ln2_g, ln2_b):
    l = 0
    seg = lax.broadcasted_iota(jnp.int32, (SEG, SEG), 0) // HEAD_DIM
    bd = (seg == seg.T).astype(BF16)
    row = lambda a: a.reshape(1, -1)
    rw_prm = {"mu": row(rw_mu[l]), "w0": row(rw_w0[l]), "a0": row(rw_a0[l]), "k_k": row(rw_k_k[l]),
              "k_a": row(rw_k_a[l]), "r_k": row(rw_r_k[l]), "lnx_g": row(rw_lnx_g[l]), "lnx_b": row(rw_lnx_b[l]),
              "w2": rw_w2[l].astype(BF16), "a2": rw_a2[l].astype(BF16), "g2": rw_g2[l].astype(BF16), "bd": bd}
    sinks = attn_sinks[l]
    bp, tp = x_prompt.shape[0], x_prompt.shape[1]
    bs, ts = x_sample.shape[0], x_sample.shape[1]
    n_p, n_s = bp * tp, bs * ts
    zeros = lambda *s: jnp.zeros(s, F32)

    p_p, h_p, tails_p, p_s, h_s, tails_s = _in_proj(
        x_prompt.reshape(n_p, D_MODEL), x_sample.reshape(n_s, D_MODEL), row(ln_in_g), row(ln_in_b),
        w_in[l].astype(BF16), _rope_tables(jnp.arange(tp, dtype=jnp.int32), 1),
        _rope_tables(PAST_LEN + jnp.arange(ts, dtype=jnp.int32), bs), TM, tp, ts)
    p3_p, p3_s = p_p.reshape(bp, tp, P_COLS), p_s.reshape(bs, ts, P_COLS)

    att_p = _attn_prompt(p3_p, sinks, ATTN_NB).reshape(n_p, ATT_WIDTH)
    att_s, s_k, s_v = _attn_sample(p_s, sinks, cache_k[l].reshape(bs * WINDOW, KV_WIDTH),
                                   cache_v[l].reshape(bs * WINDOW, KV_WIDTH), bs, ts)
    rw_p, p_wkv = _rwkv(p3_p, zeros(bp, 1, RW_COLS), zeros(bp, RW_HEADS, HEAD_DIM, HEAD_DIM), rw_prm, CHUNK, RWKV_NB)
    rw_s, s_wkv = _rwkv(p3_s, state_shift[l], state_wkv[l], rw_prm, ts, RWKV_NB)

    h1_p, h1b_p, h1_s, h1b_s = _out_proj(att_p, rw_p.reshape(n_p, RW_WIDTH), h_p, att_s, rw_s.reshape(n_s, RW_WIDTH), h_s,
                                         w_out[l], row(ln1_g[l]), row(ln1_b[l]), D_TM)
    act_p, cg_p, cv_p, act_s, cg_s, cv_s, w_down_b = _ffn_up(
        h1b_p, zeros(bp, CONV_W - 1, 2 * D_FF), tp, E_TM, h1b_s, state_ffn_conv[l], ts,
        ffn_w_up[l], ffn_conv_w[l], row(ffn_conv_b[l]), ffn_w_down[l])
    y_p, y_s = _ffn_down(act_p, h1_p, act_s, h1_s, w_down_b, row(ln2_g[l]), row(ln2_b[l]), F_TM)

    kv_shape = lambda b: (1, b, WINDOW, ATT_KV_HEADS, HEAD_DIM)
    p_k = p3_p[:, tp - WINDOW:, P_AK:P_AK + KV_WIDTH].reshape(kv_shape(bp))
    p_v = p3_p[:, tp - WINDOW:, P_AV:P_AV + KV_WIDTH].reshape(kv_shape(bp))
    shift_of = lambda tails, b: tails.reshape(b, -1, RW_COLS)[:, -1:, :][None]
    conv_of = lambda cg, cv: jnp.concatenate([cg[:, SUBLANES - 2:, :], cv[:, SUBLANES - 2:, :]], axis=-1)[None]
    return (y_p.reshape(bp, tp, D_MODEL), y_s.reshape(bs, ts, D_MODEL), p_k, p_v, p_wkv[None],
            shift_of(tails_p, bp), conv_of(cg_p, cv_p),
            s_k.reshape(kv_shape(bs)), s_v.reshape(kv_shape(bs)), s_wkv[None],
            shift_of(tails_s, bs), conv_of(cg_s, cv_s))
```

```python
import functools
import math

import jax
import jax.numpy as jnp
from jax import lax
from jax.experimental import pallas as pl
from jax.experimental.pallas import tpu as pltpu

D_MODEL = 2048
HEAD_DIM = 64
ATT_HEADS = 16
ATT_KV_HEADS = 2
ATT_GROUP = ATT_HEADS // ATT_KV_HEADS
ATT_WIDTH = ATT_HEADS * HEAD_DIM
KV_WIDTH = ATT_KV_HEADS * HEAD_DIM
CHUNK = 64
WINDOW = 128
ROPE_THETA = 10000.0
ATT_SCALE = HEAD_DIM ** -0.5
RW_HEADS = 16
RW_WIDTH = RW_HEADS * HEAD_DIM
DECAY_LORA = 64
AAA_LORA = 64
GATE_LORA = 160
LNX_EPS = 64e-5
ATT_COLS = ATT_WIDTH + 2 * KV_WIDTH
RW_COLS = 3 * RW_WIDTH + DECAY_LORA + AAA_LORA + GATE_LORA
D_FF = 5632
CONV_W = 3
LN_EPS = 1e-5
DEPTH = 1
ALPHA = (2 * DEPTH) ** 0.25
PAST_LEN = 1024

LORA_W = 512
P_R, P_K, P_V, P_Q = 0, 1024, 2048, 3072
P_LORA = 4096
P_AK = P_LORA + LORA_W
P_AV = P_AK + KV_WIDTH
P_COLS = P_AV + KV_WIDTH
LANES = 128
SUBLANES = 8
VMEM_LIMIT = 56 * 1024 * 1024

F32 = jnp.float32
BF16 = jnp.bfloat16


def _cparams(sem):
    return pltpu.CompilerParams(dimension_semantics=sem, vmem_limit_bytes=VMEM_LIMIT)


def _layer_norm(y, g, b):
    mu = jnp.mean(y, -1, keepdims=True)
    d = y - mu
    var = jnp.mean(d * d, -1, keepdims=True)
    return d * lax.rsqrt(var + LN_EPS) * g + b


A_TN = 512


def _rope128(x, cos, sin_signed):
    lane = lax.broadcasted_iota(jnp.int32, x.shape, 1)
    first_half = (lane % HEAD_DIM) < (HEAD_DIM // 2)
    rot = jnp.where(first_half, pltpu.roll(x, LANES - HEAD_DIM // 2, 1), pltpu.roll(x, HEAD_DIM // 2, 1))
    return x * cos + rot * sin_signed


def _is_rope_col(col):
    return (P_Q <= col < P_Q + ATT_WIDTH) or (P_AK <= col < P_AK + KV_WIDTH)


O_R = ATT_COLS
O_WD = O_R + RW_WIDTH
O_K = O_WD + DECAY_LORA
O_V = O_K + RW_WIDTH
O_AD = O_V + RW_WIDTH
O_GD = O_AD + AAA_LORA
IN_COLS = O_GD + GATE_LORA


def _in_proj_kernel(x_ref, g_ref, b_ref, w_ref, cos_ref, sin_ref, p_ref, h_ref, tail_ref, hb_ref):
    h = _layer_norm(x_ref[...], g_ref[...], b_ref[...])
    h_ref[...] = h
    hb_ref[...] = h.astype(BF16)
    nseq = tail_ref.shape[0]
    t_seq = x_ref.shape[0] // nseq

    def product(c0, width):
        acc = jnp.dot(hb_ref[...], w_ref[:, c0:c0 + width], preferred_element_type=F32)
        if c0 >= O_R:
            for s in range(nseq):
                tail_ref[s, :, c0 - O_R:c0 - O_R + width] = acc[(s + 1) * t_seq - SUBLANES:(s + 1) * t_seq, :]
        return acc

    for src, width, dst in ((0, ATT_WIDTH, P_Q), (ATT_WIDTH, 2 * KV_WIDTH, P_AK), (O_R, RW_WIDTH, P_R)):
        for c0 in range(0, width, A_TN):
            acc = product(src + c0, min(A_TN, width - c0))
            for l0 in range(0, acc.shape[1], LANES):
                col = dst + c0 + l0
                part = acc[:, l0:l0 + LANES]
                if _is_rope_col(col):
                    part = _rope128(part, cos_ref[...], sin_ref[...])
                p_ref[:, col:col + LANES] = part

    low = lax.broadcasted_iota(jnp.int32, (x_ref.shape[0], LANES), 1) < HEAD_DIM
    span = O_GD - O_WD
    rolled, first_piece = [], None
    for c0 in range(0, span, A_TN):
        acc = product(O_WD + c0, min(A_TN, span - c0))
        for l0 in range(0, acc.shape[1], LANES):
            piece = acc[:, l0:l0 + LANES]
            j = (c0 + l0) // LANES
            if j == 0:
                first_piece = piece
            rolled.append(pltpu.roll(piece, HEAD_DIM, 1))
            if j >= 1:
                m = j - 1
                dst = P_K + m * LANES if m < RW_WIDTH // LANES else P_V + (m - RW_WIDTH // LANES) * LANES
                p_ref[:, dst:dst + LANES] = jnp.where(low, rolled[m], rolled[m + 1])
            if j == span // LANES - 1:
                p_ref[:, P_LORA:P_LORA + LANES] = jnp.where(low, first_piece, piece)
    gd0 = P_LORA + DECAY_LORA + AAA_LORA
    p_ref[:, gd0:gd0 + GATE_LORA] = product(O_GD, GATE_LORA)
    p_ref[:, gd0 + GATE_LORA:P_LORA + LORA_W] = jnp.zeros((x_ref.shape[0], LORA_W - DECAY_LORA - AAA_LORA - GATE_LORA), F32)


def _in_proj(xp, xs, g, b, w_p, tables_p, tables_s, tm, seq_p, seq_s):
    n_p, n_s = xp.shape[0], xs.shape[0]
    assert n_s == tm
    nrb = n_p // tm
    nseq_s = tm // seq_s
    tb = seq_p // tm
    last = lambda i: jnp.minimum(i, nrb - 1)

    def body(ins, shared, outs, scratch):
        x_ref, cos_ref, sin_ref = ins
        g_ref, b_ref, w_ref = shared
        _in_proj_kernel(x_ref, g_ref, b_ref, w_ref, cos_ref, sin_ref, *outs, *scratch)

    def kern(xp_ref, cp_ref, sp_ref, xs_ref, cs_ref, ss_ref, g_ref, b_ref, w_ref,
             pp_ref, hp_ref, tp_ref, ps_ref, hs_ref, ts_ref, hb_ref):
        i = pl.program_id(0)
        pl.when(i < nrb)(lambda: body((xp_ref, cp_ref, sp_ref), (g_ref, b_ref, w_ref), (pp_ref, hp_ref, tp_ref), (hb_ref,)))
        pl.when(i == nrb)(lambda: body((xs_ref, cs_ref, ss_ref), (g_ref, b_ref, w_ref), (ps_ref, hs_ref, ts_ref), (hb_ref,)))

    const2 = lambda shape: pl.BlockSpec(shape, lambda i: (0, 0))
    return pl.pallas_call(
        kern,
        grid=(nrb + 1,),
        in_specs=[
            pl.BlockSpec((tm, D_MODEL), lambda i: (last(i), 0)),
            pl.BlockSpec((tm, LANES), lambda i: (last(i) % tb, 0)),
            pl.BlockSpec((tm, LANES), lambda i: (last(i) % tb, 0)),
            const2((tm, D_MODEL)), const2((tm, LANES)), const2((tm, LANES)),
            const2((1, D_MODEL)), const2((1, D_MODEL)),
            pl.BlockSpec((D_MODEL, IN_COLS), lambda i: (0, 0), pipeline_mode=pl.Buffered(1)),
        ],
        out_specs=[
            pl.BlockSpec((tm, P_COLS), lambda i: (last(i), 0)),
            pl.BlockSpec((tm, D_MODEL), lambda i: (last(i), 0)),
            pl.BlockSpec((1, SUBLANES, RW_COLS), lambda i: (last(i), 0, 0)),
            const2((tm, P_COLS)), const2((tm, D_MODEL)),
            pl.BlockSpec((nseq_s, SUBLANES, RW_COLS), lambda i: (0, 0, 0)),
        ],
        out_shape=[
            jax.ShapeDtypeStruct((n_p, P_COLS), F32),
            jax.ShapeDtypeStruct((n_p, D_MODEL), F32),
            jax.ShapeDtypeStruct((nrb, SUBLANES, RW_COLS), F32),
            jax.ShapeDtypeStruct((n_s, P_COLS), F32),
            jax.ShapeDtypeStruct((n_s, D_MODEL), F32),
            jax.ShapeDtypeStruct((nseq_s, SUBLANES, RW_COLS), F32),
        ],
        scratch_shapes=[pltpu.VMEM((tm, D_MODEL), BF16)],
        compiler_params=_cparams(("arbitrary",)),
        name="in_proj",
    )(xp, *tables_p, xs, *tables_s, g, b, w_p)


def _attend(q, k, v, valid, sink_ref):
    r = q.shape[0]
    outs = []
    for kv in range(ATT_KV_HEADS):
        kh = k[:, kv * HEAD_DIM:(kv + 1) * HEAD_DIM].astype(BF16)
        vh = v[:, kv * HEAD_DIM:(kv + 1) * HEAD_DIM].astype(BF16)
        heads = [kv * ATT_GROUP + g for g in range(ATT_GROUP)]
        qs = jnp.concatenate([q[:, h * HEAD_DIM:(h + 1) * HEAD_DIM] for h in heads], axis=0)
        sink = jnp.concatenate([jnp.full((r, 1), sink_ref[h], F32) for h in heads], axis=0)
        s = lax.dot_general(qs.astype(BF16), kh, (((1,), (1,)), ((), ())), preferred_element_type=F32) * ATT_SCALE
        if valid is not None:
            s = jnp.where(valid, s, -jnp.inf)
        m = jnp.maximum(jnp.max(s, -1, keepdims=True), sink)
        p = jnp.exp(s - m)
        den = jnp.sum(p, -1, keepdims=True) + jnp.exp(sink - m)
        o = jnp.dot(p.astype(BF16), vh, preferred_element_type=F32) / den
        outs.extend(o[g * r:(g + 1) * r] for g in range(ATT_GROUP))
    return jnp.concatenate(outs, axis=1)


LOG2_E = math.log2(math.e)
QB = 2 * CHUNK


def _attn_prompt_kernel(sink_ref, q_ref, kp_ref, kc_ref, vp_ref, vc_ref, o_ref):
    m = pl.program_id(1)
    nseq = q_ref.shape[0]
    span = 2 * QB
    lane = lax.broadcasted_iota(jnp.int32, (span, KV_WIDTH), 1)
    row_c = 2 + lax.broadcasted_iota(jnp.int32, (QB, span), 0) // CHUNK
    col_c = lax.broadcasted_iota(jnp.int32, (QB, span), 1) // CHUNK
    back = row_c - col_c
    valid = jnp.logical_and(jnp.logical_and(back >= 0, back <= 2), 2 * m - 2 + col_c >= 0)
    out_lane = lax.broadcasted_iota(jnp.int32, (QB, PAIR), 1)

    def dup(x, kv):
        sw = pltpu.roll(x, HEAD_DIM, 1)
        return jnp.where(lane < HEAD_DIM, x, sw) if kv == 0 else jnp.where(lane < HEAD_DIM, sw, x)

    k_st, v_st = [], []
    for b in range(nseq):
        k_all = jnp.concatenate([kp_ref[b], kc_ref[b]], axis=0)
        v_all = jnp.concatenate([vp_ref[b], vc_ref[b]], axis=0)
        k_st.append([_stack_heads(dup(k_all, kv), HEAD_DIM).astype(BF16) for kv in range(ATT_KV_HEADS)])
        v_st.append([_stack_heads(dup(v_all, kv), HEAD_DIM).astype(BF16) for kv in range(ATT_KV_HEADS)])
    pairs = [(b, j) for b in range(nseq) for j in range(ATT_HEADS // 2)]
    kv_of = lambda j: (2 * j) // ATT_GROUP
    s = [lax.dot_general((q_ref[b, :, j * PAIR:(j + 1) * PAIR] * (ATT_SCALE * LOG2_E)).astype(BF16), k_st[b][kv_of(j)],
                         (((1,), (1,)), ((), ())), preferred_element_type=F32) for b, j in pairs]
    probs, dens = [], []
    for idx, (b, j) in enumerate(pairs):
        halves, den = [], []
        for e in range(2):
            sink = sink_ref[2 * j + e] * LOG2_E
            sh = jnp.where(valid, s[idx][:, e * span:(e + 1) * span], -jnp.inf)
            mx = jnp.maximum(jnp.max(sh, -1, keepdims=True), sink)
            ph = jnp.exp2(sh - mx)
            halves.append(ph)
            den.append(jnp.sum(ph, -1, keepdims=True) + jnp.exp2(sink - mx))
        probs.append(jnp.concatenate(halves, axis=1).astype(BF16))
        dens.append(jnp.where(out_lane < HEAD_DIM, den[0], den[1]))
    o = [jnp.dot(probs[idx], v_st[b][kv_of(j)], preferred_element_type=F32) / dens[idx]
         for idx, (b, j) in enumerate(pairs)]
    n_pairs = ATT_HEADS // 2
    for b in range(nseq):
        o_ref[b] = jnp.concatenate(o[b * n_pairs:(b + 1) * n_pairs], axis=1).astype(o_ref.dtype)


def _attn_prompt(p3, sinks, nseq):
    batch, seq = p3.shape[0], p3.shape[1]
    nb = seq // QB
    kcol, vcol = P_AK // KV_WIDTH, P_AV // KV_WIDTH
    prev = lambda col: pl.BlockSpec((nseq, QB, KV_WIDTH), lambda g, m: (g, jnp.maximum(m - 1, 0), col))
    cur = lambda col: pl.BlockSpec((nseq, QB, KV_WIDTH), lambda g, m: (g, m, col))
    return pl.pallas_call(
        _attn_prompt_kernel,
        grid=(batch // nseq, nb),
        in_specs=[
            pl.BlockSpec(memory_space=pltpu.SMEM),
            pl.BlockSpec((nseq, QB, ATT_WIDTH), lambda g, m: (g, m, P_Q // ATT_WIDTH)),
            prev(kcol), cur(kcol), prev(vcol), cur(vcol),
        ],
        out_specs=pl.BlockSpec((nseq, QB, ATT_WIDTH), lambda g, m: (g, m, 0)),
        out_shape=jax.ShapeDtypeStruct((batch, seq, ATT_WIDTH), BF16),
        compiler_params=_cparams(("arbitrary", "arbitrary")),
        name="attn_prompt",
    )(sinks, p3, p3, p3, p3, p3)


def _attn_sample_kernel(sink_ref, q_ref, kn_ref, vn_ref, kc_ref, vc_ref, o_ref, kw_ref, vw_ref):
    t = q_ref.shape[0]
    k = jnp.concatenate([kc_ref[...], kn_ref[...]], axis=0)
    v = jnp.concatenate([vc_ref[...], vn_ref[...]], axis=0)
    o_ref[...] = _attend(q_ref[...], k, v, None, sink_ref).astype(o_ref.dtype)
    kw_ref[...] = k[t:]
    vw_ref[...] = v[t:]


def _attn_sample(p, sinks, cache_k, cache_v, batch, seq):
    kcol, vcol = P_AK // KV_WIDTH, P_AV // KV_WIDTH
    return pl.pallas_call(
        _attn_sample_kernel,
        grid=(batch,),
        in_specs=[
            pl.BlockSpec(memory_space=pltpu.SMEM),
            pl.BlockSpec((seq, ATT_WIDTH), lambda b: (b, P_Q // ATT_WIDTH)),
            pl.BlockSpec((seq, KV_WIDTH), lambda b: (b, kcol)),
            pl.BlockSpec((seq, KV_WIDTH), lambda b: (b, vcol)),
            pl.BlockSpec((WINDOW, KV_WIDTH), lambda b: (b, 0)),
            pl.BlockSpec((WINDOW, KV_WIDTH), lambda b: (b, 0)),
        ],
        out_specs=[
            pl.BlockSpec((seq, ATT_WIDTH), lambda b: (b, 0)),
            pl.BlockSpec((WINDOW, KV_WIDTH), lambda b: (b, 0)),
            pl.BlockSpec((WINDOW, KV_WIDTH), lambda b: (b, 0)),
        ],
        out_shape=[
            jax.ShapeDtypeStruct((batch * seq, ATT_WIDTH), BF16),
            jax.ShapeDtypeStruct((batch * WINDOW, KV_WIDTH), F32),
            jax.ShapeDtypeStruct((batch * WINDOW, KV_WIDTH), F32),
        ],
        compiler_params=_cparams(("arbitrary",)),
        name="attn_sample",
    )(sinks, p, p, p, cache_k, cache_v)


SEG = 256
EXP_NEG_HALF = math.exp(-0.5)
PAIR = 2 * HEAD_DIM
N_PAIRS = RW_HEADS // 2


def _split2(x):
    hi = x.astype(BF16)
    lo = (x - hi.astype(F32)).astype(BF16)
    return hi, lo


def _seg_sum(x, bd):
    c = x.shape[0]
    xb = x.astype(BF16)
    stack = jnp.concatenate([xb[:, s:s + SEG] for s in range(0, RW_WIDTH, SEG)], axis=0)
    r = jnp.dot(stack, bd, preferred_element_type=F32)
    return jnp.concatenate([r[i * c:(i + 1) * c] for i in range(RW_WIDTH // SEG)], axis=1)


def _shift_rows(x, prev_row):
    row = lax.broadcasted_iota(jnp.int32, x.shape, 0)
    return jnp.where(row == 0, prev_row, pltpu.roll(x, 1, 0))


def _dotb(a, b):
    return jnp.dot(a.astype(BF16), b.astype(BF16), preferred_element_type=F32)


def _dotb_nt(a, b):
    return lax.dot_general(a.astype(BF16), b.astype(BF16), (((1,), (1,)), ((), ())), preferred_element_type=F32)


def _dotb_tn(a, b):
    return lax.dot_general(a.astype(BF16), b.astype(BF16), (((0,), (0,)), ((), ())), preferred_element_type=F32)


def _stack_heads(x, half):
    lane = lax.broadcasted_iota(jnp.int32, x.shape, 1)
    first = (lane % (2 * half)) < half
    zero = jnp.zeros_like(x)
    return jnp.concatenate([jnp.where(first, x, zero), jnp.where(first, zero, x)], axis=0)


def _stack_heads_bf16(x, half):
    return _stack_heads(x.astype(BF16), half)


def _rwkv_pairs(ab, rb, bb, kb, v, gam_c, s_bd, n_double):
    c = ab[0].shape[0]
    pairs = range(len(ab))
    t_idx = lax.broadcasted_iota(jnp.int32, (c, 2 * c), 0)
    s_idx = lax.broadcasted_iota(jnp.int32, (c, 2 * c), 1) % c
    strict, incl = s_idx < t_idx, s_idx <= t_idx
    g = [_dotb_nt(jnp.concatenate([ab[p], rb[p]], axis=0),
                  jnp.concatenate([_stack_heads_bf16(bb[p], HEAD_DIM), _stack_heads_bf16(kb[p], HEAD_DIM)], axis=0))
         for p in pairs]
    l_ak = [jnp.where(strict, g[p][:c, 2 * c:], 0.0) for p in pairs]
    m_rbk = [jnp.concatenate([jnp.where(incl, g[p][c:, :2 * c], 0.0), jnp.where(incl, g[p][c:, 2 * c:], 0.0)], axis=1)
             for p in pairs]
    lp = [jnp.where(strict, g[p][:c, :2 * c], 0.0) for p in pairs]
    v_st = [_stack_heads_bf16(v[p], HEAD_DIM) for p in pairs]
    y0 = [jnp.concatenate([ab[p], _dotb(l_ak[p], v_st[p])], axis=1) for p in pairs]
    n_acc = list(lp)
    lk_b = [lp[p].astype(BF16) for p in pairs]
    lk = [_dotb(lk_b[p], _stack_heads(lk_b[p], c)) for p in pairs]
    for k in range(1, n_double):
        n_st = [_stack_heads_bf16(n_acc[p], c) for p in pairs]
        lk_b = [lk[p].astype(BF16) for p in pairs]
        if k + 1 < n_double:
            z = [_dotb(lk_b[p], jnp.concatenate([n_st[p], _stack_heads(lk_b[p], c)], axis=1)) for p in pairs]
            n_acc = [n_acc[p] + lk[p] + z[p][:, :2 * c] for p in pairs]
            lk = [z[p][:, 2 * c:] for p in pairs]
        else:
            n_acc = [n_acc[p] + lk[p] + _dotb(lk_b[p], n_st[p]) for p in pairs]
    y = [y0[p] + _dotb(n_acc[p], _stack_heads_bf16(y0[p], HEAD_DIM)) for p in pairs]
    ya = [y[p][:, :PAIR] for p in pairs]
    yu = [y[p][:, PAIR:] for p in pairs]
    ou = []
    for p in pairs:
        lhs = jnp.concatenate([rb[p], ya[p]], axis=0).astype(BF16)
        ou.append(lax.dot_general(jnp.concatenate([lhs, lhs], axis=1), jnp.concatenate(_split2(s_bd[p]), axis=1),
                                  (((1,), (1,)), ((), ())), preferred_element_type=F32))
    u = [ou[p][c:] + yu[p] for p in pairs]
    o = [ou[p][:c] + _dotb(m_rbk[p], jnp.concatenate([_stack_heads_bf16(u[p], HEAD_DIM), v_st[p]], axis=0))
         for p in pairs]
    row_h = lax.broadcasted_iota(jnp.int32, (PAIR, PAIR), 0) // HEAD_DIM
    col_h = lax.broadcasted_iota(jnp.int32, (PAIR, PAIR), 1) // HEAD_DIM
    same = row_h == col_h
    d_t = [jnp.where(same, _dotb_tn(jnp.concatenate([u[p], v[p]], axis=0),
                                    jnp.concatenate([bb[p], kb[p]], axis=0)), 0.0) for p in pairs]
    s_new = [(s_bd[p] + d_t[p]) * gam_c[p] for p in pairs]
    return o, s_new


def _split_rw_rows(x):
    o_wd, o_gd = RW_WIDTH, RW_COLS - GATE_LORA
    low = lax.broadcasted_iota(jnp.int32, (x.shape[0], LANES), 1) < HEAD_DIM
    pieces = [x[:, c0:c0 + LANES] for c0 in range(o_wd, o_gd, LANES)]
    rolled = [pltpu.roll(p, HEAD_DIM, 1) for p in pieces]
    merged = [jnp.where(low, rolled[m], rolled[m + 1]) for m in range(len(pieces) - 1)]
    n = RW_WIDTH // LANES
    return (x[:, :RW_WIDTH], jnp.concatenate(merged[:n], axis=1), jnp.concatenate(merged[n:], axis=1),
            jnp.where(low, pieces[0], pieces[-1]), x[:, o_gd:])


def _rwkv_kernel(xr_ref, xk_ref, xv_ref, xl_ref, sh_ref, mu_ref, w0_ref, a0_ref, kk_ref, ka_ref, rk_ref,
                 lg_ref, lb_ref, w2_ref, a2_ref, g2_ref, bd_ref, st0_ref,
                 out_ref, st_ref,
                 s_scr, pr_scr, pk_scr, pv_scr, pw_scr, pg_scr):
    c = pl.program_id(1)
    nc = pl.num_programs(1)
    nb, C = xr_ref.shape[0], xr_ref.shape[1]
    n_double = int(math.log2(C))
    prev_scr = (pr_scr, pk_scr, pv_scr, pw_scr, pg_scr)

    @pl.when(c == 0)
    def _():
        s_scr[...] = jnp.zeros_like(s_scr)
        for b in range(nb):
            for h in range(RW_HEADS):
                off = (h % 2) * HEAD_DIM
                s_scr[b * N_PAIRS + h // 2, off:off + HEAD_DIM, off:off + HEAD_DIM] = st0_ref[b, h]
            for scr, row in zip(prev_scr, _split_rw_rows(sh_ref[b])):
                scr[b, 0:1, :] = row

    def shifted(load, p_scr, mu):
        parts = []
        for b in range(nb):
            x = load(b)
            prev = _shift_rows(x, p_scr[b, 0:1, :])
            p_scr[b, 0:1, :] = x[C - 1:C, :]
            parts.append(x + (prev - x) * mu)
        return jnp.concatenate(parts, axis=0)

    mu_r, mu_k, mu_v, mu_w, mu_g = _split_rw_rows(mu_ref[...])
    wa_w = DECAY_LORA + AAA_LORA
    r = shifted(lambda b: xr_ref[b], pr_scr, mu_r)
    kr = shifted(lambda b: xk_ref[b], pk_scr, mu_k)
    v = shifted(lambda b: xv_ref[b], pv_scr, mu_v)
    xw = shifted(lambda b: xl_ref[b, :, :wa_w], pw_scr, mu_w)
    gd = shifted(lambda b: xl_ref[b, :, wa_w:wa_w + GATE_LORA], pg_scr, mu_g)
    wd = xw[:, 0:DECAY_LORA]
    ad = xw[:, DECAY_LORA:wa_w]

    lw = (-EXP_NEG_HALF * LOG2_E) * jax.nn.sigmoid(w0_ref[...] + _dotb(jnp.tanh(wd), w2_ref[...]))
    a = jax.nn.sigmoid(a0_ref[...] + _dotb(ad, a2_ref[...]))
    g = _dotb(jax.nn.sigmoid(gd), g2_ref[...])
    bd = bd_ref[...]
    kk = kr * kk_ref[...]
    kk = kk * lax.rsqrt(jnp.maximum(_seg_sum(kk * kk, bd), 1e-24))
    k2 = kr * (1.0 + (a - 1.0) * ka_ref[...])

    rows = nb * C
    ti = lax.broadcasted_iota(jnp.int32, (rows, rows), 0)
    si = lax.broadcasted_iota(jnp.int32, (rows, rows), 1)
    tri = jnp.logical_and(si <= ti, si // C == ti // C).astype(BF16)
    cs2 = jnp.dot(tri, jnp.concatenate(_split2(lw), axis=1), preferred_element_type=F32)
    cs = cs2[:, :RW_WIDTH] + cs2[:, RW_WIDTH:]
    gam = jnp.exp2(cs)
    ginv = jnp.exp2(-cs)
    ab = -kk * jnp.exp2(cs - lw)
    rb = r * gam
    bb = kk * a * ginv
    kb = k2 * ginv

    def per_pair(x, r0, r1):
        return [x[b * C + r0:b * C + r1, p * PAIR:(p + 1) * PAIR] for b in range(nb) for p in range(N_PAIRS)]

    chains = nb * N_PAIRS
    outs, s_new = _rwkv_pairs(per_pair(ab, 0, C), per_pair(rb, 0, C), per_pair(bb, 0, C), per_pair(kb, 0, C),
                              per_pair(v, 0, C), per_pair(gam, C - 1, C), [s_scr[q] for q in range(chains)], n_double)
    for q in range(chains):
        s_scr[q] = s_new[q]
    o = jnp.concatenate([jnp.concatenate(outs[b * N_PAIRS:(b + 1) * N_PAIRS], axis=1) for b in range(nb)], axis=0)

    mo = _seg_sum(o, bd) * (1.0 / HEAD_DIM)
    d = o - mo
    vo = _seg_sum(d * d, bd) * (1.0 / HEAD_DIM)
    on = d * lax.rsqrt(vo + LNX_EPS) * lg_ref[...] + lb_ref[...]
    bonus = _seg_sum(r * k2 * rk_ref[...], bd) * v
    res = ((on + bonus) * g).astype(out_ref.dtype)
    for b in range(nb):
        out_ref[b] = res[b * C:(b + 1) * C, :]

    @pl.when(c == nc - 1)
    def _():
        for b in range(nb):
            for h in range(RW_HEADS):
                off = (h % 2) * HEAD_DIM
                st_ref[b, h] = s_scr[b * N_PAIRS + h // 2, off:off + HEAD_DIM, off:off + HEAD_DIM]


def _rwkv(p3, state_shift, state0, prm, chunk, nb):
    batch, seq = p3.shape[0], p3.shape[1]
    nc = seq // chunk
    row = lambda w: pl.BlockSpec((1, w), lambda g, c: (0, 0))
    xspec = lambda w, col: pl.BlockSpec((nb, chunk, w), lambda g, c: (g, c, col))
    full = lambda a: pl.BlockSpec(a.shape, lambda g, c: (0,) * a.ndim)
    stspec = pl.BlockSpec((nb, RW_HEADS, HEAD_DIM, HEAD_DIM), lambda g, c: (g, 0, 0, 0))
    return pl.pallas_call(
        _rwkv_kernel,
        grid=(batch // nb, nc),
        in_specs=[
            xspec(RW_WIDTH, P_R // RW_WIDTH), xspec(RW_WIDTH, P_K // RW_WIDTH), xspec(RW_WIDTH, P_V // RW_WIDTH),
            xspec(LORA_W, P_LORA // LORA_W),
            pl.BlockSpec((nb, 1, RW_COLS), lambda g, c: (g, 0, 0)),
            row(RW_COLS),
            row(RW_WIDTH), row(RW_WIDTH), row(RW_WIDTH), row(RW_WIDTH), row(RW_WIDTH),
            row(RW_WIDTH), row(RW_WIDTH),
            full(prm["w2"]), full(prm["a2"]), full(prm["g2"]), full(prm["bd"]),
            stspec,
        ],
        out_specs=[
            pl.BlockSpec((nb, chunk, RW_WIDTH), lambda g, c: (g, c, 0)),
            stspec,
        ],
        out_shape=[
            jax.ShapeDtypeStruct((batch, seq, RW_WIDTH), BF16),
            jax.ShapeDtypeStruct((batch, RW_HEADS, HEAD_DIM, HEAD_DIM), F32),
        ],
        scratch_shapes=[
            pltpu.VMEM((nb * N_PAIRS, PAIR, PAIR), F32),
            pltpu.VMEM((nb, SUBLANES, RW_WIDTH), F32), pltpu.VMEM((nb, SUBLANES, RW_WIDTH), F32),
            pltpu.VMEM((nb, SUBLANES, RW_WIDTH), F32),
            pltpu.VMEM((nb, SUBLANES, DECAY_LORA + AAA_LORA), F32), pltpu.VMEM((nb, SUBLANES, GATE_LORA), F32),
        ],
        compiler_params=_cparams(("arbitrary", "arbitrary")),
        name="rwkv",
    )(p3, p3, p3, p3, state_shift, prm["mu"], prm["w0"], prm["a0"], prm["k_k"], prm["k_a"], prm["r_k"],
      prm["lnx_g"], prm["lnx_b"], prm["w2"], prm["a2"], prm["g2"], prm["bd"], state0)


D_TN = 512


def _out_proj_body(att_ref, rw_ref, h_ref, g_ref, b_ref, h1_ref, h1b_ref, w_scr):
    for c0 in range(0, D_MODEL, D_TN):
        m = jnp.dot(att_ref[...], w_scr[:ATT_WIDTH, c0:c0 + D_TN], preferred_element_type=F32)
        m = m + jnp.dot(rw_ref[...], w_scr[ATT_WIDTH:, c0:c0 + D_TN], preferred_element_type=F32)
        h1_ref[:, c0:c0 + D_TN] = ALPHA * h_ref[:, c0:c0 + D_TN] + m
    y = _layer_norm(h1_ref[...], g_ref[...], b_ref[...])
    h1_ref[...] = y
    h1b_ref[...] = y.astype(BF16)


def _out_proj(att_p, rw_p, h_p, att_s, rw_s, h_s, w_out, g, b, tm):
    n_p, n_s = att_p.shape[0], att_s.shape[0]
    nrb = n_p // tm
    last = lambda i: jnp.minimum(i, nrb - 1)

    def kern(ap_ref, rp_ref, hp_ref, as_ref, rs_ref, hs_ref, wa_ref, wb_ref, g_ref, b_ref,
             h1p_ref, h1bp_ref, h1s_ref, h1bs_ref, w_scr):
        i = pl.program_id(0)

        @pl.when(i == 0)
        def _():
            w_scr[:ATT_WIDTH, :] = wa_ref[...].astype(BF16)
            w_scr[ATT_WIDTH:, :] = wb_ref[...].astype(BF16)

        pl.when(i < nrb)(lambda: _out_proj_body(ap_ref, rp_ref, hp_ref, g_ref, b_ref, h1p_ref, h1bp_ref, w_scr))
        pl.when(i == nrb)(lambda: _out_proj_body(as_ref, rs_ref, hs_ref, g_ref, b_ref, h1s_ref, h1bs_ref, w_scr))

    rows_p = lambda w: pl.BlockSpec((tm, w), lambda i: (last(i), 0))
    rows_s = lambda w: pl.BlockSpec((n_s, w), lambda i: (0, 0))
    return pl.pallas_call(
        kern,
        grid=(nrb + 1,),
        in_specs=[
            rows_p(ATT_WIDTH), rows_p(RW_WIDTH), rows_p(D_MODEL),
            rows_s(ATT_WIDTH), rows_s(RW_WIDTH), rows_s(D_MODEL),
            pl.BlockSpec((ATT_WIDTH, D_MODEL), lambda i: (0, 0), pipeline_mode=pl.Buffered(1)),
            pl.BlockSpec((RW_WIDTH, D_MODEL), lambda i: (1, 0), pipeline_mode=pl.Buffered(1)),
            pl.BlockSpec((1, D_MODEL), lambda i: (0, 0)),
            pl.BlockSpec((1, D_MODEL), lambda i: (0, 0)),
        ],
        out_specs=[rows_p(D_MODEL), rows_p(D_MODEL), rows_s(D_MODEL), rows_s(D_MODEL)],
        out_shape=[
            jax.ShapeDtypeStruct((n_p, D_MODEL), F32), jax.ShapeDtypeStruct((n_p, D_MODEL), BF16),
            jax.ShapeDtypeStruct((n_s, D_MODEL), F32), jax.ShapeDtypeStruct((n_s, D_MODEL), BF16),
        ],
        scratch_shapes=[pltpu.VMEM((ATT_WIDTH + RW_WIDTH, D_MODEL), BF16)],
        compiler_params=_cparams(("arbitrary",)),
        name="out_proj",
    )(att_p, rw_p, h_p, att_s, rw_s, h_s, w_out, w_out, g, b)


E_TN = 512
E_NCB = D_FF // E_TN
E_SUB = 1024
SQRT_HALF = math.sqrt(0.5)


def _ffn_up_body(bps, nseq, i, h_ref, w_scr, cwg_ref, cwv_ref, cbg_ref, cbv_ref, sg_ref, sv_ref,
                 act_ref, og_ref, ov_ref, cg_scr, cv_scr):
    tm = h_ref.shape[0]
    sub = min(tm, E_SUB)
    n_sub = tm // sub
    t_seq = tm // nseq
    row8 = lax.broadcasted_iota(jnp.int32, (SUBLANES, E_TN), 0)

    def products(r0):
        u = jnp.dot(h_ref[r0:r0 + sub, :], w_scr[...], preferred_element_type=F32)
        return u[:, :E_TN], u[:, E_TN:]

    def conv(u, prevs, cw_ref, cb_ref):
        sh1, sh2 = pltpu.roll(u, 1, 0), pltpu.roll(u, 2, 0)
        for r, prev in prevs:
            p2, p1 = prev[0:1, :], prev[1:2, :]
            fix1 = jnp.where(row8 == 0, p1, sh1[r:r + SUBLANES, :])
            fix2 = jnp.where(row8 == 0, p2, jnp.where(row8 == 1, p1, sh2[r:r + SUBLANES, :]))
            head = lambda x: [x[:r, :]] if r > 0 else []
            tail = lambda x: [x[r + SUBLANES:, :]] if r + SUBLANES < x.shape[0] else []
            sh1 = jnp.concatenate(head(sh1) + [fix1] + tail(sh1), axis=0)
            sh2 = jnp.concatenate(head(sh2) + [fix2] + tail(sh2), axis=0)
        cw = cw_ref[...]
        c = cb_ref[...] + sh2 * cw[0:1, :]
        c = c + sh1 * cw[1:2, :]
        return c + u * cw[2:3, :]

    def epilogue(r0, ug, uv, prev_g, prev_v):
        gate = conv(ug, prev_g, cwg_ref, cbg_ref)
        val = conv(uv, prev_v, cwv_ref, cbv_ref)
        gelu = 0.5 * gate * (1.0 + lax.erf(gate * SQRT_HALF))
        act_ref[r0:r0 + sub, :] = (gelu * val).astype(act_ref.dtype)

    if nseq > 1:
        ug, uv = products(0)
        epilogue(0, ug, uv, [(s * t_seq, sg_ref[s]) for s in range(nseq)],
                 [(s * t_seq, sv_ref[s]) for s in range(nseq)])
        for s in range(nseq):
            og_ref[s] = ug[(s + 1) * t_seq - SUBLANES:(s + 1) * t_seq, :]
            ov_ref[s] = uv[(s + 1) * t_seq - SUBLANES:(s + 1) * t_seq, :]
    else:
        first = (i % bps) == 0
        prev_g = jnp.where(first, sg_ref[0], cg_scr[SUBLANES - 2:SUBLANES, :])
        prev_v = jnp.where(first, sv_ref[0], cv_scr[SUBLANES - 2:SUBLANES, :])
        nxt = products(0)
        for k in range(n_sub):
            ug, uv = nxt
            if k + 1 < n_sub:
                nxt = products((k + 1) * sub)
            epilogue(k * sub, ug, uv, [(0, prev_g)], [(0, prev_v)])
            prev_g, prev_v = ug[sub - 2:sub, :], uv[sub - 2:sub, :]
        cg_scr[...] = ug[sub - SUBLANES:sub, :]
        cv_scr[...] = uv[sub - SUBLANES:sub, :]
        og_ref[0] = ug[sub - SUBLANES:sub, :]
        ov_ref[0] = uv[sub - SUBLANES:sub, :]


def _ffn_up(h_p, conv_prev_p, seq_p, tm_p, h_s, conv_prev_s, seq_s, w_up, conv_w, conv_b, w_down):
    n_p, n_s = h_p.shape[0], h_s.shape[0]
    batch_p, batch_s = n_p // seq_p, n_s // seq_s
    bps = seq_p // tm_p
    nrb = n_p // tm_p
    slab = D_FF // (E_NCB * nrb)
    last = lambda i: jnp.maximum(i - 1, 0)

    def kern(hp_ref, sgp_ref, svp_ref, hs_ref, sgs_ref, svs_ref, wg_ref, wv_ref, cwg_ref, cwv_ref, cbg_ref, cbv_ref,
             wd_ref, actp_ref, ogp_ref, ovp_ref, acts_ref, ogs_ref, ovs_ref, wdb_ref, w_scr, cg_scr, cv_scr):
        i = pl.program_id(1)

        @pl.when(i == 0)
        def _():
            w_scr[:, :E_TN] = wg_ref[...].astype(BF16)
            w_scr[:, E_TN:] = wv_ref[...].astype(BF16)
            _ffn_up_body(1, batch_s, i, hs_ref, w_scr, cwg_ref, cwv_ref, cbg_ref, cbv_ref, sgs_ref, svs_ref,
                         acts_ref, ogs_ref, ovs_ref, cg_scr, cv_scr)

        @pl.when(i > 0)
        def _():
            wdb_ref[...] = wd_ref[...].astype(BF16)
            _ffn_up_body(bps, 1, i - 1, hp_ref, w_scr, cwg_ref, cwv_ref, cbg_ref, cbv_ref, sgp_ref, svp_ref,
                         actp_ref, ogp_ref, ovp_ref, cg_scr, cv_scr)

    col = lambda shape, off: pl.BlockSpec(shape, lambda j, i: (0,) * (len(shape) - 1) + (j + off,))
    st_p = lambda off: pl.BlockSpec((1, CONV_W - 1, E_TN), lambda j, i: (last(i) // bps, 0, j + off))
    st_s = lambda off: pl.BlockSpec((batch_s, CONV_W - 1, E_TN), lambda j, i: (0, 0, j + off))
    tail_p = pl.BlockSpec((1, SUBLANES, E_TN), lambda j, i: (last(i) // bps, 0, j))
    tail_s = pl.BlockSpec((batch_s, SUBLANES, E_TN), lambda j, i: (0, 0, j))
    wd_spec = pl.BlockSpec((slab, D_MODEL), lambda j, i: (j * nrb + last(i), 0))
    return pl.pallas_call(
        kern,
        grid=(E_NCB, nrb + 1),
        in_specs=[
            pl.BlockSpec((tm_p, D_MODEL), lambda j, i: (last(i), 0)), st_p(0), st_p(E_NCB),
            pl.BlockSpec((n_s, D_MODEL), lambda j, i: (0, 0)), st_s(0), st_s(E_NCB),
            col((D_MODEL, E_TN), 0), col((D_MODEL, E_TN), E_NCB),
            col((CONV_W, E_TN), 0), col((CONV_W, E_TN), E_NCB),
            col((1, E_TN), 0), col((1, E_TN), E_NCB),
            wd_spec,
        ],
        out_specs=[
            pl.BlockSpec((tm_p, E_TN), lambda j, i: (last(i), j)), tail_p, tail_p,
            pl.BlockSpec((n_s, E_TN), lambda j, i: (0, j)), tail_s, tail_s,
            wd_spec,
        ],
        out_shape=[
            jax.ShapeDtypeStruct((n_p, D_FF), BF16),
            jax.ShapeDtypeStruct((batch_p, SUBLANES, D_FF), F32), jax.ShapeDtypeStruct((batch_p, SUBLANES, D_FF), F32),
            jax.ShapeDtypeStruct((n_s, D_FF), BF16),
            jax.ShapeDtypeStruct((batch_s, SUBLANES, D_FF), F32), jax.ShapeDtypeStruct((batch_s, SUBLANES, D_FF), F32),
            jax.ShapeDtypeStruct((D_FF, D_MODEL), BF16),
        ],
        scratch_shapes=[pltpu.VMEM((D_MODEL, 2 * E_TN), BF16),
                        pltpu.VMEM((SUBLANES, E_TN), F32), pltpu.VMEM((SUBLANES, E_TN), F32)],
        compiler_params=_cparams(("arbitrary", "arbitrary")),
        name="ffn_up",
    )(h_p, conv_prev_p, conv_prev_p, h_s, conv_prev_s, conv_prev_s, w_up, w_up, conv_w, conv_w, conv_b, conv_b, w_down)


F_TN = 512


def _ffn_down_kernel(act_ref, w_ref, h1_ref, g_ref, b_ref, y_ref):
    for c0 in range(0, D_MODEL, F_TN):
        f = jnp.dot(act_ref[...], w_ref[:, c0:c0 + F_TN], preferred_element_type=F32)
        y_ref[:, c0:c0 + F_TN] = ALPHA * h1_ref[:, c0:c0 + F_TN] + f
    y_ref[...] = _layer_norm(y_ref[...], g_ref[...], b_ref[...])


def _ffn_down(act_p, h1_p, act_s, h1_s, w_down, g, b, tm):
    n_p, n_s = act_p.shape[0], act_s.shape[0]
    nrb = n_p // tm
    last = lambda i: jnp.minimum(i, nrb - 1)

    def kern(ap_ref, hp_ref, as_ref, hs_ref, w_ref, g_ref, b_ref, yp_ref, ys_ref):
        i = pl.program_id(0)
        pl.when(i < nrb)(lambda: _ffn_down_kernel(ap_ref, w_ref, hp_ref, g_ref, b_ref, yp_ref))
        pl.when(i == nrb)(lambda: _ffn_down_kernel(as_ref, w_ref, hs_ref, g_ref, b_ref, ys_ref))

    rows_p = lambda w: pl.BlockSpec((tm, w), lambda i: (last(i), 0))
    rows_s = lambda w: pl.BlockSpec((n_s, w), lambda i: (0, 0))
    return pl.pallas_call(
        kern,
        grid=(nrb + 1,),
        in_specs=[
            rows_p(D_FF), rows_p(D_MODEL), rows_s(D_FF), rows_s(D_MODEL),
            pl.BlockSpec((D_FF, D_MODEL), lambda i: (0, 0), pipeline_mode=pl.Buffered(1)),
            pl.BlockSpec((1, D_MODEL), lambda i: (0, 0)),
            pl.BlockSpec((1, D_MODEL), lambda i: (0, 0)),
        ],
        out_specs=[rows_p(D_MODEL), rows_s(D_MODEL)],
        out_shape=[jax.ShapeDtypeStruct((n_p, D_MODEL), F32), jax.ShapeDtypeStruct((n_s, D_MODEL), F32)],
        compiler_params=_cparams(("arbitrary",)),
        name="ffn_down",
    )(act_p, h1_p, act_s, h1_s, w_down, g, b)


def _rope_tables(pos, reps):
    half = HEAD_DIM // 2
    inv = ROPE_THETA ** (-jnp.arange(half, dtype=F32) / half)
    ang = pos.astype(F32)[:, None] * inv[None, :]
    cos, sin = jnp.cos(ang), jnp.sin(ang)
    cos128 = jnp.concatenate([cos, cos, cos, cos], axis=1)
    sin128 = jnp.concatenate([-sin, sin, -sin, sin], axis=1)
    return jnp.tile(cos128, (reps, 1)), jnp.tile(sin128, (reps, 1))


TM = 256
D_TM = 256
E_TM = 1024
F_TM = 256
RWKV_NB = 4
ATTN_NB = 2


def kernel(x_prompt, x_sample, cache_k, cache_v, state_wkv, state_shift, state_ffn_conv, ln_in_g, ln_in_b, w_in, attn_sinks, rw_mu, rw_w0, rw_w2, rw_a0, rw_a2, rw_g2, rw_k_k, rw_k_a, rw_r_k, rw_lnx_g, rw_lnx_b, w_out, ln1_g, ln1_b, ffn_w_up, ffn_conv_w, ffn_conv_b, ffn_w_down, ln2_g, ln2_b):
    l = 0
    seg = lax.broadcasted_iota(jnp.int32, (SEG, SEG), 0) // HEAD_DIM
    bd = (seg == seg.T).astype(BF16)
    row = lambda a: a.reshape(1, -1)
    rw_prm = {"mu": row(rw_mu[l]), "w0": row(rw_w0[l]), "a0": row(rw_a0[l]), "k_k": row(rw_k_k[l]),
              "k_a": row(rw_k_a[l]), "r_k": row(rw_r_k[l]), "lnx_g": row(rw_lnx_g[l]), "lnx_b": row(rw_lnx_b[l]),
              "w2": rw_w2[l].astype(BF16), "a2": rw_a2[l].astype(BF16), "g2": rw_g2[l].astype(BF16), "bd": bd}
    sinks = attn_sinks[l]
    bp, tp = x_prompt.shape[0], x_prompt.shape[1]
    bs, ts = x_sample.shape[0], x_sample.shape[1]
    n_p, n_s = bp * tp, bs * ts
    zeros = lambda *s: jnp.zeros(s, F32)

    p_p, h_p, tails_p, p_s, h_s, tails_s = _in_proj(
        x_prompt.reshape(n_p, D_MODEL), x_sample.reshape(n_s, D_MODEL), row(ln_in_g), row(ln_in_b),
        w_in[l].astype(BF16), _rope_tables(jnp.arange(tp, dtype=jnp.int32), 1),
        _rope_tables(PAST_LEN + jnp.arange(ts, dtype=jnp.int32), bs), TM, tp, ts)
    p3_p, p3_s = p_p.reshape(bp, tp, P_COLS), p_s.reshape(bs, ts, P_COLS)

    att_p = _attn_prompt(p3_p, sinks, ATTN_NB).reshape(n_p, ATT_WIDTH)
    att_s, s_k, s_v = _attn_sample(p_s, sinks, cache_k[l].reshape(bs * WINDOW, KV_WIDTH),
                                   cache_v[l].reshape(bs * WINDOW, KV_WIDTH), bs, ts)
    rw_p, p_wkv = _rwkv(p3_p, zeros(bp, 1, RW_COLS), zeros(bp, RW_HEADS, HEAD_DIM, HEAD_DIM), rw_prm, CHUNK, RWKV_NB)
    rw_s, s_wkv = _rwkv(p3_s, state_shift[l], state_wkv[l], rw_prm, ts, RWKV_NB)

    h1_p, h1b_p, h1_s, h1b_s = _out_proj(att_p, rw_p.reshape(n_p, RW_WIDTH), h_p, att_s, rw_s.reshape(n_s, RW_WIDTH), h_s,
                                         w_out[l], row(ln1_g[l]), row(ln1_b[l]), D_TM)
    act_p, cg_p, cv_p, act_s, cg_s, cv_s, w_down_b = _ffn_up(
        h1b_p, zeros(bp, CONV_W - 1, 2 * D_FF), tp, E_TM, h1b_s, state_ffn_conv[l], ts,
        ffn_w_up[l], ffn_conv_w[l], row(ffn_conv_b[l]), ffn_w_down[l])
    y_p, y_s = _ffn_down(act_p, h1_p, act_s, h1_s, w_down_b, row(ln2_g[l]), row(ln2_b[l]), F_TM)

    kv_shape = lambda b: (1, b, WINDOW, ATT_KV_HEADS, HEAD_DIM)
    p_k = p3_p[:, tp - WINDOW:, P_AK:P_AK + KV_WIDTH].reshape(kv_shape(bp))
    p_v = p3_p[:, tp - WINDOW:, P_AV:P_AV + KV_WIDTH].reshape(kv_shape(bp))
    shift_of = lambda tails, b: tails.reshape(b, -1, RW_COLS)[:, -1:, :][None]
    conv_of = lambda cg, cv: jnp.concatenate([cg[:, SUBLANES - 2:, :], cv[:, SUBLANES - 2:, :]], axis=-1)[None]
    return (y_p.reshape(bp, tp, D_MODEL), y_s.reshape(bs, ts, D_MODEL), p_k, p_v, p_wkv[None],
            shift_of(tails_p, bp), conv_of(cg_p, cv_p),
            s_k.reshape(kv_shape(bs)), s_v.reshape(kv_shape(bs)), s_wkv[None],
            shift_of(tails_s, bs), conv_of(cg_s, cv_s))
```

```python
import functools
import math

import jax
import jax.numpy as jnp
from jax import lax
from jax.experimental import pallas as pl
from jax.experimental.pallas import tpu as pltpu

D_MODEL = 2048
HEAD_DIM = 64
ATT_HEADS = 16
ATT_KV_HEADS = 2
ATT_GROUP = ATT_HEADS // ATT_KV_HEADS
ATT_WIDTH = ATT_HEADS * HEAD_DIM
KV_WIDTH = ATT_KV_HEADS * HEAD_DIM
CHUNK = 64
WINDOW = 128
ROPE_THETA = 10000.0
ATT_SCALE = HEAD_DIM ** -0.5
RW_HEADS = 16
RW_WIDTH = RW_HEADS * HEAD_DIM
DECAY_LORA = 64
AAA_LORA = 64
GATE_LORA = 160
LNX_EPS = 64e-5
ATT_COLS = ATT_WIDTH + 2 * KV_WIDTH
RW_COLS = 3 * RW_WIDTH + DECAY_LORA + AAA_LORA + GATE_LORA
D_FF = 5632
CONV_W = 3
LN_EPS = 1e-5
DEPTH = 1
ALPHA = (2 * DEPTH) ** 0.25
PAST_LEN = 1024

LORA_W = 512
P_R, P_K, P_V, P_Q = 0, 1024, 2048, 3072
P_LORA = 4096
P_AK = P_LORA + LORA_W
P_AV = P_AK + KV_WIDTH
P_COLS = P_AV + KV_WIDTH
LANES = 128
SUBLANES = 8
VMEM_LIMIT = 56 * 1024 * 1024

F32 = jnp.float32
BF16 = jnp.bfloat16


def _cparams(sem):
    return pltpu.CompilerParams(dimension_semantics=sem, vmem_limit_bytes=VMEM_LIMIT)


def _layer_norm(y, g, b):
    mu = jnp.mean(y, -1, keepdims=True)
    d = y - mu
    var = jnp.mean(d * d, -1, keepdims=True)
    return d * lax.rsqrt(var + LN_EPS) * g + b


A_TN = 512


def _rope128(x, cos, sin_signed):
    lane = lax.broadcasted_iota(jnp.int32, x.shape, 1)
    first_half = (lane % HEAD_DIM) < (HEAD_DIM // 2)
    rot = jnp.where(first_half, pltpu.roll(x, LANES - HEAD_DIM // 2, 1), pltpu.roll(x, HEAD_DIM // 2, 1))
    return x * cos + rot * sin_signed


def _is_rope_col(col):
    return (P_Q <= col < P_Q + ATT_WIDTH) or (P_AK <= col < P_AK + KV_WIDTH)


O_R = ATT_COLS
O_WD = O_R + RW_WIDTH
O_K = O_WD + DECAY_LORA
O_V = O_K + RW_WIDTH
O_AD = O_V + RW_WIDTH
O_GD = O_AD + AAA_LORA
IN_COLS = O_GD + GATE_LORA


def _in_proj_kernel(first, x_ref, g_ref, b_ref, w_ref, cos_ref, sin_ref, mu_ref, st_ref, p_ref, h_ref, tail_ref,
                    hb_ref, carry_scr):
    h = _layer_norm(x_ref[...], g_ref[...], b_ref[...])
    h_ref[...] = h
    hb_ref[...] = h.astype(BF16)
    tm = x_ref.shape[0]
    nseq = tail_ref.shape[0]
    t_seq = tm // nseq
    row8 = lax.broadcasted_iota(jnp.int32, (SUBLANES, 1), 0)
    if nseq == 1:
        @pl.when(first)
        def _():
            carry_scr[...] = jnp.zeros_like(carry_scr)

    def product(c0, width):
        acc = jnp.dot(hb_ref[...], w_ref[:, c0:c0 + width], preferred_element_type=F32)
        if c0 < O_R:
            return acc
        cols = slice(c0 - O_R, c0 - O_R + width)
        prev = pltpu.roll(acc, 1, 0)
        for s in range(nseq):
            r = s * t_seq
            before = st_ref[s][:, cols]
            if nseq == 1:
                before = jnp.where(first, before, carry_scr[SUBLANES - 1:SUBLANES, cols])
            fixed = jnp.where(row8 == 0, before, prev[r:r + SUBLANES, :])
            head = [prev[:r, :]] if r > 0 else []
            prev = jnp.concatenate(head + [fixed, prev[r + SUBLANES:, :]], axis=0)
            tail_ref[s, :, cols] = acc[r + t_seq - SUBLANES:r + t_seq, :]
        if nseq == 1:
            carry_scr[:, cols] = acc[tm - SUBLANES:tm, :]
        return acc + (prev - acc) * mu_ref[:, cols]

    for src, width, dst in ((0, ATT_WIDTH, P_Q), (ATT_WIDTH, 2 * KV_WIDTH, P_AK), (O_R, RW_WIDTH, P_R)):
        for c0 in range(0, width, A_TN):
            acc = product(src + c0, min(A_TN, width - c0))
            for l0 in range(0, acc.shape[1], LANES):
                col = dst + c0 + l0
                part = acc[:, l0:l0 + LANES]
                if _is_rope_col(col):
                    part = _rope128(part, cos_ref[...], sin_ref[...])
                p_ref[:, col:col + LANES] = part

    low = lax.broadcasted_iota(jnp.int32, (x_ref.shape[0], LANES), 1) < HEAD_DIM
    span = O_GD - O_WD
    rolled, first_piece = [], None
    for c0 in range(0, span, A_TN):
        acc = product(O_WD + c0, min(A_TN, span - c0))
        for l0 in range(0, acc.shape[1], LANES):
            piece = acc[:, l0:l0 + LANES]
            j = (c0 + l0) // LANES
            if j == 0:
                first_piece = piece
            rolled.append(pltpu.roll(piece, HEAD_DIM, 1))
            if j >= 1:
                m = j - 1
                dst = P_K + m * LANES if m < RW_WIDTH // LANES else P_V + (m - RW_WIDTH // LANES) * LANES
                p_ref[:, dst:dst + LANES] = jnp.where(low, rolled[m], rolled[m + 1])
            if j == span // LANES - 1:
                p_ref[:, P_LORA:P_LORA + LANES] = jnp.where(low, first_piece, piece)
    gd0 = P_LORA + DECAY_LORA + AAA_LORA
    p_ref[:, gd0:gd0 + GATE_LORA] = product(O_GD, GATE_LORA)
    p_ref[:, gd0 + GATE_LORA:P_LORA + LORA_W] = jnp.zeros((x_ref.shape[0], LORA_W - DECAY_LORA - AAA_LORA - GATE_LORA), F32)


def _in_proj(xp, xs, g, b, w_p, mu, shift_p, shift_s, tables_p, tables_s, tm, seq_p, seq_s):
    n_p, n_s = xp.shape[0], xs.shape[0]
    assert n_s == tm
    nrb = n_p // tm
    nseq_s = tm // seq_s
    tb = seq_p // tm
    last = lambda i: jnp.minimum(i, nrb - 1)

    def kern(xp_ref, cp_ref, sp_ref, stp_ref, xs_ref, cs_ref, ss_ref, sts_ref, g_ref, b_ref, w_ref, mu_ref,
             pp_ref, hp_ref, tp_ref, ps_ref, hs_ref, ts_ref, hb_ref, carry_scr):
        i = pl.program_id(0)
        pl.when(i < nrb)(lambda: _in_proj_kernel(i % tb == 0, xp_ref, g_ref, b_ref, w_ref, cp_ref, sp_ref, mu_ref, stp_ref,
                                                 pp_ref, hp_ref, tp_ref, hb_ref, carry_scr))
        pl.when(i == nrb)(lambda: _in_proj_kernel(None, xs_ref, g_ref, b_ref, w_ref, cs_ref, ss_ref, mu_ref, sts_ref,
                                                  ps_ref, hs_ref, ts_ref, hb_ref, carry_scr))

    const2 = lambda shape: pl.BlockSpec(shape, lambda i: (0, 0))
    return pl.pallas_call(
        kern,
        grid=(nrb + 1,),
        in_specs=[
            pl.BlockSpec((tm, D_MODEL), lambda i: (last(i), 0)),
            pl.BlockSpec((tm, LANES), lambda i: (last(i) % tb, 0)),
            pl.BlockSpec((tm, LANES), lambda i: (last(i) % tb, 0)),
            pl.BlockSpec((1, 1, RW_COLS), lambda i: (last(i) // tb, 0, 0)),
            const2((tm, D_MODEL)), const2((tm, LANES)), const2((tm, LANES)),
            pl.BlockSpec((nseq_s, 1, RW_COLS), lambda i: (0, 0, 0)),
            const2((1, D_MODEL)), const2((1, D_MODEL)),
            pl.BlockSpec((D_MODEL, IN_COLS), lambda i: (0, 0), pipeline_mode=pl.Buffered(1)),
            const2((1, RW_COLS)),
        ],
        out_specs=[
            pl.BlockSpec((tm, P_COLS), lambda i: (last(i), 0)),
            pl.BlockSpec((tm, D_MODEL), lambda i: (last(i), 0)),
            pl.BlockSpec((1, SUBLANES, RW_COLS), lambda i: (last(i), 0, 0)),
            const2((tm, P_COLS)), const2((tm, D_MODEL)),
            pl.BlockSpec((nseq_s, SUBLANES, RW_COLS), lambda i: (0, 0, 0)),
        ],
        out_shape=[
            jax.ShapeDtypeStruct((n_p, P_COLS), F32),
            jax.ShapeDtypeStruct((n_p, D_MODEL), F32),
            jax.ShapeDtypeStruct((nrb, SUBLANES, RW_COLS), F32),
            jax.ShapeDtypeStruct((n_s, P_COLS), F32),
            jax.ShapeDtypeStruct((n_s, D_MODEL), F32),
            jax.ShapeDtypeStruct((nseq_s, SUBLANES, RW_COLS), F32),
        ],
        scratch_shapes=[pltpu.VMEM((tm, D_MODEL), BF16), pltpu.VMEM((SUBLANES, RW_COLS), F32)],
        compiler_params=_cparams(("arbitrary",)),
        name="in_proj",
    )(xp, *tables_p, shift_p, xs, *tables_s, shift_s, g, b, w_p, mu)


def _attend(q, k, v, valid, sink_ref):
    r = q.shape[0]
    outs = []
    for kv in range(ATT_KV_HEADS):
        kh = k[:, kv * HEAD_DIM:(kv + 1) * HEAD_DIM].astype(BF16)
        vh = v[:, kv * HEAD_DIM:(kv + 1) * HEAD_DIM].astype(BF16)
        heads = [kv * ATT_GROUP + g for g in range(ATT_GROUP)]
        qs = jnp.concatenate([q[:, h * HEAD_DIM:(h + 1) * HEAD_DIM] for h in heads], axis=0)
        sink = jnp.concatenate([jnp.full((r, 1), sink_ref[h], F32) for h in heads], axis=0)
        s = lax.dot_general(qs.astype(BF16), kh, (((1,), (1,)), ((), ())), preferred_element_type=F32) * ATT_SCALE
        if valid is not None:
            s = jnp.where(valid, s, -jnp.inf)
        m = jnp.maximum(jnp.max(s, -1, keepdims=True), sink)
        p = jnp.exp(s - m)
        den = jnp.sum(p, -1, keepdims=True) + jnp.exp(sink - m)
        o = jnp.dot(p.astype(BF16), vh, preferred_element_type=F32) / den
        outs.extend(o[g * r:(g + 1) * r] for g in range(ATT_GROUP))
    return jnp.concatenate(outs, axis=1)


LOG2_E = math.log2(math.e)
QB = 2 * CHUNK


def _attn_prompt_kernel(sink_ref, q_ref, kp_ref, kc_ref, vp_ref, vc_ref, o_ref):
    m = pl.program_id(1)
    nseq = q_ref.shape[0]
    span = 2 * QB
    lane = lax.broadcasted_iota(jnp.int32, (span, KV_WIDTH), 1)
    row_c = 2 + lax.broadcasted_iota(jnp.int32, (QB, span), 0) // CHUNK
    col_c = lax.broadcasted_iota(jnp.int32, (QB, span), 1) // CHUNK
    back = row_c - col_c
    valid = jnp.logical_and(jnp.logical_and(back >= 0, back <= 2), 2 * m - 2 + col_c >= 0)
    out_lane = lax.broadcasted_iota(jnp.int32, (QB, PAIR), 1)

    def dup(x, kv):
        sw = pltpu.roll(x, HEAD_DIM, 1)
        return jnp.where(lane < HEAD_DIM, x, sw) if kv == 0 else jnp.where(lane < HEAD_DIM, sw, x)

    k_st, v_st = [], []
    for b in range(nseq):
        k_all = jnp.concatenate([kp_ref[b], kc_ref[b]], axis=0)
        v_all = jnp.concatenate([vp_ref[b], vc_ref[b]], axis=0)
        k_st.append([_stack_heads(dup(k_all, kv), HEAD_DIM).astype(BF16) for kv in range(ATT_KV_HEADS)])
        v_st.append([_stack_heads(dup(v_all, kv), HEAD_DIM).astype(BF16) for kv in range(ATT_KV_HEADS)])
    pairs = [(b, j) for b in range(nseq) for j in range(ATT_HEADS // 2)]
    kv_of = lambda j: (2 * j) // ATT_GROUP
    s = [lax.dot_general((q_ref[b, :, j * PAIR:(j + 1) * PAIR] * (ATT_SCALE * LOG2_E)).astype(BF16), k_st[b][kv_of(j)],
                         (((1,), (1,)), ((), ())), preferred_element_type=F32) for b, j in pairs]
    probs, dens = [], []
    for idx, (b, j) in enumerate(pairs):
        halves, den = [], []
        for e in range(2):
            sink = sink_ref[2 * j + e] * LOG2_E
            sh = jnp.where(valid, s[idx][:, e * span:(e + 1) * span], -jnp.inf)
            mx = jnp.maximum(jnp.max(sh, -1, keepdims=True), sink)
            ph = jnp.exp2(sh - mx)
            halves.append(ph)
            den.append(jnp.sum(ph, -1, keepdims=True) + jnp.exp2(sink - mx))
        probs.append(jnp.concatenate(halves, axis=1).astype(BF16))
        dens.append(jnp.where(out_lane < HEAD_DIM, den[0], den[1]))
    o = [jnp.dot(probs[idx], v_st[b][kv_of(j)], preferred_element_type=F32) / dens[idx]
         for idx, (b, j) in enumerate(pairs)]
    n_pairs = ATT_HEADS // 2
    for b in range(nseq):
        o_ref[b] = jnp.concatenate(o[b * n_pairs:(b + 1) * n_pairs], axis=1).astype(o_ref.dtype)


def _attn_prompt(p3, sinks, nseq):
    batch, seq = p3.shape[0], p3.shape[1]
    nb = seq // QB
    kcol, vcol = P_AK // KV_WIDTH, P_AV // KV_WIDTH
    prev = lambda col: pl.BlockSpec((nseq, QB, KV_WIDTH), lambda g, m: (g, jnp.maximum(m - 1, 0), col))
    cur = lambda col: pl.BlockSpec((nseq, QB, KV_WIDTH), lambda g, m: (g, m, col))
    return pl.pallas_call(
        _attn_prompt_kernel,
        grid=(batch // nseq, nb),
        in_specs=[
            pl.BlockSpec(memory_space=pltpu.SMEM),
            pl.BlockSpec((nseq, QB, ATT_WIDTH), lambda g, m: (g, m, P_Q // ATT_WIDTH)),
            prev(kcol), cur(kcol), prev(vcol), cur(vcol),
        ],
        out_specs=pl.BlockSpec((nseq, QB, ATT_WIDTH), lambda g, m: (g, m, 0)),
        out_shape=jax.ShapeDtypeStruct((batch, seq, ATT_WIDTH), BF16),
        compiler_params=_cparams(("arbitrary", "arbitrary")),
        name="attn_prompt",
    )(sinks, p3, p3, p3, p3, p3)


def _attn_sample_kernel(sink_ref, q_ref, kn_ref, vn_ref, kc_ref, vc_ref, o_ref, kw_ref, vw_ref):
    t = q_ref.shape[0]
    k = jnp.concatenate([kc_ref[...], kn_ref[...]], axis=0)
    v = jnp.concatenate([vc_ref[...], vn_ref[...]], axis=0)
    o_ref[...] = _attend(q_ref[...], k, v, None, sink_ref).astype(o_ref.dtype)
    kw_ref[...] = k[t:]
    vw_ref[...] = v[t:]


def _attn_sample(p, sinks, cache_k, cache_v, batch, seq):
    kcol, vcol = P_AK // KV_WIDTH, P_AV // KV_WIDTH
    return pl.pallas_call(
        _attn_sample_kernel,
        grid=(batch,),
        in_specs=[
            pl.BlockSpec(memory_space=pltpu.SMEM),
            pl.BlockSpec((seq, ATT_WIDTH), lambda b: (b, P_Q // ATT_WIDTH)),
            pl.BlockSpec((seq, KV_WIDTH), lambda b: (b, kcol)),
            pl.BlockSpec((seq, KV_WIDTH), lambda b: (b, vcol)),
            pl.BlockSpec((WINDOW, KV_WIDTH), lambda b: (b, 0)),
            pl.BlockSpec((WINDOW, KV_WIDTH), lambda b: (b, 0)),
        ],
        out_specs=[
            pl.BlockSpec((seq, ATT_WIDTH), lambda b: (b, 0)),
            pl.BlockSpec((WINDOW, KV_WIDTH), lambda b: (b, 0)),
            pl.BlockSpec((WINDOW, KV_WIDTH), lambda b: (b, 0)),
        ],
        out_shape=[
            jax.ShapeDtypeStruct((batch * seq, ATT_WIDTH), BF16),
            jax.ShapeDtypeStruct((batch * WINDOW, KV_WIDTH), F32),
            jax.ShapeDtypeStruct((batch * WINDOW, KV_WIDTH), F32),
        ],
        compiler_params=_cparams(("arbitrary",)),
        name="attn_sample",
    )(sinks, p, p, p, cache_k, cache_v)


SEG = 256
EXP_NEG_HALF = math.exp(-0.5)
PAIR = 2 * HEAD_DIM
N_PAIRS = RW_HEADS // 2


def _split2(x):
    hi = x.astype(BF16)
    lo = (x - hi.astype(F32)).astype(BF16)
    return hi, lo


def _seg_sum(x, bd):
    c = x.shape[0]
    xb = x.astype(BF16)
    stack = jnp.concatenate([xb[:, s:s + SEG] for s in range(0, RW_WIDTH, SEG)], axis=0)
    r = jnp.dot(stack, bd, preferred_element_type=F32)
    return jnp.concatenate([r[i * c:(i + 1) * c] for i in range(RW_WIDTH // SEG)], axis=1)


def _shift_rows(x, prev_row):
    row = lax.broadcasted_iota(jnp.int32, x.shape, 0)
    return jnp.where(row == 0, prev_row, pltpu.roll(x, 1, 0))


def _dotb(a, b):
    return jnp.dot(a.astype(BF16), b.astype(BF16), preferred_element_type=F32)


def _dotb_nt(a, b):
    return lax.dot_general(a.astype(BF16), b.astype(BF16), (((1,), (1,)), ((), ())), preferred_element_type=F32)


def _dotb_tn(a, b):
    return lax.dot_general(a.astype(BF16), b.astype(BF16), (((0,), (0,)), ((), ())), preferred_element_type=F32)


def _stack_heads(x, half):
    lane = lax.broadcasted_iota(jnp.int32, x.shape, 1)
    first = (lane % (2 * half)) < half
    zero = jnp.zeros_like(x)
    return jnp.concatenate([jnp.where(first, x, zero), jnp.where(first, zero, x)], axis=0)


def _stack_heads_bf16(x, half):
    return _stack_heads(x.astype(BF16), half)


def _rwkv_pairs(ab, rb, bb, kb, v, gam_c, s_bd, n_double):
    c = ab[0].shape[0]
    pairs = range(len(ab))
    t_idx = lax.broadcasted_iota(jnp.int32, (c, 2 * c), 0)
    s_idx = lax.broadcasted_iota(jnp.int32, (c, 2 * c), 1) % c
    strict, incl = s_idx < t_idx, s_idx <= t_idx
    g = [_dotb_nt(jnp.concatenate([ab[p], rb[p]], axis=0),
                  jnp.concatenate([_stack_heads_bf16(bb[p], HEAD_DIM), _stack_heads_bf16(kb[p], HEAD_DIM)], axis=0))
         for p in pairs]
    l_ak = [jnp.where(strict, g[p][:c, 2 * c:], 0.0) for p in pairs]
    m_rbk = [jnp.concatenate([jnp.where(incl, g[p][c:, :2 * c], 0.0), jnp.where(incl, g[p][c:, 2 * c:], 0.0)], axis=1)
             for p in pairs]
    lp = [jnp.where(strict, g[p][:c, :2 * c], 0.0) for p in pairs]
    v_st = [_stack_heads_bf16(v[p], HEAD_DIM) for p in pairs]
    y0 = [jnp.concatenate([ab[p], _dotb(l_ak[p], v_st[p])], axis=1) for p in pairs]
    n_acc = list(lp)
    lk_b = [lp[p].astype(BF16) for p in pairs]
    lk = [_dotb(lk_b[p], _stack_heads(lk_b[p], c)) for p in pairs]
    for k in range(1, n_double):
        n_st = [_stack_heads_bf16(n_acc[p], c) for p in pairs]
        lk_b = [lk[p].astype(BF16) for p in pairs]
        if k + 1 < n_double:
            z = [_dotb(lk_b[p], jnp.concatenate([n_st[p], _stack_heads(lk_b[p], c)], axis=1)) for p in pairs]
            n_acc = [n_acc[p] + lk[p] + z[p][:, :2 * c] for p in pairs]
            lk = [z[p][:, 2 * c:] for p in pairs]
        else:
            n_acc = [n_acc[p] + lk[p] + _dotb(lk_b[p], n_st[p]) for p in pairs]
    y = [y0[p] + _dotb(n_acc[p], _stack_heads_bf16(y0[p], HEAD_DIM)) for p in pairs]
    ya = [y[p][:, :PAIR] for p in pairs]
    yu = [y[p][:, PAIR:] for p in pairs]
    ou = []
    for p in pairs:
        lhs = jnp.concatenate([rb[p], ya[p]], axis=0).astype(BF16)
        ou.append(lax.dot_general(jnp.concatenate([lhs, lhs], axis=1), jnp.concatenate(_split2(s_bd[p]), axis=1),
                                  (((1,), (1,)), ((), ())), preferred_element_type=F32))
    u = [ou[p][c:] + yu[p] for p in pairs]
    o = [ou[p][:c] + _dotb(m_rbk[p], jnp.concatenate([_stack_heads_bf16(u[p], HEAD_DIM), v_st[p]], axis=0))
         for p in pairs]
    row_h = lax.broadcasted_iota(jnp.int32, (PAIR, PAIR), 0) // HEAD_DIM
    col_h = lax.broadcasted_iota(jnp.int32, (PAIR, PAIR), 1) // HEAD_DIM
    same = row_h == col_h
    d_t = [jnp.where(same, _dotb_tn(jnp.concatenate([u[p], v[p]], axis=0),
                                    jnp.concatenate([bb[p], kb[p]], axis=0)), 0.0) for p in pairs]
    s_new = [(s_bd[p] + d_t[p]) * gam_c[p] for p in pairs]
    return o, s_new


def _split_rw_rows(x):
    o_wd, o_gd = RW_WIDTH, RW_COLS - GATE_LORA
    low = lax.broadcasted_iota(jnp.int32, (x.shape[0], LANES), 1) < HEAD_DIM
    pieces = [x[:, c0:c0 + LANES] for c0 in range(o_wd, o_gd, LANES)]
    rolled = [pltpu.roll(p, HEAD_DIM, 1) for p in pieces]
    merged = [jnp.where(low, rolled[m], rolled[m + 1]) for m in range(len(pieces) - 1)]
    n = RW_WIDTH // LANES
    return (x[:, :RW_WIDTH], jnp.concatenate(merged[:n], axis=1), jnp.concatenate(merged[n:], axis=1),
            jnp.where(low, pieces[0], pieces[-1]), x[:, o_gd:])


def _rwkv_kernel(xr_ref, xk_ref, xv_ref, xl_ref, w0_ref, a0_ref, kk_ref, ka_ref, rk_ref,
                 lg_ref, lb_ref, w2_ref, a2_ref, g2_ref, bd_ref, st0_ref,
                 out_ref, st_ref, s_scr):
    c = pl.program_id(1)
    nc = pl.num_programs(1)
    nb, C = xr_ref.shape[0], xr_ref.shape[1]
    n_double = int(math.log2(C))

    @pl.when(c == 0)
    def _():
        s_scr[...] = jnp.zeros_like(s_scr)
        for b in range(nb):
            for h in range(RW_HEADS):
                off = (h % 2) * HEAD_DIM
                s_scr[b * N_PAIRS + h // 2, off:off + HEAD_DIM, off:off + HEAD_DIM] = st0_ref[b, h]

    rows_of = lambda load: jnp.concatenate([load(b) for b in range(nb)], axis=0)
    wa_w = DECAY_LORA + AAA_LORA
    r = rows_of(lambda b: xr_ref[b])
    kr = rows_of(lambda b: xk_ref[b])
    v = rows_of(lambda b: xv_ref[b])
    xw = rows_of(lambda b: xl_ref[b, :, :wa_w])
    gd = rows_of(lambda b: xl_ref[b, :, wa_w:wa_w + GATE_LORA])
    wd = xw[:, 0:DECAY_LORA]
    ad = xw[:, DECAY_LORA:wa_w]

    lw = (-EXP_NEG_HALF * LOG2_E) * jax.nn.sigmoid(w0_ref[...] + _dotb(jnp.tanh(wd), w2_ref[...]))
    a = jax.nn.sigmoid(a0_ref[...] + _dotb(ad, a2_ref[...]))
    g = _dotb(jax.nn.sigmoid(gd), g2_ref[...])
    bd = bd_ref[...]
    kk = kr * kk_ref[...]
    kk = kk * lax.rsqrt(jnp.maximum(_seg_sum(kk * kk, bd), 1e-24))
    k2 = kr * (1.0 + (a - 1.0) * ka_ref[...])

    rows = nb * C
    ti = lax.broadcasted_iota(jnp.int32, (rows, rows), 0)
    si = lax.broadcasted_iota(jnp.int32, (rows, rows), 1)
    tri = jnp.logical_and(si <= ti, si // C == ti // C).astype(BF16)
    cs2 = jnp.dot(tri, jnp.concatenate(_split2(lw), axis=1), preferred_element_type=F32)
    cs = cs2[:, :RW_WIDTH] + cs2[:, RW_WIDTH:]
    gam = jnp.exp2(cs)
    ginv = jnp.exp2(-cs)
    ab = -kk * jnp.exp2(cs - lw)
    rb = r * gam
    bb = kk * a * ginv
    kb = k2 * ginv

    def per_pair(x, r0, r1):
        return [x[b * C + r0:b * C + r1, p * PAIR:(p + 1) * PAIR] for b in range(nb) for p in range(N_PAIRS)]

    chains = nb * N_PAIRS
    outs, s_new = _rwkv_pairs(per_pair(ab, 0, C), per_pair(rb, 0, C), per_pair(bb, 0, C), per_pair(kb, 0, C),
                              per_pair(v, 0, C), per_pair(gam, C - 1, C), [s_scr[q] for q in range(chains)], n_double)
    for q in range(chains):
        s_scr[q] = s_new[q]
    o = jnp.concatenate([jnp.concatenate(outs[b * N_PAIRS:(b + 1) * N_PAIRS], axis=1) for b in range(nb)], axis=0)

    mo = _seg_sum(o, bd) * (1.0 / HEAD_DIM)
    d = o - mo
    vo = _seg_sum(d * d, bd) * (1.0 / HEAD_DIM)
    on = d * lax.rsqrt(vo + LNX_EPS) * lg_ref[...] + lb_ref[...]
    bonus = _seg_sum(r * k2 * rk_ref[...], bd) * v
    res = ((on + bonus) * g).astype(out_ref.dtype)
    for b in range(nb):
        out_ref[b] = res[b * C:(b + 1) * C, :]

    @pl.when(c == nc - 1)
    def _():
        for b in range(nb):
            for h in range(RW_HEADS):
                off = (h % 2) * HEAD_DIM
                st_ref[b, h] = s_scr[b * N_PAIRS + h // 2, off:off + HEAD_DIM, off:off + HEAD_DIM]


def _rwkv(p3, state0, prm, chunk, nb):
    batch, seq = p3.shape[0], p3.shape[1]
    nc = seq // chunk
    row = lambda w: pl.BlockSpec((1, w), lambda g, c: (0, 0))
    xspec = lambda w, col: pl.BlockSpec((nb, chunk, w), lambda g, c: (g, c, col))
    full = lambda a: pl.BlockSpec(a.shape, lambda g, c: (0,) * a.ndim)
    stspec = pl.BlockSpec((nb, RW_HEADS, HEAD_DIM, HEAD_DIM), lambda g, c: (g, 0, 0, 0))
    return pl.pallas_call(
        _rwkv_kernel,
        grid=(batch // nb, nc),
        in_specs=[
            xspec(RW_WIDTH, P_R // RW_WIDTH), xspec(RW_WIDTH, P_K // RW_WIDTH), xspec(RW_WIDTH, P_V // RW_WIDTH),
            xspec(LORA_W, P_LORA // LORA_W),
            row(RW_WIDTH), row(RW_WIDTH), row(RW_WIDTH), row(RW_WIDTH), row(RW_WIDTH),
            row(RW_WIDTH), row(RW_WIDTH),
            full(prm["w2"]), full(prm["a2"]), full(prm["g2"]), full(prm["bd"]),
            stspec,
        ],
        out_specs=[
            pl.BlockSpec((nb, chunk, RW_WIDTH), lambda g, c: (g, c, 0)),
            stspec,
        ],
        out_shape=[
            jax.ShapeDtypeStruct((batch, seq, RW_WIDTH), BF16),
            jax.ShapeDtypeStruct((batch, RW_HEADS, HEAD_DIM, HEAD_DIM), F32),
        ],
        scratch_shapes=[pltpu.VMEM((nb * N_PAIRS, PAIR, PAIR), F32)],
        compiler_params=_cparams(("arbitrary", "arbitrary")),
        name="rwkv",
    )(p3, p3, p3, p3, prm["w0"], prm["a0"], prm["k_k"], prm["k_a"], prm["r_k"],
      prm["lnx_g"], prm["lnx_b"], prm["w2"], prm["a2"], prm["g2"], prm["bd"], state0)


D_TN = 512


def _out_proj_body(att_ref, rw_ref, h_ref, g_ref, b_ref, h1_ref, h1b_ref, w_scr):
    for c0 in range(0, D_MODEL, D_TN):
        m = jnp.dot(att_ref[...], w_scr[:ATT_WIDTH, c0:c0 + D_TN], preferred_element_type=F32)
        m = m + jnp.dot(rw_ref[...], w_scr[ATT_WIDTH:, c0:c0 + D_TN], preferred_element_type=F32)
        h1_ref[:, c0:c0 + D_TN] = ALPHA * h_ref[:, c0:c0 + D_TN] + m
    y = _layer_norm(h1_ref[...], g_ref[...], b_ref[...])
    h1_ref[...] = y
    h1b_ref[...] = y.astype(BF16)


def _out_proj(att_p, rw_p, h_p, att_s, rw_s, h_s, w_out, g, b, tm):
    n_p, n_s = att_p.shape[0], att_s.shape[0]
    nrb = n_p // tm
    last = lambda i: jnp.minimum(i, nrb - 1)

    def kern(ap_ref, rp_ref, hp_ref, as_ref, rs_ref, hs_ref, wa_ref, wb_ref, g_ref, b_ref,
             h1p_ref, h1bp_ref, h1s_ref, h1bs_ref, w_scr):
        i = pl.program_id(0)

        @pl.when(i == 0)
        def _():
            w_scr[:ATT_WIDTH, :] = wa_ref[...].astype(BF16)
            w_scr[ATT_WIDTH:, :] = wb_ref[...].astype(BF16)

        pl.when(i < nrb)(lambda: _out_proj_body(ap_ref, rp_ref, hp_ref, g_ref, b_ref, h1p_ref, h1bp_ref, w_scr))
        pl.when(i == nrb)(lambda: _out_proj_body(as_ref, rs_ref, hs_ref, g_ref, b_ref, h1s_ref, h1bs_ref, w_scr))

    rows_p = lambda w: pl.BlockSpec((tm, w), lambda i: (last(i), 0))
    rows_s = lambda w: pl.BlockSpec((n_s, w), lambda i: (0, 0))
    return pl.pallas_call(
        kern,
        grid=(nrb + 1,),
        in_specs=[
            rows_p(ATT_WIDTH), rows_p(RW_WIDTH), rows_p(D_MODEL),
            rows_s(ATT_WIDTH), rows_s(RW_WIDTH), rows_s(D_MODEL),
            pl.BlockSpec((ATT_WIDTH, D_MODEL), lambda i: (0, 0), pipeline_mode=pl.Buffered(1)),
            pl.BlockSpec((RW_WIDTH, D_MODEL), lambda i: (1, 0), pipeline_mode=pl.Buffered(1)),
            pl.BlockSpec((1, D_MODEL), lambda i: (0, 0)),
            pl.BlockSpec((1, D_MODEL), lambda i: (0, 0)),
        ],
        out_specs=[rows_p(D_MODEL), rows_p(D_MODEL), rows_s(D_MODEL), rows_s(D_MODEL)],
        out_shape=[
            jax.ShapeDtypeStruct((n_p, D_MODEL), F32), jax.ShapeDtypeStruct((n_p, D_MODEL), BF16),
            jax.ShapeDtypeStruct((n_s, D_MODEL), F32), jax.ShapeDtypeStruct((n_s, D_MODEL), BF16),
        ],
        scratch_shapes=[pltpu.VMEM((ATT_WIDTH + RW_WIDTH, D_MODEL), BF16)],
        compiler_params=_cparams(("arbitrary",)),
        name="out_proj",
    )(att_p, rw_p, h_p, att_s, rw_s, h_s, w_out, w_out, g, b)


E_TN = 512
E_NCB = D_FF // E_TN
E_SUB = 1024
SQRT_HALF = math.sqrt(0.5)


def _ffn_up_body(bps, nseq, i, h_ref, w_scr, cwg_ref, cwv_ref, cbg_ref, cbv_ref, sg_ref, sv_ref,
                 act_ref, og_ref, ov_ref, cg_scr, cv_scr):
    tm = h_ref.shape[0]
    sub = min(tm, E_SUB)
    n_sub = tm // sub
    t_seq = tm // nseq
    row8 = lax.broadcasted_iota(jnp.int32, (SUBLANES, E_TN), 0)

    def products(r0):
        u = jnp.dot(h_ref[r0:r0 + sub, :], w_scr[...], preferred_element_type=F32)
        return u[:, :E_TN], u[:, E_TN:]

    def conv(u, prevs, cw_ref, cb_ref):
        sh1, sh2 = pltpu.roll(u, 1, 0), pltpu.roll(u, 2, 0)
        for r, prev in prevs:
            p2, p1 = prev[0:1, :], prev[1:2, :]
            fix1 = jnp.where(row8 == 0, p1, sh1[r:r + SUBLANES, :])
            fix2 = jnp.where(row8 == 0, p2, jnp.where(row8 == 1, p1, sh2[r:r + SUBLANES, :]))
            head = lambda x: [x[:r, :]] if r > 0 else []
            tail = lambda x: [x[r + SUBLANES:, :]] if r + SUBLANES < x.shape[0] else []
            sh1 = jnp.concatenate(head(sh1) + [fix1] + tail(sh1), axis=0)
            sh2 = jnp.concatenate(head(sh2) + [fix2] + tail(sh2), axis=0)
        cw = cw_ref[...]
        c = cb_ref[...] + sh2 * cw[0:1, :]
        c = c + sh1 * cw[1:2, :]
        return c + u * cw[2:3, :]

    def epilogue(r0, ug, uv, prev_g, prev_v):
        gate = conv(ug, prev_g, cwg_ref, cbg_ref)
        val = conv(uv, prev_v, cwv_ref, cbv_ref)
        gelu = 0.5 * gate * (1.0 + lax.erf(gate * SQRT_HALF))
        act_ref[r0:r0 + sub, :] = (gelu * val).astype(act_ref.dtype)

    if nseq > 1:
        ug, uv = products(0)
        epilogue(0, ug, uv, [(s * t_seq, sg_ref[s]) for s in range(nseq)],
                 [(s * t_seq, sv_ref[s]) for s in range(nseq)])
        for s in range(nseq):
            og_ref[s] = ug[(s + 1) * t_seq - SUBLANES:(s + 1) * t_seq, :]
            ov_ref[s] = uv[(s + 1) * t_seq - SUBLANES:(s + 1) * t_seq, :]
    else:
        first = (i % bps) == 0
        prev_g = jnp.where(first, sg_ref[0], cg_scr[SUBLANES - 2:SUBLANES, :])
        prev_v = jnp.where(first, sv_ref[0], cv_scr[SUBLANES - 2:SUBLANES, :])
        nxt = products(0)
        for k in range(n_sub):
            ug, uv = nxt
            if k + 1 < n_sub:
                nxt = products((k + 1) * sub)
            epilogue(k * sub, ug, uv, [(0, prev_g)], [(0, prev_v)])
            prev_g, prev_v = ug[sub - 2:sub, :], uv[sub - 2:sub, :]
        cg_scr[...] = ug[sub - SUBLANES:sub, :]
        cv_scr[...] = uv[sub - SUBLANES:sub, :]
        og_ref[0] = ug[sub - SUBLANES:sub, :]
        ov_ref[0] = uv[sub - SUBLANES:sub, :]


def _ffn_up(h_p, conv_prev_p, seq_p, tm_p, h_s, conv_prev_s, seq_s, w_up, conv_w, conv_b, w_down):
    n_p, n_s = h_p.shape[0], h_s.shape[0]
    batch_p, batch_s = n_p // seq_p, n_s // seq_s
    bps = seq_p // tm_p
    nrb = n_p // tm_p
    slab = D_FF // (E_NCB * nrb)
    last = lambda i: jnp.maximum(i - 1, 0)

    def kern(hp_ref, sgp_ref, svp_ref, hs_ref, sgs_ref, svs_ref, wg_ref, wv_ref, cwg_ref, cwv_ref, cbg_ref, cbv_ref,
             wd_ref, actp_ref, ogp_ref, ovp_ref, acts_ref, ogs_ref, ovs_ref, wdb_ref, w_scr, cg_scr, cv_scr):
        i = pl.program_id(1)

        @pl.when(i == 0)
        def _():
            w_scr[:, :E_TN] = wg_ref[...].astype(BF16)
            w_scr[:, E_TN:] = wv_ref[...].astype(BF16)
            _ffn_up_body(1, batch_s, i, hs_ref, w_scr, cwg_ref, cwv_ref, cbg_ref, cbv_ref, sgs_ref, svs_ref,
                         acts_ref, ogs_ref, ovs_ref, cg_scr, cv_scr)

        @pl.when(i > 0)
        def _():
            wdb_ref[...] = wd_ref[...].astype(BF16)
            _ffn_up_body(bps, 1, i - 1, hp_ref, w_scr, cwg_ref, cwv_ref, cbg_ref, cbv_ref, sgp_ref, svp_ref,
                         actp_ref, ogp_ref, ovp_ref, cg_scr, cv_scr)

    col = lambda shape, off: pl.BlockSpec(shape, lambda j, i: (0,) * (len(shape) - 1) + (j + off,))
    st_p = lambda off: pl.BlockSpec((1, CONV_W - 1, E_TN), lambda j, i: (last(i) // bps, 0, j + off))
    st_s = lambda off: pl.BlockSpec((batch_s, CONV_W - 1, E_TN), lambda j, i: (0, 0, j + off))
    tail_p = pl.BlockSpec((1, SUBLANES, E_TN), lambda j, i: (last(i) // bps, 0, j))
    tail_s = pl.BlockSpec((batch_s, SUBLANES, E_TN), lambda j, i: (0, 0, j))
    wd_spec = pl.BlockSpec((slab, D_MODEL), lambda j, i: (j * nrb + last(i), 0))
    return pl.pallas_call(
        kern,
        grid=(E_NCB, nrb + 1),
        in_specs=[
            pl.BlockSpec((tm_p, D_MODEL), lambda j, i: (last(i), 0)), st_p(0), st_p(E_NCB),
            pl.BlockSpec((n_s, D_MODEL), lambda j, i: (0, 0)), st_s(0), st_s(E_NCB),
            col((D_MODEL, E_TN), 0), col((D_MODEL, E_TN), E_NCB),
            col((CONV_W, E_TN), 0), col((CONV_W, E_TN), E_NCB),
            col((1, E_TN), 0), col((1, E_TN), E_NCB),
            wd_spec,
        ],
        out_specs=[
            pl.BlockSpec((tm_p, E_TN), lambda j, i: (last(i), j)), tail_p, tail_p,
            pl.BlockSpec((n_s, E_TN), lambda j, i: (0, j)), tail_s, tail_s,
            wd_spec,
        ],
        out_shape=[
            jax.ShapeDtypeStruct((n_p, D_FF), BF16),
            jax.ShapeDtypeStruct((batch_p, SUBLANES, D_FF), F32), jax.ShapeDtypeStruct((batch_p, SUBLANES, D_FF), F32),
            jax.ShapeDtypeStruct((n_s, D_FF), BF16),
            jax.ShapeDtypeStruct((batch_s, SUBLANES, D_FF), F32), jax.ShapeDtypeStruct((batch_s, SUBLANES, D_FF), F32),
            jax.ShapeDtypeStruct((D_FF, D_MODEL), BF16),
        ],
        scratch_shapes=[pltpu.VMEM((D_MODEL, 2 * E_TN), BF16),
                        pltpu.VMEM((SUBLANES, E_TN), F32), pltpu.VMEM((SUBLANES, E_TN), F32)],
        compiler_params=_cparams(("arbitrary", "arbitrary")),
        name="ffn_up",
    )(h_p, conv_prev_p, conv_prev_p, h_s, conv_prev_s, conv_prev_s, w_up, w_up, conv_w, conv_w, conv_b, conv_b, w_down)


F_TN = 512


def _ffn_down_kernel(act_ref, w_ref, h1_ref, g_ref, b_ref, y_ref):
    for c0 in range(0, D_MODEL, F_TN):
        f = jnp.dot(act_ref[...], w_ref[:, c0:c0 + F_TN], preferred_element_type=F32)
        y_ref[:, c0:c0 + F_TN] = ALPHA * h1_ref[:, c0:c0 + F_TN] + f
    y_ref[...] = _layer_norm(y_ref[...], g_ref[...], b_ref[...])


def _ffn_down(act_p, h1_p, act_s, h1_s, w_down, g, b, tm):
    n_p, n_s = act_p.shape[0], act_s.shape[0]
    nrb = n_p // tm
    last = lambda i: jnp.minimum(i, nrb - 1)

    def kern(ap_ref, hp_ref, as_ref, hs_ref, w_ref, g_ref, b_ref, yp_ref, ys_ref):
        i = pl.program_id(0)
        pl.when(i < nrb)(lambda: _ffn_down_kernel(ap_ref, w_ref, hp_ref, g_ref, b_ref, yp_ref))
        pl.when(i == nrb)(lambda: _ffn_down_kernel(as_ref, w_ref, hs_ref, g_ref, b_ref, ys_ref))

    rows_p = lambda w: pl.BlockSpec((tm, w), lambda i: (last(i), 0))
    rows_s = lambda w: pl.BlockSpec((n_s, w), lambda i: (0, 0))
    return pl.pallas_call(
        kern,
        grid=(nrb + 1,),
        in_specs=[
            rows_p(D_FF), rows_p(D_MODEL), rows_s(D_FF), rows_s(D_MODEL),
            pl.BlockSpec((D_FF, D_MODEL), lambda i: (0, 0), pipeline_mode=pl.Buffered(1)),
            pl.BlockSpec((1, D_MODEL), lambda i: (0, 0)),
            pl.BlockSpec((1, D_MODEL), lambda i: (0, 0)),
        ],
        out_specs=[rows_p(D_MODEL), rows_s(D_MODEL)],
        out_shape=[jax.ShapeDtypeStruct((n_p, D_MODEL), F32), jax.ShapeDtypeStruct((n_s, D_MODEL), F32)],
        compiler_params=_cparams(("arbitrary",)),
        name="ffn_down",
    )(act_p, h1_p, act_s, h1_s, w_down, g, b)


def _rope_tables(pos, reps):
    half = HEAD_DIM // 2
    inv = ROPE_THETA ** (-jnp.arange(half, dtype=F32) / half)
    ang = pos.astype(F32)[:, None] * inv[None, :]
    cos, sin = jnp.cos(ang), jnp.sin(ang)
    cos128 = jnp.concatenate([cos, cos, cos, cos], axis=1)
    sin128 = jnp.concatenate([-sin, sin, -sin, sin], axis=1)
    return jnp.tile(cos128, (reps, 1)), jnp.tile(sin128, (reps, 1))


TM = 256
D_TM = 256
E_TM = 1024
F_TM = 256
RWKV_NB = 4
ATTN_NB = 2


def kernel(x_prompt, x_sample, cache_k, cache_v, state_wkv, state_shift, state_ffn_conv, ln_in_g, ln_in_b, w_in, attn_sinks, rw_mu, rw_w0, rw_w2, rw_a0, rw_a2, rw_g2, rw_k_k, rw_k_a, rw_r_k, rw_lnx_g, rw_lnx_b, w_out, ln1_g, ln1_b, ffn_w_up, ffn_conv_w, ffn_conv_b, ffn_w_down, ln2_g, ln2_b):
    l = 0
    seg = lax.broadcasted_iota(jnp.int32, (SEG, SEG), 0) // HEAD_DIM
    bd = (seg == seg.T).astype(BF16)
    row = lambda a: a.reshape(1, -1)
    rw_prm = {"w0": row(rw_w0[l]), "a0": row(rw_a0[l]), "k_k": row(rw_k_k[l]),
              "k_a": row(rw_k_a[l]), "r_k": row(rw_r_k[l]), "lnx_g": row(rw_lnx_g[l]), "lnx_b": row(rw_lnx_b[l]),
              "w2": rw_w2[l].astype(BF16), "a2": rw_a2[l].astype(BF16), "g2": rw_g2[l].astype(BF16), "bd": bd}
    sinks = attn_sinks[l]
    bp, tp = x_prompt.shape[0], x_prompt.shape[1]
    bs, ts = x_sample.shape[0], x_sample.shape[1]
    n_p, n_s = bp * tp, bs * ts
    zeros = lambda *s: jnp.zeros(s, F32)

    p_p, h_p, tails_p, p_s, h_s, tails_s = _in_proj(
        x_prompt.reshape(n_p, D_MODEL), x_sample.reshape(n_s, D_MODEL), row(ln_in_g), row(ln_in_b),
        w_in[l].astype(BF16), row(rw_mu[l]), zeros(bp, 1, RW_COLS), state_shift[l],
        _rope_tables(jnp.arange(tp, dtype=jnp.int32), 1),
        _rope_tables(PAST_LEN + jnp.arange(ts, dtype=jnp.int32), bs), TM, tp, ts)
    p3_p, p3_s = p_p.reshape(bp, tp, P_COLS), p_s.reshape(bs, ts, P_COLS)

    att_p = _attn_prompt(p3_p, sinks, ATTN_NB).reshape(n_p, ATT_WIDTH)
    att_s, s_k, s_v = _attn_sample(p_s, sinks, cache_k[l].reshape(bs * WINDOW, KV_WIDTH),
                                   cache_v[l].reshape(bs * WINDOW, KV_WIDTH), bs, ts)
    rw_p, p_wkv = _rwkv(p3_p, zeros(bp, RW_HEADS, HEAD_DIM, HEAD_DIM), rw_prm, CHUNK, RWKV_NB)
    rw_s, s_wkv = _rwkv(p3_s, state_wkv[l], rw_prm, ts, RWKV_NB)

    h1_p, h1b_p, h1_s, h1b_s = _out_proj(att_p, rw_p.reshape(n_p, RW_WIDTH), h_p, att_s, rw_s.reshape(n_s, RW_WIDTH), h_s,
                                         w_out[l], row(ln1_g[l]), row(ln1_b[l]), D_TM)
    act_p, cg_p, cv_p, act_s, cg_s, cv_s, w_down_b = _ffn_up(
        h1b_p, zeros(bp, CONV_W - 1, 2 * D_FF), tp, E_TM, h1b_s, state_ffn_conv[l], ts,
        ffn_w_up[l], ffn_conv_w[l], row(ffn_conv_b[l]), ffn_w_down[l])
    y_p, y_s = _ffn_down(act_p, h1_p, act_s, h1_s, w_down_b, row(ln2_g[l]), row(ln2_b[l]), F_TM)

    kv_shape = lambda b: (1, b, WINDOW, ATT_KV_HEADS, HEAD_DIM)
    p_k = p3_p[:, tp - WINDOW:, P_AK:P_AK + KV_WIDTH].reshape(kv_shape(bp))
    p_v = p3_p[:, tp - WINDOW:, P_AV:P_AV + KV_WIDTH].reshape(kv_shape(bp))
    shift_of = lambda tails, b: tails.reshape(b, -1, RW_COLS)[:, -1:, :][None]
    conv_of = lambda cg, cv: jnp.concatenate([cg[:, SUBLANES - 2:, :], cv[:, SUBLANES - 2:, :]], axis=-1)[None]
    return (y_p.reshape(bp, tp, D_MODEL), y_s.reshape(bs, ts, D_MODEL), p_k, p_v, p_wkv[None],
            shift_of(tails_p, bp), conv_of(cg_p, cv_p),
            s_k.reshape(kv_shape(bs)), s_v.reshape(kv_shape(bs)), s_wkv[None],
            shift_of(tails_s, bs), conv_of(cg_s, cv_s))
```

```python
import math

import jax
import jax.numpy as jnp
from jax import lax
from jax.experimental import pallas as pl
from jax.experimental.pallas import tpu as pltpu

D_MODEL = 2048
HEAD_DIM = 64
ATT_HEADS = 16
ATT_KV_HEADS = 2
ATT_GROUP = ATT_HEADS // ATT_KV_HEADS
ATT_WIDTH = ATT_HEADS * HEAD_DIM
KV_WIDTH = ATT_KV_HEADS * HEAD_DIM
CHUNK = 64
WINDOW = 128
ROPE_THETA = 10000.0
ATT_SCALE = HEAD_DIM ** -0.5
RW_HEADS = 16
RW_WIDTH = RW_HEADS * HEAD_DIM
DECAY_LORA = 64
AAA_LORA = 64
GATE_LORA = 160
LNX_EPS = 64e-5
ATT_COLS = ATT_WIDTH + 2 * KV_WIDTH
RW_COLS = 3 * RW_WIDTH + DECAY_LORA + AAA_LORA + GATE_LORA
D_FF = 5632
CONV_W = 3
LN_EPS = 1e-5
DEPTH = 1
ALPHA = (2 * DEPTH) ** 0.25
PAST_LEN = 1024

LORA_W = 512
P_R, P_K, P_V, P_Q = 0, 1024, 2048, 3072
P_LORA = 4096
P_AK = P_LORA + LORA_W
P_AV = P_AK + KV_WIDTH
P_COLS = P_AV + KV_WIDTH
LANES = 128
SUBLANES = 8
VMEM_LIMIT = 56 * 1024 * 1024

F32 = jnp.float32
BF16 = jnp.bfloat16


def _cparams(sem):
    return pltpu.CompilerParams(dimension_semantics=sem, vmem_limit_bytes=VMEM_LIMIT)


def _layer_norm(y, g, b):
    mu = jnp.mean(y, -1, keepdims=True)
    d = y - mu
    var = jnp.mean(d * d, -1, keepdims=True)
    return d * lax.rsqrt(var + LN_EPS) * g + b


A_TN = 512


def _rope128(x, cos, sin_signed):
    lane = lax.broadcasted_iota(jnp.int32, x.shape, 1)
    first_half = (lane % HEAD_DIM) < (HEAD_DIM // 2)
    rot = jnp.where(first_half, pltpu.roll(x, LANES - HEAD_DIM // 2, 1), pltpu.roll(x, HEAD_DIM // 2, 1))
    return x * cos + rot * sin_signed


def _is_rope_col(col):
    return (P_Q <= col < P_Q + ATT_WIDTH) or (P_AK <= col < P_AK + KV_WIDTH)


O_R = ATT_COLS
O_WD = O_R + RW_WIDTH
O_K = O_WD + DECAY_LORA
O_V = O_K + RW_WIDTH
O_AD = O_V + RW_WIDTH
O_GD = O_AD + AAA_LORA
IN_COLS = O_GD + GATE_LORA


def _in_proj_kernel(first, x_ref, g_ref, b_ref, w_ref, cos_ref, sin_ref, mu_ref, st_ref, p_ref, h_ref, tail_ref,
                    hb_ref, carry_scr):
    h = _layer_norm(x_ref[...], g_ref[...], b_ref[...])
    h_ref[...] = h
    hb_ref[...] = h.astype(BF16)
    tm = x_ref.shape[0]
    nseq = tail_ref.shape[0]
    t_seq = tm // nseq
    row8 = lax.broadcasted_iota(jnp.int32, (SUBLANES, 1), 0)
    if nseq == 1:
        @pl.when(first)
        def _():
            carry_scr[...] = jnp.zeros_like(carry_scr)

    def product(c0, width):
        acc = jnp.dot(hb_ref[...], w_ref[:, c0:c0 + width], preferred_element_type=F32)
        if c0 < O_R:
            return acc
        cols = slice(c0 - O_R, c0 - O_R + width)
        prev = pltpu.roll(acc, 1, 0)
        for s in range(nseq):
            r = s * t_seq
            before = st_ref[s][:, cols]
            if nseq == 1:
                before = jnp.where(first, before, carry_scr[SUBLANES - 1:SUBLANES, cols])
            fixed = jnp.where(row8 == 0, before, prev[r:r + SUBLANES, :])
            head = [prev[:r, :]] if r > 0 else []
            prev = jnp.concatenate(head + [fixed, prev[r + SUBLANES:, :]], axis=0)
            tail_ref[s, :, cols] = acc[r + t_seq - SUBLANES:r + t_seq, :]
        if nseq == 1:
            carry_scr[:, cols] = acc[tm - SUBLANES:tm, :]
        return acc + (prev - acc) * mu_ref[:, cols]

    for src, width, dst in ((0, ATT_WIDTH, P_Q), (ATT_WIDTH, 2 * KV_WIDTH, P_AK), (O_R, RW_WIDTH, P_R)):
        for c0 in range(0, width, A_TN):
            acc = product(src + c0, min(A_TN, width - c0))
            for l0 in range(0, acc.shape[1], LANES):
                col = dst + c0 + l0
                part = acc[:, l0:l0 + LANES]
                if _is_rope_col(col):
                    part = _rope128(part, cos_ref[...], sin_ref[...])
                p_ref[:, col:col + LANES] = part

    low = lax.broadcasted_iota(jnp.int32, (x_ref.shape[0], LANES), 1) < HEAD_DIM
    span = O_GD - O_WD
    rolled, first_piece = [], None
    for c0 in range(0, span, A_TN):
        acc = product(O_WD + c0, min(A_TN, span - c0))
        for l0 in range(0, acc.shape[1], LANES):
            piece = acc[:, l0:l0 + LANES]
            j = (c0 + l0) // LANES
            if j == 0:
                first_piece = piece
            rolled.append(pltpu.roll(piece, HEAD_DIM, 1))
            if j >= 1:
                m = j - 1
                dst = P_K + m * LANES if m < RW_WIDTH // LANES else P_V + (m - RW_WIDTH // LANES) * LANES
                p_ref[:, dst:dst + LANES] = jnp.where(low, rolled[m], rolled[m + 1])
            if j == span // LANES - 1:
                p_ref[:, P_LORA:P_LORA + LANES] = jnp.where(low, first_piece, piece)
    gd0 = P_LORA + DECAY_LORA + AAA_LORA
    p_ref[:, gd0:gd0 + GATE_LORA] = product(O_GD, GATE_LORA)
    p_ref[:, gd0 + GATE_LORA:P_LORA + LORA_W] = jnp.zeros((x_ref.shape[0], LORA_W - DECAY_LORA - AAA_LORA - GATE_LORA), F32)


def _in_proj(xp, xs, g, b, w_p, mu, shift_p, shift_s, tables_p, tables_s, tm, seq_p, seq_s):
    n_p, n_s = xp.shape[0], xs.shape[0]
    assert n_s == tm
    nrb = n_p // tm
    nseq_s = tm // seq_s
    tb = seq_p // tm
    last = lambda i: jnp.minimum(i, nrb - 1)

    def kern(xp_ref, cp_ref, sp_ref, stp_ref, xs_ref, cs_ref, ss_ref, sts_ref, g_ref, b_ref, w_ref, mu_ref,
             pp_ref, hp_ref, tp_ref, ps_ref, hs_ref, ts_ref, hb_ref, carry_scr):
        i = pl.program_id(0)
        pl.when(i < nrb)(lambda: _in_proj_kernel(i % tb == 0, xp_ref, g_ref, b_ref, w_ref, cp_ref, sp_ref, mu_ref, stp_ref,
                                                 pp_ref, hp_ref, tp_ref, hb_ref, carry_scr))
        pl.when(i == nrb)(lambda: _in_proj_kernel(None, xs_ref, g_ref, b_ref, w_ref, cs_ref, ss_ref, mu_ref, sts_ref,
                                                  ps_ref, hs_ref, ts_ref, hb_ref, carry_scr))

    const2 = lambda shape: pl.BlockSpec(shape, lambda i: (0, 0))
    return pl.pallas_call(
        kern,
        grid=(nrb + 1,),
        in_specs=[
            pl.BlockSpec((tm, D_MODEL), lambda i: (last(i), 0)),
            pl.BlockSpec((tm, LANES), lambda i: (last(i) % tb, 0)),
            pl.BlockSpec((tm, LANES), lambda i: (last(i) % tb, 0)),
            pl.BlockSpec((1, 1, RW_COLS), lambda i: (last(i) // tb, 0, 0)),
            const2((tm, D_MODEL)), const2((tm, LANES)), const2((tm, LANES)),
            pl.BlockSpec((nseq_s, 1, RW_COLS), lambda i: (0, 0, 0)),
            const2((1, D_MODEL)), const2((1, D_MODEL)),
            pl.BlockSpec((D_MODEL, IN_COLS), lambda i: (0, 0), pipeline_mode=pl.Buffered(1)),
            const2((1, RW_COLS)),
        ],
        out_specs=[
            pl.BlockSpec((tm, P_COLS), lambda i: (last(i), 0)),
            pl.BlockSpec((tm, D_MODEL), lambda i: (last(i), 0)),
            pl.BlockSpec((1, SUBLANES, RW_COLS), lambda i: (last(i), 0, 0)),
            const2((tm, P_COLS)), const2((tm, D_MODEL)),
            pl.BlockSpec((nseq_s, SUBLANES, RW_COLS), lambda i: (0, 0, 0)),
        ],
        out_shape=[
            jax.ShapeDtypeStruct((n_p, P_COLS), F32),
            jax.ShapeDtypeStruct((n_p, D_MODEL), F32),
            jax.ShapeDtypeStruct((nrb, SUBLANES, RW_COLS), F32),
            jax.ShapeDtypeStruct((n_s, P_COLS), F32),
            jax.ShapeDtypeStruct((n_s, D_MODEL), F32),
            jax.ShapeDtypeStruct((nseq_s, SUBLANES, RW_COLS), F32),
        ],
        scratch_shapes=[pltpu.VMEM((tm, D_MODEL), BF16), pltpu.VMEM((SUBLANES, RW_COLS), F32)],
        compiler_params=_cparams(("arbitrary",)),
        name="in_proj",
    )(xp, *tables_p, shift_p, xs, *tables_s, shift_s, g, b, w_p, mu)


def _attend(q, k, v, valid, sink_ref):
    r = q.shape[0]
    outs = []
    for kv in range(ATT_KV_HEADS):
        kh = k[:, kv * HEAD_DIM:(kv + 1) * HEAD_DIM].astype(BF16)
        vh = v[:, kv * HEAD_DIM:(kv + 1) * HEAD_DIM].astype(BF16)
        heads = [kv * ATT_GROUP + g for g in range(ATT_GROUP)]
        qs = jnp.concatenate([q[:, h * HEAD_DIM:(h + 1) * HEAD_DIM] for h in heads], axis=0)
        sink = jnp.concatenate([jnp.full((r, 1), sink_ref[h], F32) for h in heads], axis=0)
        s = lax.dot_general(qs.astype(BF16), kh, (((1,), (1,)), ((), ())), preferred_element_type=F32) * ATT_SCALE
        if valid is not None:
            s = jnp.where(valid, s, -jnp.inf)
        m = jnp.maximum(jnp.max(s, -1, keepdims=True), sink)
        p = jnp.exp(s - m)
        den = jnp.sum(p, -1, keepdims=True) + jnp.exp(sink - m)
        o = jnp.dot(p.astype(BF16), vh, preferred_element_type=F32) / den
        outs.extend(o[g * r:(g + 1) * r] for g in range(ATT_GROUP))
    return jnp.concatenate(outs, axis=1)


LOG2_E = math.log2(math.e)
QB = 2 * CHUNK


def _attn_prompt_kernel(sink_ref, q_ref, kp_ref, kc_ref, vp_ref, vc_ref, o_ref):
    m = pl.program_id(1)
    nseq = q_ref.shape[0]
    span = 2 * QB
    lane = lax.broadcasted_iota(jnp.int32, (span, KV_WIDTH), 1)
    row_c = 2 + lax.broadcasted_iota(jnp.int32, (QB, span), 0) // CHUNK
    col_c = lax.broadcasted_iota(jnp.int32, (QB, span), 1) // CHUNK
    back = row_c - col_c
    valid = jnp.logical_and(jnp.logical_and(back >= 0, back <= 2), 2 * m - 2 + col_c >= 0)
    out_lane = lax.broadcasted_iota(jnp.int32, (QB, PAIR), 1)

    def dup(x, kv):
        sw = pltpu.roll(x, HEAD_DIM, 1)
        return jnp.where(lane < HEAD_DIM, x, sw) if kv == 0 else jnp.where(lane < HEAD_DIM, sw, x)

    k_st, v_st = [], []
    for b in range(nseq):
        k_all = jnp.concatenate([kp_ref[b], kc_ref[b]], axis=0)
        v_all = jnp.concatenate([vp_ref[b], vc_ref[b]], axis=0)
        k_st.append([_stack_heads(dup(k_all, kv), HEAD_DIM).astype(BF16) for kv in range(ATT_KV_HEADS)])
        v_st.append([_stack_heads(dup(v_all, kv), HEAD_DIM).astype(BF16) for kv in range(ATT_KV_HEADS)])
    pairs = [(b, j) for b in range(nseq) for j in range(ATT_HEADS // 2)]
    kv_of = lambda j: (2 * j) // ATT_GROUP
    s = [lax.dot_general((q_ref[b, :, j * PAIR:(j + 1) * PAIR] * (ATT_SCALE * LOG2_E)).astype(BF16), k_st[b][kv_of(j)],
                         (((1,), (1,)), ((), ())), preferred_element_type=F32) for b, j in pairs]
    probs, dens = [], []
    for idx, (b, j) in enumerate(pairs):
        halves, den = [], []
        for e in range(2):
            sink = sink_ref[2 * j + e] * LOG2_E
            sh = jnp.where(valid, s[idx][:, e * span:(e + 1) * span], -jnp.inf)
            mx = jnp.maximum(jnp.max(sh, -1, keepdims=True), sink)
            ph = jnp.exp2(sh - mx)
            halves.append(ph)
            den.append(jnp.sum(ph, -1, keepdims=True) + jnp.exp2(sink - mx))
        probs.append(jnp.concatenate(halves, axis=1).astype(BF16))
        dens.append(jnp.where(out_lane < HEAD_DIM, den[0], den[1]))
    o = [jnp.dot(probs[idx], v_st[b][kv_of(j)], preferred_element_type=F32) / dens[idx]
         for idx, (b, j) in enumerate(pairs)]
    n_pairs = ATT_HEADS // 2
    for b in range(nseq):
        o_ref[b] = jnp.concatenate(o[b * n_pairs:(b + 1) * n_pairs], axis=1).astype(o_ref.dtype)


def _attn_prompt(p3, sinks, nseq):
    batch, seq = p3.shape[0], p3.shape[1]
    nb = seq // QB
    kcol, vcol = P_AK // KV_WIDTH, P_AV // KV_WIDTH
    prev = lambda col: pl.BlockSpec((nseq, QB, KV_WIDTH), lambda g, m: (g, jnp.maximum(m - 1, 0), col))
    cur = lambda col: pl.BlockSpec((nseq, QB, KV_WIDTH), lambda g, m: (g, m, col))
    return pl.pallas_call(
        _attn_prompt_kernel,
        grid=(batch // nseq, nb),
        in_specs=[
            pl.BlockSpec(memory_space=pltpu.SMEM),
            pl.BlockSpec((nseq, QB, ATT_WIDTH), lambda g, m: (g, m, P_Q // ATT_WIDTH)),
            prev(kcol), cur(kcol), prev(vcol), cur(vcol),
        ],
        out_specs=pl.BlockSpec((nseq, QB, ATT_WIDTH), lambda g, m: (g, m, 0)),
        out_shape=jax.ShapeDtypeStruct((batch, seq, ATT_WIDTH), BF16),
        compiler_params=_cparams(("arbitrary", "arbitrary")),
        name="attn_prompt",
    )(sinks, p3, p3, p3, p3, p3)


def _attn_sample_kernel(sink_ref, q_ref, kn_ref, vn_ref, kc_ref, vc_ref, o_ref, kw_ref, vw_ref):
    t = q_ref.shape[0]
    k = jnp.concatenate([kc_ref[...], kn_ref[...]], axis=0)
    v = jnp.concatenate([vc_ref[...], vn_ref[...]], axis=0)
    o_ref[...] = _attend(q_ref[...], k, v, None, sink_ref).astype(o_ref.dtype)
    kw_ref[...] = k[t:]
    vw_ref[...] = v[t:]


def _attn_sample(p, sinks, cache_k, cache_v, batch, seq):
    kcol, vcol = P_AK // KV_WIDTH, P_AV // KV_WIDTH
    return pl.pallas_call(
        _attn_sample_kernel,
        grid=(batch,),
        in_specs=[
            pl.BlockSpec(memory_space=pltpu.SMEM),
            pl.BlockSpec((seq, ATT_WIDTH), lambda b: (b, P_Q // ATT_WIDTH)),
            pl.BlockSpec((seq, KV_WIDTH), lambda b: (b, kcol)),
            pl.BlockSpec((seq, KV_WIDTH), lambda b: (b, vcol)),
            pl.BlockSpec((WINDOW, KV_WIDTH), lambda b: (b, 0)),
            pl.BlockSpec((WINDOW, KV_WIDTH), lambda b: (b, 0)),
        ],
        out_specs=[
            pl.BlockSpec((seq, ATT_WIDTH), lambda b: (b, 0)),
            pl.BlockSpec((WINDOW, KV_WIDTH), lambda b: (b, 0)),
            pl.BlockSpec((WINDOW, KV_WIDTH), lambda b: (b, 0)),
        ],
        out_shape=[
            jax.ShapeDtypeStruct((batch * seq, ATT_WIDTH), BF16),
            jax.ShapeDtypeStruct((batch * WINDOW, KV_WIDTH), F32),
            jax.ShapeDtypeStruct((batch * WINDOW, KV_WIDTH), F32),
        ],
        compiler_params=_cparams(("arbitrary",)),
        name="attn_sample",
    )(sinks, p, p, p, cache_k, cache_v)


SEG = 256
EXP_NEG_HALF = math.exp(-0.5)
PAIR = 2 * HEAD_DIM
N_PAIRS = RW_HEADS // 2


def _split2(x):
    hi = x.astype(BF16)
    lo = (x - hi.astype(F32)).astype(BF16)
    return hi, lo


def _seg_sum(x, bd):
    c = x.shape[0]
    xb = x.astype(BF16)
    stack = jnp.concatenate([xb[:, s:s + SEG] for s in range(0, RW_WIDTH, SEG)], axis=0)
    r = jnp.dot(stack, bd, preferred_element_type=F32)
    return jnp.concatenate([r[i * c:(i + 1) * c] for i in range(RW_WIDTH // SEG)], axis=1)


def _dotb(a, b):
    return jnp.dot(a.astype(BF16), b.astype(BF16), preferred_element_type=F32)


def _dotb_nt(a, b):
    return lax.dot_general(a.astype(BF16), b.astype(BF16), (((1,), (1,)), ((), ())), preferred_element_type=F32)


def _dotb_tn(a, b):
    return lax.dot_general(a.astype(BF16), b.astype(BF16), (((0,), (0,)), ((), ())), preferred_element_type=F32)


def _stack_heads(x, half):
    lane = lax.broadcasted_iota(jnp.int32, x.shape, 1)
    first = (lane % (2 * half)) < half
    zero = jnp.zeros_like(x)
    return jnp.concatenate([jnp.where(first, x, zero), jnp.where(first, zero, x)], axis=0)


def _stack_heads_bf16(x, half):
    return _stack_heads(x.astype(BF16), half)


def _rwkv_pairs(ab, rb, bb, kb, v, gam_c, s_bd, n_double):
    c = ab[0].shape[0]
    pairs = range(len(ab))
    t_idx = lax.broadcasted_iota(jnp.int32, (c, 2 * c), 0)
    s_idx = lax.broadcasted_iota(jnp.int32, (c, 2 * c), 1) % c
    strict, incl = s_idx < t_idx, s_idx <= t_idx
    g = [_dotb_nt(jnp.concatenate([ab[p], rb[p]], axis=0),
                  jnp.concatenate([_stack_heads_bf16(bb[p], HEAD_DIM), _stack_heads_bf16(kb[p], HEAD_DIM)], axis=0))
         for p in pairs]
    l_ak = [jnp.where(strict, g[p][:c, 2 * c:], 0.0) for p in pairs]
    m_rbk = [jnp.concatenate([jnp.where(incl, g[p][c:, :2 * c], 0.0), jnp.where(incl, g[p][c:, 2 * c:], 0.0)], axis=1)
             for p in pairs]
    lp = [jnp.where(strict, g[p][:c, :2 * c], 0.0) for p in pairs]
    v_st = [_stack_heads_bf16(v[p], HEAD_DIM) for p in pairs]
    y0 = [jnp.concatenate([ab[p], _dotb(l_ak[p], v_st[p])], axis=1) for p in pairs]
    n_acc = list(lp)
    lk_b = [lp[p].astype(BF16) for p in pairs]
    lk = [_dotb(lk_b[p], _stack_heads(lk_b[p], c)) for p in pairs]
    for k in range(1, n_double):
        n_st = [_stack_heads_bf16(n_acc[p], c) for p in pairs]
        lk_b = [lk[p].astype(BF16) for p in pairs]
        if k + 1 < n_double:
            z = [_dotb(lk_b[p], jnp.concatenate([n_st[p], _stack_heads(lk_b[p], c)], axis=1)) for p in pairs]
            n_acc = [n_acc[p] + lk[p] + z[p][:, :2 * c] for p in pairs]
            lk = [z[p][:, 2 * c:] for p in pairs]
        else:
            n_acc = [n_acc[p] + lk[p] + _dotb(lk_b[p], n_st[p]) for p in pairs]
    y = [y0[p] + _dotb(n_acc[p], _stack_heads_bf16(y0[p], HEAD_DIM)) for p in pairs]
    ya = [y[p][:, :PAIR] for p in pairs]
    yu = [y[p][:, PAIR:] for p in pairs]
    ou = []
    for p in pairs:
        lhs = jnp.concatenate([rb[p], ya[p]], axis=0).astype(BF16)
        ou.append(lax.dot_general(jnp.concatenate([lhs, lhs], axis=1), jnp.concatenate(_split2(s_bd[p]), axis=1),
                                  (((1,), (1,)), ((), ())), preferred_element_type=F32))
    u = [ou[p][c:] + yu[p] for p in pairs]
    o = [ou[p][:c] + _dotb(m_rbk[p], jnp.concatenate([_stack_heads_bf16(u[p], HEAD_DIM), v_st[p]], axis=0))
         for p in pairs]
    row_h = lax.broadcasted_iota(jnp.int32, (PAIR, PAIR), 0) // HEAD_DIM
    col_h = lax.broadcasted_iota(jnp.int32, (PAIR, PAIR), 1) // HEAD_DIM
    same = row_h == col_h
    d_t = [jnp.where(same, _dotb_tn(jnp.concatenate([u[p], v[p]], axis=0),
                                    jnp.concatenate([bb[p], kb[p]], axis=0)), 0.0) for p in pairs]
    s_new = [(s_bd[p] + d_t[p]) * gam_c[p] for p in pairs]
    return o, s_new


def _rwkv_kernel(xr_ref, xk_ref, xv_ref, xl_ref, w0_ref, a0_ref, kk_ref, ka_ref, rk_ref,
                 lg_ref, lb_ref, w2_ref, a2_ref, g2_ref, bd_ref, st0_ref,
                 out_ref, st_ref, s_scr):
    c = pl.program_id(1)
    nc = pl.num_programs(1)
    nb, C = xr_ref.shape[0], xr_ref.shape[1]
    n_double = int(math.log2(C))

    @pl.when(c == 0)
    def _():
        s_scr[...] = jnp.zeros_like(s_scr)
        for b in range(nb):
            for h in range(RW_HEADS):
                off = (h % 2) * HEAD_DIM
                s_scr[b * N_PAIRS + h // 2, off:off + HEAD_DIM, off:off + HEAD_DIM] = st0_ref[b, h]

    rows_of = lambda load: jnp.concatenate([load(b) for b in range(nb)], axis=0)
    wa_w = DECAY_LORA + AAA_LORA
    r = rows_of(lambda b: xr_ref[b])
    kr = rows_of(lambda b: xk_ref[b])
    v = rows_of(lambda b: xv_ref[b])
    xw = rows_of(lambda b: xl_ref[b, :, :wa_w])
    gd = rows_of(lambda b: xl_ref[b, :, wa_w:wa_w + GATE_LORA])
    wd = xw[:, 0:DECAY_LORA]
    ad = xw[:, DECAY_LORA:wa_w]

    lw = (-EXP_NEG_HALF * LOG2_E) * jax.nn.sigmoid(w0_ref[...] + _dotb(jnp.tanh(wd), w2_ref[...]))
    a = jax.nn.sigmoid(a0_ref[...] + _dotb(ad, a2_ref[...]))
    g = _dotb(jax.nn.sigmoid(gd), g2_ref[...])
    bd = bd_ref[...]
    kk = kr * kk_ref[...]
    kk = kk * lax.rsqrt(jnp.maximum(_seg_sum(kk * kk, bd), 1e-24))
    k2 = kr * (1.0 + (a - 1.0) * ka_ref[...])

    rows = nb * C
    ti = lax.broadcasted_iota(jnp.int32, (rows, rows), 0)
    si = lax.broadcasted_iota(jnp.int32, (rows, rows), 1)
    tri = jnp.logical_and(si <= ti, si // C == ti // C).astype(BF16)
    cs2 = jnp.dot(tri, jnp.concatenate(_split2(lw), axis=1), preferred_element_type=F32)
    cs = cs2[:, :RW_WIDTH] + cs2[:, RW_WIDTH:]
    gam = jnp.exp2(cs)
    ginv = jnp.exp2(-cs)
    ab = -kk * jnp.exp2(cs - lw)
    rb = r * gam
    bb = kk * a * ginv
    kb = k2 * ginv

    def per_pair(x, r0, r1):
        return [x[b * C + r0:b * C + r1, p * PAIR:(p + 1) * PAIR] for b in range(nb) for p in range(N_PAIRS)]

    chains = nb * N_PAIRS
    outs, s_new = _rwkv_pairs(per_pair(ab, 0, C), per_pair(rb, 0, C), per_pair(bb, 0, C), per_pair(kb, 0, C),
                              per_pair(v, 0, C), per_pair(gam, C - 1, C), [s_scr[q] for q in range(chains)], n_double)
    for q in range(chains):
        s_scr[q] = s_new[q]
    o = jnp.concatenate([jnp.concatenate(outs[b * N_PAIRS:(b + 1) * N_PAIRS], axis=1) for b in range(nb)], axis=0)

    mo = _seg_sum(o, bd) * (1.0 / HEAD_DIM)
    d = o - mo
    vo = _seg_sum(d * d, bd) * (1.0 / HEAD_DIM)
    on = d * lax.rsqrt(vo + LNX_EPS) * lg_ref[...] + lb_ref[...]
    bonus = _seg_sum(r * k2 * rk_ref[...], bd) * v
    res = ((on + bonus) * g).astype(out_ref.dtype)
    for b in range(nb):
        out_ref[b] = res[b * C:(b + 1) * C, :]

    @pl.when(c == nc - 1)
    def _():
        for b in range(nb):
            for h in range(RW_HEADS):
                off = (h % 2) * HEAD_DIM
                st_ref[b, h] = s_scr[b * N_PAIRS + h // 2, off:off + HEAD_DIM, off:off + HEAD_DIM]


def _rwkv(p3, state0, prm, chunk, nb):
    batch, seq = p3.shape[0], p3.shape[1]
    nc = seq // chunk
    row = lambda w: pl.BlockSpec((1, w), lambda g, c: (0, 0))
    xspec = lambda w, col: pl.BlockSpec((nb, chunk, w), lambda g, c: (g, c, col))
    full = lambda a: pl.BlockSpec(a.shape, lambda g, c: (0,) * a.ndim)
    stspec = pl.BlockSpec((nb, RW_HEADS, HEAD_DIM, HEAD_DIM), lambda g, c: (g, 0, 0, 0))
    return pl.pallas_call(
        _rwkv_kernel,
        grid=(batch // nb, nc),
        in_specs=[
            xspec(RW_WIDTH, P_R // RW_WIDTH), xspec(RW_WIDTH, P_K // RW_WIDTH), xspec(RW_WIDTH, P_V // RW_WIDTH),
            xspec(LORA_W, P_LORA // LORA_W),
            row(RW_WIDTH), row(RW_WIDTH), row(RW_WIDTH), row(RW_WIDTH), row(RW_WIDTH),
            row(RW_WIDTH), row(RW_WIDTH),
            full(prm["w2"]), full(prm["a2"]), full(prm["g2"]), full(prm["bd"]),
            stspec,
        ],
        out_specs=[
            pl.BlockSpec((nb, chunk, RW_WIDTH), lambda g, c: (g, c, 0)),
            stspec,
        ],
        out_shape=[
            jax.ShapeDtypeStruct((batch, seq, RW_WIDTH), BF16),
            jax.ShapeDtypeStruct((batch, RW_HEADS, HEAD_DIM, HEAD_DIM), F32),
        ],
        scratch_shapes=[pltpu.VMEM((nb * N_PAIRS, PAIR, PAIR), F32)],
        compiler_params=_cparams(("arbitrary", "arbitrary")),
        name="rwkv",
    )(p3, p3, p3, p3, prm["w0"], prm["a0"], prm["k_k"], prm["k_a"], prm["r_k"],
      prm["lnx_g"], prm["lnx_b"], prm["w2"], prm["a2"], prm["g2"], prm["bd"], state0)


D_TN = 512


def _out_proj_body(att_ref, rw_ref, h_ref, g_ref, b_ref, h1_ref, h1b_ref, w_scr):
    for c0 in range(0, D_MODEL, D_TN):
        m = jnp.dot(att_ref[...], w_scr[:ATT_WIDTH, c0:c0 + D_TN], preferred_element_type=F32)
        m = m + jnp.dot(rw_ref[...], w_scr[ATT_WIDTH:, c0:c0 + D_TN], preferred_element_type=F32)
        h1_ref[:, c0:c0 + D_TN] = ALPHA * h_ref[:, c0:c0 + D_TN] + m
    y = _layer_norm(h1_ref[...], g_ref[...], b_ref[...])
    h1_ref[...] = y
    h1b_ref[...] = y.astype(BF16)


def _out_proj(att_p, rw_p, h_p, att_s, rw_s, h_s, w_out, g, b, tm):
    n_p, n_s = att_p.shape[0], att_s.shape[0]
    nrb = n_p // tm
    last = lambda i: jnp.minimum(i, nrb - 1)

    def kern(ap_ref, rp_ref, hp_ref, as_ref, rs_ref, hs_ref, wa_ref, wb_ref, g_ref, b_ref,
             h1p_ref, h1bp_ref, h1s_ref, h1bs_ref, w_scr):
        i = pl.program_id(0)

        @pl.when(i == 0)
        def _():
            w_scr[:ATT_WIDTH, :] = wa_ref[...].astype(BF16)
            w_scr[ATT_WIDTH:, :] = wb_ref[...].astype(BF16)

        pl.when(i < nrb)(lambda: _out_proj_body(ap_ref, rp_ref, hp_ref, g_ref, b_ref, h1p_ref, h1bp_ref, w_scr))
        pl.when(i == nrb)(lambda: _out_proj_body(as_ref, rs_ref, hs_ref, g_ref, b_ref, h1s_ref, h1bs_ref, w_scr))

    rows_p = lambda w: pl.BlockSpec((tm, w), lambda i: (last(i), 0))
    rows_s = lambda w: pl.BlockSpec((n_s, w), lambda i: (0, 0))
    return pl.pallas_call(
        kern,
        grid=(nrb + 1,),
        in_specs=[
            rows_p(ATT_WIDTH), rows_p(RW_WIDTH), rows_p(D_MODEL),
            rows_s(ATT_WIDTH), rows_s(RW_WIDTH), rows_s(D_MODEL),
            pl.BlockSpec((ATT_WIDTH, D_MODEL), lambda i: (0, 0), pipeline_mode=pl.Buffered(1)),
            pl.BlockSpec((RW_WIDTH, D_MODEL), lambda i: (1, 0), pipeline_mode=pl.Buffered(1)),
            pl.BlockSpec((1, D_MODEL), lambda i: (0, 0)),
            pl.BlockSpec((1, D_MODEL), lambda i: (0, 0)),
        ],
        out_specs=[rows_p(D_MODEL), rows_p(D_MODEL), rows_s(D_MODEL), rows_s(D_MODEL)],
        out_shape=[
            jax.ShapeDtypeStruct((n_p, D_MODEL), F32), jax.ShapeDtypeStruct((n_p, D_MODEL), BF16),
            jax.ShapeDtypeStruct((n_s, D_MODEL), F32), jax.ShapeDtypeStruct((n_s, D_MODEL), BF16),
        ],
        scratch_shapes=[pltpu.VMEM((ATT_WIDTH + RW_WIDTH, D_MODEL), BF16)],
        compiler_params=_cparams(("arbitrary",)),
        name="out_proj",
    )(att_p, rw_p, h_p, att_s, rw_s, h_s, w_out, w_out, g, b)


E_TN = 512
E_NCB = D_FF // E_TN
E_SUB = 1024
SQRT_HALF = math.sqrt(0.5)


def _ffn_up_body(bps, nseq, i, h_ref, w_scr, cwg_ref, cwv_ref, cbg_ref, cbv_ref, sg_ref, sv_ref,
                 act_ref, og_ref, ov_ref, cg_scr, cv_scr):
    tm = h_ref.shape[0]
    sub = min(tm, E_SUB)
    n_sub = tm // sub
    t_seq = tm // nseq
    row8 = lax.broadcasted_iota(jnp.int32, (SUBLANES, E_TN), 0)

    def products(r0):
        u = jnp.dot(h_ref[r0:r0 + sub, :], w_scr[...], preferred_element_type=F32)
        return u[:, :E_TN], u[:, E_TN:]

    def conv(u, prevs, cw_ref, cb_ref):
        sh1, sh2 = pltpu.roll(u, 1, 0), pltpu.roll(u, 2, 0)
        for r, prev in prevs:
            p2, p1 = prev[0:1, :], prev[1:2, :]
            fix1 = jnp.where(row8 == 0, p1, sh1[r:r + SUBLANES, :])
            fix2 = jnp.where(row8 == 0, p2, jnp.where(row8 == 1, p1, sh2[r:r + SUBLANES, :]))
            head = lambda x: [x[:r, :]] if r > 0 else []
            tail = lambda x: [x[r + SUBLANES:, :]] if r + SUBLANES < x.shape[0] else []
            sh1 = jnp.concatenate(head(sh1) + [fix1] + tail(sh1), axis=0)
            sh2 = jnp.concatenate(head(sh2) + [fix2] + tail(sh2), axis=0)
        cw = cw_ref[...]
        c = cb_ref[...] + sh2 * cw[0:1, :]
        c = c + sh1 * cw[1:2, :]
        return c + u * cw[2:3, :]

    def epilogue(r0, ug, uv, prev_g, prev_v):
        gate = conv(ug, prev_g, cwg_ref, cbg_ref)
        val = conv(uv, prev_v, cwv_ref, cbv_ref)
        gelu = 0.5 * gate * (1.0 + lax.erf(gate * SQRT_HALF))
        act_ref[r0:r0 + sub, :] = (gelu * val).astype(act_ref.dtype)

    if nseq > 1:
        ug, uv = products(0)
        epilogue(0, ug, uv, [(s * t_seq, sg_ref[s]) for s in range(nseq)],
                 [(s * t_seq, sv_ref[s]) for s in range(nseq)])
        for s in range(nseq):
            og_ref[s] = ug[(s + 1) * t_seq - SUBLANES:(s + 1) * t_seq, :]
            ov_ref[s] = uv[(s + 1) * t_seq - SUBLANES:(s + 1) * t_seq, :]
    else:
        first = (i % bps) == 0
        prev_g = jnp.where(first, sg_ref[0], cg_scr[SUBLANES - 2:SUBLANES, :])
        prev_v = jnp.where(first, sv_ref[0], cv_scr[SUBLANES - 2:SUBLANES, :])
        nxt = products(0)
        for k in range(n_sub):
            ug, uv = nxt
            if k + 1 < n_sub:
                nxt = products((k + 1) * sub)
            epilogue(k * sub, ug, uv, [(0, prev_g)], [(0, prev_v)])
            prev_g, prev_v = ug[sub - 2:sub, :], uv[sub - 2:sub, :]
        cg_scr[...] = ug[sub - SUBLANES:sub, :]
        cv_scr[...] = uv[sub - SUBLANES:sub, :]
        og_ref[0] = ug[sub - SUBLANES:sub, :]
        ov_ref[0] = uv[sub - SUBLANES:sub, :]


def _ffn_up(h_p, conv_prev_p, seq_p, tm_p, h_s, conv_prev_s, seq_s, w_up, conv_w, conv_b, w_down):
    n_p, n_s = h_p.shape[0], h_s.shape[0]
    batch_p, batch_s = n_p // seq_p, n_s // seq_s
    bps = seq_p // tm_p
    nrb = n_p // tm_p
    slab = D_FF // (E_NCB * nrb)
    last = lambda i: jnp.maximum(i - 1, 0)

    def kern(hp_ref, sgp_ref, svp_ref, hs_ref, sgs_ref, svs_ref, wg_ref, wv_ref, cwg_ref, cwv_ref, cbg_ref, cbv_ref,
             wd_ref, actp_ref, ogp_ref, ovp_ref, acts_ref, ogs_ref, ovs_ref, wdb_ref, w_scr, cg_scr, cv_scr):
        i = pl.program_id(1)

        @pl.when(i == 0)
        def _():
            w_scr[:, :E_TN] = wg_ref[...].astype(BF16)
            w_scr[:, E_TN:] = wv_ref[...].astype(BF16)
            _ffn_up_body(1, batch_s, i, hs_ref, w_scr, cwg_ref, cwv_ref, cbg_ref, cbv_ref, sgs_ref, svs_ref,
                         acts_ref, ogs_ref, ovs_ref, cg_scr, cv_scr)

        @pl.when(i > 0)
        def _():
            wdb_ref[...] = wd_ref[...].astype(BF16)
            _ffn_up_body(bps, 1, i - 1, hp_ref, w_scr, cwg_ref, cwv_ref, cbg_ref, cbv_ref, sgp_ref, svp_ref,
                         actp_ref, ogp_ref, ovp_ref, cg_scr, cv_scr)

    col = lambda shape, off: pl.BlockSpec(shape, lambda j, i: (0,) * (len(shape) - 1) + (j + off,))
    st_p = lambda off: pl.BlockSpec((1, CONV_W - 1, E_TN), lambda j, i: (last(i) // bps, 0, j + off))
    st_s = lambda off: pl.BlockSpec((batch_s, CONV_W - 1, E_TN), lambda j, i: (0, 0, j + off))
    tail_p = pl.BlockSpec((1, SUBLANES, E_TN), lambda j, i: (last(i) // bps, 0, j))
    tail_s = pl.BlockSpec((batch_s, SUBLANES, E_TN), lambda j, i: (0, 0, j))
    wd_spec = pl.BlockSpec((slab, D_MODEL), lambda j, i: (j * nrb + last(i), 0))
    return pl.pallas_call(
        kern,
        grid=(E_NCB, nrb + 1),
        in_specs=[
            pl.BlockSpec((tm_p, D_MODEL), lambda j, i: (last(i), 0)), st_p(0), st_p(E_NCB),
            pl.BlockSpec((n_s, D_MODEL), lambda j, i: (0, 0)), st_s(0), st_s(E_NCB),
            col((D_MODEL, E_TN), 0), col((D_MODEL, E_TN), E_NCB),
            col((CONV_W, E_TN), 0), col((CONV_W, E_TN), E_NCB),
            col((1, E_TN), 0), col((1, E_TN), E_NCB),
            wd_spec,
        ],
        out_specs=[
            pl.BlockSpec((tm_p, E_TN), lambda j, i: (last(i), j)), tail_p, tail_p,
            pl.BlockSpec((n_s, E_TN), lambda j, i: (0, j)), tail_s, tail_s,
            wd_spec,
        ],
        out_shape=[
            jax.ShapeDtypeStruct((n_p, D_FF), BF16),
            jax.ShapeDtypeStruct((batch_p, SUBLANES, D_FF), F32), jax.ShapeDtypeStruct((batch_p, SUBLANES, D_FF), F32),
            jax.ShapeDtypeStruct((n_s, D_FF), BF16),
            jax.ShapeDtypeStruct((batch_s, SUBLANES, D_FF), F32), jax.ShapeDtypeStruct((batch_s, SUBLANES, D_FF), F32),
            jax.ShapeDtypeStruct((D_FF, D_MODEL), BF16),
        ],
        scratch_shapes=[pltpu.VMEM((D_MODEL, 2 * E_TN), BF16),
                        pltpu.VMEM((SUBLANES, E_TN), F32), pltpu.VMEM((SUBLANES, E_TN), F32)],
        compiler_params=_cparams(("arbitrary", "arbitrary")),
        name="ffn_up",
    )(h_p, conv_prev_p, conv_prev_p, h_s, conv_prev_s, conv_prev_s, w_up, w_up, conv_w, conv_w, conv_b, conv_b, w_down)


F_TN = 512


def _ffn_down_kernel(act_ref, w_ref, h1_ref, g_ref, b_ref, y_ref):
    for c0 in range(0, D_MODEL, F_TN):
        f = jnp.dot(act_ref[...], w_ref[:, c0:c0 + F_TN], preferred_element_type=F32)
        y_ref[:, c0:c0 + F_TN] = ALPHA * h1_ref[:, c0:c0 + F_TN] + f
    y_ref[...] = _layer_norm(y_ref[...], g_ref[...], b_ref[...])


def _ffn_down(act_p, h1_p, act_s, h1_s, w_down, g, b, tm):
    n_p, n_s = act_p.shape[0], act_s.shape[0]
    nrb = n_p // tm
    last = lambda i: jnp.minimum(i, nrb - 1)

    def kern(ap_ref, hp_ref, as_ref, hs_ref, w_ref, g_ref, b_ref, yp_ref, ys_ref):
        i = pl.program_id(0)
        pl.when(i < nrb)(lambda: _ffn_down_kernel(ap_ref, w_ref, hp_ref, g_ref, b_ref, yp_ref))
        pl.when(i == nrb)(lambda: _ffn_down_kernel(as_ref, w_ref, hs_ref, g_ref, b_ref, ys_ref))

    rows_p = lambda w: pl.BlockSpec((tm, w), lambda i: (last(i), 0))
    rows_s = lambda w: pl.BlockSpec((n_s, w), lambda i: (0, 0))
    return pl.pallas_call(
        kern,
        grid=(nrb + 1,),
        in_specs=[
            rows_p(D_FF), rows_p(D_MODEL), rows_s(D_FF), rows_s(D_MODEL),
            pl.BlockSpec((D_FF, D_MODEL), lambda i: (0, 0), pipeline_mode=pl.Buffered(1)),
            pl.BlockSpec((1, D_MODEL), lambda i: (0, 0)),
            pl.BlockSpec((1, D_MODEL), lambda i: (0, 0)),
        ],
        out_specs=[rows_p(D_MODEL), rows_s(D_MODEL)],
        out_shape=[jax.ShapeDtypeStruct((n_p, D_MODEL), F32), jax.ShapeDtypeStruct((n_s, D_MODEL), F32)],
        compiler_params=_cparams(("arbitrary",)),
        name="ffn_down",
    )(act_p, h1_p, act_s, h1_s, w_down, g, b)


def _rope_tables(pos, reps):
    half = HEAD_DIM // 2
    inv = ROPE_THETA ** (-jnp.arange(half, dtype=F32) / half)
    ang = pos.astype(F32)[:, None] * inv[None, :]
    cos, sin = jnp.cos(ang), jnp.sin(ang)
    cos128 = jnp.concatenate([cos, cos, cos, cos], axis=1)
    sin128 = jnp.concatenate([-sin, sin, -sin, sin], axis=1)
    return jnp.tile(cos128, (reps, 1)), jnp.tile(sin128, (reps, 1))


TM = 256
D_TM = 256
E_TM = 1024
F_TM = 256
RWKV_NB = 4
ATTN_NB = 2


def kernel(x_prompt, x_sample, cache_k, cache_v, state_wkv, state_shift, state_ffn_conv, ln_in_g, ln_in_b, w_in, attn_sinks, rw_mu, rw_w0, rw_w2, rw_a0, rw_a2, rw_g2, rw_k_k, rw_k_a, rw_r_k, rw_lnx_g, rw_lnx_b, w_out, ln1_g, ln1_b, ffn_w_up, ffn_conv_w, ffn_conv_b, ffn_w_down, ln2_g, ln2_b):
    l = 0
    seg = lax.broadcasted_iota(jnp.int32, (SEG, SEG), 0) // HEAD_DIM
    bd = (seg == seg.T).astype(BF16)
    row = lambda a: a.reshape(1, -1)
    rw_prm = {"w0": row(rw_w0[l]), "a0": row(rw_a0[l]), "k_k": row(rw_k_k[l]),
              "k_a": row(rw_k_a[l]), "r_k": row(rw_r_k[l]), "lnx_g": row(rw_lnx_g[l]), "lnx_b": row(rw_lnx_b[l]),
              "w2": rw_w2[l].astype(BF16), "a2": rw_a2[l].astype(BF16), "g2": rw_g2[l].astype(BF16), "bd": bd}
    sinks = attn_sinks[l]
    bp, tp = x_prompt.shape[0], x_prompt.shape[1]
    bs, ts = x_sample.shape[0], x_sample.shape[1]
    n_p, n_s = bp * tp, bs * ts
    zeros = lambda *s: jnp.zeros(s, F32)

    p_p, h_p, tails_p, p_s, h_s, tails_s = _in_proj(
        x_prompt.reshape(n_p, D_MODEL), x_sample.reshape(n_s, D_MODEL), row(ln_in_g), row(ln_in_b),
        w_in[l].astype(BF16), row(rw_mu[l]), zeros(bp, 1, RW_COLS), state_shift[l],
        _rope_tables(jnp.arange(tp, dtype=jnp.int32), 1),
        _rope_tables(PAST_LEN + jnp.arange(ts, dtype=jnp.int32), bs), TM, tp, ts)
    p3_p, p3_s = p_p.reshape(bp, tp, P_COLS), p_s.reshape(bs, ts, P_COLS)

    att_p = _attn_prompt(p3_p, sinks, ATTN_NB).reshape(n_p, ATT_WIDTH)
    att_s, s_k, s_v = _attn_sample(p_s, sinks, cache_k[l].reshape(bs * WINDOW, KV_WIDTH),
                                   cache_v[l].reshape(bs * WINDOW, KV_WIDTH), bs, ts)
    rw_p, p_wkv = _rwkv(p3_p, zeros(bp, RW_HEADS, HEAD_DIM, HEAD_DIM), rw_prm, CHUNK, RWKV_NB)
    rw_s, s_wkv = _rwkv(p3_s, state_wkv[l], rw_prm, ts, RWKV_NB)

    h1_p, h1b_p, h1_s, h1b_s = _out_proj(att_p, rw_p.reshape(n_p, RW_WIDTH), h_p, att_s, rw_s.reshape(n_s, RW_WIDTH), h_s,
                                         w_out[l], row(ln1_g[l]), row(ln1_b[l]), D_TM)
    act_p, cg_p, cv_p, act_s, cg_s, cv_s, w_down_b = _ffn_up(
        h1b_p, zeros(bp, CONV_W - 1, 2 * D_FF), tp, E_TM, h1b_s, state_ffn_conv[l], ts,
        ffn_w_up[l], ffn_conv_w[l], row(ffn_conv_b[l]), ffn_w_down[l])
    y_p, y_s = _ffn_down(act_p, h1_p, act_s, h1_s, w_down_b, row(ln2_g[l]), row(ln2_b[l]), F_TM)

    kv_shape = lambda b: (1, b, WINDOW, ATT_KV_HEADS, HEAD_DIM)
    p_k = p3_p[:, tp - WINDOW:, P_AK:P_AK + KV_WIDTH].reshape(kv_shape(bp))
    p_v = p3_p[:, tp - WINDOW:, P_AV:P_AV + KV_WIDTH].reshape(kv_shape(bp))
    shift_of = lambda tails, b: tails.reshape(b, -1, RW_COLS)[:, -1:, :][None]
    conv_of = lambda cg, cv: jnp.concatenate([cg[:, SUBLANES - 2:, :], cv[:, SUBLANES - 2:, :]], axis=-1)[None]
    return (y_p.reshape(bp, tp, D_MODEL), y_s.reshape(bs, ts, D_MODEL), p_k, p_v, p_wkv[None],
            shift_of(tails_p, bp), conv_of(cg_p, cv_p),
            s_k.reshape(kv_shape(bs)), s_v.reshape(kv_shape(bs)), s_wkv[None],
            shift_of(tails_s, bs), conv_of(cg_s, cv_s))
```

```python
import math

import jax
import jax.numpy as jnp
from jax import lax
from jax.experimental import pallas as pl
from jax.experimental.pallas import tpu as pltpu

D_MODEL = 2048
HEAD_DIM = 64
ATT_HEADS = 16
ATT_KV_HEADS = 2
ATT_GROUP = ATT_HEADS // ATT_KV_HEADS
ATT_WIDTH = ATT_HEADS * HEAD_DIM
KV_WIDTH = ATT_KV_HEADS * HEAD_DIM
CHUNK = 64
WINDOW = 128
ROPE_THETA = 10000.0
ATT_SCALE = HEAD_DIM ** -0.5
RW_HEADS = 16
RW_WIDTH = RW_HEADS * HEAD_DIM
DECAY_LORA = 64
AAA_LORA = 64
GATE_LORA = 160
LNX_EPS = 64e-5
ATT_COLS = ATT_WIDTH + 2 * KV_WIDTH
RW_COLS = 3 * RW_WIDTH + DECAY_LORA + AAA_LORA + GATE_LORA
D_FF = 5632
CONV_W = 3
LN_EPS = 1e-5
DEPTH = 1
ALPHA = (2 * DEPTH) ** 0.25
PAST_LEN = 1024

LORA_W = 512
P_R, P_K, P_V, P_Q = 0, 1024, 2048, 3072
P_LORA = 4096
P_AK = P_LORA + LORA_W
P_AV = P_AK + KV_WIDTH
P_COLS = P_AV + KV_WIDTH
LANES = 128
SUBLANES = 8
VMEM_LIMIT = 56 * 1024 * 1024

F32 = jnp.float32
BF16 = jnp.bfloat16


def _cparams(sem):
    return pltpu.CompilerParams(dimension_semantics=sem, vmem_limit_bytes=VMEM_LIMIT)


def _layer_norm(y, g, b):
    mu = jnp.mean(y, -1, keepdims=True)
    d = y - mu
    var = jnp.mean(d * d, -1, keepdims=True)
    return d * lax.rsqrt(var + LN_EPS) * g + b


A_TN = 512


def _rope128(x, cos, sin_signed):
    lane = lax.broadcasted_iota(jnp.int32, x.shape, 1)
    first_half = (lane % HEAD_DIM) < (HEAD_DIM // 2)
    rot = jnp.where(first_half, pltpu.roll(x, LANES - HEAD_DIM // 2, 1), pltpu.roll(x, HEAD_DIM // 2, 1))
    return x * cos + rot * sin_signed


def _is_rope_col(col):
    return (P_Q <= col < P_Q + ATT_WIDTH) or (P_AK <= col < P_AK + KV_WIDTH)


O_R = ATT_COLS
O_WD = O_R + RW_WIDTH
O_K = O_WD + DECAY_LORA
O_V = O_K + RW_WIDTH
O_AD = O_V + RW_WIDTH
O_GD = O_AD + AAA_LORA
IN_COLS = O_GD + GATE_LORA
IN_COLS_PAD = -(-IN_COLS // LANES) * LANES


def _in_proj_kernel(first, x_ref, g_ref, b_ref, w_ref, cos_ref, sin_ref, mu_ref, st_ref, p_ref, h_ref, tail_ref,
                    hb_ref, carry_scr):
    h = _layer_norm(x_ref[...], g_ref[...], b_ref[...])
    h_ref[...] = h
    hb_ref[...] = h.astype(BF16)
    tm = x_ref.shape[0]
    nseq = tail_ref.shape[0]
    t_seq = tm // nseq
    row8 = lax.broadcasted_iota(jnp.int32, (SUBLANES, 1), 0)
    if nseq == 1:
        @pl.when(first)
        def _():
            carry_scr[...] = jnp.zeros_like(carry_scr)

    def product(c0, width):
        acc = jnp.dot(hb_ref[...], w_ref[:, c0:c0 + width], preferred_element_type=F32)
        if c0 < O_R:
            return acc
        cols = slice(c0 - O_R, c0 - O_R + width)
        prev = pltpu.roll(acc, 1, 0)
        for s in range(nseq):
            r = s * t_seq
            before = st_ref[s][:, cols]
            if nseq == 1:
                before = jnp.where(first, before, carry_scr[SUBLANES - 1:SUBLANES, cols])
            fixed = jnp.where(row8 == 0, before, prev[r:r + SUBLANES, :])
            head = [prev[:r, :]] if r > 0 else []
            prev = jnp.concatenate(head + [fixed, prev[r + SUBLANES:, :]], axis=0)
            tail_ref[s, :, cols] = acc[r + t_seq - SUBLANES:r + t_seq, :]
        if nseq == 1:
            carry_scr[:, cols] = acc[tm - SUBLANES:tm, :]
        return acc + (prev - acc) * mu_ref[:, cols]

    for src, width, dst in ((0, ATT_WIDTH, P_Q), (ATT_WIDTH, 2 * KV_WIDTH, P_AK), (O_R, RW_WIDTH, P_R)):
        for c0 in range(0, width, A_TN):
            acc = product(src + c0, min(A_TN, width - c0))
            for l0 in range(0, acc.shape[1], LANES):
                col = dst + c0 + l0
                part = acc[:, l0:l0 + LANES]
                if _is_rope_col(col):
                    part = _rope128(part, cos_ref[...], sin_ref[...])
                p_ref[:, col:col + LANES] = part

    low = lax.broadcasted_iota(jnp.int32, (x_ref.shape[0], LANES), 1) < HEAD_DIM
    span = O_GD - O_WD
    rolled, first_piece = [], None
    for c0 in range(0, span, A_TN):
        acc = product(O_WD + c0, min(A_TN, span - c0))
        for l0 in range(0, acc.shape[1], LANES):
            piece = acc[:, l0:l0 + LANES]
            j = (c0 + l0) // LANES
            if j == 0:
                first_piece = piece
            rolled.append(pltpu.roll(piece, HEAD_DIM, 1))
            if j >= 1:
                m = j - 1
                dst = P_K + m * LANES if m < RW_WIDTH // LANES else P_V + (m - RW_WIDTH // LANES) * LANES
                p_ref[:, dst:dst + LANES] = jnp.where(low, rolled[m], rolled[m + 1])
            if j == span // LANES - 1:
                p_ref[:, P_LORA:P_LORA + LANES] = jnp.where(low, first_piece, piece)
    gd0 = P_LORA + DECAY_LORA + AAA_LORA
    p_ref[:, gd0:gd0 + GATE_LORA] = product(O_GD, GATE_LORA)
    p_ref[:, gd0 + GATE_LORA:P_LORA + LORA_W] = jnp.zeros((x_ref.shape[0], LORA_W - DECAY_LORA - AAA_LORA - GATE_LORA), F32)


def _in_proj(xp, xs, g, b, w_p, mu, shift_p, shift_s, tables_p, tables_s, tm, seq_p, seq_s):
    n_p, n_s = xp.shape[0], xs.shape[0]
    assert n_s == tm
    nrb = n_p // tm
    nseq_s = tm // seq_s
    tb = seq_p // tm
    last = lambda i: jnp.minimum(i, nrb - 1)

    def kern(xp_ref, cp_ref, sp_ref, stp_ref, xs_ref, cs_ref, ss_ref, sts_ref, g_ref, b_ref, w_ref, mu_ref,
             pp_ref, hp_ref, tp_ref, ps_ref, hs_ref, ts_ref, hb_ref, carry_scr):
        i = pl.program_id(0)
        pl.when(i < nrb)(lambda: _in_proj_kernel(i % tb == 0, xp_ref, g_ref, b_ref, w_ref, cp_ref, sp_ref, mu_ref, stp_ref,
                                                 pp_ref, hp_ref, tp_ref, hb_ref, carry_scr))
        pl.when(i == nrb)(lambda: _in_proj_kernel(None, xs_ref, g_ref, b_ref, w_ref, cs_ref, ss_ref, mu_ref, sts_ref,
                                                  ps_ref, hs_ref, ts_ref, hb_ref, carry_scr))

    const2 = lambda shape: pl.BlockSpec(shape, lambda i: (0, 0))
    return pl.pallas_call(
        kern,
        grid=(nrb + 1,),
        in_specs=[
            pl.BlockSpec((tm, D_MODEL), lambda i: (last(i), 0)),
            pl.BlockSpec((tm, LANES), lambda i: (last(i) % tb, 0)),
            pl.BlockSpec((tm, LANES), lambda i: (last(i) % tb, 0)),
            pl.BlockSpec((1, 1, RW_COLS), lambda i: (last(i) // tb, 0, 0)),
            const2((tm, D_MODEL)), const2((tm, LANES)), const2((tm, LANES)),
            pl.BlockSpec((nseq_s, 1, RW_COLS), lambda i: (0, 0, 0)),
            const2((1, D_MODEL)), const2((1, D_MODEL)),
            pl.BlockSpec((D_MODEL, IN_COLS_PAD), lambda i: (0, 0), pipeline_mode=pl.Buffered(1)),
            const2((1, RW_COLS)),
        ],
        out_specs=[
            pl.BlockSpec((tm, P_COLS), lambda i: (last(i), 0)),
            pl.BlockSpec((tm, D_MODEL), lambda i: (last(i), 0)),
            pl.BlockSpec((1, SUBLANES, RW_COLS), lambda i: (last(i), 0, 0)),
            const2((tm, P_COLS)), const2((tm, D_MODEL)),
            pl.BlockSpec((nseq_s, SUBLANES, RW_COLS), lambda i: (0, 0, 0)),
        ],
        out_shape=[
            jax.ShapeDtypeStruct((n_p, P_COLS), F32),
            jax.ShapeDtypeStruct((n_p, D_MODEL), F32),
            jax.ShapeDtypeStruct((nrb, SUBLANES, RW_COLS), F32),
            jax.ShapeDtypeStruct((n_s, P_COLS), F32),
            jax.ShapeDtypeStruct((n_s, D_MODEL), F32),
            jax.ShapeDtypeStruct((nseq_s, SUBLANES, RW_COLS), F32),
        ],
        scratch_shapes=[pltpu.VMEM((tm, D_MODEL), BF16), pltpu.VMEM((SUBLANES, RW_COLS), F32)],
        compiler_params=_cparams(("arbitrary",)),
        name="in_proj",
    )(xp, *tables_p, shift_p, xs, *tables_s, shift_s, g, b, w_p, mu)


def _attend(q, k, v, valid, sink_ref):
    r = q.shape[0]
    outs = []
    for kv in range(ATT_KV_HEADS):
        kh = k[:, kv * HEAD_DIM:(kv + 1) * HEAD_DIM].astype(BF16)
        vh = v[:, kv * HEAD_DIM:(kv + 1) * HEAD_DIM].astype(BF16)
        heads = [kv * ATT_GROUP + g for g in range(ATT_GROUP)]
        qs = jnp.concatenate([q[:, h * HEAD_DIM:(h + 1) * HEAD_DIM] for h in heads], axis=0)
        sink = jnp.concatenate([jnp.full((r, 1), sink_ref[h], F32) for h in heads], axis=0)
        s = lax.dot_general(qs.astype(BF16), kh, (((1,), (1,)), ((), ())), preferred_element_type=F32) * ATT_SCALE
        if valid is not None:
            s = jnp.where(valid, s, -jnp.inf)
        m = jnp.maximum(jnp.max(s, -1, keepdims=True), sink)
        p = jnp.exp(s - m)
        den = jnp.sum(p, -1, keepdims=True) + jnp.exp(sink - m)
        o = jnp.dot(p.astype(BF16), vh, preferred_element_type=F32) / den
        outs.extend(o[g * r:(g + 1) * r] for g in range(ATT_GROUP))
    return jnp.concatenate(outs, axis=1)


LOG2_E = math.log2(math.e)
QB = 2 * CHUNK


def _attn_prompt_kernel(sink_ref, q_ref, kp_ref, kc_ref, vp_ref, vc_ref, o_ref):
    m = pl.program_id(1)
    nseq = q_ref.shape[0]
    span = 2 * QB
    lane = lax.broadcasted_iota(jnp.int32, (span, KV_WIDTH), 1)
    row_c = 2 + lax.broadcasted_iota(jnp.int32, (QB, span), 0) // CHUNK
    col_c = lax.broadcasted_iota(jnp.int32, (QB, span), 1) // CHUNK
    back = row_c - col_c
    valid = jnp.logical_and(jnp.logical_and(back >= 0, back <= 2), 2 * m - 2 + col_c >= 0)
    out_lane = lax.broadcasted_iota(jnp.int32, (QB, PAIR), 1)

    def dup(x, kv):
        sw = pltpu.roll(x, HEAD_DIM, 1)
        return jnp.where(lane < HEAD_DIM, x, sw) if kv == 0 else jnp.where(lane < HEAD_DIM, sw, x)

    k_st, v_st = [], []
    for b in range(nseq):
        k_all = jnp.concatenate([kp_ref[b], kc_ref[b]], axis=0)
        v_all = jnp.concatenate([vp_ref[b], vc_ref[b]], axis=0)
        k_st.append([_stack_heads(dup(k_all, kv), HEAD_DIM).astype(BF16) for kv in range(ATT_KV_HEADS)])
        v_st.append([_stack_heads(dup(v_all, kv), HEAD_DIM).astype(BF16) for kv in range(ATT_KV_HEADS)])
    pairs = [(b, j) for b in range(nseq) for j in range(ATT_HEADS // 2)]
    kv_of = lambda j: (2 * j) // ATT_GROUP
    s = [lax.dot_general((q_ref[b, :, j * PAIR:(j + 1) * PAIR] * (ATT_SCALE * LOG2_E)).astype(BF16), k_st[b][kv_of(j)],
                         (((1,), (1,)), ((), ())), preferred_element_type=F32) for b, j in pairs]
    probs, dens = [], []
    for idx, (b, j) in enumerate(pairs):
        halves, den = [], []
        for e in range(2):
            sink = sink_ref[2 * j + e] * LOG2_E
            sh = jnp.where(valid, s[idx][:, e * span:(e + 1) * span], -jnp.inf)
            mx = jnp.maximum(jnp.max(sh, -1, keepdims=True), sink)
            ph = jnp.exp2(sh - mx)
            halves.append(ph)
            den.append(jnp.sum(ph, -1, keepdims=True) + jnp.exp2(sink - mx))
        probs.append(jnp.concatenate(halves, axis=1).astype(BF16))
        dens.append(jnp.where(out_lane < HEAD_DIM, den[0], den[1]))
    o = [jnp.dot(probs[idx], v_st[b][kv_of(j)], preferred_element_type=F32) / dens[idx]
         for idx, (b, j) in enumerate(pairs)]
    n_pairs = ATT_HEADS // 2
    for b in range(nseq):
        o_ref[b] = jnp.concatenate(o[b * n_pairs:(b + 1) * n_pairs], axis=1).astype(o_ref.dtype)


def _attn_prompt(p3, sinks, nseq):
    batch, seq = p3.shape[0], p3.shape[1]
    nb = seq // QB
    kcol, vcol = P_AK // KV_WIDTH, P_AV // KV_WIDTH
    prev = lambda col: pl.BlockSpec((nseq, QB, KV_WIDTH), lambda g, m: (g, jnp.maximum(m - 1, 0), col))
    cur = lambda col: pl.BlockSpec((nseq, QB, KV_WIDTH), lambda g, m: (g, m, col))
    return pl.pallas_call(
        _attn_prompt_kernel,
        grid=(batch // nseq, nb),
        in_specs=[
            pl.BlockSpec(memory_space=pltpu.SMEM),
            pl.BlockSpec((nseq, QB, ATT_WIDTH), lambda g, m: (g, m, P_Q // ATT_WIDTH)),
            prev(kcol), cur(kcol), prev(vcol), cur(vcol),
        ],
        out_specs=pl.BlockSpec((nseq, QB, ATT_WIDTH), lambda g, m: (g, m, 0)),
        out_shape=jax.ShapeDtypeStruct((batch, seq, ATT_WIDTH), BF16),
        compiler_params=_cparams(("arbitrary", "arbitrary")),
        name="attn_prompt",
    )(sinks, p3, p3, p3, p3, p3)


def _attn_sample_kernel(sink_ref, q_ref, kn_ref, vn_ref, kc_ref, vc_ref, o_ref, kw_ref, vw_ref):
    t = q_ref.shape[0]
    k = jnp.concatenate([kc_ref[...], kn_ref[...]], axis=0)
    v = jnp.concatenate([vc_ref[...], vn_ref[...]], axis=0)
    o_ref[...] = _attend(q_ref[...], k, v, None, sink_ref).astype(o_ref.dtype)
    kw_ref[...] = k[t:]
    vw_ref[...] = v[t:]


def _attn_sample(p, sinks, cache_k, cache_v, batch, seq):
    kcol, vcol = P_AK // KV_WIDTH, P_AV // KV_WIDTH
    return pl.pallas_call(
        _attn_sample_kernel,
        grid=(batch,),
        in_specs=[
            pl.BlockSpec(memory_space=pltpu.SMEM),
            pl.BlockSpec((seq, ATT_WIDTH), lambda b: (b, P_Q // ATT_WIDTH)),
            pl.BlockSpec((seq, KV_WIDTH), lambda b: (b, kcol)),
            pl.BlockSpec((seq, KV_WIDTH), lambda b: (b, vcol)),
            pl.BlockSpec((WINDOW, KV_WIDTH), lambda b: (b, 0)),
            pl.BlockSpec((WINDOW, KV_WIDTH), lambda b: (b, 0)),
        ],
        out_specs=[
            pl.BlockSpec((seq, ATT_WIDTH), lambda b: (b, 0)),
            pl.BlockSpec((WINDOW, KV_WIDTH), lambda b: (b, 0)),
            pl.BlockSpec((WINDOW, KV_WIDTH), lambda b: (b, 0)),
        ],
        out_shape=[
            jax.ShapeDtypeStruct((batch * seq, ATT_WIDTH), BF16),
            jax.ShapeDtypeStruct((batch * WINDOW, KV_WIDTH), F32),
            jax.ShapeDtypeStruct((batch * WINDOW, KV_WIDTH), F32),
        ],
        compiler_params=_cparams(("arbitrary",)),
        name="attn_sample",
    )(sinks, p, p, p, cache_k, cache_v)


SEG = 256
EXP_NEG_HALF = math.exp(-0.5)
PAIR = 2 * HEAD_DIM
N_PAIRS = RW_HEADS // 2


def _split2(x):
    hi = x.astype(BF16)
    lo = (x - hi.astype(F32)).astype(BF16)
    return hi, lo


def _seg_sum(x, bd):
    c = x.shape[0]
    xb = x.astype(BF16)
    stack = jnp.concatenate([xb[:, s:s + SEG] for s in range(0, RW_WIDTH, SEG)], axis=0)
    r = jnp.dot(stack, bd, preferred_element_type=F32)
    return jnp.concatenate([r[i * c:(i + 1) * c] for i in range(RW_WIDTH // SEG)], axis=1)


def _dotb(a, b):
    return jnp.dot(a.astype(BF16), b.astype(BF16), preferred_element_type=F32)


def _dotb_nt(a, b):
    return lax.dot_general(a.astype(BF16), b.astype(BF16), (((1,), (1,)), ((), ())), preferred_element_type=F32)


def _dotb_tn(a, b):
    return lax.dot_general(a.astype(BF16), b.astype(BF16), (((0,), (0,)), ((), ())), preferred_element_type=F32)


def _stack_heads(x, half):
    lane = lax.broadcasted_iota(jnp.int32, x.shape, 1)
    first = (lane % (2 * half)) < half
    zero = jnp.zeros_like(x)
    return jnp.concatenate([jnp.where(first, x, zero), jnp.where(first, zero, x)], axis=0)


def _stack_heads_bf16(x, half):
    return _stack_heads(x.astype(BF16), half)


def _rwkv_pairs(ab, rb, bb, kb, v, gam_c, s_bd, n_double):
    c = ab[0].shape[0]
    pairs = range(len(ab))
    t_idx = lax.broadcasted_iota(jnp.int32, (c, 2 * c), 0)
    s_idx = lax.broadcasted_iota(jnp.int32, (c, 2 * c), 1) % c
    strict, incl = s_idx < t_idx, s_idx <= t_idx
    g = [_dotb_nt(jnp.concatenate([ab[p], rb[p]], axis=0),
                  jnp.concatenate([_stack_heads_bf16(bb[p], HEAD_DIM), _stack_heads_bf16(kb[p], HEAD_DIM)], axis=0))
         for p in pairs]
    l_ak = [jnp.where(strict, g[p][:c, 2 * c:], 0.0) for p in pairs]
    m_rbk = [jnp.concatenate([jnp.where(incl, g[p][c:, :2 * c], 0.0), jnp.where(incl, g[p][c:, 2 * c:], 0.0)], axis=1)
             for p in pairs]
    lp = [jnp.where(strict, g[p][:c, :2 * c], 0.0) for p in pairs]
    v_st = [_stack_heads_bf16(v[p], HEAD_DIM) for p in pairs]
    y0 = [jnp.concatenate([ab[p], _dotb(l_ak[p], v_st[p])], axis=1) for p in pairs]
    n_acc = list(lp)
    lk_b = [lp[p].astype(BF16) for p in pairs]
    lk = [_dotb(lk_b[p], _stack_heads(lk_b[p], c)) for p in pairs]
    for k in range(1, n_double):
        n_st = [_stack_heads_bf16(n_acc[p], c) for p in pairs]
        lk_b = [lk[p].astype(BF16) for p in pairs]
        if k + 1 < n_double:
            z = [_dotb(lk_b[p], jnp.concatenate([n_st[p], _stack_heads(lk_b[p], c)], axis=1)) for p in pairs]
            n_acc = [n_acc[p] + lk[p] + z[p][:, :2 * c] for p in pairs]
            lk = [z[p][:, 2 * c:] for p in pairs]
        else:
            n_acc = [n_acc[p] + lk[p] + _dotb(lk_b[p], n_st[p]) for p in pairs]
    y = [y0[p] + _dotb(n_acc[p], _stack_heads_bf16(y0[p], HEAD_DIM)) for p in pairs]
    ya = [y[p][:, :PAIR] for p in pairs]
    yu = [y[p][:, PAIR:] for p in pairs]
    ou = []
    for p in pairs:
        lhs = jnp.concatenate([rb[p], ya[p]], axis=0).astype(BF16)
        ou.append(lax.dot_general(jnp.concatenate([lhs, lhs], axis=1), jnp.concatenate(_split2(s_bd[p]), axis=1),
                                  (((1,), (1,)), ((), ())), preferred_element_type=F32))
    u = [ou[p][c:] + yu[p] for p in pairs]
    o = [ou[p][:c] + _dotb(m_rbk[p], jnp.concatenate([_stack_heads_bf16(u[p], HEAD_DIM), v_st[p]], axis=0))
         for p in pairs]
    row_h = lax.broadcasted_iota(jnp.int32, (PAIR, PAIR), 0) // HEAD_DIM
    col_h = lax.broadcasted_iota(jnp.int32, (PAIR, PAIR), 1) // HEAD_DIM
    same = row_h == col_h
    d_t = [jnp.where(same, _dotb_tn(jnp.concatenate([u[p], v[p]], axis=0),
                                    jnp.concatenate([bb[p], kb[p]], axis=0)), 0.0) for p in pairs]
    s_new = [(s_bd[p] + d_t[p]) * gam_c[p] for p in pairs]
    return o, s_new


def _rwkv_kernel(xr_ref, xk_ref, xv_ref, xl_ref, w0_ref, a0_ref, kk_ref, ka_ref, rk_ref,
                 lg_ref, lb_ref, w2_ref, a2_ref, g2_ref, bd_ref, st0_ref,
                 out_ref, st_ref, s_scr):
    c = pl.program_id(1)
    nc = pl.num_programs(1)
    nb, C = xr_ref.shape[0], xr_ref.shape[1]
    n_double = int(math.log2(C))

    @pl.when(c == 0)
    def _():
        s_scr[...] = jnp.zeros_like(s_scr)
        for b in range(nb):
            for h in range(RW_HEADS):
                off = (h % 2) * HEAD_DIM
                s_scr[b * N_PAIRS + h // 2, off:off + HEAD_DIM, off:off + HEAD_DIM] = st0_ref[b, h]

    rows_of = lambda load: jnp.concatenate([load(b) for b in range(nb)], axis=0)
    wa_w = DECAY_LORA + AAA_LORA
    r = rows_of(lambda b: xr_ref[b])
    kr = rows_of(lambda b: xk_ref[b])
    v = rows_of(lambda b: xv_ref[b])
    xw = rows_of(lambda b: xl_ref[b, :, :wa_w])
    gd = rows_of(lambda b: xl_ref[b, :, wa_w:wa_w + GATE_LORA])
    wd = xw[:, 0:DECAY_LORA]
    ad = xw[:, DECAY_LORA:wa_w]

    lw = (-EXP_NEG_HALF * LOG2_E) * jax.nn.sigmoid(w0_ref[...] + _dotb(jnp.tanh(wd), w2_ref[...]))
    a = jax.nn.sigmoid(a0_ref[...] + _dotb(ad, a2_ref[...]))
    g = _dotb(jax.nn.sigmoid(gd), g2_ref[...])
    bd = bd_ref[...]
    kk = kr * kk_ref[...]
    kk = kk * lax.rsqrt(jnp.maximum(_seg_sum(kk * kk, bd), 1e-24))
    k2 = kr * (1.0 + (a - 1.0) * ka_ref[...])

    rows = nb * C
    ti = lax.broadcasted_iota(jnp.int32, (rows, rows), 0)
    si = lax.broadcasted_iota(jnp.int32, (rows, rows), 1)
    tri = jnp.logical_and(si <= ti, si // C == ti // C).astype(BF16)
    cs2 = jnp.dot(tri, jnp.concatenate(_split2(lw), axis=1), preferred_element_type=F32)
    cs = cs2[:, :RW_WIDTH] + cs2[:, RW_WIDTH:]
    gam = jnp.exp2(cs)
    ginv = jnp.exp2(-cs)
    ab = -kk * jnp.exp2(cs - lw)
    rb = r * gam
    bb = kk * a * ginv
    kb = k2 * ginv

    def per_pair(x, r0, r1):
        return [x[b * C + r0:b * C + r1, p * PAIR:(p + 1) * PAIR] for b in range(nb) for p in range(N_PAIRS)]

    chains = nb * N_PAIRS
    outs, s_new = _rwkv_pairs(per_pair(ab, 0, C), per_pair(rb, 0, C), per_pair(bb, 0, C), per_pair(kb, 0, C),
                              per_pair(v, 0, C), per_pair(gam, C - 1, C), [s_scr[q] for q in range(chains)], n_double)
    for q in range(chains):
        s_scr[q] = s_new[q]
    o = jnp.concatenate([jnp.concatenate(outs[b * N_PAIRS:(b + 1) * N_PAIRS], axis=1) for b in range(nb)], axis=0)

    mo = _seg_sum(o, bd) * (1.0 / HEAD_DIM)
    d = o - mo
    vo = _seg_sum(d * d, bd) * (1.0 / HEAD_DIM)
    on = d * lax.rsqrt(vo + LNX_EPS) * lg_ref[...] + lb_ref[...]
    bonus = _seg_sum(r * k2 * rk_ref[...], bd) * v
    res = ((on + bonus) * g).astype(out_ref.dtype)
    for b in range(nb):
        out_ref[b] = res[b * C:(b + 1) * C, :]

    @pl.when(c == nc - 1)
    def _():
        for b in range(nb):
            for h in range(RW_HEADS):
                off = (h % 2) * HEAD_DIM
                st_ref[b, h] = s_scr[b * N_PAIRS + h // 2, off:off + HEAD_DIM, off:off + HEAD_DIM]


def _rwkv(p3, state0, prm, chunk, nb):
    batch, seq = p3.shape[0], p3.shape[1]
    nc = seq // chunk
    row = lambda w: pl.BlockSpec((1, w), lambda g, c: (0, 0))
    xspec = lambda w, col: pl.BlockSpec((nb, chunk, w), lambda g, c: (g, c, col))
    full = lambda a: pl.BlockSpec(a.shape, lambda g, c: (0,) * a.ndim)
    stspec = pl.BlockSpec((nb, RW_HEADS, HEAD_DIM, HEAD_DIM), lambda g, c: (g, 0, 0, 0))
    return pl.pallas_call(
        _rwkv_kernel,
        grid=(batch // nb, nc),
        in_specs=[
            xspec(RW_WIDTH, P_R // RW_WIDTH), xspec(RW_WIDTH, P_K // RW_WIDTH), xspec(RW_WIDTH, P_V // RW_WIDTH),
            xspec(LORA_W, P_LORA // LORA_W),
            row(RW_WIDTH), row(RW_WIDTH), row(RW_WIDTH), row(RW_WIDTH), row(RW_WIDTH),
            row(RW_WIDTH), row(RW_WIDTH),
            full(prm["w2"]), full(prm["a2"]), full(prm["g2"]), full(prm["bd"]),
            stspec,
        ],
        out_specs=[
            pl.BlockSpec((nb, chunk, RW_WIDTH), lambda g, c: (g, c, 0)),
            stspec,
        ],
        out_shape=[
            jax.ShapeDtypeStruct((batch, seq, RW_WIDTH), BF16),
            jax.ShapeDtypeStruct((batch, RW_HEADS, HEAD_DIM, HEAD_DIM), F32),
        ],
        scratch_shapes=[pltpu.VMEM((nb * N_PAIRS, PAIR, PAIR), F32)],
        compiler_params=_cparams(("arbitrary", "arbitrary")),
        name="rwkv",
    )(p3, p3, p3, p3, prm["w0"], prm["a0"], prm["k_k"], prm["k_a"], prm["r_k"],
      prm["lnx_g"], prm["lnx_b"], prm["w2"], prm["a2"], prm["g2"], prm["bd"], state0)


D_TN = 512


def _out_proj_body(att_ref, rw_ref, h_ref, g_ref, b_ref, h1_ref, h1b_ref, w_scr):
    for c0 in range(0, D_MODEL, D_TN):
        m = jnp.dot(att_ref[...], w_scr[:ATT_WIDTH, c0:c0 + D_TN], preferred_element_type=F32)
        m = m + jnp.dot(rw_ref[...], w_scr[ATT_WIDTH:, c0:c0 + D_TN], preferred_element_type=F32)
        h1_ref[:, c0:c0 + D_TN] = ALPHA * h_ref[:, c0:c0 + D_TN] + m
    y = _layer_norm(h1_ref[...], g_ref[...], b_ref[...])
    h1_ref[...] = y
    h1b_ref[...] = y.astype(BF16)


def _out_proj(att_p, rw_p, h_p, att_s, rw_s, h_s, w_out, g, b, tm):
    n_p, n_s = att_p.shape[0], att_s.shape[0]
    nrb = n_p // tm
    last = lambda i: jnp.minimum(i, nrb - 1)

    def kern(ap_ref, rp_ref, hp_ref, as_ref, rs_ref, hs_ref, wa_ref, wb_ref, g_ref, b_ref,
             h1p_ref, h1bp_ref, h1s_ref, h1bs_ref, w_scr):
        i = pl.program_id(0)

        @pl.when(i == 0)
        def _():
            w_scr[:ATT_WIDTH, :] = wa_ref[...].astype(BF16)
            w_scr[ATT_WIDTH:, :] = wb_ref[...].astype(BF16)

        pl.when(i < nrb)(lambda: _out_proj_body(ap_ref, rp_ref, hp_ref, g_ref, b_ref, h1p_ref, h1bp_ref, w_scr))
        pl.when(i == nrb)(lambda: _out_proj_body(as_ref, rs_ref, hs_ref, g_ref, b_ref, h1s_ref, h1bs_ref, w_scr))

    rows_p = lambda w: pl.BlockSpec((tm, w), lambda i: (last(i), 0))
    rows_s = lambda w: pl.BlockSpec((n_s, w), lambda i: (0, 0))
    return pl.pallas_call(
        kern,
        grid=(nrb + 1,),
        in_specs=[
            rows_p(ATT_WIDTH), rows_p(RW_WIDTH), rows_p(D_MODEL),
            rows_s(ATT_WIDTH), rows_s(RW_WIDTH), rows_s(D_MODEL),
            pl.BlockSpec((ATT_WIDTH, D_MODEL), lambda i: (0, 0), pipeline_mode=pl.Buffered(1)),
            pl.BlockSpec((RW_WIDTH, D_MODEL), lambda i: (1, 0), pipeline_mode=pl.Buffered(1)),
            pl.BlockSpec((1, D_MODEL), lambda i: (0, 0)),
            pl.BlockSpec((1, D_MODEL), lambda i: (0, 0)),
        ],
        out_specs=[rows_p(D_MODEL), rows_p(D_MODEL), rows_s(D_MODEL), rows_s(D_MODEL)],
        out_shape=[
            jax.ShapeDtypeStruct((n_p, D_MODEL), F32), jax.ShapeDtypeStruct((n_p, D_MODEL), BF16),
            jax.ShapeDtypeStruct((n_s, D_MODEL), F32), jax.ShapeDtypeStruct((n_s, D_MODEL), BF16),
        ],
        scratch_shapes=[pltpu.VMEM((ATT_WIDTH + RW_WIDTH, D_MODEL), BF16)],
        compiler_params=_cparams(("arbitrary",)),
        name="out_proj",
    )(att_p, rw_p, h_p, att_s, rw_s, h_s, w_out, w_out, g, b)


E_TN = 512
E_NCB = D_FF // E_TN
E_SUB = 1024
SQRT_HALF = math.sqrt(0.5)


def _ffn_up_body(bps, nseq, i, h_ref, w_scr, cwg_ref, cwv_ref, cbg_ref, cbv_ref, sg_ref, sv_ref,
                 act_ref, og_ref, ov_ref, cg_scr, cv_scr):
    tm = h_ref.shape[0]
    sub = min(tm, E_SUB)
    n_sub = tm // sub
    t_seq = tm // nseq
    row8 = lax.broadcasted_iota(jnp.int32, (SUBLANES, E_TN), 0)

    def products(r0):
        u = jnp.dot(h_ref[r0:r0 + sub, :], w_scr[...], preferred_element_type=F32)
        return u[:, :E_TN], u[:, E_TN:]

    def conv(u, prevs, cw_ref, cb_ref):
        sh1, sh2 = pltpu.roll(u, 1, 0), pltpu.roll(u, 2, 0)
        for r, prev in prevs:
            p2, p1 = prev[0:1, :], prev[1:2, :]
            fix1 = jnp.where(row8 == 0, p1, sh1[r:r + SUBLANES, :])
            fix2 = jnp.where(row8 == 0, p2, jnp.where(row8 == 1, p1, sh2[r:r + SUBLANES, :]))
            head = lambda x: [x[:r, :]] if r > 0 else []
            tail = lambda x: [x[r + SUBLANES:, :]] if r + SUBLANES < x.shape[0] else []
            sh1 = jnp.concatenate(head(sh1) + [fix1] + tail(sh1), axis=0)
            sh2 = jnp.concatenate(head(sh2) + [fix2] + tail(sh2), axis=0)
        cw = cw_ref[...]
        c = cb_ref[...] + sh2 * cw[0:1, :]
        c = c + sh1 * cw[1:2, :]
        return c + u * cw[2:3, :]

    def epilogue(r0, ug, uv, prev_g, prev_v):
        gate = conv(ug, prev_g, cwg_ref, cbg_ref)
        val = conv(uv, prev_v, cwv_ref, cbv_ref)
        gelu = 0.5 * gate * (1.0 + lax.erf(gate * SQRT_HALF))
        act_ref[r0:r0 + sub, :] = (gelu * val).astype(act_ref.dtype)

    if nseq > 1:
        ug, uv = products(0)
        epilogue(0, ug, uv, [(s * t_seq, sg_ref[s]) for s in range(nseq)],
                 [(s * t_seq, sv_ref[s]) for s in range(nseq)])
        for s in range(nseq):
            og_ref[s] = ug[(s + 1) * t_seq - SUBLANES:(s + 1) * t_seq, :]
            ov_ref[s] = uv[(s + 1) * t_seq - SUBLANES:(s + 1) * t_seq, :]
    else:
        first = (i % bps) == 0
        prev_g = jnp.where(first, sg_ref[0], cg_scr[SUBLANES - 2:SUBLANES, :])
        prev_v = jnp.where(first, sv_ref[0], cv_scr[SUBLANES - 2:SUBLANES, :])
        nxt = products(0)
        for k in range(n_sub):
            ug, uv = nxt
            if k + 1 < n_sub:
                nxt = products((k + 1) * sub)
            epilogue(k * sub, ug, uv, [(0, prev_g)], [(0, prev_v)])
            prev_g, prev_v = ug[sub - 2:sub, :], uv[sub - 2:sub, :]
        cg_scr[...] = ug[sub - SUBLANES:sub, :]
        cv_scr[...] = uv[sub - SUBLANES:sub, :]
        og_ref[0] = ug[sub - SUBLANES:sub, :]
        ov_ref[0] = uv[sub - SUBLANES:sub, :]


def _ffn_up(h_p, conv_prev_p, seq_p, tm_p, h_s, conv_prev_s, seq_s, w_up, conv_w, conv_b, w_down):
    n_p, n_s = h_p.shape[0], h_s.shape[0]
    batch_p, batch_s = n_p // seq_p, n_s // seq_s
    bps = seq_p // tm_p
    nrb = n_p // tm_p
    slab = D_FF // (E_NCB * nrb)
    last = lambda i: jnp.maximum(i - 1, 0)

    def kern(hp_ref, sgp_ref, svp_ref, hs_ref, sgs_ref, svs_ref, wg_ref, wv_ref, cwg_ref, cwv_ref, cbg_ref, cbv_ref,
             wd_ref, actp_ref, ogp_ref, ovp_ref, acts_ref, ogs_ref, ovs_ref, wdb_ref, w_scr, cg_scr, cv_scr):
        i = pl.program_id(1)

        @pl.when(i == 0)
        def _():
            w_scr[:, :E_TN] = wg_ref[...].astype(BF16)
            w_scr[:, E_TN:] = wv_ref[...].astype(BF16)
            _ffn_up_body(1, batch_s, i, hs_ref, w_scr, cwg_ref, cwv_ref, cbg_ref, cbv_ref, sgs_ref, svs_ref,
                         acts_ref, ogs_ref, ovs_ref, cg_scr, cv_scr)

        @pl.when(i > 0)
        def _():
            wdb_ref[...] = wd_ref[...].astype(BF16)
            _ffn_up_body(bps, 1, i - 1, hp_ref, w_scr, cwg_ref, cwv_ref, cbg_ref, cbv_ref, sgp_ref, svp_ref,
                         actp_ref, ogp_ref, ovp_ref, cg_scr, cv_scr)

    col = lambda shape, off: pl.BlockSpec(shape, lambda j, i: (0,) * (len(shape) - 1) + (j + off,))
    st_p = lambda off: pl.BlockSpec((1, CONV_W - 1, E_TN), lambda j, i: (last(i) // bps, 0, j + off))
    st_s = lambda off: pl.BlockSpec((batch_s, CONV_W - 1, E_TN), lambda j, i: (0, 0, j + off))
    tail_p = pl.BlockSpec((1, SUBLANES, E_TN), lambda j, i: (last(i) // bps, 0, j))
    tail_s = pl.BlockSpec((batch_s, SUBLANES, E_TN), lambda j, i: (0, 0, j))
    wd_spec = pl.BlockSpec((slab, D_MODEL), lambda j, i: (j * nrb + last(i), 0))
    return pl.pallas_call(
        kern,
        grid=(E_NCB, nrb + 1),
        in_specs=[
            pl.BlockSpec((tm_p, D_MODEL), lambda j, i: (last(i), 0)), st_p(0), st_p(E_NCB),
            pl.BlockSpec((n_s, D_MODEL), lambda j, i: (0, 0)), st_s(0), st_s(E_NCB),
            col((D_MODEL, E_TN), 0), col((D_MODEL, E_TN), E_NCB),
            col((CONV_W, E_TN), 0), col((CONV_W, E_TN), E_NCB),
            col((1, E_TN), 0), col((1, E_TN), E_NCB),
            wd_spec,
        ],
        out_specs=[
            pl.BlockSpec((tm_p, E_TN), lambda j, i: (last(i), j)), tail_p, tail_p,
            pl.BlockSpec((n_s, E_TN), lambda j, i: (0, j)), tail_s, tail_s,
            wd_spec,
        ],
        out_shape=[
            jax.ShapeDtypeStruct((n_p, D_FF), BF16),
            jax.ShapeDtypeStruct((batch_p, SUBLANES, D_FF), F32), jax.ShapeDtypeStruct((batch_p, SUBLANES, D_FF), F32),
            jax.ShapeDtypeStruct((n_s, D_FF), BF16),
            jax.ShapeDtypeStruct((batch_s, SUBLANES, D_FF), F32), jax.ShapeDtypeStruct((batch_s, SUBLANES, D_FF), F32),
            jax.ShapeDtypeStruct((D_FF, D_MODEL), BF16),
        ],
        scratch_shapes=[pltpu.VMEM((D_MODEL, 2 * E_TN), BF16),
                        pltpu.VMEM((SUBLANES, E_TN), F32), pltpu.VMEM((SUBLANES, E_TN), F32)],
        compiler_params=_cparams(("arbitrary", "arbitrary")),
        name="ffn_up",
    )(h_p, conv_prev_p, conv_prev_p, h_s, conv_prev_s, conv_prev_s, w_up, w_up, conv_w, conv_w, conv_b, conv_b, w_down)


F_TN = 512


def _ffn_down_kernel(act_ref, w_ref, h1_ref, g_ref, b_ref, y_ref):
    for c0 in range(0, D_MODEL, F_TN):
        f = jnp.dot(act_ref[...], w_ref[:, c0:c0 + F_TN], preferred_element_type=F32)
        y_ref[:, c0:c0 + F_TN] = ALPHA * h1_ref[:, c0:c0 + F_TN] + f
    y_ref[...] = _layer_norm(y_ref[...], g_ref[...], b_ref[...])


def _ffn_down(act_p, h1_p, act_s, h1_s, w_down, g, b, tm):
    n_p, n_s = act_p.shape[0], act_s.shape[0]
    nrb = n_p // tm
    last = lambda i: jnp.minimum(i, nrb - 1)

    def kern(ap_ref, hp_ref, as_ref, hs_ref, w_ref, g_ref, b_ref, yp_ref, ys_ref):
        i = pl.program_id(0)
        pl.when(i < nrb)(lambda: _ffn_down_kernel(ap_ref, w_ref, hp_ref, g_ref, b_ref, yp_ref))
        pl.when(i == nrb)(lambda: _ffn_down_kernel(as_ref, w_ref, hs_ref, g_ref, b_ref, ys_ref))

    rows_p = lambda w: pl.BlockSpec((tm, w), lambda i: (last(i), 0))
    rows_s = lambda w: pl.BlockSpec((n_s, w), lambda i: (0, 0))
    return pl.pallas_call(
        kern,
        grid=(nrb + 1,),
        in_specs=[
            rows_p(D_FF), rows_p(D_MODEL), rows_s(D_FF), rows_s(D_MODEL),
            pl.BlockSpec((D_FF, D_MODEL), lambda i: (0, 0), pipeline_mode=pl.Buffered(1)),
            pl.BlockSpec((1, D_MODEL), lambda i: (0, 0)),
            pl.BlockSpec((1, D_MODEL), lambda i: (0, 0)),
        ],
        out_specs=[rows_p(D_MODEL), rows_s(D_MODEL)],
        out_shape=[jax.ShapeDtypeStruct((n_p, D_MODEL), F32), jax.ShapeDtypeStruct((n_s, D_MODEL), F32)],
        compiler_params=_cparams(("arbitrary",)),
        name="ffn_down",
    )(act_p, h1_p, act_s, h1_s, w_down, g, b)


def _rope_tables(pos, reps):
    half = HEAD_DIM // 2
    inv = ROPE_THETA ** (-jnp.arange(half, dtype=F32) / half)
    ang = pos.astype(F32)[:, None] * inv[None, :]
    cos, sin = jnp.cos(ang), jnp.sin(ang)
    cos128 = jnp.concatenate([cos, cos, cos, cos], axis=1)
    sin128 = jnp.concatenate([-sin, sin, -sin, sin], axis=1)
    return jnp.tile(cos128, (reps, 1)), jnp.tile(sin128, (reps, 1))


TM = 256
D_TM = 256
E_TM = 1024
F_TM = 256
RWKV_NB = 4
ATTN_NB = 2


def kernel(x_prompt, x_sample, cache_k, cache_v, state_wkv, state_shift, state_ffn_conv, ln_in_g, ln_in_b, w_in, attn_sinks, rw_mu, rw_w0, rw_w2, rw_a0, rw_a2, rw_g2, rw_k_k, rw_k_a, rw_r_k, rw_lnx_g, rw_lnx_b, w_out, ln1_g, ln1_b, ffn_w_up, ffn_conv_w, ffn_conv_b, ffn_w_down, ln2_g, ln2_b):
    l = 0
    seg = lax.broadcasted_iota(jnp.int32, (SEG, SEG), 0) // HEAD_DIM
    bd = (seg == seg.T).astype(BF16)
    row = lambda a: a.reshape(1, -1)
    rw_prm = {"w0": row(rw_w0[l]), "a0": row(rw_a0[l]), "k_k": row(rw_k_k[l]),
              "k_a": row(rw_k_a[l]), "r_k": row(rw_r_k[l]), "lnx_g": row(rw_lnx_g[l]), "lnx_b": row(rw_lnx_b[l]),
              "w2": rw_w2[l].astype(BF16), "a2": rw_a2[l].astype(BF16), "g2": rw_g2[l].astype(BF16), "bd": bd}
    sinks = attn_sinks[l]
    bp, tp = x_prompt.shape[0], x_prompt.shape[1]
    bs, ts = x_sample.shape[0], x_sample.shape[1]
    n_p, n_s = bp * tp, bs * ts
    zeros = lambda *s: jnp.zeros(s, F32)

    p_p, h_p, tails_p, p_s, h_s, tails_s = _in_proj(
        x_prompt.reshape(n_p, D_MODEL), x_sample.reshape(n_s, D_MODEL), row(ln_in_g), row(ln_in_b),
        jnp.pad(w_in[l].astype(BF16), ((0, 0), (0, IN_COLS_PAD - IN_COLS))), row(rw_mu[l]), zeros(bp, 1, RW_COLS),
        state_shift[l],
        _rope_tables(jnp.arange(tp, dtype=jnp.int32), 1),
        _rope_tables(PAST_LEN + jnp.arange(ts, dtype=jnp.int32), bs), TM, tp, ts)
    p3_p, p3_s = p_p.reshape(bp, tp, P_COLS), p_s.reshape(bs, ts, P_COLS)

    att_p = _attn_prompt(p3_p, sinks, ATTN_NB).reshape(n_p, ATT_WIDTH)
    att_s, s_k, s_v = _attn_sample(p_s, sinks, cache_k[l].reshape(bs * WINDOW, KV_WIDTH),
                                   cache_v[l].reshape(bs * WINDOW, KV_WIDTH), bs, ts)
    rw_p, p_wkv = _rwkv(p3_p, zeros(bp, RW_HEADS, HEAD_DIM, HEAD_DIM), rw_prm, CHUNK, RWKV_NB)
    rw_s, s_wkv = _rwkv(p3_s, state_wkv[l], rw_prm, ts, RWKV_NB)

    h1_p, h1b_p, h1_s, h1b_s = _out_proj(att_p, rw_p.reshape(n_p, RW_WIDTH), h_p, att_s, rw_s.reshape(n_s, RW_WIDTH), h_s,
                                         w_out[l], row(ln1_g[l]), row(ln1_b[l]), D_TM)
    act_p, cg_p, cv_p, act_s, cg_s, cv_s, w_down_b = _ffn_up(
        h1b_p, zeros(bp, CONV_W - 1, 2 * D_FF), tp, E_TM, h1b_s, state_ffn_conv[l], ts,
        ffn_w_up[l], ffn_conv_w[l], row(ffn_conv_b[l]), ffn_w_down[l])
    y_p, y_s = _ffn_down(act_p, h1_p, act_s, h1_s, w_down_b, row(ln2_g[l]), row(ln2_b[l]), F_TM)

    kv_shape = lambda b: (1, b, WINDOW, ATT_KV_HEADS, HEAD_DIM)
    p_k = p3_p[:, tp - WINDOW:, P_AK:P_AK + KV_WIDTH].reshape(kv_shape(bp))
    p_v = p3_p[:, tp - WINDOW:, P_AV:P_AV + KV_WIDTH].reshape(kv_shape(bp))
    shift_of = lambda tails, b: tails.reshape(b, -1, RW_COLS)[:, -1:, :][None]
    conv_of = lambda cg, cv: jnp.concatenate([cg[:, SUBLANES - 2:, :], cv[:, SUBLANES - 2:, :]], axis=-1)[None]
    return (y_p.reshape(bp, tp, D_MODEL), y_s.reshape(bs, ts, D_MODEL), p_k, p_v, p_wkv[None],
            shift_of(tails_p, bp), conv_of(cg_p, cv_p),
            s_k.reshape(kv_shape(bs)), s_v.reshape(kv_shape(bs)), s_wkv[None],
            shift_of(tails_s, bs), conv_of(cg_s, cv_s))
```

```python
import math

import jax
import jax.numpy as jnp
from jax import lax
from jax.experimental import pallas as pl
from jax.experimental.pallas import tpu as pltpu

D_MODEL = 2048
HEAD_DIM = 64
ATT_HEADS = 16
ATT_KV_HEADS = 2
ATT_GROUP = ATT_HEADS // ATT_KV_HEADS
ATT_WIDTH = ATT_HEADS * HEAD_DIM
KV_WIDTH = ATT_KV_HEADS * HEAD_DIM
CHUNK = 64
WINDOW = 128
ROPE_THETA = 10000.0
ATT_SCALE = HEAD_DIM ** -0.5
RW_HEADS = 16
RW_WIDTH = RW_HEADS * HEAD_DIM
DECAY_LORA = 64
AAA_LORA = 64
GATE_LORA = 160
LNX_EPS = 64e-5
ATT_COLS = ATT_WIDTH + 2 * KV_WIDTH
RW_COLS = 3 * RW_WIDTH + DECAY_LORA + AAA_LORA + GATE_LORA
D_FF = 5632
CONV_W = 3
LN_EPS = 1e-5
DEPTH = 1
ALPHA = (2 * DEPTH) ** 0.25
PAST_LEN = 1024

LORA_W = 512
P_R, P_K, P_V, P_Q = 0, 1024, 2048, 3072
P_LORA = 4096
P_AK = P_LORA + LORA_W
P_AV = P_AK + KV_WIDTH
P_COLS = P_AV + KV_WIDTH
LANES = 128
SUBLANES = 8
VMEM_LIMIT = 56 * 1024 * 1024

F32 = jnp.float32
BF16 = jnp.bfloat16


def _cparams(sem):
    return pltpu.CompilerParams(dimension_semantics=sem, vmem_limit_bytes=VMEM_LIMIT)


def _layer_norm(y, g, b):
    mu = jnp.mean(y, -1, keepdims=True)
    d = y - mu
    var = jnp.mean(d * d, -1, keepdims=True)
    return d * lax.rsqrt(var + LN_EPS) * g + b


A_TN = 512


def _rope128(x, cos, sin_signed):
    lane = lax.broadcasted_iota(jnp.int32, x.shape, 1)
    first_half = (lane % HEAD_DIM) < (HEAD_DIM // 2)
    rot = jnp.where(first_half, pltpu.roll(x, LANES - HEAD_DIM // 2, 1), pltpu.roll(x, HEAD_DIM // 2, 1))
    return x * cos + rot * sin_signed


def _is_rope_col(col):
    return (P_Q <= col < P_Q + ATT_WIDTH) or (P_AK <= col < P_AK + KV_WIDTH)


O_R = ATT_COLS
O_WD = O_R + RW_WIDTH
O_K = O_WD + DECAY_LORA
O_V = O_K + RW_WIDTH
O_AD = O_V + RW_WIDTH
O_GD = O_AD + AAA_LORA
IN_COLS = O_GD + GATE_LORA


def _in_proj_kernel(first, x_ref, g_ref, b_ref, w_ref, cos_ref, sin_ref, mu_ref, st_ref, p_ref, h_ref, tail_ref,
                    hb_ref, carry_scr):
    h = _layer_norm(x_ref[...], g_ref[...], b_ref[...])
    h_ref[...] = h
    hb_ref[...] = h.astype(BF16)
    tm = x_ref.shape[0]
    nseq = tail_ref.shape[0]
    t_seq = tm // nseq
    row8 = lax.broadcasted_iota(jnp.int32, (SUBLANES, 1), 0)
    if nseq == 1:
        @pl.when(first)
        def _():
            carry_scr[...] = jnp.zeros_like(carry_scr)

    def product(c0, width):
        acc = jnp.dot(hb_ref[...], w_ref[:, c0:c0 + width], preferred_element_type=F32)
        if c0 < O_R:
            return acc
        cols = slice(c0 - O_R, c0 - O_R + width)
        prev = pltpu.roll(acc, 1, 0)
        for s in range(nseq):
            r = s * t_seq
            before = st_ref[s][:, cols]
            if nseq == 1:
                before = jnp.where(first, before, carry_scr[SUBLANES - 1:SUBLANES, cols])
            fixed = jnp.where(row8 == 0, before, prev[r:r + SUBLANES, :])
            head = [prev[:r, :]] if r > 0 else []
            prev = jnp.concatenate(head + [fixed, prev[r + SUBLANES:, :]], axis=0)
            tail_ref[s, :, cols] = acc[r + t_seq - SUBLANES:r + t_seq, :]
        if nseq == 1:
            carry_scr[:, cols] = acc[tm - SUBLANES:tm, :]
        return acc + (prev - acc) * mu_ref[:, cols]

    for src, width, dst in ((0, ATT_WIDTH, P_Q), (ATT_WIDTH, 2 * KV_WIDTH, P_AK), (O_R, RW_WIDTH, P_R)):
        for c0 in range(0, width, A_TN):
            acc = product(src + c0, min(A_TN, width - c0))
            for l0 in range(0, acc.shape[1], LANES):
                col = dst + c0 + l0
                part = acc[:, l0:l0 + LANES]
                if _is_rope_col(col):
                    part = _rope128(part, cos_ref[...], sin_ref[...])
                p_ref[:, col:col + LANES] = part

    low = lax.broadcasted_iota(jnp.int32, (x_ref.shape[0], LANES), 1) < HEAD_DIM
    span = O_GD - O_WD
    rolled, first_piece = [], None
    for c0 in range(0, span, A_TN):
        acc = product(O_WD + c0, min(A_TN, span - c0))
        for l0 in range(0, acc.shape[1], LANES):
            piece = acc[:, l0:l0 + LANES]
            j = (c0 + l0) // LANES
            if j == 0:
                first_piece = piece
            rolled.append(pltpu.roll(piece, HEAD_DIM, 1))
            if j >= 1:
                m = j - 1
                dst = P_K + m * LANES if m < RW_WIDTH // LANES else P_V + (m - RW_WIDTH // LANES) * LANES
                p_ref[:, dst:dst + LANES] = jnp.where(low, rolled[m], rolled[m + 1])
            if j == span // LANES - 1:
                p_ref[:, P_LORA:P_LORA + LANES] = jnp.where(low, first_piece, piece)
    gd0 = P_LORA + DECAY_LORA + AAA_LORA
    p_ref[:, gd0:gd0 + GATE_LORA] = product(O_GD, GATE_LORA)
    p_ref[:, gd0 + GATE_LORA:P_LORA + LORA_W] = jnp.zeros((x_ref.shape[0], LORA_W - DECAY_LORA - AAA_LORA - GATE_LORA), F32)


def _in_proj(xp, xs, g, b, w_p, mu, shift_p, shift_s, tables_p, tables_s, tm, seq_p, seq_s):
    n_p, n_s = xp.shape[0], xs.shape[0]
    assert n_s == tm
    nrb = n_p // tm
    nseq_s = tm // seq_s
    tb = seq_p // tm
    last = lambda i: jnp.minimum(i, nrb - 1)

    def kern(xp_ref, cp_ref, sp_ref, stp_ref, xs_ref, cs_ref, ss_ref, sts_ref, g_ref, b_ref, w_ref, mu_ref,
             pp_ref, hp_ref, tp_ref, ps_ref, hs_ref, ts_ref, hb_ref, carry_scr):
        i = pl.program_id(0)
        pl.when(i < nrb)(lambda: _in_proj_kernel(i % tb == 0, xp_ref, g_ref, b_ref, w_ref, cp_ref, sp_ref, mu_ref, stp_ref,
                                                 pp_ref, hp_ref, tp_ref, hb_ref, carry_scr))
        pl.when(i == nrb)(lambda: _in_proj_kernel(None, xs_ref, g_ref, b_ref, w_ref, cs_ref, ss_ref, mu_ref, sts_ref,
                                                  ps_ref, hs_ref, ts_ref, hb_ref, carry_scr))

    const2 = lambda shape: pl.BlockSpec(shape, lambda i: (0, 0))
    return pl.pallas_call(
        kern,
        grid=(nrb + 1,),
        in_specs=[
            pl.BlockSpec((tm, D_MODEL), lambda i: (last(i), 0)),
            pl.BlockSpec((tm, LANES), lambda i: (last(i) % tb, 0)),
            pl.BlockSpec((tm, LANES), lambda i: (last(i) % tb, 0)),
            pl.BlockSpec((1, 1, RW_COLS), lambda i: (last(i) // tb, 0, 0)),
            const2((tm, D_MODEL)), const2((tm, LANES)), const2((tm, LANES)),
            pl.BlockSpec((nseq_s, 1, RW_COLS), lambda i: (0, 0, 0)),
            const2((1, D_MODEL)), const2((1, D_MODEL)),
            pl.BlockSpec((D_MODEL, IN_COLS), lambda i: (0, 0), pipeline_mode=pl.Buffered(1)),
            const2((1, RW_COLS)),
        ],
        out_specs=[
            pl.BlockSpec((tm, P_COLS), lambda i: (last(i), 0)),
            pl.BlockSpec((tm, D_MODEL), lambda i: (last(i), 0)),
            pl.BlockSpec((1, SUBLANES, RW_COLS), lambda i: (last(i), 0, 0)),
            const2((tm, P_COLS)), const2((tm, D_MODEL)),
            pl.BlockSpec((nseq_s, SUBLANES, RW_COLS), lambda i: (0, 0, 0)),
        ],
        out_shape=[
            jax.ShapeDtypeStruct((n_p, P_COLS), F32),
            jax.ShapeDtypeStruct((n_p, D_MODEL), F32),
            jax.ShapeDtypeStruct((nrb, SUBLANES, RW_COLS), F32),
            jax.ShapeDtypeStruct((n_s, P_COLS), F32),
            jax.ShapeDtypeStruct((n_s, D_MODEL), F32),
            jax.ShapeDtypeStruct((nseq_s, SUBLANES, RW_COLS), F32),
        ],
        scratch_shapes=[pltpu.VMEM((tm, D_MODEL), BF16), pltpu.VMEM((SUBLANES, RW_COLS), F32)],
        compiler_params=_cparams(("arbitrary",)),
        name="in_proj",
    )(xp, *tables_p, shift_p, xs, *tables_s, shift_s, g, b, w_p, mu)


def _attend(q, k, v, valid, sink_ref):
    r = q.shape[0]
    outs = []
    for kv in range(ATT_KV_HEADS):
        kh = k[:, kv * HEAD_DIM:(kv + 1) * HEAD_DIM].astype(BF16)
        vh = v[:, kv * HEAD_DIM:(kv + 1) * HEAD_DIM].astype(BF16)
        heads = [kv * ATT_GROUP + g for g in range(ATT_GROUP)]
        qs = jnp.concatenate([q[:, h * HEAD_DIM:(h + 1) * HEAD_DIM] for h in heads], axis=0)
        sink = jnp.concatenate([jnp.full((r, 1), sink_ref[h], F32) for h in heads], axis=0)
        s = lax.dot_general(qs.astype(BF16), kh, (((1,), (1,)), ((), ())), preferred_element_type=F32) * ATT_SCALE
        if valid is not None:
            s = jnp.where(valid, s, -jnp.inf)
        m = jnp.maximum(jnp.max(s, -1, keepdims=True), sink)
        p = jnp.exp(s - m)
        den = jnp.sum(p, -1, keepdims=True) + jnp.exp(sink - m)
        o = jnp.dot(p.astype(BF16), vh, preferred_element_type=F32) / den
        outs.extend(o[g * r:(g + 1) * r] for g in range(ATT_GROUP))
    return jnp.concatenate(outs, axis=1)


LOG2_E = math.log2(math.e)
QB = 2 * CHUNK


def _attn_prompt_kernel(sink_ref, q_ref, kp_ref, kc_ref, vp_ref, vc_ref, o_ref):
    m = pl.program_id(1)
    nseq = q_ref.shape[0]
    span = 2 * QB
    lane = lax.broadcasted_iota(jnp.int32, (span, KV_WIDTH), 1)
    row_c = 2 + lax.broadcasted_iota(jnp.int32, (QB, span), 0) // CHUNK
    col_c = lax.broadcasted_iota(jnp.int32, (QB, span), 1) // CHUNK
    back = row_c - col_c
    valid = jnp.logical_and(jnp.logical_and(back >= 0, back <= 2), 2 * m - 2 + col_c >= 0)
    out_lane = lax.broadcasted_iota(jnp.int32, (QB, PAIR), 1)

    def dup(x, kv):
        sw = pltpu.roll(x, HEAD_DIM, 1)
        return jnp.where(lane < HEAD_DIM, x, sw) if kv == 0 else jnp.where(lane < HEAD_DIM, sw, x)

    k_st, v_st = [], []
    for b in range(nseq):
        k_all = jnp.concatenate([kp_ref[b], kc_ref[b]], axis=0)
        v_all = jnp.concatenate([vp_ref[b], vc_ref[b]], axis=0)
        k_st.append([_stack_heads(dup(k_all, kv), HEAD_DIM).astype(BF16) for kv in range(ATT_KV_HEADS)])
        v_st.append([_stack_heads(dup(v_all, kv), HEAD_DIM).astype(BF16) for kv in range(ATT_KV_HEADS)])
    pairs = [(b, j) for b in range(nseq) for j in range(ATT_HEADS // 2)]
    kv_of = lambda j: (2 * j) // ATT_GROUP
    s = [lax.dot_general((q_ref[b, :, j * PAIR:(j + 1) * PAIR] * (ATT_SCALE * LOG2_E)).astype(BF16), k_st[b][kv_of(j)],
                         (((1,), (1,)), ((), ())), preferred_element_type=F32) for b, j in pairs]
    probs, dens = [], []
    for idx, (b, j) in enumerate(pairs):
        halves, den = [], []
        for e in range(2):
            sink = sink_ref[2 * j + e] * LOG2_E
            sh = jnp.where(valid, s[idx][:, e * span:(e + 1) * span], -jnp.inf)
            mx = jnp.maximum(jnp.max(sh, -1, keepdims=True), sink)
            ph = jnp.exp2(sh - mx)
            halves.append(ph)
            den.append(jnp.sum(ph, -1, keepdims=True) + jnp.exp2(sink - mx))
        probs.append(jnp.concatenate(halves, axis=1).astype(BF16))
        dens.append(jnp.where(out_lane < HEAD_DIM, den[0], den[1]))
    o = [jnp.dot(probs[idx], v_st[b][kv_of(j)], preferred_element_type=F32) / dens[idx]
         for idx, (b, j) in enumerate(pairs)]
    n_pairs = ATT_HEADS // 2
    for b in range(nseq):
        o_ref[b] = jnp.concatenate(o[b * n_pairs:(b + 1) * n_pairs], axis=1).astype(o_ref.dtype)


def _attn_prompt(p3, sinks, nseq):
    batch, seq = p3.shape[0], p3.shape[1]
    nb = seq // QB
    kcol, vcol = P_AK // KV_WIDTH, P_AV // KV_WIDTH
    prev = lambda col: pl.BlockSpec((nseq, QB, KV_WIDTH), lambda g, m: (g, jnp.maximum(m - 1, 0), col))
    cur = lambda col: pl.BlockSpec((nseq, QB, KV_WIDTH), lambda g, m: (g, m, col))
    return pl.pallas_call(
        _attn_prompt_kernel,
        grid=(batch // nseq, nb),
        in_specs=[
            pl.BlockSpec(memory_space=pltpu.SMEM),
            pl.BlockSpec((nseq, QB, ATT_WIDTH), lambda g, m: (g, m, P_Q // ATT_WIDTH)),
            prev(kcol), cur(kcol), prev(vcol), cur(vcol),
        ],
        out_specs=pl.BlockSpec((nseq, QB, ATT_WIDTH), lambda g, m: (g, m, 0)),
        out_shape=jax.ShapeDtypeStruct((batch, seq, ATT_WIDTH), BF16),
        compiler_params=_cparams(("arbitrary", "arbitrary")),
        name="attn_prompt",
    )(sinks, p3, p3, p3, p3, p3)


def _attn_sample_kernel(sink_ref, q_ref, kn_ref, vn_ref, kc_ref, vc_ref, o_ref, kw_ref, vw_ref):
    t = q_ref.shape[0]
    k = jnp.concatenate([kc_ref[...], kn_ref[...]], axis=0)
    v = jnp.concatenate([vc_ref[...], vn_ref[...]], axis=0)
    o_ref[...] = _attend(q_ref[...], k, v, None, sink_ref).astype(o_ref.dtype)
    kw_ref[...] = k[t:]
    vw_ref[...] = v[t:]


def _attn_sample(p, sinks, cache_k, cache_v, batch, seq):
    kcol, vcol = P_AK // KV_WIDTH, P_AV // KV_WIDTH
    return pl.pallas_call(
        _attn_sample_kernel,
        grid=(batch,),
        in_specs=[
            pl.BlockSpec(memory_space=pltpu.SMEM),
            pl.BlockSpec((seq, ATT_WIDTH), lambda b: (b, P_Q // ATT_WIDTH)),
            pl.BlockSpec((seq, KV_WIDTH), lambda b: (b, kcol)),
            pl.BlockSpec((seq, KV_WIDTH), lambda b: (b, vcol)),
            pl.BlockSpec((WINDOW, KV_WIDTH), lambda b: (b, 0)),
            pl.BlockSpec((WINDOW, KV_WIDTH), lambda b: (b, 0)),
        ],
        out_specs=[
            pl.BlockSpec((seq, ATT_WIDTH), lambda b: (b, 0)),
            pl.BlockSpec((WINDOW, KV_WIDTH), lambda b: (b, 0)),
            pl.BlockSpec((WINDOW, KV_WIDTH), lambda b: (b, 0)),
        ],
        out_shape=[
            jax.ShapeDtypeStruct((batch * seq, ATT_WIDTH), BF16),
            jax.ShapeDtypeStruct((batch * WINDOW, KV_WIDTH), F32),
            jax.ShapeDtypeStruct((batch * WINDOW, KV_WIDTH), F32),
        ],
        compiler_params=_cparams(("arbitrary",)),
        name="attn_sample",
    )(sinks, p, p, p, cache_k, cache_v)


SEG = 256
EXP_NEG_HALF = math.exp(-0.5)
PAIR = 2 * HEAD_DIM
N_PAIRS = RW_HEADS // 2


def _split2(x):
    hi = x.astype(BF16)
    lo = (x - hi.astype(F32)).astype(BF16)
    return hi, lo


def _seg_sum(x, bd):
    c = x.shape[0]
    xb = x.astype(BF16)
    stack = jnp.concatenate([xb[:, s:s + SEG] for s in range(0, RW_WIDTH, SEG)], axis=0)
    r = jnp.dot(stack, bd, preferred_element_type=F32)
    return jnp.concatenate([r[i * c:(i + 1) * c] for i in range(RW_WIDTH // SEG)], axis=1)


def _dotb(a, b):
    return jnp.dot(a.astype(BF16), b.astype(BF16), preferred_element_type=F32)


def _dotb_nt(a, b):
    return lax.dot_general(a.astype(BF16), b.astype(BF16), (((1,), (1,)), ((), ())), preferred_element_type=F32)


def _dotb_tn(a, b):
    return lax.dot_general(a.astype(BF16), b.astype(BF16), (((0,), (0,)), ((), ())), preferred_element_type=F32)


def _stack_heads(x, half):
    lane = lax.broadcasted_iota(jnp.int32, x.shape, 1)
    first = (lane % (2 * half)) < half
    zero = jnp.zeros_like(x)
    return jnp.concatenate([jnp.where(first, x, zero), jnp.where(first, zero, x)], axis=0)


def _stack_heads_bf16(x, half):
    return _stack_heads(x.astype(BF16), half)


def _rwkv_pairs(ab, rb, bb, kb, v, gam_c, s_bd, n_double):
    c = ab[0].shape[0]
    pairs = range(len(ab))
    t_idx = lax.broadcasted_iota(jnp.int32, (c, 2 * c), 0)
    s_idx = lax.broadcasted_iota(jnp.int32, (c, 2 * c), 1) % c
    strict, incl = s_idx < t_idx, s_idx <= t_idx
    g = [_dotb_nt(jnp.concatenate([ab[p], rb[p]], axis=0),
                  jnp.concatenate([_stack_heads_bf16(bb[p], HEAD_DIM), _stack_heads_bf16(kb[p], HEAD_DIM)], axis=0))
         for p in pairs]
    l_ak = [jnp.where(strict, g[p][:c, 2 * c:], 0.0) for p in pairs]
    m_rbk = [jnp.concatenate([jnp.where(incl, g[p][c:, :2 * c], 0.0), jnp.where(incl, g[p][c:, 2 * c:], 0.0)], axis=1)
             for p in pairs]
    lp = [jnp.where(strict, g[p][:c, :2 * c], 0.0) for p in pairs]
    v_st = [_stack_heads_bf16(v[p], HEAD_DIM) for p in pairs]
    y0 = [jnp.concatenate([ab[p], _dotb(l_ak[p], v_st[p])], axis=1) for p in pairs]
    n_acc = list(lp)
    lk_b = [lp[p].astype(BF16) for p in pairs]
    lk = [_dotb(lk_b[p], _stack_heads(lk_b[p], c)) for p in pairs]
    for k in range(1, n_double):
        n_st = [_stack_heads_bf16(n_acc[p], c) for p in pairs]
        lk_b = [lk[p].astype(BF16) for p in pairs]
        if k + 1 < n_double:
            z = [_dotb(lk_b[p], jnp.concatenate([n_st[p], _stack_heads(lk_b[p], c)], axis=1)) for p in pairs]
            n_acc = [n_acc[p] + lk[p] + z[p][:, :2 * c] for p in pairs]
            lk = [z[p][:, 2 * c:] for p in pairs]
        else:
            n_acc = [n_acc[p] + lk[p] + _dotb(lk_b[p], n_st[p]) for p in pairs]
    y = [y0[p] + _dotb(n_acc[p], _stack_heads_bf16(y0[p], HEAD_DIM)) for p in pairs]
    ya = [y[p][:, :PAIR] for p in pairs]
    yu = [y[p][:, PAIR:] for p in pairs]
    ou = []
    for p in pairs:
        lhs = jnp.concatenate([rb[p], ya[p]], axis=0).astype(BF16)
        ou.append(lax.dot_general(jnp.concatenate([lhs, lhs], axis=1), jnp.concatenate(_split2(s_bd[p]), axis=1),
                                  (((1,), (1,)), ((), ())), preferred_element_type=F32))
    u = [ou[p][c:] + yu[p] for p in pairs]
    o = [ou[p][:c] + _dotb(m_rbk[p], jnp.concatenate([_stack_heads_bf16(u[p], HEAD_DIM), v_st[p]], axis=0))
         for p in pairs]
    row_h = lax.broadcasted_iota(jnp.int32, (PAIR, PAIR), 0) // HEAD_DIM
    col_h = lax.broadcasted_iota(jnp.int32, (PAIR, PAIR), 1) // HEAD_DIM
    same = row_h == col_h
    d_t = [jnp.where(same, _dotb_tn(jnp.concatenate([u[p], v[p]], axis=0),
                                    jnp.concatenate([bb[p], kb[p]], axis=0)), 0.0) for p in pairs]
    s_new = [(s_bd[p] + d_t[p]) * gam_c[p] for p in pairs]
    return o, s_new


def _rwkv_kernel(xr_ref, xk_ref, xv_ref, xl_ref, w0_ref, a0_ref, kk_ref, ka_ref, rk_ref,
                 lg_ref, lb_ref, w2_ref, a2_ref, g2_ref, bd_ref, st0_ref,
                 out_ref, st_ref, s_scr):
    c = pl.program_id(1)
    nc = pl.num_programs(1)
    nb, C = xr_ref.shape[0], xr_ref.shape[1]
    n_double = int(math.log2(C))

    @pl.when(c == 0)
    def _():
        s_scr[...] = jnp.zeros_like(s_scr)
        for b in range(nb):
            for h in range(RW_HEADS):
                off = (h % 2) * HEAD_DIM
                s_scr[b * N_PAIRS + h // 2, off:off + HEAD_DIM, off:off + HEAD_DIM] = st0_ref[b, h]

    rows_of = lambda load: jnp.concatenate([load(b) for b in range(nb)], axis=0)
    wa_w = DECAY_LORA + AAA_LORA
    r = rows_of(lambda b: xr_ref[b])
    kr = rows_of(lambda b: xk_ref[b])
    v = rows_of(lambda b: xv_ref[b])
    xw = rows_of(lambda b: xl_ref[b, :, :wa_w])
    gd = rows_of(lambda b: xl_ref[b, :, wa_w:wa_w + GATE_LORA])
    wd = xw[:, 0:DECAY_LORA]
    ad = xw[:, DECAY_LORA:wa_w]

    lw = (-EXP_NEG_HALF * LOG2_E) * jax.nn.sigmoid(w0_ref[...] + _dotb(jnp.tanh(wd), w2_ref[...]))
    a = jax.nn.sigmoid(a0_ref[...] + _dotb(ad, a2_ref[...]))
    g = _dotb(jax.nn.sigmoid(gd), g2_ref[...])
    bd = bd_ref[...]
    kk = kr * kk_ref[...]
    kk = kk * lax.rsqrt(jnp.maximum(_seg_sum(kk * kk, bd), 1e-24))
    k2 = kr * (1.0 + (a - 1.0) * ka_ref[...])

    rows = nb * C
    ti = lax.broadcasted_iota(jnp.int32, (rows, rows), 0)
    si = lax.broadcasted_iota(jnp.int32, (rows, rows), 1)
    tri = jnp.logical_and(si <= ti, si // C == ti // C).astype(BF16)
    cs2 = jnp.dot(tri, jnp.concatenate(_split2(lw), axis=1), preferred_element_type=F32)
    cs = cs2[:, :RW_WIDTH] + cs2[:, RW_WIDTH:]
    gam = jnp.exp2(cs)
    ginv = jnp.exp2(-cs)
    ab = -kk * jnp.exp2(cs - lw)
    rb = r * gam
    bb = kk * a * ginv
    kb = k2 * ginv

    def per_pair(x, r0, r1):
        return [x[b * C + r0:b * C + r1, p * PAIR:(p + 1) * PAIR] for b in range(nb) for p in range(N_PAIRS)]

    chains = nb * N_PAIRS
    outs, s_new = _rwkv_pairs(per_pair(ab, 0, C), per_pair(rb, 0, C), per_pair(bb, 0, C), per_pair(kb, 0, C),
                              per_pair(v, 0, C), per_pair(gam, C - 1, C), [s_scr[q] for q in range(chains)], n_double)
    for q in range(chains):
        s_scr[q] = s_new[q]
    o = jnp.concatenate([jnp.concatenate(outs[b * N_PAIRS:(b + 1) * N_PAIRS], axis=1) for b in range(nb)], axis=0)

    mo = _seg_sum(o, bd) * (1.0 / HEAD_DIM)
    d = o - mo
    vo = _seg_sum(d * d, bd) * (1.0 / HEAD_DIM)
    on = d * lax.rsqrt(vo + LNX_EPS) * lg_ref[...] + lb_ref[...]
    bonus = _seg_sum(r * k2 * rk_ref[...], bd) * v
    res = ((on + bonus) * g).astype(out_ref.dtype)
    for b in range(nb):
        out_ref[b] = res[b * C:(b + 1) * C, :]

    @pl.when(c == nc - 1)
    def _():
        for b in range(nb):
            for h in range(RW_HEADS):
                off = (h % 2) * HEAD_DIM
                st_ref[b, h] = s_scr[b * N_PAIRS + h // 2, off:off + HEAD_DIM, off:off + HEAD_DIM]


def _rwkv(p3, state0, prm, chunk, nb):
    batch, seq = p3.shape[0], p3.shape[1]
    nc = seq // chunk
    row = lambda w: pl.BlockSpec((1, w), lambda g, c: (0, 0))
    xspec = lambda w, col: pl.BlockSpec((nb, chunk, w), lambda g, c: (g, c, col))
    full = lambda a: pl.BlockSpec(a.shape, lambda g, c: (0,) * a.ndim)
    stspec = pl.BlockSpec((nb, RW_HEADS, HEAD_DIM, HEAD_DIM), lambda g, c: (g, 0, 0, 0))
    return pl.pallas_call(
        _rwkv_kernel,
        grid=(batch // nb, nc),
        in_specs=[
            xspec(RW_WIDTH, P_R // RW_WIDTH), xspec(RW_WIDTH, P_K // RW_WIDTH), xspec(RW_WIDTH, P_V // RW_WIDTH),
            xspec(LORA_W, P_LORA // LORA_W),
            row(RW_WIDTH), row(RW_WIDTH), row(RW_WIDTH), row(RW_WIDTH), row(RW_WIDTH),
            row(RW_WIDTH), row(RW_WIDTH),
            full(prm["w2"]), full(prm["a2"]), full(prm["g2"]), full(prm["bd"]),
            stspec,
        ],
        out_specs=[
            pl.BlockSpec((nb, chunk, RW_WIDTH), lambda g, c: (g, c, 0)),
            stspec,
        ],
        out_shape=[
            jax.ShapeDtypeStruct((batch, seq, RW_WIDTH), BF16),
            jax.ShapeDtypeStruct((batch, RW_HEADS, HEAD_DIM, HEAD_DIM), F32),
        ],
        scratch_shapes=[pltpu.VMEM((nb * N_PAIRS, PAIR, PAIR), F32)],
        compiler_params=_cparams(("arbitrary", "arbitrary")),
        name="rwkv",
    )(p3, p3, p3, p3, prm["w0"], prm["a0"], prm["k_k"], prm["k_a"], prm["r_k"],
      prm["lnx_g"], prm["lnx_b"], prm["w2"], prm["a2"], prm["g2"], prm["bd"], state0)


D_TN = 512


def _out_proj_body(att_ref, rw_ref, h_ref, g_ref, b_ref, h1_ref, h1b_ref, w_scr):
    for c0 in range(0, D_MODEL, D_TN):
        m = jnp.dot(att_ref[...], w_scr[:ATT_WIDTH, c0:c0 + D_TN], preferred_element_type=F32)
        m = m + jnp.dot(rw_ref[...], w_scr[ATT_WIDTH:, c0:c0 + D_TN], preferred_element_type=F32)
        h1_ref[:, c0:c0 + D_TN] = ALPHA * h_ref[:, c0:c0 + D_TN] + m
    y = _layer_norm(h1_ref[...], g_ref[...], b_ref[...])
    h1_ref[...] = y
    h1b_ref[...] = y.astype(BF16)


def _out_proj(att_p, rw_p, h_p, att_s, rw_s, h_s, w_out, g, b, tm):
    n_p, n_s = att_p.shape[0], att_s.shape[0]
    nrb = n_p // tm
    last = lambda i: jnp.minimum(i, nrb - 1)

    def kern(ap_ref, rp_ref, hp_ref, as_ref, rs_ref, hs_ref, wa_ref, wb_ref, g_ref, b_ref,
             h1p_ref, h1bp_ref, h1s_ref, h1bs_ref, w_scr):
        i = pl.program_id(0)

        @pl.when(i == 0)
        def _():
            w_scr[:ATT_WIDTH, :] = wa_ref[...].astype(BF16)
            w_scr[ATT_WIDTH:, :] = wb_ref[...].astype(BF16)

        pl.when(i < nrb)(lambda: _out_proj_body(ap_ref, rp_ref, hp_ref, g_ref, b_ref, h1p_ref, h1bp_ref, w_scr))
        pl.when(i == nrb)(lambda: _out_proj_body(as_ref, rs_ref, hs_ref, g_ref, b_ref, h1s_ref, h1bs_ref, w_scr))

    rows_p = lambda w: pl.BlockSpec((tm, w), lambda i: (last(i), 0))
    rows_s = lambda w: pl.BlockSpec((n_s, w), lambda i: (0, 0))
    return pl.pallas_call(
        kern,
        grid=(nrb + 1,),
        in_specs=[
            rows_p(ATT_WIDTH), rows_p(RW_WIDTH), rows_p(D_MODEL),
            rows_s(ATT_WIDTH), rows_s(RW_WIDTH), rows_s(D_MODEL),
            pl.BlockSpec((ATT_WIDTH, D_MODEL), lambda i: (0, 0), pipeline_mode=pl.Buffered(1)),
            pl.BlockSpec((RW_WIDTH, D_MODEL), lambda i: (1, 0), pipeline_mode=pl.Buffered(1)),
            pl.BlockSpec((1, D_MODEL), lambda i: (0, 0)),
            pl.BlockSpec((1, D_MODEL), lambda i: (0, 0)),
        ],
        out_specs=[rows_p(D_MODEL), rows_p(D_MODEL), rows_s(D_MODEL), rows_s(D_MODEL)],
        out_shape=[
            jax.ShapeDtypeStruct((n_p, D_MODEL), F32), jax.ShapeDtypeStruct((n_p, D_MODEL), BF16),
            jax.ShapeDtypeStruct((n_s, D_MODEL), F32), jax.ShapeDtypeStruct((n_s, D_MODEL), BF16),
        ],
        scratch_shapes=[pltpu.VMEM((ATT_WIDTH + RW_WIDTH, D_MODEL), BF16)],
        compiler_params=_cparams(("arbitrary",)),
        name="out_proj",
    )(att_p, rw_p, h_p, att_s, rw_s, h_s, w_out, w_out, g, b)


E_TN = 512
E_NCB = D_FF // E_TN
E_SUB = 1024
SQRT_HALF = math.sqrt(0.5)


def _ffn_up_body(bps, nseq, i, h_ref, w_scr, cwg_ref, cwv_ref, cbg_ref, cbv_ref, sg_ref, sv_ref,
                 act_ref, og_ref, ov_ref, cg_scr, cv_scr):
    tm = h_ref.shape[0]
    sub = min(tm, E_SUB)
    n_sub = tm // sub
    t_seq = tm // nseq
    row8 = lax.broadcasted_iota(jnp.int32, (SUBLANES, E_TN), 0)

    def products(r0):
        u = jnp.dot(h_ref[r0:r0 + sub, :], w_scr[...], preferred_element_type=F32)
        return u[:, :E_TN], u[:, E_TN:]

    def conv(u, prevs, cw_ref, cb_ref):
        sh1, sh2 = pltpu.roll(u, 1, 0), pltpu.roll(u, 2, 0)
        for r, prev in prevs:
            p2, p1 = prev[0:1, :], prev[1:2, :]
            fix1 = jnp.where(row8 == 0, p1, sh1[r:r + SUBLANES, :])
            fix2 = jnp.where(row8 == 0, p2, jnp.where(row8 == 1, p1, sh2[r:r + SUBLANES, :]))
            head = lambda x: [x[:r, :]] if r > 0 else []
            tail = lambda x: [x[r + SUBLANES:, :]] if r + SUBLANES < x.shape[0] else []
            sh1 = jnp.concatenate(head(sh1) + [fix1] + tail(sh1), axis=0)
            sh2 = jnp.concatenate(head(sh2) + [fix2] + tail(sh2), axis=0)
        cw = cw_ref[...]
        c = cb_ref[...] + sh2 * cw[0:1, :]
        c = c + sh1 * cw[1:2, :]
        return c + u * cw[2:3, :]

    def epilogue(r0, ug, uv, prev_g, prev_v):
        gate = conv(ug, prev_g, cwg_ref, cbg_ref)
        val = conv(uv, prev_v, cwv_ref, cbv_ref)
        gelu = 0.5 * gate * (1.0 + lax.erf(gate * SQRT_HALF))
        act_ref[r0:r0 + sub, :] = (gelu * val).astype(act_ref.dtype)

    if nseq > 1:
        ug, uv = products(0)
        epilogue(0, ug, uv, [(s * t_seq, sg_ref[s]) for s in range(nseq)],
                 [(s * t_seq, sv_ref[s]) for s in range(nseq)])
        for s in range(nseq):
            og_ref[s] = ug[(s + 1) * t_seq - SUBLANES:(s + 1) * t_seq, :]
            ov_ref[s] = uv[(s + 1) * t_seq - SUBLANES:(s + 1) * t_seq, :]
    else:
        first = (i % bps) == 0
        prev_g = jnp.where(first, sg_ref[0], cg_scr[SUBLANES - 2:SUBLANES, :])
        prev_v = jnp.where(first, sv_ref[0], cv_scr[SUBLANES - 2:SUBLANES, :])
        nxt = products(0)
        for k in range(n_sub):
            ug, uv = nxt
            if k + 1 < n_sub:
                nxt = products((k + 1) * sub)
            epilogue(k * sub, ug, uv, [(0, prev_g)], [(0, prev_v)])
            prev_g, prev_v = ug[sub - 2:sub, :], uv[sub - 2:sub, :]
        cg_scr[...] = ug[sub - SUBLANES:sub, :]
        cv_scr[...] = uv[sub - SUBLANES:sub, :]
        og_ref[0] = ug[sub - SUBLANES:sub, :]
        ov_ref[0] = uv[sub - SUBLANES:sub, :]


def _ffn_up(h_p, conv_prev_p, seq_p, tm_p, h_s, conv_prev_s, seq_s, w_up, conv_w, conv_b, w_down):
    n_p, n_s = h_p.shape[0], h_s.shape[0]
    batch_p, batch_s = n_p // seq_p, n_s // seq_s
    bps = seq_p // tm_p
    nrb = n_p // tm_p
    slab = D_FF // (E_NCB * nrb)
    last = lambda i: jnp.maximum(i - 1, 0)

    def kern(hp_ref, sgp_ref, svp_ref, hs_ref, sgs_ref, svs_ref, wg_ref, wv_ref, cwg_ref, cwv_ref, cbg_ref, cbv_ref,
             wd_ref, actp_ref, ogp_ref, ovp_ref, acts_ref, ogs_ref, ovs_ref, wdb_ref, w_scr, cg_scr, cv_scr):
        i = pl.program_id(1)

        @pl.when(i == 0)
        def _():
            w_scr[:, :E_TN] = wg_ref[...].astype(BF16)
            w_scr[:, E_TN:] = wv_ref[...].astype(BF16)
            _ffn_up_body(1, batch_s, i, hs_ref, w_scr, cwg_ref, cwv_ref, cbg_ref, cbv_ref, sgs_ref, svs_ref,
                         acts_ref, ogs_ref, ovs_ref, cg_scr, cv_scr)

        @pl.when(i > 0)
        def _():
            wdb_ref[...] = wd_ref[...].astype(BF16)
            _ffn_up_body(bps, 1, i - 1, hp_ref, w_scr, cwg_ref, cwv_ref, cbg_ref, cbv_ref, sgp_ref, svp_ref,
                         actp_ref, ogp_ref, ovp_ref, cg_scr, cv_scr)

    col = lambda shape, off: pl.BlockSpec(shape, lambda j, i: (0,) * (len(shape) - 1) + (j + off,))
    st_p = lambda off: pl.BlockSpec((1, CONV_W - 1, E_TN), lambda j, i: (last(i) // bps, 0, j + off))
    st_s = lambda off: pl.BlockSpec((batch_s, CONV_W - 1, E_TN), lambda j, i: (0, 0, j + off))
    tail_p = pl.BlockSpec((1, SUBLANES, E_TN), lambda j, i: (last(i) // bps, 0, j))
    tail_s = pl.BlockSpec((batch_s, SUBLANES, E_TN), lambda j, i: (0, 0, j))
    wd_spec = pl.BlockSpec((slab, D_MODEL), lambda j, i: (j * nrb + last(i), 0))
    return pl.pallas_call(
        kern,
        grid=(E_NCB, nrb + 1),
        in_specs=[
            pl.BlockSpec((tm_p, D_MODEL), lambda j, i: (last(i), 0)), st_p(0), st_p(E_NCB),
            pl.BlockSpec((n_s, D_MODEL), lambda j, i: (0, 0)), st_s(0), st_s(E_NCB),
            col((D_MODEL, E_TN), 0), col((D_MODEL, E_TN), E_NCB),
            col((CONV_W, E_TN), 0), col((CONV_W, E_TN), E_NCB),
            col((1, E_TN), 0), col((1, E_TN), E_NCB),
            wd_spec,
        ],
        out_specs=[
            pl.BlockSpec((tm_p, E_TN), lambda j, i: (last(i), j)), tail_p, tail_p,
            pl.BlockSpec((n_s, E_TN), lambda j, i: (0, j)), tail_s, tail_s,
            wd_spec,
        ],
        out_shape=[
            jax.ShapeDtypeStruct((n_p, D_FF), BF16),
            jax.ShapeDtypeStruct((batch_p, SUBLANES, D_FF), F32), jax.ShapeDtypeStruct((batch_p, SUBLANES, D_FF), F32),
            jax.ShapeDtypeStruct((n_s, D_FF), BF16),
            jax.ShapeDtypeStruct((batch_s, SUBLANES, D_FF), F32), jax.ShapeDtypeStruct((batch_s, SUBLANES, D_FF), F32),
            jax.ShapeDtypeStruct((D_FF, D_MODEL), BF16),
        ],
        scratch_shapes=[pltpu.VMEM((D_MODEL, 2 * E_TN), BF16),
                        pltpu.VMEM((SUBLANES, E_TN), F32), pltpu.VMEM((SUBLANES, E_TN), F32)],
        compiler_params=_cparams(("arbitrary", "arbitrary")),
        name="ffn_up",
    )(h_p, conv_prev_p, conv_prev_p, h_s, conv_prev_s, conv_prev_s, w_up, w_up, conv_w, conv_w, conv_b, conv_b, w_down)


F_TN = 512


def _ffn_down_kernel(act_ref, w_ref, h1_ref, g_ref, b_ref, y_ref):
    for c0 in range(0, D_MODEL, F_TN):
        f = jnp.dot(act_ref[...], w_ref[:, c0:c0 + F_TN], preferred_element_type=F32)
        y_ref[:, c0:c0 + F_TN] = ALPHA * h1_ref[:, c0:c0 + F_TN] + f
    y_ref[...] = _layer_norm(y_ref[...], g_ref[...], b_ref[...])


def _ffn_down(act_p, h1_p, act_s, h1_s, w_down, g, b, tm):
    n_p, n_s = act_p.shape[0], act_s.shape[0]
    nrb = n_p // tm
    last = lambda i: jnp.minimum(i, nrb - 1)

    def kern(ap_ref, hp_ref, as_ref, hs_ref, w_ref, g_ref, b_ref, yp_ref, ys_ref):
        i = pl.program_id(0)
        pl.when(i < nrb)(lambda: _ffn_down_kernel(ap_ref, w_ref, hp_ref, g_ref, b_ref, yp_ref))
        pl.when(i == nrb)(lambda: _ffn_down_kernel(as_ref, w_ref, hs_ref, g_ref, b_ref, ys_ref))

    rows_p = lambda w: pl.BlockSpec((tm, w), lambda i: (last(i), 0))
    rows_s = lambda w: pl.BlockSpec((n_s, w), lambda i: (0, 0))
    return pl.pallas_call(
        kern,
        grid=(nrb + 1,),
        in_specs=[
            rows_p(D_FF), rows_p(D_MODEL), rows_s(D_FF), rows_s(D_MODEL),
            pl.BlockSpec((D_FF, D_MODEL), lambda i: (0, 0), pipeline_mode=pl.Buffered(1)),
            pl.BlockSpec((1, D_MODEL), lambda i: (0, 0)),
            pl.BlockSpec((1, D_MODEL), lambda i: (0, 0)),
        ],
        out_specs=[rows_p(D_MODEL), rows_s(D_MODEL)],
        out_shape=[jax.ShapeDtypeStruct((n_p, D_MODEL), F32), jax.ShapeDtypeStruct((n_s, D_MODEL), F32)],
        compiler_params=_cparams(("arbitrary",)),
        name="ffn_down",
    )(act_p, h1_p, act_s, h1_s, w_down, g, b)


def _rope_tables(pos, reps):
    half = HEAD_DIM // 2
    inv = ROPE_THETA ** (-jnp.arange(half, dtype=F32) / half)
    ang = pos.astype(F32)[:, None] * inv[None, :]
    cos, sin = jnp.cos(ang), jnp.sin(ang)
    cos128 = jnp.concatenate([cos, cos, cos, cos], axis=1)
    sin128 = jnp.concatenate([-sin, sin, -sin, sin], axis=1)
    return jnp.tile(cos128, (reps, 1)), jnp.tile(sin128, (reps, 1))


TM = 256
D_TM = 256
E_TM = 1024
F_TM = 256
RWKV_NB = 4
ATTN_NB = 2


def kernel(x_prompt, x_sample, cache_k, cache_v, state_wkv, state_shift, state_ffn_conv, ln_in_g, ln_in_b, w_in, attn_sinks, rw_mu, rw_w0, rw_w2, rw_a0, rw_a2, rw_g2, rw_k_k, rw_k_a, rw_r_k, rw_lnx_g, rw_lnx_b, w_out, ln1_g, ln1_b, ffn_w_up, ffn_conv_w, ffn_conv_b, ffn_w_down, ln2_g, ln2_b):
    l = 0
    seg = lax.broadcasted_iota(jnp.int32, (SEG, SEG), 0) // HEAD_DIM
    bd = (seg == seg.T).astype(BF16)
    row = lambda a: a.reshape(1, -1)
    rw_prm = {"w0": row(rw_w0[l]), "a0": row(rw_a0[l]), "k_k": row(rw_k_k[l]),
              "k_a": row(rw_k_a[l]), "r_k": row(rw_r_k[l]), "lnx_g": row(rw_lnx_g[l]), "lnx_b": row(rw_lnx_b[l]),
              "w2": rw_w2[l].astype(BF16), "a2": rw_a2[l].astype(BF16), "g2": rw_g2[l].astype(BF16), "bd": bd}
    sinks = attn_sinks[l]
    bp, tp = x_prompt.shape[0], x_prompt.shape[1]
    bs, ts = x_sample.shape[0], x_sample.shape[1]
    n_p, n_s = bp * tp, bs * ts
    zeros = lambda *s: jnp.zeros(s, F32)

    p_p, h_p, tails_p, p_s, h_s, tails_s = _in_proj(
        x_prompt.reshape(n_p, D_MODEL), x_sample.reshape(n_s, D_MODEL), row(ln_in_g), row(ln_in_b),
        w_in[l].astype(BF16), row(rw_mu[l]), zeros(bp, 1, RW_COLS), state_shift[l],
        _rope_tables(jnp.arange(tp, dtype=jnp.int32), 1),
        _rope_tables(PAST_LEN + jnp.arange(ts, dtype=jnp.int32), bs), TM, tp, ts)
    p3_p, p3_s = p_p.reshape(bp, tp, P_COLS), p_s.reshape(bs, ts, P_COLS)

    att_p = _attn_prompt(p3_p, sinks, ATTN_NB).reshape(n_p, ATT_WIDTH)
    att_s, s_k, s_v = _attn_sample(p_s, sinks, cache_k[l].reshape(bs * WINDOW, KV_WIDTH),
                                   cache_v[l].reshape(bs * WINDOW, KV_WIDTH), bs, ts)
    rw_p, p_wkv = _rwkv(p3_p, zeros(bp, RW_HEADS, HEAD_DIM, HEAD_DIM), rw_prm, CHUNK, RWKV_NB)
    rw_s, s_wkv = _rwkv(p3_s, state_wkv[l], rw_prm, ts, RWKV_NB)

    h1_p, h1b_p, h1_s, h1b_s = _out_proj(att_p, rw_p.reshape(n_p, RW_WIDTH), h_p, att_s, rw_s.reshape(n_s, RW_WIDTH), h_s,
                                         w_out[l], row(ln1_g[l]), row(ln1_b[l]), D_TM)
    act_p, cg_p, cv_p, act_s, cg_s, cv_s, w_down_b = _ffn_up(
        h1b_p, zeros(bp, CONV_W - 1, 2 * D_FF), tp, E_TM, h1b_s, state_ffn_conv[l], ts,
        ffn_w_up[l], ffn_conv_w[l], row(ffn_conv_b[l]), ffn_w_down[l])
    y_p, y_s = _ffn_down(act_p, h1_p, act_s, h1_s, w_down_b, row(ln2_g[l]), row(ln2_b[l]), F_TM)

    kv_shape = lambda b: (1, b, WINDOW, ATT_KV_HEADS, HEAD_DIM)
    p_k = p3_p[:, tp - WINDOW:, P_AK:P_AK + KV_WIDTH].reshape(kv_shape(bp))
    p_v = p3_p[:, tp - WINDOW:, P_AV:P_AV + KV_WIDTH].reshape(kv_shape(bp))
    shift_of = lambda tails, b: tails.reshape(b, -1, RW_COLS)[:, -1:, :][None]
    conv_of = lambda cg, cv: jnp.concatenate([cg[:, SUBLANES - 2:, :], cv[:, SUBLANES - 2:, :]], axis=-1)[None]
    return (y_p.reshape(bp, tp, D_MODEL), y_s.reshape(bs, ts, D_MODEL), p_k, p_v, p_wkv[None],
            shift_of(tails_p, bp), conv_of(cg_p, cv_p),
            s_k.reshape(kv_shape(bs)), s_v.reshape(kv_shape(bs)), s_wkv[None],
            shift_of(tails_s, bs), conv_of(cg_s, cv_s))
```

```python
import math

import jax
import jax.numpy as jnp
from jax import lax
from jax.experimental import pallas as pl
from jax.experimental.pallas import tpu as pltpu

D_MODEL = 2048
HEAD_DIM = 64
ATT_HEADS = 16
ATT_KV_HEADS = 2
ATT_GROUP = ATT_HEADS // ATT_KV_HEADS
ATT_WIDTH = ATT_HEADS * HEAD_DIM
KV_WIDTH = ATT_KV_HEADS * HEAD_DIM
CHUNK = 64
WINDOW = 128
ROPE_THETA = 10000.0
ATT_SCALE = HEAD_DIM ** -0.5
RW_HEADS = 16
RW_WIDTH = RW_HEADS * HEAD_DIM
DECAY_LORA = 64
AAA_LORA = 64
GATE_LORA = 160
LNX_EPS = 64e-5
ATT_COLS = ATT_WIDTH + 2 * KV_WIDTH
RW_COLS = 3 * RW_WIDTH + DECAY_LORA + AAA_LORA + GATE_LORA
D_FF = 5632
CONV_W = 3
LN_EPS = 1e-5
DEPTH = 1
ALPHA = (2 * DEPTH) ** 0.25
PAST_LEN = 1024

LORA_W = 512
P_R, P_K, P_V, P_Q = 0, 1024, 2048, 3072
P_LORA = 4096
P_AK = P_LORA + LORA_W
P_AV = P_AK + KV_WIDTH
P_COLS = P_AV + KV_WIDTH
LANES = 128
SUBLANES = 8
VMEM_LIMIT = 56 * 1024 * 1024

F32 = jnp.float32
BF16 = jnp.bfloat16


def _cparams(sem):
    return pltpu.CompilerParams(dimension_semantics=sem, vmem_limit_bytes=VMEM_LIMIT)


def _layer_norm(y, g, b):
    mu = jnp.mean(y, -1, keepdims=True)
    d = y - mu
    var = jnp.mean(d * d, -1, keepdims=True)
    return d * lax.rsqrt(var + LN_EPS) * g + b


A_TN = 512


def _rope128(x, cos, sin_signed):
    lane = lax.broadcasted_iota(jnp.int32, x.shape, 1)
    first_half = (lane % HEAD_DIM) < (HEAD_DIM // 2)
    rot = jnp.where(first_half, pltpu.roll(x, LANES - HEAD_DIM // 2, 1), pltpu.roll(x, HEAD_DIM // 2, 1))
    return x * cos + rot * sin_signed


def _is_rope_col(col):
    return (P_Q <= col < P_Q + ATT_WIDTH) or (P_AK <= col < P_AK + KV_WIDTH)


O_R = ATT_COLS
O_WD = O_R + RW_WIDTH
O_K = O_WD + DECAY_LORA
O_V = O_K + RW_WIDTH
O_AD = O_V + RW_WIDTH
O_GD = O_AD + AAA_LORA
IN_COLS = O_GD + GATE_LORA


def _in_proj_kernel(first, x_ref, g_ref, b_ref, w_ref, cos_ref, sin_ref, mu_ref, st_ref, p_ref, h_ref, tail_ref,
                    hb_ref, carry_scr):
    h = _layer_norm(x_ref[...], g_ref[...], b_ref[...])
    h_ref[...] = h
    hb_ref[...] = h.astype(BF16)
    tm = x_ref.shape[0]
    nseq = tail_ref.shape[0]
    t_seq = tm // nseq
    row8 = lax.broadcasted_iota(jnp.int32, (SUBLANES, 1), 0)
    if nseq == 1:
        @pl.when(first)
        def _():
            carry_scr[...] = jnp.zeros_like(carry_scr)

    def product(c0, width):
        acc = jnp.dot(hb_ref[...], w_ref[:, c0:c0 + width], preferred_element_type=F32)
        if c0 < O_R:
            return acc
        cols = slice(c0 - O_R, c0 - O_R + width)
        prev = pltpu.roll(acc, 1, 0)
        for s in range(nseq):
            r = s * t_seq
            before = st_ref[s][:, cols]
            if nseq == 1:
                before = jnp.where(first, before, carry_scr[SUBLANES - 1:SUBLANES, cols])
            fixed = jnp.where(row8 == 0, before, prev[r:r + SUBLANES, :])
            head = [prev[:r, :]] if r > 0 else []
            prev = jnp.concatenate(head + [fixed, prev[r + SUBLANES:, :]], axis=0)
            tail_ref[s, :, cols] = acc[r + t_seq - SUBLANES:r + t_seq, :]
        if nseq == 1:
            carry_scr[:, cols] = acc[tm - SUBLANES:tm, :]
        return acc + (prev - acc) * mu_ref[:, cols]

    for src, width, dst in ((0, ATT_WIDTH, P_Q), (ATT_WIDTH, 2 * KV_WIDTH, P_AK), (O_R, RW_WIDTH, P_R)):
        for c0 in range(0, width, A_TN):
            acc = product(src + c0, min(A_TN, width - c0))
            for l0 in range(0, acc.shape[1], LANES):
                col = dst + c0 + l0
                part = acc[:, l0:l0 + LANES]
                if _is_rope_col(col):
                    part = _rope128(part, cos_ref[...], sin_ref[...])
                p_ref[:, col:col + LANES] = part

    low = lax.broadcasted_iota(jnp.int32, (x_ref.shape[0], LANES), 1) < HEAD_DIM
    span = O_GD - O_WD
    rolled, first_piece = [], None
    for c0 in range(0, span, A_TN):
        acc = product(O_WD + c0, min(A_TN, span - c0))
        for l0 in range(0, acc.shape[1], LANES):
            piece = acc[:, l0:l0 + LANES]
            j = (c0 + l0) // LANES
            if j == 0:
                first_piece = piece
            rolled.append(pltpu.roll(piece, HEAD_DIM, 1))
            if j >= 1:
                m = j - 1
                dst = P_K + m * LANES if m < RW_WIDTH // LANES else P_V + (m - RW_WIDTH // LANES) * LANES
                p_ref[:, dst:dst + LANES] = jnp.where(low, rolled[m], rolled[m + 1])
            if j == span // LANES - 1:
                p_ref[:, P_LORA:P_LORA + LANES] = jnp.where(low, first_piece, piece)
    gd0 = P_LORA + DECAY_LORA + AAA_LORA
    p_ref[:, gd0:gd0 + GATE_LORA] = product(O_GD, GATE_LORA)
    p_ref[:, gd0 + GATE_LORA:P_LORA + LORA_W] = jnp.zeros((x_ref.shape[0], LORA_W - DECAY_LORA - AAA_LORA - GATE_LORA), F32)


def _in_proj(xp, xs, g, b, w_p, mu, shift_p, shift_s, tables_p, tables_s, tm, seq_p, seq_s):
    n_p, n_s = xp.shape[0], xs.shape[0]
    assert n_s == tm
    nrb = n_p // tm
    nseq_s = tm // seq_s
    tb = seq_p // tm
    last = lambda i: jnp.minimum(i, nrb - 1)

    def kern(xp_ref, cp_ref, sp_ref, stp_ref, xs_ref, cs_ref, ss_ref, sts_ref, g_ref, b_ref, w_ref, mu_ref,
             pp_ref, hp_ref, tp_ref, ps_ref, hs_ref, ts_ref, hb_ref, carry_scr):
        i = pl.program_id(0)
        pl.when(i < nrb)(lambda: _in_proj_kernel(i % tb == 0, xp_ref, g_ref, b_ref, w_ref, cp_ref, sp_ref, mu_ref, stp_ref,
                                                 pp_ref, hp_ref, tp_ref, hb_ref, carry_scr))
        pl.when(i == nrb)(lambda: _in_proj_kernel(None, xs_ref, g_ref, b_ref, w_ref, cs_ref, ss_ref, mu_ref, sts_ref,
                                                  ps_ref, hs_ref, ts_ref, hb_ref, carry_scr))

    const2 = lambda shape: pl.BlockSpec(shape, lambda i: (0, 0))
    return pl.pallas_call(
        kern,
        grid=(nrb + 1,),
        in_specs=[
            pl.BlockSpec((tm, D_MODEL), lambda i: (last(i), 0)),
            pl.BlockSpec((tm, LANES), lambda i: (last(i) % tb, 0)),
            pl.BlockSpec((tm, LANES), lambda i: (last(i) % tb, 0)),
            pl.BlockSpec((1, 1, RW_COLS), lambda i: (last(i) // tb, 0, 0)),
            const2((tm, D_MODEL)), const2((tm, LANES)), const2((tm, LANES)),
            pl.BlockSpec((nseq_s, 1, RW_COLS), lambda i: (0, 0, 0)),
            const2((1, D_MODEL)), const2((1, D_MODEL)),
            pl.BlockSpec((D_MODEL, IN_COLS), lambda i: (0, 0), pipeline_mode=pl.Buffered(1)),
            const2((1, RW_COLS)),
        ],
        out_specs=[
            pl.BlockSpec((tm, P_COLS), lambda i: (last(i), 0)),
            pl.BlockSpec((tm, D_MODEL), lambda i: (last(i), 0)),
            pl.BlockSpec((1, SUBLANES, RW_COLS), lambda i: (last(i), 0, 0)),
            const2((tm, P_COLS)), const2((tm, D_MODEL)),
            pl.BlockSpec((nseq_s, SUBLANES, RW_COLS), lambda i: (0, 0, 0)),
        ],
        out_shape=[
            jax.ShapeDtypeStruct((n_p, P_COLS), F32),
            jax.ShapeDtypeStruct((n_p, D_MODEL), F32),
            jax.ShapeDtypeStruct((nrb, SUBLANES, RW_COLS), F32),
            jax.ShapeDtypeStruct((n_s, P_COLS), F32),
            jax.ShapeDtypeStruct((n_s, D_MODEL), F32),
            jax.ShapeDtypeStruct((nseq_s, SUBLANES, RW_COLS), F32),
        ],
        scratch_shapes=[pltpu.VMEM((tm, D_MODEL), BF16), pltpu.VMEM((SUBLANES, RW_COLS), F32)],
        compiler_params=_cparams(("arbitrary",)),
        name="in_proj",
    )(xp, *tables_p, shift_p, xs, *tables_s, shift_s, g, b, w_p, mu)


def _attend(q, k, v, valid, sink_ref):
    r = q.shape[0]
    outs = []
    for kv in range(ATT_KV_HEADS):
        kh = k[:, kv * HEAD_DIM:(kv + 1) * HEAD_DIM].astype(BF16)
        vh = v[:, kv * HEAD_DIM:(kv + 1) * HEAD_DIM].astype(BF16)
        heads = [kv * ATT_GROUP + g for g in range(ATT_GROUP)]
        qs = jnp.concatenate([q[:, h * HEAD_DIM:(h + 1) * HEAD_DIM] for h in heads], axis=0)
        sink = jnp.concatenate([jnp.full((r, 1), sink_ref[h], F32) for h in heads], axis=0)
        s = lax.dot_general(qs.astype(BF16), kh, (((1,), (1,)), ((), ())), preferred_element_type=F32) * ATT_SCALE
        if valid is not None:
            s = jnp.where(valid, s, -jnp.inf)
        m = jnp.maximum(jnp.max(s, -1, keepdims=True), sink)
        p = jnp.exp(s - m)
        den = jnp.sum(p, -1, keepdims=True) + jnp.exp(sink - m)
        o = jnp.dot(p.astype(BF16), vh, preferred_element_type=F32) / den
        outs.extend(o[g * r:(g + 1) * r] for g in range(ATT_GROUP))
    return jnp.concatenate(outs, axis=1)


LOG2_E = math.log2(math.e)
QB = 2 * CHUNK


def _attn_prompt_kernel(sink_ref, q_ref, kp_ref, kc_ref, vp_ref, vc_ref, o_ref):
    m = pl.program_id(1)
    nseq = q_ref.shape[0]
    span = 2 * QB
    lane = lax.broadcasted_iota(jnp.int32, (span, KV_WIDTH), 1)
    row_c = 2 + lax.broadcasted_iota(jnp.int32, (QB, span), 0) // CHUNK
    col_c = lax.broadcasted_iota(jnp.int32, (QB, span), 1) // CHUNK
    back = row_c - col_c
    valid = jnp.logical_and(jnp.logical_and(back >= 0, back <= 2), 2 * m - 2 + col_c >= 0)
    out_lane = lax.broadcasted_iota(jnp.int32, (QB, PAIR), 1)

    def dup(x, kv):
        sw = pltpu.roll(x, HEAD_DIM, 1)
        return jnp.where(lane < HEAD_DIM, x, sw) if kv == 0 else jnp.where(lane < HEAD_DIM, sw, x)

    k_st, v_st = [], []
    for b in range(nseq):
        k_all = jnp.concatenate([kp_ref[b], kc_ref[b]], axis=0)
        v_all = jnp.concatenate([vp_ref[b], vc_ref[b]], axis=0)
        k_st.append([_stack_heads(dup(k_all, kv), HEAD_DIM).astype(BF16) for kv in range(ATT_KV_HEADS)])
        v_st.append([_stack_heads(dup(v_all, kv), HEAD_DIM).astype(BF16) for kv in range(ATT_KV_HEADS)])
    pairs = [(b, j) for b in range(nseq) for j in range(ATT_HEADS // 2)]
    kv_of = lambda j: (2 * j) // ATT_GROUP
    s = [lax.dot_general((q_ref[b, :, j * PAIR:(j + 1) * PAIR] * (ATT_SCALE * LOG2_E)).astype(BF16), k_st[b][kv_of(j)],
                         (((1,), (1,)), ((), ())), preferred_element_type=F32) for b, j in pairs]
    probs, dens = [], []
    for idx, (b, j) in enumerate(pairs):
        halves, den = [], []
        for e in range(2):
            sink = sink_ref[2 * j + e] * LOG2_E
            sh = jnp.where(valid, s[idx][:, e * span:(e + 1) * span], -jnp.inf)
            mx = jnp.maximum(jnp.max(sh, -1, keepdims=True), sink)
            ph = jnp.exp2(sh - mx)
            halves.append(ph)
            den.append(jnp.sum(ph, -1, keepdims=True) + jnp.exp2(sink - mx))
        probs.append(jnp.concatenate(halves, axis=1).astype(BF16))
        dens.append(jnp.where(out_lane < HEAD_DIM, den[0], den[1]))
    o = [jnp.dot(probs[idx], v_st[b][kv_of(j)], preferred_element_type=F32) / dens[idx]
         for idx, (b, j) in enumerate(pairs)]
    n_pairs = ATT_HEADS // 2
    for b in range(nseq):
        o_ref[b] = jnp.concatenate(o[b * n_pairs:(b + 1) * n_pairs], axis=1).astype(o_ref.dtype)


def _attn_prompt(p3, sinks, nseq):
    batch, seq = p3.shape[0], p3.shape[1]
    nb = seq // QB
    kcol, vcol = P_AK // KV_WIDTH, P_AV // KV_WIDTH
    prev = lambda col: pl.BlockSpec((nseq, QB, KV_WIDTH), lambda g, m: (g, jnp.maximum(m - 1, 0), col))
    cur = lambda col: pl.BlockSpec((nseq, QB, KV_WIDTH), lambda g, m: (g, m, col))
    return pl.pallas_call(
        _attn_prompt_kernel,
        grid=(batch // nseq, nb),
        in_specs=[
            pl.BlockSpec(memory_space=pltpu.SMEM),
            pl.BlockSpec((nseq, QB, ATT_WIDTH), lambda g, m: (g, m, P_Q // ATT_WIDTH)),
            prev(kcol), cur(kcol), prev(vcol), cur(vcol),
        ],
        out_specs=pl.BlockSpec((nseq, QB, ATT_WIDTH), lambda g, m: (g, m, 0)),
        out_shape=jax.ShapeDtypeStruct((batch, seq, ATT_WIDTH), BF16),
        compiler_params=_cparams(("parallel", "parallel")),
        name="attn_prompt",
    )(sinks, p3, p3, p3, p3, p3)


def _attn_sample_kernel(sink_ref, q_ref, kn_ref, vn_ref, kc_ref, vc_ref, o_ref, kw_ref, vw_ref):
    t = q_ref.shape[0]
    k = jnp.concatenate([kc_ref[...], kn_ref[...]], axis=0)
    v = jnp.concatenate([vc_ref[...], vn_ref[...]], axis=0)
    o_ref[...] = _attend(q_ref[...], k, v, None, sink_ref).astype(o_ref.dtype)
    kw_ref[...] = k[t:]
    vw_ref[...] = v[t:]


def _attn_sample(p, sinks, cache_k, cache_v, batch, seq):
    kcol, vcol = P_AK // KV_WIDTH, P_AV // KV_WIDTH
    return pl.pallas_call(
        _attn_sample_kernel,
        grid=(batch,),
        in_specs=[
            pl.BlockSpec(memory_space=pltpu.SMEM),
            pl.BlockSpec((seq, ATT_WIDTH), lambda b: (b, P_Q // ATT_WIDTH)),
            pl.BlockSpec((seq, KV_WIDTH), lambda b: (b, kcol)),
            pl.BlockSpec((seq, KV_WIDTH), lambda b: (b, vcol)),
            pl.BlockSpec((WINDOW, KV_WIDTH), lambda b: (b, 0)),
            pl.BlockSpec((WINDOW, KV_WIDTH), lambda b: (b, 0)),
        ],
        out_specs=[
            pl.BlockSpec((seq, ATT_WIDTH), lambda b: (b, 0)),
            pl.BlockSpec((WINDOW, KV_WIDTH), lambda b: (b, 0)),
            pl.BlockSpec((WINDOW, KV_WIDTH), lambda b: (b, 0)),
        ],
        out_shape=[
            jax.ShapeDtypeStruct((batch * seq, ATT_WIDTH), BF16),
            jax.ShapeDtypeStruct((batch * WINDOW, KV_WIDTH), F32),
            jax.ShapeDtypeStruct((batch * WINDOW, KV_WIDTH), F32),
        ],
        compiler_params=_cparams(("parallel",)),
        name="attn_sample",
    )(sinks, p, p, p, cache_k, cache_v)


SEG = 256
EXP_NEG_HALF = math.exp(-0.5)
PAIR = 2 * HEAD_DIM
N_PAIRS = RW_HEADS // 2


def _split2(x):
    hi = x.astype(BF16)
    lo = (x - hi.astype(F32)).astype(BF16)
    return hi, lo


def _seg_sum(x, bd):
    c = x.shape[0]
    xb = x.astype(BF16)
    stack = jnp.concatenate([xb[:, s:s + SEG] for s in range(0, RW_WIDTH, SEG)], axis=0)
    r = jnp.dot(stack, bd, preferred_element_type=F32)
    return jnp.concatenate([r[i * c:(i + 1) * c] for i in range(RW_WIDTH // SEG)], axis=1)


def _dotb(a, b):
    return jnp.dot(a.astype(BF16), b.astype(BF16), preferred_element_type=F32)


def _dotb_nt(a, b):
    return lax.dot_general(a.astype(BF16), b.astype(BF16), (((1,), (1,)), ((), ())), preferred_element_type=F32)


def _dotb_tn(a, b):
    return lax.dot_general(a.astype(BF16), b.astype(BF16), (((0,), (0,)), ((), ())), preferred_element_type=F32)


def _stack_heads(x, half):
    lane = lax.broadcasted_iota(jnp.int32, x.shape, 1)
    first = (lane % (2 * half)) < half
    zero = jnp.zeros_like(x)
    return jnp.concatenate([jnp.where(first, x, zero), jnp.where(first, zero, x)], axis=0)


def _stack_heads_bf16(x, half):
    return _stack_heads(x.astype(BF16), half)


def _rwkv_pairs(ab, rb, bb, kb, v, gam_c, s_bd, n_double):
    c = ab[0].shape[0]
    pairs = range(len(ab))
    t_idx = lax.broadcasted_iota(jnp.int32, (c, 2 * c), 0)
    s_idx = lax.broadcasted_iota(jnp.int32, (c, 2 * c), 1) % c
    strict, incl = s_idx < t_idx, s_idx <= t_idx
    g = [_dotb_nt(jnp.concatenate([ab[p], rb[p]], axis=0),
                  jnp.concatenate([_stack_heads_bf16(bb[p], HEAD_DIM), _stack_heads_bf16(kb[p], HEAD_DIM)], axis=0))
         for p in pairs]
    l_ak = [jnp.where(strict, g[p][:c, 2 * c:], 0.0) for p in pairs]
    m_rbk = [jnp.concatenate([jnp.where(incl, g[p][c:, :2 * c], 0.0), jnp.where(incl, g[p][c:, 2 * c:], 0.0)], axis=1)
             for p in pairs]
    lp = [jnp.where(strict, g[p][:c, :2 * c], 0.0) for p in pairs]
    v_st = [_stack_heads_bf16(v[p], HEAD_DIM) for p in pairs]
    y0 = [jnp.concatenate([ab[p], _dotb(l_ak[p], v_st[p])], axis=1) for p in pairs]
    n_acc = list(lp)
    lk_b = [lp[p].astype(BF16) for p in pairs]
    lk = [_dotb(lk_b[p], _stack_heads(lk_b[p], c)) for p in pairs]
    for k in range(1, n_double):
        n_st = [_stack_heads_bf16(n_acc[p], c) for p in pairs]
        lk_b = [lk[p].astype(BF16) for p in pairs]
        if k + 1 < n_double:
            z = [_dotb(lk_b[p], jnp.concatenate([n_st[p], _stack_heads(lk_b[p], c)], axis=1)) for p in pairs]
            n_acc = [n_acc[p] + lk[p] + z[p][:, :2 * c] for p in pairs]
            lk = [z[p][:, 2 * c:] for p in pairs]
        else:
            n_acc = [n_acc[p] + lk[p] + _dotb(lk_b[p], n_st[p]) for p in pairs]
    y = [y0[p] + _dotb(n_acc[p], _stack_heads_bf16(y0[p], HEAD_DIM)) for p in pairs]
    ya = [y[p][:, :PAIR] for p in pairs]
    yu = [y[p][:, PAIR:] for p in pairs]
    ou = []
    for p in pairs:
        lhs = jnp.concatenate([rb[p], ya[p]], axis=0).astype(BF16)
        ou.append(lax.dot_general(jnp.concatenate([lhs, lhs], axis=1), jnp.concatenate(_split2(s_bd[p]), axis=1),
                                  (((1,), (1,)), ((), ())), preferred_element_type=F32))
    u = [ou[p][c:] + yu[p] for p in pairs]
    o = [ou[p][:c] + _dotb(m_rbk[p], jnp.concatenate([_stack_heads_bf16(u[p], HEAD_DIM), v_st[p]], axis=0))
         for p in pairs]
    row_h = lax.broadcasted_iota(jnp.int32, (PAIR, PAIR), 0) // HEAD_DIM
    col_h = lax.broadcasted_iota(jnp.int32, (PAIR, PAIR), 1) // HEAD_DIM
    same = row_h == col_h
    d_t = [jnp.where(same, _dotb_tn(jnp.concatenate([u[p], v[p]], axis=0),
                                    jnp.concatenate([bb[p], kb[p]], axis=0)), 0.0) for p in pairs]
    s_new = [(s_bd[p] + d_t[p]) * gam_c[p] for p in pairs]
    return o, s_new


def _rwkv_kernel(xr_ref, xk_ref, xv_ref, xl_ref, w0_ref, a0_ref, kk_ref, ka_ref, rk_ref,
                 lg_ref, lb_ref, w2_ref, a2_ref, g2_ref, bd_ref, st0_ref,
                 out_ref, st_ref, s_scr):
    c = pl.program_id(1)
    nc = pl.num_programs(1)
    nb, C = xr_ref.shape[0], xr_ref.shape[1]
    n_double = int(math.log2(C))

    @pl.when(c == 0)
    def _():
        s_scr[...] = jnp.zeros_like(s_scr)
        for b in range(nb):
            for h in range(RW_HEADS):
                off = (h % 2) * HEAD_DIM
                s_scr[b * N_PAIRS + h // 2, off:off + HEAD_DIM, off:off + HEAD_DIM] = st0_ref[b, h]

    rows_of = lambda load: jnp.concatenate([load(b) for b in range(nb)], axis=0)
    wa_w = DECAY_LORA + AAA_LORA
    r = rows_of(lambda b: xr_ref[b])
    kr = rows_of(lambda b: xk_ref[b])
    v = rows_of(lambda b: xv_ref[b])
    xw = rows_of(lambda b: xl_ref[b, :, :wa_w])
    gd = rows_of(lambda b: xl_ref[b, :, wa_w:wa_w + GATE_LORA])
    wd = xw[:, 0:DECAY_LORA]
    ad = xw[:, DECAY_LORA:wa_w]

    lw = (-EXP_NEG_HALF * LOG2_E) * jax.nn.sigmoid(w0_ref[...] + _dotb(jnp.tanh(wd), w2_ref[...]))
    a = jax.nn.sigmoid(a0_ref[...] + _dotb(ad, a2_ref[...]))
    g = _dotb(jax.nn.sigmoid(gd), g2_ref[...])
    bd = bd_ref[...]
    kk = kr * kk_ref[...]
    kk = kk * lax.rsqrt(jnp.maximum(_seg_sum(kk * kk, bd), 1e-24))
    k2 = kr * (1.0 + (a - 1.0) * ka_ref[...])

    rows = nb * C
    ti = lax.broadcasted_iota(jnp.int32, (rows, rows), 0)
    si = lax.broadcasted_iota(jnp.int32, (rows, rows), 1)
    tri = jnp.logical_and(si <= ti, si // C == ti // C).astype(BF16)
    cs2 = jnp.dot(tri, jnp.concatenate(_split2(lw), axis=1), preferred_element_type=F32)
    cs = cs2[:, :RW_WIDTH] + cs2[:, RW_WIDTH:]
    gam = jnp.exp2(cs)
    ginv = jnp.exp2(-cs)
    ab = -kk * jnp.exp2(cs - lw)
    rb = r * gam
    bb = kk * a * ginv
    kb = k2 * ginv

    def per_pair(x, r0, r1):
        return [x[b * C + r0:b * C + r1, p * PAIR:(p + 1) * PAIR] for b in range(nb) for p in range(N_PAIRS)]

    chains = nb * N_PAIRS
    outs, s_new = _rwkv_pairs(per_pair(ab, 0, C), per_pair(rb, 0, C), per_pair(bb, 0, C), per_pair(kb, 0, C),
                              per_pair(v, 0, C), per_pair(gam, C - 1, C), [s_scr[q] for q in range(chains)], n_double)
    for q in range(chains):
        s_scr[q] = s_new[q]
    o = jnp.concatenate([jnp.concatenate(outs[b * N_PAIRS:(b + 1) * N_PAIRS], axis=1) for b in range(nb)], axis=0)

    mo = _seg_sum(o, bd) * (1.0 / HEAD_DIM)
    d = o - mo
    vo = _seg_sum(d * d, bd) * (1.0 / HEAD_DIM)
    on = d * lax.rsqrt(vo + LNX_EPS) * lg_ref[...] + lb_ref[...]
    bonus = _seg_sum(r * k2 * rk_ref[...], bd) * v
    res = ((on + bonus) * g).astype(out_ref.dtype)
    for b in range(nb):
        out_ref[b] = res[b * C:(b + 1) * C, :]

    @pl.when(c == nc - 1)
    def _():
        for b in range(nb):
            for h in range(RW_HEADS):
                off = (h % 2) * HEAD_DIM
                st_ref[b, h] = s_scr[b * N_PAIRS + h // 2, off:off + HEAD_DIM, off:off + HEAD_DIM]


def _rwkv(p3, state0, prm, chunk, nb):
    batch, seq = p3.shape[0], p3.shape[1]
    nc = seq // chunk
    row = lambda w: pl.BlockSpec((1, w), lambda g, c: (0, 0))
    xspec = lambda w, col: pl.BlockSpec((nb, chunk, w), lambda g, c: (g, c, col))
    full = lambda a: pl.BlockSpec(a.shape, lambda g, c: (0,) * a.ndim)
    stspec = pl.BlockSpec((nb, RW_HEADS, HEAD_DIM, HEAD_DIM), lambda g, c: (g, 0, 0, 0))
    return pl.pallas_call(
        _rwkv_kernel,
        grid=(batch // nb, nc),
        in_specs=[
            xspec(RW_WIDTH, P_R // RW_WIDTH), xspec(RW_WIDTH, P_K // RW_WIDTH), xspec(RW_WIDTH, P_V // RW_WIDTH),
            xspec(LORA_W, P_LORA // LORA_W),
            row(RW_WIDTH), row(RW_WIDTH), row(RW_WIDTH), row(RW_WIDTH), row(RW_WIDTH),
            row(RW_WIDTH), row(RW_WIDTH),
            full(prm["w2"]), full(prm["a2"]), full(prm["g2"]), full(prm["bd"]),
            stspec,
        ],
        out_specs=[
            pl.BlockSpec((nb, chunk, RW_WIDTH), lambda g, c: (g, c, 0)),
            stspec,
        ],
        out_shape=[
            jax.ShapeDtypeStruct((batch, seq, RW_WIDTH), BF16),
            jax.ShapeDtypeStruct((batch, RW_HEADS, HEAD_DIM, HEAD_DIM), F32),
        ],
        scratch_shapes=[pltpu.VMEM((nb * N_PAIRS, PAIR, PAIR), F32)],
        compiler_params=_cparams(("parallel", "arbitrary")),
        name="rwkv",
    )(p3, p3, p3, p3, prm["w0"], prm["a0"], prm["k_k"], prm["k_a"], prm["r_k"],
      prm["lnx_g"], prm["lnx_b"], prm["w2"], prm["a2"], prm["g2"], prm["bd"], state0)


D_TN = 512


def _out_proj_body(att_ref, rw_ref, h_ref, g_ref, b_ref, h1_ref, h1b_ref, w_scr):
    for c0 in range(0, D_MODEL, D_TN):
        m = jnp.dot(att_ref[...], w_scr[:ATT_WIDTH, c0:c0 + D_TN], preferred_element_type=F32)
        m = m + jnp.dot(rw_ref[...], w_scr[ATT_WIDTH:, c0:c0 + D_TN], preferred_element_type=F32)
        h1_ref[:, c0:c0 + D_TN] = ALPHA * h_ref[:, c0:c0 + D_TN] + m
    y = _layer_norm(h1_ref[...], g_ref[...], b_ref[...])
    h1_ref[...] = y
    h1b_ref[...] = y.astype(BF16)


def _out_proj(att_p, rw_p, h_p, att_s, rw_s, h_s, w_out, g, b, tm):
    n_p, n_s = att_p.shape[0], att_s.shape[0]
    nrb = n_p // tm
    last = lambda i: jnp.minimum(i, nrb - 1)

    def kern(ap_ref, rp_ref, hp_ref, as_ref, rs_ref, hs_ref, wa_ref, wb_ref, g_ref, b_ref,
             h1p_ref, h1bp_ref, h1s_ref, h1bs_ref, w_scr):
        i = pl.program_id(0)

        @pl.when(i == 0)
        def _():
            w_scr[:ATT_WIDTH, :] = wa_ref[...].astype(BF16)
            w_scr[ATT_WIDTH:, :] = wb_ref[...].astype(BF16)

        pl.when(i < nrb)(lambda: _out_proj_body(ap_ref, rp_ref, hp_ref, g_ref, b_ref, h1p_ref, h1bp_ref, w_scr))
        pl.when(i == nrb)(lambda: _out_proj_body(as_ref, rs_ref, hs_ref, g_ref, b_ref, h1s_ref, h1bs_ref, w_scr))

    rows_p = lambda w: pl.BlockSpec((tm, w), lambda i: (last(i), 0))
    rows_s = lambda w: pl.BlockSpec((n_s, w), lambda i: (0, 0))
    return pl.pallas_call(
        kern,
        grid=(nrb + 1,),
        in_specs=[
            rows_p(ATT_WIDTH), rows_p(RW_WIDTH), rows_p(D_MODEL),
            rows_s(ATT_WIDTH), rows_s(RW_WIDTH), rows_s(D_MODEL),
            pl.BlockSpec((ATT_WIDTH, D_MODEL), lambda i: (0, 0), pipeline_mode=pl.Buffered(1)),
            pl.BlockSpec((RW_WIDTH, D_MODEL), lambda i: (1, 0), pipeline_mode=pl.Buffered(1)),
            pl.BlockSpec((1, D_MODEL), lambda i: (0, 0)),
            pl.BlockSpec((1, D_MODEL), lambda i: (0, 0)),
        ],
        out_specs=[rows_p(D_MODEL), rows_p(D_MODEL), rows_s(D_MODEL), rows_s(D_MODEL)],
        out_shape=[
            jax.ShapeDtypeStruct((n_p, D_MODEL), F32), jax.ShapeDtypeStruct((n_p, D_MODEL), BF16),
            jax.ShapeDtypeStruct((n_s, D_MODEL), F32), jax.ShapeDtypeStruct((n_s, D_MODEL), BF16),
        ],
        scratch_shapes=[pltpu.VMEM((ATT_WIDTH + RW_WIDTH, D_MODEL), BF16)],
        compiler_params=_cparams(("arbitrary",)),
        name="out_proj",
    )(att_p, rw_p, h_p, att_s, rw_s, h_s, w_out, w_out, g, b)


E_TN = 512
E_NCB = D_FF // E_TN
E_SUB = 1024
SQRT_HALF = math.sqrt(0.5)


def _ffn_up_body(bps, nseq, i, h_ref, w_scr, cwg_ref, cwv_ref, cbg_ref, cbv_ref, sg_ref, sv_ref,
                 act_ref, og_ref, ov_ref, cg_scr, cv_scr):
    tm = h_ref.shape[0]
    sub = min(tm, E_SUB)
    n_sub = tm // sub
    t_seq = tm // nseq
    row8 = lax.broadcasted_iota(jnp.int32, (SUBLANES, E_TN), 0)

    def products(r0):
        u = jnp.dot(h_ref[r0:r0 + sub, :], w_scr[...], preferred_element_type=F32)
        return u[:, :E_TN], u[:, E_TN:]

    def conv(u, prevs, cw_ref, cb_ref):
        sh1, sh2 = pltpu.roll(u, 1, 0), pltpu.roll(u, 2, 0)
        for r, prev in prevs:
            p2, p1 = prev[0:1, :], prev[1:2, :]
            fix1 = jnp.where(row8 == 0, p1, sh1[r:r + SUBLANES, :])
            fix2 = jnp.where(row8 == 0, p2, jnp.where(row8 == 1, p1, sh2[r:r + SUBLANES, :]))
            head = lambda x: [x[:r, :]] if r > 0 else []
            tail = lambda x: [x[r + SUBLANES:, :]] if r + SUBLANES < x.shape[0] else []
            sh1 = jnp.concatenate(head(sh1) + [fix1] + tail(sh1), axis=0)
            sh2 = jnp.concatenate(head(sh2) + [fix2] + tail(sh2), axis=0)
        cw = cw_ref[...]
        c = cb_ref[...] + sh2 * cw[0:1, :]
        c = c + sh1 * cw[1:2, :]
        return c + u * cw[2:3, :]

    def epilogue(r0, ug, uv, prev_g, prev_v):
        gate = conv(ug, prev_g, cwg_ref, cbg_ref)
        val = conv(uv, prev_v, cwv_ref, cbv_ref)
        gelu = 0.5 * gate * (1.0 + lax.erf(gate * SQRT_HALF))
        act_ref[r0:r0 + sub, :] = (gelu * val).astype(act_ref.dtype)

    if nseq > 1:
        ug, uv = products(0)
        epilogue(0, ug, uv, [(s * t_seq, sg_ref[s]) for s in range(nseq)],
                 [(s * t_seq, sv_ref[s]) for s in range(nseq)])
        for s in range(nseq):
            og_ref[s] = ug[(s + 1) * t_seq - SUBLANES:(s + 1) * t_seq, :]
            ov_ref[s] = uv[(s + 1) * t_seq - SUBLANES:(s + 1) * t_seq, :]
    else:
        first = (i % bps) == 0
        prev_g = jnp.where(first, sg_ref[0], cg_scr[SUBLANES - 2:SUBLANES, :])
        prev_v = jnp.where(first, sv_ref[0], cv_scr[SUBLANES - 2:SUBLANES, :])
        nxt = products(0)
        for k in range(n_sub):
            ug, uv = nxt
            if k + 1 < n_sub:
                nxt = products((k + 1) * sub)
            epilogue(k * sub, ug, uv, [(0, prev_g)], [(0, prev_v)])
            prev_g, prev_v = ug[sub - 2:sub, :], uv[sub - 2:sub, :]
        cg_scr[...] = ug[sub - SUBLANES:sub, :]
        cv_scr[...] = uv[sub - SUBLANES:sub, :]
        og_ref[0] = ug[sub - SUBLANES:sub, :]
        ov_ref[0] = uv[sub - SUBLANES:sub, :]


def _ffn_up(h_p, conv_prev_p, seq_p, tm_p, h_s, conv_prev_s, seq_s, w_up, conv_w, conv_b, w_down):
    n_p, n_s = h_p.shape[0], h_s.shape[0]
    batch_p, batch_s = n_p // seq_p, n_s // seq_s
    bps = seq_p // tm_p
    nrb = n_p // tm_p
    slab = D_FF // (E_NCB * nrb)
    last = lambda i: jnp.maximum(i - 1, 0)

    def kern(hp_ref, sgp_ref, svp_ref, hs_ref, sgs_ref, svs_ref, wg_ref, wv_ref, cwg_ref, cwv_ref, cbg_ref, cbv_ref,
             wd_ref, actp_ref, ogp_ref, ovp_ref, acts_ref, ogs_ref, ovs_ref, wdb_ref, w_scr, cg_scr, cv_scr):
        i = pl.program_id(1)

        @pl.when(i == 0)
        def _():
            w_scr[:, :E_TN] = wg_ref[...].astype(BF16)
            w_scr[:, E_TN:] = wv_ref[...].astype(BF16)
            _ffn_up_body(1, batch_s, i, hs_ref, w_scr, cwg_ref, cwv_ref, cbg_ref, cbv_ref, sgs_ref, svs_ref,
                         acts_ref, ogs_ref, ovs_ref, cg_scr, cv_scr)

        @pl.when(i > 0)
        def _():
            wdb_ref[...] = wd_ref[...].astype(BF16)
            _ffn_up_body(bps, 1, i - 1, hp_ref, w_scr, cwg_ref, cwv_ref, cbg_ref, cbv_ref, sgp_ref, svp_ref,
                         actp_ref, ogp_ref, ovp_ref, cg_scr, cv_scr)

    col = lambda shape, off: pl.BlockSpec(shape, lambda j, i: (0,) * (len(shape) - 1) + (j + off,))
    st_p = lambda off: pl.BlockSpec((1, CONV_W - 1, E_TN), lambda j, i: (last(i) // bps, 0, j + off))
    st_s = lambda off: pl.BlockSpec((batch_s, CONV_W - 1, E_TN), lambda j, i: (0, 0, j + off))
    tail_p = pl.BlockSpec((1, SUBLANES, E_TN), lambda j, i: (last(i) // bps, 0, j))
    tail_s = pl.BlockSpec((batch_s, SUBLANES, E_TN), lambda j, i: (0, 0, j))
    wd_spec = pl.BlockSpec((slab, D_MODEL), lambda j, i: (j * nrb + last(i), 0))
    return pl.pallas_call(
        kern,
        grid=(E_NCB, nrb + 1),
        in_specs=[
            pl.BlockSpec((tm_p, D_MODEL), lambda j, i: (last(i), 0)), st_p(0), st_p(E_NCB),
            pl.BlockSpec((n_s, D_MODEL), lambda j, i: (0, 0)), st_s(0), st_s(E_NCB),
            col((D_MODEL, E_TN), 0), col((D_MODEL, E_TN), E_NCB),
            col((CONV_W, E_TN), 0), col((CONV_W, E_TN), E_NCB),
            col((1, E_TN), 0), col((1, E_TN), E_NCB),
            wd_spec,
        ],
        out_specs=[
            pl.BlockSpec((tm_p, E_TN), lambda j, i: (last(i), j)), tail_p, tail_p,
            pl.BlockSpec((n_s, E_TN), lambda j, i: (0, j)), tail_s, tail_s,
            wd_spec,
        ],
        out_shape=[
            jax.ShapeDtypeStruct((n_p, D_FF), BF16),
            jax.ShapeDtypeStruct((batch_p, SUBLANES, D_FF), F32), jax.ShapeDtypeStruct((batch_p, SUBLANES, D_FF), F32),
            jax.ShapeDtypeStruct((n_s, D_FF), BF16),
            jax.ShapeDtypeStruct((batch_s, SUBLANES, D_FF), F32), jax.ShapeDtypeStruct((batch_s, SUBLANES, D_FF), F32),
            jax.ShapeDtypeStruct((D_FF, D_MODEL), BF16),
        ],
        scratch_shapes=[pltpu.VMEM((D_MODEL, 2 * E_TN), BF16),
                        pltpu.VMEM((SUBLANES, E_TN), F32), pltpu.VMEM((SUBLANES, E_TN), F32)],
        compiler_params=_cparams(("parallel", "arbitrary")),
        name="ffn_up",
    )(h_p, conv_prev_p, conv_prev_p, h_s, conv_prev_s, conv_prev_s, w_up, w_up, conv_w, conv_w, conv_b, conv_b, w_down)


F_TN = 512


def _ffn_down_kernel(act_ref, w_ref, h1_ref, g_ref, b_ref, y_ref):
    for c0 in range(0, D_MODEL, F_TN):
        f = jnp.dot(act_ref[...], w_ref[:, c0:c0 + F_TN], preferred_element_type=F32)
        y_ref[:, c0:c0 + F_TN] = ALPHA * h1_ref[:, c0:c0 + F_TN] + f
    y_ref[...] = _layer_norm(y_ref[...], g_ref[...], b_ref[...])


def _ffn_down(act_p, h1_p, act_s, h1_s, w_down, g, b, tm):
    n_p, n_s = act_p.shape[0], act_s.shape[0]
    nrb = n_p // tm
    last = lambda i: jnp.minimum(i, nrb - 1)

    def kern(ap_ref, hp_ref, as_ref, hs_ref, w_ref, g_ref, b_ref, yp_ref, ys_ref):
        i = pl.program_id(0)
        pl.when(i < nrb)(lambda: _ffn_down_kernel(ap_ref, w_ref, hp_ref, g_ref, b_ref, yp_ref))
        pl.when(i == nrb)(lambda: _ffn_down_kernel(as_ref, w_ref, hs_ref, g_ref, b_ref, ys_ref))

    rows_p = lambda w: pl.BlockSpec((tm, w), lambda i: (last(i), 0))
    rows_s = lambda w: pl.BlockSpec((n_s, w), lambda i: (0, 0))
    return pl.pallas_call(
        kern,
        grid=(nrb + 1,),
        in_specs=[
            rows_p(D_FF), rows_p(D_MODEL), rows_s(D_FF), rows_s(D_MODEL),
            pl.BlockSpec((D_FF, D_MODEL), lambda i: (0, 0), pipeline_mode=pl.Buffered(1)),
            pl.BlockSpec((1, D_MODEL), lambda i: (0, 0)),
            pl.BlockSpec((1, D_MODEL), lambda i: (0, 0)),
        ],
        out_specs=[rows_p(D_MODEL), rows_s(D_MODEL)],
        out_shape=[jax.ShapeDtypeStruct((n_p, D_MODEL), F32), jax.ShapeDtypeStruct((n_s, D_MODEL), F32)],
        compiler_params=_cparams(("arbitrary",)),
        name="ffn_down",
    )(act_p, h1_p, act_s, h1_s, w_down, g, b)


def _rope_tables(pos, reps):
    half = HEAD_DIM // 2
    inv = ROPE_THETA ** (-jnp.arange(half, dtype=F32) / half)
    ang = pos.astype(F32)[:, None] * inv[None, :]
    cos, sin = jnp.cos(ang), jnp.sin(ang)
    cos128 = jnp.concatenate([cos, cos, cos, cos], axis=1)
    sin128 = jnp.concatenate([-sin, sin, -sin, sin], axis=1)
    return jnp.tile(cos128, (reps, 1)), jnp.tile(sin128, (reps, 1))


TM = 256
D_TM = 256
E_TM = 1024
F_TM = 256
RWKV_NB = 4
ATTN_NB = 2


def kernel(x_prompt, x_sample, cache_k, cache_v, state_wkv, state_shift, state_ffn_conv, ln_in_g, ln_in_b, w_in, attn_sinks, rw_mu, rw_w0, rw_w2, rw_a0, rw_a2, rw_g2, rw_k_k, rw_k_a, rw_r_k, rw_lnx_g, rw_lnx_b, w_out, ln1_g, ln1_b, ffn_w_up, ffn_conv_w, ffn_conv_b, ffn_w_down, ln2_g, ln2_b):
    l = 0
    seg = lax.broadcasted_iota(jnp.int32, (SEG, SEG), 0) // HEAD_DIM
    bd = (seg == seg.T).astype(BF16)
    row = lambda a: a.reshape(1, -1)
    rw_prm = {"w0": row(rw_w0[l]), "a0": row(rw_a0[l]), "k_k": row(rw_k_k[l]),
              "k_a": row(rw_k_a[l]), "r_k": row(rw_r_k[l]), "lnx_g": row(rw_lnx_g[l]), "lnx_b": row(rw_lnx_b[l]),
              "w2": rw_w2[l].astype(BF16), "a2": rw_a2[l].astype(BF16), "g2": rw_g2[l].astype(BF16), "bd": bd}
    sinks = attn_sinks[l]
    bp, tp = x_prompt.shape[0], x_prompt.shape[1]
    bs, ts = x_sample.shape[0], x_sample.shape[1]
    n_p, n_s = bp * tp, bs * ts
    zeros = lambda *s: jnp.zeros(s, F32)

    p_p, h_p, tails_p, p_s, h_s, tails_s = _in_proj(
        x_prompt.reshape(n_p, D_MODEL), x_sample.reshape(n_s, D_MODEL), row(ln_in_g), row(ln_in_b),
        w_in[l].astype(BF16), row(rw_mu[l]), zeros(bp, 1, RW_COLS), state_shift[l],
        _rope_tables(jnp.arange(tp, dtype=jnp.int32), 1),
        _rope_tables(PAST_LEN + jnp.arange(ts, dtype=jnp.int32), bs), TM, tp, ts)
    p3_p, p3_s = p_p.reshape(bp, tp, P_COLS), p_s.reshape(bs, ts, P_COLS)

    att_p = _attn_prompt(p3_p, sinks, ATTN_NB).reshape(n_p, ATT_WIDTH)
    att_s, s_k, s_v = _attn_sample(p_s, sinks, cache_k[l].reshape(bs * WINDOW, KV_WIDTH),
                                   cache_v[l].reshape(bs * WINDOW, KV_WIDTH), bs, ts)
    rw_p, p_wkv = _rwkv(p3_p, zeros(bp, RW_HEADS, HEAD_DIM, HEAD_DIM), rw_prm, CHUNK, RWKV_NB)
    rw_s, s_wkv = _rwkv(p3_s, state_wkv[l], rw_prm, ts, RWKV_NB)

    h1_p, h1b_p, h1_s, h1b_s = _out_proj(att_p, rw_p.reshape(n_p, RW_WIDTH), h_p, att_s, rw_s.reshape(n_s, RW_WIDTH), h_s,
                                         w_out[l], row(ln1_g[l]), row(ln1_b[l]), D_TM)
    act_p, cg_p, cv_p, act_s, cg_s, cv_s, w_down_b = _ffn_up(
        h1b_p, zeros(bp, CONV_W - 1, 2 * D_FF), tp, E_TM, h1b_s, state_ffn_conv[l], ts,
        ffn_w_up[l], ffn_conv_w[l], row(ffn_conv_b[l]), ffn_w_down[l])
    y_p, y_s = _ffn_down(act_p, h1_p, act_s, h1_s, w_down_b, row(ln2_g[l]), row(ln2_b[l]), F_TM)

    kv_shape = lambda b: (1, b, WINDOW, ATT_KV_HEADS, HEAD_DIM)
    p_k = p3_p[:, tp - WINDOW:, P_AK:P_AK + KV_WIDTH].reshape(kv_shape(bp))
    p_v = p3_p[:, tp - WINDOW:, P_AV:P_AV + KV_WIDTH].reshape(kv_shape(bp))
    shift_of = lambda tails, b: tails.reshape(b, -1, RW_COLS)[:, -1:, :][None]
    conv_of = lambda cg, cv: jnp.concatenate([cg[:, SUBLANES - 2:, :], cv[:, SUBLANES - 2:, :]], axis=-1)[None]
    return (y_p.reshape(bp, tp, D_MODEL), y_s.reshape(bs, ts, D_MODEL), p_k, p_v, p_wkv[None],
            shift_of(tails_p, bp), conv_of(cg_p, cv_p),
            s_k.reshape(kv_shape(bs)), s_v.reshape(kv_shape(bs)), s_wkv[None],
            shift_of(tails_s, bs), conv_of(cg_s, cv_s))
```
